```python
import math, functools
import jax, jax.numpy as jnp
from jax import lax
import numpy as np

D_MODEL = 2048
BATCH = 8
SEQ = 4096
DEPTH = 2

DN_HEADS = 8
DN_HEAD_DIM = 128
DN_WIDTH = DN_HEADS * DN_HEAD_DIM
DN_CONV = 4
DN_CHUNK = 64
SG_GROUPS = 8
SG_GROUP_DIM = 128
SG_WIDTH = SG_GROUPS * SG_GROUP_DIM
SG_CHUNK = 128
D_FF = 5632
FFN_CONV = 3
EPS = 1e-6

SPLIT_SIZES = (3 * DN_WIDTH, DN_WIDTH, DN_HEADS, DN_HEADS, SG_WIDTH, SG_WIDTH, D_MODEL, D_MODEL)
SPLIT_POINTS = tuple(int(s) for s in np.cumsum(SPLIT_SIZES)[:-1])
IN_COLS = int(sum(SPLIT_SIZES))

kernel_name = "hybrid_deltanet_gmlp_convffn_gated_merge"


def rmsnorm(x, g):
    xf = x.astype(jnp.float32)
    y = xf * lax.rsqrt(jnp.mean(xf * xf, axis=-1, keepdims=True) + EPS)
    return (y * g.astype(jnp.float32)).astype(x.dtype)


def layernorm(x, g, b):
    xf = x.astype(jnp.float32)
    mu = jnp.mean(xf, axis=-1, keepdims=True)
    xc = xf - mu
    y = xc * lax.rsqrt(jnp.mean(xc * xc, axis=-1, keepdims=True) + EPS)
    return (y * g.astype(jnp.float32) + b.astype(jnp.float32)).astype(x.dtype)


def l2norm(x):
    xf = x.astype(jnp.float32)
    return xf * lax.rsqrt(jnp.sum(xf * xf, axis=-1, keepdims=True) + EPS)


def causal_dwconv(x, w):
    K = w.shape[0]
    T = x.shape[1]
    xp = jnp.pad(x, ((0, 0), (K - 1, 0), (0, 0)))
    out = xp[:, 0:T] * w[0]
    for j in range(1, K):
        out = out + xp[:, j:j + T] * w[j]
    return out


def gated_delta_rule_chunked(q, k, v, g, beta):
    B, T, H, Dk = q.shape
    Dv = v.shape[-1]
    C = DN_CHUNK
    N = T // C

    def to_chunks(t):
        t = t.astype(jnp.float32).reshape((B, N, C, H) + t.shape[3:])
        return jnp.moveaxis(t, 3, 1)

    q = to_chunks(q) * (Dk ** -0.5)
    k = to_chunks(k)
    v = to_chunks(v)
    g = jnp.cumsum(to_chunks(g), axis=-1)
    beta = to_chunks(beta)
    k_beta = k * beta[..., None]
    v_beta = v * beta[..., None]

    causal = jnp.tril(jnp.ones((C, C), dtype=bool))
    strict = jnp.tril(jnp.ones((C, C), dtype=bool), -1)
    decay = jnp.exp(jnp.where(causal, g[..., :, None] - g[..., None, :], -jnp.inf))

    L = jnp.where(strict, jnp.einsum('bhnid,bhnjd->bhnij', k_beta, k) * decay, 0.0)
    eye = jnp.eye(C, dtype=jnp.float32)
    Tinv = lax.linalg.triangular_solve(L + eye, jnp.broadcast_to(eye, L.shape),
                                       left_side=True, lower=True, unit_diagonal=True)
    u = jnp.einsum('bhnij,bhnjv->bhniv', Tinv, v_beta)
    w = jnp.einsum('bhnij,bhnjk->bhnik', Tinv, k_beta * jnp.exp(g)[..., None])

    attn = jnp.where(causal, jnp.einsum('bhnid,bhnjd->bhnij', q, k) * decay, 0.0)
    q_dec = q * jnp.exp(g)[..., None]
    g_last = g[..., -1]
    k_dec = k * jnp.exp(g_last[..., None] - g)[..., None]

    xs = tuple(jnp.moveaxis(t, 2, 0) for t in (u, w, attn, q_dec, k_dec, g_last))

    def step(S, inp):
        u_n, w_n, a_n, qd_n, kd_n, gl_n = inp
        v_new = u_n - jnp.einsum('bhck,bhkv->bhcv', w_n, S)
        o_n = (jnp.einsum('bhck,bhkv->bhcv', qd_n, S)
               + jnp.einsum('bhij,bhjv->bhiv', a_n, v_new))
        S = S * jnp.exp(gl_n)[..., None, None] + jnp.einsum('bhck,bhcv->bhkv', kd_n, v_new)
        return S, o_n

    S0 = jnp.zeros((B, H, Dk, Dv), jnp.float32)
    _, o = lax.scan(step, S0, xs)
    return jnp.transpose(o, (1, 0, 3, 2, 4)).reshape(B, T, H, Dv)


def _fwd_setup_inputs(seed: int = 0) -> dict:
    key = jax.random.key(seed)
    ks = jax.random.split(key, 24)
    f32 = jnp.float32

    def nrm(k, shape, scale):
        return jax.random.normal(k, shape, f32) * scale

    x = jax.random.normal(ks[0], (BATCH, SEQ, D_MODEL), f32)
    norm1_g = 1.0 + nrm(ks[1], (DEPTH, D_MODEL), 0.02)
    w_in = nrm(ks[2], (DEPTH, D_MODEL, IN_COLS), D_MODEL ** -0.5)
    dn_conv_w = nrm(ks[3], (DEPTH, DN_CONV, 3 * DN_WIDTH), DN_CONV ** -0.5)
    dn_a_log = jnp.log(jax.random.uniform(ks[4], (DEPTH, DN_HEADS), f32, 1.0, 16.0))
    dt = jnp.exp(jax.random.uniform(ks[5], (DEPTH, DN_HEADS), f32,
                                    math.log(0.001), math.log(0.1)))
    dn_dt_bias = dt + jnp.log(-jnp.expm1(-dt))
    dn_onorm_g = 1.0 + nrm(ks[6], (DEPTH, DN_HEAD_DIM), 0.02)
    sg_ln_g = 1.0 + nrm(ks[7], (DEPTH, SG_WIDTH), 0.02)
    sg_ln_b = nrm(ks[8], (DEPTH, SG_WIDTH), 0.02)
    sg_w = nrm(ks[9], (DEPTH, SG_GROUPS, SG_CHUNK, SG_CHUNK), 0.5 * SG_CHUNK ** -0.5)
    sg_b = 1.0 + nrm(ks[10], (DEPTH, SG_GROUPS, SG_CHUNK), 0.02)
    w_branch_a = nrm(ks[11], (DEPTH, DN_WIDTH, D_MODEL), DN_WIDTH ** -0.5)
    w_branch_b = nrm(ks[12], (DEPTH, SG_WIDTH, D_MODEL), SG_WIDTH ** -0.5)
    w_out = nrm(ks[13], (DEPTH, D_MODEL, D_MODEL), D_MODEL ** -0.5)
    norm2_g = 1.0 + nrm(ks[14], (DEPTH, D_MODEL), 0.02)
    ffn_w_gate = nrm(ks[15], (DEPTH, D_MODEL, D_FF), D_MODEL ** -0.5)
    ffn_w_up = nrm(ks[16], (DEPTH, D_MODEL, D_FF), D_MODEL ** -0.5)
    ffn_conv_w = nrm(ks[17], (DEPTH, FFN_CONV, D_FF), FFN_CONV ** -0.5)
    ffn_conv_b = nrm(ks[18], (DEPTH, D_FF), 0.02)
    ffn_w_down = nrm(ks[19], (DEPTH, D_FF, D_MODEL), D_FF ** -0.5)
    final_norm_g = 1.0 + nrm(ks[20], (D_MODEL,), 0.02)
    return {"x": x, "norm1_g": norm1_g, "w_in": w_in, "dn_conv_w": dn_conv_w,
            "dn_a_log": dn_a_log, "dn_dt_bias": dn_dt_bias, "dn_onorm_g": dn_onorm_g,
            "sg_ln_g": sg_ln_g, "sg_ln_b": sg_ln_b, "sg_w": sg_w, "sg_b": sg_b,
            "w_branch_a": w_branch_a, "w_branch_b": w_branch_b, "w_out": w_out,
            "norm2_g": norm2_g, "ffn_w_gate": ffn_w_gate, "ffn_w_up": ffn_w_up,
            "ffn_conv_w": ffn_conv_w, "ffn_conv_b": ffn_conv_b, "ffn_w_down": ffn_w_down,
            "final_norm_g": final_norm_g}


def _fwd_reference(x, norm1_g, w_in, dn_conv_w, dn_a_log, dn_dt_bias, dn_onorm_g,
              sg_ln_g, sg_ln_b, sg_w, sg_b, w_branch_a, w_branch_b, w_out,
              norm2_g, ffn_w_gate, ffn_w_up, ffn_conv_w, ffn_conv_b, ffn_w_down,
              final_norm_g):
    B, T, _ = x.shape
    sg_mask = jnp.tril(jnp.ones((SG_CHUNK, SG_CHUNK), dtype=bool))
    for l in range(DEPTH):
        h = rmsnorm(x, norm1_g[l])
        proj = h @ w_in[l]
        qkv, z, b_raw, a_raw, u_raw, v_raw, ga_raw, gb_raw = jnp.split(proj, SPLIT_POINTS, axis=-1)

        qkv = jax.nn.silu(causal_dwconv(qkv, dn_conv_w[l]))
        q, k, v = jnp.split(qkv, 3, axis=-1)
        q = l2norm(q.reshape(B, T, DN_HEADS, DN_HEAD_DIM))
        k = l2norm(k.reshape(B, T, DN_HEADS, DN_HEAD_DIM))
        v = v.reshape(B, T, DN_HEADS, DN_HEAD_DIM)
        beta = jax.nn.sigmoid(b_raw.astype(jnp.float32))
        g = -jnp.exp(dn_a_log[l].astype(jnp.float32)) * jax.nn.softplus(
            a_raw.astype(jnp.float32) + dn_dt_bias[l].astype(jnp.float32))
        o = gated_delta_rule_chunked(q, k, v, g, beta)
        o = rmsnorm(o, dn_onorm_g[l]) * jax.nn.silu(
            z.reshape(B, T, DN_HEADS, DN_HEAD_DIM).astype(jnp.float32))
        y_a = o.reshape(B, T, DN_WIDTH).astype(x.dtype)

        u = jax.nn.gelu(u_raw, approximate=False)
        vg = layernorm(jax.nn.gelu(v_raw, approximate=False), sg_ln_g[l], sg_ln_b[l])
        vg = vg.reshape(B, T // SG_CHUNK, SG_CHUNK, SG_GROUPS, SG_GROUP_DIM)
        ws = jnp.where(sg_mask, sg_w[l], 0.0)
        mixed = (jnp.einsum('gij,bnjgc->bnigc', ws, vg)
                 + jnp.transpose(sg_b[l])[None, None, :, :, None])
        y_b = u * mixed.reshape(B, T, SG_WIDTH)

        merged = (jax.nn.sigmoid(ga_raw) * (y_a @ w_branch_a[l])
                  + jax.nn.sigmoid(gb_raw) * (y_b @ w_branch_b[l]))
        x = x + merged @ w_out[l]

        h2 = rmsnorm(x, norm2_g[l])
        gate = causal_dwconv(h2 @ ffn_w_gate[l], ffn_conv_w[l]) + ffn_conv_b[l]
        x = x + (jax.nn.silu(gate) * (h2 @ ffn_w_up[l])) @ ffn_w_down[l]
    return rmsnorm(x, final_norm_g)


import jax as _jax
import jax.numpy as _jnp

TWIN_FORMAT = 'train_step'
FWD_PARAMS = ['x', 'norm1_g', 'w_in', 'dn_conv_w', 'dn_a_log', 'dn_dt_bias', 'dn_onorm_g', 'sg_ln_g', 'sg_ln_b', 'sg_w', 'sg_b', 'w_branch_a', 'w_branch_b', 'w_out', 'norm2_g', 'ffn_w_gate', 'ffn_w_up', 'ffn_conv_w', 'ffn_conv_b', 'ffn_w_down', 'final_norm_g']
TWIN_WEIGHTS = ['norm1_g', 'w_in', 'dn_conv_w', 'dn_a_log', 'dn_dt_bias', 'dn_onorm_g', 'sg_ln_g', 'sg_ln_b', 'sg_w', 'sg_b', 'w_branch_a', 'w_branch_b', 'w_out', 'norm2_g', 'ffn_w_gate', 'ffn_w_up', 'ffn_conv_w', 'ffn_conv_b', 'ffn_w_down', 'final_norm_g']
TWIN_DIFF_INPUT = 'x'
TWIN_INPUTS = ['x', 'norm1_g', 'w_in', 'dn_conv_w', 'dn_a_log', 'dn_dt_bias', 'dn_onorm_g', 'sg_ln_g', 'sg_ln_b', 'sg_w', 'sg_b', 'w_branch_a', 'w_branch_b', 'w_out', 'norm2_g', 'ffn_w_gate', 'ffn_w_up', 'ffn_conv_w', 'ffn_conv_b', 'ffn_w_down', 'final_norm_g', 'loss_target', 'm_norm1_g', 'm_w_in', 'm_dn_conv_w', 'm_dn_a_log', 'm_dn_dt_bias', 'm_dn_onorm_g', 'm_sg_ln_g', 'm_sg_ln_b', 'm_sg_w', 'm_sg_b', 'm_w_branch_a', 'm_w_branch_b', 'm_w_out', 'm_norm2_g', 'm_ffn_w_gate', 'm_ffn_w_up', 'm_ffn_conv_w', 'm_ffn_conv_b', 'm_ffn_w_down', 'm_final_norm_g', 'v_norm1_g', 'v_w_in', 'v_dn_conv_w', 'v_dn_a_log', 'v_dn_dt_bias', 'v_dn_onorm_g', 'v_sg_ln_g', 'v_sg_ln_b', 'v_sg_w', 'v_sg_b', 'v_w_branch_a', 'v_w_branch_b', 'v_w_out', 'v_norm2_g', 'v_ffn_w_gate', 'v_ffn_w_up', 'v_ffn_conv_w', 'v_ffn_conv_b', 'v_ffn_w_down', 'v_final_norm_g']
TWIN_OUTPUTS = ['loss', 'grad_x', 'grad_norm1_g', 'grad_w_in', 'grad_dn_conv_w', 'grad_dn_a_log', 'grad_dn_dt_bias', 'grad_dn_onorm_g', 'grad_sg_ln_g', 'grad_sg_ln_b', 'grad_sg_w', 'grad_sg_b', 'grad_w_branch_a', 'grad_w_branch_b', 'grad_w_out', 'grad_norm2_g', 'grad_ffn_w_gate', 'grad_ffn_w_up', 'grad_ffn_conv_w', 'grad_ffn_conv_b', 'grad_ffn_w_down', 'grad_final_norm_g', 'delta_norm1_g', 'delta_w_in', 'delta_dn_conv_w', 'delta_dn_a_log', 'delta_dn_dt_bias', 'delta_dn_onorm_g', 'delta_sg_ln_g', 'delta_sg_ln_b', 'delta_sg_w', 'delta_sg_b', 'delta_w_branch_a', 'delta_w_branch_b', 'delta_w_out', 'delta_norm2_g', 'delta_ffn_w_gate', 'delta_ffn_w_up', 'delta_ffn_conv_w', 'delta_ffn_conv_b', 'delta_ffn_w_down', 'delta_final_norm_g', 'new_m_norm1_g', 'new_m_w_in', 'new_m_dn_conv_w', 'new_m_dn_a_log', 'new_m_dn_dt_bias', 'new_m_dn_onorm_g', 'new_m_sg_ln_g', 'new_m_sg_ln_b', 'new_m_sg_w', 'new_m_sg_b', 'new_m_w_branch_a', 'new_m_w_branch_b', 'new_m_w_out', 'new_m_norm2_g', 'new_m_ffn_w_gate', 'new_m_ffn_w_up', 'new_m_ffn_conv_w', 'new_m_ffn_conv_b', 'new_m_ffn_w_down', 'new_m_final_norm_g', 'new_v_norm1_g', 'new_v_w_in', 'new_v_dn_conv_w', 'new_v_dn_a_log', 'new_v_dn_dt_bias', 'new_v_dn_onorm_g', 'new_v_sg_ln_g', 'new_v_sg_ln_b', 'new_v_sg_w', 'new_v_sg_b', 'new_v_w_branch_a', 'new_v_w_branch_b', 'new_v_w_out', 'new_v_norm2_g', 'new_v_ffn_w_gate', 'new_v_ffn_w_up', 'new_v_ffn_conv_w', 'new_v_ffn_conv_b', 'new_v_ffn_w_down', 'new_v_final_norm_g']
TWIN_LEAF_KINDS = {'loss': 'loss', 'grad_x': 'grad_x', 'grad_norm1_g': 'grad_w', 'grad_w_in': 'grad_w', 'grad_dn_conv_w': 'grad_w', 'grad_dn_a_log': 'grad_w', 'grad_dn_dt_bias': 'grad_w', 'grad_dn_onorm_g': 'grad_w', 'grad_sg_ln_g': 'grad_w', 'grad_sg_ln_b': 'grad_w', 'grad_sg_w': 'grad_w', 'grad_sg_b': 'grad_w', 'grad_w_branch_a': 'grad_w', 'grad_w_branch_b': 'grad_w', 'grad_w_out': 'grad_w', 'grad_norm2_g': 'grad_w', 'grad_ffn_w_gate': 'grad_w', 'grad_ffn_w_up': 'grad_w', 'grad_ffn_conv_w': 'grad_w', 'grad_ffn_conv_b': 'grad_w', 'grad_ffn_w_down': 'grad_w', 'grad_final_norm_g': 'grad_w', 'delta_norm1_g': 'delta_w', 'delta_w_in': 'delta_w', 'delta_dn_conv_w': 'delta_w', 'delta_dn_a_log': 'delta_w', 'delta_dn_dt_bias': 'delta_w', 'delta_dn_onorm_g': 'delta_w', 'delta_sg_ln_g': 'delta_w', 'delta_sg_ln_b': 'delta_w', 'delta_sg_w': 'delta_w', 'delta_sg_b': 'delta_w', 'delta_w_branch_a': 'delta_w', 'delta_w_branch_b': 'delta_w', 'delta_w_out': 'delta_w', 'delta_norm2_g': 'delta_w', 'delta_ffn_w_gate': 'delta_w', 'delta_ffn_w_up': 'delta_w', 'delta_ffn_conv_w': 'delta_w', 'delta_ffn_conv_b': 'delta_w', 'delta_ffn_w_down': 'delta_w', 'delta_final_norm_g': 'delta_w', 'new_m_norm1_g': 'new_m', 'new_m_w_in': 'new_m', 'new_m_dn_conv_w': 'new_m', 'new_m_dn_a_log': 'new_m', 'new_m_dn_dt_bias': 'new_m', 'new_m_dn_onorm_g': 'new_m', 'new_m_sg_ln_g': 'new_m', 'new_m_sg_ln_b': 'new_m', 'new_m_sg_w': 'new_m', 'new_m_sg_b': 'new_m', 'new_m_w_branch_a': 'new_m', 'new_m_w_branch_b': 'new_m', 'new_m_w_out': 'new_m', 'new_m_norm2_g': 'new_m', 'new_m_ffn_w_gate': 'new_m', 'new_m_ffn_w_up': 'new_m', 'new_m_ffn_conv_w': 'new_m', 'new_m_ffn_conv_b': 'new_m', 'new_m_ffn_w_down': 'new_m', 'new_m_final_norm_g': 'new_m', 'new_v_norm1_g': 'new_v', 'new_v_w_in': 'new_v', 'new_v_dn_conv_w': 'new_v', 'new_v_dn_a_log': 'new_v', 'new_v_dn_dt_bias': 'new_v', 'new_v_dn_onorm_g': 'new_v', 'new_v_sg_ln_g': 'new_v', 'new_v_sg_ln_b': 'new_v', 'new_v_sg_w': 'new_v', 'new_v_sg_b': 'new_v', 'new_v_w_branch_a': 'new_v', 'new_v_w_branch_b': 'new_v', 'new_v_w_out': 'new_v', 'new_v_norm2_g': 'new_v', 'new_v_ffn_w_gate': 'new_v', 'new_v_ffn_w_up': 'new_v', 'new_v_ffn_conv_w': 'new_v', 'new_v_ffn_conv_b': 'new_v', 'new_v_ffn_w_down': 'new_v', 'new_v_final_norm_g': 'new_v'}


def _forward(args):
    return _fwd_reference(*[args[k] for k in FWD_PARAMS])


def _output_shape():
    def fwd():
        inp = _fwd_setup_inputs(0)
        return _fwd_reference(*[inp[k] for k in FWD_PARAMS])
    out = _jax.eval_shape(fwd)
    return out.shape, out.dtype

N_MICROBATCH = 1
ADAM_LR = 0.001
ADAM_B1 = 0.9
ADAM_B2 = 0.999
ADAM_EPS = 1e-08
ADAM_WD = 0.01
ADAM_STEP = 10
PER_EXAMPLE_BATCH_AXIS = {'x': 0, 'loss_target': 0}
SHARED_INPUTS = []
_WEIGHT_DTYPES = {'norm1_g': _jnp.float32, 'w_in': _jnp.float32, 'dn_conv_w': _jnp.float32, 'dn_a_log': _jnp.float32, 'dn_dt_bias': _jnp.float32, 'dn_onorm_g': _jnp.float32, 'sg_ln_g': _jnp.float32, 'sg_ln_b': _jnp.float32, 'sg_w': _jnp.float32, 'sg_b': _jnp.float32, 'w_branch_a': _jnp.float32, 'w_branch_b': _jnp.float32, 'w_out': _jnp.float32, 'norm2_g': _jnp.float32, 'ffn_w_gate': _jnp.float32, 'ffn_w_up': _jnp.float32, 'ffn_conv_w': _jnp.float32, 'ffn_conv_b': _jnp.float32, 'ffn_w_down': _jnp.float32, 'final_norm_g': _jnp.float32}
MOMENT_SCALE = {'norm1_g': 6.266646e-02, 'w_in': 2.791791e-02, 'dn_conv_w': 3.131049e-02, 'dn_a_log': 1.478695e-01, 'dn_dt_bias': 1.393570e-01, 'dn_onorm_g': 1.442272e-01, 'sg_ln_g': 1.563002e-02, 'sg_ln_b': 1.573899e-02, 'sg_w': 3.131238e-02, 'sg_b': 4.230681e-02, 'w_branch_a': 2.813854e-02, 'w_branch_b': 3.403150e-02, 'w_out': 4.411956e-02, 'norm2_g': 6.291538e-02, 'ffn_w_gate': 2.694530e-02, 'ffn_w_up': 2.620282e-02, 'ffn_conv_w': 2.714569e-02, 'ffn_conv_b': 2.583307e-02, 'ffn_w_down': 4.345900e-02, 'final_norm_g': 1.602529e+01}


def _to_microbatches(a, axis):
    t = _jnp.moveaxis(a, axis, 0)
    t = t.reshape((N_MICROBATCH, t.shape[0] // N_MICROBATCH) + t.shape[1:])
    return _jnp.moveaxis(t, 1, axis + 1)


def setup_inputs(seed: int = 0) -> dict:
    inp = _fwd_setup_inputs(seed)
    key = _jax.random.fold_in(_jax.random.key(seed), 7919)
    shape, _ = _output_shape()
    out = dict(inp)
    out["loss_target"] = _jax.random.normal(_jax.random.fold_in(key, 0), shape, _jnp.float32)
    for i, name in enumerate(TWIN_WEIGHTS):
        w = inp[name].astype(_jnp.float32)
        if MOMENT_SCALE is None:
            s = _jnp.sqrt(_jnp.mean(_jnp.square(w)) + 1e-30)
        else:
            s = MOMENT_SCALE[name]
        km, kv = _jax.random.split(_jax.random.fold_in(key, i + 1))
        out[name] = w
        out["m_" + name] = s * _jax.random.normal(km, w.shape, _jnp.float32)
        out["v_" + name] = (s * s) * _jax.random.uniform(kv, w.shape, _jnp.float32, 0.5, 1.5)
    if N_MICROBATCH > 1:
        for name, axis in PER_EXAMPLE_BATCH_AXIS.items():
            out[name] = _to_microbatches(out[name], axis)
    return {'x': out['x'], 'norm1_g': out['norm1_g'], 'w_in': out['w_in'], 'dn_conv_w': out['dn_conv_w'], 'dn_a_log': out['dn_a_log'], 'dn_dt_bias': out['dn_dt_bias'], 'dn_onorm_g': out['dn_onorm_g'], 'sg_ln_g': out['sg_ln_g'], 'sg_ln_b': out['sg_ln_b'], 'sg_w': out['sg_w'], 'sg_b': out['sg_b'], 'w_branch_a': out['w_branch_a'], 'w_branch_b': out['w_branch_b'], 'w_out': out['w_out'], 'norm2_g': out['norm2_g'], 'ffn_w_gate': out['ffn_w_gate'], 'ffn_w_up': out['ffn_w_up'], 'ffn_conv_w': out['ffn_conv_w'], 'ffn_conv_b': out['ffn_conv_b'], 'ffn_w_down': out['ffn_w_down'], 'final_norm_g': out['final_norm_g'], 'loss_target': out['loss_target'], 'm_norm1_g': out['m_norm1_g'], 'm_w_in': out['m_w_in'], 'm_dn_conv_w': out['m_dn_conv_w'], 'm_dn_a_log': out['m_dn_a_log'], 'm_dn_dt_bias': out['m_dn_dt_bias'], 'm_dn_onorm_g': out['m_dn_onorm_g'], 'm_sg_ln_g': out['m_sg_ln_g'], 'm_sg_ln_b': out['m_sg_ln_b'], 'm_sg_w': out['m_sg_w'], 'm_sg_b': out['m_sg_b'], 'm_w_branch_a': out['m_w_branch_a'], 'm_w_branch_b': out['m_w_branch_b'], 'm_w_out': out['m_w_out'], 'm_norm2_g': out['m_norm2_g'], 'm_ffn_w_gate': out['m_ffn_w_gate'], 'm_ffn_w_up': out['m_ffn_w_up'], 'm_ffn_conv_w': out['m_ffn_conv_w'], 'm_ffn_conv_b': out['m_ffn_conv_b'], 'm_ffn_w_down': out['m_ffn_w_down'], 'm_final_norm_g': out['m_final_norm_g'], 'v_norm1_g': out['v_norm1_g'], 'v_w_in': out['v_w_in'], 'v_dn_conv_w': out['v_dn_conv_w'], 'v_dn_a_log': out['v_dn_a_log'], 'v_dn_dt_bias': out['v_dn_dt_bias'], 'v_dn_onorm_g': out['v_dn_onorm_g'], 'v_sg_ln_g': out['v_sg_ln_g'], 'v_sg_ln_b': out['v_sg_ln_b'], 'v_sg_w': out['v_sg_w'], 'v_sg_b': out['v_sg_b'], 'v_w_branch_a': out['v_w_branch_a'], 'v_w_branch_b': out['v_w_branch_b'], 'v_w_out': out['v_w_out'], 'v_norm2_g': out['v_norm2_g'], 'v_ffn_w_gate': out['v_ffn_w_gate'], 'v_ffn_w_up': out['v_ffn_w_up'], 'v_ffn_conv_w': out['v_ffn_conv_w'], 'v_ffn_conv_b': out['v_ffn_conv_b'], 'v_ffn_w_down': out['v_ffn_w_down'], 'v_final_norm_g': out['v_final_norm_g']}


def _loss(weights, diff, rest, loss_target):
    with _jax.named_scope("forward"):
        args = {**rest, TWIN_DIFF_INPUT: diff, **{k: w.astype(_WEIGHT_DTYPES[k]) for k, w in weights.items()}}
        y = _forward(args)
    with _jax.named_scope("loss_head"):
        err = _jnp.square(y.astype(_jnp.float32) - loss_target)
        return 0.5 * _jnp.sum(_jnp.mean(err, axis=-1)) if err.ndim else 0.5 * err


def _adamw(w, g, m, v):
    m = ADAM_B1 * m + (1.0 - ADAM_B1) * g
    v = ADAM_B2 * v + (1.0 - ADAM_B2) * _jnp.square(g)
    m_hat = m / (1.0 - ADAM_B1 ** ADAM_STEP)
    v_hat = v / (1.0 - ADAM_B2 ** ADAM_STEP)
    delta = -ADAM_LR * (m_hat / (_jnp.sqrt(v_hat) + ADAM_EPS) + ADAM_WD * w)
    return delta, m, v


def reference(x, norm1_g, w_in, dn_conv_w, dn_a_log, dn_dt_bias, dn_onorm_g, sg_ln_g, sg_ln_b, sg_w, sg_b, w_branch_a, w_branch_b, w_out, norm2_g, ffn_w_gate, ffn_w_up, ffn_conv_w, ffn_conv_b, ffn_w_down, final_norm_g, loss_target, m_norm1_g, m_w_in, m_dn_conv_w, m_dn_a_log, m_dn_dt_bias, m_dn_onorm_g, m_sg_ln_g, m_sg_ln_b, m_sg_w, m_sg_b, m_w_branch_a, m_w_branch_b, m_w_out, m_norm2_g, m_ffn_w_gate, m_ffn_w_up, m_ffn_conv_w, m_ffn_conv_b, m_ffn_w_down, m_final_norm_g, v_norm1_g, v_w_in, v_dn_conv_w, v_dn_a_log, v_dn_dt_bias, v_dn_onorm_g, v_sg_ln_g, v_sg_ln_b, v_sg_w, v_sg_b, v_w_branch_a, v_w_branch_b, v_w_out, v_norm2_g, v_ffn_w_gate, v_ffn_w_up, v_ffn_conv_w, v_ffn_conv_b, v_ffn_w_down, v_final_norm_g):
    given = dict(x=x, norm1_g=norm1_g, w_in=w_in, dn_conv_w=dn_conv_w, dn_a_log=dn_a_log, dn_dt_bias=dn_dt_bias, dn_onorm_g=dn_onorm_g, sg_ln_g=sg_ln_g, sg_ln_b=sg_ln_b, sg_w=sg_w, sg_b=sg_b, w_branch_a=w_branch_a, w_branch_b=w_branch_b, w_out=w_out, norm2_g=norm2_g, ffn_w_gate=ffn_w_gate, ffn_w_up=ffn_w_up, ffn_conv_w=ffn_conv_w, ffn_conv_b=ffn_conv_b, ffn_w_down=ffn_w_down, final_norm_g=final_norm_g, loss_target=loss_target, m_norm1_g=m_norm1_g, m_w_in=m_w_in, m_dn_conv_w=m_dn_conv_w, m_dn_a_log=m_dn_a_log, m_dn_dt_bias=m_dn_dt_bias, m_dn_onorm_g=m_dn_onorm_g, m_sg_ln_g=m_sg_ln_g, m_sg_ln_b=m_sg_ln_b, m_sg_w=m_sg_w, m_sg_b=m_sg_b, m_w_branch_a=m_w_branch_a, m_w_branch_b=m_w_branch_b, m_w_out=m_w_out, m_norm2_g=m_norm2_g, m_ffn_w_gate=m_ffn_w_gate, m_ffn_w_up=m_ffn_w_up, m_ffn_conv_w=m_ffn_conv_w, m_ffn_conv_b=m_ffn_conv_b, m_ffn_w_down=m_ffn_w_down, m_final_norm_g=m_final_norm_g, v_norm1_g=v_norm1_g, v_w_in=v_w_in, v_dn_conv_w=v_dn_conv_w, v_dn_a_log=v_dn_a_log, v_dn_dt_bias=v_dn_dt_bias, v_dn_onorm_g=v_dn_onorm_g, v_sg_ln_g=v_sg_ln_g, v_sg_ln_b=v_sg_ln_b, v_sg_w=v_sg_w, v_sg_b=v_sg_b, v_w_branch_a=v_w_branch_a, v_w_branch_b=v_w_branch_b, v_w_out=v_w_out, v_norm2_g=v_norm2_g, v_ffn_w_gate=v_ffn_w_gate, v_ffn_w_up=v_ffn_w_up, v_ffn_conv_w=v_ffn_conv_w, v_ffn_conv_b=v_ffn_conv_b, v_ffn_w_down=v_ffn_w_down, v_final_norm_g=v_final_norm_g)
    weights = {n: given[n] for n in TWIN_WEIGHTS}
    shared = {n: given[n] for n in SHARED_INPUTS}
    per_example = {n: given[n] for n in ['x']}
    grad_fn = _jax.value_and_grad(_loss, argnums=(0, 1))

    def one_microbatch(ex, loss_target):
        ex = dict(ex)
        diff = ex.pop(TWIN_DIFF_INPUT)
        return grad_fn(weights, diff, {**shared, **ex}, loss_target)

    if N_MICROBATCH == 1:
        loss, (grad_w, grad_x) = one_microbatch(per_example, given["loss_target"])
    else:
        def body(carry, xs):
            loss_sum, grad_sum = carry
            l_k, (gw_k, gx_k) = one_microbatch(xs[0], xs[1])
            with _jax.named_scope("update"):
                return (loss_sum + l_k, _jax.tree.map(_jnp.add, grad_sum, gw_k)), gx_k

        init = (_jnp.zeros((), _jnp.float32), _jax.tree.map(_jnp.zeros_like, weights))
        (loss, grad_w), grad_x = _jax.lax.scan(body, init, (per_example, given["loss_target"]))
    with _jax.named_scope("update"):
        delta_w, new_m, new_v = {}, {}, {}
        for n in TWIN_WEIGHTS:
            delta_w[n], new_m[n], new_v[n] = _adamw(weights[n], grad_w[n], given["m_" + n], given["v_" + n])
    return (loss, grad_x, *[grad_w[n] for n in TWIN_WEIGHTS], *[delta_w[n] for n in TWIN_WEIGHTS],
            *[new_m[n] for n in TWIN_WEIGHTS], *[new_v[n] for n in TWIN_WEIGHTS])
```

```python
import functools
import math

import jax
import jax.numpy as jnp
from jax import lax
from jax.experimental import pallas as pl
from jax.experimental.pallas import tpu as pltpu

F32 = jnp.float32
BF16 = jnp.bfloat16
MESH = pl.DeviceIdType.MESH

EPS = 1e-6
H = 8
DH = 128
WD = H * DH
DNC = 64
SGC = 128
DN_K = 4
FF_K = 3
DEPTH = 2
N_CHIPS = 4

ADAM_LR = 0.001
ADAM_B1 = 0.9
ADAM_B2 = 0.999
ADAM_EPS = 1e-08
ADAM_WD = 0.01
ADAM_STEP = 10

VMEM_LIMIT = 56 * 1024 * 1024

NN = (((1,), (0,)), ((), ()))
NT = (((1,), (1,)), ((), ()))
TN = (((0,), (0,)), ((), ()))

OQ, OZ, OU, OV, OGA = 0, 3 * WD, 4 * WD, 5 * WD, 6 * WD


def _cp(*sem):
    return pltpu.CompilerParams(dimension_semantics=sem or None, vmem_limit_bytes=VMEM_LIMIT)


def _tile(dim, pref, unit=128):
    if dim <= pref:
        return dim
    t = (pref // unit) * unit
    while t >= unit:
        if dim % t == 0:
            return t
        t -= unit
    return dim


def _hdot(a, b, dn=NN):
    return lax.dot_general(a, b, dn, precision=lax.Precision.HIGHEST, preferred_element_type=F32)


def _bdot(a, b, dn=NN):
    return lax.dot_general(a.astype(BF16), b.astype(BF16), dn, preferred_element_type=F32)


def _lsum(x):
    return jnp.sum(x, axis=1, keepdims=True)


def _sig(x):
    return jax.nn.sigmoid(x)


def _dsilu(x):
    s = _sig(x)
    return s * (1.0 + x * (1.0 - s))


def _erf(x):
    a = jnp.abs(x)
    t = 1.0 / (1.0 + 0.3275911 * a)
    poly = t * (0.254829592 + t * (-0.284496736 + t * (1.421413741 + t * (-1.453152027 + t * 1.061405429))))
    r = 1.0 - poly * jnp.exp(-a * a)
    return jnp.where(x < 0, -r, r)


def _gelu(x):
    return 0.5 * x * (1.0 + _erf(x * (2.0 ** -0.5)))


def _dgelu(x):
    cdf = 0.5 * (1.0 + _erf(x * (2.0 ** -0.5)))
    pdf = jnp.exp(-0.5 * x * x) * (1.0 / math.sqrt(2.0 * math.pi))
    return cdf + x * pdf


def _shift_down(x, k):
    if k == 0:
        return x
    rows = lax.broadcasted_iota(jnp.int32, x.shape, 0)
    return jnp.where(rows >= k, pltpu.roll(x, k, 0), 0.0)


def _shift_up(x, k):
    if k == 0:
        return x
    n = x.shape[0]
    rows = lax.broadcasted_iota(jnp.int32, x.shape, 0)
    return jnp.where(rows < n - k, pltpu.roll(x, n - k, 0), 0.0)


def _mm(a, b, mode, out_dtype, add=None, name="mm"):
    if mode == "tn":
        K, M = a.shape
    else:
        M, K = a.shape
    N = b.shape[0] if mode == "nt" else b.shape[1]
    tm, tn, tk = _tile(M, 1024), _tile(N, 1536), _tile(K, 512)
    nk = K // tk
    dn = {"nn": NN, "nt": NT, "tn": TN}[mode]

    def body(a_ref, b_ref, *rest):
        if add is None:
            o_ref, acc_ref = rest
        else:
            add_ref, o_ref, acc_ref = rest
        k = pl.program_id(2)

        @pl.when(k == 0)
        def _():
            acc_ref[...] = jnp.zeros_like(acc_ref)

        acc_ref[...] += lax.dot_general(a_ref[...], b_ref[...], dn, preferred_element_type=F32)

        @pl.when(k == nk - 1)
        def _():
            r = acc_ref[...]
            if add is not None:
                r = r + add_ref[...]
            o_ref[...] = r.astype(o_ref.dtype)

    a_spec = (pl.BlockSpec((tk, tm), lambda i, j, k: (k, i)) if mode == "tn"
              else pl.BlockSpec((tm, tk), lambda i, j, k: (i, k)))
    b_spec = (pl.BlockSpec((tn, tk), lambda i, j, k: (j, k)) if mode == "nt"
              else pl.BlockSpec((tk, tn), lambda i, j, k: (k, j)))
    o_spec = pl.BlockSpec((tm, tn), lambda i, j, k: (i, j))
    in_specs = [a_spec, b_spec] + ([o_spec] if add is not None else [])
    args = (a, b) + ((add,) if add is not None else ())
    return pl.pallas_call(
        body, name=name, grid=(M // tm, N // tn, nk), in_specs=in_specs, out_specs=o_spec,
        out_shape=jax.ShapeDtypeStruct((M, N), out_dtype),
        scratch_shapes=[pltpu.VMEM((tm, tn), F32)],
        compiler_params=_cp("parallel", "parallel", "arbitrary"),
    )(*args)


def _rms_fwd(x, g, name):
    T, D = x.shape
    tt = _tile(T, 256, 16)

    def body(x_ref, g_ref, o_ref):
        xv = x_ref[...]
        r = lax.rsqrt(jnp.mean(xv * xv, axis=-1, keepdims=True) + EPS)
        o_ref[...] = (xv * r * g_ref[...]).astype(o_ref.dtype)

    return pl.pallas_call(
        body, name=name, grid=(T // tt,),
        in_specs=[pl.BlockSpec((tt, D), lambda i: (i, 0)), pl.BlockSpec((1, D), lambda i: (0, 0))],
        out_specs=pl.BlockSpec((tt, D), lambda i: (i, 0)),
        out_shape=jax.ShapeDtypeStruct((T, D), BF16), compiler_params=_cp("parallel"),
    )(x, g)


def _rms_bwd(x, g, dh, dres, name):
    T, D = x.shape
    tt = _tile(T, 256, 16)

    def body(x_ref, g_ref, dh_ref, dres_ref, dx_ref, dg_ref):
        @pl.when(pl.program_id(0) == 0)
        def _():
            dg_ref[...] = jnp.zeros_like(dg_ref)

        xv = x_ref[...]
        r = lax.rsqrt(jnp.mean(xv * xv, axis=-1, keepdims=True) + EPS)
        xh = xv * r
        dh_v = dh_ref[...]
        dy = dh_v * g_ref[...]
        dx_ref[...] = dres_ref[...] + r * (dy - xh * jnp.mean(dy * xh, axis=-1, keepdims=True))
        dg_ref[...] += jnp.sum(dh_v * xh, axis=0, keepdims=True)

    row = pl.BlockSpec((tt, D), lambda i: (i, 0))
    vec = pl.BlockSpec((1, D), lambda i: (0, 0))
    return pl.pallas_call(
        body, name=name, grid=(T // tt,), in_specs=[row, vec, row, row], out_specs=[row, vec],
        out_shape=[jax.ShapeDtypeStruct((T, D), F32), jax.ShapeDtypeStruct((1, D), F32)],
        compiler_params=_cp("arbitrary"),
    )(x, g, dh, dres)


def _loss_head(x, g, tgt, name="loss_head"):
    T, D = x.shape
    tt = _tile(T, 256, 16)

    def body(x_ref, g_ref, t_ref, dx_ref, dg_ref, loss_ref):
        @pl.when(pl.program_id(0) == 0)
        def _():
            dg_ref[...] = jnp.zeros_like(dg_ref)
            loss_ref[...] = jnp.zeros_like(loss_ref)

        xv = x_ref[...]
        r = lax.rsqrt(jnp.mean(xv * xv, axis=-1, keepdims=True) + EPS)
        xh = xv * r
        err = xh * g_ref[...] - t_ref[...]
        part = 0.5 * jnp.sum(jnp.mean(err * err, axis=-1, keepdims=True), axis=0, keepdims=True)
        loss_ref[...] += jnp.broadcast_to(part, loss_ref.shape)
        dy = err * (1.0 / D)
        dg_ref[...] += jnp.sum(dy * xh, axis=0, keepdims=True)
        dyh = dy * g_ref[...]
        dx_ref[...] = r * (dyh - xh * jnp.mean(dyh * xh, axis=-1, keepdims=True))

    row = pl.BlockSpec((tt, D), lambda i: (i, 0))
    vec = pl.BlockSpec((1, D), lambda i: (0, 0))
    return pl.pallas_call(
        body, name=name, grid=(T // tt,), in_specs=[row, vec, row],
        out_specs=[row, vec, pl.BlockSpec((1, 128), lambda i: (0, 0))],
        out_shape=[jax.ShapeDtypeStruct((T, D), F32), jax.ShapeDtypeStruct((1, D), F32),
                   jax.ShapeDtypeStruct((1, 128), F32)],
        compiler_params=_cp("arbitrary"),
    )(x, g, tgt)


def _ba_fwd(proj, alog, dtb, oba, name="dn_ba_fwd"):
    T = proj.shape[0]
    tt = _tile(T, 512, 8)

    def body(p_ref, al_ref, dt_ref, o_ref):
        raw = p_ref[...]
        lane = lax.broadcasted_iota(jnp.int32, raw.shape, 1)
        z = raw + dt_ref[...]
        sp = jnp.maximum(z, 0.0) + jnp.log(1.0 + jnp.exp(-jnp.abs(z)))
        gl = -jnp.exp(al_ref[...]) * sp
        o_ref[...] = jnp.where(lane < H, _sig(raw), jnp.where(lane < 2 * H, gl, 0.0))

    vec = pl.BlockSpec((1, 128), lambda i: (0, 0))
    return pl.pallas_call(
        body, name=name, grid=(T // tt,),
        in_specs=[pl.BlockSpec((tt, 128), lambda i: (i, oba // 128)), vec, vec],
        out_specs=pl.BlockSpec((tt, 128), lambda i: (i, 0)),
        out_shape=jax.ShapeDtypeStruct((T, 128), F32), compiler_params=_cp("parallel"),
    )(proj, alog, dtb)


def _ba_bwd(proj, alog, dtb, dbg, oba, name="dn_ba_bwd"):
    T = proj.shape[0]
    tt = _tile(T, 512, 16)

    def body(p_ref, al_ref, dt_ref, d_ref, o_ref, dal_ref, ddt_ref):
        @pl.when(pl.program_id(0) == 0)
        def _():
            dal_ref[...] = jnp.zeros_like(dal_ref)
            ddt_ref[...] = jnp.zeros_like(ddt_ref)

        raw = p_ref[...]
        d = d_ref[0]
        for hh in range(1, H):
            d = d + d_ref[hh]
        lane = lax.broadcasted_iota(jnp.int32, raw.shape, 1)
        z = raw + dt_ref[...]
        sp = jnp.maximum(z, 0.0) + jnp.log(1.0 + jnp.exp(-jnp.abs(z)))
        na = -jnp.exp(al_ref[...])
        is_g = jnp.logical_and(lane >= H, lane < 2 * H)
        b = _sig(raw)
        dz = jnp.where(is_g, d * na * _sig(z), 0.0)
        o_ref[...] = jnp.where(lane < H, d * b * (1.0 - b), dz).astype(o_ref.dtype)
        dal_ref[...] += jnp.sum(jnp.where(is_g, d * na * sp, 0.0), axis=0, keepdims=True)
        ddt_ref[...] += jnp.sum(dz, axis=0, keepdims=True)

    vec = pl.BlockSpec((1, 128), lambda i: (0, 0))
    return pl.pallas_call(
        body, name=name, grid=(T // tt,),
        in_specs=[pl.BlockSpec((tt, 128), lambda i: (i, oba // 128)), vec, vec,
                  pl.BlockSpec((H, tt, 128), lambda i: (0, i, 0))],
        out_specs=[pl.BlockSpec((tt, 128), lambda i: (i, 0)), vec, vec],
        out_shape=[jax.ShapeDtypeStruct((T, 128), BF16), jax.ShapeDtypeStruct((1, 128), F32),
                   jax.ShapeDtypeStruct((1, 128), F32)],
        compiler_params=_cp("arbitrary"),
    )(proj, alog, dtb, dbg)


def _dn_prep_fwd(proj, convw, name="dn_prep_fwd"):
    T = proj.shape[0]
    nblk = 3 * H

    def body(p_ref, w_ref, o_ref):
        j = pl.program_id(0)
        xv = p_ref[...]
        w = w_ref[...]
        c = xv * w[DN_K - 1:DN_K, :]
        for k in range(1, DN_K):
            c = c + _shift_down(xv, k) * w[DN_K - 1 - k:DN_K - k, :]
        s = c * _sig(c)
        r = lax.rsqrt(_lsum(s * s) + EPS)
        o_ref[...] = jnp.where(j < 2 * H, s * r, s)

    return pl.pallas_call(
        body, name=name, grid=(nblk,),
        in_specs=[pl.BlockSpec((T, DH), lambda j: (0, j)), pl.BlockSpec((DN_K, DH), lambda j: (0, j))],
        out_specs=pl.BlockSpec((T, DH), lambda j: (0, j)),
        out_shape=jax.ShapeDtypeStruct((T, 3 * WD), F32), compiler_params=_cp("parallel"),
    )(proj, convw)


def _dn_prep_bwd(proj, convw, dq, dk, dv, name="dn_prep_bwd"):
    T = proj.shape[0]
    nblk = 3 * H

    def body(p_ref, w_ref, dq_ref, dk_ref, dv_ref, dx_ref, dw_ref):
        j = pl.program_id(0)
        xv = p_ref[...]
        w = w_ref[...]
        shifted = [_shift_down(xv, k) for k in range(DN_K)]
        c = shifted[0] * w[DN_K - 1:DN_K, :]
        for k in range(1, DN_K):
            c = c + shifted[k] * w[DN_K - 1 - k:DN_K - k, :]
        s = c * _sig(c)
        r = lax.rsqrt(_lsum(s * s) + EPS)
        y = s * r
        dy = jnp.where(j < H, dq_ref[...], jnp.where(j < 2 * H, dk_ref[...], dv_ref[...]))
        ds = jnp.where(j < 2 * H, r * (dy - y * _lsum(dy * y)), dy)
        dc = ds * _dsilu(c)
        dx = dc * w[DN_K - 1:DN_K, :]
        for k in range(1, DN_K):
            dx = dx + _shift_up(dc, k) * w[DN_K - 1 - k:DN_K - k, :]
        dx_ref[...] = dx.astype(dx_ref.dtype)
        rows = [jnp.sum(dc * shifted[DN_K - 1 - t], axis=0, keepdims=True) for t in range(DN_K)]
        dw_ref[...] = jnp.concatenate(rows, axis=0)

    hb = lambda off: pl.BlockSpec((T, DH), lambda j: (0, jnp.maximum(jnp.minimum(j - off, H - 1), 0)))
    return pl.pallas_call(
        body, name=name, grid=(nblk,),
        in_specs=[pl.BlockSpec((T, DH), lambda j: (0, j)), pl.BlockSpec((DN_K, DH), lambda j: (0, j)),
                  hb(0), hb(H), hb(2 * H)],
        out_specs=[pl.BlockSpec((T, DH), lambda j: (0, j)), pl.BlockSpec((DN_K, DH), lambda j: (0, j))],
        out_shape=[jax.ShapeDtypeStruct((T, 3 * WD), BF16), jax.ShapeDtypeStruct((DN_K, 3 * WD), F32)],
        compiler_params=_cp("parallel"),
    )(proj, convw, dq, dk, dv)


def _tri_inv(A):
    ri = lax.broadcasted_iota(jnp.int32, A.shape, 0)
    ci = lax.broadcasted_iota(jnp.int32, A.shape, 1)
    X = -A
    P = jnp.where(ri == ci, 1.0, 0.0) + X
    Y = X
    for _ in range(int(math.log2(DNC)) - 1):
        Y = _hdot(Y, Y)
        P = P + _hdot(P, Y)
    return P


def _dn_chunk(q_ref, k_ref, v_ref, bg_ref, r0, h):
    ri = lax.broadcasted_iota(jnp.int32, (DNC, DNC), 0)
    ci = lax.broadcasted_iota(jnp.int32, (DNC, DNC), 1)
    lane = lax.broadcasted_iota(jnp.int32, (DNC, DH), 1)
    row = lax.broadcasted_iota(jnp.int32, (DNC, DH), 0)
    causal = ri >= ci
    strict = ri > ci
    q = q_ref[pl.ds(r0, DNC), :] * (DH ** -0.5)
    k = k_ref[pl.ds(r0, DNC), :]
    v = v_ref[pl.ds(r0, DNC), :]
    bg = bg_ref[pl.ds(r0, DNC), :]
    beta = _lsum(jnp.where(lane == h, bg, 0.0))
    g = _lsum(jnp.where(lane == h + H, bg, 0.0))
    tril1 = jnp.where(causal, 1.0, 0.0)
    gc = _hdot(tril1, jnp.broadcast_to(g, (DNC, DH)))
    gcol = _hdot(tril1, jnp.broadcast_to(g, (DNC, DNC)))
    grow = _hdot(jnp.ones((DNC, DNC), F32), jnp.where(ri == ci, gcol, 0.0))
    dec = jnp.where(causal, jnp.exp(jnp.where(causal, gcol - grow, 0.0)), 0.0)
    eg = jnp.exp(gc)
    gl = jnp.sum(jnp.where(row == DNC - 1, gc, 0.0), axis=0, keepdims=True)
    ek = jnp.exp(gl - gc)
    egl = jnp.exp(gl)
    kb = k * beta
    vb = v * beta
    kbe = kb * eg
    A = jnp.where(strict, _bdot(kb, k, NT) * dec, 0.0)
    P = jnp.where(causal, _bdot(q, k, NT) * dec, 0.0)
    return dict(q=q, k=k, v=v, beta=beta, dec=dec, eg=eg, ek=ek, egl=egl, kb=kb, vb=vb, kbe=kbe, A=A, P=P,
                qd=q * eg, kd=k * ek, causal=causal, strict=strict, lane=lane, row=row)


def _dn_core_fwd(qkv, bg, name="dn_core_fwd"):
    T = qkv.shape[0]
    n_chunks = T // DNC

    def body(q_ref, k_ref, v_ref, bg_ref, o_ref, s_ref, tm_ref, S_scr):
        h = pl.program_id(0)
        S_scr[...] = jnp.zeros_like(S_scr)

        def chunk(n, carry):
            r0 = pl.multiple_of(n * DNC, DNC)
            c = _dn_chunk(q_ref, k_ref, v_ref, bg_ref, r0, h)
            Tm = _tri_inv(c["A"])
            tm_ref[0, n] = Tm
            S = S_scr[...]
            s_ref[0, n] = S
            u = _bdot(Tm, c["vb"])
            w = _bdot(Tm, c["kbe"])
            vn = u - _bdot(w, S)
            o_ref[pl.ds(r0, DNC), :] = _bdot(c["qd"], S) + _bdot(c["P"], vn)
            S_scr[...] = S * c["egl"] + _bdot(c["kd"], vn, TN)
            return carry

        lax.fori_loop(0, n_chunks, chunk, 0)

    col = lambda off: pl.BlockSpec((T, DH), lambda h: (0, h + off))
    return pl.pallas_call(
        body, name=name, grid=(H,),
        in_specs=[col(0), col(H), col(2 * H), pl.BlockSpec((T, 128), lambda h: (0, 0))],
        out_specs=[pl.BlockSpec((T, DH), lambda h: (0, h)),
                   pl.BlockSpec((1, n_chunks, DH, DH), lambda h: (h, 0, 0, 0)),
                   pl.BlockSpec((1, n_chunks, DNC, DNC), lambda h: (h, 0, 0, 0))],
        out_shape=[jax.ShapeDtypeStruct((T, WD), F32), jax.ShapeDtypeStruct((H, n_chunks, DH, DH), F32),
                   jax.ShapeDtypeStruct((H, n_chunks, DNC, DNC), F32)],
        scratch_shapes=[pltpu.VMEM((DH, DH), F32)],
        compiler_params=_cp("parallel"),
    )(qkv, qkv, qkv, bg)


def _dn_core_bwd(qkv, bg, s_all, tm_all, do, name="dn_core_bwd"):
    T = qkv.shape[0]
    n_chunks = T // DNC

    def body(q_ref, k_ref, v_ref, bg_ref, s_ref, tm_ref, do_ref, dq_ref, dk_ref, dv_ref, dbg_ref, dS_scr):
        h = pl.program_id(0)
        dS_scr[...] = jnp.zeros_like(dS_scr)

        def chunk(i, carry):
            n = n_chunks - 1 - i
            r0 = pl.multiple_of(n * DNC, DNC)
            c = _dn_chunk(q_ref, k_ref, v_ref, bg_ref, r0, h)
            q, k, v, beta = c["q"], c["k"], c["v"], c["beta"]
            dec, eg, ek, egl = c["dec"], c["eg"], c["ek"], c["egl"]
            kb, vb, kbe, A, P, qd, kd = c["kb"], c["vb"], c["kbe"], c["A"], c["P"], c["qd"], c["kd"]
            S = s_ref[0, n]
            Tm = tm_ref[0, n]
            u = _bdot(Tm, vb)
            w = _bdot(Tm, kbe)
            vn = u - _bdot(w, S)
            d_o = do_ref[pl.ds(r0, DNC), :]
            dS1 = dS_scr[...]
            d_qd = _bdot(d_o, S, NT)
            dP = jnp.where(c["causal"], _bdot(d_o, vn, NT), 0.0)
            d_vn = _bdot(P, d_o, TN) + _bdot(kd, dS1)
            d_kd = _bdot(vn, dS1, NT)
            d_egl = jnp.sum(jnp.sum(dS1 * S, axis=1, keepdims=True), axis=0, keepdims=True)
            dS_scr[...] = dS1 * egl + _bdot(qd, d_o, TN) - _bdot(w, d_vn, TN)
            d_w = -_bdot(d_vn, S, NT)
            d_vb = _bdot(Tm, d_vn, TN)
            d_kbe = _bdot(Tm, d_w, TN)
            dA = jnp.where(c["strict"], -(_bdot(d_vb, u, NT) + _bdot(d_kbe, w, NT)), 0.0)
            dMA = dA * dec
            dMP = dP * dec
            d_kb = _bdot(dMA, k) + d_kbe * eg
            d_k = _bdot(dMA, kb, TN) + _bdot(dMP, q, TN) + d_kd * ek + d_kb * beta
            d_qs = _bdot(dMP, k) + d_qd * eg
            E = dA * A + dP * P
            ones = jnp.ones((DNC, DH), F32)
            t_kd = _lsum(d_kd * kd)
            d_gc = _hdot(E, ones) - _hdot(E, ones, TN) + _lsum(d_qd * qd) + _lsum(d_kbe * kbe) - t_kd
            d_gl = jnp.sum(t_kd, axis=0, keepdims=True) + d_egl * egl
            d_gc = d_gc + jnp.where(c["row"] == DNC - 1, d_gl, 0.0)
            triu1 = jnp.where(c["strict"], 0.0, 1.0)
            d_g = _hdot(triu1, d_gc)
            d_beta = _lsum(d_kb * k) + _lsum(d_vb * v)
            dq_ref[pl.ds(r0, DNC), :] = d_qs * (DH ** -0.5)
            dk_ref[pl.ds(r0, DNC), :] = d_k
            dv_ref[pl.ds(r0, DNC), :] = d_vb * beta
            lane = c["lane"]
            dbg_ref[0, pl.ds(r0, DNC), :] = (jnp.where(lane == h, d_beta, 0.0) + jnp.where(lane == h + H, d_g, 0.0))
            return carry

        lax.fori_loop(0, n_chunks, chunk, 0)

    col = lambda off: pl.BlockSpec((T, DH), lambda h: (0, h + off))
    hcol = pl.BlockSpec((T, DH), lambda h: (0, h))
    return pl.pallas_call(
        body, name=name, grid=(H,),
        in_specs=[col(0), col(H), col(2 * H), pl.BlockSpec((T, 128), lambda h: (0, 0)),
                  pl.BlockSpec((1, n_chunks, DH, DH), lambda h: (h, 0, 0, 0)),
                  pl.BlockSpec((1, n_chunks, DNC, DNC), lambda h: (h, 0, 0, 0)), hcol],
        out_specs=[hcol, hcol, hcol, pl.BlockSpec((1, T, 128), lambda h: (h, 0, 0))],
        out_shape=[jax.ShapeDtypeStruct((T, WD), F32)] * 3 + [jax.ShapeDtypeStruct((H, T, 128), F32)],
        scratch_shapes=[pltpu.VMEM((DH, DH), F32)],
        compiler_params=_cp("parallel"),
    )(qkv, qkv, qkv, bg, s_all, tm_all, do)


def _dn_post_fwd(o, proj, gon, name="dn_post_fwd"):
    T = o.shape[0]
    tt = _tile(T, 256, 16)

    def body(o_ref, z_ref, g_ref, y_ref):
        for hh in range(H):
            sl = slice(hh * DH, (hh + 1) * DH)
            ov = o_ref[:, sl]
            zv = z_ref[:, sl]
            r = lax.rsqrt(jnp.mean(ov * ov, axis=-1, keepdims=True) + EPS)
            y_ref[:, sl] = (ov * r * g_ref[...] * (zv * _sig(zv))).astype(y_ref.dtype)

    return pl.pallas_call(
        body, name=name, grid=(T // tt,),
        in_specs=[pl.BlockSpec((tt, WD), lambda i: (i, 0)), pl.BlockSpec((tt, WD), lambda i: (i, OZ // WD)),
                  pl.BlockSpec((1, DH), lambda i: (0, 0))],
        out_specs=pl.BlockSpec((tt, WD), lambda i: (i, 0)),
        out_shape=jax.ShapeDtypeStruct((T, WD), BF16), compiler_params=_cp("parallel"),
    )(o, proj, gon)


def _dn_post_bwd(o, proj, gon, dy, name="dn_post_bwd"):
    T = o.shape[0]
    tt = _tile(T, 256, 16)

    def body(o_ref, z_ref, g_ref, dy_ref, do_ref, dz_ref, dg_ref):
        @pl.when(pl.program_id(0) == 0)
        def _():
            dg_ref[...] = jnp.zeros_like(dg_ref)

        acc = jnp.zeros((1, DH), F32)
        for hh in range(H):
            sl = slice(hh * DH, (hh + 1) * DH)
            ov = o_ref[:, sl]
            zv = z_ref[:, sl]
            dyv = dy_ref[:, sl]
            r = lax.rsqrt(jnp.mean(ov * ov, axis=-1, keepdims=True) + EPS)
            oh = ov * r
            nrm = oh * g_ref[...]
            dn = dyv * (zv * _sig(zv))
            dz_ref[:, sl] = (dyv * nrm * _dsilu(zv)).astype(dz_ref.dtype)
            doh = dn * g_ref[...]
            do_ref[:, sl] = r * (doh - oh * jnp.mean(doh * oh, axis=-1, keepdims=True))
            acc = acc + jnp.sum(dn * oh, axis=0, keepdims=True)
        dg_ref[...] += acc

    row = pl.BlockSpec((tt, WD), lambda i: (i, 0))
    vec = pl.BlockSpec((1, DH), lambda i: (0, 0))
    return pl.pallas_call(
        body, name=name, grid=(T // tt,),
        in_specs=[row, pl.BlockSpec((tt, WD), lambda i: (i, OZ // WD)), vec, row],
        out_specs=[row, row, vec],
        out_shape=[jax.ShapeDtypeStruct((T, WD), F32), jax.ShapeDtypeStruct((T, WD), BF16),
                   jax.ShapeDtypeStruct((1, DH), F32)],
        compiler_params=_cp("arbitrary"),
    )(o, proj, gon, dy)


def _sg_common(u_ref, v_ref, lng_ref, lnb_ref):
    ur = u_ref[...]
    vr = v_ref[...]
    vgel = _gelu(vr)
    mu = jnp.mean(vgel, axis=-1, keepdims=True)
    xc = vgel - mu
    rs = lax.rsqrt(jnp.mean(xc * xc, axis=-1, keepdims=True) + EPS)
    xh = xc * rs
    vg = xh * lng_ref[...] + lnb_ref[...]
    return ur, vr, rs, xh, vg


def _sg_fwd(proj, lng, lnb, sgw, sgbt, name="sg_fwd"):
    T = proj.shape[0]

    def body(u_ref, v_ref, lng_ref, lnb_ref, w_ref, bt_ref, y_ref):
        ur, _, _, _, vg = _sg_common(u_ref, v_ref, lng_ref, lnb_ref)
        ri = lax.broadcasted_iota(jnp.int32, (SGC, SGC), 0)
        ci = lax.broadcasted_iota(jnp.int32, (SGC, SGC), 1)
        ug = _gelu(ur)
        for g in range(H):
            sl = slice(g * DH, (g + 1) * DH)
            ws = jnp.where(ri >= ci, w_ref[g], 0.0)
            mixed = _bdot(ws, vg[:, sl]) + bt_ref[:, g:g + 1]
            y_ref[:, sl] = (ug[:, sl] * mixed).astype(y_ref.dtype)

    vec = pl.BlockSpec((1, WD), lambda i: (0, 0))
    return pl.pallas_call(
        body, name=name, grid=(T // SGC,),
        in_specs=[pl.BlockSpec((SGC, WD), lambda i: (i, OU // WD)), pl.BlockSpec((SGC, WD), lambda i: (i, OV // WD)),
                  vec, vec, pl.BlockSpec((H, SGC, SGC), lambda i: (0, 0, 0)),
                  pl.BlockSpec((SGC, H), lambda i: (0, 0))],
        out_specs=pl.BlockSpec((SGC, WD), lambda i: (i, 0)),
        out_shape=jax.ShapeDtypeStruct((T, WD), BF16), compiler_params=_cp("parallel"),
    )(proj, proj, lng, lnb, sgw, sgbt)


def _sg_bwd(proj, lng, lnb, sgw, sgbt, dy, name="sg_bwd"):
    T = proj.shape[0]

    def body(u_ref, v_ref, lng_ref, lnb_ref, w_ref, bt_ref, dy_ref,
             du_ref, dv_ref, dw_ref, dbt_ref, dlng_ref, dlnb_ref):
        @pl.when(pl.program_id(0) == 0)
        def _():
            dw_ref[...] = jnp.zeros_like(dw_ref)
            dbt_ref[...] = jnp.zeros_like(dbt_ref)
            dlng_ref[...] = jnp.zeros_like(dlng_ref)
            dlnb_ref[...] = jnp.zeros_like(dlnb_ref)

        ur, vr, rs, xh, vg = _sg_common(u_ref, v_ref, lng_ref, lnb_ref)
        ri = lax.broadcasted_iota(jnp.int32, (SGC, SGC), 0)
        ci = lax.broadcasted_iota(jnp.int32, (SGC, SGC), 1)
        ug = _gelu(ur)
        dyv = dy_ref[...]
        dbt = jnp.zeros((SGC, 128), F32)
        dvg_parts = []
        for g in range(H):
            sl = slice(g * DH, (g + 1) * DH)
            ws = jnp.where(ri >= ci, w_ref[g], 0.0)
            mixed = _bdot(ws, vg[:, sl]) + bt_ref[:, g:g + 1]
            dyg = dyv[:, sl]
            du_ref[:, sl] = (dyg * mixed * _dgelu(ur[:, sl])).astype(du_ref.dtype)
            dmix = dyg * ug[:, sl]
            dw_ref[g] += jnp.where(ri >= ci, _bdot(dmix, vg[:, sl], NT), 0.0)
            dbt = dbt + jnp.where(ci == g, _lsum(dmix), 0.0)
            dvg_parts.append(_bdot(ws, dmix, TN))
        dbt_ref[...] += dbt
        dvg = jnp.concatenate(dvg_parts, axis=1)
        dlng_ref[...] += jnp.sum(dvg * xh, axis=0, keepdims=True)
        dlnb_ref[...] += jnp.sum(dvg, axis=0, keepdims=True)
        dxh = dvg * lng_ref[...]
        dvgel = rs * (dxh - jnp.mean(dxh, axis=-1, keepdims=True) - xh * jnp.mean(dxh * xh, axis=-1, keepdims=True))
        dv_ref[...] = (dvgel * _dgelu(vr)).astype(dv_ref.dtype)

    vec = pl.BlockSpec((1, WD), lambda i: (0, 0))
    row = pl.BlockSpec((SGC, WD), lambda i: (i, 0))
    return pl.pallas_call(
        body, name=name, grid=(T // SGC,),
        in_specs=[pl.BlockSpec((SGC, WD), lambda i: (i, OU // WD)), pl.BlockSpec((SGC, WD), lambda i: (i, OV // WD)),
                  vec, vec, pl.BlockSpec((H, SGC, SGC), lambda i: (0, 0, 0)),
                  pl.BlockSpec((SGC, H), lambda i: (0, 0)), row],
        out_specs=[row, row, pl.BlockSpec((H, SGC, SGC), lambda i: (0, 0, 0)),
                   pl.BlockSpec((SGC, 128), lambda i: (0, 0)), vec, vec],
        out_shape=[jax.ShapeDtypeStruct((T, WD), BF16), jax.ShapeDtypeStruct((T, WD), BF16),
                   jax.ShapeDtypeStruct((H, SGC, SGC), F32), jax.ShapeDtypeStruct((SGC, 128), F32),
                   jax.ShapeDtypeStruct((1, WD), F32), jax.ShapeDtypeStruct((1, WD), F32)],
        compiler_params=_cp("arbitrary"),
    )(proj, proj, lng, lnb, sgw, sgbt, dy)


def _merge_fwd(proj, yap, ybp, D, name="merge_fwd"):
    T = proj.shape[0]
    tt = _tile(T, 256, 16)

    def body(ga_ref, gb_ref, a_ref, b_ref, o_ref):
        o_ref[...] = (_sig(ga_ref[...]) * a_ref[...] + _sig(gb_ref[...]) * b_ref[...]).astype(o_ref.dtype)

    row = pl.BlockSpec((tt, D), lambda i: (i, 0))
    return pl.pallas_call(
        body, name=name, grid=(T // tt,),
        in_specs=[pl.BlockSpec((tt, D), lambda i: (i, OGA // D)), pl.BlockSpec((tt, D), lambda i: (i, OGA // D + 1)),
                  row, row],
        out_specs=row, out_shape=jax.ShapeDtypeStruct((T, D), BF16), compiler_params=_cp("parallel"),
    )(proj, proj, yap, ybp)


def _merge_bwd(proj, yap, ybp, dm, D, name="merge_bwd"):
    T = proj.shape[0]
    tt = _tile(T, 256, 16)

    def body(ga_ref, gb_ref, a_ref, b_ref, dm_ref, da_ref, db_ref, dga_ref, dgb_ref):
        d = dm_ref[...]
        sa = _sig(ga_ref[...])
        sb = _sig(gb_ref[...])
        da_ref[...] = (d * sa).astype(da_ref.dtype)
        db_ref[...] = (d * sb).astype(db_ref.dtype)
        dga_ref[...] = (d * a_ref[...] * sa * (1.0 - sa)).astype(dga_ref.dtype)
        dgb_ref[...] = (d * b_ref[...] * sb * (1.0 - sb)).astype(dgb_ref.dtype)

    row = pl.BlockSpec((tt, D), lambda i: (i, 0))
    return pl.pallas_call(
        body, name=name, grid=(T // tt,),
        in_specs=[pl.BlockSpec((tt, D), lambda i: (i, OGA // D)), pl.BlockSpec((tt, D), lambda i: (i, OGA // D + 1)),
                  row, row, row],
        out_specs=[row] * 4, out_shape=[jax.ShapeDtypeStruct((T, D), BF16)] * 4,
        compiler_params=_cp("parallel"),
    )(proj, proj, yap, ybp, dm)


def _ffn_act_fwd(gp, up, cw, cb, name="ffn_act_fwd"):
    T, F = gp.shape

    def body(g_ref, u_ref, w_ref, b_ref, o_ref):
        gv = g_ref[...]
        w = w_ref[...]
        c = gv * w[FF_K - 1:FF_K, :] + b_ref[...]
        for k in range(1, FF_K):
            c = c + _shift_down(gv, k) * w[FF_K - 1 - k:FF_K - k, :]
        o_ref[...] = (c * _sig(c) * u_ref[...]).astype(o_ref.dtype)

    col = pl.BlockSpec((T, 128), lambda j: (0, j))
    return pl.pallas_call(
        body, name=name, grid=(F // 128,),
        in_specs=[col, col, pl.BlockSpec((FF_K, 128), lambda j: (0, j)), pl.BlockSpec((1, 128), lambda j: (0, j))],
        out_specs=col, out_shape=jax.ShapeDtypeStruct((T, F), BF16), compiler_params=_cp("parallel"),
    )(gp, up, cw, cb)


def _ffn_act_bwd(gp, up, cw, cb, dact, name="ffn_act_bwd"):
    T, F = gp.shape

    def body(g_ref, u_ref, w_ref, b_ref, d_ref, dg_ref, du_ref, dw_ref, db_ref):
        gv = g_ref[...]
        w = w_ref[...]
        shifted = [_shift_down(gv, k) for k in range(FF_K)]
        c = shifted[0] * w[FF_K - 1:FF_K, :] + b_ref[...]
        for k in range(1, FF_K):
            c = c + shifted[k] * w[FF_K - 1 - k:FF_K - k, :]
        d = d_ref[...]
        du_ref[...] = (d * c * _sig(c)).astype(du_ref.dtype)
        dc = d * u_ref[...] * _dsilu(c)
        dg = dc * w[FF_K - 1:FF_K, :]
        for k in range(1, FF_K):
            dg = dg + _shift_up(dc, k) * w[FF_K - 1 - k:FF_K - k, :]
        dg_ref[...] = dg.astype(dg_ref.dtype)
        rows = [jnp.sum(dc * shifted[FF_K - 1 - t], axis=0, keepdims=True) for t in range(FF_K)]
        dw_ref[...] = jnp.concatenate(rows, axis=0)
        db_ref[...] = jnp.sum(dc, axis=0, keepdims=True)

    col = pl.BlockSpec((T, 128), lambda j: (0, j))
    wspec = pl.BlockSpec((FF_K, 128), lambda j: (0, j))
    bspec = pl.BlockSpec((1, 128), lambda j: (0, j))
    return pl.pallas_call(
        body, name=name, grid=(F // 128,),
        in_specs=[col, col, wspec, bspec, col], out_specs=[col, col, wspec, bspec],
        out_shape=[jax.ShapeDtypeStruct((T, F), BF16), jax.ShapeDtypeStruct((T, F), BF16),
                   jax.ShapeDtypeStruct((FF_K, F), F32), jax.ShapeDtypeStruct((1, F), F32)],
        compiler_params=_cp("parallel"),
    )(gp, up, cw, cb, dact)


def _layer_fwd(x, w):
    D = x.shape[1]
    oba = OGA + 2 * D
    h = _rms_fwd(x, w["norm1_g"], "rms1_fwd")
    proj = _mm(h, w["w_in"], "nn", F32, name="mm_proj")
    bg = _ba_fwd(proj, w["alog_row"], w["dtb_row"], oba)
    qkv = _dn_prep_fwd(proj, w["dn_conv_w"])
    o, s_all, tm_all = _dn_core_fwd(qkv, bg)
    ya = _dn_post_fwd(o, proj, w["dn_onorm_g"])
    yb = _sg_fwd(proj, w["sg_ln_g"], w["sg_ln_b"], w["sg_w"], w["sg_bt"])
    yap = _mm(ya, w["w_branch_a"], "nn", F32, name="mm_branch")
    ybp = _mm(yb, w["w_branch_b"], "nn", F32, name="mm_branch")
    merged = _merge_fwd(proj, yap, ybp, D)
    x1 = _mm(merged, w["w_out"], "nn", F32, add=x, name="mm_out")
    h2 = _rms_fwd(x1, w["norm2_g"], "rms2_fwd")
    gp = _mm(h2, w["ffn_w_gate"], "nn", F32, name="mm_ffn_in")
    up = _mm(h2, w["ffn_w_up"], "nn", F32, name="mm_ffn_in")
    act = _ffn_act_fwd(gp, up, w["ffn_conv_w"], w["ffn_conv_b"])
    x2 = _mm(act, w["ffn_w_down"], "nn", F32, add=x1, name="mm_ffn_down")
    saved = dict(x=x, h=h, proj=proj, bg=bg, qkv=qkv, o=o, s_all=s_all, tm_all=tm_all, ya=ya, yb=yb, yap=yap,
                 ybp=ybp, merged=merged, x1=x1, h2=h2, gp=gp, up=up, act=act)
    return x2, saved


def _layer_bwd(dx2, w, s):
    D = dx2.shape[1]
    oba = OGA + 2 * D
    g = {}
    dx2b = dx2.astype(BF16)
    dact = _mm(dx2b, w["ffn_w_down"], "nt", F32, name="mm_d_act")
    g["ffn_w_down"] = _mm(s["act"], dx2b, "tn", BF16, name="mm_dw_down")
    dgp, dup, g["ffn_conv_w"], g["ffn_conv_b"] = _ffn_act_bwd(s["gp"], s["up"], w["ffn_conv_w"], w["ffn_conv_b"], dact)
    dh2 = _mm(dgp, w["ffn_w_gate"], "nt", F32, name="mm_dh2")
    dh2 = _mm(dup, w["ffn_w_up"], "nt", F32, add=dh2, name="mm_dh2_acc")
    g["ffn_w_gate"] = _mm(s["h2"], dgp, "tn", BF16, name="mm_dw_ffn_in")
    g["ffn_w_up"] = _mm(s["h2"], dup, "tn", BF16, name="mm_dw_ffn_in")
    dx1, g["norm2_g"] = _rms_bwd(s["x1"], w["norm2_g"], dh2, dx2, "rms2_bwd")
    dx1b = dx1.astype(BF16)
    dm = _mm(dx1b, w["w_out"], "nt", F32, name="mm_d_merged")
    g["w_out"] = _mm(s["merged"], dx1b, "tn", BF16, name="mm_dw_out")
    dyap, dybp, dga, dgb = _merge_bwd(s["proj"], s["yap"], s["ybp"], dm, D)
    dya = _mm(dyap, w["w_branch_a"], "nt", F32, name="mm_d_branch")
    dyb = _mm(dybp, w["w_branch_b"], "nt", F32, name="mm_d_branch")
    g["w_branch_a"] = _mm(s["ya"], dyap, "tn", BF16, name="mm_dw_branch")
    g["w_branch_b"] = _mm(s["yb"], dybp, "tn", BF16, name="mm_dw_branch")
    du, dv, g["sg_w"], dbt, g["sg_ln_g"], g["sg_ln_b"] = _sg_bwd(
        s["proj"], w["sg_ln_g"], w["sg_ln_b"], w["sg_w"], w["sg_bt"], dyb)
    g["sg_b"] = jnp.transpose(dbt[:, :H])
    do, dz, g["dn_onorm_g"] = _dn_post_bwd(s["o"], s["proj"], w["dn_onorm_g"], dya)
    dq, dk, dvv, dbg = _dn_core_bwd(s["qkv"], s["bg"], s["s_all"], s["tm_all"], do)
    dqkv, g["dn_conv_w"] = _dn_prep_bwd(s["proj"], w["dn_conv_w"], dq, dk, dvv)
    dba, dal, ddt = _ba_bwd(s["proj"], w["alog_row"], w["dtb_row"], dbg, oba)
    g["dn_a_log"] = dal[0, H:2 * H]
    g["dn_dt_bias"] = ddt[0, H:2 * H]
    dproj = jnp.concatenate([dqkv, dz, du, dv, dga, dgb, dba], axis=1)
    dh = _mm(dproj, w["w_in"], "nt", F32, name="mm_dh")
    g["w_in"] = _mm(s["h"], dproj, "tn", BF16, name="mm_dw_in")
    dx, g["norm1_g"] = _rms_bwd(s["x"], w["norm1_g"], dh, dx1, "rms1_bwd")
    return dx, g


def _local_step(x, tgt, layers, final_g):
    saved = []
    for w in layers:
        x, s = _layer_fwd(x, w)
        saved.append(s)
    dx, dgf, loss = _loss_head(x, final_g, tgt)
    grads = [None] * len(layers)
    for l in reversed(range(len(layers))):
        dx, grads[l] = _layer_bwd(dx, layers[l], saved[l])
    return loss[0, 0], dx, grads, dgf


def _w_in_pad(w):
    c1 = 4 * WD
    return jnp.concatenate([w[:, :c1], w[:, c1 + 2 * H:], w[:, c1:c1 + 2 * H],
                            jnp.zeros((w.shape[0], 128 - 2 * H), w.dtype)], axis=1)


def _w_in_unpad(g):
    c1 = 4 * WD
    n = g.shape[1] - 128
    return jnp.concatenate([g[:, :c1], g[:, n:n + 2 * H], g[:, c1:n]], axis=1)


def _row128(v, off):
    return jnp.zeros((1, 128), F32).at[0, off:off + v.shape[0]].set(v)


def _prep_layer(p):
    return dict(
        norm1_g=p["norm1_g"][None], w_in=_w_in_pad(p["w_in"]),
        alog_row=_row128(p["dn_a_log"], H), dtb_row=_row128(p["dn_dt_bias"], H),
        dn_conv_w=p["dn_conv_w"], dn_onorm_g=p["dn_onorm_g"][None],
        sg_ln_g=p["sg_ln_g"][None], sg_ln_b=p["sg_ln_b"][None], sg_w=p["sg_w"], sg_bt=jnp.transpose(p["sg_b"]),
        w_branch_a=p["w_branch_a"], w_branch_b=p["w_branch_b"], w_out=p["w_out"], norm2_g=p["norm2_g"][None],
        ffn_w_gate=p["ffn_w_gate"], ffn_w_up=p["ffn_w_up"], ffn_conv_w=p["ffn_conv_w"],
        ffn_conv_b=p["ffn_conv_b"][None], ffn_w_down=p["ffn_w_down"])


HBM_SPEC = pl.BlockSpec(memory_space=pltpu.HBM)


def _coords():
    return lax.axis_index("x"), lax.axis_index("y"), lax.axis_index("c")


def _other_chips(x, y):
    return [(1 - x, y), (x, 1 - y), (1 - x, 1 - y)]


def _remote(src, dst, send_sems, recv_sems, k, dev):
    return pltpu.make_async_remote_copy(src_ref=src, dst_ref=dst, send_sem=send_sems.at[k], recv_sem=recv_sems.at[k],
                                        device_id=dev, device_id_type=MESH)


def _ag_big(w):
    N = w.shape[0]
    Nh = N // 2

    def body(w_ref, out_ref, send_sems, recv_sems, local_sem):
        x, y, c = _coords()
        me = 2 * x + y
        sib = (x, y, 1 - c)
        chips = _other_chips(x, y)
        my_half = pl.ds(pl.multiple_of(c * Nh, 16), Nh)
        sib_half = pl.ds(pl.multiple_of((1 - c) * Nh, 16), Nh)
        local = pltpu.make_async_copy(w_ref, out_ref.at[me], local_sem)
        local.start()

        def ici(j, chip, owner):
            return _remote(w_ref.at[my_half], out_ref.at[owner, my_half], send_sems, recv_sems, j, (chip[0], chip[1], c))

        def d2d(j, owner, half):
            return _remote(out_ref.at[owner, half], out_ref.at[owner, half], send_sems, recv_sems, 3 + j, sib)

        sends = [ici(j, chip, me) for j, chip in enumerate(chips)]
        for cp in sends:
            cp.start()
        passed = []
        for j, chip in enumerate(chips):
            owner = 2 * chip[0] + chip[1]
            ici(j, chip, owner).wait_recv()
            cp = d2d(j, owner, my_half)
            cp.start()
            passed.append(cp)
        for j, chip in enumerate(chips):
            d2d(j, 2 * chip[0] + chip[1], sib_half).wait_recv()
        for cp in sends + passed:
            cp.wait_send()
        local.wait()

    return pl.pallas_call(
        body, name="ag_weights", out_shape=jax.ShapeDtypeStruct((N_CHIPS, N, 128), w.dtype),
        in_specs=[HBM_SPEC], out_specs=HBM_SPEC,
        scratch_shapes=[pltpu.SemaphoreType.DMA((6,)), pltpu.SemaphoreType.DMA((6,)), pltpu.SemaphoreType.DMA],
    )(w)


def _rs_pair_exchange(G):
    _, _, Nh, _ = G.shape

    def body(g_ref, b_ref, send_sems, recv_sems):
        x, y, c = _coords()
        cps = [_remote(g_ref.at[i, 1 - c], b_ref.at[i], send_sems, recv_sems, i, (x, y, 1 - c)) for i in range(N_CHIPS)]
        for cp in cps:
            cp.start()
        for cp in cps:
            cp.wait()

    return pl.pallas_call(
        body, name="rs_pair_exchange", out_shape=jax.ShapeDtypeStruct((N_CHIPS, Nh, 128), G.dtype),
        in_specs=[HBM_SPEC], out_specs=HBM_SPEC,
        scratch_shapes=[pltpu.SemaphoreType.DMA((N_CHIPS,)), pltpu.SemaphoreType.DMA((N_CHIPS,))],
    )(G)


def _rs_add_pair(G, B, c):
    _, _, Nh, _ = G.shape
    tr = _tile(Nh, 4096, 16)

    def body(c_ref, g_ref, b_ref, o_ref):
        o_ref[0] = (g_ref[0, 0].astype(F32) + b_ref[0].astype(F32)).astype(o_ref.dtype)

    grid_spec = pltpu.PrefetchScalarGridSpec(
        num_scalar_prefetch=1, grid=(N_CHIPS, Nh // tr),
        in_specs=[pl.BlockSpec((1, 1, tr, 128), lambda i, r, c_ref: (i, c_ref[0], r, 0)),
                  pl.BlockSpec((1, tr, 128), lambda i, r, c_ref: (i, r, 0))],
        out_specs=pl.BlockSpec((1, tr, 128), lambda i, r, c_ref: (i, r, 0)))
    return pl.pallas_call(
        body, name="rs_add_pair", grid_spec=grid_spec, out_shape=jax.ShapeDtypeStruct((N_CHIPS, Nh, 128), G.dtype),
        compiler_params=_cp("parallel", "parallel"),
    )(jnp.reshape(c, (1,)).astype(jnp.int32), G, B)


def _rs_chip_exchange(P):
    _, Nh, _ = P.shape

    def body(p_ref, b_ref, send_sems, recv_sems, local_sem):
        x, y, c = _coords()
        me = 2 * x + y
        chips = _other_chips(x, y)
        local = pltpu.make_async_copy(p_ref.at[me], b_ref.at[me], local_sem)
        local.start()
        sends = [_remote(p_ref.at[2 * chip[0] + chip[1]], b_ref.at[me], send_sems, recv_sems, j, (chip[0], chip[1], c))
                 for j, chip in enumerate(chips)]
        for cp in sends:
            cp.start()
        for j, chip in enumerate(chips):
            owner = 2 * chip[0] + chip[1]
            _remote(p_ref.at[owner], b_ref.at[owner], send_sems, recv_sems, j, (chip[0], chip[1], c)).wait_recv()
        for cp in sends:
            cp.wait_send()
        local.wait()

    return pl.pallas_call(
        body, name="rs_chip_exchange", out_shape=jax.ShapeDtypeStruct(P.shape, P.dtype),
        in_specs=[HBM_SPEC], out_specs=HBM_SPEC,
        scratch_shapes=[pltpu.SemaphoreType.DMA((3,)), pltpu.SemaphoreType.DMA((3,)), pltpu.SemaphoreType.DMA],
    )(P)


def _sum_slots(B, name):
    S, R, _ = B.shape
    tr = _tile(R, 2048, 8)

    def body(b_ref, o_ref):
        acc = b_ref[0].astype(F32)
        for i in range(1, S):
            acc = acc + b_ref[i].astype(F32)
        o_ref[...] = acc

    return pl.pallas_call(
        body, name=name, grid=(R // tr,), in_specs=[pl.BlockSpec((S, tr, 128), lambda r: (0, r, 0))],
        out_specs=pl.BlockSpec((tr, 128), lambda r: (r, 0)), out_shape=jax.ShapeDtypeStruct((R, 128), F32),
        compiler_params=_cp("parallel"),
    )(B)


def _rs_pair_gather(Rc):
    Nh = Rc.shape[0]

    def body(r_ref, o_ref, send_sems, recv_sems, local_sem):
        x, y, c = _coords()
        local = pltpu.make_async_copy(r_ref, o_ref.at[c], local_sem)
        local.start()
        cp = _remote(r_ref, o_ref.at[c], send_sems, recv_sems, 0, (x, y, 1 - c))
        cp.start()
        _remote(r_ref, o_ref.at[1 - c], send_sems, recv_sems, 0, (x, y, 1 - c)).wait_recv()
        cp.wait_send()
        local.wait()

    return pl.pallas_call(
        body, name="rs_pair_gather", out_shape=jax.ShapeDtypeStruct((2, Nh, 128), Rc.dtype),
        in_specs=[HBM_SPEC], out_specs=HBM_SPEC,
        scratch_shapes=[pltpu.SemaphoreType.DMA((1,)), pltpu.SemaphoreType.DMA((1,)), pltpu.SemaphoreType.DMA],
    )(Rc)


def _ag8(v):
    R = v.shape[0]

    def body(v_ref, out_ref, send_sems, recv_sems, local_sem):
        x, y, c = _coords()
        me, sib = (x, y, c), (x, y, 1 - c)
        chips = _other_chips(x, y)

        def slot(p):
            return out_ref.at[4 * p[0] + 2 * p[1] + p[2]]

        def copy(k, block, to, src=None):
            return _remote(slot(block) if src is None else src, slot(block), send_sems, recv_sems, k, to)

        mine = pltpu.make_async_copy(v_ref, slot(me), local_sem)
        mine.start()
        first = [copy(0, me, sib, src=v_ref)]
        first += [copy(1 + j, me, (chip[0], chip[1], c), src=v_ref) for j, chip in enumerate(chips)]
        for cp in first:
            cp.start()
        passed = [copy(4 + j, (chip[0], chip[1], c), sib) for j, chip in enumerate(chips)]
        for j, chip in enumerate(chips):
            copy(1 + j, (chip[0], chip[1], c), me).wait_recv()
            passed[j].start()
        copy(0, sib, me).wait_recv()
        for j, chip in enumerate(chips):
            copy(4 + j, (chip[0], chip[1], 1 - c), me).wait_recv()
        for cp in first + passed:
            cp.wait_send()
        mine.wait()

    return pl.pallas_call(
        body, name="ag8_small", out_shape=jax.ShapeDtypeStruct((8, R, 128), v.dtype),
        in_specs=[pl.BlockSpec(memory_space=pltpu.VMEM)], out_specs=pl.BlockSpec(memory_space=pltpu.VMEM),
        scratch_shapes=[pltpu.SemaphoreType.DMA((7,)), pltpu.SemaphoreType.DMA((7,)), pltpu.SemaphoreType.DMA],
        compiler_params=pltpu.CompilerParams(vmem_limit_bytes=VMEM_LIMIT),
    )(v)


def _adamw(w, g, m, v, name):
    L, R, C = w.shape
    tr = _tile(R, 128, 8)

    def body(w_ref, g_ref, m_ref, v_ref, d_ref, mo_ref, vo_ref):
        gv = g_ref[...]
        m2 = ADAM_B1 * m_ref[...] + (1.0 - ADAM_B1) * gv
        v2 = ADAM_B2 * v_ref[...] + (1.0 - ADAM_B2) * jnp.square(gv)
        m_hat = m2 / (1.0 - ADAM_B1 ** ADAM_STEP)
        v_hat = v2 / (1.0 - ADAM_B2 ** ADAM_STEP)
        d_ref[...] = -ADAM_LR * (m_hat / (jnp.sqrt(v_hat) + ADAM_EPS) + ADAM_WD * w_ref[...])
        mo_ref[...] = m2
        vo_ref[...] = v2

    blk = pl.BlockSpec((1, tr, C), lambda l, r: (l, r, 0))
    return pl.pallas_call(
        body, name=name, grid=(L, R // tr), in_specs=[blk] * 4, out_specs=[blk] * 3,
        out_shape=[jax.ShapeDtypeStruct(w.shape, F32)] * 3, compiler_params=_cp("parallel", "parallel"),
    )(w, g, m, v)


BIG = ("w_in", "w_branch_a", "w_branch_b", "w_out", "ffn_w_gate", "ffn_w_up", "ffn_w_down")
ROW_SHARDED = ("w_out", "ffn_w_down")
SMALL = ("norm1_g", "dn_conv_w", "dn_a_log", "dn_dt_bias", "dn_onorm_g", "sg_ln_g", "sg_ln_b", "sg_w", "sg_b",
         "norm2_g", "ffn_conv_w", "ffn_conv_b", "final_norm_g")
SMALL_SHARDED = ("dn_conv_w", "ffn_conv_w")


def _pack_rows(arrs, mult):
    flat = jnp.concatenate([jnp.reshape(a, (-1,)) for a in arrs])
    n = flat.shape[0]
    rows = -(-n // (128 * mult)) * mult
    return jnp.reshape(jnp.pad(flat, (0, rows * 128 - n)), (rows, 128))


def _unpack(flat2d, shapes):
    flat = jnp.reshape(flat2d, (-1,))
    out, off = [], 0
    for shp in shapes:
        n = math.prod(shp)
        out.append(jnp.reshape(flat[off:off + n], shp))
        off += n
    return out


def _shards_to_full(a, row_sharded):
    if row_sharded:
        a = jnp.moveaxis(a, 0, 1)
        return jnp.reshape(a, (a.shape[0], a.shape[1] * a.shape[2], a.shape[3]))
    a = jnp.moveaxis(a, 0, 2)
    return jnp.reshape(a, (a.shape[0], a.shape[1], a.shape[2] * a.shape[3]))


def _full_to_shards(a, row_sharded):
    L, R, C = a.shape
    if row_sharded:
        return jnp.moveaxis(jnp.reshape(a, (L, N_CHIPS, R // N_CHIPS, C)), 1, 0)
    return jnp.moveaxis(jnp.reshape(a, (L, R, N_CHIPS, C // N_CHIPS)), 2, 0)


def kernel(x, norm1_g, w_in, dn_conv_w, dn_a_log, dn_dt_bias, dn_onorm_g, sg_ln_g, sg_ln_b, sg_w, sg_b, w_branch_a, w_branch_b, w_out, norm2_g, ffn_w_gate, ffn_w_up, ffn_conv_w, ffn_conv_b, ffn_w_down, final_norm_g, loss_target, m_norm1_g, m_w_in, m_dn_conv_w, m_dn_a_log, m_dn_dt_bias, m_dn_onorm_g, m_sg_ln_g, m_sg_ln_b, m_sg_w, m_sg_b, m_w_branch_a, m_w_branch_b, m_w_out, m_norm2_g, m_ffn_w_gate, m_ffn_w_up, m_ffn_conv_w, m_ffn_conv_b, m_ffn_w_down, m_final_norm_g, v_norm1_g, v_w_in, v_dn_conv_w, v_dn_a_log, v_dn_dt_bias, v_dn_onorm_g, v_sg_ln_g, v_sg_ln_b, v_sg_w, v_sg_b, v_w_branch_a, v_w_branch_b, v_w_out, v_norm2_g, v_ffn_w_gate, v_ffn_w_up, v_ffn_conv_w, v_ffn_conv_b, v_ffn_w_down, v_final_norm_g):
    W = dict(norm1_g=norm1_g, w_in=w_in, dn_conv_w=dn_conv_w, dn_a_log=dn_a_log, dn_dt_bias=dn_dt_bias,
             dn_onorm_g=dn_onorm_g, sg_ln_g=sg_ln_g, sg_ln_b=sg_ln_b, sg_w=sg_w, sg_b=sg_b, w_branch_a=w_branch_a,
             w_branch_b=w_branch_b, w_out=w_out, norm2_g=norm2_g, ffn_w_gate=ffn_w_gate, ffn_w_up=ffn_w_up,
             ffn_conv_w=ffn_conv_w, ffn_conv_b=ffn_conv_b, ffn_w_down=ffn_w_down, final_norm_g=final_norm_g)
    M = dict(norm1_g=m_norm1_g, w_in=m_w_in, dn_conv_w=m_dn_conv_w, dn_a_log=m_dn_a_log, dn_dt_bias=m_dn_dt_bias,
             dn_onorm_g=m_dn_onorm_g, sg_ln_g=m_sg_ln_g, sg_ln_b=m_sg_ln_b, sg_w=m_sg_w, sg_b=m_sg_b,
             w_branch_a=m_w_branch_a, w_branch_b=m_w_branch_b, w_out=m_w_out, norm2_g=m_norm2_g,
             ffn_w_gate=m_ffn_w_gate, ffn_w_up=m_ffn_w_up, ffn_conv_w=m_ffn_conv_w, ffn_conv_b=m_ffn_conv_b,
             ffn_w_down=m_ffn_w_down, final_norm_g=m_final_norm_g)
    V = dict(norm1_g=v_norm1_g, w_in=v_w_in, dn_conv_w=v_dn_conv_w, dn_a_log=v_dn_a_log, dn_dt_bias=v_dn_dt_bias,
             dn_onorm_g=v_dn_onorm_g, sg_ln_g=v_sg_ln_g, sg_ln_b=v_sg_ln_b, sg_w=v_sg_w, sg_b=v_sg_b,
             w_branch_a=v_w_branch_a, w_branch_b=v_w_branch_b, w_out=v_w_out, norm2_g=v_norm2_g,
             ffn_w_gate=v_ffn_w_gate, ffn_w_up=v_ffn_w_up, ffn_conv_w=v_ffn_conv_w, ffn_conv_b=v_ffn_conv_b,
             ffn_w_down=v_ffn_w_down, final_norm_g=v_final_norm_g)
    cx, cy, cc = _coords()
    chip = 2 * cx + cy
    L = w_in.shape[0]

    wflat = _pack_rows([W[n].astype(BF16) for n in BIG], 32)
    wall = _ag_big(wflat)
    shards = _unpack_shards(wall, [W[n].shape for n in BIG])
    full = {n: _shards_to_full(shards[i], n in ROW_SHARDED) for i, n in enumerate(BIG)}
    taps = _ag8(_pack_rows([W[n] for n in SMALL_SHARDED], 8))
    tap_shards = [_unpack(taps[2 * i], [W[n].shape for n in SMALL_SHARDED]) for i in range(N_CHIPS)]
    for k, n in enumerate(SMALL_SHARDED):
        full[n] = jnp.concatenate([tap_shards[i][k] for i in range(N_CHIPS)], axis=-1)

    layers = []
    for l in range(L):
        p = {n: (full[n][l] if n in full else W[n][l]) for n in W if n != "final_norm_g"}
        layers.append(_prep_layer(p))
    loss, dx, grads, dgf = _local_step(x[0], loss_target[0], layers, final_norm_g[None])

    gbig = []
    for n in BIG:
        gl = [(_w_in_unpad(g[n]) if n == "w_in" else g[n]) for g in grads]
        gbig.append(_full_to_shards(jnp.stack(gl), n in ROW_SHARDED))
    G = jnp.concatenate([jnp.reshape(a, (N_CHIPS, -1)) for a in gbig], axis=1)
    n_el = G.shape[1]
    n_rows = -(-n_el // (128 * 32)) * 32
    G = jnp.reshape(jnp.pad(G, ((0, 0), (0, n_rows * 128 - n_el))), (N_CHIPS, 2, n_rows // 2, 128))
    B1 = _rs_pair_exchange(G)
    P = _rs_add_pair(G, B1, cc)
    B2 = _rs_chip_exchange(P)
    Rc = _sum_slots(B2, "rs_sum_chips")
    Rfull = _rs_pair_gather(Rc)
    g_big = dict(zip(BIG, _unpack(Rfull, [W[n].shape for n in BIG])))

    small = {n: jnp.stack([g[n] for g in grads]) for n in SMALL if n != "final_norm_g"}
    small["final_norm_g"] = dgf
    shapes = [(L,) + tuple(full[n].shape[1:]) if n in SMALL_SHARDED else W[n].shape for n in SMALL] + [(1,)]
    sflat = _pack_rows([small[n] for n in SMALL] + [jnp.reshape(loss, (1,))], 8)
    sred = _unpack(_sum_slots(_ag8(sflat), "sum_small"), shapes)
    g_small = dict(zip(SMALL, sred[:-1]))
    loss_total = sred[-1][0]
    for n in SMALL_SHARDED:
        cs = W[n].shape[-1]
        g_small[n] = lax.dynamic_slice_in_dim(g_small[n], chip * cs, cs, axis=-1)

    delta, new_m, new_v = {}, {}, {}
    for n in BIG:
        delta[n], new_m[n], new_v[n] = _adamw(W[n], g_big[n], M[n], V[n], "adamw_" + n)
    s_shapes = [W[n].shape for n in SMALL]
    packed = [_pack_rows([d[n] for n in SMALL], 8) for d in (W, g_small, M, V)]
    outs = _adamw(*[a[None] for a in packed], "adamw_small")
    for d, o in zip((delta, new_m, new_v), outs):
        d.update(zip(SMALL, _unpack(o[0], s_shapes)))

    names = list(W)
    grad_w = {**g_big, **g_small}
    return (loss_total, dx[None], *[grad_w[n] for n in names], *[delta[n] for n in names],
            *[new_m[n] for n in names], *[new_v[n] for n in names])


def _unpack_shards(wall, shapes):
    flat = jnp.reshape(wall, (N_CHIPS, -1))
    out, off = [], 0
    for shp in shapes:
        n = math.prod(shp)
        out.append(jnp.reshape(flat[:, off:off + n], (N_CHIPS,) + tuple(shp)))
        off += n
    return out
```

```python
import functools
import math

import jax
import jax.numpy as jnp
from jax import lax
from jax.experimental import pallas as pl
from jax.experimental.pallas import tpu as pltpu

F32 = jnp.float32
BF16 = jnp.bfloat16
MESH = pl.DeviceIdType.MESH

EPS = 1e-6
H = 8
DH = 128
WD = H * DH
DNC = 64
SGC = 128
DN_K = 4
FF_K = 3
DEPTH = 2
N_CHIPS = 4

ADAM_LR = 0.001
ADAM_B1 = 0.9
ADAM_B2 = 0.999
ADAM_EPS = 1e-08
ADAM_WD = 0.01
ADAM_STEP = 10

VMEM_LIMIT = 56 * 1024 * 1024

NN = (((1,), (0,)), ((), ()))
NT = (((1,), (1,)), ((), ()))
TN = (((0,), (0,)), ((), ()))

OQ, OZ, OU, OV, OGA = 0, 3 * WD, 4 * WD, 5 * WD, 6 * WD


def _cp(*sem):
    return pltpu.CompilerParams(dimension_semantics=sem or None, vmem_limit_bytes=VMEM_LIMIT)


def _tile(dim, pref, unit=128):
    if dim <= pref:
        return dim
    t = (pref // unit) * unit
    while t >= unit:
        if dim % t == 0:
            return t
        t -= unit
    return dim


def _hdot(a, b, dn=NN):
    return lax.dot_general(a, b, dn, precision=lax.Precision.HIGHEST, preferred_element_type=F32)


def _bdot(a, b, dn=NN):
    return lax.dot_general(a.astype(BF16), b.astype(BF16), dn, preferred_element_type=F32)


def _lsum(x):
    return jnp.sum(x, axis=1, keepdims=True)


def _sig(x):
    return jax.nn.sigmoid(x)


def _dsilu(x):
    s = _sig(x)
    return s * (1.0 + x * (1.0 - s))


def _erf(x):
    a = jnp.abs(x)
    t = 1.0 / (1.0 + 0.3275911 * a)
    poly = t * (0.254829592 + t * (-0.284496736 + t * (1.421413741 + t * (-1.453152027 + t * 1.061405429))))
    r = 1.0 - poly * jnp.exp(-a * a)
    return jnp.where(x < 0, -r, r)


def _gelu(x):
    return 0.5 * x * (1.0 + _erf(x * (2.0 ** -0.5)))


def _dgelu(x):
    cdf = 0.5 * (1.0 + _erf(x * (2.0 ** -0.5)))
    pdf = jnp.exp(-0.5 * x * x) * (1.0 / math.sqrt(2.0 * math.pi))
    return cdf + x * pdf


def _shift_down(x, k):
    if k == 0:
        return x
    rows = lax.broadcasted_iota(jnp.int32, x.shape, 0)
    return jnp.where(rows >= k, pltpu.roll(x, k, 0), 0.0)


def _shift_up(x, k):
    if k == 0:
        return x
    n = x.shape[0]
    rows = lax.broadcasted_iota(jnp.int32, x.shape, 0)
    return jnp.where(rows < n - k, pltpu.roll(x, n - k, 0), 0.0)


def _mm(a, b, mode, out_dtype, add=None, name="mm"):
    if mode == "tn":
        K, M = a.shape
    else:
        M, K = a.shape
    N = b.shape[0] if mode == "nt" else b.shape[1]
    tm, tn, tk = _tile(M, 1024), _tile(N, 1536), _tile(K, 512)
    nk = K // tk
    dn = {"nn": NN, "nt": NT, "tn": TN}[mode]

    def body(a_ref, b_ref, *rest):
        if add is None:
            o_ref, acc_ref = rest
        else:
            add_ref, o_ref, acc_ref = rest
        k = pl.program_id(2)

        @pl.when(k == 0)
        def _():
            acc_ref[...] = jnp.zeros_like(acc_ref)

        acc_ref[...] += lax.dot_general(a_ref[...], b_ref[...], dn, preferred_element_type=F32)

        @pl.when(k == nk - 1)
        def _():
            r = acc_ref[...]
            if add is not None:
                r = r + add_ref[...]
            o_ref[...] = r.astype(o_ref.dtype)

    a_spec = (pl.BlockSpec((tk, tm), lambda i, j, k: (k, i)) if mode == "tn"
              else pl.BlockSpec((tm, tk), lambda i, j, k: (i, k)))
    b_spec = (pl.BlockSpec((tn, tk), lambda i, j, k: (j, k)) if mode == "nt"
              else pl.BlockSpec((tk, tn), lambda i, j, k: (k, j)))
    o_spec = pl.BlockSpec((tm, tn), lambda i, j, k: (i, j))
    in_specs = [a_spec, b_spec] + ([o_spec] if add is not None else [])
    args = (a, b) + ((add,) if add is not None else ())
    return pl.pallas_call(
        body, name=name, grid=(M // tm, N // tn, nk), in_specs=in_specs, out_specs=o_spec,
        out_shape=jax.ShapeDtypeStruct((M, N), out_dtype),
        scratch_shapes=[pltpu.VMEM((tm, tn), F32)],
        compiler_params=_cp("parallel", "parallel", "arbitrary"),
    )(*args)


def _rms_fwd(x, g, name):
    T, D = x.shape
    tt = _tile(T, 256, 16)

    def body(x_ref, g_ref, o_ref):
        xv = x_ref[...]
        r = lax.rsqrt(jnp.mean(xv * xv, axis=-1, keepdims=True) + EPS)
        o_ref[...] = (xv * r * g_ref[...]).astype(o_ref.dtype)

    return pl.pallas_call(
        body, name=name, grid=(T // tt,),
        in_specs=[pl.BlockSpec((tt, D), lambda i: (i, 0)), pl.BlockSpec((1, D), lambda i: (0, 0))],
        out_specs=pl.BlockSpec((tt, D), lambda i: (i, 0)),
        out_shape=jax.ShapeDtypeStruct((T, D), BF16), compiler_params=_cp("parallel"),
    )(x, g)


def _rms_bwd(x, g, dh, dres, name):
    T, D = x.shape
    tt = _tile(T, 256, 16)

    def body(x_ref, g_ref, dh_ref, dres_ref, dx_ref, dg_ref):
        @pl.when(pl.program_id(0) == 0)
        def _():
            dg_ref[...] = jnp.zeros_like(dg_ref)

        xv = x_ref[...]
        r = lax.rsqrt(jnp.mean(xv * xv, axis=-1, keepdims=True) + EPS)
        xh = xv * r
        dh_v = dh_ref[...]
        dy = dh_v * g_ref[...]
        dx_ref[...] = dres_ref[...] + r * (dy - xh * jnp.mean(dy * xh, axis=-1, keepdims=True))
        dg_ref[...] += jnp.sum(dh_v * xh, axis=0, keepdims=True)

    row = pl.BlockSpec((tt, D), lambda i: (i, 0))
    vec = pl.BlockSpec((1, D), lambda i: (0, 0))
    return pl.pallas_call(
        body, name=name, grid=(T // tt,), in_specs=[row, vec, row, row], out_specs=[row, vec],
        out_shape=[jax.ShapeDtypeStruct((T, D), F32), jax.ShapeDtypeStruct((1, D), F32)],
        compiler_params=_cp("arbitrary"),
    )(x, g, dh, dres)


def _loss_head(x, g, tgt, name="loss_head"):
    T, D = x.shape
    tt = _tile(T, 256, 16)

    def body(x_ref, g_ref, t_ref, dx_ref, dg_ref, loss_ref):
        @pl.when(pl.program_id(0) == 0)
        def _():
            dg_ref[...] = jnp.zeros_like(dg_ref)
            loss_ref[...] = jnp.zeros_like(loss_ref)

        xv = x_ref[...]
        r = lax.rsqrt(jnp.mean(xv * xv, axis=-1, keepdims=True) + EPS)
        xh = xv * r
        err = xh * g_ref[...] - t_ref[...]
        part = 0.5 * jnp.sum(jnp.mean(err * err, axis=-1, keepdims=True), axis=0, keepdims=True)
        loss_ref[...] += jnp.broadcast_to(part, loss_ref.shape)
        dy = err * (1.0 / D)
        dg_ref[...] += jnp.sum(dy * xh, axis=0, keepdims=True)
        dyh = dy * g_ref[...]
        dx_ref[...] = r * (dyh - xh * jnp.mean(dyh * xh, axis=-1, keepdims=True))

    row = pl.BlockSpec((tt, D), lambda i: (i, 0))
    vec = pl.BlockSpec((1, D), lambda i: (0, 0))
    return pl.pallas_call(
        body, name=name, grid=(T // tt,), in_specs=[row, vec, row],
        out_specs=[row, vec, pl.BlockSpec((1, 128), lambda i: (0, 0))],
        out_shape=[jax.ShapeDtypeStruct((T, D), F32), jax.ShapeDtypeStruct((1, D), F32),
                   jax.ShapeDtypeStruct((1, 128), F32)],
        compiler_params=_cp("arbitrary"),
    )(x, g, tgt)


def _ba_fwd(proj, alog, dtb, oba, name="dn_ba_fwd"):
    T = proj.shape[0]
    tt = _tile(T, 512, 8)

    def body(p_ref, al_ref, dt_ref, o_ref):
        raw = p_ref[...]
        lane = lax.broadcasted_iota(jnp.int32, raw.shape, 1)
        z = raw + dt_ref[...]
        sp = jnp.maximum(z, 0.0) + jnp.log(1.0 + jnp.exp(-jnp.abs(z)))
        gl = -jnp.exp(al_ref[...]) * sp
        o_ref[...] = jnp.where(lane < H, _sig(raw), jnp.where(lane < 2 * H, gl, 0.0))

    vec = pl.BlockSpec((1, 128), lambda i: (0, 0))
    return pl.pallas_call(
        body, name=name, grid=(T // tt,),
        in_specs=[pl.BlockSpec((tt, 128), lambda i: (i, oba // 128)), vec, vec],
        out_specs=pl.BlockSpec((tt, 128), lambda i: (i, 0)),
        out_shape=jax.ShapeDtypeStruct((T, 128), F32), compiler_params=_cp("parallel"),
    )(proj, alog, dtb)


def _ba_bwd(proj, alog, dtb, dbg, oba, name="dn_ba_bwd"):
    T = proj.shape[0]
    tt = _tile(T, 512, 16)

    def body(p_ref, al_ref, dt_ref, d_ref, o_ref, dal_ref, ddt_ref):
        @pl.when(pl.program_id(0) == 0)
        def _():
            dal_ref[...] = jnp.zeros_like(dal_ref)
            ddt_ref[...] = jnp.zeros_like(ddt_ref)

        raw = p_ref[...]
        d = d_ref[0]
        for hh in range(1, H):
            d = d + d_ref[hh]
        lane = lax.broadcasted_iota(jnp.int32, raw.shape, 1)
        z = raw + dt_ref[...]
        sp = jnp.maximum(z, 0.0) + jnp.log(1.0 + jnp.exp(-jnp.abs(z)))
        na = -jnp.exp(al_ref[...])
        is_g = jnp.logical_and(lane >= H, lane < 2 * H)
        b = _sig(raw)
        dz = jnp.where(is_g, d * na * _sig(z), 0.0)
        o_ref[...] = jnp.where(lane < H, d * b * (1.0 - b), dz).astype(o_ref.dtype)
        dal_ref[...] += jnp.sum(jnp.where(is_g, d * na * sp, 0.0), axis=0, keepdims=True)
        ddt_ref[...] += jnp.sum(dz, axis=0, keepdims=True)

    vec = pl.BlockSpec((1, 128), lambda i: (0, 0))
    return pl.pallas_call(
        body, name=name, grid=(T // tt,),
        in_specs=[pl.BlockSpec((tt, 128), lambda i: (i, oba // 128)), vec, vec,
                  pl.BlockSpec((H, tt, 128), lambda i: (0, i, 0))],
        out_specs=[pl.BlockSpec((tt, 128), lambda i: (i, 0)), vec, vec],
        out_shape=[jax.ShapeDtypeStruct((T, 128), BF16), jax.ShapeDtypeStruct((1, 128), F32),
                   jax.ShapeDtypeStruct((1, 128), F32)],
        compiler_params=_cp("arbitrary"),
    )(proj, alog, dtb, dbg)


def _dn_prep_fwd(proj, convw, name="dn_prep_fwd"):
    T = proj.shape[0]
    nblk = 3 * H

    def body(p_ref, w_ref, o_ref):
        j = pl.program_id(0)
        xv = p_ref[...]
        w = w_ref[...]
        c = xv * w[DN_K - 1:DN_K, :]
        for k in range(1, DN_K):
            c = c + _shift_down(xv, k) * w[DN_K - 1 - k:DN_K - k, :]
        s = c * _sig(c)
        r = lax.rsqrt(_lsum(s * s) + EPS)
        o_ref[...] = jnp.where(j < 2 * H, s * r, s)

    return pl.pallas_call(
        body, name=name, grid=(nblk,),
        in_specs=[pl.BlockSpec((T, DH), lambda j: (0, j)), pl.BlockSpec((DN_K, DH), lambda j: (0, j))],
        out_specs=pl.BlockSpec((T, DH), lambda j: (0, j)),
        out_shape=jax.ShapeDtypeStruct((T, 3 * WD), F32), compiler_params=_cp("parallel"),
    )(proj, convw)


def _dn_prep_bwd(proj, convw, dq, dk, dv, name="dn_prep_bwd"):
    T = proj.shape[0]
    nblk = 3 * H

    def body(p_ref, w_ref, dq_ref, dk_ref, dv_ref, dx_ref, dw_ref):
        j = pl.program_id(0)
        xv = p_ref[...]
        w = w_ref[...]
        shifted = [_shift_down(xv, k) for k in range(DN_K)]
        c = shifted[0] * w[DN_K - 1:DN_K, :]
        for k in range(1, DN_K):
            c = c + shifted[k] * w[DN_K - 1 - k:DN_K - k, :]
        s = c * _sig(c)
        r = lax.rsqrt(_lsum(s * s) + EPS)
        y = s * r
        dy = jnp.where(j < H, dq_ref[...], jnp.where(j < 2 * H, dk_ref[...], dv_ref[...]))
        ds = jnp.where(j < 2 * H, r * (dy - y * _lsum(dy * y)), dy)
        dc = ds * _dsilu(c)
        dx = dc * w[DN_K - 1:DN_K, :]
        for k in range(1, DN_K):
            dx = dx + _shift_up(dc, k) * w[DN_K - 1 - k:DN_K - k, :]
        dx_ref[...] = dx.astype(dx_ref.dtype)
        rows = [jnp.sum(dc * shifted[DN_K - 1 - t], axis=0, keepdims=True) for t in range(DN_K)]
        dw_ref[...] = jnp.concatenate(rows, axis=0)

    hb = lambda off: pl.BlockSpec((T, DH), lambda j: (0, jnp.maximum(jnp.minimum(j - off, H - 1), 0)))
    return pl.pallas_call(
        body, name=name, grid=(nblk,),
        in_specs=[pl.BlockSpec((T, DH), lambda j: (0, j)), pl.BlockSpec((DN_K, DH), lambda j: (0, j)),
                  hb(0), hb(H), hb(2 * H)],
        out_specs=[pl.BlockSpec((T, DH), lambda j: (0, j)), pl.BlockSpec((DN_K, DH), lambda j: (0, j))],
        out_shape=[jax.ShapeDtypeStruct((T, 3 * WD), BF16), jax.ShapeDtypeStruct((DN_K, 3 * WD), F32)],
        compiler_params=_cp("parallel"),
    )(proj, convw, dq, dk, dv)


def _tri_inv(A):
    ri = lax.broadcasted_iota(jnp.int32, A.shape, 0)
    ci = lax.broadcasted_iota(jnp.int32, A.shape, 1)
    X = -A
    P = jnp.where(ri == ci, 1.0, 0.0) + X
    Y = X
    for _ in range(int(math.log2(DNC)) - 1):
        Y = _hdot(Y, Y)
        P = P + _hdot(P, Y)
    return P


def _dn_chunk(q_ref, k_ref, v_ref, bg_ref, r0, h):
    ri = lax.broadcasted_iota(jnp.int32, (DNC, DNC), 0)
    ci = lax.broadcasted_iota(jnp.int32, (DNC, DNC), 1)
    lane = lax.broadcasted_iota(jnp.int32, (DNC, DH), 1)
    row = lax.broadcasted_iota(jnp.int32, (DNC, DH), 0)
    causal = ri >= ci
    strict = ri > ci
    q = q_ref[pl.ds(r0, DNC), :] * (DH ** -0.5)
    k = k_ref[pl.ds(r0, DNC), :]
    v = v_ref[pl.ds(r0, DNC), :]
    bg = bg_ref[pl.ds(r0, DNC), :]
    beta = _lsum(jnp.where(lane == h, bg, 0.0))
    g = _lsum(jnp.where(lane == h + H, bg, 0.0))
    tril1 = jnp.where(causal, 1.0, 0.0)
    gc = _hdot(tril1, jnp.broadcast_to(g, (DNC, DH)))
    gcol = _hdot(tril1, jnp.broadcast_to(g, (DNC, DNC)))
    grow = _hdot(jnp.ones((DNC, DNC), F32), jnp.where(ri == ci, gcol, 0.0))
    dec = jnp.where(causal, jnp.exp(jnp.where(causal, gcol - grow, 0.0)), 0.0)
    eg = jnp.exp(gc)
    gl = jnp.sum(jnp.where(row == DNC - 1, gc, 0.0), axis=0, keepdims=True)
    ek = jnp.exp(gl - gc)
    egl = jnp.exp(gl)
    kb = k * beta
    vb = v * beta
    kbe = kb * eg
    A = jnp.where(strict, _bdot(kb, k, NT) * dec, 0.0)
    P = jnp.where(causal, _bdot(q, k, NT) * dec, 0.0)
    return dict(q=q, k=k, v=v, beta=beta, dec=dec, eg=eg, ek=ek, egl=egl, kb=kb, vb=vb, kbe=kbe, A=A, P=P,
                qd=q * eg, kd=k * ek, causal=causal, strict=strict, lane=lane, row=row)


def _dn_core_fwd(qkv, bg, name="dn_core_fwd"):
    T = qkv.shape[0]
    n_chunks = T // DNC

    def body(q_ref, k_ref, v_ref, bg_ref, o_ref, s_ref, tm_ref, S_scr):
        h = pl.program_id(0)
        S_scr[...] = jnp.zeros_like(S_scr)

        def chunk(n, carry):
            r0 = pl.multiple_of(n * DNC, DNC)
            c = _dn_chunk(q_ref, k_ref, v_ref, bg_ref, r0, h)
            Tm = _tri_inv(c["A"])
            tm_ref[0, n] = Tm
            S = S_scr[...]
            s_ref[0, n] = S
            u = _bdot(Tm, c["vb"])
            w = _bdot(Tm, c["kbe"])
            vn = u - _bdot(w, S)
            o_ref[pl.ds(r0, DNC), :] = _bdot(c["qd"], S) + _bdot(c["P"], vn)
            S_scr[...] = S * c["egl"] + _bdot(c["kd"], vn, TN)
            return carry

        lax.fori_loop(0, n_chunks, chunk, 0)

    col = lambda off: pl.BlockSpec((T, DH), lambda h: (0, h + off))
    return pl.pallas_call(
        body, name=name, grid=(H,),
        in_specs=[col(0), col(H), col(2 * H), pl.BlockSpec((T, 128), lambda h: (0, 0))],
        out_specs=[pl.BlockSpec((T, DH), lambda h: (0, h)),
                   pl.BlockSpec((1, n_chunks, DH, DH), lambda h: (h, 0, 0, 0)),
                   pl.BlockSpec((1, n_chunks, DNC, DNC), lambda h: (h, 0, 0, 0))],
        out_shape=[jax.ShapeDtypeStruct((T, WD), F32), jax.ShapeDtypeStruct((H, n_chunks, DH, DH), F32),
                   jax.ShapeDtypeStruct((H, n_chunks, DNC, DNC), F32)],
        scratch_shapes=[pltpu.VMEM((DH, DH), F32)],
        compiler_params=_cp("parallel"),
    )(qkv, qkv, qkv, bg)


def _dn_core_bwd(qkv, bg, s_all, tm_all, do, name="dn_core_bwd"):
    T = qkv.shape[0]
    n_chunks = T // DNC

    def body(q_ref, k_ref, v_ref, bg_ref, s_ref, tm_ref, do_ref, dq_ref, dk_ref, dv_ref, dbg_ref, dS_scr):
        h = pl.program_id(0)
        dS_scr[...] = jnp.zeros_like(dS_scr)

        def chunk(i, carry):
            n = n_chunks - 1 - i
            r0 = pl.multiple_of(n * DNC, DNC)
            c = _dn_chunk(q_ref, k_ref, v_ref, bg_ref, r0, h)
            q, k, v, beta = c["q"], c["k"], c["v"], c["beta"]
            dec, eg, ek, egl = c["dec"], c["eg"], c["ek"], c["egl"]
            kb, vb, kbe, A, P, qd, kd = c["kb"], c["vb"], c["kbe"], c["A"], c["P"], c["qd"], c["kd"]
            S = s_ref[0, n]
            Tm = tm_ref[0, n]
            u = _bdot(Tm, vb)
            w = _bdot(Tm, kbe)
            vn = u - _bdot(w, S)
            d_o = do_ref[pl.ds(r0, DNC), :]
            dS1 = dS_scr[...]
            d_qd = _bdot(d_o, S, NT)
            dP = jnp.where(c["causal"], _bdot(d_o, vn, NT), 0.0)
            d_vn = _bdot(P, d_o, TN) + _bdot(kd, dS1)
            d_kd = _bdot(vn, dS1, NT)
            d_egl = jnp.sum(jnp.sum(dS1 * S, axis=1, keepdims=True), axis=0, keepdims=True)
            dS_scr[...] = dS1 * egl + _bdot(qd, d_o, TN) - _bdot(w, d_vn, TN)
            d_w = -_bdot(d_vn, S, NT)
            d_vb = _bdot(Tm, d_vn, TN)
            d_kbe = _bdot(Tm, d_w, TN)
            dA = jnp.where(c["strict"], -(_bdot(d_vb, u, NT) + _bdot(d_kbe, w, NT)), 0.0)
            dMA = dA * dec
            dMP = dP * dec
            d_kb = _bdot(dMA, k) + d_kbe * eg
            d_k = _bdot(dMA, kb, TN) + _bdot(dMP, q, TN) + d_kd * ek + d_kb * beta
            d_qs = _bdot(dMP, k) + d_qd * eg
            E = dA * A + dP * P
            ones = jnp.ones((DNC, DH), F32)
            t_kd = _lsum(d_kd * kd)
            d_gc = _hdot(E, ones) - _hdot(E, ones, TN) + _lsum(d_qd * qd) + _lsum(d_kbe * kbe) - t_kd
            d_gl = jnp.sum(t_kd, axis=0, keepdims=True) + d_egl * egl
            d_gc = d_gc + jnp.where(c["row"] == DNC - 1, d_gl, 0.0)
            triu1 = jnp.where(c["strict"], 0.0, 1.0)
            d_g = _hdot(triu1, d_gc)
            d_beta = _lsum(d_kb * k) + _lsum(d_vb * v)
            dq_ref[pl.ds(r0, DNC), :] = d_qs * (DH ** -0.5)
            dk_ref[pl.ds(r0, DNC), :] = d_k
            dv_ref[pl.ds(r0, DNC), :] = d_vb * beta
            lane = c["lane"]
            dbg_ref[0, pl.ds(r0, DNC), :] = (jnp.where(lane == h, d_beta, 0.0) + jnp.where(lane == h + H, d_g, 0.0))
            return carry

        lax.fori_loop(0, n_chunks, chunk, 0)

    col = lambda off: pl.BlockSpec((T, DH), lambda h: (0, h + off))
    hcol = pl.BlockSpec((T, DH), lambda h: (0, h))
    return pl.pallas_call(
        body, name=name, grid=(H,),
        in_specs=[col(0), col(H), col(2 * H), pl.BlockSpec((T, 128), lambda h: (0, 0)),
                  pl.BlockSpec((1, n_chunks, DH, DH), lambda h: (h, 0, 0, 0)),
                  pl.BlockSpec((1, n_chunks, DNC, DNC), lambda h: (h, 0, 0, 0)), hcol],
        out_specs=[hcol, hcol, hcol, pl.BlockSpec((1, T, 128), lambda h: (h, 0, 0))],
        out_shape=[jax.ShapeDtypeStruct((T, WD), F32)] * 3 + [jax.ShapeDtypeStruct((H, T, 128), F32)],
        scratch_shapes=[pltpu.VMEM((DH, DH), F32)],
        compiler_params=_cp("parallel"),
    )(qkv, qkv, qkv, bg, s_all, tm_all, do)


def _dn_post_fwd(o, proj, gon, name="dn_post_fwd"):
    T = o.shape[0]
    tt = _tile(T, 256, 16)

    def body(o_ref, z_ref, g_ref, y_ref):
        for hh in range(H):
            sl = slice(hh * DH, (hh + 1) * DH)
            ov = o_ref[:, sl]
            zv = z_ref[:, sl]
            r = lax.rsqrt(jnp.mean(ov * ov, axis=-1, keepdims=True) + EPS)
            y_ref[:, sl] = (ov * r * g_ref[...] * (zv * _sig(zv))).astype(y_ref.dtype)

    return pl.pallas_call(
        body, name=name, grid=(T // tt,),
        in_specs=[pl.BlockSpec((tt, WD), lambda i: (i, 0)), pl.BlockSpec((tt, WD), lambda i: (i, OZ // WD)),
                  pl.BlockSpec((1, DH), lambda i: (0, 0))],
        out_specs=pl.BlockSpec((tt, WD), lambda i: (i, 0)),
        out_shape=jax.ShapeDtypeStruct((T, WD), BF16), compiler_params=_cp("parallel"),
    )(o, proj, gon)


def _dn_post_bwd(o, proj, gon, dy, name="dn_post_bwd"):
    T = o.shape[0]
    tt = _tile(T, 256, 16)

    def body(o_ref, z_ref, g_ref, dy_ref, do_ref, dz_ref, dg_ref):
        @pl.when(pl.program_id(0) == 0)
        def _():
            dg_ref[...] = jnp.zeros_like(dg_ref)

        acc = jnp.zeros((1, DH), F32)
        for hh in range(H):
            sl = slice(hh * DH, (hh + 1) * DH)
            ov = o_ref[:, sl]
            zv = z_ref[:, sl]
            dyv = dy_ref[:, sl]
            r = lax.rsqrt(jnp.mean(ov * ov, axis=-1, keepdims=True) + EPS)
            oh = ov * r
            nrm = oh * g_ref[...]
            dn = dyv * (zv * _sig(zv))
            dz_ref[:, sl] = (dyv * nrm * _dsilu(zv)).astype(dz_ref.dtype)
            doh = dn * g_ref[...]
            do_ref[:, sl] = r * (doh - oh * jnp.mean(doh * oh, axis=-1, keepdims=True))
            acc = acc + jnp.sum(dn * oh, axis=0, keepdims=True)
        dg_ref[...] += acc

    row = pl.BlockSpec((tt, WD), lambda i: (i, 0))
    vec = pl.BlockSpec((1, DH), lambda i: (0, 0))
    return pl.pallas_call(
        body, name=name, grid=(T // tt,),
        in_specs=[row, pl.BlockSpec((tt, WD), lambda i: (i, OZ // WD)), vec, row],
        out_specs=[row, row, vec],
        out_shape=[jax.ShapeDtypeStruct((T, WD), F32), jax.ShapeDtypeStruct((T, WD), BF16),
                   jax.ShapeDtypeStruct((1, DH), F32)],
        compiler_params=_cp("arbitrary"),
    )(o, proj, gon, dy)


def _sg_common(u_ref, v_ref, lng_ref, lnb_ref):
    ur = u_ref[...]
    vr = v_ref[...]
    vgel = _gelu(vr)
    mu = jnp.mean(vgel, axis=-1, keepdims=True)
    xc = vgel - mu
    rs = lax.rsqrt(jnp.mean(xc * xc, axis=-1, keepdims=True) + EPS)
    xh = xc * rs
    vg = xh * lng_ref[...] + lnb_ref[...]
    return ur, vr, rs, xh, vg


def _sg_fwd(proj, lng, lnb, sgw, sgbt, name="sg_fwd"):
    T = proj.shape[0]

    def body(u_ref, v_ref, lng_ref, lnb_ref, w_ref, bt_ref, y_ref):
        ur, _, _, _, vg = _sg_common(u_ref, v_ref, lng_ref, lnb_ref)
        ri = lax.broadcasted_iota(jnp.int32, (SGC, SGC), 0)
        ci = lax.broadcasted_iota(jnp.int32, (SGC, SGC), 1)
        ug = _gelu(ur)
        for g in range(H):
            sl = slice(g * DH, (g + 1) * DH)
            ws = jnp.where(ri >= ci, w_ref[g], 0.0)
            mixed = _bdot(ws, vg[:, sl]) + bt_ref[:, g:g + 1]
            y_ref[:, sl] = (ug[:, sl] * mixed).astype(y_ref.dtype)

    vec = pl.BlockSpec((1, WD), lambda i: (0, 0))
    return pl.pallas_call(
        body, name=name, grid=(T // SGC,),
        in_specs=[pl.BlockSpec((SGC, WD), lambda i: (i, OU // WD)), pl.BlockSpec((SGC, WD), lambda i: (i, OV // WD)),
                  vec, vec, pl.BlockSpec((H, SGC, SGC), lambda i: (0, 0, 0)),
                  pl.BlockSpec((SGC, H), lambda i: (0, 0))],
        out_specs=pl.BlockSpec((SGC, WD), lambda i: (i, 0)),
        out_shape=jax.ShapeDtypeStruct((T, WD), BF16), compiler_params=_cp("parallel"),
    )(proj, proj, lng, lnb, sgw, sgbt)


def _sg_bwd(proj, lng, lnb, sgw, sgbt, dy, name="sg_bwd"):
    T = proj.shape[0]

    def body(u_ref, v_ref, lng_ref, lnb_ref, w_ref, bt_ref, dy_ref,
             du_ref, dv_ref, dw_ref, dbt_ref, dlng_ref, dlnb_ref):
        @pl.when(pl.program_id(0) == 0)
        def _():
            dw_ref[...] = jnp.zeros_like(dw_ref)
            dbt_ref[...] = jnp.zeros_like(dbt_ref)
            dlng_ref[...] = jnp.zeros_like(dlng_ref)
            dlnb_ref[...] = jnp.zeros_like(dlnb_ref)

        ur, vr, rs, xh, vg = _sg_common(u_ref, v_ref, lng_ref, lnb_ref)
        ri = lax.broadcasted_iota(jnp.int32, (SGC, SGC), 0)
        ci = lax.broadcasted_iota(jnp.int32, (SGC, SGC), 1)
        ug = _gelu(ur)
        dyv = dy_ref[...]
        dbt = jnp.zeros((SGC, 128), F32)
        dvg_parts = []
        for g in range(H):
            sl = slice(g * DH, (g + 1) * DH)
            ws = jnp.where(ri >= ci, w_ref[g], 0.0)
            mixed = _bdot(ws, vg[:, sl]) + bt_ref[:, g:g + 1]
            dyg = dyv[:, sl]
            du_ref[:, sl] = (dyg * mixed * _dgelu(ur[:, sl])).astype(du_ref.dtype)
            dmix = dyg * ug[:, sl]
            dw_ref[g] += jnp.where(ri >= ci, _bdot(dmix, vg[:, sl], NT), 0.0)
            dbt = dbt + jnp.where(ci == g, _lsum(dmix), 0.0)
            dvg_parts.append(_bdot(ws, dmix, TN))
        dbt_ref[...] += dbt
        dvg = jnp.concatenate(dvg_parts, axis=1)
        dlng_ref[...] += jnp.sum(dvg * xh, axis=0, keepdims=True)
        dlnb_ref[...] += jnp.sum(dvg, axis=0, keepdims=True)
        dxh = dvg * lng_ref[...]
        dvgel = rs * (dxh - jnp.mean(dxh, axis=-1, keepdims=True) - xh * jnp.mean(dxh * xh, axis=-1, keepdims=True))
        dv_ref[...] = (dvgel * _dgelu(vr)).astype(dv_ref.dtype)

    vec = pl.BlockSpec((1, WD), lambda i: (0, 0))
    row = pl.BlockSpec((SGC, WD), lambda i: (i, 0))
    return pl.pallas_call(
        body, name=name, grid=(T // SGC,),
        in_specs=[pl.BlockSpec((SGC, WD), lambda i: (i, OU // WD)), pl.BlockSpec((SGC, WD), lambda i: (i, OV // WD)),
                  vec, vec, pl.BlockSpec((H, SGC, SGC), lambda i: (0, 0, 0)),
                  pl.BlockSpec((SGC, H), lambda i: (0, 0)), row],
        out_specs=[row, row, pl.BlockSpec((H, SGC, SGC), lambda i: (0, 0, 0)),
                   pl.BlockSpec((SGC, 128), lambda i: (0, 0)), vec, vec],
        out_shape=[jax.ShapeDtypeStruct((T, WD), BF16), jax.ShapeDtypeStruct((T, WD), BF16),
                   jax.ShapeDtypeStruct((H, SGC, SGC), F32), jax.ShapeDtypeStruct((SGC, 128), F32),
                   jax.ShapeDtypeStruct((1, WD), F32), jax.ShapeDtypeStruct((1, WD), F32)],
        compiler_params=_cp("arbitrary"),
    )(proj, proj, lng, lnb, sgw, sgbt, dy)


def _merge_fwd(proj, yap, ybp, D, name="merge_fwd"):
    T = proj.shape[0]
    tt = _tile(T, 256, 16)

    def body(ga_ref, gb_ref, a_ref, b_ref, o_ref):
        o_ref[...] = (_sig(ga_ref[...]) * a_ref[...] + _sig(gb_ref[...]) * b_ref[...]).astype(o_ref.dtype)

    row = pl.BlockSpec((tt, D), lambda i: (i, 0))
    return pl.pallas_call(
        body, name=name, grid=(T // tt,),
        in_specs=[pl.BlockSpec((tt, D), lambda i: (i, OGA // D)), pl.BlockSpec((tt, D), lambda i: (i, OGA // D + 1)),
                  row, row],
        out_specs=row, out_shape=jax.ShapeDtypeStruct((T, D), BF16), compiler_params=_cp("parallel"),
    )(proj, proj, yap, ybp)


def _merge_bwd(proj, yap, ybp, dm, D, name="merge_bwd"):
    T = proj.shape[0]
    tt = _tile(T, 256, 16)

    def body(ga_ref, gb_ref, a_ref, b_ref, dm_ref, da_ref, db_ref, dga_ref, dgb_ref):
        d = dm_ref[...]
        sa = _sig(ga_ref[...])
        sb = _sig(gb_ref[...])
        da_ref[...] = (d * sa).astype(da_ref.dtype)
        db_ref[...] = (d * sb).astype(db_ref.dtype)
        dga_ref[...] = (d * a_ref[...] * sa * (1.0 - sa)).astype(dga_ref.dtype)
        dgb_ref[...] = (d * b_ref[...] * sb * (1.0 - sb)).astype(dgb_ref.dtype)

    row = pl.BlockSpec((tt, D), lambda i: (i, 0))
    return pl.pallas_call(
        body, name=name, grid=(T // tt,),
        in_specs=[pl.BlockSpec((tt, D), lambda i: (i, OGA // D)), pl.BlockSpec((tt, D), lambda i: (i, OGA // D + 1)),
                  row, row, row],
        out_specs=[row] * 4, out_shape=[jax.ShapeDtypeStruct((T, D), BF16)] * 4,
        compiler_params=_cp("parallel"),
    )(proj, proj, yap, ybp, dm)


def _ffn_act_fwd(gp, up, cw, cb, name="ffn_act_fwd"):
    T, F = gp.shape

    def body(g_ref, u_ref, w_ref, b_ref, o_ref):
        gv = g_ref[...]
        w = w_ref[...]
        c = gv * w[FF_K - 1:FF_K, :] + b_ref[...]
        for k in range(1, FF_K):
            c = c + _shift_down(gv, k) * w[FF_K - 1 - k:FF_K - k, :]
        o_ref[...] = (c * _sig(c) * u_ref[...]).astype(o_ref.dtype)

    col = pl.BlockSpec((T, 128), lambda j: (0, j))
    return pl.pallas_call(
        body, name=name, grid=(F // 128,),
        in_specs=[col, col, pl.BlockSpec((FF_K, 128), lambda j: (0, j)), pl.BlockSpec((1, 128), lambda j: (0, j))],
        out_specs=col, out_shape=jax.ShapeDtypeStruct((T, F), BF16), compiler_params=_cp("parallel"),
    )(gp, up, cw, cb)


def _ffn_act_bwd(gp, up, cw, cb, dact, name="ffn_act_bwd"):
    T, F = gp.shape

    def body(g_ref, u_ref, w_ref, b_ref, d_ref, dg_ref, du_ref, dw_ref, db_ref):
        gv = g_ref[...]
        w = w_ref[...]
        shifted = [_shift_down(gv, k) for k in range(FF_K)]
        c = shifted[0] * w[FF_K - 1:FF_K, :] + b_ref[...]
        for k in range(1, FF_K):
            c = c + shifted[k] * w[FF_K - 1 - k:FF_K - k, :]
        d = d_ref[...]
        du_ref[...] = (d * c * _sig(c)).astype(du_ref.dtype)
        dc = d * u_ref[...] * _dsilu(c)
        dg = dc * w[FF_K - 1:FF_K, :]
        for k in range(1, FF_K):
            dg = dg + _shift_up(dc, k) * w[FF_K - 1 - k:FF_K - k, :]
        dg_ref[...] = dg.astype(dg_ref.dtype)
        rows = [jnp.sum(dc * shifted[FF_K - 1 - t], axis=0, keepdims=True) for t in range(FF_K)]
        dw_ref[...] = jnp.concatenate(rows, axis=0)
        db_ref[...] = jnp.sum(dc, axis=0, keepdims=True)

    col = pl.BlockSpec((T, 128), lambda j: (0, j))
    wspec = pl.BlockSpec((FF_K, 128), lambda j: (0, j))
    bspec = pl.BlockSpec((1, 128), lambda j: (0, j))
    return pl.pallas_call(
        body, name=name, grid=(F // 128,),
        in_specs=[col, col, wspec, bspec, col], out_specs=[col, col, wspec, bspec],
        out_shape=[jax.ShapeDtypeStruct((T, F), BF16), jax.ShapeDtypeStruct((T, F), BF16),
                   jax.ShapeDtypeStruct((FF_K, F), F32), jax.ShapeDtypeStruct((1, F), F32)],
        compiler_params=_cp("parallel"),
    )(gp, up, cw, cb, dact)


def _layer_fwd(x, w):
    D = x.shape[1]
    oba = OGA + 2 * D
    h = _rms_fwd(x, w["norm1_g"], "rms1_fwd")
    proj = _mm(h, w["w_in_t"], "nt", F32, name="mm_proj")
    bg = _ba_fwd(proj, w["alog_row"], w["dtb_row"], oba)
    qkv = _dn_prep_fwd(proj, w["dn_conv_w"])
    o, s_all, tm_all = _dn_core_fwd(qkv, bg)
    ya = _dn_post_fwd(o, proj, w["dn_onorm_g"])
    yb = _sg_fwd(proj, w["sg_ln_g"], w["sg_ln_b"], w["sg_w"], w["sg_bt"])
    yap = _mm(ya, w["w_branch_a"], "nn", F32, name="mm_branch")
    ybp = _mm(yb, w["w_branch_b"], "nn", F32, name="mm_branch")
    merged = _merge_fwd(proj, yap, ybp, D)
    x1 = _mm(merged, w["w_out"], "nn", F32, add=x, name="mm_out")
    h2 = _rms_fwd(x1, w["norm2_g"], "rms2_fwd")
    gp = _mm(h2, w["ffn_w_gate"], "nn", F32, name="mm_ffn_in")
    up = _mm(h2, w["ffn_w_up"], "nn", F32, name="mm_ffn_in")
    act = _ffn_act_fwd(gp, up, w["ffn_conv_w"], w["ffn_conv_b"])
    x2 = _mm(act, w["ffn_w_down"], "nn", F32, add=x1, name="mm_ffn_down")
    saved = dict(x=x, h=h, proj=proj, bg=bg, qkv=qkv, o=o, s_all=s_all, tm_all=tm_all, ya=ya, yb=yb, yap=yap,
                 ybp=ybp, merged=merged, x1=x1, h2=h2, gp=gp, up=up, act=act)
    return x2, saved


def _layer_bwd(dx2, w, s):
    D = dx2.shape[1]
    oba = OGA + 2 * D
    g = {}
    dx2b = dx2.astype(BF16)
    dact = _mm(dx2b, w["ffn_w_down"], "nt", F32, name="mm_d_act")
    g["ffn_w_down"] = _mm(s["act"], dx2b, "tn", BF16, name="mm_dw_down")
    dgp, dup, g["ffn_conv_w"], g["ffn_conv_b"] = _ffn_act_bwd(s["gp"], s["up"], w["ffn_conv_w"], w["ffn_conv_b"], dact)
    dh2 = _mm(dgp, w["ffn_w_gate"], "nt", F32, name="mm_dh2")
    dh2 = _mm(dup, w["ffn_w_up"], "nt", F32, add=dh2, name="mm_dh2_acc")
    g["ffn_w_gate"] = _mm(s["h2"], dgp, "tn", BF16, name="mm_dw_ffn_in")
    g["ffn_w_up"] = _mm(s["h2"], dup, "tn", BF16, name="mm_dw_ffn_in")
    dx1, g["norm2_g"] = _rms_bwd(s["x1"], w["norm2_g"], dh2, dx2, "rms2_bwd")
    dx1b = dx1.astype(BF16)
    dm = _mm(dx1b, w["w_out"], "nt", F32, name="mm_d_merged")
    g["w_out"] = _mm(s["merged"], dx1b, "tn", BF16, name="mm_dw_out")
    dyap, dybp, dga, dgb = _merge_bwd(s["proj"], s["yap"], s["ybp"], dm, D)
    dya = _mm(dyap, w["w_branch_a"], "nt", F32, name="mm_d_branch")
    dyb = _mm(dybp, w["w_branch_b"], "nt", F32, name="mm_d_branch")
    g["w_branch_a"] = _mm(s["ya"], dyap, "tn", BF16, name="mm_dw_branch")
    g["w_branch_b"] = _mm(s["yb"], dybp, "tn", BF16, name="mm_dw_branch")
    du, dv, g["sg_w"], dbt, g["sg_ln_g"], g["sg_ln_b"] = _sg_bwd(
        s["proj"], w["sg_ln_g"], w["sg_ln_b"], w["sg_w"], w["sg_bt"], dyb)
    g["sg_b"] = jnp.transpose(dbt[:, :H])
    do, dz, g["dn_onorm_g"] = _dn_post_bwd(s["o"], s["proj"], w["dn_onorm_g"], dya)
    dq, dk, dvv, dbg = _dn_core_bwd(s["qkv"], s["bg"], s["s_all"], s["tm_all"], do)
    dqkv, g["dn_conv_w"] = _dn_prep_bwd(s["proj"], w["dn_conv_w"], dq, dk, dvv)
    dba, dal, ddt = _ba_bwd(s["proj"], w["alog_row"], w["dtb_row"], dbg, oba)
    g["dn_a_log"] = dal[0, H:2 * H]
    g["dn_dt_bias"] = ddt[0, H:2 * H]
    dproj = jnp.concatenate([dqkv, dz, du, dv, dga, dgb, dba], axis=1)
    dh = _mm(dproj, w["w_in_t"], "nn", F32, name="mm_dh")
    g["w_in_t"] = _mm(dproj, s["h"], "tn", BF16, name="mm_dw_in")
    dx, g["norm1_g"] = _rms_bwd(s["x"], w["norm1_g"], dh, dx1, "rms1_bwd")
    return dx, g


def _local_step(x, tgt, layers, final_g):
    saved = []
    for w in layers:
        x, s = _layer_fwd(x, w)
        saved.append(s)
    dx, dgf, loss = _loss_head(x, final_g, tgt)
    grads = [None] * len(layers)
    for l in reversed(range(len(layers))):
        dx, grads[l] = _layer_bwd(dx, layers[l], saved[l])
    return loss[0, 0], dx, grads, dgf


def _w_in_pad(wt):
    c1 = 4 * WD
    return jnp.concatenate([wt[:c1], wt[c1 + 2 * H:], wt[c1:c1 + 2 * H],
                            jnp.zeros((128 - 2 * H, wt.shape[1]), wt.dtype)], axis=0)


def _w_in_unpad(gt):
    c1 = 4 * WD
    n = gt.shape[0] - 128
    return jnp.concatenate([gt[:c1], gt[n:n + 2 * H], gt[c1:n]], axis=0)


def _row128(v, off):
    return jnp.pad(v, (off, 128 - off - v.shape[0]))[None]


def _prep_layer(p):
    return dict(
        norm1_g=p["norm1_g"][None], w_in_t=_w_in_pad(p["w_in_t"]),
        alog_row=_row128(p["dn_a_log"], H), dtb_row=_row128(p["dn_dt_bias"], H),
        dn_conv_w=p["dn_conv_w"], dn_onorm_g=p["dn_onorm_g"][None],
        sg_ln_g=p["sg_ln_g"][None], sg_ln_b=p["sg_ln_b"][None], sg_w=p["sg_w"], sg_bt=jnp.transpose(p["sg_b"]),
        w_branch_a=p["w_branch_a"], w_branch_b=p["w_branch_b"], w_out=p["w_out"], norm2_g=p["norm2_g"][None],
        ffn_w_gate=p["ffn_w_gate"], ffn_w_up=p["ffn_w_up"], ffn_conv_w=p["ffn_conv_w"],
        ffn_conv_b=p["ffn_conv_b"][None], ffn_w_down=p["ffn_w_down"])


HBM_SPEC = pl.BlockSpec(memory_space=pltpu.HBM)


def _coords():
    return lax.axis_index("x"), lax.axis_index("y"), lax.axis_index("c")


def _other_chips(x, y):
    return [(1 - x, y), (x, 1 - y), (1 - x, 1 - y)]


def _remote(src, dst, send_sems, recv_sems, k, dev):
    return pltpu.make_async_remote_copy(src_ref=src, dst_ref=dst, send_sem=send_sems.at[k], recv_sem=recv_sems.at[k],
                                        device_id=dev, device_id_type=MESH)


def _ag_layers(ws):
    n = len(ws)

    def body(*refs):
        w_refs, o_refs = refs[:n], refs[n:2 * n]
        send_sems, recv_sems, local_sems = refs[2 * n:]
        x, y, c = _coords()
        me = 2 * x + y
        sib = (x, y, 1 - c)
        chips = _other_chips(x, y)
        local = [pltpu.make_async_copy(w_refs[k], o_refs[k].at[me], local_sems.at[k]) for k in range(n)]
        for cp in local:
            cp.start()

        def ici(k, j, chip, owner):
            return _remote(w_refs[k].at[c], o_refs[k].at[owner, c], send_sems, recv_sems, 6 * k + j,
                           (chip[0], chip[1], c))

        def d2d(k, j, owner, layer):
            return _remote(o_refs[k].at[owner, layer], o_refs[k].at[owner, layer], send_sems, recv_sems,
                           6 * k + 3 + j, sib)

        sends = [ici(k, j, chip, me) for k in range(n) for j, chip in enumerate(chips)]
        for cp in sends:
            cp.start()
        passed = []
        for k in range(n):
            for j, chip in enumerate(chips):
                owner = 2 * chip[0] + chip[1]
                ici(k, j, chip, owner).wait_recv()
                cp = d2d(k, j, owner, c)
                cp.start()
                passed.append(cp)
        for k in range(n):
            for j, chip in enumerate(chips):
                d2d(k, j, 2 * chip[0] + chip[1], 1 - c).wait_recv()
        for cp in sends + passed:
            cp.wait_send()
        for cp in local:
            cp.wait()

    return pl.pallas_call(
        body, name="ag_weights", out_shape=[jax.ShapeDtypeStruct((N_CHIPS,) + w.shape, w.dtype) for w in ws],
        in_specs=[HBM_SPEC] * n, out_specs=[HBM_SPEC] * n,
        scratch_shapes=[pltpu.SemaphoreType.DMA((6 * n,)), pltpu.SemaphoreType.DMA((6 * n,)),
                        pltpu.SemaphoreType.DMA((n,))],
    )(*ws)


def _rs_pair_exchange(Gs):
    n = len(Gs)

    def body(*refs):
        g_refs, b_refs = refs[:n], refs[n:2 * n]
        send_sems, recv_sems = refs[2 * n:]
        x, y, c = _coords()
        cps = [_remote(g_refs[k].at[i, 1 - c], b_refs[k].at[i], send_sems, recv_sems, N_CHIPS * k + i, (x, y, 1 - c))
               for k in range(n) for i in range(N_CHIPS)]
        for cp in cps:
            cp.start()
        for cp in cps:
            cp.wait()

    return pl.pallas_call(
        body, name="rs_pair_exchange",
        out_shape=[jax.ShapeDtypeStruct((N_CHIPS,) + g.shape[2:], g.dtype) for g in Gs],
        in_specs=[HBM_SPEC] * n, out_specs=[HBM_SPEC] * n,
        scratch_shapes=[pltpu.SemaphoreType.DMA((N_CHIPS * n,)), pltpu.SemaphoreType.DMA((N_CHIPS * n,))],
    )(*Gs)


def _rs_add_pair(G, B, c, name):
    _, _, R, C = G.shape
    tr = _tile(R, 256, 16)

    def body(c_ref, g_ref, b_ref, o_ref):
        o_ref[0] = (g_ref[0, 0].astype(F32) + b_ref[0].astype(F32)).astype(o_ref.dtype)

    grid_spec = pltpu.PrefetchScalarGridSpec(
        num_scalar_prefetch=1, grid=(N_CHIPS, R // tr),
        in_specs=[pl.BlockSpec((1, 1, tr, C), lambda i, r, c_ref: (i, c_ref[0], r, 0)),
                  pl.BlockSpec((1, tr, C), lambda i, r, c_ref: (i, r, 0))],
        out_specs=pl.BlockSpec((1, tr, C), lambda i, r, c_ref: (i, r, 0)))
    return pl.pallas_call(
        body, name=name, grid_spec=grid_spec, out_shape=jax.ShapeDtypeStruct((N_CHIPS, R, C), G.dtype),
        compiler_params=_cp("parallel", "parallel"),
    )(jnp.reshape(c, (1,)).astype(jnp.int32), G, B)


def _rs_chip_exchange(Ps):
    n = len(Ps)

    def body(*refs):
        p_refs, b_refs = refs[:n], refs[n:2 * n]
        send_sems, recv_sems, local_sems = refs[2 * n:]
        x, y, c = _coords()
        me = 2 * x + y
        chips = _other_chips(x, y)
        local = [pltpu.make_async_copy(p_refs[k].at[me], b_refs[k].at[me], local_sems.at[k]) for k in range(n)]
        for cp in local:
            cp.start()

        def cp_(k, j, chip, src_slot, dst_slot):
            return _remote(p_refs[k].at[src_slot], b_refs[k].at[dst_slot], send_sems, recv_sems, 3 * k + j,
                           (chip[0], chip[1], c))

        sends = [cp_(k, j, chip, 2 * chip[0] + chip[1], me) for k in range(n) for j, chip in enumerate(chips)]
        for cp in sends:
            cp.start()
        for k in range(n):
            for j, chip in enumerate(chips):
                owner = 2 * chip[0] + chip[1]
                cp_(k, j, chip, owner, owner).wait_recv()
        for cp in sends:
            cp.wait_send()
        for cp in local:
            cp.wait()

    return pl.pallas_call(
        body, name="rs_chip_exchange", out_shape=[jax.ShapeDtypeStruct(p.shape, p.dtype) for p in Ps],
        in_specs=[HBM_SPEC] * n, out_specs=[HBM_SPEC] * n,
        scratch_shapes=[pltpu.SemaphoreType.DMA((3 * n,)), pltpu.SemaphoreType.DMA((3 * n,)),
                        pltpu.SemaphoreType.DMA((n,))],
    )(*Ps)


def _sum_slots(B, name):
    S, R, C = B.shape
    tr = _tile(R, 256, 16)

    def body(b_ref, o_ref):
        acc = b_ref[0].astype(F32)
        for i in range(1, S):
            acc = acc + b_ref[i].astype(F32)
        o_ref[...] = acc

    return pl.pallas_call(
        body, name=name, grid=(R // tr,), in_specs=[pl.BlockSpec((S, tr, C), lambda r: (0, r, 0))],
        out_specs=pl.BlockSpec((tr, C), lambda r: (r, 0)), out_shape=jax.ShapeDtypeStruct((R, C), F32),
        compiler_params=_cp("parallel"),
    )(B)


def _rs_pair_gather(Rs):
    n = len(Rs)

    def body(*refs):
        r_refs, o_refs = refs[:n], refs[n:2 * n]
        send_sems, recv_sems, local_sems = refs[2 * n:]
        x, y, c = _coords()
        sib = (x, y, 1 - c)
        local = [pltpu.make_async_copy(r_refs[k], o_refs[k].at[c], local_sems.at[k]) for k in range(n)]
        sends = [_remote(r_refs[k], o_refs[k].at[c], send_sems, recv_sems, k, sib) for k in range(n)]
        for cp in local + sends:
            cp.start()
        for k in range(n):
            _remote(r_refs[k], o_refs[k].at[1 - c], send_sems, recv_sems, k, sib).wait_recv()
        for cp in sends:
            cp.wait_send()
        for cp in local:
            cp.wait()

    return pl.pallas_call(
        body, name="rs_pair_gather", out_shape=[jax.ShapeDtypeStruct((2,) + r.shape, r.dtype) for r in Rs],
        in_specs=[HBM_SPEC] * n, out_specs=[HBM_SPEC] * n,
        scratch_shapes=[pltpu.SemaphoreType.DMA((n,)), pltpu.SemaphoreType.DMA((n,)), pltpu.SemaphoreType.DMA((n,))],
    )(*Rs)


def _ag8(v):
    R = v.shape[0]

    def body(v_ref, out_ref, send_sems, recv_sems, local_sem):
        x, y, c = _coords()
        me, sib = (x, y, c), (x, y, 1 - c)
        chips = _other_chips(x, y)

        def slot(p):
            return out_ref.at[4 * p[0] + 2 * p[1] + p[2]]

        def copy(k, block, to, src=None):
            return _remote(slot(block) if src is None else src, slot(block), send_sems, recv_sems, k, to)

        mine = pltpu.make_async_copy(v_ref, slot(me), local_sem)
        mine.start()
        first = [copy(0, me, sib, src=v_ref)]
        first += [copy(1 + j, me, (chip[0], chip[1], c), src=v_ref) for j, chip in enumerate(chips)]
        for cp in first:
            cp.start()
        passed = [copy(4 + j, (chip[0], chip[1], c), sib) for j, chip in enumerate(chips)]
        for j, chip in enumerate(chips):
            copy(1 + j, (chip[0], chip[1], c), me).wait_recv()
            passed[j].start()
        copy(0, sib, me).wait_recv()
        for j, chip in enumerate(chips):
            copy(4 + j, (chip[0], chip[1], 1 - c), me).wait_recv()
        for cp in first + passed:
            cp.wait_send()
        mine.wait()

    return pl.pallas_call(
        body, name="ag8_small", out_shape=jax.ShapeDtypeStruct((8, R, 128), v.dtype),
        in_specs=[pl.BlockSpec(memory_space=pltpu.VMEM)], out_specs=pl.BlockSpec(memory_space=pltpu.VMEM),
        scratch_shapes=[pltpu.SemaphoreType.DMA((7,)), pltpu.SemaphoreType.DMA((7,)), pltpu.SemaphoreType.DMA],
        compiler_params=pltpu.CompilerParams(vmem_limit_bytes=VMEM_LIMIT),
    )(v)


def _adamw(w, g, m, v, name):
    L, R, C = w.shape
    tr = _tile(R, 128, 8)

    def body(w_ref, g_ref, m_ref, v_ref, d_ref, mo_ref, vo_ref):
        gv = g_ref[...]
        m2 = ADAM_B1 * m_ref[...] + (1.0 - ADAM_B1) * gv
        v2 = ADAM_B2 * v_ref[...] + (1.0 - ADAM_B2) * jnp.square(gv)
        m_hat = m2 / (1.0 - ADAM_B1 ** ADAM_STEP)
        v_hat = v2 / (1.0 - ADAM_B2 ** ADAM_STEP)
        d_ref[...] = -ADAM_LR * (m_hat / (jnp.sqrt(v_hat) + ADAM_EPS) + ADAM_WD * w_ref[...])
        mo_ref[...] = m2
        vo_ref[...] = v2

    blk = pl.BlockSpec((1, tr, C), lambda l, r: (l, r, 0))
    return pl.pallas_call(
        body, name=name, grid=(L, R // tr), in_specs=[blk] * 4, out_specs=[blk] * 3,
        out_shape=[jax.ShapeDtypeStruct(w.shape, F32)] * 3, compiler_params=_cp("parallel", "parallel"),
    )(w, g, m, v)


BIG = ("w_in", "w_branch_a", "w_branch_b", "w_out", "ffn_w_gate", "ffn_w_up", "ffn_w_down")
ROW_SHARDED = ("w_out", "ffn_w_down")
SMALL = ("norm1_g", "dn_conv_w", "dn_a_log", "dn_dt_bias", "dn_onorm_g", "sg_ln_g", "sg_ln_b", "sg_w", "sg_b",
         "norm2_g", "ffn_conv_w", "ffn_conv_b", "final_norm_g")
SMALL_SHARDED = ("dn_conv_w", "ffn_conv_w")


def _pack_rows(arrs, mult):
    flat = jnp.concatenate([jnp.reshape(a, (-1,)) for a in arrs])
    n = flat.shape[0]
    rows = -(-n // (128 * mult)) * mult
    return jnp.reshape(jnp.pad(flat, (0, rows * 128 - n)), (rows, 128))


def _unpack(flat2d, shapes):
    flat = jnp.reshape(flat2d, (-1,))
    out, off = [], 0
    for shp in shapes:
        n = math.prod(shp)
        out.append(jnp.reshape(flat[off:off + n], shp))
        off += n
    return out


def _shards_to_full(a, row_sharded):
    if row_sharded:
        a = jnp.moveaxis(a, 0, 1)
        return jnp.reshape(a, (a.shape[0], a.shape[1] * a.shape[2], a.shape[3]))
    a = jnp.moveaxis(a, 0, 2)
    return jnp.reshape(a, (a.shape[0], a.shape[1], a.shape[2] * a.shape[3]))


def _full_to_shards(a, row_sharded):
    L, R, C = a.shape
    if row_sharded:
        return jnp.moveaxis(jnp.reshape(a, (L, N_CHIPS, R // N_CHIPS, C)), 1, 0)
    return jnp.moveaxis(jnp.reshape(a, (L, R, N_CHIPS, C // N_CHIPS)), 2, 0)


def kernel(x, norm1_g, w_in, dn_conv_w, dn_a_log, dn_dt_bias, dn_onorm_g, sg_ln_g, sg_ln_b, sg_w, sg_b, w_branch_a, w_branch_b, w_out, norm2_g, ffn_w_gate, ffn_w_up, ffn_conv_w, ffn_conv_b, ffn_w_down, final_norm_g, loss_target, m_norm1_g, m_w_in, m_dn_conv_w, m_dn_a_log, m_dn_dt_bias, m_dn_onorm_g, m_sg_ln_g, m_sg_ln_b, m_sg_w, m_sg_b, m_w_branch_a, m_w_branch_b, m_w_out, m_norm2_g, m_ffn_w_gate, m_ffn_w_up, m_ffn_conv_w, m_ffn_conv_b, m_ffn_w_down, m_final_norm_g, v_norm1_g, v_w_in, v_dn_conv_w, v_dn_a_log, v_dn_dt_bias, v_dn_onorm_g, v_sg_ln_g, v_sg_ln_b, v_sg_w, v_sg_b, v_w_branch_a, v_w_branch_b, v_w_out, v_norm2_g, v_ffn_w_gate, v_ffn_w_up, v_ffn_conv_w, v_ffn_conv_b, v_ffn_w_down, v_final_norm_g):
    W = dict(norm1_g=norm1_g, w_in=w_in, dn_conv_w=dn_conv_w, dn_a_log=dn_a_log, dn_dt_bias=dn_dt_bias,
             dn_onorm_g=dn_onorm_g, sg_ln_g=sg_ln_g, sg_ln_b=sg_ln_b, sg_w=sg_w, sg_b=sg_b, w_branch_a=w_branch_a,
             w_branch_b=w_branch_b, w_out=w_out, norm2_g=norm2_g, ffn_w_gate=ffn_w_gate, ffn_w_up=ffn_w_up,
             ffn_conv_w=ffn_conv_w, ffn_conv_b=ffn_conv_b, ffn_w_down=ffn_w_down, final_norm_g=final_norm_g)
    M = dict(norm1_g=m_norm1_g, w_in=m_w_in, dn_conv_w=m_dn_conv_w, dn_a_log=m_dn_a_log, dn_dt_bias=m_dn_dt_bias,
             dn_onorm_g=m_dn_onorm_g, sg_ln_g=m_sg_ln_g, sg_ln_b=m_sg_ln_b, sg_w=m_sg_w, sg_b=m_sg_b,
             w_branch_a=m_w_branch_a, w_branch_b=m_w_branch_b, w_out=m_w_out, norm2_g=m_norm2_g,
             ffn_w_gate=m_ffn_w_gate, ffn_w_up=m_ffn_w_up, ffn_conv_w=m_ffn_conv_w, ffn_conv_b=m_ffn_conv_b,
             ffn_w_down=m_ffn_w_down, final_norm_g=m_final_norm_g)
    V = dict(norm1_g=v_norm1_g, w_in=v_w_in, dn_conv_w=v_dn_conv_w, dn_a_log=v_dn_a_log, dn_dt_bias=v_dn_dt_bias,
             dn_onorm_g=v_dn_onorm_g, sg_ln_g=v_sg_ln_g, sg_ln_b=v_sg_ln_b, sg_w=v_sg_w, sg_b=v_sg_b,
             w_branch_a=v_w_branch_a, w_branch_b=v_w_branch_b, w_out=v_w_out, norm2_g=v_norm2_g,
             ffn_w_gate=v_ffn_w_gate, ffn_w_up=v_ffn_w_up, ffn_conv_w=v_ffn_conv_w, ffn_conv_b=v_ffn_conv_b,
             ffn_w_down=v_ffn_w_down, final_norm_g=v_final_norm_g)
    cx, cy, cc = _coords()
    chip = 2 * cx + cy
    L = w_in.shape[0]

    D = w_in.shape[1]
    cs_in = w_in.shape[2]
    rp_in = -(-cs_in // 128) * 128

    def shard_for_gather(n):
        if n == "w_in":
            return jnp.pad(jnp.swapaxes(W[n], 1, 2).astype(BF16), ((0, 0), (0, rp_in - cs_in), (0, 0)))
        return W[n].astype(BF16)

    gathered = dict(zip(BIG, _ag_layers([shard_for_gather(n) for n in BIG])))
    full = {n: _shards_to_full(gathered[n], n in ROW_SHARDED) for n in BIG if n != "w_in"}
    full["w_in_t"] = jnp.reshape(jnp.moveaxis(gathered["w_in"][:, :, :cs_in], 0, 1), (L, N_CHIPS * cs_in, D))
    taps = _ag8(_pack_rows([W[n] for n in SMALL_SHARDED], 16))
    tap_shards = [_unpack(taps[2 * i], [W[n].shape for n in SMALL_SHARDED]) for i in range(N_CHIPS)]
    for k, n in enumerate(SMALL_SHARDED):
        full[n] = jnp.concatenate([tap_shards[i][k] for i in range(N_CHIPS)], axis=-1)

    layers = []
    for l in range(L):
        p = {n: (full[n][l] if n in full else W[n][l]) for n in W if n not in ("final_norm_g", "w_in")}
        p["w_in_t"] = full["w_in_t"][l]
        layers.append(_prep_layer(p))
    loss, dx, grads, dgf = _local_step(x[0], loss_target[0], layers, final_norm_g[None])

    def grad_shards(n):
        if n == "w_in":
            gt = jnp.stack([_w_in_unpad(g["w_in_t"]) for g in grads])
            gt = jnp.pad(jnp.reshape(gt, (L, N_CHIPS, cs_in, D)), ((0, 0), (0, 0), (0, rp_in - cs_in), (0, 0)))
            return jnp.moveaxis(gt, 1, 0)
        return _full_to_shards(jnp.stack([g[n] for g in grads]), n in ROW_SHARDED)

    Gs = [grad_shards(n) for n in BIG]
    B1s = _rs_pair_exchange(Gs)
    Ps = [_rs_add_pair(g, b, cc, "rs_add_pair_" + n) for n, g, b in zip(BIG, Gs, B1s)]
    B2s = _rs_chip_exchange(Ps)
    Rcs = [_sum_slots(b, "rs_sum_chips_" + n) for n, b in zip(BIG, B2s)]
    g_big = dict(zip(BIG, _rs_pair_gather(Rcs)))
    g_big["w_in"] = jnp.swapaxes(g_big["w_in"][:, :cs_in], 1, 2)

    small = {n: jnp.stack([g[n] for g in grads]) for n in SMALL if n != "final_norm_g"}
    small["final_norm_g"] = dgf
    shapes = [(L,) + tuple(full[n].shape[1:]) if n in SMALL_SHARDED else W[n].shape for n in SMALL] + [(1,)]
    sflat = _pack_rows([small[n] for n in SMALL] + [jnp.reshape(loss, (1,))], 16)
    sred = _unpack(_sum_slots(_ag8(sflat), "sum_small"), shapes)
    g_small = dict(zip(SMALL, sred[:-1]))
    loss_total = sred[-1][0]
    for n in SMALL_SHARDED:
        cs = W[n].shape[-1]
        g_small[n] = lax.dynamic_slice_in_dim(g_small[n], chip * cs, cs, axis=-1)

    delta, new_m, new_v = {}, {}, {}
    for n in BIG:
        delta[n], new_m[n], new_v[n] = _adamw(W[n], g_big[n], M[n], V[n], "adamw_" + n)
    s_shapes = [W[n].shape for n in SMALL]
    packed = [_pack_rows([d[n] for n in SMALL], 8) for d in (W, g_small, M, V)]
    outs = _adamw(*[a[None] for a in packed], "adamw_small")
    for d, o in zip((delta, new_m, new_v), outs):
        d.update(zip(SMALL, _unpack(o[0], s_shapes)))

    names = list(W)
    grad_w = {**g_big, **g_small}
    return (loss_total, dx[None], *[grad_w[n] for n in names], *[delta[n] for n in names],
            *[new_m[n] for n in names], *[new_v[n] for n in names])
```

```python
import functools
import math

import jax
import jax.numpy as jnp
from jax import lax
from jax.experimental import pallas as pl
from jax.experimental.pallas import tpu as pltpu

F32 = jnp.float32
BF16 = jnp.bfloat16
MESH = pl.DeviceIdType.MESH

EPS = 1e-6
H = 8
DH = 128
WD = H * DH
DNC = 64
SGC = 128
DN_K = 4
FF_K = 3
DEPTH = 2
N_CHIPS = 4

ADAM_LR = 0.001
ADAM_B1 = 0.9
ADAM_B2 = 0.999
ADAM_EPS = 1e-08
ADAM_WD = 0.01
ADAM_STEP = 10

VMEM_LIMIT = 56 * 1024 * 1024

NN = (((1,), (0,)), ((), ()))
NT = (((1,), (1,)), ((), ()))
TN = (((0,), (0,)), ((), ()))

OQ, OZ, OU, OV, OGA = 0, 3 * WD, 4 * WD, 5 * WD, 6 * WD


def _cp(*sem):
    return pltpu.CompilerParams(dimension_semantics=sem or None, vmem_limit_bytes=VMEM_LIMIT)


def _tile(dim, pref, unit=128):
    if dim <= pref:
        return dim
    t = (pref // unit) * unit
    while t >= unit:
        if dim % t == 0:
            return t
        t -= unit
    return dim


def _hdot(a, b, dn=NN):
    return lax.dot_general(a, b, dn, precision=lax.Precision.HIGHEST, preferred_element_type=F32)


def _bdot(a, b, dn=NN):
    return lax.dot_general(a.astype(BF16), b.astype(BF16), dn, preferred_element_type=F32)


def _lsum(x):
    return jnp.sum(x, axis=1, keepdims=True)


def _sig(x):
    return jax.nn.sigmoid(x)


def _dsilu(x):
    s = _sig(x)
    return s * (1.0 + x * (1.0 - s))


def _erf(x):
    a = jnp.abs(x)
    t = 1.0 / (1.0 + 0.3275911 * a)
    poly = t * (0.254829592 + t * (-0.284496736 + t * (1.421413741 + t * (-1.453152027 + t * 1.061405429))))
    r = 1.0 - poly * jnp.exp(-a * a)
    return jnp.where(x < 0, -r, r)


def _gelu(x):
    return 0.5 * x * (1.0 + _erf(x * (2.0 ** -0.5)))


def _dgelu(x):
    cdf = 0.5 * (1.0 + _erf(x * (2.0 ** -0.5)))
    pdf = jnp.exp(-0.5 * x * x) * (1.0 / math.sqrt(2.0 * math.pi))
    return cdf + x * pdf


def _shift_down(x, k):
    if k == 0:
        return x
    rows = lax.broadcasted_iota(jnp.int32, x.shape, 0)
    return jnp.where(rows >= k, pltpu.roll(x, k, 0), 0.0)


def _shift_up(x, k):
    if k == 0:
        return x
    n = x.shape[0]
    rows = lax.broadcasted_iota(jnp.int32, x.shape, 0)
    return jnp.where(rows < n - k, pltpu.roll(x, n - k, 0), 0.0)


def _mm(a, b, mode, out_dtype, add=None, name="mm"):
    if mode == "tn":
        K, M = a.shape
    else:
        M, K = a.shape
    N = b.shape[0] if mode == "nt" else b.shape[1]
    tm, tn, tk = _tile(M, 1152), _tile(N, 1536), _tile(K, 2048)
    nk = K // tk
    dn = {"nn": NN, "nt": NT, "tn": TN}[mode]

    def body(a_ref, b_ref, *rest):
        add_ref = rest[0] if add is not None else None
        o_ref = rest[-2] if nk > 1 else rest[-1]

        def finish(r):
            if add is not None:
                r = r + add_ref[...]
            o_ref[...] = r.astype(o_ref.dtype)

        part = lax.dot_general(a_ref[...], b_ref[...], dn, preferred_element_type=F32)
        if nk == 1:
            finish(part)
            return
        acc_ref = rest[-1]
        k = pl.program_id(2)

        @pl.when(k == 0)
        def _():
            acc_ref[...] = part

        @pl.when(k > 0)
        def _():
            acc_ref[...] += part

        @pl.when(k == nk - 1)
        def _():
            finish(acc_ref[...])

    a_spec = (pl.BlockSpec((tk, tm), lambda i, j, k: (k, i)) if mode == "tn"
              else pl.BlockSpec((tm, tk), lambda i, j, k: (i, k)))
    b_spec = (pl.BlockSpec((tn, tk), lambda i, j, k: (j, k)) if mode == "nt"
              else pl.BlockSpec((tk, tn), lambda i, j, k: (k, j)))
    o_spec = pl.BlockSpec((tm, tn), lambda i, j, k: (i, j))
    in_specs = [a_spec, b_spec] + ([o_spec] if add is not None else [])
    args = (a, b) + ((add,) if add is not None else ())
    return pl.pallas_call(
        body, name=name, grid=(M // tm, N // tn, nk), in_specs=in_specs, out_specs=o_spec,
        out_shape=jax.ShapeDtypeStruct((M, N), out_dtype),
        scratch_shapes=[pltpu.VMEM((tm, tn), F32)] if nk > 1 else [],
        compiler_params=_cp("parallel", "parallel", "arbitrary"),
    )(*args)


def _rms_fwd(x, g, name):
    T, D = x.shape
    tt = _tile(T, 256, 16)

    def body(x_ref, g_ref, o_ref):
        xv = x_ref[...]
        r = lax.rsqrt(jnp.mean(xv * xv, axis=-1, keepdims=True) + EPS)
        o_ref[...] = (xv * r * g_ref[...]).astype(o_ref.dtype)

    return pl.pallas_call(
        body, name=name, grid=(T // tt,),
        in_specs=[pl.BlockSpec((tt, D), lambda i: (i, 0)), pl.BlockSpec((1, D), lambda i: (0, 0))],
        out_specs=pl.BlockSpec((tt, D), lambda i: (i, 0)),
        out_shape=jax.ShapeDtypeStruct((T, D), BF16), compiler_params=_cp("parallel"),
    )(x, g)


def _rms_bwd(x, g, dh, dres, name):
    T, D = x.shape
    tt = _tile(T, 256, 16)

    def body(x_ref, g_ref, dh_ref, dres_ref, dx_ref, dg_ref):
        @pl.when(pl.program_id(0) == 0)
        def _():
            dg_ref[...] = jnp.zeros_like(dg_ref)

        xv = x_ref[...]
        r = lax.rsqrt(jnp.mean(xv * xv, axis=-1, keepdims=True) + EPS)
        xh = xv * r
        dh_v = dh_ref[...]
        dy = dh_v * g_ref[...]
        dx_ref[...] = dres_ref[...] + r * (dy - xh * jnp.mean(dy * xh, axis=-1, keepdims=True))
        dg_ref[...] += jnp.sum(dh_v * xh, axis=0, keepdims=True)

    row = pl.BlockSpec((tt, D), lambda i: (i, 0))
    vec = pl.BlockSpec((1, D), lambda i: (0, 0))
    return pl.pallas_call(
        body, name=name, grid=(T // tt,), in_specs=[row, vec, row, row], out_specs=[row, vec],
        out_shape=[jax.ShapeDtypeStruct((T, D), F32), jax.ShapeDtypeStruct((1, D), F32)],
        compiler_params=_cp("arbitrary"),
    )(x, g, dh, dres)


def _loss_head(x, g, tgt, name="loss_head"):
    T, D = x.shape
    tt = _tile(T, 256, 16)

    def body(x_ref, g_ref, t_ref, dx_ref, dg_ref, loss_ref):
        @pl.when(pl.program_id(0) == 0)
        def _():
            dg_ref[...] = jnp.zeros_like(dg_ref)
            loss_ref[...] = jnp.zeros_like(loss_ref)

        xv = x_ref[...]
        r = lax.rsqrt(jnp.mean(xv * xv, axis=-1, keepdims=True) + EPS)
        xh = xv * r
        err = xh * g_ref[...] - t_ref[...]
        part = 0.5 * jnp.sum(jnp.mean(err * err, axis=-1, keepdims=True), axis=0, keepdims=True)
        loss_ref[...] += jnp.broadcast_to(part, loss_ref.shape)
        dy = err * (1.0 / D)
        dg_ref[...] += jnp.sum(dy * xh, axis=0, keepdims=True)
        dyh = dy * g_ref[...]
        dx_ref[...] = r * (dyh - xh * jnp.mean(dyh * xh, axis=-1, keepdims=True))

    row = pl.BlockSpec((tt, D), lambda i: (i, 0))
    vec = pl.BlockSpec((1, D), lambda i: (0, 0))
    return pl.pallas_call(
        body, name=name, grid=(T // tt,), in_specs=[row, vec, row],
        out_specs=[row, vec, pl.BlockSpec((1, 128), lambda i: (0, 0))],
        out_shape=[jax.ShapeDtypeStruct((T, D), F32), jax.ShapeDtypeStruct((1, D), F32),
                   jax.ShapeDtypeStruct((1, 128), F32)],
        compiler_params=_cp("arbitrary"),
    )(x, g, tgt)


def _ba_fwd(proj, alog, dtb, oba, name="dn_ba_fwd"):
    T = proj.shape[0]
    tt = _tile(T, 512, 8)

    def body(p_ref, al_ref, dt_ref, o_ref):
        raw = p_ref[...]
        lane = lax.broadcasted_iota(jnp.int32, raw.shape, 1)
        z = raw + dt_ref[...]
        sp = jnp.maximum(z, 0.0) + jnp.log(1.0 + jnp.exp(-jnp.abs(z)))
        gl = -jnp.exp(al_ref[...]) * sp
        o_ref[...] = jnp.where(lane < H, _sig(raw), jnp.where(lane < 2 * H, gl, 0.0))

    vec = pl.BlockSpec((1, 128), lambda i: (0, 0))
    return pl.pallas_call(
        body, name=name, grid=(T // tt,),
        in_specs=[pl.BlockSpec((tt, 128), lambda i: (i, oba // 128)), vec, vec],
        out_specs=pl.BlockSpec((tt, 128), lambda i: (i, 0)),
        out_shape=jax.ShapeDtypeStruct((T, 128), F32), compiler_params=_cp("parallel"),
    )(proj, alog, dtb)


def _ba_bwd(proj, alog, dtb, dbg, oba, name="dn_ba_bwd"):
    T = proj.shape[0]
    tt = _tile(T, 512, 16)

    def body(p_ref, al_ref, dt_ref, d_ref, o_ref, dal_ref, ddt_ref):
        @pl.when(pl.program_id(0) == 0)
        def _():
            dal_ref[...] = jnp.zeros_like(dal_ref)
            ddt_ref[...] = jnp.zeros_like(ddt_ref)

        raw = p_ref[...]
        d = d_ref[...]
        lane = lax.broadcasted_iota(jnp.int32, raw.shape, 1)
        z = raw + dt_ref[...]
        sp = jnp.maximum(z, 0.0) + jnp.log(1.0 + jnp.exp(-jnp.abs(z)))
        na = -jnp.exp(al_ref[...])
        is_g = jnp.logical_and(lane >= H, lane < 2 * H)
        b = _sig(raw)
        dz = jnp.where(is_g, d * na * _sig(z), 0.0)
        o_ref[...] = jnp.where(lane < H, d * b * (1.0 - b), dz).astype(o_ref.dtype)
        dal_ref[...] += jnp.sum(jnp.where(is_g, d * na * sp, 0.0), axis=0, keepdims=True)
        ddt_ref[...] += jnp.sum(dz, axis=0, keepdims=True)

    vec = pl.BlockSpec((1, 128), lambda i: (0, 0))
    return pl.pallas_call(
        body, name=name, grid=(T // tt,),
        in_specs=[pl.BlockSpec((tt, 128), lambda i: (i, oba // 128)), vec, vec,
                  pl.BlockSpec((tt, 128), lambda i: (i, 0))],
        out_specs=[pl.BlockSpec((tt, 128), lambda i: (i, 0)), vec, vec],
        out_shape=[jax.ShapeDtypeStruct((T, 128), BF16), jax.ShapeDtypeStruct((1, 128), F32),
                   jax.ShapeDtypeStruct((1, 128), F32)],
        compiler_params=_cp("arbitrary"),
    )(proj, alog, dtb, dbg)


def _dn_prep_fwd(proj, convw, name="dn_prep_fwd"):
    T = proj.shape[0]
    nblk = 3 * H

    def body(p_ref, w_ref, o_ref):
        j = pl.program_id(0)
        xv = p_ref[...]
        w = w_ref[...]
        c = xv * w[DN_K - 1:DN_K, :]
        for k in range(1, DN_K):
            c = c + _shift_down(xv, k) * w[DN_K - 1 - k:DN_K - k, :]
        s = c * _sig(c)
        r = lax.rsqrt(_lsum(s * s) + EPS)
        o_ref[...] = jnp.where(j < 2 * H, s * r, s)

    return pl.pallas_call(
        body, name=name, grid=(nblk,),
        in_specs=[pl.BlockSpec((T, DH), lambda j: (0, j)), pl.BlockSpec((DN_K, DH), lambda j: (0, j))],
        out_specs=pl.BlockSpec((T, DH), lambda j: (0, j)),
        out_shape=jax.ShapeDtypeStruct((T, 3 * WD), F32), compiler_params=_cp("parallel"),
    )(proj, convw)


def _dn_prep_bwd(proj, convw, dq, dk, dv, name="dn_prep_bwd"):
    T = proj.shape[0]
    nblk = 3 * H

    def body(p_ref, w_ref, dq_ref, dk_ref, dv_ref, dx_ref, dw_ref):
        j = pl.program_id(0)
        xv = p_ref[...]
        w = w_ref[...]
        shifted = [_shift_down(xv, k) for k in range(DN_K)]
        c = shifted[0] * w[DN_K - 1:DN_K, :]
        for k in range(1, DN_K):
            c = c + shifted[k] * w[DN_K - 1 - k:DN_K - k, :]
        s = c * _sig(c)
        r = lax.rsqrt(_lsum(s * s) + EPS)
        y = s * r
        dy = jnp.where(j < H, dq_ref[...], jnp.where(j < 2 * H, dk_ref[...], dv_ref[...]))
        ds = jnp.where(j < 2 * H, r * (dy - y * _lsum(dy * y)), dy)
        dc = ds * _dsilu(c)
        dx = dc * w[DN_K - 1:DN_K, :]
        for k in range(1, DN_K):
            dx = dx + _shift_up(dc, k) * w[DN_K - 1 - k:DN_K - k, :]
        dx_ref[...] = dx.astype(dx_ref.dtype)
        rows = [jnp.sum(dc * shifted[DN_K - 1 - t], axis=0, keepdims=True) for t in range(DN_K)]
        dw_ref[...] = jnp.concatenate(rows, axis=0)

    hb = lambda off: pl.BlockSpec((T, DH), lambda j: (0, jnp.maximum(jnp.minimum(j - off, H - 1), 0)))
    return pl.pallas_call(
        body, name=name, grid=(nblk,),
        in_specs=[pl.BlockSpec((T, DH), lambda j: (0, j)), pl.BlockSpec((DN_K, DH), lambda j: (0, j)),
                  hb(0), hb(H), hb(2 * H)],
        out_specs=[pl.BlockSpec((T, DH), lambda j: (0, j)), pl.BlockSpec((DN_K, DH), lambda j: (0, j))],
        out_shape=[jax.ShapeDtypeStruct((T, 3 * WD), BF16), jax.ShapeDtypeStruct((DN_K, 3 * WD), F32)],
        compiler_params=_cp("parallel"),
    )(proj, convw, dq, dk, dv)


DN_BLOCK = 8


def _split3(a):
    hi = a.astype(BF16)
    r1 = a - hi.astype(F32)
    mid = r1.astype(BF16)
    return hi, mid, (r1 - mid.astype(F32)).astype(BF16)


def _dot3(a, b, dn=NN):
    ah, al, _ = _split3(a)
    bh, bl, _ = _split3(b)
    d = lambda p, q: lax.dot_general(p, q, dn, preferred_element_type=F32)
    return d(ah, bh) + d(ah, bl) + d(al, bh)


def _mask_dot(m, b, dn=NN):
    mb = m.astype(BF16)
    d = lambda q: (lax.dot_general(mb, q, dn, preferred_element_type=F32) if dn != TN
                   else lax.dot_general(q, mb, dn, preferred_element_type=F32))
    b0, b1, b2 = _split3(b)
    return d(b0) + d(b1) + d(b2)


def _tri_inv(A):
    ri = lax.broadcasted_iota(jnp.int32, A.shape, 0)
    ci = lax.broadcasted_iota(jnp.int32, A.shape, 1)
    X = -A
    P = jnp.where(ri == ci, 1.0, 0.0) + X
    Y = X
    for _ in range(int(math.log2(DNC)) - 1):
        Y = _dot3(Y, Y)
        P = P + _dot3(P, Y)
    return P


def _dn_masks():
    ri = lax.broadcasted_iota(jnp.int32, (DNC, DNC), 0)
    ci = lax.broadcasted_iota(jnp.int32, (DNC, DNC), 1)
    return ri >= ci, ri > ci


def _dn_decays(bg):
    causal, strict = _dn_masks()
    gc_cols = _mask_dot(jnp.where(causal, 1.0, 0.0), bg)
    gc_rows = _mask_dot(jnp.where(strict, 0.0, 1.0), bg, TN)
    return gc_cols, gc_rows


def _dn_chunk(q, k, v, bg, gc_cols, gc_rows, h):
    causal, strict = _dn_masks()
    q = q * (DH ** -0.5)
    beta = bg[:, h:h + 1]
    gcol = gc_cols[:, H + h:H + h + 1]
    grow = gc_rows[H + h:H + h + 1, :]
    dec = jnp.where(causal, jnp.exp(jnp.where(causal, gcol - grow, 0.0)), 0.0)
    eg = jnp.exp(gcol)
    gl = gcol[DNC - 1:DNC, :]
    ek = jnp.exp(gl - gcol)
    egl = jnp.exp(gl)
    kb = k * beta
    vb = v * beta
    kbe = kb * eg
    A = jnp.where(strict, _bdot(kb, k, NT) * dec, 0.0)
    P = jnp.where(causal, _bdot(q, k, NT) * dec, 0.0)
    return dict(q=q, k=k, v=v, beta=beta, dec=dec, eg=eg, ek=ek, egl=egl, kb=kb, vb=vb, kbe=kbe, A=A, P=P,
                qd=q * eg, kd=k * ek, causal=causal, strict=strict)


def _dn_core_fwd(qkv, bg, name="dn_core_fwd"):
    T = qkv.shape[0]
    n_chunks = T // DNC
    nb = _tile(n_chunks, DN_BLOCK, 1)
    tb = nb * DNC

    def body(q_ref, k_ref, v_ref, bg_ref, o_ref, s_ref, tm_ref, S_scr):
        @pl.when(pl.program_id(0) == 0)
        def _():
            S_scr[...] = jnp.zeros_like(S_scr)

        def chunk(n, carry):
            rows = pl.ds(pl.multiple_of(n * DNC, DNC), DNC)
            bgc = bg_ref[rows, :]
            gc_cols, gc_rows = _dn_decays(bgc)
            for h in range(H):
                sl = slice(h * DH, (h + 1) * DH)
                c = _dn_chunk(q_ref[rows, sl], k_ref[rows, sl], v_ref[rows, sl], bgc, gc_cols, gc_rows, h)
                Tm = _tri_inv(c["A"])
                tm_ref[h, n] = Tm
                S = S_scr[h]
                s_ref[h, n] = S
                u = _bdot(Tm, c["vb"])
                w = _bdot(Tm, c["kbe"])
                vn = u - _bdot(w, S)
                o_ref[rows, sl] = _bdot(c["qd"], S) + _bdot(c["P"], vn)
                S_scr[h] = S * c["egl"] + _bdot(c["kd"], vn, TN)
            return carry

        lax.fori_loop(0, nb, chunk, 0)

    blk = lambda j: pl.BlockSpec((tb, WD), lambda i: (i, j))
    return pl.pallas_call(
        body, name=name, grid=(n_chunks // nb,),
        in_specs=[blk(0), blk(1), blk(2), pl.BlockSpec((tb, 128), lambda i: (i, 0))],
        out_specs=[blk(0), pl.BlockSpec((H, nb, DH, DH), lambda i: (0, i, 0, 0)),
                   pl.BlockSpec((H, nb, DNC, DNC), lambda i: (0, i, 0, 0))],
        out_shape=[jax.ShapeDtypeStruct((T, WD), F32), jax.ShapeDtypeStruct((H, n_chunks, DH, DH), F32),
                   jax.ShapeDtypeStruct((H, n_chunks, DNC, DNC), F32)],
        scratch_shapes=[pltpu.VMEM((H, DH, DH), F32)],
        compiler_params=_cp("arbitrary"),
    )(qkv, qkv, qkv, bg)


def _dn_core_bwd(qkv, bg, s_all, tm_all, do, name="dn_core_bwd"):
    T = qkv.shape[0]
    n_chunks = T // DNC
    nb = _tile(n_chunks, DN_BLOCK, 1)
    tb = nb * DNC
    n_blocks = n_chunks // nb

    def body(q_ref, k_ref, v_ref, bg_ref, s_ref, tm_ref, do_ref, dq_ref, dk_ref, dv_ref, dbg_ref, dS_scr):
        @pl.when(pl.program_id(0) == 0)
        def _():
            dS_scr[...] = jnp.zeros_like(dS_scr)

        lane = lax.broadcasted_iota(jnp.int32, (DNC, 128), 1)
        row = lax.broadcasted_iota(jnp.int32, (DNC, 1), 0)
        ones = jnp.ones((DNC, DNC), F32)
        _, strict = _dn_masks()

        def chunk(i, carry):
            n = nb - 1 - i
            rows = pl.ds(pl.multiple_of(n * DNC, DNC), DNC)
            bgc = bg_ref[rows, :]
            gc_cols, gc_rows = _dn_decays(bgc)
            d_beta_all = jnp.zeros((DNC, 128), F32)
            d_gc_all = jnp.zeros((DNC, 128), F32)
            for h in range(H):
                sl = slice(h * DH, (h + 1) * DH)
                c = _dn_chunk(q_ref[rows, sl], k_ref[rows, sl], v_ref[rows, sl], bgc, gc_cols, gc_rows, h)
                q, k, v, beta = c["q"], c["k"], c["v"], c["beta"]
                dec, eg, ek, egl = c["dec"], c["eg"], c["ek"], c["egl"]
                kb, vb, kbe, A, P, qd, kd = c["kb"], c["vb"], c["kbe"], c["A"], c["P"], c["qd"], c["kd"]
                S = s_ref[h, n]
                Tm = tm_ref[h, n]
                u = _bdot(Tm, vb)
                w = _bdot(Tm, kbe)
                vn = u - _bdot(w, S)
                d_o = do_ref[rows, sl]
                dS1 = dS_scr[h]
                d_qd = _bdot(d_o, S, NT)
                dP = jnp.where(c["causal"], _bdot(d_o, vn, NT), 0.0)
                d_vn = _bdot(P, d_o, TN) + _bdot(kd, dS1)
                d_kd = _bdot(vn, dS1, NT)
                d_egl = jnp.sum(_lsum(dS1 * S), axis=0, keepdims=True)
                dS_scr[h] = dS1 * egl + _bdot(qd, d_o, TN) - _bdot(w, d_vn, TN)
                d_w = -_bdot(d_vn, S, NT)
                d_vb = _bdot(Tm, d_vn, TN)
                d_kbe = _bdot(Tm, d_w, TN)
                dA = jnp.where(c["strict"], -(_bdot(d_vb, u, NT) + _bdot(d_kbe, w, NT)), 0.0)
                dMA = dA * dec
                dMP = dP * dec
                d_kb = _bdot(dMA, k) + d_kbe * eg
                d_k = _bdot(dMA, kb, TN) + _bdot(dMP, q, TN) + d_kd * ek + d_kb * beta
                d_qs = _bdot(dMP, k) + d_qd * eg
                E = dA * A + dP * P
                col_sums = _mask_dot(ones, E, TN)[:, :1]
                t_kd = _lsum(d_kd * kd)
                d_gl = jnp.sum(t_kd, axis=0, keepdims=True) + d_egl * egl
                d_gc = (_lsum(E) - col_sums + _lsum(d_qd * qd) + _lsum(d_kbe * kbe) - t_kd
                        + jnp.where(row == DNC - 1, d_gl, 0.0))
                d_beta = _lsum(d_kb * k) + _lsum(d_vb * v)
                d_beta_all = d_beta_all + jnp.where(lane == h, d_beta, 0.0)
                d_gc_all = d_gc_all + jnp.where(lane == h + H, d_gc, 0.0)
                dq_ref[rows, sl] = d_qs * (DH ** -0.5)
                dk_ref[rows, sl] = d_k
                dv_ref[rows, sl] = d_vb * beta
            dbg_ref[rows, :] = d_beta_all + _mask_dot(jnp.where(strict, 0.0, 1.0), d_gc_all)
            return carry

        lax.fori_loop(0, nb, chunk, 0)

    blk = lambda j: pl.BlockSpec((tb, WD), lambda i: (n_blocks - 1 - i, j))
    small = pl.BlockSpec((tb, 128), lambda i: (n_blocks - 1 - i, 0))
    return pl.pallas_call(
        body, name=name, grid=(n_blocks,),
        in_specs=[blk(0), blk(1), blk(2), small,
                  pl.BlockSpec((H, nb, DH, DH), lambda i: (0, n_blocks - 1 - i, 0, 0)),
                  pl.BlockSpec((H, nb, DNC, DNC), lambda i: (0, n_blocks - 1 - i, 0, 0)), blk(0)],
        out_specs=[blk(0), blk(0), blk(0), small],
        out_shape=[jax.ShapeDtypeStruct((T, WD), F32)] * 3 + [jax.ShapeDtypeStruct((T, 128), F32)],
        scratch_shapes=[pltpu.VMEM((H, DH, DH), F32)],
        compiler_params=_cp("arbitrary"),
    )(qkv, qkv, qkv, bg, s_all, tm_all, do)


def _dn_post_fwd(o, proj, gon, name="dn_post_fwd"):
    T = o.shape[0]
    tt = _tile(T, 256, 16)

    def body(o_ref, z_ref, g_ref, y_ref):
        for hh in range(H):
            sl = slice(hh * DH, (hh + 1) * DH)
            ov = o_ref[:, sl]
            zv = z_ref[:, sl]
            r = lax.rsqrt(jnp.mean(ov * ov, axis=-1, keepdims=True) + EPS)
            y_ref[:, sl] = (ov * r * g_ref[...] * (zv * _sig(zv))).astype(y_ref.dtype)

    return pl.pallas_call(
        body, name=name, grid=(T // tt,),
        in_specs=[pl.BlockSpec((tt, WD), lambda i: (i, 0)), pl.BlockSpec((tt, WD), lambda i: (i, OZ // WD)),
                  pl.BlockSpec((1, DH), lambda i: (0, 0))],
        out_specs=pl.BlockSpec((tt, WD), lambda i: (i, 0)),
        out_shape=jax.ShapeDtypeStruct((T, WD), BF16), compiler_params=_cp("parallel"),
    )(o, proj, gon)


def _dn_post_bwd(o, proj, gon, dy, name="dn_post_bwd"):
    T = o.shape[0]
    tt = _tile(T, 256, 16)

    def body(o_ref, z_ref, g_ref, dy_ref, do_ref, dz_ref, dg_ref):
        @pl.when(pl.program_id(0) == 0)
        def _():
            dg_ref[...] = jnp.zeros_like(dg_ref)

        acc = jnp.zeros((1, DH), F32)
        for hh in range(H):
            sl = slice(hh * DH, (hh + 1) * DH)
            ov = o_ref[:, sl]
            zv = z_ref[:, sl]
            dyv = dy_ref[:, sl]
            r = lax.rsqrt(jnp.mean(ov * ov, axis=-1, keepdims=True) + EPS)
            oh = ov * r
            nrm = oh * g_ref[...]
            dn = dyv * (zv * _sig(zv))
            dz_ref[:, sl] = (dyv * nrm * _dsilu(zv)).astype(dz_ref.dtype)
            doh = dn * g_ref[...]
            do_ref[:, sl] = r * (doh - oh * jnp.mean(doh * oh, axis=-1, keepdims=True))
            acc = acc + jnp.sum(dn * oh, axis=0, keepdims=True)
        dg_ref[...] += acc

    row = pl.BlockSpec((tt, WD), lambda i: (i, 0))
    vec = pl.BlockSpec((1, DH), lambda i: (0, 0))
    return pl.pallas_call(
        body, name=name, grid=(T // tt,),
        in_specs=[row, pl.BlockSpec((tt, WD), lambda i: (i, OZ // WD)), vec, row],
        out_specs=[row, row, vec],
        out_shape=[jax.ShapeDtypeStruct((T, WD), F32), jax.ShapeDtypeStruct((T, WD), BF16),
                   jax.ShapeDtypeStruct((1, DH), F32)],
        compiler_params=_cp("arbitrary"),
    )(o, proj, gon, dy)


def _sg_common(u_ref, v_ref, lng_ref, lnb_ref):
    ur = u_ref[...]
    vr = v_ref[...]
    vgel = _gelu(vr)
    mu = jnp.mean(vgel, axis=-1, keepdims=True)
    xc = vgel - mu
    rs = lax.rsqrt(jnp.mean(xc * xc, axis=-1, keepdims=True) + EPS)
    xh = xc * rs
    vg = xh * lng_ref[...] + lnb_ref[...]
    return ur, vr, rs, xh, vg


def _sg_fwd(proj, lng, lnb, sgw, sgbt, name="sg_fwd"):
    T = proj.shape[0]

    def body(u_ref, v_ref, lng_ref, lnb_ref, w_ref, bt_ref, y_ref):
        ur, _, _, _, vg = _sg_common(u_ref, v_ref, lng_ref, lnb_ref)
        ri = lax.broadcasted_iota(jnp.int32, (SGC, SGC), 0)
        ci = lax.broadcasted_iota(jnp.int32, (SGC, SGC), 1)
        ug = _gelu(ur)
        for g in range(H):
            sl = slice(g * DH, (g + 1) * DH)
            ws = jnp.where(ri >= ci, w_ref[g], 0.0)
            mixed = _bdot(ws, vg[:, sl]) + bt_ref[:, g:g + 1]
            y_ref[:, sl] = (ug[:, sl] * mixed).astype(y_ref.dtype)

    vec = pl.BlockSpec((1, WD), lambda i: (0, 0))
    return pl.pallas_call(
        body, name=name, grid=(T // SGC,),
        in_specs=[pl.BlockSpec((SGC, WD), lambda i: (i, OU // WD)), pl.BlockSpec((SGC, WD), lambda i: (i, OV // WD)),
                  vec, vec, pl.BlockSpec((H, SGC, SGC), lambda i: (0, 0, 0)),
                  pl.BlockSpec((SGC, H), lambda i: (0, 0))],
        out_specs=pl.BlockSpec((SGC, WD), lambda i: (i, 0)),
        out_shape=jax.ShapeDtypeStruct((T, WD), BF16), compiler_params=_cp("parallel"),
    )(proj, proj, lng, lnb, sgw, sgbt)


def _sg_bwd(proj, lng, lnb, sgw, sgbt, dy, name="sg_bwd"):
    T = proj.shape[0]

    def body(u_ref, v_ref, lng_ref, lnb_ref, w_ref, bt_ref, dy_ref,
             du_ref, dv_ref, dw_ref, dbt_ref, dlng_ref, dlnb_ref):
        @pl.when(pl.program_id(0) == 0)
        def _():
            dw_ref[...] = jnp.zeros_like(dw_ref)
            dbt_ref[...] = jnp.zeros_like(dbt_ref)
            dlng_ref[...] = jnp.zeros_like(dlng_ref)
            dlnb_ref[...] = jnp.zeros_like(dlnb_ref)

        ur, vr, rs, xh, vg = _sg_common(u_ref, v_ref, lng_ref, lnb_ref)
        ri = lax.broadcasted_iota(jnp.int32, (SGC, SGC), 0)
        ci = lax.broadcasted_iota(jnp.int32, (SGC, SGC), 1)
        ug = _gelu(ur)
        dyv = dy_ref[...]
        dbt = jnp.zeros((SGC, 128), F32)
        dvg_parts = []
        for g in range(H):
            sl = slice(g * DH, (g + 1) * DH)
            ws = jnp.where(ri >= ci, w_ref[g], 0.0)
            mixed = _bdot(ws, vg[:, sl]) + bt_ref[:, g:g + 1]
            dyg = dyv[:, sl]
            du_ref[:, sl] = (dyg * mixed * _dgelu(ur[:, sl])).astype(du_ref.dtype)
            dmix = dyg * ug[:, sl]
            dw_ref[g] += jnp.where(ri >= ci, _bdot(dmix, vg[:, sl], NT), 0.0)
            dbt = dbt + jnp.where(ci == g, _lsum(dmix), 0.0)
            dvg_parts.append(_bdot(ws, dmix, TN))
        dbt_ref[...] += dbt
        dvg = jnp.concatenate(dvg_parts, axis=1)
        dlng_ref[...] += jnp.sum(dvg * xh, axis=0, keepdims=True)
        dlnb_ref[...] += jnp.sum(dvg, axis=0, keepdims=True)
        dxh = dvg * lng_ref[...]
        dvgel = rs * (dxh - jnp.mean(dxh, axis=-1, keepdims=True) - xh * jnp.mean(dxh * xh, axis=-1, keepdims=True))
        dv_ref[...] = (dvgel * _dgelu(vr)).astype(dv_ref.dtype)

    vec = pl.BlockSpec((1, WD), lambda i: (0, 0))
    row = pl.BlockSpec((SGC, WD), lambda i: (i, 0))
    return pl.pallas_call(
        body, name=name, grid=(T // SGC,),
        in_specs=[pl.BlockSpec((SGC, WD), lambda i: (i, OU // WD)), pl.BlockSpec((SGC, WD), lambda i: (i, OV // WD)),
                  vec, vec, pl.BlockSpec((H, SGC, SGC), lambda i: (0, 0, 0)),
                  pl.BlockSpec((SGC, H), lambda i: (0, 0)), row],
        out_specs=[row, row, pl.BlockSpec((H, SGC, SGC), lambda i: (0, 0, 0)),
                   pl.BlockSpec((SGC, 128), lambda i: (0, 0)), vec, vec],
        out_shape=[jax.ShapeDtypeStruct((T, WD), BF16), jax.ShapeDtypeStruct((T, WD), BF16),
                   jax.ShapeDtypeStruct((H, SGC, SGC), F32), jax.ShapeDtypeStruct((SGC, 128), F32),
                   jax.ShapeDtypeStruct((1, WD), F32), jax.ShapeDtypeStruct((1, WD), F32)],
        compiler_params=_cp("arbitrary"),
    )(proj, proj, lng, lnb, sgw, sgbt, dy)


def _merge_fwd(proj, yap, ybp, D, name="merge_fwd"):
    T = proj.shape[0]
    tt = _tile(T, 256, 16)

    def body(ga_ref, gb_ref, a_ref, b_ref, o_ref):
        o_ref[...] = (_sig(ga_ref[...]) * a_ref[...] + _sig(gb_ref[...]) * b_ref[...]).astype(o_ref.dtype)

    row = pl.BlockSpec((tt, D), lambda i: (i, 0))
    return pl.pallas_call(
        body, name=name, grid=(T // tt,),
        in_specs=[pl.BlockSpec((tt, D), lambda i: (i, OGA // D)), pl.BlockSpec((tt, D), lambda i: (i, OGA // D + 1)),
                  row, row],
        out_specs=row, out_shape=jax.ShapeDtypeStruct((T, D), BF16), compiler_params=_cp("parallel"),
    )(proj, proj, yap, ybp)


def _merge_bwd(proj, yap, ybp, dm, D, name="merge_bwd"):
    T = proj.shape[0]
    tt = _tile(T, 256, 16)

    def body(ga_ref, gb_ref, a_ref, b_ref, dm_ref, da_ref, db_ref, dga_ref, dgb_ref):
        d = dm_ref[...]
        sa = _sig(ga_ref[...])
        sb = _sig(gb_ref[...])
        da_ref[...] = (d * sa).astype(da_ref.dtype)
        db_ref[...] = (d * sb).astype(db_ref.dtype)
        dga_ref[...] = (d * a_ref[...] * sa * (1.0 - sa)).astype(dga_ref.dtype)
        dgb_ref[...] = (d * b_ref[...] * sb * (1.0 - sb)).astype(dgb_ref.dtype)

    row = pl.BlockSpec((tt, D), lambda i: (i, 0))
    return pl.pallas_call(
        body, name=name, grid=(T // tt,),
        in_specs=[pl.BlockSpec((tt, D), lambda i: (i, OGA // D)), pl.BlockSpec((tt, D), lambda i: (i, OGA // D + 1)),
                  row, row, row],
        out_specs=[row] * 4, out_shape=[jax.ShapeDtypeStruct((T, D), BF16)] * 4,
        compiler_params=_cp("parallel"),
    )(proj, proj, yap, ybp, dm)


def _ffn_act_fwd(gp, up, cw, cb, name="ffn_act_fwd"):
    T, F = gp.shape

    def body(g_ref, u_ref, w_ref, b_ref, o_ref):
        gv = g_ref[...]
        w = w_ref[...]
        c = gv * w[FF_K - 1:FF_K, :] + b_ref[...]
        for k in range(1, FF_K):
            c = c + _shift_down(gv, k) * w[FF_K - 1 - k:FF_K - k, :]
        o_ref[...] = (c * _sig(c) * u_ref[...]).astype(o_ref.dtype)

    col = pl.BlockSpec((T, 128), lambda j: (0, j))
    return pl.pallas_call(
        body, name=name, grid=(F // 128,),
        in_specs=[col, col, pl.BlockSpec((FF_K, 128), lambda j: (0, j)), pl.BlockSpec((1, 128), lambda j: (0, j))],
        out_specs=col, out_shape=jax.ShapeDtypeStruct((T, F), BF16), compiler_params=_cp("parallel"),
    )(gp, up, cw, cb)


def _ffn_act_bwd(gp, up, cw, cb, dact, name="ffn_act_bwd"):
    T, F = gp.shape

    def body(g_ref, u_ref, w_ref, b_ref, d_ref, dg_ref, du_ref, dw_ref, db_ref):
        gv = g_ref[...]
        w = w_ref[...]
        shifted = [_shift_down(gv, k) for k in range(FF_K)]
        c = shifted[0] * w[FF_K - 1:FF_K, :] + b_ref[...]
        for k in range(1, FF_K):
            c = c + shifted[k] * w[FF_K - 1 - k:FF_K - k, :]
        d = d_ref[...]
        du_ref[...] = (d * c * _sig(c)).astype(du_ref.dtype)
        dc = d * u_ref[...] * _dsilu(c)
        dg = dc * w[FF_K - 1:FF_K, :]
        for k in range(1, FF_K):
            dg = dg + _shift_up(dc, k) * w[FF_K - 1 - k:FF_K - k, :]
        dg_ref[...] = dg.astype(dg_ref.dtype)
        rows = [jnp.sum(dc * shifted[FF_K - 1 - t], axis=0, keepdims=True) for t in range(FF_K)]
        dw_ref[...] = jnp.concatenate(rows, axis=0)
        db_ref[...] = jnp.sum(dc, axis=0, keepdims=True)

    col = pl.BlockSpec((T, 128), lambda j: (0, j))
    wspec = pl.BlockSpec((FF_K, 128), lambda j: (0, j))
    bspec = pl.BlockSpec((1, 128), lambda j: (0, j))
    return pl.pallas_call(
        body, name=name, grid=(F // 128,),
        in_specs=[col, col, wspec, bspec, col], out_specs=[col, col, wspec, bspec],
        out_shape=[jax.ShapeDtypeStruct((T, F), BF16), jax.ShapeDtypeStruct((T, F), BF16),
                   jax.ShapeDtypeStruct((FF_K, F), F32), jax.ShapeDtypeStruct((1, F), F32)],
        compiler_params=_cp("parallel"),
    )(gp, up, cw, cb, dact)


def _layer_fwd(x, w):
    D = x.shape[1]
    oba = OGA + 2 * D
    h = _rms_fwd(x, w["norm1_g"], "rms1_fwd")
    proj = _mm(h, w["w_in_t"], "nt", F32, name="mm_proj")
    bg = _ba_fwd(proj, w["alog_row"], w["dtb_row"], oba)
    qkv = _dn_prep_fwd(proj, w["dn_conv_w"])
    o, s_all, tm_all = _dn_core_fwd(qkv, bg)
    ya = _dn_post_fwd(o, proj, w["dn_onorm_g"])
    yb = _sg_fwd(proj, w["sg_ln_g"], w["sg_ln_b"], w["sg_w"], w["sg_bt"])
    yap = _mm(ya, w["w_branch_a"], "nn", F32, name="mm_branch")
    ybp = _mm(yb, w["w_branch_b"], "nn", F32, name="mm_branch")
    merged = _merge_fwd(proj, yap, ybp, D)
    x1 = _mm(merged, w["w_out"], "nn", F32, add=x, name="mm_out")
    h2 = _rms_fwd(x1, w["norm2_g"], "rms2_fwd")
    gp = _mm(h2, w["ffn_w_gate"], "nn", F32, name="mm_ffn_in")
    up = _mm(h2, w["ffn_w_up"], "nn", F32, name="mm_ffn_in")
    act = _ffn_act_fwd(gp, up, w["ffn_conv_w"], w["ffn_conv_b"])
    x2 = _mm(act, w["ffn_w_down"], "nn", F32, add=x1, name="mm_ffn_down")
    saved = dict(x=x, h=h, proj=proj, bg=bg, qkv=qkv, o=o, s_all=s_all, tm_all=tm_all, ya=ya, yb=yb, yap=yap,
                 ybp=ybp, merged=merged, x1=x1, h2=h2, gp=gp, up=up, act=act)
    return x2, saved


def _layer_bwd(dx2, w, s):
    D = dx2.shape[1]
    oba = OGA + 2 * D
    g = {}
    dx2b = dx2.astype(BF16)
    dact = _mm(dx2b, w["ffn_w_down"], "nt", F32, name="mm_d_act")
    g["ffn_w_down"] = _mm(s["act"], dx2b, "tn", BF16, name="mm_dw_down")
    dgp, dup, g["ffn_conv_w"], g["ffn_conv_b"] = _ffn_act_bwd(s["gp"], s["up"], w["ffn_conv_w"], w["ffn_conv_b"], dact)
    dh2 = _mm(dgp, w["ffn_w_gate"], "nt", F32, name="mm_dh2")
    dh2 = _mm(dup, w["ffn_w_up"], "nt", F32, add=dh2, name="mm_dh2_acc")
    g["ffn_w_gate"] = _mm(s["h2"], dgp, "tn", BF16, name="mm_dw_ffn_in")
    g["ffn_w_up"] = _mm(s["h2"], dup, "tn", BF16, name="mm_dw_ffn_in")
    dx1, g["norm2_g"] = _rms_bwd(s["x1"], w["norm2_g"], dh2, dx2, "rms2_bwd")
    dx1b = dx1.astype(BF16)
    dm = _mm(dx1b, w["w_out"], "nt", F32, name="mm_d_merged")
    g["w_out"] = _mm(s["merged"], dx1b, "tn", BF16, name="mm_dw_out")
    dyap, dybp, dga, dgb = _merge_bwd(s["proj"], s["yap"], s["ybp"], dm, D)
    dya = _mm(dyap, w["w_branch_a"], "nt", F32, name="mm_d_branch")
    dyb = _mm(dybp, w["w_branch_b"], "nt", F32, name="mm_d_branch")
    g["w_branch_a"] = _mm(s["ya"], dyap, "tn", BF16, name="mm_dw_branch")
    g["w_branch_b"] = _mm(s["yb"], dybp, "tn", BF16, name="mm_dw_branch")
    du, dv, g["sg_w"], dbt, g["sg_ln_g"], g["sg_ln_b"] = _sg_bwd(
        s["proj"], w["sg_ln_g"], w["sg_ln_b"], w["sg_w"], w["sg_bt"], dyb)
    g["sg_b"] = jnp.transpose(dbt[:, :H])
    do, dz, g["dn_onorm_g"] = _dn_post_bwd(s["o"], s["proj"], w["dn_onorm_g"], dya)
    dq, dk, dvv, dbg = _dn_core_bwd(s["qkv"], s["bg"], s["s_all"], s["tm_all"], do)
    dqkv, g["dn_conv_w"] = _dn_prep_bwd(s["proj"], w["dn_conv_w"], dq, dk, dvv)
    dba, dal, ddt = _ba_bwd(s["proj"], w["alog_row"], w["dtb_row"], dbg, oba)
    g["dn_a_log"] = dal[0, H:2 * H]
    g["dn_dt_bias"] = ddt[0, H:2 * H]
    dproj = jnp.concatenate([dqkv, dz, du, dv, dga, dgb, dba], axis=1)
    dh = _mm(dproj, w["w_in_t"], "nn", F32, name="mm_dh")
    g["w_in_t"] = _mm(dproj, s["h"], "tn", BF16, name="mm_dw_in")
    dx, g["norm1_g"] = _rms_bwd(s["x"], w["norm1_g"], dh, dx1, "rms1_bwd")
    return dx, g


def _local_step(x, tgt, layers, final_g):
    saved = []
    for w in layers:
        x, s = _layer_fwd(x, w)
        saved.append(s)
    dx, dgf, loss = _loss_head(x, final_g, tgt)
    grads = [None] * len(layers)
    for l in reversed(range(len(layers))):
        dx, grads[l] = _layer_bwd(dx, layers[l], saved[l])
    return loss[0, 0], dx, grads, dgf


def _w_in_pad(wt):
    c1 = 4 * WD
    return jnp.concatenate([wt[:c1], wt[c1 + 2 * H:], wt[c1:c1 + 2 * H],
                            jnp.zeros((128 - 2 * H, wt.shape[1]), wt.dtype)], axis=0)


def _w_in_unpad(gt):
    c1 = 4 * WD
    n = gt.shape[0] - 128
    return jnp.concatenate([gt[:c1], gt[n:n + 2 * H], gt[c1:n]], axis=0)


def _row128(v, off):
    return jnp.pad(v, (off, 128 - off - v.shape[0]))[None]


def _prep_layer(p):
    return dict(
        norm1_g=p["norm1_g"][None], w_in_t=_w_in_pad(p["w_in_t"]),
        alog_row=_row128(p["dn_a_log"], H), dtb_row=_row128(p["dn_dt_bias"], H),
        dn_conv_w=p["dn_conv_w"], dn_onorm_g=p["dn_onorm_g"][None],
        sg_ln_g=p["sg_ln_g"][None], sg_ln_b=p["sg_ln_b"][None], sg_w=p["sg_w"], sg_bt=jnp.transpose(p["sg_b"]),
        w_branch_a=p["w_branch_a"], w_branch_b=p["w_branch_b"], w_out=p["w_out"], norm2_g=p["norm2_g"][None],
        ffn_w_gate=p["ffn_w_gate"], ffn_w_up=p["ffn_w_up"], ffn_conv_w=p["ffn_conv_w"],
        ffn_conv_b=p["ffn_conv_b"][None], ffn_w_down=p["ffn_w_down"])


HBM_SPEC = pl.BlockSpec(memory_space=pltpu.HBM)


def _coords():
    return lax.axis_index("x"), lax.axis_index("y"), lax.axis_index("c")


def _other_chips(x, y):
    return [(1 - x, y), (x, 1 - y), (1 - x, 1 - y)]


def _remote(src, dst, send_sems, recv_sems, k, dev):
    return pltpu.make_async_remote_copy(src_ref=src, dst_ref=dst, send_sem=send_sems.at[k], recv_sem=recv_sems.at[k],
                                        device_id=dev, device_id_type=MESH)


def _ag_layers(ws):
    n = len(ws)

    def body(*refs):
        w_refs, o_refs = refs[:n], refs[n:2 * n]
        send_sems, recv_sems = refs[2 * n:]
        x, y, c = _coords()
        me = 2 * x + y
        sib = (x, y, 1 - c)
        chips = _other_chips(x, y)

        def ici(k, j, chip, owner):
            return _remote(w_refs[k].at[c], o_refs[k].at[owner, c], send_sems, recv_sems, 6 * k + j,
                           (chip[0], chip[1], c))

        def d2d(k, j, owner, layer):
            return _remote(o_refs[k].at[owner, layer], o_refs[k].at[owner, layer], send_sems, recv_sems,
                           6 * k + 3 + j, sib)

        sends = [ici(k, j, chip, me) for k in range(n) for j, chip in enumerate(chips)]
        for cp in sends:
            cp.start()
        passed = []
        for k in range(n):
            for j, chip in enumerate(chips):
                owner = 2 * chip[0] + chip[1]
                ici(k, j, chip, owner).wait_recv()
                cp = d2d(k, j, owner, c)
                cp.start()
                passed.append(cp)
        for k in range(n):
            for j, chip in enumerate(chips):
                d2d(k, j, 2 * chip[0] + chip[1], 1 - c).wait_recv()
        for cp in sends + passed:
            cp.wait_send()

    return pl.pallas_call(
        body, name="ag_weights", out_shape=[jax.ShapeDtypeStruct((N_CHIPS,) + w.shape, w.dtype) for w in ws],
        in_specs=[HBM_SPEC] * n, out_specs=[HBM_SPEC] * n,
        scratch_shapes=[pltpu.SemaphoreType.DMA((6 * n,)), pltpu.SemaphoreType.DMA((6 * n,))],
    )(*ws)


def _rs_pair_exchange(Gs):
    n = len(Gs)

    def body(*refs):
        g_refs, b_refs = refs[:n], refs[n:2 * n]
        send_sems, recv_sems = refs[2 * n:]
        x, y, c = _coords()
        cps = [_remote(g_refs[k].at[i, 1 - c], b_refs[k].at[i], send_sems, recv_sems, N_CHIPS * k + i, (x, y, 1 - c))
               for k in range(n) for i in range(N_CHIPS)]
        for cp in cps:
            cp.start()
        for cp in cps:
            cp.wait()

    return pl.pallas_call(
        body, name="rs_pair_exchange",
        out_shape=[jax.ShapeDtypeStruct((N_CHIPS,) + g.shape[2:], g.dtype) for g in Gs],
        in_specs=[HBM_SPEC] * n, out_specs=[HBM_SPEC] * n,
        scratch_shapes=[pltpu.SemaphoreType.DMA((N_CHIPS * n,)), pltpu.SemaphoreType.DMA((N_CHIPS * n,))],
    )(*Gs)


def _rs_add_pair(G, B, c, name):
    _, _, R, C = G.shape
    tr = _tile(R, 256, 16)

    def body(c_ref, g_ref, b_ref, o_ref):
        o_ref[0] = (g_ref[0, 0].astype(F32) + b_ref[0].astype(F32)).astype(o_ref.dtype)

    grid_spec = pltpu.PrefetchScalarGridSpec(
        num_scalar_prefetch=1, grid=(N_CHIPS, R // tr),
        in_specs=[pl.BlockSpec((1, 1, tr, C), lambda i, r, c_ref: (i, c_ref[0], r, 0)),
                  pl.BlockSpec((1, tr, C), lambda i, r, c_ref: (i, r, 0))],
        out_specs=pl.BlockSpec((1, tr, C), lambda i, r, c_ref: (i, r, 0)))
    return pl.pallas_call(
        body, name=name, grid_spec=grid_spec, out_shape=jax.ShapeDtypeStruct((N_CHIPS, R, C), G.dtype),
        compiler_params=_cp("parallel", "parallel"),
    )(jnp.reshape(c, (1,)).astype(jnp.int32), G, B)


def _rs_chip_exchange(Ps):
    n = len(Ps)

    def body(*refs):
        p_refs, b_refs = refs[:n], refs[n:2 * n]
        send_sems, recv_sems = refs[2 * n:]
        x, y, c = _coords()
        me = 2 * x + y
        chips = _other_chips(x, y)

        def cp_(k, j, chip, src_slot, dst_slot):
            return _remote(p_refs[k].at[src_slot], b_refs[k].at[dst_slot], send_sems, recv_sems, 3 * k + j,
                           (chip[0], chip[1], c))

        sends = [cp_(k, j, chip, 2 * chip[0] + chip[1], me) for k in range(n) for j, chip in enumerate(chips)]
        for cp in sends:
            cp.start()
        for k in range(n):
            for j, chip in enumerate(chips):
                owner = 2 * chip[0] + chip[1]
                cp_(k, j, chip, owner, owner).wait_recv()
        for cp in sends:
            cp.wait_send()

    return pl.pallas_call(
        body, name="rs_chip_exchange", out_shape=[jax.ShapeDtypeStruct(p.shape, p.dtype) for p in Ps],
        in_specs=[HBM_SPEC] * n, out_specs=[HBM_SPEC] * n,
        scratch_shapes=[pltpu.SemaphoreType.DMA((3 * n,)), pltpu.SemaphoreType.DMA((3 * n,))],
    )(*Ps)


def _rs_sum_chips(P, B, me, name):
    _, R, C = P.shape
    tr = _tile(R, 256, 16)

    def body(me_ref, p_ref, b1_ref, b2_ref, b3_ref, o_ref):
        o_ref[...] = ((p_ref[0].astype(F32) + b1_ref[0].astype(F32)) + b2_ref[0].astype(F32)) + b3_ref[0].astype(F32)

    slot = lambda d: pl.BlockSpec((1, tr, C), lambda r, me_ref: ((me_ref[0] + d) % N_CHIPS, r, 0))
    grid_spec = pltpu.PrefetchScalarGridSpec(
        num_scalar_prefetch=1, grid=(R // tr,), in_specs=[slot(0), slot(1), slot(2), slot(3)],
        out_specs=pl.BlockSpec((tr, C), lambda r, me_ref: (r, 0)))
    return pl.pallas_call(
        body, name=name, grid_spec=grid_spec, out_shape=jax.ShapeDtypeStruct((R, C), F32),
        compiler_params=_cp("parallel"),
    )(jnp.reshape(me, (1,)).astype(jnp.int32), P, B, B, B)


def _sum_slots(B, name):
    S, R, C = B.shape
    tr = _tile(R, 256, 16)

    def body(b_ref, o_ref):
        acc = b_ref[0].astype(F32)
        for i in range(1, S):
            acc = acc + b_ref[i].astype(F32)
        o_ref[...] = acc

    return pl.pallas_call(
        body, name=name, grid=(R // tr,), in_specs=[pl.BlockSpec((S, tr, C), lambda r: (0, r, 0))],
        out_specs=pl.BlockSpec((tr, C), lambda r: (r, 0)), out_shape=jax.ShapeDtypeStruct((R, C), F32),
        compiler_params=_cp("parallel"),
    )(B)


def _rs_pair_swap(Rs):
    n = len(Rs)

    def body(*refs):
        r_refs, o_refs = refs[:n], refs[n:2 * n]
        send_sems, recv_sems = refs[2 * n:]
        x, y, c = _coords()
        cps = [_remote(r_refs[k], o_refs[k], send_sems, recv_sems, k, (x, y, 1 - c)) for k in range(n)]
        for cp in cps:
            cp.start()
        for cp in cps:
            cp.wait()

    return pl.pallas_call(
        body, name="rs_pair_swap", out_shape=[jax.ShapeDtypeStruct(r.shape, r.dtype) for r in Rs],
        in_specs=[HBM_SPEC] * n, out_specs=[HBM_SPEC] * n,
        scratch_shapes=[pltpu.SemaphoreType.DMA((n,)), pltpu.SemaphoreType.DMA((n,))],
    )(*Rs)


def _ag8(v):
    R = v.shape[0]

    def body(v_ref, out_ref, send_sems, recv_sems, local_sem):
        x, y, c = _coords()
        me, sib = (x, y, c), (x, y, 1 - c)
        chips = _other_chips(x, y)

        def slot(p):
            return out_ref.at[4 * p[0] + 2 * p[1] + p[2]]

        def copy(k, block, to, src=None):
            return _remote(slot(block) if src is None else src, slot(block), send_sems, recv_sems, k, to)

        mine = pltpu.make_async_copy(v_ref, slot(me), local_sem)
        mine.start()
        first = [copy(0, me, sib, src=v_ref)]
        first += [copy(1 + j, me, (chip[0], chip[1], c), src=v_ref) for j, chip in enumerate(chips)]
        for cp in first:
            cp.start()
        passed = [copy(4 + j, (chip[0], chip[1], c), sib) for j, chip in enumerate(chips)]
        for j, chip in enumerate(chips):
            copy(1 + j, (chip[0], chip[1], c), me).wait_recv()
            passed[j].start()
        copy(0, sib, me).wait_recv()
        for j, chip in enumerate(chips):
            copy(4 + j, (chip[0], chip[1], 1 - c), me).wait_recv()
        for cp in first + passed:
            cp.wait_send()
        mine.wait()

    return pl.pallas_call(
        body, name="ag8_small", out_shape=jax.ShapeDtypeStruct((8, R, 128), v.dtype),
        in_specs=[pl.BlockSpec(memory_space=pltpu.VMEM)], out_specs=pl.BlockSpec(memory_space=pltpu.VMEM),
        scratch_shapes=[pltpu.SemaphoreType.DMA((7,)), pltpu.SemaphoreType.DMA((7,)), pltpu.SemaphoreType.DMA],
        compiler_params=pltpu.CompilerParams(vmem_limit_bytes=VMEM_LIMIT),
    )(v)


def _adamw(w, g, m, v, name):
    L, R, C = w.shape
    tr = _tile(R, 128, 8)

    def body(w_ref, g_ref, m_ref, v_ref, d_ref, mo_ref, vo_ref):
        gv = g_ref[...]
        m2 = ADAM_B1 * m_ref[...] + (1.0 - ADAM_B1) * gv
        v2 = ADAM_B2 * v_ref[...] + (1.0 - ADAM_B2) * jnp.square(gv)
        m_hat = m2 / (1.0 - ADAM_B1 ** ADAM_STEP)
        v_hat = v2 / (1.0 - ADAM_B2 ** ADAM_STEP)
        d_ref[...] = -ADAM_LR * (m_hat / (jnp.sqrt(v_hat) + ADAM_EPS) + ADAM_WD * w_ref[...])
        mo_ref[...] = m2
        vo_ref[...] = v2

    blk = pl.BlockSpec((1, tr, C), lambda l, r: (l, r, 0))
    return pl.pallas_call(
        body, name=name, grid=(L, R // tr), in_specs=[blk] * 4, out_specs=[blk] * 3,
        out_shape=[jax.ShapeDtypeStruct(w.shape, F32)] * 3, compiler_params=_cp("parallel", "parallel"),
    )(w, g, m, v)


def _adamw_layers(w, g_mine, g_other, c, m, v, name):
    L, R, C = w.shape
    tr = _tile(R, 128, 8)

    def body(c_ref, w_ref, gm_ref, go_ref, m_ref, v_ref, g_ref, d_ref, mo_ref, vo_ref):
        gv = jnp.where(pl.program_id(0) == c_ref[0], gm_ref[...], go_ref[...])[None]
        g_ref[...] = gv
        m2 = ADAM_B1 * m_ref[...] + (1.0 - ADAM_B1) * gv
        v2 = ADAM_B2 * v_ref[...] + (1.0 - ADAM_B2) * jnp.square(gv)
        m_hat = m2 / (1.0 - ADAM_B1 ** ADAM_STEP)
        v_hat = v2 / (1.0 - ADAM_B2 ** ADAM_STEP)
        d_ref[...] = -ADAM_LR * (m_hat / (jnp.sqrt(v_hat) + ADAM_EPS) + ADAM_WD * w_ref[...])
        mo_ref[...] = m2
        vo_ref[...] = v2

    blk = pl.BlockSpec((1, tr, C), lambda l, r, c_ref: (l, r, 0))
    gblk = pl.BlockSpec((tr, C), lambda l, r, c_ref: (r, 0))
    grid_spec = pltpu.PrefetchScalarGridSpec(
        num_scalar_prefetch=1, grid=(L, R // tr), in_specs=[blk, gblk, gblk, blk, blk], out_specs=[blk] * 4)
    return pl.pallas_call(
        body, name=name, grid_spec=grid_spec, out_shape=[jax.ShapeDtypeStruct(w.shape, F32)] * 4,
        compiler_params=_cp("parallel", "parallel"),
    )(jnp.reshape(c, (1,)).astype(jnp.int32), w, g_mine, g_other, m, v)


BIG = ("w_in", "w_branch_a", "w_branch_b", "w_out", "ffn_w_gate", "ffn_w_up", "ffn_w_down")
ROW_SHARDED = ("w_out", "ffn_w_down")
SMALL = ("norm1_g", "dn_conv_w", "dn_a_log", "dn_dt_bias", "dn_onorm_g", "sg_ln_g", "sg_ln_b", "sg_w", "sg_b",
         "norm2_g", "ffn_conv_w", "ffn_conv_b", "final_norm_g")
SMALL_SHARDED = ("dn_conv_w", "ffn_conv_w")


def _pack_rows(arrs, mult):
    flat = jnp.concatenate([jnp.reshape(a, (-1,)) for a in arrs])
    n = flat.shape[0]
    rows = -(-n // (128 * mult)) * mult
    return jnp.reshape(jnp.pad(flat, (0, rows * 128 - n)), (rows, 128))


def _unpack(flat2d, shapes):
    flat = jnp.reshape(flat2d, (-1,))
    out, off = [], 0
    for shp in shapes:
        n = math.prod(shp)
        out.append(jnp.reshape(flat[off:off + n], shp))
        off += n
    return out


def _shards_to_full(a, row_sharded):
    if row_sharded:
        a = jnp.moveaxis(a, 0, 1)
        return jnp.reshape(a, (a.shape[0], a.shape[1] * a.shape[2], a.shape[3]))
    a = jnp.moveaxis(a, 0, 2)
    return jnp.reshape(a, (a.shape[0], a.shape[1], a.shape[2] * a.shape[3]))


def _full_to_shards(a, row_sharded):
    L, R, C = a.shape
    if row_sharded:
        return jnp.moveaxis(jnp.reshape(a, (L, N_CHIPS, R // N_CHIPS, C)), 1, 0)
    return jnp.moveaxis(jnp.reshape(a, (L, R, N_CHIPS, C // N_CHIPS)), 2, 0)


def kernel(x, norm1_g, w_in, dn_conv_w, dn_a_log, dn_dt_bias, dn_onorm_g, sg_ln_g, sg_ln_b, sg_w, sg_b, w_branch_a, w_branch_b, w_out, norm2_g, ffn_w_gate, ffn_w_up, ffn_conv_w, ffn_conv_b, ffn_w_down, final_norm_g, loss_target, m_norm1_g, m_w_in, m_dn_conv_w, m_dn_a_log, m_dn_dt_bias, m_dn_onorm_g, m_sg_ln_g, m_sg_ln_b, m_sg_w, m_sg_b, m_w_branch_a, m_w_branch_b, m_w_out, m_norm2_g, m_ffn_w_gate, m_ffn_w_up, m_ffn_conv_w, m_ffn_conv_b, m_ffn_w_down, m_final_norm_g, v_norm1_g, v_w_in, v_dn_conv_w, v_dn_a_log, v_dn_dt_bias, v_dn_onorm_g, v_sg_ln_g, v_sg_ln_b, v_sg_w, v_sg_b, v_w_branch_a, v_w_branch_b, v_w_out, v_norm2_g, v_ffn_w_gate, v_ffn_w_up, v_ffn_conv_w, v_ffn_conv_b, v_ffn_w_down, v_final_norm_g):
    W = dict(norm1_g=norm1_g, w_in=w_in, dn_conv_w=dn_conv_w, dn_a_log=dn_a_log, dn_dt_bias=dn_dt_bias,
             dn_onorm_g=dn_onorm_g, sg_ln_g=sg_ln_g, sg_ln_b=sg_ln_b, sg_w=sg_w, sg_b=sg_b, w_branch_a=w_branch_a,
             w_branch_b=w_branch_b, w_out=w_out, norm2_g=norm2_g, ffn_w_gate=ffn_w_gate, ffn_w_up=ffn_w_up,
             ffn_conv_w=ffn_conv_w, ffn_conv_b=ffn_conv_b, ffn_w_down=ffn_w_down, final_norm_g=final_norm_g)
    M = dict(norm1_g=m_norm1_g, w_in=m_w_in, dn_conv_w=m_dn_conv_w, dn_a_log=m_dn_a_log, dn_dt_bias=m_dn_dt_bias,
             dn_onorm_g=m_dn_onorm_g, sg_ln_g=m_sg_ln_g, sg_ln_b=m_sg_ln_b, sg_w=m_sg_w, sg_b=m_sg_b,
             w_branch_a=m_w_branch_a, w_branch_b=m_w_branch_b, w_out=m_w_out, norm2_g=m_norm2_g,
             ffn_w_gate=m_ffn_w_gate, ffn_w_up=m_ffn_w_up, ffn_conv_w=m_ffn_conv_w, ffn_conv_b=m_ffn_conv_b,
             ffn_w_down=m_ffn_w_down, final_norm_g=m_final_norm_g)
    V = dict(norm1_g=v_norm1_g, w_in=v_w_in, dn_conv_w=v_dn_conv_w, dn_a_log=v_dn_a_log, dn_dt_bias=v_dn_dt_bias,
             dn_onorm_g=v_dn_onorm_g, sg_ln_g=v_sg_ln_g, sg_ln_b=v_sg_ln_b, sg_w=v_sg_w, sg_b=v_sg_b,
             w_branch_a=v_w_branch_a, w_branch_b=v_w_branch_b, w_out=v_w_out, norm2_g=v_norm2_g,
             ffn_w_gate=v_ffn_w_gate, ffn_w_up=v_ffn_w_up, ffn_conv_w=v_ffn_conv_w, ffn_conv_b=v_ffn_conv_b,
             ffn_w_down=v_ffn_w_down, final_norm_g=v_final_norm_g)
    cx, cy, cc = _coords()
    chip = 2 * cx + cy
    L = w_in.shape[0]

    D = w_in.shape[1]
    cs_in = w_in.shape[2]
    rp_in = -(-cs_in // 128) * 128

    def shard_for_gather(n):
        if n == "w_in":
            return jnp.pad(jnp.swapaxes(W[n], 1, 2).astype(BF16), ((0, 0), (0, rp_in - cs_in), (0, 0)))
        return W[n].astype(BF16)

    mine = [shard_for_gather(n) for n in BIG]
    gathered = {n: lax.dynamic_update_slice_in_dim(a, s[None], chip, axis=0)
                for n, a, s in zip(BIG, _ag_layers(mine), mine)}
    full = {n: _shards_to_full(gathered[n], n in ROW_SHARDED) for n in BIG if n != "w_in"}
    full["w_in_t"] = jnp.reshape(jnp.moveaxis(gathered["w_in"][:, :, :cs_in], 0, 1), (L, N_CHIPS * cs_in, D))
    taps = _ag8(_pack_rows([W[n] for n in SMALL_SHARDED], 16))
    tap_shards = [_unpack(taps[2 * i], [W[n].shape for n in SMALL_SHARDED]) for i in range(N_CHIPS)]
    for k, n in enumerate(SMALL_SHARDED):
        full[n] = jnp.concatenate([tap_shards[i][k] for i in range(N_CHIPS)], axis=-1)

    layers = []
    for l in range(L):
        p = {n: (full[n][l] if n in full else W[n][l]) for n in W if n not in ("final_norm_g", "w_in")}
        p["w_in_t"] = full["w_in_t"][l]
        layers.append(_prep_layer(p))
    loss, dx, grads, dgf = _local_step(x[0], loss_target[0], layers, final_norm_g[None])

    def grad_shards(n):
        if n == "w_in":
            gt = jnp.stack([_w_in_unpad(g["w_in_t"]) for g in grads])
            gt = jnp.pad(jnp.reshape(gt, (L, N_CHIPS, cs_in, D)), ((0, 0), (0, 0), (0, rp_in - cs_in), (0, 0)))
            return jnp.moveaxis(gt, 1, 0)
        return _full_to_shards(jnp.stack([g[n] for g in grads]), n in ROW_SHARDED)

    Gs = [grad_shards(n) for n in BIG]
    B1s = _rs_pair_exchange(Gs)
    Ps = [_rs_add_pair(g, b, cc, "rs_add_pair_" + n) for n, g, b in zip(BIG, Gs, B1s)]
    B2s = _rs_chip_exchange(Ps)
    g_mine = {n: _rs_sum_chips(p, b, chip, "rs_sum_chips_" + n) for n, p, b in zip(BIG, Ps, B2s)}
    g_other = dict(zip(BIG, _rs_pair_swap([g_mine[n] for n in BIG])))

    small = {n: jnp.stack([g[n] for g in grads]) for n in SMALL if n != "final_norm_g"}
    small["final_norm_g"] = dgf
    shapes = [(L,) + tuple(full[n].shape[1:]) if n in SMALL_SHARDED else W[n].shape for n in SMALL] + [(1,)]
    sflat = _pack_rows([small[n] for n in SMALL] + [jnp.reshape(loss, (1,))], 16)
    sred = _unpack(_sum_slots(_ag8(sflat), "sum_small"), shapes)
    g_small = dict(zip(SMALL, sred[:-1]))
    loss_total = sred[-1][0]
    for n in SMALL_SHARDED:
        cs = W[n].shape[-1]
        g_small[n] = lax.dynamic_slice_in_dim(g_small[n], chip * cs, cs, axis=-1)

    g_big, delta, new_m, new_v = {}, {}, {}, {}
    for n in BIG:
        if n == "w_in":
            both = jnp.stack([jnp.where(cc == 0, g_mine[n], g_other[n]), jnp.where(cc == 0, g_other[n], g_mine[n])])
            g_big[n] = jnp.swapaxes(both[:, :cs_in], 1, 2)
            delta[n], new_m[n], new_v[n] = _adamw(W[n], g_big[n], M[n], V[n], "adamw_" + n)
        else:
            g_big[n], delta[n], new_m[n], new_v[n] = _adamw_layers(W[n], g_mine[n], g_other[n], cc, M[n], V[n],
                                                                   "adamw_" + n)
    s_shapes = [W[n].shape for n in SMALL]
    packed = [_pack_rows([d[n] for n in SMALL], 8) for d in (W, g_small, M, V)]
    outs = _adamw(*[a[None] for a in packed], "adamw_small")
    for d, o in zip((delta, new_m, new_v), outs):
        d.update(zip(SMALL, _unpack(o[0], s_shapes)))

    names = list(W)
    grad_w = {**g_big, **g_small}
    return (loss_total, dx[None], *[grad_w[n] for n in names], *[delta[n] for n in names],
            *[new_m[n] for n in names], *[new_v[n] for n in names])
```

```python
import functools
import math

import jax
import jax.numpy as jnp
from jax import lax
from jax.experimental import pallas as pl
from jax.experimental.pallas import tpu as pltpu

F32 = jnp.float32
BF16 = jnp.bfloat16
MESH = pl.DeviceIdType.MESH

EPS = 1e-6
H = 8
DH = 128
WD = H * DH
DNC = 64
SGC = 128
DN_K = 4
FF_K = 3
DEPTH = 2
N_CHIPS = 4

ADAM_LR = 0.001
ADAM_B1 = 0.9
ADAM_B2 = 0.999
ADAM_EPS = 1e-08
ADAM_WD = 0.01
ADAM_STEP = 10

VMEM_LIMIT = 56 * 1024 * 1024

NN = (((1,), (0,)), ((), ()))
NT = (((1,), (1,)), ((), ()))
TN = (((0,), (0,)), ((), ()))

OQ, OZ, OU, OV, OGA = 0, 3 * WD, 4 * WD, 5 * WD, 6 * WD


def _cp(*sem):
    return pltpu.CompilerParams(dimension_semantics=sem or None, vmem_limit_bytes=VMEM_LIMIT)


def _tile(dim, pref, unit=128):
    if dim <= pref:
        return dim
    t = (pref // unit) * unit
    while t >= unit:
        if dim % t == 0:
            return t
        t -= unit
    return dim


def _hdot(a, b, dn=NN):
    return lax.dot_general(a, b, dn, precision=lax.Precision.HIGHEST, preferred_element_type=F32)


def _bdot(a, b, dn=NN):
    return lax.dot_general(a.astype(BF16), b.astype(BF16), dn, preferred_element_type=F32)


def _lsum(x):
    return jnp.sum(x, axis=1, keepdims=True)


def _sig(x):
    return jax.nn.sigmoid(x)


def _dsilu(x):
    s = _sig(x)
    return s * (1.0 + x * (1.0 - s))


def _erf(x):
    a = jnp.abs(x)
    t = 1.0 / (1.0 + 0.3275911 * a)
    poly = t * (0.254829592 + t * (-0.284496736 + t * (1.421413741 + t * (-1.453152027 + t * 1.061405429))))
    r = 1.0 - poly * jnp.exp(-a * a)
    return jnp.where(x < 0, -r, r)


def _gelu(x):
    return 0.5 * x * (1.0 + _erf(x * (2.0 ** -0.5)))


def _dgelu(x):
    cdf = 0.5 * (1.0 + _erf(x * (2.0 ** -0.5)))
    pdf = jnp.exp(-0.5 * x * x) * (1.0 / math.sqrt(2.0 * math.pi))
    return cdf + x * pdf


def _shift_down(x, k):
    if k == 0:
        return x
    rows = lax.broadcasted_iota(jnp.int32, x.shape, 0)
    return jnp.where(rows >= k, pltpu.roll(x, k, 0), 0.0)


def _shift_up(x, k):
    if k == 0:
        return x
    n = x.shape[0]
    rows = lax.broadcasted_iota(jnp.int32, x.shape, 0)
    return jnp.where(rows < n - k, pltpu.roll(x, n - k, 0), 0.0)


def _mm(a, b, mode, out_dtype, add=None, name="mm"):
    if mode == "tn":
        K, M = a.shape
    else:
        M, K = a.shape
    N = b.shape[0] if mode == "nt" else b.shape[1]
    tm, tn, tk = _tile(M, 1152), _tile(N, 1536), _tile(K, 2048)
    nk = K // tk
    dn = {"nn": NN, "nt": NT, "tn": TN}[mode]

    def body(a_ref, b_ref, *rest):
        add_ref = rest[0] if add is not None else None
        o_ref = rest[-2] if nk > 1 else rest[-1]

        def finish(r):
            if add is not None:
                r = r + add_ref[...]
            o_ref[...] = r.astype(o_ref.dtype)

        part = lax.dot_general(a_ref[...], b_ref[...], dn, preferred_element_type=F32)
        if nk == 1:
            finish(part)
            return
        acc_ref = rest[-1]
        k = pl.program_id(2)

        @pl.when(k == 0)
        def _():
            acc_ref[...] = part

        @pl.when(k > 0)
        def _():
            acc_ref[...] += part

        @pl.when(k == nk - 1)
        def _():
            finish(acc_ref[...])

    a_spec = (pl.BlockSpec((tk, tm), lambda i, j, k: (k, i)) if mode == "tn"
              else pl.BlockSpec((tm, tk), lambda i, j, k: (i, k)))
    b_spec = (pl.BlockSpec((tn, tk), lambda i, j, k: (j, k)) if mode == "nt"
              else pl.BlockSpec((tk, tn), lambda i, j, k: (k, j)))
    o_spec = pl.BlockSpec((tm, tn), lambda i, j, k: (i, j))
    in_specs = [a_spec, b_spec] + ([o_spec] if add is not None else [])
    args = (a, b) + ((add,) if add is not None else ())
    return pl.pallas_call(
        body, name=name, grid=(M // tm, N // tn, nk), in_specs=in_specs, out_specs=o_spec,
        out_shape=jax.ShapeDtypeStruct((M, N), out_dtype),
        scratch_shapes=[pltpu.VMEM((tm, tn), F32)] if nk > 1 else [],
        compiler_params=_cp("parallel", "parallel", "arbitrary"),
    )(*args)


def _rms_fwd(x, g, name):
    T, D = x.shape
    tt = _tile(T, 256, 16)

    def body(x_ref, g_ref, o_ref):
        xv = x_ref[...]
        r = lax.rsqrt(jnp.mean(xv * xv, axis=-1, keepdims=True) + EPS)
        o_ref[...] = (xv * r * g_ref[...]).astype(o_ref.dtype)

    return pl.pallas_call(
        body, name=name, grid=(T // tt,),
        in_specs=[pl.BlockSpec((tt, D), lambda i: (i, 0)), pl.BlockSpec((1, D), lambda i: (0, 0))],
        out_specs=pl.BlockSpec((tt, D), lambda i: (i, 0)),
        out_shape=jax.ShapeDtypeStruct((T, D), BF16), compiler_params=_cp("parallel"),
    )(x, g)


def _rms_bwd(x, g, dh, dres, name):
    T, D = x.shape
    tt = _tile(T, 256, 16)

    def body(x_ref, g_ref, dh_ref, dres_ref, dx_ref, dg_ref):
        @pl.when(pl.program_id(0) == 0)
        def _():
            dg_ref[...] = jnp.zeros_like(dg_ref)

        xv = x_ref[...]
        r = lax.rsqrt(jnp.mean(xv * xv, axis=-1, keepdims=True) + EPS)
        xh = xv * r
        dh_v = dh_ref[...]
        dy = dh_v * g_ref[...]
        dx_ref[...] = dres_ref[...] + r * (dy - xh * jnp.mean(dy * xh, axis=-1, keepdims=True))
        dg_ref[...] += jnp.sum(dh_v * xh, axis=0, keepdims=True)

    row = pl.BlockSpec((tt, D), lambda i: (i, 0))
    vec = pl.BlockSpec((1, D), lambda i: (0, 0))
    return pl.pallas_call(
        body, name=name, grid=(T // tt,), in_specs=[row, vec, row, row], out_specs=[row, vec],
        out_shape=[jax.ShapeDtypeStruct((T, D), F32), jax.ShapeDtypeStruct((1, D), F32)],
        compiler_params=_cp("arbitrary"),
    )(x, g, dh, dres)


def _loss_head(x, g, tgt, name="loss_head"):
    T, D = x.shape
    tt = _tile(T, 256, 16)

    def body(x_ref, g_ref, t_ref, dx_ref, dg_ref, loss_ref):
        @pl.when(pl.program_id(0) == 0)
        def _():
            dg_ref[...] = jnp.zeros_like(dg_ref)
            loss_ref[...] = jnp.zeros_like(loss_ref)

        xv = x_ref[...]
        r = lax.rsqrt(jnp.mean(xv * xv, axis=-1, keepdims=True) + EPS)
        xh = xv * r
        err = xh * g_ref[...] - t_ref[...]
        part = 0.5 * jnp.sum(jnp.mean(err * err, axis=-1, keepdims=True), axis=0, keepdims=True)
        loss_ref[...] += jnp.broadcast_to(part, loss_ref.shape)
        dy = err * (1.0 / D)
        dg_ref[...] += jnp.sum(dy * xh, axis=0, keepdims=True)
        dyh = dy * g_ref[...]
        dx_ref[...] = r * (dyh - xh * jnp.mean(dyh * xh, axis=-1, keepdims=True))

    row = pl.BlockSpec((tt, D), lambda i: (i, 0))
    vec = pl.BlockSpec((1, D), lambda i: (0, 0))
    return pl.pallas_call(
        body, name=name, grid=(T // tt,), in_specs=[row, vec, row],
        out_specs=[row, vec, pl.BlockSpec((1, 128), lambda i: (0, 0))],
        out_shape=[jax.ShapeDtypeStruct((T, D), F32), jax.ShapeDtypeStruct((1, D), F32),
                   jax.ShapeDtypeStruct((1, 128), F32)],
        compiler_params=_cp("arbitrary"),
    )(x, g, tgt)


def _ba_fwd(proj, alog, dtb, oba, name="dn_ba_fwd"):
    T = proj.shape[0]
    tt = _tile(T, 512, 8)

    def body(p_ref, al_ref, dt_ref, o_ref):
        raw = p_ref[...]
        lane = lax.broadcasted_iota(jnp.int32, raw.shape, 1)
        z = raw + dt_ref[...]
        sp = jnp.maximum(z, 0.0) + jnp.log(1.0 + jnp.exp(-jnp.abs(z)))
        gl = -jnp.exp(al_ref[...]) * sp
        o_ref[...] = jnp.where(lane < H, _sig(raw), jnp.where(lane < 2 * H, gl, 0.0))

    vec = pl.BlockSpec((1, 128), lambda i: (0, 0))
    return pl.pallas_call(
        body, name=name, grid=(T // tt,),
        in_specs=[pl.BlockSpec((tt, 128), lambda i: (i, oba // 128)), vec, vec],
        out_specs=pl.BlockSpec((tt, 128), lambda i: (i, 0)),
        out_shape=jax.ShapeDtypeStruct((T, 128), F32), compiler_params=_cp("parallel"),
    )(proj, alog, dtb)


def _ba_bwd(proj, alog, dtb, dbg, oba, name="dn_ba_bwd"):
    T = proj.shape[0]
    tt = _tile(T, 512, 16)

    def body(p_ref, al_ref, dt_ref, d_ref, o_ref, dal_ref, ddt_ref):
        @pl.when(pl.program_id(0) == 0)
        def _():
            dal_ref[...] = jnp.zeros_like(dal_ref)
            ddt_ref[...] = jnp.zeros_like(ddt_ref)

        raw = p_ref[...]
        d = d_ref[...]
        lane = lax.broadcasted_iota(jnp.int32, raw.shape, 1)
        z = raw + dt_ref[...]
        sp = jnp.maximum(z, 0.0) + jnp.log(1.0 + jnp.exp(-jnp.abs(z)))
        na = -jnp.exp(al_ref[...])
        is_g = jnp.logical_and(lane >= H, lane < 2 * H)
        b = _sig(raw)
        dz = jnp.where(is_g, d * na * _sig(z), 0.0)
        o_ref[...] = jnp.where(lane < H, d * b * (1.0 - b), dz).astype(o_ref.dtype)
        dal_ref[...] += jnp.sum(jnp.where(is_g, d * na * sp, 0.0), axis=0, keepdims=True)
        ddt_ref[...] += jnp.sum(dz, axis=0, keepdims=True)

    vec = pl.BlockSpec((1, 128), lambda i: (0, 0))
    return pl.pallas_call(
        body, name=name, grid=(T // tt,),
        in_specs=[pl.BlockSpec((tt, 128), lambda i: (i, oba // 128)), vec, vec,
                  pl.BlockSpec((tt, 128), lambda i: (i, 0))],
        out_specs=[pl.BlockSpec((tt, 128), lambda i: (i, 0)), vec, vec],
        out_shape=[jax.ShapeDtypeStruct((T, 128), BF16), jax.ShapeDtypeStruct((1, 128), F32),
                   jax.ShapeDtypeStruct((1, 128), F32)],
        compiler_params=_cp("arbitrary"),
    )(proj, alog, dtb, dbg)


def _dn_prep_fwd(proj, convw, name="dn_prep_fwd"):
    T = proj.shape[0]
    nblk = 3 * H

    def body(p_ref, w_ref, o_ref):
        j = pl.program_id(0)
        xv = p_ref[...]
        w = w_ref[...]
        c = xv * w[DN_K - 1:DN_K, :]
        for k in range(1, DN_K):
            c = c + _shift_down(xv, k) * w[DN_K - 1 - k:DN_K - k, :]
        s = c * _sig(c)
        r = lax.rsqrt(_lsum(s * s) + EPS)
        o_ref[...] = jnp.where(j < 2 * H, s * r, s)

    return pl.pallas_call(
        body, name=name, grid=(nblk,),
        in_specs=[pl.BlockSpec((T, DH), lambda j: (0, j)), pl.BlockSpec((DN_K, DH), lambda j: (0, j))],
        out_specs=pl.BlockSpec((T, DH), lambda j: (0, j)),
        out_shape=jax.ShapeDtypeStruct((T, 3 * WD), F32), compiler_params=_cp("parallel"),
    )(proj, convw)


def _dn_prep_bwd(proj, convw, dq, dk, dv, name="dn_prep_bwd"):
    T = proj.shape[0]
    nblk = 3 * H

    def body(p_ref, w_ref, dq_ref, dk_ref, dv_ref, dx_ref, dw_ref):
        j = pl.program_id(0)
        xv = p_ref[...]
        w = w_ref[...]
        shifted = [_shift_down(xv, k) for k in range(DN_K)]
        c = shifted[0] * w[DN_K - 1:DN_K, :]
        for k in range(1, DN_K):
            c = c + shifted[k] * w[DN_K - 1 - k:DN_K - k, :]
        s = c * _sig(c)
        r = lax.rsqrt(_lsum(s * s) + EPS)
        y = s * r
        dy = jnp.where(j < H, dq_ref[...], jnp.where(j < 2 * H, dk_ref[...], dv_ref[...]))
        ds = jnp.where(j < 2 * H, r * (dy - y * _lsum(dy * y)), dy)
        dc = ds * _dsilu(c)
        dx = dc * w[DN_K - 1:DN_K, :]
        for k in range(1, DN_K):
            dx = dx + _shift_up(dc, k) * w[DN_K - 1 - k:DN_K - k, :]
        dx_ref[...] = dx.astype(dx_ref.dtype)
        rows = [jnp.sum(dc * shifted[DN_K - 1 - t], axis=0, keepdims=True) for t in range(DN_K)]
        dw_ref[...] = jnp.concatenate(rows, axis=0)

    hb = lambda off: pl.BlockSpec((T, DH), lambda j: (0, jnp.maximum(jnp.minimum(j - off, H - 1), 0)))
    return pl.pallas_call(
        body, name=name, grid=(nblk,),
        in_specs=[pl.BlockSpec((T, DH), lambda j: (0, j)), pl.BlockSpec((DN_K, DH), lambda j: (0, j)),
                  hb(0), hb(H), hb(2 * H)],
        out_specs=[pl.BlockSpec((T, DH), lambda j: (0, j)), pl.BlockSpec((DN_K, DH), lambda j: (0, j))],
        out_shape=[jax.ShapeDtypeStruct((T, 3 * WD), BF16), jax.ShapeDtypeStruct((DN_K, 3 * WD), F32)],
        compiler_params=_cp("parallel"),
    )(proj, convw, dq, dk, dv)


DN_BLOCK = 8


def _split3(a):
    hi = a.astype(BF16)
    r1 = a - hi.astype(F32)
    mid = r1.astype(BF16)
    return hi, mid, (r1 - mid.astype(F32)).astype(BF16)


def _dot3(a, b, dn=NN):
    ah, al, _ = _split3(a)
    bh, bl, _ = _split3(b)
    d = lambda p, q: lax.dot_general(p, q, dn, preferred_element_type=F32)
    return d(ah, bh) + d(ah, bl) + d(al, bh)


def _mask_dot(m, b, dn=NN):
    mb = m.astype(BF16)
    d = lambda q: (lax.dot_general(mb, q, dn, preferred_element_type=F32) if dn != TN
                   else lax.dot_general(q, mb, dn, preferred_element_type=F32))
    b0, b1, b2 = _split3(b)
    return d(b0) + d(b1) + d(b2)


def _tri_inv(A):
    ri = lax.broadcasted_iota(jnp.int32, A.shape, 0)
    ci = lax.broadcasted_iota(jnp.int32, A.shape, 1)
    X = -A
    P = jnp.where(ri == ci, 1.0, 0.0) + X
    Y = X
    for _ in range(int(math.log2(DNC)) - 1):
        Y = _dot3(Y, Y)
        P = P + _dot3(P, Y)
    return P


def _dn_masks():
    ri = lax.broadcasted_iota(jnp.int32, (DNC, DNC), 0)
    ci = lax.broadcasted_iota(jnp.int32, (DNC, DNC), 1)
    return ri >= ci, ri > ci


def _dn_decays(bg):
    causal, strict = _dn_masks()
    gc_cols = _mask_dot(jnp.where(causal, 1.0, 0.0), bg)
    gc_rows = _mask_dot(jnp.where(strict, 0.0, 1.0), bg, TN)
    return gc_cols, gc_rows


def _dn_chunk(q, k, v, bg, gc_cols, gc_rows, h):
    causal, strict = _dn_masks()
    q = q * (DH ** -0.5)
    beta = bg[:, h:h + 1]
    gcol = gc_cols[:, H + h:H + h + 1]
    grow = gc_rows[H + h:H + h + 1, :]
    dec = jnp.where(causal, jnp.exp(jnp.where(causal, gcol - grow, 0.0)), 0.0)
    eg = jnp.exp(gcol)
    gl = gcol[DNC - 1:DNC, :]
    ek = jnp.exp(gl - gcol)
    egl = jnp.exp(gl)
    kb = k * beta
    vb = v * beta
    kbe = kb * eg
    A = jnp.where(strict, _bdot(kb, k, NT) * dec, 0.0)
    P = jnp.where(causal, _bdot(q, k, NT) * dec, 0.0)
    return dict(q=q, k=k, v=v, beta=beta, dec=dec, eg=eg, ek=ek, egl=egl, kb=kb, vb=vb, kbe=kbe, A=A, P=P,
                qd=q * eg, kd=k * ek, causal=causal, strict=strict)


def _dn_core_fwd(qkv, bg, gather=(), name="dn_core_fwd"):
    T = qkv.shape[0]
    n_chunks = T // DNC
    nb = _tile(n_chunks, DN_BLOCK, 1)
    tb = nb * DNC

    ng = len(gather)
    n_steps = n_chunks // nb

    def body(*refs):
        q_ref, k_ref, v_ref, bg_ref = refs[:4]
        o_ref, s_ref, tm_ref = refs[4 + ng:7 + ng]
        S_scr = refs[7 + 2 * ng]
        comm = (refs[4:4 + ng], refs[7 + ng:7 + 2 * ng]) + tuple(refs[8 + 2 * ng:])

        @pl.when(pl.program_id(0) == 0)
        def _():
            S_scr[...] = jnp.zeros_like(S_scr)
            if ng:
                _ag_start(*comm)

        def chunk(n, carry):
            rows = pl.ds(pl.multiple_of(n * DNC, DNC), DNC)
            bgc = bg_ref[rows, :]
            gc_cols, gc_rows = _dn_decays(bgc)
            for h in range(H):
                sl = slice(h * DH, (h + 1) * DH)
                c = _dn_chunk(q_ref[rows, sl], k_ref[rows, sl], v_ref[rows, sl], bgc, gc_cols, gc_rows, h)
                Tm = _tri_inv(c["A"])
                tm_ref[h, n] = Tm
                S = S_scr[h]
                s_ref[h, n] = S
                u = _bdot(Tm, c["vb"])
                w = _bdot(Tm, c["kbe"])
                vn = u - _bdot(w, S)
                o_ref[rows, sl] = _bdot(c["qd"], S) + _bdot(c["P"], vn)
                S_scr[h] = S * c["egl"] + _bdot(c["kd"], vn, TN)
            return carry

        lax.fori_loop(0, nb, chunk, 0)

        if ng:
            @pl.when(pl.program_id(0) == n_steps - 1)
            def _():
                _ag_finish(*comm)

    blk = lambda j: pl.BlockSpec((tb, WD), lambda i: (i, j))
    sp = _ag_specs(gather)
    outs = pl.pallas_call(
        body, name=name + ("_gather" if ng else ""), grid=(n_steps,),
        in_specs=[blk(0), blk(1), blk(2), pl.BlockSpec((tb, 128), lambda i: (i, 0))] + sp["specs"],
        out_specs=[blk(0), pl.BlockSpec((H, nb, DH, DH), lambda i: (0, i, 0, 0)),
                   pl.BlockSpec((H, nb, DNC, DNC), lambda i: (0, i, 0, 0))] + sp["specs"],
        out_shape=[jax.ShapeDtypeStruct((T, WD), F32), jax.ShapeDtypeStruct((H, n_chunks, DH, DH), F32),
                   jax.ShapeDtypeStruct((H, n_chunks, DNC, DNC), F32)] + sp["out_shape"],
        scratch_shapes=[pltpu.VMEM((H, DH, DH), F32)] + (sp["sems"] if ng else []),
        compiler_params=_cp("arbitrary"),
    )(qkv, qkv, qkv, bg, *gather)
    return outs[0], outs[1], outs[2], list(outs[3:])


def _dn_core_bwd(qkv, bg, s_all, tm_all, do, exchange=(), name="dn_core_bwd"):
    T = qkv.shape[0]
    n_chunks = T // DNC
    nb = _tile(n_chunks, DN_BLOCK, 1)
    tb = nb * DNC
    n_blocks = n_chunks // nb

    nx = len(exchange)

    def body(*refs):
        q_ref, k_ref, v_ref, bg_ref, s_ref, tm_ref, do_ref = refs[:7]
        dq_ref, dk_ref, dv_ref, dbg_ref = refs[7 + nx:11 + nx]
        dS_scr = refs[11 + 2 * nx]
        comm = (refs[7:7 + nx], refs[11 + nx:11 + 2 * nx]) + tuple(refs[12 + 2 * nx:])

        @pl.when(pl.program_id(0) == 0)
        def _():
            dS_scr[...] = jnp.zeros_like(dS_scr)
            if nx:
                _rsx_start(*comm)

        lane = lax.broadcasted_iota(jnp.int32, (DNC, 128), 1)
        row = lax.broadcasted_iota(jnp.int32, (DNC, 1), 0)
        ones = jnp.ones((DNC, DNC), F32)
        _, strict = _dn_masks()

        def chunk(i, carry):
            n = nb - 1 - i
            rows = pl.ds(pl.multiple_of(n * DNC, DNC), DNC)
            bgc = bg_ref[rows, :]
            gc_cols, gc_rows = _dn_decays(bgc)
            d_beta_all = jnp.zeros((DNC, 128), F32)
            d_gc_all = jnp.zeros((DNC, 128), F32)
            for h in range(H):
                sl = slice(h * DH, (h + 1) * DH)
                c = _dn_chunk(q_ref[rows, sl], k_ref[rows, sl], v_ref[rows, sl], bgc, gc_cols, gc_rows, h)
                q, k, v, beta = c["q"], c["k"], c["v"], c["beta"]
                dec, eg, ek, egl = c["dec"], c["eg"], c["ek"], c["egl"]
                kb, vb, kbe, A, P, qd, kd = c["kb"], c["vb"], c["kbe"], c["A"], c["P"], c["qd"], c["kd"]
                S = s_ref[h, n]
                Tm = tm_ref[h, n]
                u = _bdot(Tm, vb)
                w = _bdot(Tm, kbe)
                vn = u - _bdot(w, S)
                d_o = do_ref[rows, sl]
                dS1 = dS_scr[h]
                d_qd = _bdot(d_o, S, NT)
                dP = jnp.where(c["causal"], _bdot(d_o, vn, NT), 0.0)
                d_vn = _bdot(P, d_o, TN) + _bdot(kd, dS1)
                d_kd = _bdot(vn, dS1, NT)
                d_egl = jnp.sum(_lsum(dS1 * S), axis=0, keepdims=True)
                dS_scr[h] = dS1 * egl + _bdot(qd, d_o, TN) - _bdot(w, d_vn, TN)
                d_w = -_bdot(d_vn, S, NT)
                d_vb = _bdot(Tm, d_vn, TN)
                d_kbe = _bdot(Tm, d_w, TN)
                dA = jnp.where(c["strict"], -(_bdot(d_vb, u, NT) + _bdot(d_kbe, w, NT)), 0.0)
                dMA = dA * dec
                dMP = dP * dec
                d_kb = _bdot(dMA, k) + d_kbe * eg
                d_k = _bdot(dMA, kb, TN) + _bdot(dMP, q, TN) + d_kd * ek + d_kb * beta
                d_qs = _bdot(dMP, k) + d_qd * eg
                E = dA * A + dP * P
                col_sums = _mask_dot(ones, E, TN)[:, :1]
                t_kd = _lsum(d_kd * kd)
                d_gl = jnp.sum(t_kd, axis=0, keepdims=True) + d_egl * egl
                d_gc = (_lsum(E) - col_sums + _lsum(d_qd * qd) + _lsum(d_kbe * kbe) - t_kd
                        + jnp.where(row == DNC - 1, d_gl, 0.0))
                d_beta = _lsum(d_kb * k) + _lsum(d_vb * v)
                d_beta_all = d_beta_all + jnp.where(lane == h, d_beta, 0.0)
                d_gc_all = d_gc_all + jnp.where(lane == h + H, d_gc, 0.0)
                dq_ref[rows, sl] = d_qs * (DH ** -0.5)
                dk_ref[rows, sl] = d_k
                dv_ref[rows, sl] = d_vb * beta
            dbg_ref[rows, :] = d_beta_all + _mask_dot(jnp.where(strict, 0.0, 1.0), d_gc_all)
            return carry

        lax.fori_loop(0, nb, chunk, 0)

        if nx:
            @pl.when(pl.program_id(0) == n_blocks - 1)
            def _():
                _rsx_finish(*comm)

    blk = lambda j: pl.BlockSpec((tb, WD), lambda i: (n_blocks - 1 - i, j))
    small = pl.BlockSpec((tb, 128), lambda i: (n_blocks - 1 - i, 0))
    sp = _rsx_specs(exchange)
    outs = pl.pallas_call(
        body, name=name + ("_exchange" if nx else ""), grid=(n_blocks,),
        in_specs=[blk(0), blk(1), blk(2), small,
                  pl.BlockSpec((H, nb, DH, DH), lambda i: (0, n_blocks - 1 - i, 0, 0)),
                  pl.BlockSpec((H, nb, DNC, DNC), lambda i: (0, n_blocks - 1 - i, 0, 0)), blk(0)] + sp["specs"],
        out_specs=[blk(0), blk(0), blk(0), small] + sp["specs"],
        out_shape=[jax.ShapeDtypeStruct((T, WD), F32)] * 3 + [jax.ShapeDtypeStruct((T, 128), F32)] + sp["out_shape"],
        scratch_shapes=[pltpu.VMEM((H, DH, DH), F32)] + (sp["sems"] if nx else []),
        compiler_params=_cp("arbitrary"),
    )(qkv, qkv, qkv, bg, s_all, tm_all, do, *exchange)
    return outs[0], outs[1], outs[2], outs[3], list(outs[4:])


def _dn_post_fwd(o, proj, gon, name="dn_post_fwd"):
    T = o.shape[0]
    tt = _tile(T, 256, 16)

    def body(o_ref, z_ref, g_ref, y_ref):
        for hh in range(H):
            sl = slice(hh * DH, (hh + 1) * DH)
            ov = o_ref[:, sl]
            zv = z_ref[:, sl]
            r = lax.rsqrt(jnp.mean(ov * ov, axis=-1, keepdims=True) + EPS)
            y_ref[:, sl] = (ov * r * g_ref[...] * (zv * _sig(zv))).astype(y_ref.dtype)

    return pl.pallas_call(
        body, name=name, grid=(T // tt,),
        in_specs=[pl.BlockSpec((tt, WD), lambda i: (i, 0)), pl.BlockSpec((tt, WD), lambda i: (i, OZ // WD)),
                  pl.BlockSpec((1, DH), lambda i: (0, 0))],
        out_specs=pl.BlockSpec((tt, WD), lambda i: (i, 0)),
        out_shape=jax.ShapeDtypeStruct((T, WD), BF16), compiler_params=_cp("parallel"),
    )(o, proj, gon)


def _dn_post_bwd(o, proj, gon, dy, name="dn_post_bwd"):
    T = o.shape[0]
    tt = _tile(T, 256, 16)

    def body(o_ref, z_ref, g_ref, dy_ref, do_ref, dz_ref, dg_ref):
        @pl.when(pl.program_id(0) == 0)
        def _():
            dg_ref[...] = jnp.zeros_like(dg_ref)

        acc = jnp.zeros((1, DH), F32)
        for hh in range(H):
            sl = slice(hh * DH, (hh + 1) * DH)
            ov = o_ref[:, sl]
            zv = z_ref[:, sl]
            dyv = dy_ref[:, sl]
            r = lax.rsqrt(jnp.mean(ov * ov, axis=-1, keepdims=True) + EPS)
            oh = ov * r
            nrm = oh * g_ref[...]
            dn = dyv * (zv * _sig(zv))
            dz_ref[:, sl] = (dyv * nrm * _dsilu(zv)).astype(dz_ref.dtype)
            doh = dn * g_ref[...]
            do_ref[:, sl] = r * (doh - oh * jnp.mean(doh * oh, axis=-1, keepdims=True))
            acc = acc + jnp.sum(dn * oh, axis=0, keepdims=True)
        dg_ref[...] += acc

    row = pl.BlockSpec((tt, WD), lambda i: (i, 0))
    vec = pl.BlockSpec((1, DH), lambda i: (0, 0))
    return pl.pallas_call(
        body, name=name, grid=(T // tt,),
        in_specs=[row, pl.BlockSpec((tt, WD), lambda i: (i, OZ // WD)), vec, row],
        out_specs=[row, row, vec],
        out_shape=[jax.ShapeDtypeStruct((T, WD), F32), jax.ShapeDtypeStruct((T, WD), BF16),
                   jax.ShapeDtypeStruct((1, DH), F32)],
        compiler_params=_cp("arbitrary"),
    )(o, proj, gon, dy)


def _sg_common(u_ref, v_ref, lng_ref, lnb_ref):
    ur = u_ref[...]
    vr = v_ref[...]
    vgel = _gelu(vr)
    mu = jnp.mean(vgel, axis=-1, keepdims=True)
    xc = vgel - mu
    rs = lax.rsqrt(jnp.mean(xc * xc, axis=-1, keepdims=True) + EPS)
    xh = xc * rs
    vg = xh * lng_ref[...] + lnb_ref[...]
    return ur, vr, rs, xh, vg


def _sg_fwd(proj, lng, lnb, sgw, sgbt, name="sg_fwd"):
    T = proj.shape[0]

    def body(u_ref, v_ref, lng_ref, lnb_ref, w_ref, bt_ref, y_ref):
        ur, _, _, _, vg = _sg_common(u_ref, v_ref, lng_ref, lnb_ref)
        ri = lax.broadcasted_iota(jnp.int32, (SGC, SGC), 0)
        ci = lax.broadcasted_iota(jnp.int32, (SGC, SGC), 1)
        ug = _gelu(ur)
        for g in range(H):
            sl = slice(g * DH, (g + 1) * DH)
            ws = jnp.where(ri >= ci, w_ref[g], 0.0)
            mixed = _bdot(ws, vg[:, sl]) + bt_ref[:, g:g + 1]
            y_ref[:, sl] = (ug[:, sl] * mixed).astype(y_ref.dtype)

    vec = pl.BlockSpec((1, WD), lambda i: (0, 0))
    return pl.pallas_call(
        body, name=name, grid=(T // SGC,),
        in_specs=[pl.BlockSpec((SGC, WD), lambda i: (i, OU // WD)), pl.BlockSpec((SGC, WD), lambda i: (i, OV // WD)),
                  vec, vec, pl.BlockSpec((H, SGC, SGC), lambda i: (0, 0, 0)),
                  pl.BlockSpec((SGC, H), lambda i: (0, 0))],
        out_specs=pl.BlockSpec((SGC, WD), lambda i: (i, 0)),
        out_shape=jax.ShapeDtypeStruct((T, WD), BF16), compiler_params=_cp("parallel"),
    )(proj, proj, lng, lnb, sgw, sgbt)


def _sg_bwd(proj, lng, lnb, sgw, sgbt, dy, name="sg_bwd"):
    T = proj.shape[0]

    def body(u_ref, v_ref, lng_ref, lnb_ref, w_ref, bt_ref, dy_ref,
             du_ref, dv_ref, dw_ref, dbt_ref, dlng_ref, dlnb_ref):
        @pl.when(pl.program_id(0) == 0)
        def _():
            dw_ref[...] = jnp.zeros_like(dw_ref)
            dbt_ref[...] = jnp.zeros_like(dbt_ref)
            dlng_ref[...] = jnp.zeros_like(dlng_ref)
            dlnb_ref[...] = jnp.zeros_like(dlnb_ref)

        ur, vr, rs, xh, vg = _sg_common(u_ref, v_ref, lng_ref, lnb_ref)
        ri = lax.broadcasted_iota(jnp.int32, (SGC, SGC), 0)
        ci = lax.broadcasted_iota(jnp.int32, (SGC, SGC), 1)
        ug = _gelu(ur)
        dyv = dy_ref[...]
        dbt = jnp.zeros((SGC, 128), F32)
        dvg_parts = []
        for g in range(H):
            sl = slice(g * DH, (g + 1) * DH)
            ws = jnp.where(ri >= ci, w_ref[g], 0.0)
            mixed = _bdot(ws, vg[:, sl]) + bt_ref[:, g:g + 1]
            dyg = dyv[:, sl]
            du_ref[:, sl] = (dyg * mixed * _dgelu(ur[:, sl])).astype(du_ref.dtype)
            dmix = dyg * ug[:, sl]
            dw_ref[g] += jnp.where(ri >= ci, _bdot(dmix, vg[:, sl], NT), 0.0)
            dbt = dbt + jnp.where(ci == g, _lsum(dmix), 0.0)
            dvg_parts.append(_bdot(ws, dmix, TN))
        dbt_ref[...] += dbt
        dvg = jnp.concatenate(dvg_parts, axis=1)
        dlng_ref[...] += jnp.sum(dvg * xh, axis=0, keepdims=True)
        dlnb_ref[...] += jnp.sum(dvg, axis=0, keepdims=True)
        dxh = dvg * lng_ref[...]
        dvgel = rs * (dxh - jnp.mean(dxh, axis=-1, keepdims=True) - xh * jnp.mean(dxh * xh, axis=-1, keepdims=True))
        dv_ref[...] = (dvgel * _dgelu(vr)).astype(dv_ref.dtype)

    vec = pl.BlockSpec((1, WD), lambda i: (0, 0))
    row = pl.BlockSpec((SGC, WD), lambda i: (i, 0))
    return pl.pallas_call(
        body, name=name, grid=(T // SGC,),
        in_specs=[pl.BlockSpec((SGC, WD), lambda i: (i, OU // WD)), pl.BlockSpec((SGC, WD), lambda i: (i, OV // WD)),
                  vec, vec, pl.BlockSpec((H, SGC, SGC), lambda i: (0, 0, 0)),
                  pl.BlockSpec((SGC, H), lambda i: (0, 0)), row],
        out_specs=[row, row, pl.BlockSpec((H, SGC, SGC), lambda i: (0, 0, 0)),
                   pl.BlockSpec((SGC, 128), lambda i: (0, 0)), vec, vec],
        out_shape=[jax.ShapeDtypeStruct((T, WD), BF16), jax.ShapeDtypeStruct((T, WD), BF16),
                   jax.ShapeDtypeStruct((H, SGC, SGC), F32), jax.ShapeDtypeStruct((SGC, 128), F32),
                   jax.ShapeDtypeStruct((1, WD), F32), jax.ShapeDtypeStruct((1, WD), F32)],
        compiler_params=_cp("arbitrary"),
    )(proj, proj, lng, lnb, sgw, sgbt, dy)


def _merge_fwd(proj, yap, ybp, D, name="merge_fwd"):
    T = proj.shape[0]
    tt = _tile(T, 256, 16)

    def body(ga_ref, gb_ref, a_ref, b_ref, o_ref):
        o_ref[...] = (_sig(ga_ref[...]) * a_ref[...] + _sig(gb_ref[...]) * b_ref[...]).astype(o_ref.dtype)

    row = pl.BlockSpec((tt, D), lambda i: (i, 0))
    return pl.pallas_call(
        body, name=name, grid=(T // tt,),
        in_specs=[pl.BlockSpec((tt, D), lambda i: (i, OGA // D)), pl.BlockSpec((tt, D), lambda i: (i, OGA // D + 1)),
                  row, row],
        out_specs=row, out_shape=jax.ShapeDtypeStruct((T, D), BF16), compiler_params=_cp("parallel"),
    )(proj, proj, yap, ybp)


def _merge_bwd(proj, yap, ybp, dm, D, name="merge_bwd"):
    T = proj.shape[0]
    tt = _tile(T, 256, 16)

    def body(ga_ref, gb_ref, a_ref, b_ref, dm_ref, da_ref, db_ref, dga_ref, dgb_ref):
        d = dm_ref[...]
        sa = _sig(ga_ref[...])
        sb = _sig(gb_ref[...])
        da_ref[...] = (d * sa).astype(da_ref.dtype)
        db_ref[...] = (d * sb).astype(db_ref.dtype)
        dga_ref[...] = (d * a_ref[...] * sa * (1.0 - sa)).astype(dga_ref.dtype)
        dgb_ref[...] = (d * b_ref[...] * sb * (1.0 - sb)).astype(dgb_ref.dtype)

    row = pl.BlockSpec((tt, D), lambda i: (i, 0))
    return pl.pallas_call(
        body, name=name, grid=(T // tt,),
        in_specs=[pl.BlockSpec((tt, D), lambda i: (i, OGA // D)), pl.BlockSpec((tt, D), lambda i: (i, OGA // D + 1)),
                  row, row, row],
        out_specs=[row] * 4, out_shape=[jax.ShapeDtypeStruct((T, D), BF16)] * 4,
        compiler_params=_cp("parallel"),
    )(proj, proj, yap, ybp, dm)


def _ffn_act_fwd(gp, up, cw, cb, name="ffn_act_fwd"):
    T, F = gp.shape

    def body(g_ref, u_ref, w_ref, b_ref, o_ref):
        gv = g_ref[...]
        w = w_ref[...]
        c = gv * w[FF_K - 1:FF_K, :] + b_ref[...]
        for k in range(1, FF_K):
            c = c + _shift_down(gv, k) * w[FF_K - 1 - k:FF_K - k, :]
        o_ref[...] = (c * _sig(c) * u_ref[...]).astype(o_ref.dtype)

    col = pl.BlockSpec((T, 128), lambda j: (0, j))
    return pl.pallas_call(
        body, name=name, grid=(F // 128,),
        in_specs=[col, col, pl.BlockSpec((FF_K, 128), lambda j: (0, j)), pl.BlockSpec((1, 128), lambda j: (0, j))],
        out_specs=col, out_shape=jax.ShapeDtypeStruct((T, F), BF16), compiler_params=_cp("parallel"),
    )(gp, up, cw, cb)


def _ffn_act_bwd(gp, up, cw, cb, dact, name="ffn_act_bwd"):
    T, F = gp.shape

    def body(g_ref, u_ref, w_ref, b_ref, d_ref, dg_ref, du_ref, dw_ref, db_ref):
        gv = g_ref[...]
        w = w_ref[...]
        shifted = [_shift_down(gv, k) for k in range(FF_K)]
        c = shifted[0] * w[FF_K - 1:FF_K, :] + b_ref[...]
        for k in range(1, FF_K):
            c = c + shifted[k] * w[FF_K - 1 - k:FF_K - k, :]
        d = d_ref[...]
        du_ref[...] = (d * c * _sig(c)).astype(du_ref.dtype)
        dc = d * u_ref[...] * _dsilu(c)
        dg = dc * w[FF_K - 1:FF_K, :]
        for k in range(1, FF_K):
            dg = dg + _shift_up(dc, k) * w[FF_K - 1 - k:FF_K - k, :]
        dg_ref[...] = dg.astype(dg_ref.dtype)
        rows = [jnp.sum(dc * shifted[FF_K - 1 - t], axis=0, keepdims=True) for t in range(FF_K)]
        dw_ref[...] = jnp.concatenate(rows, axis=0)
        db_ref[...] = jnp.sum(dc, axis=0, keepdims=True)

    col = pl.BlockSpec((T, 128), lambda j: (0, j))
    wspec = pl.BlockSpec((FF_K, 128), lambda j: (0, j))
    bspec = pl.BlockSpec((1, 128), lambda j: (0, j))
    return pl.pallas_call(
        body, name=name, grid=(F // 128,),
        in_specs=[col, col, wspec, bspec, col], out_specs=[col, col, wspec, bspec],
        out_shape=[jax.ShapeDtypeStruct((T, F), BF16), jax.ShapeDtypeStruct((T, F), BF16),
                   jax.ShapeDtypeStruct((FF_K, F), F32), jax.ShapeDtypeStruct((1, F), F32)],
        compiler_params=_cp("parallel"),
    )(gp, up, cw, cb, dact)


def _layer_fwd(x, w, gather=()):
    D = x.shape[1]
    oba = OGA + 2 * D
    h = _rms_fwd(x, w["norm1_g"], "rms1_fwd")
    proj = _mm(h, w["w_in_t"], "nt", F32, name="mm_proj")
    bg = _ba_fwd(proj, w["alog_row"], w["dtb_row"], oba)
    qkv = _dn_prep_fwd(proj, w["dn_conv_w"])
    o, s_all, tm_all, gathered = _dn_core_fwd(qkv, bg, gather)
    ya = _dn_post_fwd(o, proj, w["dn_onorm_g"])
    yb = _sg_fwd(proj, w["sg_ln_g"], w["sg_ln_b"], w["sg_w"], w["sg_bt"])
    yap = _mm(ya, w["w_branch_a"], "nn", F32, name="mm_branch")
    ybp = _mm(yb, w["w_branch_b"], "nn", F32, name="mm_branch")
    merged = _merge_fwd(proj, yap, ybp, D)
    x1 = _mm(merged, w["w_out"], "nn", F32, add=x, name="mm_out")
    h2 = _rms_fwd(x1, w["norm2_g"], "rms2_fwd")
    gp = _mm(h2, w["ffn_w_gate"], "nn", F32, name="mm_ffn_in")
    up = _mm(h2, w["ffn_w_up"], "nn", F32, name="mm_ffn_in")
    act = _ffn_act_fwd(gp, up, w["ffn_conv_w"], w["ffn_conv_b"])
    x2 = _mm(act, w["ffn_w_down"], "nn", F32, add=x1, name="mm_ffn_down")
    saved = dict(x=x, h=h, proj=proj, bg=bg, qkv=qkv, o=o, s_all=s_all, tm_all=tm_all, ya=ya, yb=yb, yap=yap,
                 ybp=ybp, merged=merged, x1=x1, h2=h2, gp=gp, up=up, act=act)
    return x2, saved, gathered


def _layer_bwd(dx2, w, s, exchange=()):
    D = dx2.shape[1]
    oba = OGA + 2 * D
    g = {}
    dx2b = dx2.astype(BF16)
    dact = _mm(dx2b, w["ffn_w_down"], "nt", F32, name="mm_d_act")
    g["ffn_w_down"] = _mm(s["act"], dx2b, "tn", BF16, name="mm_dw_down")
    dgp, dup, g["ffn_conv_w"], g["ffn_conv_b"] = _ffn_act_bwd(s["gp"], s["up"], w["ffn_conv_w"], w["ffn_conv_b"], dact)
    dh2 = _mm(dgp, w["ffn_w_gate"], "nt", F32, name="mm_dh2")
    dh2 = _mm(dup, w["ffn_w_up"], "nt", F32, add=dh2, name="mm_dh2_acc")
    g["ffn_w_gate"] = _mm(s["h2"], dgp, "tn", BF16, name="mm_dw_ffn_in")
    g["ffn_w_up"] = _mm(s["h2"], dup, "tn", BF16, name="mm_dw_ffn_in")
    dx1, g["norm2_g"] = _rms_bwd(s["x1"], w["norm2_g"], dh2, dx2, "rms2_bwd")
    dx1b = dx1.astype(BF16)
    dm = _mm(dx1b, w["w_out"], "nt", F32, name="mm_d_merged")
    g["w_out"] = _mm(s["merged"], dx1b, "tn", BF16, name="mm_dw_out")
    dyap, dybp, dga, dgb = _merge_bwd(s["proj"], s["yap"], s["ybp"], dm, D)
    dya = _mm(dyap, w["w_branch_a"], "nt", F32, name="mm_d_branch")
    dyb = _mm(dybp, w["w_branch_b"], "nt", F32, name="mm_d_branch")
    g["w_branch_a"] = _mm(s["ya"], dyap, "tn", BF16, name="mm_dw_branch")
    g["w_branch_b"] = _mm(s["yb"], dybp, "tn", BF16, name="mm_dw_branch")
    du, dv, g["sg_w"], dbt, g["sg_ln_g"], g["sg_ln_b"] = _sg_bwd(
        s["proj"], w["sg_ln_g"], w["sg_ln_b"], w["sg_w"], w["sg_bt"], dyb)
    g["sg_b"] = jnp.transpose(dbt[:, :H])
    do, dz, g["dn_onorm_g"] = _dn_post_bwd(s["o"], s["proj"], w["dn_onorm_g"], dya)
    dq, dk, dvv, dbg, exchanged = _dn_core_bwd(s["qkv"], s["bg"], s["s_all"], s["tm_all"], do, exchange)
    dqkv, g["dn_conv_w"] = _dn_prep_bwd(s["proj"], w["dn_conv_w"], dq, dk, dvv)
    dba, dal, ddt = _ba_bwd(s["proj"], w["alog_row"], w["dtb_row"], dbg, oba)
    g["dn_a_log"] = dal[0, H:2 * H]
    g["dn_dt_bias"] = ddt[0, H:2 * H]
    dproj = jnp.concatenate([dqkv, dz, du, dv, dga, dgb, dba], axis=1)
    dh = _mm(dproj, w["w_in_t"], "nn", F32, name="mm_dh")
    g["w_in_t"] = _mm(dproj, s["h"], "tn", BF16, name="mm_dw_in")
    dx, g["norm1_g"] = _rms_bwd(s["x"], w["norm1_g"], dh, dx1, "rms1_bwd")
    return dx, g, exchanged


def _local_step(x, tgt, layers, final_g):
    saved = []
    for w in layers:
        x, s, _ = _layer_fwd(x, w)
        saved.append(s)
    dx, dgf, loss = _loss_head(x, final_g, tgt)
    grads = [None] * len(layers)
    for l in reversed(range(len(layers))):
        dx, grads[l], _ = _layer_bwd(dx, layers[l], saved[l])
    return loss[0, 0], dx, grads, dgf


def _w_in_pad(wt):
    c1 = 4 * WD
    return jnp.concatenate([wt[:c1], wt[c1 + 2 * H:], wt[c1:c1 + 2 * H],
                            jnp.zeros((128 - 2 * H, wt.shape[1]), wt.dtype)], axis=0)


def _w_in_unpad(gt):
    c1 = 4 * WD
    n = gt.shape[0] - 128
    return jnp.concatenate([gt[:c1], gt[n:n + 2 * H], gt[c1:n]], axis=0)


def _row128(v, off):
    return jnp.pad(v, (off, 128 - off - v.shape[0]))[None]


def _prep_layer(p):
    return dict(
        norm1_g=p["norm1_g"][None], w_in_t=_w_in_pad(p["w_in_t"]),
        alog_row=_row128(p["dn_a_log"], H), dtb_row=_row128(p["dn_dt_bias"], H),
        dn_conv_w=p["dn_conv_w"], dn_onorm_g=p["dn_onorm_g"][None],
        sg_ln_g=p["sg_ln_g"][None], sg_ln_b=p["sg_ln_b"][None], sg_w=p["sg_w"], sg_bt=jnp.transpose(p["sg_b"]),
        w_branch_a=p["w_branch_a"], w_branch_b=p["w_branch_b"], w_out=p["w_out"], norm2_g=p["norm2_g"][None],
        ffn_w_gate=p["ffn_w_gate"], ffn_w_up=p["ffn_w_up"], ffn_conv_w=p["ffn_conv_w"],
        ffn_conv_b=p["ffn_conv_b"][None], ffn_w_down=p["ffn_w_down"])


HBM_SPEC = pl.BlockSpec(memory_space=pltpu.HBM)


def _coords():
    return lax.axis_index("x"), lax.axis_index("y"), lax.axis_index("c")


def _other_chips(x, y):
    return [(1 - x, y), (x, 1 - y), (1 - x, 1 - y)]


def _remote(src, dst, send_sems, recv_sems, k, dev):
    return pltpu.make_async_remote_copy(src_ref=src, dst_ref=dst, send_sem=send_sems.at[k], recv_sem=recv_sems.at[k],
                                        device_id=dev, device_id_type=MESH)


def _ag_copies(w_refs, o_refs, send_sems, recv_sems):
    x, y, c = _coords()
    me = 2 * x + y
    chips = _other_chips(x, y)

    def ici(k, j, owner):
        chip = chips[j]
        return _remote(w_refs[k].at[c], o_refs[k].at[owner, c], send_sems, recv_sems, 6 * k + j, (chip[0], chip[1], c))

    def d2d(k, j, part):
        owner = 2 * chips[j][0] + chips[j][1]
        return _remote(o_refs[k].at[owner, part], o_refs[k].at[owner, part], send_sems, recv_sems, 6 * k + 3 + j,
                       (x, y, 1 - c))

    n = len(w_refs)
    return me, c, chips, ici, d2d, [(k, j) for k in range(n) for j in range(3)]


def _ag_start(w_refs, o_refs, send_sems, recv_sems):
    me, _, _, ici, _, pairs = _ag_copies(w_refs, o_refs, send_sems, recv_sems)
    for k, j in pairs:
        ici(k, j, me).start()


def _ag_finish(w_refs, o_refs, send_sems, recv_sems):
    me, c, chips, ici, d2d, pairs = _ag_copies(w_refs, o_refs, send_sems, recv_sems)
    for k, j in pairs:
        ici(k, j, 2 * chips[j][0] + chips[j][1]).wait_recv()
        d2d(k, j, c).start()
    for k, j in pairs:
        d2d(k, j, 1 - c).wait_recv()
    for k, j in pairs:
        ici(k, j, me).wait_send()
        d2d(k, j, c).wait_send()


def _ag_specs(ws):
    n = len(ws)
    return dict(out_shape=[jax.ShapeDtypeStruct((N_CHIPS,) + w.shape, w.dtype) for w in ws],
                specs=[HBM_SPEC] * n, sems=[pltpu.SemaphoreType.DMA((6 * n,)), pltpu.SemaphoreType.DMA((6 * n,))])


def _ag_layers(ws):
    n = len(ws)

    def body(*refs):
        _ag_start(refs[:n], refs[n:2 * n], *refs[2 * n:])
        _ag_finish(refs[:n], refs[n:2 * n], *refs[2 * n:])

    sp = _ag_specs(ws)
    return pl.pallas_call(
        body, name="ag_weights", out_shape=sp["out_shape"], in_specs=sp["specs"], out_specs=sp["specs"],
        scratch_shapes=sp["sems"],
    )(*ws)


def _rs_pair_exchange(Gs):
    n = len(Gs)

    def body(*refs):
        g_refs, b_refs = refs[:n], refs[n:2 * n]
        send_sems, recv_sems = refs[2 * n:]
        x, y, c = _coords()
        cps = [_remote(g_refs[k].at[i, 1 - c], b_refs[k].at[i], send_sems, recv_sems, N_CHIPS * k + i, (x, y, 1 - c))
               for k in range(n) for i in range(N_CHIPS)]
        for cp in cps:
            cp.start()
        for cp in cps:
            cp.wait()

    return pl.pallas_call(
        body, name="rs_pair_exchange",
        out_shape=[jax.ShapeDtypeStruct((N_CHIPS,) + g.shape[2:], g.dtype) for g in Gs],
        in_specs=[HBM_SPEC] * n, out_specs=[HBM_SPEC] * n,
        scratch_shapes=[pltpu.SemaphoreType.DMA((N_CHIPS * n,)), pltpu.SemaphoreType.DMA((N_CHIPS * n,))],
    )(*Gs)


def _rs_add_pair(G, B, c, name):
    _, _, R, C = G.shape
    tr = _tile(R, 256, 16)

    def body(c_ref, g_ref, b_ref, o_ref):
        o_ref[0] = (g_ref[0, 0].astype(F32) + b_ref[0].astype(F32)).astype(o_ref.dtype)

    grid_spec = pltpu.PrefetchScalarGridSpec(
        num_scalar_prefetch=1, grid=(N_CHIPS, R // tr),
        in_specs=[pl.BlockSpec((1, 1, tr, C), lambda i, r, c_ref: (i, c_ref[0], r, 0)),
                  pl.BlockSpec((1, tr, C), lambda i, r, c_ref: (i, r, 0))],
        out_specs=pl.BlockSpec((1, tr, C), lambda i, r, c_ref: (i, r, 0)))
    return pl.pallas_call(
        body, name=name, grid_spec=grid_spec, out_shape=jax.ShapeDtypeStruct((N_CHIPS, R, C), G.dtype),
        compiler_params=_cp("parallel", "parallel"),
    )(jnp.reshape(c, (1,)).astype(jnp.int32), G, B)


def _rs_chip_exchange(Ps):
    n = len(Ps)

    def body(*refs):
        _rsx_start(refs[:n], refs[n:2 * n], *refs[2 * n:])
        _rsx_finish(refs[:n], refs[n:2 * n], *refs[2 * n:])

    sp = _rsx_specs(Ps)
    return pl.pallas_call(
        body, name="rs_chip_exchange", out_shape=sp["out_shape"], in_specs=sp["specs"], out_specs=sp["specs"],
        scratch_shapes=sp["sems"],
    )(*Ps)


def _rsx_copies(p_refs, b_refs, send_sems, recv_sems):
    x, y, c = _coords()
    me = 2 * x + y
    chips = _other_chips(x, y)

    def cp(k, j, src_slot, dst_slot):
        return _remote(p_refs[k].at[src_slot], b_refs[k].at[dst_slot], send_sems, recv_sems, 3 * k + j,
                       (chips[j][0], chips[j][1], c))

    return me, chips, cp, [(k, j) for k in range(len(p_refs)) for j in range(3)]


def _rsx_start(p_refs, b_refs, send_sems, recv_sems):
    me, chips, cp, pairs = _rsx_copies(p_refs, b_refs, send_sems, recv_sems)
    for k, j in pairs:
        cp(k, j, 2 * chips[j][0] + chips[j][1], me).start()


def _rsx_finish(p_refs, b_refs, send_sems, recv_sems):
    me, chips, cp, pairs = _rsx_copies(p_refs, b_refs, send_sems, recv_sems)
    for k, j in pairs:
        owner = 2 * chips[j][0] + chips[j][1]
        cp(k, j, owner, owner).wait_recv()
    for k, j in pairs:
        cp(k, j, 2 * chips[j][0] + chips[j][1], me).wait_send()


def _rsx_specs(Ps):
    n = len(Ps)
    return dict(out_shape=[jax.ShapeDtypeStruct(p.shape, p.dtype) for p in Ps], specs=[HBM_SPEC] * n,
                sems=[pltpu.SemaphoreType.DMA((3 * n,)), pltpu.SemaphoreType.DMA((3 * n,))])


def _rs_sum_chips(P, B, me, name):
    _, R, C = P.shape
    tr = _tile(R, 256, 16)

    def body(me_ref, p_ref, b1_ref, b2_ref, b3_ref, o_ref):
        o_ref[...] = ((p_ref[0].astype(F32) + b1_ref[0].astype(F32)) + b2_ref[0].astype(F32)) + b3_ref[0].astype(F32)

    slot = lambda d: pl.BlockSpec((1, tr, C), lambda r, me_ref: ((me_ref[0] + d) % N_CHIPS, r, 0))
    grid_spec = pltpu.PrefetchScalarGridSpec(
        num_scalar_prefetch=1, grid=(R // tr,), in_specs=[slot(0), slot(1), slot(2), slot(3)],
        out_specs=pl.BlockSpec((tr, C), lambda r, me_ref: (r, 0)))
    return pl.pallas_call(
        body, name=name, grid_spec=grid_spec, out_shape=jax.ShapeDtypeStruct((R, C), F32),
        compiler_params=_cp("parallel"),
    )(jnp.reshape(me, (1,)).astype(jnp.int32), P, B, B, B)


def _sum_slots(B, name):
    S, R, C = B.shape
    tr = _tile(R, 256, 16)

    def body(b_ref, o_ref):
        acc = b_ref[0].astype(F32)
        for i in range(1, S):
            acc = acc + b_ref[i].astype(F32)
        o_ref[...] = acc

    return pl.pallas_call(
        body, name=name, grid=(R // tr,), in_specs=[pl.BlockSpec((S, tr, C), lambda r: (0, r, 0))],
        out_specs=pl.BlockSpec((tr, C), lambda r: (r, 0)), out_shape=jax.ShapeDtypeStruct((R, C), F32),
        compiler_params=_cp("parallel"),
    )(B)


def _rs_pair_swap(Rs):
    n = len(Rs)

    def body(*refs):
        r_refs, o_refs = refs[:n], refs[n:2 * n]
        send_sems, recv_sems = refs[2 * n:]
        x, y, c = _coords()
        cps = [_remote(r_refs[k], o_refs[k], send_sems, recv_sems, k, (x, y, 1 - c)) for k in range(n)]
        for cp in cps:
            cp.start()
        for cp in cps:
            cp.wait()

    return pl.pallas_call(
        body, name="rs_pair_swap", out_shape=[jax.ShapeDtypeStruct(r.shape, r.dtype) for r in Rs],
        in_specs=[HBM_SPEC] * n, out_specs=[HBM_SPEC] * n,
        scratch_shapes=[pltpu.SemaphoreType.DMA((n,)), pltpu.SemaphoreType.DMA((n,))],
    )(*Rs)


def _ag8(v):
    R = v.shape[0]

    def body(v_ref, out_ref, send_sems, recv_sems, local_sem):
        x, y, c = _coords()
        me, sib = (x, y, c), (x, y, 1 - c)
        chips = _other_chips(x, y)

        def slot(p):
            return out_ref.at[4 * p[0] + 2 * p[1] + p[2]]

        def copy(k, block, to, src=None):
            return _remote(slot(block) if src is None else src, slot(block), send_sems, recv_sems, k, to)

        mine = pltpu.make_async_copy(v_ref, slot(me), local_sem)
        mine.start()
        first = [copy(0, me, sib, src=v_ref)]
        first += [copy(1 + j, me, (chip[0], chip[1], c), src=v_ref) for j, chip in enumerate(chips)]
        for cp in first:
            cp.start()
        passed = [copy(4 + j, (chip[0], chip[1], c), sib) for j, chip in enumerate(chips)]
        for j, chip in enumerate(chips):
            copy(1 + j, (chip[0], chip[1], c), me).wait_recv()
            passed[j].start()
        copy(0, sib, me).wait_recv()
        for j, chip in enumerate(chips):
            copy(4 + j, (chip[0], chip[1], 1 - c), me).wait_recv()
        for cp in first + passed:
            cp.wait_send()
        mine.wait()

    return pl.pallas_call(
        body, name="ag8_small", out_shape=jax.ShapeDtypeStruct((8, R, 128), v.dtype),
        in_specs=[pl.BlockSpec(memory_space=pltpu.VMEM)], out_specs=pl.BlockSpec(memory_space=pltpu.VMEM),
        scratch_shapes=[pltpu.SemaphoreType.DMA((7,)), pltpu.SemaphoreType.DMA((7,)), pltpu.SemaphoreType.DMA],
        compiler_params=pltpu.CompilerParams(vmem_limit_bytes=VMEM_LIMIT),
    )(v)


def _adamw(w, g, m, v, name):
    L, R, C = w.shape
    tr = _tile(R, 128, 8)

    def body(w_ref, g_ref, m_ref, v_ref, d_ref, mo_ref, vo_ref):
        gv = g_ref[...]
        m2 = ADAM_B1 * m_ref[...] + (1.0 - ADAM_B1) * gv
        v2 = ADAM_B2 * v_ref[...] + (1.0 - ADAM_B2) * jnp.square(gv)
        m_hat = m2 / (1.0 - ADAM_B1 ** ADAM_STEP)
        v_hat = v2 / (1.0 - ADAM_B2 ** ADAM_STEP)
        d_ref[...] = -ADAM_LR * (m_hat / (jnp.sqrt(v_hat) + ADAM_EPS) + ADAM_WD * w_ref[...])
        mo_ref[...] = m2
        vo_ref[...] = v2

    blk = pl.BlockSpec((1, tr, C), lambda l, r: (l, r, 0))
    return pl.pallas_call(
        body, name=name, grid=(L, R // tr), in_specs=[blk] * 4, out_specs=[blk] * 3,
        out_shape=[jax.ShapeDtypeStruct(w.shape, F32)] * 3, compiler_params=_cp("parallel", "parallel"),
    )(w, g, m, v)


def _adamw_halves(w, g_mine, g_other, c, m, v, name):
    L, _, R, C = w.shape
    tr = _tile(R, 128, 8)

    def body(c_ref, w_ref, *rest):
        g_refs = rest[:2 * L]
        m_ref, v_ref, g_ref, d_ref, mo_ref, vo_ref = rest[2 * L:]
        l, h = pl.program_id(0), pl.program_id(1)
        gm, go = g_refs[0][...], g_refs[L][...]
        for i in range(1, L):
            gm = jnp.where(l == i, g_refs[i][...], gm)
            go = jnp.where(l == i, g_refs[L + i][...], go)
        gv = jnp.where(h == c_ref[0], gm, go)[None, None]
        g_ref[...] = gv
        m2 = ADAM_B1 * m_ref[...] + (1.0 - ADAM_B1) * gv
        v2 = ADAM_B2 * v_ref[...] + (1.0 - ADAM_B2) * jnp.square(gv)
        m_hat = m2 / (1.0 - ADAM_B1 ** ADAM_STEP)
        v_hat = v2 / (1.0 - ADAM_B2 ** ADAM_STEP)
        d_ref[...] = -ADAM_LR * (m_hat / (jnp.sqrt(v_hat) + ADAM_EPS) + ADAM_WD * w_ref[...])
        mo_ref[...] = m2
        vo_ref[...] = v2

    blk = pl.BlockSpec((1, 1, tr, C), lambda l, h, r, c_ref: (l, h, r, 0))

    def gblk(i, mine):
        def index(l, h, r, c_ref):
            use = jnp.logical_and(l == i, (h == c_ref[0]) == mine)
            return (jnp.where(use, r, 0), 0)
        return pl.BlockSpec((tr, C), index)

    grid_spec = pltpu.PrefetchScalarGridSpec(
        num_scalar_prefetch=1, grid=(L, 2, R // tr),
        in_specs=[blk] + [gblk(i, True) for i in range(L)] + [gblk(i, False) for i in range(L)] + [blk, blk],
        out_specs=[blk] * 4)
    return pl.pallas_call(
        body, name=name, grid_spec=grid_spec, out_shape=[jax.ShapeDtypeStruct(w.shape, F32)] * 4,
        compiler_params=_cp("parallel", "parallel", "parallel"),
    )(jnp.reshape(c, (1,)).astype(jnp.int32), w, *g_mine, *g_other, m, v)


BIG = ("w_in", "w_branch_a", "w_branch_b", "w_out", "ffn_w_gate", "ffn_w_up", "ffn_w_down")
ROW_SHARDED = ("w_out", "ffn_w_down")
SMALL = ("norm1_g", "dn_conv_w", "dn_a_log", "dn_dt_bias", "dn_onorm_g", "sg_ln_g", "sg_ln_b", "sg_w", "sg_b",
         "norm2_g", "ffn_conv_w", "ffn_conv_b", "final_norm_g")
SMALL_SHARDED = ("dn_conv_w", "ffn_conv_w")


def _pack_rows(arrs, mult):
    flat = jnp.concatenate([jnp.reshape(a, (-1,)) for a in arrs])
    n = flat.shape[0]
    rows = -(-n // (128 * mult)) * mult
    return jnp.reshape(jnp.pad(flat, (0, rows * 128 - n)), (rows, 128))


def _unpack(flat2d, shapes):
    flat = jnp.reshape(flat2d, (-1,))
    out, off = [], 0
    for shp in shapes:
        n = math.prod(shp)
        out.append(jnp.reshape(flat[off:off + n], shp))
        off += n
    return out


def _shards_to_full(a, row_sharded):
    if row_sharded:
        a = jnp.moveaxis(a, 0, 1)
        return jnp.reshape(a, (a.shape[0], a.shape[1] * a.shape[2], a.shape[3]))
    a = jnp.moveaxis(a, 0, 2)
    return jnp.reshape(a, (a.shape[0], a.shape[1], a.shape[2] * a.shape[3]))


def _full_to_shards(a, row_sharded):
    L, R, C = a.shape
    if row_sharded:
        return jnp.moveaxis(jnp.reshape(a, (L, N_CHIPS, R // N_CHIPS, C)), 1, 0)
    return jnp.moveaxis(jnp.reshape(a, (L, R, N_CHIPS, C // N_CHIPS)), 2, 0)


def kernel(x, norm1_g, w_in, dn_conv_w, dn_a_log, dn_dt_bias, dn_onorm_g, sg_ln_g, sg_ln_b, sg_w, sg_b, w_branch_a, w_branch_b, w_out, norm2_g, ffn_w_gate, ffn_w_up, ffn_conv_w, ffn_conv_b, ffn_w_down, final_norm_g, loss_target, m_norm1_g, m_w_in, m_dn_conv_w, m_dn_a_log, m_dn_dt_bias, m_dn_onorm_g, m_sg_ln_g, m_sg_ln_b, m_sg_w, m_sg_b, m_w_branch_a, m_w_branch_b, m_w_out, m_norm2_g, m_ffn_w_gate, m_ffn_w_up, m_ffn_conv_w, m_ffn_conv_b, m_ffn_w_down, m_final_norm_g, v_norm1_g, v_w_in, v_dn_conv_w, v_dn_a_log, v_dn_dt_bias, v_dn_onorm_g, v_sg_ln_g, v_sg_ln_b, v_sg_w, v_sg_b, v_w_branch_a, v_w_branch_b, v_w_out, v_norm2_g, v_ffn_w_gate, v_ffn_w_up, v_ffn_conv_w, v_ffn_conv_b, v_ffn_w_down, v_final_norm_g):
    W = dict(norm1_g=norm1_g, w_in=w_in, dn_conv_w=dn_conv_w, dn_a_log=dn_a_log, dn_dt_bias=dn_dt_bias,
             dn_onorm_g=dn_onorm_g, sg_ln_g=sg_ln_g, sg_ln_b=sg_ln_b, sg_w=sg_w, sg_b=sg_b, w_branch_a=w_branch_a,
             w_branch_b=w_branch_b, w_out=w_out, norm2_g=norm2_g, ffn_w_gate=ffn_w_gate, ffn_w_up=ffn_w_up,
             ffn_conv_w=ffn_conv_w, ffn_conv_b=ffn_conv_b, ffn_w_down=ffn_w_down, final_norm_g=final_norm_g)
    M = dict(norm1_g=m_norm1_g, w_in=m_w_in, dn_conv_w=m_dn_conv_w, dn_a_log=m_dn_a_log, dn_dt_bias=m_dn_dt_bias,
             dn_onorm_g=m_dn_onorm_g, sg_ln_g=m_sg_ln_g, sg_ln_b=m_sg_ln_b, sg_w=m_sg_w, sg_b=m_sg_b,
             w_branch_a=m_w_branch_a, w_branch_b=m_w_branch_b, w_out=m_w_out, norm2_g=m_norm2_g,
             ffn_w_gate=m_ffn_w_gate, ffn_w_up=m_ffn_w_up, ffn_conv_w=m_ffn_conv_w, ffn_conv_b=m_ffn_conv_b,
             ffn_w_down=m_ffn_w_down, final_norm_g=m_final_norm_g)
    V = dict(norm1_g=v_norm1_g, w_in=v_w_in, dn_conv_w=v_dn_conv_w, dn_a_log=v_dn_a_log, dn_dt_bias=v_dn_dt_bias,
             dn_onorm_g=v_dn_onorm_g, sg_ln_g=v_sg_ln_g, sg_ln_b=v_sg_ln_b, sg_w=v_sg_w, sg_b=v_sg_b,
             w_branch_a=v_w_branch_a, w_branch_b=v_w_branch_b, w_out=v_w_out, norm2_g=v_norm2_g,
             ffn_w_gate=v_ffn_w_gate, ffn_w_up=v_ffn_w_up, ffn_conv_w=v_ffn_conv_w, ffn_conv_b=v_ffn_conv_b,
             ffn_w_down=v_ffn_w_down, final_norm_g=v_final_norm_g)
    cx, cy, cc = _coords()
    chip = 2 * cx + cy
    L = w_in.shape[0]

    D = w_in.shape[1]
    cs_in = w_in.shape[2]
    rp_in = -(-cs_in // 128) * 128

    def shard_for_gather(n):
        if n == "w_in":
            return jnp.pad(jnp.swapaxes(W[n], 1, 2).astype(BF16), ((0, 0), (0, rp_in - cs_in), (0, 0)))
        return W[n].astype(BF16)

    mine = {n: shard_for_gather(n) for n in BIG}

    def halves(a, lead=0):
        return jnp.reshape(a, a.shape[:lead] + (2, a.shape[lead] // 2) + a.shape[lead + 1:])

    def gather_args(l):
        return [halves(mine[n][l]) for n in BIG]

    def layer_operands(l, gathered):
        p = {n: W[n][l] for n in W if n not in ("final_norm_g",) + BIG + SMALL_SHARDED}
        for n, a in zip(BIG, gathered):
            a = jnp.reshape(a, (N_CHIPS,) + mine[n].shape[1:])
            a = lax.dynamic_update_slice_in_dim(a, mine[n][l][None], chip, axis=0)
            if n == "w_in":
                p["w_in_t"] = jnp.reshape(a[:, :cs_in], (N_CHIPS * cs_in, D))
            elif n in ROW_SHARDED:
                p[n] = jnp.reshape(a, (N_CHIPS * a.shape[1], a.shape[2]))
            else:
                p[n] = jnp.reshape(jnp.moveaxis(a, 0, 1), (a.shape[1], N_CHIPS * a.shape[2]))
        for n in SMALL_SHARDED:
            p[n] = taps_full[n][l]
        return _prep_layer(p)

    taps = _ag8(_pack_rows([W[n] for n in SMALL_SHARDED], 16))
    tap_shards = [_unpack(taps[2 * i], [W[n].shape for n in SMALL_SHARDED]) for i in range(N_CHIPS)]
    taps_full = {n: jnp.concatenate([tap_shards[i][k] for i in range(N_CHIPS)], axis=-1)
                 for k, n in enumerate(SMALL_SHARDED)}

    def grad_partials(g):
        Gs = []
        for n in BIG:
            if n == "w_in":
                gt = jnp.reshape(_w_in_unpad(g["w_in_t"]), (N_CHIPS, cs_in, D))
                a = jnp.pad(gt, ((0, 0), (0, rp_in - cs_in), (0, 0)))
            elif n in ROW_SHARDED:
                a = jnp.reshape(g[n], (N_CHIPS, g[n].shape[0] // N_CHIPS, g[n].shape[1]))
            else:
                a = jnp.moveaxis(jnp.reshape(g[n], (g[n].shape[0], N_CHIPS, g[n].shape[1] // N_CHIPS)), 1, 0)
            Gs.append(halves(a, 1))
        B1s = _rs_pair_exchange(Gs)
        return [_rs_add_pair(a, b, cc, "rs_add_pair_" + n) for n, a, b in zip(BIG, Gs, B1s)]

    layer0 = layer_operands(0, _ag_layers(gather_args(0)))
    x1, saved0, gathered1 = _layer_fwd(x[0], layer0, gather=gather_args(1))
    layer1 = layer_operands(1, gathered1)
    x2, saved1, _ = _layer_fwd(x1, layer1)
    dx, dgf, loss = _loss_head(x2, final_norm_g[None], loss_target[0])
    loss = loss[0, 0]
    dx, grads1, _ = _layer_bwd(dx, layer1, saved1)
    Ps1 = grad_partials(grads1)
    dx, grads0, B2s1 = _layer_bwd(dx, layer0, saved0, exchange=Ps1)
    grads = [grads0, grads1]
    Ps0 = grad_partials(grads0)
    B2s0 = _rs_chip_exchange(Ps0)
    g_mine = [[_rs_sum_chips(p, b, chip, "rs_sum_chips_" + n) for n, p, b in zip(BIG, Ps, B2s)]
              for Ps, B2s in ((Ps0, B2s0), (Ps1, B2s1))]
    swapped = _rs_pair_swap(g_mine[0] + g_mine[1])
    g_other = [swapped[:len(BIG)], swapped[len(BIG):]]

    small = {n: jnp.stack([g[n] for g in grads]) for n in SMALL if n != "final_norm_g"}
    small["final_norm_g"] = dgf
    shapes = [taps_full[n].shape if n in SMALL_SHARDED else W[n].shape for n in SMALL] + [(1,)]
    sflat = _pack_rows([small[n] for n in SMALL] + [jnp.reshape(loss, (1,))], 16)
    sred = _unpack(_sum_slots(_ag8(sflat), "sum_small"), shapes)
    g_small = dict(zip(SMALL, sred[:-1]))
    loss_total = sred[-1][0]
    for n in SMALL_SHARDED:
        cs = W[n].shape[-1]
        g_small[n] = lax.dynamic_slice_in_dim(g_small[n], chip * cs, cs, axis=-1)

    g_big, delta, new_m, new_v = {}, {}, {}, {}
    for k, n in enumerate(BIG):
        gm, go = [g_mine[l][k] for l in range(L)], [g_other[l][k] for l in range(L)]
        if n == "w_in":
            rows = [jnp.where(cc == 0, jnp.concatenate([a, b]), jnp.concatenate([b, a])) for a, b in zip(gm, go)]
            g_big[n] = jnp.stack([jnp.transpose(r[:cs_in]) for r in rows])
            delta[n], new_m[n], new_v[n] = _adamw(W[n], g_big[n], M[n], V[n], "adamw_" + n)
        else:
            outs = _adamw_halves(halves(W[n], 1), gm, go, cc, halves(M[n], 1), halves(V[n], 1), "adamw_" + n)
            g_big[n], delta[n], new_m[n], new_v[n] = [jnp.reshape(o, W[n].shape) for o in outs]
    s_shapes = [W[n].shape for n in SMALL]
    packed = [_pack_rows([d[n] for n in SMALL], 8) for d in (W, g_small, M, V)]
    outs = _adamw(*[a[None] for a in packed], "adamw_small")
    for d, o in zip((delta, new_m, new_v), outs):
        d.update(zip(SMALL, _unpack(o[0], s_shapes)))

    names = list(W)
    grad_w = {**g_big, **g_small}
    return (loss_total, dx[None], *[grad_w[n] for n in names], *[delta[n] for n in names],
            *[new_m[n] for n in names], *[new_v[n] for n in names])
```

```python
import functools
import math

import jax
import jax.numpy as jnp
from jax import lax
from jax.experimental import pallas as pl
from jax.experimental.pallas import tpu as pltpu

F32 = jnp.float32
BF16 = jnp.bfloat16
MESH = pl.DeviceIdType.MESH

EPS = 1e-6
H = 8
DH = 128
WD = H * DH
DNC = 64
SGC = 128
DN_K = 4
FF_K = 3
DEPTH = 2
N_CHIPS = 4

ADAM_LR = 0.001
ADAM_B1 = 0.9
ADAM_B2 = 0.999
ADAM_EPS = 1e-08
ADAM_WD = 0.01
ADAM_STEP = 10

VMEM_LIMIT = 56 * 1024 * 1024

NN = (((1,), (0,)), ((), ()))
NT = (((1,), (1,)), ((), ()))
TN = (((0,), (0,)), ((), ()))

OQ, OZ, OU, OV, OGA = 0, 3 * WD, 4 * WD, 5 * WD, 6 * WD


def _cp(*sem):
    return pltpu.CompilerParams(dimension_semantics=sem or None, vmem_limit_bytes=VMEM_LIMIT)


def _tile(dim, pref, unit=128):
    if dim <= pref:
        return dim
    t = (pref // unit) * unit
    while t >= unit:
        if dim % t == 0:
            return t
        t -= unit
    return dim


def _hdot(a, b, dn=NN):
    return lax.dot_general(a, b, dn, precision=lax.Precision.HIGHEST, preferred_element_type=F32)


def _bdot(a, b, dn=NN):
    return lax.dot_general(a.astype(BF16), b.astype(BF16), dn, preferred_element_type=F32)


def _lsum(x):
    return jnp.sum(x, axis=1, keepdims=True)


def _sig(x):
    return jax.nn.sigmoid(x)


def _dsilu(x):
    s = _sig(x)
    return s * (1.0 + x * (1.0 - s))


def _erf(x):
    a = jnp.abs(x)
    t = 1.0 / (1.0 + 0.3275911 * a)
    poly = t * (0.254829592 + t * (-0.284496736 + t * (1.421413741 + t * (-1.453152027 + t * 1.061405429))))
    r = 1.0 - poly * jnp.exp(-a * a)
    return jnp.where(x < 0, -r, r)


def _gelu(x):
    return 0.5 * x * (1.0 + _erf(x * (2.0 ** -0.5)))


def _dgelu(x):
    cdf = 0.5 * (1.0 + _erf(x * (2.0 ** -0.5)))
    pdf = jnp.exp(-0.5 * x * x) * (1.0 / math.sqrt(2.0 * math.pi))
    return cdf + x * pdf


def _shift_down(x, k):
    if k == 0:
        return x
    rows = lax.broadcasted_iota(jnp.int32, x.shape, 0)
    return jnp.where(rows >= k, pltpu.roll(x, k, 0), 0.0)


def _shift_up(x, k):
    if k == 0:
        return x
    n = x.shape[0]
    rows = lax.broadcasted_iota(jnp.int32, x.shape, 0)
    return jnp.where(rows < n - k, pltpu.roll(x, n - k, 0), 0.0)


def _mm(a, b, mode, out_dtype, add=None, name="mm"):
    if mode == "tn":
        K, M = a.shape
    else:
        M, K = a.shape
    N = b.shape[0] if mode == "nt" else b.shape[1]
    tm, tn, tk = _tile(M, 1152), _tile(N, 1536), _tile(K, 2048)
    nk = K // tk
    dn = {"nn": NN, "nt": NT, "tn": TN}[mode]

    def body(a_ref, b_ref, *rest):
        add_ref = rest[0] if add is not None else None
        o_ref = rest[-2] if nk > 1 else rest[-1]

        def finish(r):
            if add is not None:
                r = r + add_ref[...]
            o_ref[...] = r.astype(o_ref.dtype)

        part = lax.dot_general(a_ref[...], b_ref[...], dn, preferred_element_type=F32)
        if nk == 1:
            finish(part)
            return
        acc_ref = rest[-1]
        k = pl.program_id(2)

        @pl.when(k == 0)
        def _():
            acc_ref[...] = part

        @pl.when(k > 0)
        def _():
            acc_ref[...] += part

        @pl.when(k == nk - 1)
        def _():
            finish(acc_ref[...])

    a_spec = (pl.BlockSpec((tk, tm), lambda i, j, k: (k, i)) if mode == "tn"
              else pl.BlockSpec((tm, tk), lambda i, j, k: (i, k)))
    b_spec = (pl.BlockSpec((tn, tk), lambda i, j, k: (j, k)) if mode == "nt"
              else pl.BlockSpec((tk, tn), lambda i, j, k: (k, j)))
    o_spec = pl.BlockSpec((tm, tn), lambda i, j, k: (i, j))
    in_specs = [a_spec, b_spec] + ([o_spec] if add is not None else [])
    args = (a, b) + ((add,) if add is not None else ())
    return pl.pallas_call(
        body, name=name, grid=(M // tm, N // tn, nk), in_specs=in_specs, out_specs=o_spec,
        out_shape=jax.ShapeDtypeStruct((M, N), out_dtype),
        scratch_shapes=[pltpu.VMEM((tm, tn), F32)] if nk > 1 else [],
        compiler_params=_cp("parallel", "parallel", "arbitrary"),
    )(*args)


def _rms_fwd(x, g, name):
    T, D = x.shape
    tt = _tile(T, 256, 16)

    def body(x_ref, g_ref, o_ref):
        xv = x_ref[...]
        r = lax.rsqrt(jnp.mean(xv * xv, axis=-1, keepdims=True) + EPS)
        o_ref[...] = (xv * r * g_ref[...]).astype(o_ref.dtype)

    return pl.pallas_call(
        body, name=name, grid=(T // tt,),
        in_specs=[pl.BlockSpec((tt, D), lambda i: (i, 0)), pl.BlockSpec((1, D), lambda i: (0, 0))],
        out_specs=pl.BlockSpec((tt, D), lambda i: (i, 0)),
        out_shape=jax.ShapeDtypeStruct((T, D), BF16), compiler_params=_cp("parallel"),
    )(x, g)


def _rms_bwd(x, g, dh, dres, name):
    T, D = x.shape
    tt = _tile(T, 256, 16)

    def body(x_ref, g_ref, dh_ref, dres_ref, dx_ref, dg_ref):
        @pl.when(pl.program_id(0) == 0)
        def _():
            dg_ref[...] = jnp.zeros_like(dg_ref)

        xv = x_ref[...]
        r = lax.rsqrt(jnp.mean(xv * xv, axis=-1, keepdims=True) + EPS)
        xh = xv * r
        dh_v = dh_ref[...]
        dy = dh_v * g_ref[...]
        dx_ref[...] = dres_ref[...] + r * (dy - xh * jnp.mean(dy * xh, axis=-1, keepdims=True))
        dg_ref[...] += jnp.sum(dh_v * xh, axis=0, keepdims=True)

    row = pl.BlockSpec((tt, D), lambda i: (i, 0))
    vec = pl.BlockSpec((1, D), lambda i: (0, 0))
    return pl.pallas_call(
        body, name=name, grid=(T // tt,), in_specs=[row, vec, row, row], out_specs=[row, vec],
        out_shape=[jax.ShapeDtypeStruct((T, D), F32), jax.ShapeDtypeStruct((1, D), F32)],
        compiler_params=_cp("arbitrary"),
    )(x, g, dh, dres)


def _loss_head(x, g, tgt, name="loss_head"):
    T, D = x.shape
    tt = _tile(T, 256, 16)

    def body(x_ref, g_ref, t_ref, dx_ref, dg_ref, loss_ref):
        @pl.when(pl.program_id(0) == 0)
        def _():
            dg_ref[...] = jnp.zeros_like(dg_ref)
            loss_ref[...] = jnp.zeros_like(loss_ref)

        xv = x_ref[...]
        r = lax.rsqrt(jnp.mean(xv * xv, axis=-1, keepdims=True) + EPS)
        xh = xv * r
        err = xh * g_ref[...] - t_ref[...]
        part = 0.5 * jnp.sum(jnp.mean(err * err, axis=-1, keepdims=True), axis=0, keepdims=True)
        loss_ref[...] += jnp.broadcast_to(part, loss_ref.shape)
        dy = err * (1.0 / D)
        dg_ref[...] += jnp.sum(dy * xh, axis=0, keepdims=True)
        dyh = dy * g_ref[...]
        dx_ref[...] = r * (dyh - xh * jnp.mean(dyh * xh, axis=-1, keepdims=True))

    row = pl.BlockSpec((tt, D), lambda i: (i, 0))
    vec = pl.BlockSpec((1, D), lambda i: (0, 0))
    return pl.pallas_call(
        body, name=name, grid=(T // tt,), in_specs=[row, vec, row],
        out_specs=[row, vec, pl.BlockSpec((1, 128), lambda i: (0, 0))],
        out_shape=[jax.ShapeDtypeStruct((T, D), F32), jax.ShapeDtypeStruct((1, D), F32),
                   jax.ShapeDtypeStruct((1, 128), F32)],
        compiler_params=_cp("arbitrary"),
    )(x, g, tgt)


def _ba_fwd(proj, alog, dtb, oba, name="dn_ba_fwd"):
    T = proj.shape[0]
    tt = _tile(T, 512, 8)

    def body(p_ref, al_ref, dt_ref, o_ref):
        raw = p_ref[...]
        lane = lax.broadcasted_iota(jnp.int32, raw.shape, 1)
        z = raw + dt_ref[...]
        sp = jnp.maximum(z, 0.0) + jnp.log(1.0 + jnp.exp(-jnp.abs(z)))
        gl = -jnp.exp(al_ref[...]) * sp
        o_ref[...] = jnp.where(lane < H, _sig(raw), jnp.where(lane < 2 * H, gl, 0.0))

    vec = pl.BlockSpec((1, 128), lambda i: (0, 0))
    return pl.pallas_call(
        body, name=name, grid=(T // tt,),
        in_specs=[pl.BlockSpec((tt, 128), lambda i: (i, oba // 128)), vec, vec],
        out_specs=pl.BlockSpec((tt, 128), lambda i: (i, 0)),
        out_shape=jax.ShapeDtypeStruct((T, 128), F32), compiler_params=_cp("parallel"),
    )(proj, alog, dtb)


def _ba_bwd(proj, alog, dtb, dbg, oba, name="dn_ba_bwd"):
    T = proj.shape[0]
    tt = _tile(T, 512, 16)

    def body(p_ref, al_ref, dt_ref, d_ref, o_ref, dal_ref, ddt_ref):
        @pl.when(pl.program_id(0) == 0)
        def _():
            dal_ref[...] = jnp.zeros_like(dal_ref)
            ddt_ref[...] = jnp.zeros_like(ddt_ref)

        raw = p_ref[...]
        d = d_ref[...]
        lane = lax.broadcasted_iota(jnp.int32, raw.shape, 1)
        z = raw + dt_ref[...]
        sp = jnp.maximum(z, 0.0) + jnp.log(1.0 + jnp.exp(-jnp.abs(z)))
        na = -jnp.exp(al_ref[...])
        is_g = jnp.logical_and(lane >= H, lane < 2 * H)
        b = _sig(raw)
        dz = jnp.where(is_g, d * na * _sig(z), 0.0)
        o_ref[...] = jnp.where(lane < H, d * b * (1.0 - b), dz).astype(o_ref.dtype)
        dal_ref[...] += jnp.sum(jnp.where(is_g, d * na * sp, 0.0), axis=0, keepdims=True)
        ddt_ref[...] += jnp.sum(dz, axis=0, keepdims=True)

    vec = pl.BlockSpec((1, 128), lambda i: (0, 0))
    return pl.pallas_call(
        body, name=name, grid=(T // tt,),
        in_specs=[pl.BlockSpec((tt, 128), lambda i: (i, oba // 128)), vec, vec,
                  pl.BlockSpec((tt, 128), lambda i: (i, 0))],
        out_specs=[pl.BlockSpec((tt, 128), lambda i: (i, 0)), vec, vec],
        out_shape=[jax.ShapeDtypeStruct((T, 128), BF16), jax.ShapeDtypeStruct((1, 128), F32),
                   jax.ShapeDtypeStruct((1, 128), F32)],
        compiler_params=_cp("arbitrary"),
    )(proj, alog, dtb, dbg)


def _dn_prep_fwd(proj, convw, name="dn_prep_fwd"):
    T = proj.shape[0]
    nblk = 3 * H

    def body(p_ref, w_ref, o_ref):
        j = pl.program_id(0)
        xv = p_ref[...]
        w = w_ref[...]
        c = xv * w[DN_K - 1:DN_K, :]
        for k in range(1, DN_K):
            c = c + _shift_down(xv, k) * w[DN_K - 1 - k:DN_K - k, :]
        s = c * _sig(c)
        r = lax.rsqrt(_lsum(s * s) + EPS)
        o_ref[...] = jnp.where(j < 2 * H, s * r, s)

    return pl.pallas_call(
        body, name=name, grid=(nblk,),
        in_specs=[pl.BlockSpec((T, DH), lambda j: (0, j)), pl.BlockSpec((DN_K, DH), lambda j: (0, j))],
        out_specs=pl.BlockSpec((T, DH), lambda j: (0, j)),
        out_shape=jax.ShapeDtypeStruct((T, 3 * WD), F32), compiler_params=_cp("parallel"),
    )(proj, convw)


def _dn_prep_bwd(proj, convw, dq, dk, dv, name="dn_prep_bwd"):
    T = proj.shape[0]
    nblk = 3 * H

    def body(p_ref, w_ref, dq_ref, dk_ref, dv_ref, dx_ref, dw_ref):
        j = pl.program_id(0)
        xv = p_ref[...]
        w = w_ref[...]
        shifted = [_shift_down(xv, k) for k in range(DN_K)]
        c = shifted[0] * w[DN_K - 1:DN_K, :]
        for k in range(1, DN_K):
            c = c + shifted[k] * w[DN_K - 1 - k:DN_K - k, :]
        s = c * _sig(c)
        r = lax.rsqrt(_lsum(s * s) + EPS)
        y = s * r
        dy = jnp.where(j < H, dq_ref[...], jnp.where(j < 2 * H, dk_ref[...], dv_ref[...]))
        ds = jnp.where(j < 2 * H, r * (dy - y * _lsum(dy * y)), dy)
        dc = ds * _dsilu(c)
        dx = dc * w[DN_K - 1:DN_K, :]
        for k in range(1, DN_K):
            dx = dx + _shift_up(dc, k) * w[DN_K - 1 - k:DN_K - k, :]
        dx_ref[...] = dx.astype(dx_ref.dtype)
        rows = [jnp.sum(dc * shifted[DN_K - 1 - t], axis=0, keepdims=True) for t in range(DN_K)]
        dw_ref[...] = jnp.concatenate(rows, axis=0)

    hb = lambda off: pl.BlockSpec((T, DH), lambda j: (0, jnp.maximum(jnp.minimum(j - off, H - 1), 0)))
    return pl.pallas_call(
        body, name=name, grid=(nblk,),
        in_specs=[pl.BlockSpec((T, DH), lambda j: (0, j)), pl.BlockSpec((DN_K, DH), lambda j: (0, j)),
                  hb(0), hb(H), hb(2 * H)],
        out_specs=[pl.BlockSpec((T, DH), lambda j: (0, j)), pl.BlockSpec((DN_K, DH), lambda j: (0, j))],
        out_shape=[jax.ShapeDtypeStruct((T, 3 * WD), BF16), jax.ShapeDtypeStruct((DN_K, 3 * WD), F32)],
        compiler_params=_cp("parallel"),
    )(proj, convw, dq, dk, dv)


DN_BLOCK = 4


def _split3(a):
    hi = a.astype(BF16)
    r1 = a - hi.astype(F32)
    mid = r1.astype(BF16)
    return hi, mid, (r1 - mid.astype(F32)).astype(BF16)


def _dot3(a, b, dn=NN):
    ah, al, _ = _split3(a)
    bh, bl, _ = _split3(b)
    d = lambda p, q: lax.dot_general(p, q, dn, preferred_element_type=F32)
    return d(ah, bh) + d(ah, bl) + d(al, bh)


def _mask_dot(m, b, dn=NN):
    mb = m.astype(BF16)
    d = lambda q: (lax.dot_general(mb, q, dn, preferred_element_type=F32) if dn != TN
                   else lax.dot_general(q, mb, dn, preferred_element_type=F32))
    b0, b1, b2 = _split3(b)
    return d(b0) + d(b1) + d(b2)


def _tri_inv(A):
    ri = lax.broadcasted_iota(jnp.int32, A.shape, 0)
    ci = lax.broadcasted_iota(jnp.int32, A.shape, 1)
    X = -A
    P = jnp.where(ri == ci, 1.0, 0.0) + X
    Y = X
    for _ in range(int(math.log2(DNC)) - 1):
        Y = _dot3(Y, Y)
        P = P + _dot3(P, Y)
    return P


GH = 4
NG = H // GH
GR = GH * DNC
GK = GH * DH


def _dn_masks():
    ri = lax.broadcasted_iota(jnp.int32, (GR, GR), 0)
    ci = lax.broadcasted_iota(jnp.int32, (GR, GR), 1)
    blk = (ri // DNC) == (ci // DNC)
    wide = (lax.broadcasted_iota(jnp.int32, (GR, GK), 0) // DNC) == (lax.broadcasted_iota(jnp.int32, (GR, GK), 1) // DH)
    return dict(blk=blk, causal=jnp.logical_and(blk, ri >= ci), strict=jnp.logical_and(blk, ri > ci),
                upper=jnp.logical_and(blk, ri <= ci), eye=ri == ci, wide=wide)


def _wide(a, mk):
    return jnp.where(mk["wide"], jnp.tile(a, (1, GH)), 0.0)


def _fold(a, mk):
    a = jnp.where(mk["wide"], a, 0.0)
    out = a[:, :DH]
    for j in range(1, GH):
        out = out + a[:, j * DH:(j + 1) * DH]
    return out


def _stack_heads(ref, rows, g):
    return jnp.concatenate([ref[rows, (g * GH + j) * DH:(g * GH + j + 1) * DH] for j in range(GH)], axis=0)


def _dn_group(q_ref, k_ref, v_ref, rows, bg, gc_cols, g, mk):
    heads = [g * GH + j for j in range(GH)]
    col = lambda a, lane: jnp.concatenate([a[:, lane(h):lane(h) + 1] for h in heads], axis=0)
    q = _stack_heads(q_ref, rows, g) * (DH ** -0.5)
    k = _stack_heads(k_ref, rows, g)
    v = _stack_heads(v_ref, rows, g)
    beta = col(bg, lambda h: h)
    gcol = col(gc_cols, lambda h: H + h)
    last = [gc_cols[DNC - 1:DNC, H + h:H + h + 1] for h in heads]
    gl = jnp.concatenate([jnp.broadcast_to(t, (DNC, 1)) for t in last], axis=0)
    egl_state = jnp.concatenate([jnp.broadcast_to(jnp.exp(t), (DH, 1)) for t in last], axis=0)
    grow = _mask_dot(jnp.ones((GR, GR), F32), jnp.where(mk["eye"], gcol, 0.0))
    dec = jnp.where(mk["causal"], jnp.exp(jnp.where(mk["causal"], gcol - grow, 0.0)), 0.0)
    eg = jnp.exp(gcol)
    ek = jnp.exp(gl - gcol)
    kb = k * beta
    vb = v * beta
    kbe = kb * eg
    A = jnp.where(mk["strict"], _bdot(kb, k, NT) * dec, 0.0)
    P = jnp.where(mk["causal"], _bdot(q, k, NT) * dec, 0.0)
    return dict(q=q, k=k, v=v, beta=beta, dec=dec, eg=eg, ek=ek, egl=jnp.exp(gl), egl_state=egl_state, kb=kb, vb=vb,
                kbe=kbe, A=A, P=P, qd=q * eg, kd=k * ek, heads=heads)


def _gc_cols(bg):
    ri = lax.broadcasted_iota(jnp.int32, (DNC, DNC), 0)
    ci = lax.broadcasted_iota(jnp.int32, (DNC, DNC), 1)
    return _mask_dot(jnp.where(ri >= ci, 1.0, 0.0), bg)


def _dn_core_fwd(qkv, bg, gather=(), name="dn_core_fwd"):
    T = qkv.shape[0]
    n_chunks = T // DNC
    nb = _tile(n_chunks, DN_BLOCK, 1)
    tb = nb * DNC

    ng = len(gather)
    n_steps = n_chunks // nb

    def body(*refs):
        q_ref, k_ref, v_ref, bg_ref = refs[:4]
        o_ref, s_ref, tm_ref = refs[4 + ng:7 + ng]
        S_scr = refs[7 + 2 * ng]
        comm = (refs[4:4 + ng], refs[7 + ng:7 + 2 * ng]) + tuple(refs[8 + 2 * ng:])

        @pl.when(pl.program_id(0) == 0)
        def _():
            S_scr[...] = jnp.zeros_like(S_scr)
            if ng:
                _ag_start(*comm)

        def chunk(n, carry):
            rows = pl.ds(pl.multiple_of(n * DNC, DNC), DNC)
            mk = _dn_masks()
            bgc = bg_ref[rows, :]
            gc_cols = _gc_cols(bgc)
            for g in range(NG):
                c = _dn_group(q_ref, k_ref, v_ref, rows, bgc, gc_cols, g, mk)
                Tm = _tri_inv(c["A"])
                tm_ref[n, g] = Tm
                S = S_scr[g]
                s_ref[n, g] = S
                u = _bdot(Tm, c["vb"])
                w = _bdot(Tm, c["kbe"])
                vn = u - _bdot(_wide(w, mk), S)
                o = _bdot(_wide(c["qd"], mk), S) + _bdot(c["P"], vn)
                for j, h in enumerate(c["heads"]):
                    o_ref[rows, h * DH:(h + 1) * DH] = o[j * DNC:(j + 1) * DNC]
                S_scr[g] = S * c["egl_state"] + _bdot(_wide(c["kd"], mk), vn, TN)
            return carry

        lax.fori_loop(0, nb, chunk, 0)

        if ng:
            @pl.when(pl.program_id(0) == n_steps - 1)
            def _():
                _ag_finish(*comm)

    blk = lambda j: pl.BlockSpec((tb, WD), lambda i: (i, j))
    sp = _ag_specs(gather)
    outs = pl.pallas_call(
        body, name=name + ("_gather" if ng else ""), grid=(n_steps,),
        in_specs=[blk(0), blk(1), blk(2), pl.BlockSpec((tb, 128), lambda i: (i, 0))] + sp["specs"],
        out_specs=[blk(0), pl.BlockSpec((nb, NG, GK, DH), lambda i: (i, 0, 0, 0)),
                   pl.BlockSpec((nb, NG, GR, GR), lambda i: (i, 0, 0, 0))] + sp["specs"],
        out_shape=[jax.ShapeDtypeStruct((T, WD), F32), jax.ShapeDtypeStruct((n_chunks, NG, GK, DH), F32),
                   jax.ShapeDtypeStruct((n_chunks, NG, GR, GR), F32)] + sp["out_shape"],
        scratch_shapes=[pltpu.VMEM((NG, GK, DH), F32)] + (sp["sems"] if ng else []),
        compiler_params=_cp("arbitrary"),
    )(qkv, qkv, qkv, bg, *gather)
    return outs[0], outs[1], outs[2], list(outs[3:])


def _dn_core_bwd(qkv, bg, s_all, tm_all, do, exchange=(), name="dn_core_bwd"):
    T = qkv.shape[0]
    n_chunks = T // DNC
    nb = _tile(n_chunks, DN_BLOCK, 1)
    tb = nb * DNC
    n_blocks = n_chunks // nb

    nx = len(exchange)

    def body(*refs):
        q_ref, k_ref, v_ref, bg_ref, s_ref, tm_ref, do_ref = refs[:7]
        dq_ref, dk_ref, dv_ref, dbg_ref = refs[7 + nx:11 + nx]
        dS_scr = refs[11 + 2 * nx]
        comm = (refs[7:7 + nx], refs[11 + nx:11 + 2 * nx]) + tuple(refs[12 + 2 * nx:])

        @pl.when(pl.program_id(0) == 0)
        def _():
            dS_scr[...] = jnp.zeros_like(dS_scr)
            if nx:
                _rsx_start(*comm)

        lane = lax.broadcasted_iota(jnp.int32, (DNC, 128), 1)
        row = lax.broadcasted_iota(jnp.int32, (GR, 1), 0)

        def chunk(i, carry):
            n = nb - 1 - i
            rows = pl.ds(pl.multiple_of(n * DNC, DNC), DNC)
            mk = _dn_masks()
            ones = jnp.ones((GR, GR), F32)
            blk_f = jnp.where(mk["blk"], 1.0, 0.0)
            wide_f = jnp.where(mk["wide"], 1.0, 0.0)
            per_row = lambda m, a: _mask_dot(m, jnp.broadcast_to(a, (a.shape[0], DH)))[:, :1]
            bgc = bg_ref[rows, :]
            gc_cols = _gc_cols(bgc)
            dbg = jnp.zeros((DNC, 128), F32)
            for g in range(NG):
                c = _dn_group(q_ref, k_ref, v_ref, rows, bgc, gc_cols, g, mk)
                q, k, v, beta = c["q"], c["k"], c["v"], c["beta"]
                dec, eg, ek, egl = c["dec"], c["eg"], c["ek"], c["egl"]
                kb, vb, kbe, A, P, qd, kd = c["kb"], c["vb"], c["kbe"], c["A"], c["P"], c["qd"], c["kd"]
                S = s_ref[n, g]
                Tm = tm_ref[n, g]
                u = _bdot(Tm, vb)
                w = _bdot(Tm, kbe)
                w_wide = _wide(w, mk)
                vn = u - _bdot(w_wide, S)
                d_o = _stack_heads(do_ref, rows, g)
                dS1 = dS_scr[g]
                d_qd = _fold(_bdot(d_o, S, NT), mk)
                dP = jnp.where(mk["causal"], _bdot(d_o, vn, NT), 0.0)
                d_vn = _bdot(P, d_o, TN) + _bdot(_wide(kd, mk), dS1)
                d_kd = _fold(_bdot(vn, dS1, NT), mk)
                d_egl = per_row(wide_f, _lsum(dS1 * S))
                dS_scr[g] = dS1 * c["egl_state"] + _bdot(_wide(qd, mk), d_o, TN) - _bdot(w_wide, d_vn, TN)
                d_w = -_fold(_bdot(d_vn, S, NT), mk)
                d_vb = _bdot(Tm, d_vn, TN)
                d_kbe = _bdot(Tm, d_w, TN)
                dA = jnp.where(mk["strict"], -(_bdot(d_vb, u, NT) + _bdot(d_kbe, w, NT)), 0.0)
                dMA = dA * dec
                dMP = dP * dec
                d_kb = _bdot(dMA, k) + d_kbe * eg
                d_k = _bdot(dMA, kb, TN) + _bdot(dMP, q, TN) + d_kd * ek + d_kb * beta
                d_qs = (_bdot(dMP, k) + d_qd * eg) * (DH ** -0.5)
                d_v = d_vb * beta
                E = dA * A + dP * P
                col_sums = _mask_dot(ones, E, TN)[:, :1]
                t_kd = _lsum(d_kd * kd)
                d_gl = per_row(blk_f, t_kd) + d_egl * egl
                d_gc = (_lsum(E) - col_sums + _lsum(d_qd * qd) + _lsum(d_kbe * kbe) - t_kd
                        + jnp.where(row % DNC == DNC - 1, d_gl, 0.0))
                d_g = per_row(jnp.where(mk["upper"], 1.0, 0.0), d_gc)
                d_beta = _lsum(d_kb * k) + _lsum(d_vb * v)
                for j, h in enumerate(c["heads"]):
                    rs = slice(j * DNC, (j + 1) * DNC)
                    dq_ref[rows, h * DH:(h + 1) * DH] = d_qs[rs]
                    dk_ref[rows, h * DH:(h + 1) * DH] = d_k[rs]
                    dv_ref[rows, h * DH:(h + 1) * DH] = d_v[rs]
                    dbg = dbg + jnp.where(lane == h, d_beta[rs], 0.0) + jnp.where(lane == h + H, d_g[rs], 0.0)
            dbg_ref[rows, :] = dbg
            return carry

        lax.fori_loop(0, nb, chunk, 0)

        if nx:
            @pl.when(pl.program_id(0) == n_blocks - 1)
            def _():
                _rsx_finish(*comm)

    blk = lambda j: pl.BlockSpec((tb, WD), lambda i: (n_blocks - 1 - i, j))
    small = pl.BlockSpec((tb, 128), lambda i: (n_blocks - 1 - i, 0))
    sp = _rsx_specs(exchange)
    outs = pl.pallas_call(
        body, name=name + ("_exchange" if nx else ""), grid=(n_blocks,),
        in_specs=[blk(0), blk(1), blk(2), small,
                  pl.BlockSpec((nb, NG, GK, DH), lambda i: (n_blocks - 1 - i, 0, 0, 0)),
                  pl.BlockSpec((nb, NG, GR, GR), lambda i: (n_blocks - 1 - i, 0, 0, 0)), blk(0)] + sp["specs"],
        out_specs=[blk(0), blk(0), blk(0), small] + sp["specs"],
        out_shape=[jax.ShapeDtypeStruct((T, WD), F32)] * 3 + [jax.ShapeDtypeStruct((T, 128), F32)] + sp["out_shape"],
        scratch_shapes=[pltpu.VMEM((NG, GK, DH), F32)] + (sp["sems"] if nx else []),
        compiler_params=_cp("arbitrary"),
    )(qkv, qkv, qkv, bg, s_all, tm_all, do, *exchange)
    return outs[0], outs[1], outs[2], outs[3], list(outs[4:])


def _dn_post_fwd(o, proj, gon, name="dn_post_fwd"):
    T = o.shape[0]
    tt = _tile(T, 256, 16)

    def body(o_ref, z_ref, g_ref, y_ref):
        for hh in range(H):
            sl = slice(hh * DH, (hh + 1) * DH)
            ov = o_ref[:, sl]
            zv = z_ref[:, sl]
            r = lax.rsqrt(jnp.mean(ov * ov, axis=-1, keepdims=True) + EPS)
            y_ref[:, sl] = (ov * r * g_ref[...] * (zv * _sig(zv))).astype(y_ref.dtype)

    return pl.pallas_call(
        body, name=name, grid=(T // tt,),
        in_specs=[pl.BlockSpec((tt, WD), lambda i: (i, 0)), pl.BlockSpec((tt, WD), lambda i: (i, OZ // WD)),
                  pl.BlockSpec((1, DH), lambda i: (0, 0))],
        out_specs=pl.BlockSpec((tt, WD), lambda i: (i, 0)),
        out_shape=jax.ShapeDtypeStruct((T, WD), BF16), compiler_params=_cp("parallel"),
    )(o, proj, gon)


def _dn_post_bwd(o, proj, gon, dy, name="dn_post_bwd"):
    T = o.shape[0]
    tt = _tile(T, 256, 16)

    def body(o_ref, z_ref, g_ref, dy_ref, do_ref, dz_ref, dg_ref):
        @pl.when(pl.program_id(0) == 0)
        def _():
            dg_ref[...] = jnp.zeros_like(dg_ref)

        acc = jnp.zeros((1, DH), F32)
        for hh in range(H):
            sl = slice(hh * DH, (hh + 1) * DH)
            ov = o_ref[:, sl]
            zv = z_ref[:, sl]
            dyv = dy_ref[:, sl]
            r = lax.rsqrt(jnp.mean(ov * ov, axis=-1, keepdims=True) + EPS)
            oh = ov * r
            nrm = oh * g_ref[...]
            dn = dyv * (zv * _sig(zv))
            dz_ref[:, sl] = (dyv * nrm * _dsilu(zv)).astype(dz_ref.dtype)
            doh = dn * g_ref[...]
            do_ref[:, sl] = r * (doh - oh * jnp.mean(doh * oh, axis=-1, keepdims=True))
            acc = acc + jnp.sum(dn * oh, axis=0, keepdims=True)
        dg_ref[...] += acc

    row = pl.BlockSpec((tt, WD), lambda i: (i, 0))
    vec = pl.BlockSpec((1, DH), lambda i: (0, 0))
    return pl.pallas_call(
        body, name=name, grid=(T // tt,),
        in_specs=[row, pl.BlockSpec((tt, WD), lambda i: (i, OZ // WD)), vec, row],
        out_specs=[row, row, vec],
        out_shape=[jax.ShapeDtypeStruct((T, WD), F32), jax.ShapeDtypeStruct((T, WD), BF16),
                   jax.ShapeDtypeStruct((1, DH), F32)],
        compiler_params=_cp("arbitrary"),
    )(o, proj, gon, dy)


def _sg_common(u_ref, v_ref, lng_ref, lnb_ref):
    ur = u_ref[...]
    vr = v_ref[...]
    vgel = _gelu(vr)
    mu = jnp.mean(vgel, axis=-1, keepdims=True)
    xc = vgel - mu
    rs = lax.rsqrt(jnp.mean(xc * xc, axis=-1, keepdims=True) + EPS)
    xh = xc * rs
    vg = xh * lng_ref[...] + lnb_ref[...]
    return ur, vr, rs, xh, vg


def _sg_fwd(proj, lng, lnb, sgw, sgbt, name="sg_fwd"):
    T = proj.shape[0]

    def body(u_ref, v_ref, lng_ref, lnb_ref, w_ref, bt_ref, y_ref):
        ur, _, _, _, vg = _sg_common(u_ref, v_ref, lng_ref, lnb_ref)
        ri = lax.broadcasted_iota(jnp.int32, (SGC, SGC), 0)
        ci = lax.broadcasted_iota(jnp.int32, (SGC, SGC), 1)
        ug = _gelu(ur)
        for g in range(H):
            sl = slice(g * DH, (g + 1) * DH)
            ws = jnp.where(ri >= ci, w_ref[g], 0.0)
            mixed = _bdot(ws, vg[:, sl]) + bt_ref[:, g:g + 1]
            y_ref[:, sl] = (ug[:, sl] * mixed).astype(y_ref.dtype)

    vec = pl.BlockSpec((1, WD), lambda i: (0, 0))
    return pl.pallas_call(
        body, name=name, grid=(T // SGC,),
        in_specs=[pl.BlockSpec((SGC, WD), lambda i: (i, OU // WD)), pl.BlockSpec((SGC, WD), lambda i: (i, OV // WD)),
                  vec, vec, pl.BlockSpec((H, SGC, SGC), lambda i: (0, 0, 0)),
                  pl.BlockSpec((SGC, H), lambda i: (0, 0))],
        out_specs=pl.BlockSpec((SGC, WD), lambda i: (i, 0)),
        out_shape=jax.ShapeDtypeStruct((T, WD), BF16), compiler_params=_cp("parallel"),
    )(proj, proj, lng, lnb, sgw, sgbt)


def _sg_bwd(proj, lng, lnb, sgw, sgbt, dy, name="sg_bwd"):
    T = proj.shape[0]

    def body(u_ref, v_ref, lng_ref, lnb_ref, w_ref, bt_ref, dy_ref,
             du_ref, dv_ref, dw_ref, dbt_ref, dlng_ref, dlnb_ref):
        @pl.when(pl.program_id(0) == 0)
        def _():
            dw_ref[...] = jnp.zeros_like(dw_ref)
            dbt_ref[...] = jnp.zeros_like(dbt_ref)
            dlng_ref[...] = jnp.zeros_like(dlng_ref)
            dlnb_ref[...] = jnp.zeros_like(dlnb_ref)

        ur, vr, rs, xh, vg = _sg_common(u_ref, v_ref, lng_ref, lnb_ref)
        ri = lax.broadcasted_iota(jnp.int32, (SGC, SGC), 0)
        ci = lax.broadcasted_iota(jnp.int32, (SGC, SGC), 1)
        ug = _gelu(ur)
        dyv = dy_ref[...]
        dbt = jnp.zeros((SGC, 128), F32)
        dvg_parts = []
        for g in range(H):
            sl = slice(g * DH, (g + 1) * DH)
            ws = jnp.where(ri >= ci, w_ref[g], 0.0)
            mixed = _bdot(ws, vg[:, sl]) + bt_ref[:, g:g + 1]
            dyg = dyv[:, sl]
            du_ref[:, sl] = (dyg * mixed * _dgelu(ur[:, sl])).astype(du_ref.dtype)
            dmix = dyg * ug[:, sl]
            dw_ref[g] += jnp.where(ri >= ci, _bdot(dmix, vg[:, sl], NT), 0.0)
            dbt = dbt + jnp.where(ci == g, _lsum(dmix), 0.0)
            dvg_parts.append(_bdot(ws, dmix, TN))
        dbt_ref[...] += dbt
        dvg = jnp.concatenate(dvg_parts, axis=1)
        dlng_ref[...] += jnp.sum(dvg * xh, axis=0, keepdims=True)
        dlnb_ref[...] += jnp.sum(dvg, axis=0, keepdims=True)
        dxh = dvg * lng_ref[...]
        dvgel = rs * (dxh - jnp.mean(dxh, axis=-1, keepdims=True) - xh * jnp.mean(dxh * xh, axis=-1, keepdims=True))
        dv_ref[...] = (dvgel * _dgelu(vr)).astype(dv_ref.dtype)

    vec = pl.BlockSpec((1, WD), lambda i: (0, 0))
    row = pl.BlockSpec((SGC, WD), lambda i: (i, 0))
    return pl.pallas_call(
        body, name=name, grid=(T // SGC,),
        in_specs=[pl.BlockSpec((SGC, WD), lambda i: (i, OU // WD)), pl.BlockSpec((SGC, WD), lambda i: (i, OV // WD)),
                  vec, vec, pl.BlockSpec((H, SGC, SGC), lambda i: (0, 0, 0)),
                  pl.BlockSpec((SGC, H), lambda i: (0, 0)), row],
        out_specs=[row, row, pl.BlockSpec((H, SGC, SGC), lambda i: (0, 0, 0)),
                   pl.BlockSpec((SGC, 128), lambda i: (0, 0)), vec, vec],
        out_shape=[jax.ShapeDtypeStruct((T, WD), BF16), jax.ShapeDtypeStruct((T, WD), BF16),
                   jax.ShapeDtypeStruct((H, SGC, SGC), F32), jax.ShapeDtypeStruct((SGC, 128), F32),
                   jax.ShapeDtypeStruct((1, WD), F32), jax.ShapeDtypeStruct((1, WD), F32)],
        compiler_params=_cp("arbitrary"),
    )(proj, proj, lng, lnb, sgw, sgbt, dy)


def _merge_fwd(proj, yap, ybp, D, name="merge_fwd"):
    T = proj.shape[0]
    tt = _tile(T, 256, 16)

    def body(ga_ref, gb_ref, a_ref, b_ref, o_ref):
        o_ref[...] = (_sig(ga_ref[...]) * a_ref[...] + _sig(gb_ref[...]) * b_ref[...]).astype(o_ref.dtype)

    row = pl.BlockSpec((tt, D), lambda i: (i, 0))
    return pl.pallas_call(
        body, name=name, grid=(T // tt,),
        in_specs=[pl.BlockSpec((tt, D), lambda i: (i, OGA // D)), pl.BlockSpec((tt, D), lambda i: (i, OGA // D + 1)),
                  row, row],
        out_specs=row, out_shape=jax.ShapeDtypeStruct((T, D), BF16), compiler_params=_cp("parallel"),
    )(proj, proj, yap, ybp)


def _merge_bwd(proj, yap, ybp, dm, D, name="merge_bwd"):
    T = proj.shape[0]
    tt = _tile(T, 256, 16)

    def body(ga_ref, gb_ref, a_ref, b_ref, dm_ref, da_ref, db_ref, dga_ref, dgb_ref):
        d = dm_ref[...]
        sa = _sig(ga_ref[...])
        sb = _sig(gb_ref[...])
        da_ref[...] = (d * sa).astype(da_ref.dtype)
        db_ref[...] = (d * sb).astype(db_ref.dtype)
        dga_ref[...] = (d * a_ref[...] * sa * (1.0 - sa)).astype(dga_ref.dtype)
        dgb_ref[...] = (d * b_ref[...] * sb * (1.0 - sb)).astype(dgb_ref.dtype)

    row = pl.BlockSpec((tt, D), lambda i: (i, 0))
    return pl.pallas_call(
        body, name=name, grid=(T // tt,),
        in_specs=[pl.BlockSpec((tt, D), lambda i: (i, OGA // D)), pl.BlockSpec((tt, D), lambda i: (i, OGA // D + 1)),
                  row, row, row],
        out_specs=[row] * 4, out_shape=[jax.ShapeDtypeStruct((T, D), BF16)] * 4,
        compiler_params=_cp("parallel"),
    )(proj, proj, yap, ybp, dm)


def _ffn_act_fwd(gp, up, cw, cb, name="ffn_act_fwd"):
    T, F = gp.shape

    def body(g_ref, u_ref, w_ref, b_ref, o_ref):
        gv = g_ref[...]
        w = w_ref[...]
        c = gv * w[FF_K - 1:FF_K, :] + b_ref[...]
        for k in range(1, FF_K):
            c = c + _shift_down(gv, k) * w[FF_K - 1 - k:FF_K - k, :]
        o_ref[...] = (c * _sig(c) * u_ref[...]).astype(o_ref.dtype)

    col = pl.BlockSpec((T, 128), lambda j: (0, j))
    return pl.pallas_call(
        body, name=name, grid=(F // 128,),
        in_specs=[col, col, pl.BlockSpec((FF_K, 128), lambda j: (0, j)), pl.BlockSpec((1, 128), lambda j: (0, j))],
        out_specs=col, out_shape=jax.ShapeDtypeStruct((T, F), BF16), compiler_params=_cp("parallel"),
    )(gp, up, cw, cb)


def _ffn_act_bwd(gp, up, cw, cb, dact, name="ffn_act_bwd"):
    T, F = gp.shape

    def body(g_ref, u_ref, w_ref, b_ref, d_ref, dg_ref, du_ref, dw_ref, db_ref):
        gv = g_ref[...]
        w = w_ref[...]
        shifted = [_shift_down(gv, k) for k in range(FF_K)]
        c = shifted[0] * w[FF_K - 1:FF_K, :] + b_ref[...]
        for k in range(1, FF_K):
            c = c + shifted[k] * w[FF_K - 1 - k:FF_K - k, :]
        d = d_ref[...]
        du_ref[...] = (d * c * _sig(c)).astype(du_ref.dtype)
        dc = d * u_ref[...] * _dsilu(c)
        dg = dc * w[FF_K - 1:FF_K, :]
        for k in range(1, FF_K):
            dg = dg + _shift_up(dc, k) * w[FF_K - 1 - k:FF_K - k, :]
        dg_ref[...] = dg.astype(dg_ref.dtype)
        rows = [jnp.sum(dc * shifted[FF_K - 1 - t], axis=0, keepdims=True) for t in range(FF_K)]
        dw_ref[...] = jnp.concatenate(rows, axis=0)
        db_ref[...] = jnp.sum(dc, axis=0, keepdims=True)

    col = pl.BlockSpec((T, 128), lambda j: (0, j))
    wspec = pl.BlockSpec((FF_K, 128), lambda j: (0, j))
    bspec = pl.BlockSpec((1, 128), lambda j: (0, j))
    return pl.pallas_call(
        body, name=name, grid=(F // 128,),
        in_specs=[col, col, wspec, bspec, col], out_specs=[col, col, wspec, bspec],
        out_shape=[jax.ShapeDtypeStruct((T, F), BF16), jax.ShapeDtypeStruct((T, F), BF16),
                   jax.ShapeDtypeStruct((FF_K, F), F32), jax.ShapeDtypeStruct((1, F), F32)],
        compiler_params=_cp("parallel"),
    )(gp, up, cw, cb, dact)


def _layer_fwd(x, w, gather=()):
    D = x.shape[1]
    oba = OGA + 2 * D
    h = _rms_fwd(x, w["norm1_g"], "rms1_fwd")
    proj = _mm(h, w["w_in_t"], "nt", F32, name="mm_proj")
    bg = _ba_fwd(proj, w["alog_row"], w["dtb_row"], oba)
    qkv = _dn_prep_fwd(proj, w["dn_conv_w"])
    o, s_all, tm_all, gathered = _dn_core_fwd(qkv, bg, gather)
    ya = _dn_post_fwd(o, proj, w["dn_onorm_g"])
    yb = _sg_fwd(proj, w["sg_ln_g"], w["sg_ln_b"], w["sg_w"], w["sg_bt"])
    yap = _mm(ya, w["w_branch_a"], "nn", F32, name="mm_branch")
    ybp = _mm(yb, w["w_branch_b"], "nn", F32, name="mm_branch")
    merged = _merge_fwd(proj, yap, ybp, D)
    x1 = _mm(merged, w["w_out"], "nn", F32, add=x, name="mm_out")
    h2 = _rms_fwd(x1, w["norm2_g"], "rms2_fwd")
    gp = _mm(h2, w["ffn_w_gate"], "nn", F32, name="mm_ffn_in")
    up = _mm(h2, w["ffn_w_up"], "nn", F32, name="mm_ffn_in")
    act = _ffn_act_fwd(gp, up, w["ffn_conv_w"], w["ffn_conv_b"])
    x2 = _mm(act, w["ffn_w_down"], "nn", F32, add=x1, name="mm_ffn_down")
    saved = dict(x=x, h=h, proj=proj, bg=bg, qkv=qkv, o=o, s_all=s_all, tm_all=tm_all, ya=ya, yb=yb, yap=yap,
                 ybp=ybp, merged=merged, x1=x1, h2=h2, gp=gp, up=up, act=act)
    return x2, saved, gathered


def _layer_bwd(dx2, w, s, exchange=()):
    D = dx2.shape[1]
    oba = OGA + 2 * D
    g = {}
    dx2b = dx2.astype(BF16)
    dact = _mm(dx2b, w["ffn_w_down"], "nt", F32, name="mm_d_act")
    g["ffn_w_down"] = _mm(s["act"], dx2b, "tn", BF16, name="mm_dw_down")
    dgp, dup, g["ffn_conv_w"], g["ffn_conv_b"] = _ffn_act_bwd(s["gp"], s["up"], w["ffn_conv_w"], w["ffn_conv_b"], dact)
    dh2 = _mm(dgp, w["ffn_w_gate"], "nt", F32, name="mm_dh2")
    dh2 = _mm(dup, w["ffn_w_up"], "nt", F32, add=dh2, name="mm_dh2_acc")
    g["ffn_w_gate"] = _mm(s["h2"], dgp, "tn", BF16, name="mm_dw_ffn_in")
    g["ffn_w_up"] = _mm(s["h2"], dup, "tn", BF16, name="mm_dw_ffn_in")
    dx1, g["norm2_g"] = _rms_bwd(s["x1"], w["norm2_g"], dh2, dx2, "rms2_bwd")
    dx1b = dx1.astype(BF16)
    dm = _mm(dx1b, w["w_out"], "nt", F32, name="mm_d_merged")
    g["w_out"] = _mm(s["merged"], dx1b, "tn", BF16, name="mm_dw_out")
    dyap, dybp, dga, dgb = _merge_bwd(s["proj"], s["yap"], s["ybp"], dm, D)
    dya = _mm(dyap, w["w_branch_a"], "nt", F32, name="mm_d_branch")
    dyb = _mm(dybp, w["w_branch_b"], "nt", F32, name="mm_d_branch")
    g["w_branch_a"] = _mm(s["ya"], dyap, "tn", BF16, name="mm_dw_branch")
    g["w_branch_b"] = _mm(s["yb"], dybp, "tn", BF16, name="mm_dw_branch")
    du, dv, g["sg_w"], dbt, g["sg_ln_g"], g["sg_ln_b"] = _sg_bwd(
        s["proj"], w["sg_ln_g"], w["sg_ln_b"], w["sg_w"], w["sg_bt"], dyb)
    g["sg_b"] = jnp.transpose(dbt[:, :H])
    do, dz, g["dn_onorm_g"] = _dn_post_bwd(s["o"], s["proj"], w["dn_onorm_g"], dya)
    dq, dk, dvv, dbg, exchanged = _dn_core_bwd(s["qkv"], s["bg"], s["s_all"], s["tm_all"], do, exchange)
    dqkv, g["dn_conv_w"] = _dn_prep_bwd(s["proj"], w["dn_conv_w"], dq, dk, dvv)
    dba, dal, ddt = _ba_bwd(s["proj"], w["alog_row"], w["dtb_row"], dbg, oba)
    g["dn_a_log"] = dal[0, H:2 * H]
    g["dn_dt_bias"] = ddt[0, H:2 * H]
    dproj = jnp.concatenate([dqkv, dz, du, dv, dga, dgb, dba], axis=1)
    dh = _mm(dproj, w["w_in_t"], "nn", F32, name="mm_dh")
    g["w_in_t"] = _mm(dproj, s["h"], "tn", BF16, name="mm_dw_in")
    dx, g["norm1_g"] = _rms_bwd(s["x"], w["norm1_g"], dh, dx1, "rms1_bwd")
    return dx, g, exchanged


def _local_step(x, tgt, layers, final_g):
    saved = []
    for w in layers:
        x, s, _ = _layer_fwd(x, w)
        saved.append(s)
    dx, dgf, loss = _loss_head(x, final_g, tgt)
    grads = [None] * len(layers)
    for l in reversed(range(len(layers))):
        dx, grads[l], _ = _layer_bwd(dx, layers[l], saved[l])
    return loss[0, 0], dx, grads, dgf


def _w_in_pad(wt):
    c1 = 4 * WD
    return jnp.concatenate([wt[:c1], wt[c1 + 2 * H:], wt[c1:c1 + 2 * H],
                            jnp.zeros((128 - 2 * H, wt.shape[1]), wt.dtype)], axis=0)


def _w_in_unpad(gt):
    c1 = 4 * WD
    n = gt.shape[0] - 128
    return jnp.concatenate([gt[:c1], gt[n:n + 2 * H], gt[c1:n]], axis=0)


def _row128(v, off):
    return jnp.pad(v, (off, 128 - off - v.shape[0]))[None]


def _prep_layer(p):
    return dict(
        norm1_g=p["norm1_g"][None], w_in_t=_w_in_pad(p["w_in_t"]),
        alog_row=_row128(p["dn_a_log"], H), dtb_row=_row128(p["dn_dt_bias"], H),
        dn_conv_w=p["dn_conv_w"], dn_onorm_g=p["dn_onorm_g"][None],
        sg_ln_g=p["sg_ln_g"][None], sg_ln_b=p["sg_ln_b"][None], sg_w=p["sg_w"], sg_bt=jnp.transpose(p["sg_b"]),
        w_branch_a=p["w_branch_a"], w_branch_b=p["w_branch_b"], w_out=p["w_out"], norm2_g=p["norm2_g"][None],
        ffn_w_gate=p["ffn_w_gate"], ffn_w_up=p["ffn_w_up"], ffn_conv_w=p["ffn_conv_w"],
        ffn_conv_b=p["ffn_conv_b"][None], ffn_w_down=p["ffn_w_down"])


HBM_SPEC = pl.BlockSpec(memory_space=pltpu.HBM)


def _coords():
    return lax.axis_index("x"), lax.axis_index("y"), lax.axis_index("c")


def _other_chips(x, y):
    return [(1 - x, y), (x, 1 - y), (1 - x, 1 - y)]


def _remote(src, dst, send_sems, recv_sems, k, dev):
    return pltpu.make_async_remote_copy(src_ref=src, dst_ref=dst, send_sem=send_sems.at[k], recv_sem=recv_sems.at[k],
                                        device_id=dev, device_id_type=MESH)


def _ag_copies(w_refs, o_refs, send_sems, recv_sems):
    x, y, c = _coords()
    me = 2 * x + y
    chips = _other_chips(x, y)

    def ici(k, j, owner):
        chip = chips[j]
        return _remote(w_refs[k].at[c], o_refs[k].at[owner, c], send_sems, recv_sems, 6 * k + j, (chip[0], chip[1], c))

    def d2d(k, j, part):
        owner = 2 * chips[j][0] + chips[j][1]
        return _remote(o_refs[k].at[owner, part], o_refs[k].at[owner, part], send_sems, recv_sems, 6 * k + 3 + j,
                       (x, y, 1 - c))

    n = len(w_refs)
    return me, c, chips, ici, d2d, [(k, j) for k in range(n) for j in range(3)]


def _ag_start(w_refs, o_refs, send_sems, recv_sems):
    me, _, _, ici, _, pairs = _ag_copies(w_refs, o_refs, send_sems, recv_sems)
    for k, j in pairs:
        ici(k, j, me).start()


def _ag_finish(w_refs, o_refs, send_sems, recv_sems):
    me, c, chips, ici, d2d, pairs = _ag_copies(w_refs, o_refs, send_sems, recv_sems)
    for k, j in pairs:
        ici(k, j, 2 * chips[j][0] + chips[j][1]).wait_recv()
        d2d(k, j, c).start()
    for k, j in pairs:
        d2d(k, j, 1 - c).wait_recv()
    for k, j in pairs:
        ici(k, j, me).wait_send()
        d2d(k, j, c).wait_send()


def _ag_specs(ws):
    n = len(ws)
    return dict(out_shape=[jax.ShapeDtypeStruct((N_CHIPS,) + w.shape, w.dtype) for w in ws],
                specs=[HBM_SPEC] * n, sems=[pltpu.SemaphoreType.DMA((6 * n,)), pltpu.SemaphoreType.DMA((6 * n,))])


def _ag_layers(ws):
    n = len(ws)

    def body(*refs):
        _ag_start(refs[:n], refs[n:2 * n], *refs[2 * n:])
        _ag_finish(refs[:n], refs[n:2 * n], *refs[2 * n:])

    sp = _ag_specs(ws)
    return pl.pallas_call(
        body, name="ag_weights", out_shape=sp["out_shape"], in_specs=sp["specs"], out_specs=sp["specs"],
        scratch_shapes=sp["sems"],
    )(*ws)


def _rs_pair_exchange(Gs):
    n = len(Gs)

    def body(*refs):
        g_refs, b_refs = refs[:n], refs[n:2 * n]
        send_sems, recv_sems = refs[2 * n:]
        x, y, c = _coords()
        cps = [_remote(g_refs[k].at[i, 1 - c], b_refs[k].at[i], send_sems, recv_sems, N_CHIPS * k + i, (x, y, 1 - c))
               for k in range(n) for i in range(N_CHIPS)]
        for cp in cps:
            cp.start()
        for cp in cps:
            cp.wait()

    return pl.pallas_call(
        body, name="rs_pair_exchange",
        out_shape=[jax.ShapeDtypeStruct((N_CHIPS,) + g.shape[2:], g.dtype) for g in Gs],
        in_specs=[HBM_SPEC] * n, out_specs=[HBM_SPEC] * n,
        scratch_shapes=[pltpu.SemaphoreType.DMA((N_CHIPS * n,)), pltpu.SemaphoreType.DMA((N_CHIPS * n,))],
    )(*Gs)


def _rs_add_pair(G, B, c, name):
    _, _, R, C = G.shape
    tr = _tile(R, 256, 16)

    def body(c_ref, g_ref, b_ref, o_ref):
        o_ref[0] = (g_ref[0, 0].astype(F32) + b_ref[0].astype(F32)).astype(o_ref.dtype)

    grid_spec = pltpu.PrefetchScalarGridSpec(
        num_scalar_prefetch=1, grid=(N_CHIPS, R // tr),
        in_specs=[pl.BlockSpec((1, 1, tr, C), lambda i, r, c_ref: (i, c_ref[0], r, 0)),
                  pl.BlockSpec((1, tr, C), lambda i, r, c_ref: (i, r, 0))],
        out_specs=pl.BlockSpec((1, tr, C), lambda i, r, c_ref: (i, r, 0)))
    return pl.pallas_call(
        body, name=name, grid_spec=grid_spec, out_shape=jax.ShapeDtypeStruct((N_CHIPS, R, C), G.dtype),
        compiler_params=_cp("parallel", "parallel"),
    )(jnp.reshape(c, (1,)).astype(jnp.int32), G, B)


def _rs_chip_exchange(Ps):
    n = len(Ps)

    def body(*refs):
        _rsx_start(refs[:n], refs[n:2 * n], *refs[2 * n:])
        _rsx_finish(refs[:n], refs[n:2 * n], *refs[2 * n:])

    sp = _rsx_specs(Ps)
    return pl.pallas_call(
        body, name="rs_chip_exchange", out_shape=sp["out_shape"], in_specs=sp["specs"], out_specs=sp["specs"],
        scratch_shapes=sp["sems"],
    )(*Ps)


def _rsx_copies(p_refs, b_refs, send_sems, recv_sems):
    x, y, c = _coords()
    me = 2 * x + y
    chips = _other_chips(x, y)

    def cp(k, j, src_slot, dst_slot):
        return _remote(p_refs[k].at[src_slot], b_refs[k].at[dst_slot], send_sems, recv_sems, 3 * k + j,
                       (chips[j][0], chips[j][1], c))

    return me, chips, cp, [(k, j) for k in range(len(p_refs)) for j in range(3)]


def _rsx_start(p_refs, b_refs, send_sems, recv_sems):
    me, chips, cp, pairs = _rsx_copies(p_refs, b_refs, send_sems, recv_sems)
    for k, j in pairs:
        cp(k, j, 2 * chips[j][0] + chips[j][1], me).start()


def _rsx_finish(p_refs, b_refs, send_sems, recv_sems):
    me, chips, cp, pairs = _rsx_copies(p_refs, b_refs, send_sems, recv_sems)
    for k, j in pairs:
        owner = 2 * chips[j][0] + chips[j][1]
        cp(k, j, owner, owner).wait_recv()
    for k, j in pairs:
        cp(k, j, 2 * chips[j][0] + chips[j][1], me).wait_send()


def _rsx_specs(Ps):
    n = len(Ps)
    return dict(out_shape=[jax.ShapeDtypeStruct(p.shape, p.dtype) for p in Ps], specs=[HBM_SPEC] * n,
                sems=[pltpu.SemaphoreType.DMA((3 * n,)), pltpu.SemaphoreType.DMA((3 * n,))])


def _rs_sum_chips(P, B, me, name):
    _, R, C = P.shape
    tr = _tile(R, 256, 16)

    def body(me_ref, p_ref, b1_ref, b2_ref, b3_ref, o_ref):
        o_ref[...] = ((p_ref[0].astype(F32) + b1_ref[0].astype(F32)) + b2_ref[0].astype(F32)) + b3_ref[0].astype(F32)

    slot = lambda d: pl.BlockSpec((1, tr, C), lambda r, me_ref: ((me_ref[0] + d) % N_CHIPS, r, 0))
    grid_spec = pltpu.PrefetchScalarGridSpec(
        num_scalar_prefetch=1, grid=(R // tr,), in_specs=[slot(0), slot(1), slot(2), slot(3)],
        out_specs=pl.BlockSpec((tr, C), lambda r, me_ref: (r, 0)))
    return pl.pallas_call(
        body, name=name, grid_spec=grid_spec, out_shape=jax.ShapeDtypeStruct((R, C), F32),
        compiler_params=_cp("parallel"),
    )(jnp.reshape(me, (1,)).astype(jnp.int32), P, B, B, B)


def _sum_slots(B, name):
    S, R, C = B.shape
    tr = _tile(R, 256, 16)

    def body(b_ref, o_ref):
        acc = b_ref[0].astype(F32)
        for i in range(1, S):
            acc = acc + b_ref[i].astype(F32)
        o_ref[...] = acc

    return pl.pallas_call(
        body, name=name, grid=(R // tr,), in_specs=[pl.BlockSpec((S, tr, C), lambda r: (0, r, 0))],
        out_specs=pl.BlockSpec((tr, C), lambda r: (r, 0)), out_shape=jax.ShapeDtypeStruct((R, C), F32),
        compiler_params=_cp("parallel"),
    )(B)


def _rs_pair_swap(Rs):
    n = len(Rs)

    def body(*refs):
        r_refs, o_refs = refs[:n], refs[n:2 * n]
        send_sems, recv_sems = refs[2 * n:]
        x, y, c = _coords()
        cps = [_remote(r_refs[k], o_refs[k], send_sems, recv_sems, k, (x, y, 1 - c)) for k in range(n)]
        for cp in cps:
            cp.start()
        for cp in cps:
            cp.wait()

    return pl.pallas_call(
        body, name="rs_pair_swap", out_shape=[jax.ShapeDtypeStruct(r.shape, r.dtype) for r in Rs],
        in_specs=[HBM_SPEC] * n, out_specs=[HBM_SPEC] * n,
        scratch_shapes=[pltpu.SemaphoreType.DMA((n,)), pltpu.SemaphoreType.DMA((n,))],
    )(*Rs)


def _ag8(v):
    R = v.shape[0]

    def body(v_ref, out_ref, send_sems, recv_sems, local_sem):
        x, y, c = _coords()
        me, sib = (x, y, c), (x, y, 1 - c)
        chips = _other_chips(x, y)

        def slot(p):
            return out_ref.at[4 * p[0] + 2 * p[1] + p[2]]

        def copy(k, block, to, src=None):
            return _remote(slot(block) if src is None else src, slot(block), send_sems, recv_sems, k, to)

        mine = pltpu.make_async_copy(v_ref, slot(me), local_sem)
        mine.start()
        first = [copy(0, me, sib, src=v_ref)]
        first += [copy(1 + j, me, (chip[0], chip[1], c), src=v_ref) for j, chip in enumerate(chips)]
        for cp in first:
            cp.start()
        passed = [copy(4 + j, (chip[0], chip[1], c), sib) for j, chip in enumerate(chips)]
        for j, chip in enumerate(chips):
            copy(1 + j, (chip[0], chip[1], c), me).wait_recv()
            passed[j].start()
        copy(0, sib, me).wait_recv()
        for j, chip in enumerate(chips):
            copy(4 + j, (chip[0], chip[1], 1 - c), me).wait_recv()
        for cp in first + passed:
            cp.wait_send()
        mine.wait()

    return pl.pallas_call(
        body, name="ag8_small", out_shape=jax.ShapeDtypeStruct((8, R, 128), v.dtype),
        in_specs=[pl.BlockSpec(memory_space=pltpu.VMEM)], out_specs=pl.BlockSpec(memory_space=pltpu.VMEM),
        scratch_shapes=[pltpu.SemaphoreType.DMA((7,)), pltpu.SemaphoreType.DMA((7,)), pltpu.SemaphoreType.DMA],
        compiler_params=pltpu.CompilerParams(vmem_limit_bytes=VMEM_LIMIT),
    )(v)


def _adamw(w, g, m, v, name):
    L, R, C = w.shape
    tr = _tile(R, 128, 8)

    def body(w_ref, g_ref, m_ref, v_ref, d_ref, mo_ref, vo_ref):
        gv = g_ref[...]
        m2 = ADAM_B1 * m_ref[...] + (1.0 - ADAM_B1) * gv
        v2 = ADAM_B2 * v_ref[...] + (1.0 - ADAM_B2) * jnp.square(gv)
        m_hat = m2 / (1.0 - ADAM_B1 ** ADAM_STEP)
        v_hat = v2 / (1.0 - ADAM_B2 ** ADAM_STEP)
        d_ref[...] = -ADAM_LR * (m_hat / (jnp.sqrt(v_hat) + ADAM_EPS) + ADAM_WD * w_ref[...])
        mo_ref[...] = m2
        vo_ref[...] = v2

    blk = pl.BlockSpec((1, tr, C), lambda l, r: (l, r, 0))
    return pl.pallas_call(
        body, name=name, grid=(L, R // tr), in_specs=[blk] * 4, out_specs=[blk] * 3,
        out_shape=[jax.ShapeDtypeStruct(w.shape, F32)] * 3, compiler_params=_cp("parallel", "parallel"),
    )(w, g, m, v)


def _adamw_halves(w, g_mine, g_other, c, m, v, name):
    L, _, R, C = w.shape
    tr = _tile(R, 128, 8)

    def body(c_ref, w_ref, *rest):
        g_refs = rest[:2 * L]
        m_ref, v_ref, g_ref, d_ref, mo_ref, vo_ref = rest[2 * L:]
        l, h = pl.program_id(0), pl.program_id(1)
        gm, go = g_refs[0][...], g_refs[L][...]
        for i in range(1, L):
            gm = jnp.where(l == i, g_refs[i][...], gm)
            go = jnp.where(l == i, g_refs[L + i][...], go)
        gv = jnp.where(h == c_ref[0], gm, go)[None, None]
        g_ref[...] = gv
        m2 = ADAM_B1 * m_ref[...] + (1.0 - ADAM_B1) * gv
        v2 = ADAM_B2 * v_ref[...] + (1.0 - ADAM_B2) * jnp.square(gv)
        m_hat = m2 / (1.0 - ADAM_B1 ** ADAM_STEP)
        v_hat = v2 / (1.0 - ADAM_B2 ** ADAM_STEP)
        d_ref[...] = -ADAM_LR * (m_hat / (jnp.sqrt(v_hat) + ADAM_EPS) + ADAM_WD * w_ref[...])
        mo_ref[...] = m2
        vo_ref[...] = v2

    blk = pl.BlockSpec((1, 1, tr, C), lambda l, h, r, c_ref: (l, h, r, 0))

    def gblk(i, mine):
        def index(l, h, r, c_ref):
            use = jnp.logical_and(l == i, (h == c_ref[0]) == mine)
            return (jnp.where(use, r, 0), 0)
        return pl.BlockSpec((tr, C), index)

    grid_spec = pltpu.PrefetchScalarGridSpec(
        num_scalar_prefetch=1, grid=(L, 2, R // tr),
        in_specs=[blk] + [gblk(i, True) for i in range(L)] + [gblk(i, False) for i in range(L)] + [blk, blk],
        out_specs=[blk] * 4)
    return pl.pallas_call(
        body, name=name, grid_spec=grid_spec, out_shape=[jax.ShapeDtypeStruct(w.shape, F32)] * 4,
        compiler_params=_cp("parallel", "parallel", "parallel"),
    )(jnp.reshape(c, (1,)).astype(jnp.int32), w, *g_mine, *g_other, m, v)


BIG = ("w_in", "w_branch_a", "w_branch_b", "w_out", "ffn_w_gate", "ffn_w_up", "ffn_w_down")
ROW_SHARDED = ("w_out", "ffn_w_down")
SMALL = ("norm1_g", "dn_conv_w", "dn_a_log", "dn_dt_bias", "dn_onorm_g", "sg_ln_g", "sg_ln_b", "sg_w", "sg_b",
         "norm2_g", "ffn_conv_w", "ffn_conv_b", "final_norm_g")
SMALL_SHARDED = ("dn_conv_w", "ffn_conv_w")


def _pack_rows(arrs, mult):
    flat = jnp.concatenate([jnp.reshape(a, (-1,)) for a in arrs])
    n = flat.shape[0]
    rows = -(-n // (128 * mult)) * mult
    return jnp.reshape(jnp.pad(flat, (0, rows * 128 - n)), (rows, 128))


def _unpack(flat2d, shapes):
    flat = jnp.reshape(flat2d, (-1,))
    out, off = [], 0
    for shp in shapes:
        n = math.prod(shp)
        out.append(jnp.reshape(flat[off:off + n], shp))
        off += n
    return out


def _shards_to_full(a, row_sharded):
    if row_sharded:
        a = jnp.moveaxis(a, 0, 1)
        return jnp.reshape(a, (a.shape[0], a.shape[1] * a.shape[2], a.shape[3]))
    a = jnp.moveaxis(a, 0, 2)
    return jnp.reshape(a, (a.shape[0], a.shape[1], a.shape[2] * a.shape[3]))


def _full_to_shards(a, row_sharded):
    L, R, C = a.shape
    if row_sharded:
        return jnp.moveaxis(jnp.reshape(a, (L, N_CHIPS, R // N_CHIPS, C)), 1, 0)
    return jnp.moveaxis(jnp.reshape(a, (L, R, N_CHIPS, C // N_CHIPS)), 2, 0)


def kernel(x, norm1_g, w_in, dn_conv_w, dn_a_log, dn_dt_bias, dn_onorm_g, sg_ln_g, sg_ln_b, sg_w, sg_b, w_branch_a, w_branch_b, w_out, norm2_g, ffn_w_gate, ffn_w_up, ffn_conv_w, ffn_conv_b, ffn_w_down, final_norm_g, loss_target, m_norm1_g, m_w_in, m_dn_conv_w, m_dn_a_log, m_dn_dt_bias, m_dn_onorm_g, m_sg_ln_g, m_sg_ln_b, m_sg_w, m_sg_b, m_w_branch_a, m_w_branch_b, m_w_out, m_norm2_g, m_ffn_w_gate, m_ffn_w_up, m_ffn_conv_w, m_ffn_conv_b, m_ffn_w_down, m_final_norm_g, v_norm1_g, v_w_in, v_dn_conv_w, v_dn_a_log, v_dn_dt_bias, v_dn_onorm_g, v_sg_ln_g, v_sg_ln_b, v_sg_w, v_sg_b, v_w_branch_a, v_w_branch_b, v_w_out, v_norm2_g, v_ffn_w_gate, v_ffn_w_up, v_ffn_conv_w, v_ffn_conv_b, v_ffn_w_down, v_final_norm_g):
    W = dict(norm1_g=norm1_g, w_in=w_in, dn_conv_w=dn_conv_w, dn_a_log=dn_a_log, dn_dt_bias=dn_dt_bias,
             dn_onorm_g=dn_onorm_g, sg_ln_g=sg_ln_g, sg_ln_b=sg_ln_b, sg_w=sg_w, sg_b=sg_b, w_branch_a=w_branch_a,
             w_branch_b=w_branch_b, w_out=w_out, norm2_g=norm2_g, ffn_w_gate=ffn_w_gate, ffn_w_up=ffn_w_up,
             ffn_conv_w=ffn_conv_w, ffn_conv_b=ffn_conv_b, ffn_w_down=ffn_w_down, final_norm_g=final_norm_g)
    M = dict(norm1_g=m_norm1_g, w_in=m_w_in, dn_conv_w=m_dn_conv_w, dn_a_log=m_dn_a_log, dn_dt_bias=m_dn_dt_bias,
             dn_onorm_g=m_dn_onorm_g, sg_ln_g=m_sg_ln_g, sg_ln_b=m_sg_ln_b, sg_w=m_sg_w, sg_b=m_sg_b,
             w_branch_a=m_w_branch_a, w_branch_b=m_w_branch_b, w_out=m_w_out, norm2_g=m_norm2_g,
             ffn_w_gate=m_ffn_w_gate, ffn_w_up=m_ffn_w_up, ffn_conv_w=m_ffn_conv_w, ffn_conv_b=m_ffn_conv_b,
             ffn_w_down=m_ffn_w_down, final_norm_g=m_final_norm_g)
    V = dict(norm1_g=v_norm1_g, w_in=v_w_in, dn_conv_w=v_dn_conv_w, dn_a_log=v_dn_a_log, dn_dt_bias=v_dn_dt_bias,
             dn_onorm_g=v_dn_onorm_g, sg_ln_g=v_sg_ln_g, sg_ln_b=v_sg_ln_b, sg_w=v_sg_w, sg_b=v_sg_b,
             w_branch_a=v_w_branch_a, w_branch_b=v_w_branch_b, w_out=v_w_out, norm2_g=v_norm2_g,
             ffn_w_gate=v_ffn_w_gate, ffn_w_up=v_ffn_w_up, ffn_conv_w=v_ffn_conv_w, ffn_conv_b=v_ffn_conv_b,
             ffn_w_down=v_ffn_w_down, final_norm_g=v_final_norm_g)
    cx, cy, cc = _coords()
    chip = 2 * cx + cy
    L = w_in.shape[0]

    D = w_in.shape[1]
    cs_in = w_in.shape[2]
    rp_in = -(-cs_in // 128) * 128

    def shard_for_gather(n):
        if n == "w_in":
            return jnp.pad(jnp.swapaxes(W[n], 1, 2).astype(BF16), ((0, 0), (0, rp_in - cs_in), (0, 0)))
        return W[n].astype(BF16)

    mine = {n: shard_for_gather(n) for n in BIG}

    def halves(a, lead=0):
        return jnp.reshape(a, a.shape[:lead] + (2, a.shape[lead] // 2) + a.shape[lead + 1:])

    def gather_args(l):
        return [halves(mine[n][l]) for n in BIG]

    def layer_operands(l, gathered):
        p = {n: W[n][l] for n in W if n not in ("final_norm_g",) + BIG + SMALL_SHARDED}
        for n, a in zip(BIG, gathered):
            a = jnp.reshape(a, (N_CHIPS,) + mine[n].shape[1:])
            parts = [jnp.where(chip == i, mine[n][l], a[i]) for i in range(N_CHIPS)]
            if n == "w_in":
                p["w_in_t"] = jnp.concatenate([q[:cs_in] for q in parts], axis=0)
            else:
                p[n] = jnp.concatenate(parts, axis=0 if n in ROW_SHARDED else 1)
        for n in SMALL_SHARDED:
            p[n] = taps_full[n][l]
        return _prep_layer(p)

    taps = _ag8(_pack_rows([W[n] for n in SMALL_SHARDED], 16))
    tap_shards = [_unpack(taps[2 * i], [W[n].shape for n in SMALL_SHARDED]) for i in range(N_CHIPS)]
    taps_full = {n: jnp.concatenate([tap_shards[i][k] for i in range(N_CHIPS)], axis=-1)
                 for k, n in enumerate(SMALL_SHARDED)}

    def grad_partials(g):
        Gs = []
        for n in BIG:
            if n == "w_in":
                gt = jnp.reshape(_w_in_unpad(g["w_in_t"]), (N_CHIPS, cs_in, D))
                a = jnp.pad(gt, ((0, 0), (0, rp_in - cs_in), (0, 0)))
            elif n in ROW_SHARDED:
                a = jnp.reshape(g[n], (N_CHIPS, g[n].shape[0] // N_CHIPS, g[n].shape[1]))
            else:
                a = jnp.moveaxis(jnp.reshape(g[n], (g[n].shape[0], N_CHIPS, g[n].shape[1] // N_CHIPS)), 1, 0)
            Gs.append(halves(a, 1))
        B1s = _rs_pair_exchange(Gs)
        return [_rs_add_pair(a, b, cc, "rs_add_pair_" + n) for n, a, b in zip(BIG, Gs, B1s)]

    layer0 = layer_operands(0, _ag_layers(gather_args(0)))
    x1, saved0, gathered1 = _layer_fwd(x[0], layer0, gather=gather_args(1))
    layer1 = layer_operands(1, gathered1)
    x2, saved1, _ = _layer_fwd(x1, layer1)
    dx, dgf, loss = _loss_head(x2, final_norm_g[None], loss_target[0])
    loss = loss[0, 0]
    dx, grads1, _ = _layer_bwd(dx, layer1, saved1)
    Ps1 = grad_partials(grads1)
    dx, grads0, B2s1 = _layer_bwd(dx, layer0, saved0, exchange=Ps1)
    grads = [grads0, grads1]
    Ps0 = grad_partials(grads0)
    B2s0 = _rs_chip_exchange(Ps0)
    g_mine = [[_rs_sum_chips(p, b, chip, "rs_sum_chips_" + n) for n, p, b in zip(BIG, Ps, B2s)]
              for Ps, B2s in ((Ps0, B2s0), (Ps1, B2s1))]
    swapped = _rs_pair_swap(g_mine[0] + g_mine[1])
    g_other = [swapped[:len(BIG)], swapped[len(BIG):]]

    small = {n: jnp.stack([g[n] for g in grads]) for n in SMALL if n != "final_norm_g"}
    small["final_norm_g"] = dgf
    shapes = [taps_full[n].shape if n in SMALL_SHARDED else W[n].shape for n in SMALL] + [(1,)]
    sflat = _pack_rows([small[n] for n in SMALL] + [jnp.reshape(loss, (1,))], 16)
    sred = _unpack(_sum_slots(_ag8(sflat), "sum_small"), shapes)
    g_small = dict(zip(SMALL, sred[:-1]))
    loss_total = sred[-1][0]
    for n in SMALL_SHARDED:
        cs = W[n].shape[-1]
        g_small[n] = lax.dynamic_slice_in_dim(g_small[n], chip * cs, cs, axis=-1)

    g_big, delta, new_m, new_v = {}, {}, {}, {}
    for k, n in enumerate(BIG):
        gm, go = [g_mine[l][k] for l in range(L)], [g_other[l][k] for l in range(L)]
        if n == "w_in":
            rows = [jnp.where(cc == 0, jnp.concatenate([a, b]), jnp.concatenate([b, a])) for a, b in zip(gm, go)]
            g_big[n] = jnp.stack([jnp.transpose(r[:cs_in]) for r in rows])
            delta[n], new_m[n], new_v[n] = _adamw(W[n], g_big[n], M[n], V[n], "adamw_" + n)
        else:
            outs = _adamw_halves(halves(W[n], 1), gm, go, cc, halves(M[n], 1), halves(V[n], 1), "adamw_" + n)
            g_big[n], delta[n], new_m[n], new_v[n] = [jnp.reshape(o, W[n].shape) for o in outs]
    s_shapes = [W[n].shape for n in SMALL]
    packed = [_pack_rows([d[n] for n in SMALL], 8) for d in (W, g_small, M, V)]
    outs = _adamw(*[a[None] for a in packed], "adamw_small")
    for d, o in zip((delta, new_m, new_v), outs):
        d.update(zip(SMALL, _unpack(o[0], s_shapes)))

    names = list(W)
    grad_w = {**g_big, **g_small}
    return (loss_total, dx[None], *[grad_w[n] for n in names], *[delta[n] for n in names],
            *[new_m[n] for n in names], *[new_v[n] for n in names])
```

```python
import functools
import math

import jax
import jax.numpy as jnp
from jax import lax
from jax.experimental import pallas as pl
from jax.experimental.pallas import tpu as pltpu

F32 = jnp.float32
BF16 = jnp.bfloat16
MESH = pl.DeviceIdType.MESH

EPS = 1e-6
H = 8
DH = 128
WD = H * DH
DNC = 64
SGC = 128
DN_K = 4
FF_K = 3
DEPTH = 2
N_CHIPS = 4

ADAM_LR = 0.001
ADAM_B1 = 0.9
ADAM_B2 = 0.999
ADAM_EPS = 1e-08
ADAM_WD = 0.01
ADAM_STEP = 10

VMEM_LIMIT = 56 * 1024 * 1024

NN = (((1,), (0,)), ((), ()))
NT = (((1,), (1,)), ((), ()))
TN = (((0,), (0,)), ((), ()))

OQ, OZ, OU, OV, OGA = 0, 3 * WD, 4 * WD, 5 * WD, 6 * WD


def _cp(*sem):
    return pltpu.CompilerParams(dimension_semantics=sem or None, vmem_limit_bytes=VMEM_LIMIT)


def _tile(dim, pref, unit=128):
    if dim <= pref:
        return dim
    t = (pref // unit) * unit
    while t >= unit:
        if dim % t == 0:
            return t
        t -= unit
    return dim


def _hdot(a, b, dn=NN):
    return lax.dot_general(a, b, dn, precision=lax.Precision.HIGHEST, preferred_element_type=F32)


def _bdot(a, b, dn=NN):
    return lax.dot_general(a.astype(BF16), b.astype(BF16), dn, preferred_element_type=F32)


def _lsum(x):
    return jnp.sum(x, axis=1, keepdims=True)


def _sig(x):
    return jax.nn.sigmoid(x)


def _dsilu(x):
    s = _sig(x)
    return s * (1.0 + x * (1.0 - s))


def _erf(x):
    a = jnp.abs(x)
    t = 1.0 / (1.0 + 0.3275911 * a)
    poly = t * (0.254829592 + t * (-0.284496736 + t * (1.421413741 + t * (-1.453152027 + t * 1.061405429))))
    r = 1.0 - poly * jnp.exp(-a * a)
    return jnp.where(x < 0, -r, r)


def _gelu(x):
    return 0.5 * x * (1.0 + _erf(x * (2.0 ** -0.5)))


def _dgelu(x):
    cdf = 0.5 * (1.0 + _erf(x * (2.0 ** -0.5)))
    pdf = jnp.exp(-0.5 * x * x) * (1.0 / math.sqrt(2.0 * math.pi))
    return cdf + x * pdf


def _shift_down(x, k):
    if k == 0:
        return x
    rows = lax.broadcasted_iota(jnp.int32, x.shape, 0)
    return jnp.where(rows >= k, pltpu.roll(x, k, 0), 0.0)


def _shift_up(x, k):
    if k == 0:
        return x
    n = x.shape[0]
    rows = lax.broadcasted_iota(jnp.int32, x.shape, 0)
    return jnp.where(rows < n - k, pltpu.roll(x, n - k, 0), 0.0)


def _comm_fns(comm):
    if not comm:
        return None, None, dict(out_shape=[], specs=[], sems=[]), ()
    kind, arrays = comm
    start, finish, specs = {"gather": (_ag_start, _ag_finish, _ag_specs),
                            "exchange": (_rsx_start, _rsx_finish, _rsx_specs)}[kind]
    return start, finish, specs(arrays), tuple(arrays)


def _mm(a, b, mode, out_dtype, add=None, name="mm", comm=None):
    if mode == "tn":
        K, M = a.shape
    else:
        M, K = a.shape
    N = b.shape[0] if mode == "nt" else b.shape[1]
    tm, tn, tk = _tile(M, 1152), _tile(N, 1536), _tile(K, 2048)
    nk = K // tk
    ni, nj = M // tm, N // tn
    dn = {"nn": NN, "nt": NT, "tn": TN}[mode]
    c_start, c_finish, c_sp, payload = _comm_fns(comm)
    nc = len(payload)
    n_add = 0 if add is None else 1

    def body(*refs):
        a_ref, b_ref = refs[:2]
        add_ref = refs[2] if n_add else None
        c_in = refs[2 + n_add:2 + n_add + nc]
        o_ref = refs[2 + n_add + nc]
        c_out = refs[3 + n_add + nc:3 + n_add + 2 * nc]
        rest = refs[3 + n_add + 2 * nc:]
        acc_ref = rest[0] if nk > 1 else None
        sems = rest[1:] if nk > 1 else rest
        i, j, k = pl.program_id(0), pl.program_id(1), pl.program_id(2)

        if nc:
            @pl.when(jnp.logical_and(jnp.logical_and(i == 0, j == 0), k == 0))
            def _():
                c_start(c_in, c_out, *sems)

        def finish(r):
            if add is not None:
                r = r + add_ref[...]
            o_ref[...] = r.astype(o_ref.dtype)

        part = lax.dot_general(a_ref[...], b_ref[...], dn, preferred_element_type=F32)
        if nk == 1:
            finish(part)
        else:
            @pl.when(k == 0)
            def _():
                acc_ref[...] = part

            @pl.when(k > 0)
            def _():
                acc_ref[...] += part

            @pl.when(k == nk - 1)
            def _():
                finish(acc_ref[...])

        if nc:
            @pl.when(jnp.logical_and(jnp.logical_and(i == ni - 1, j == nj - 1), k == nk - 1))
            def _():
                c_finish(c_in, c_out, *sems)

    a_spec = (pl.BlockSpec((tk, tm), lambda i, j, k: (k, i)) if mode == "tn"
              else pl.BlockSpec((tm, tk), lambda i, j, k: (i, k)))
    b_spec = (pl.BlockSpec((tn, tk), lambda i, j, k: (j, k)) if mode == "nt"
              else pl.BlockSpec((tk, tn), lambda i, j, k: (k, j)))
    o_spec = pl.BlockSpec((tm, tn), lambda i, j, k: (i, j))
    in_specs = [a_spec, b_spec] + ([o_spec] if add is not None else []) + c_sp["specs"]
    args = (a, b) + ((add,) if add is not None else ()) + payload
    outs = pl.pallas_call(
        body, name=name + ("_" + comm[0] if nc else ""), grid=(ni, nj, nk), in_specs=in_specs,
        out_specs=[o_spec] + c_sp["specs"],
        out_shape=[jax.ShapeDtypeStruct((M, N), out_dtype)] + c_sp["out_shape"],
        scratch_shapes=([pltpu.VMEM((tm, tn), F32)] if nk > 1 else []) + c_sp["sems"],
        compiler_params=_cp("arbitrary", "arbitrary", "arbitrary") if nc else _cp("parallel", "parallel", "arbitrary"),
    )(*args)
    return (outs[0], list(outs[1:])) if nc else outs[0]


def _rms_fwd(x, g, name):
    T, D = x.shape
    tt = _tile(T, 256, 16)

    def body(x_ref, g_ref, o_ref):
        xv = x_ref[...]
        r = lax.rsqrt(jnp.mean(xv * xv, axis=-1, keepdims=True) + EPS)
        o_ref[...] = (xv * r * g_ref[...]).astype(o_ref.dtype)

    return pl.pallas_call(
        body, name=name, grid=(T // tt,),
        in_specs=[pl.BlockSpec((tt, D), lambda i: (i, 0)), pl.BlockSpec((1, D), lambda i: (0, 0))],
        out_specs=pl.BlockSpec((tt, D), lambda i: (i, 0)),
        out_shape=jax.ShapeDtypeStruct((T, D), BF16), compiler_params=_cp("parallel"),
    )(x, g)


def _rms_bwd(x, g, dh, dres, name):
    T, D = x.shape
    tt = _tile(T, 256, 16)

    def body(x_ref, g_ref, dh_ref, dres_ref, dx_ref, dg_ref):
        @pl.when(pl.program_id(0) == 0)
        def _():
            dg_ref[...] = jnp.zeros_like(dg_ref)

        xv = x_ref[...]
        r = lax.rsqrt(jnp.mean(xv * xv, axis=-1, keepdims=True) + EPS)
        xh = xv * r
        dh_v = dh_ref[...]
        dy = dh_v * g_ref[...]
        dx_ref[...] = dres_ref[...] + r * (dy - xh * jnp.mean(dy * xh, axis=-1, keepdims=True))
        dg_ref[...] += jnp.sum(dh_v * xh, axis=0, keepdims=True)

    row = pl.BlockSpec((tt, D), lambda i: (i, 0))
    vec = pl.BlockSpec((1, D), lambda i: (0, 0))
    return pl.pallas_call(
        body, name=name, grid=(T // tt,), in_specs=[row, vec, row, row], out_specs=[row, vec],
        out_shape=[jax.ShapeDtypeStruct((T, D), F32), jax.ShapeDtypeStruct((1, D), F32)],
        compiler_params=_cp("arbitrary"),
    )(x, g, dh, dres)


def _loss_head(x, g, tgt, name="loss_head"):
    T, D = x.shape
    tt = _tile(T, 256, 16)

    def body(x_ref, g_ref, t_ref, dx_ref, dg_ref, loss_ref):
        @pl.when(pl.program_id(0) == 0)
        def _():
            dg_ref[...] = jnp.zeros_like(dg_ref)
            loss_ref[...] = jnp.zeros_like(loss_ref)

        xv = x_ref[...]
        r = lax.rsqrt(jnp.mean(xv * xv, axis=-1, keepdims=True) + EPS)
        xh = xv * r
        err = xh * g_ref[...] - t_ref[...]
        part = 0.5 * jnp.sum(jnp.mean(err * err, axis=-1, keepdims=True), axis=0, keepdims=True)
        loss_ref[...] += jnp.broadcast_to(part, loss_ref.shape)
        dy = err * (1.0 / D)
        dg_ref[...] += jnp.sum(dy * xh, axis=0, keepdims=True)
        dyh = dy * g_ref[...]
        dx_ref[...] = r * (dyh - xh * jnp.mean(dyh * xh, axis=-1, keepdims=True))

    row = pl.BlockSpec((tt, D), lambda i: (i, 0))
    vec = pl.BlockSpec((1, D), lambda i: (0, 0))
    return pl.pallas_call(
        body, name=name, grid=(T // tt,), in_specs=[row, vec, row],
        out_specs=[row, vec, pl.BlockSpec((1, 128), lambda i: (0, 0))],
        out_shape=[jax.ShapeDtypeStruct((T, D), F32), jax.ShapeDtypeStruct((1, D), F32),
                   jax.ShapeDtypeStruct((1, 128), F32)],
        compiler_params=_cp("arbitrary"),
    )(x, g, tgt)


def _ba_fwd(proj, alog, dtb, oba, name="dn_ba_fwd"):
    T = proj.shape[0]
    tt = _tile(T, 512, 8)

    def body(p_ref, al_ref, dt_ref, o_ref):
        raw = p_ref[...]
        lane = lax.broadcasted_iota(jnp.int32, raw.shape, 1)
        z = raw + dt_ref[...]
        sp = jnp.maximum(z, 0.0) + jnp.log(1.0 + jnp.exp(-jnp.abs(z)))
        gl = -jnp.exp(al_ref[...]) * sp
        o_ref[...] = jnp.where(lane < H, _sig(raw), jnp.where(lane < 2 * H, gl, 0.0))

    vec = pl.BlockSpec((1, 128), lambda i: (0, 0))
    return pl.pallas_call(
        body, name=name, grid=(T // tt,),
        in_specs=[pl.BlockSpec((tt, 128), lambda i: (i, oba // 128)), vec, vec],
        out_specs=pl.BlockSpec((tt, 128), lambda i: (i, 0)),
        out_shape=jax.ShapeDtypeStruct((T, 128), F32), compiler_params=_cp("parallel"),
    )(proj, alog, dtb)


def _ba_bwd(proj, alog, dtb, dbg, oba, name="dn_ba_bwd"):
    T = proj.shape[0]
    tt = _tile(T, 512, 16)

    def body(p_ref, al_ref, dt_ref, d_ref, o_ref, dal_ref, ddt_ref):
        @pl.when(pl.program_id(0) == 0)
        def _():
            dal_ref[...] = jnp.zeros_like(dal_ref)
            ddt_ref[...] = jnp.zeros_like(ddt_ref)

        raw = p_ref[...]
        d = d_ref[...]
        lane = lax.broadcasted_iota(jnp.int32, raw.shape, 1)
        z = raw + dt_ref[...]
        sp = jnp.maximum(z, 0.0) + jnp.log(1.0 + jnp.exp(-jnp.abs(z)))
        na = -jnp.exp(al_ref[...])
        is_g = jnp.logical_and(lane >= H, lane < 2 * H)
        b = _sig(raw)
        dz = jnp.where(is_g, d * na * _sig(z), 0.0)
        o_ref[...] = jnp.where(lane < H, d * b * (1.0 - b), dz).astype(o_ref.dtype)
        dal_ref[...] += jnp.sum(jnp.where(is_g, d * na * sp, 0.0), axis=0, keepdims=True)
        ddt_ref[...] += jnp.sum(dz, axis=0, keepdims=True)

    vec = pl.BlockSpec((1, 128), lambda i: (0, 0))
    return pl.pallas_call(
        body, name=name, grid=(T // tt,),
        in_specs=[pl.BlockSpec((tt, 128), lambda i: (i, oba // 128)), vec, vec,
                  pl.BlockSpec((tt, 128), lambda i: (i, 0))],
        out_specs=[pl.BlockSpec((tt, 128), lambda i: (i, 0)), vec, vec],
        out_shape=[jax.ShapeDtypeStruct((T, 128), BF16), jax.ShapeDtypeStruct((1, 128), F32),
                   jax.ShapeDtypeStruct((1, 128), F32)],
        compiler_params=_cp("arbitrary"),
    )(proj, alog, dtb, dbg)


def _dn_prep_fwd(proj, convw, name="dn_prep_fwd"):
    T = proj.shape[0]
    nblk = 3 * H

    def body(p_ref, w_ref, o_ref):
        j = pl.program_id(0)
        xv = p_ref[...]
        w = w_ref[...]
        c = xv * w[DN_K - 1:DN_K, :]
        for k in range(1, DN_K):
            c = c + _shift_down(xv, k) * w[DN_K - 1 - k:DN_K - k, :]
        s = c * _sig(c)
        r = lax.rsqrt(_lsum(s * s) + EPS)
        o_ref[...] = jnp.where(j < 2 * H, s * r, s)

    return pl.pallas_call(
        body, name=name, grid=(nblk,),
        in_specs=[pl.BlockSpec((T, DH), lambda j: (0, j)), pl.BlockSpec((DN_K, DH), lambda j: (0, j))],
        out_specs=pl.BlockSpec((T, DH), lambda j: (0, j)),
        out_shape=jax.ShapeDtypeStruct((T, 3 * WD), F32), compiler_params=_cp("parallel"),
    )(proj, convw)


def _dn_prep_bwd(proj, convw, dq, dk, dv, name="dn_prep_bwd"):
    T = proj.shape[0]
    nblk = 3 * H

    def body(p_ref, w_ref, dq_ref, dk_ref, dv_ref, dx_ref, dw_ref):
        j = pl.program_id(0)
        xv = p_ref[...]
        w = w_ref[...]
        shifted = [_shift_down(xv, k) for k in range(DN_K)]
        c = shifted[0] * w[DN_K - 1:DN_K, :]
        for k in range(1, DN_K):
            c = c + shifted[k] * w[DN_K - 1 - k:DN_K - k, :]
        s = c * _sig(c)
        r = lax.rsqrt(_lsum(s * s) + EPS)
        y = s * r
        dy = jnp.where(j < H, dq_ref[...], jnp.where(j < 2 * H, dk_ref[...], dv_ref[...]))
        ds = jnp.where(j < 2 * H, r * (dy - y * _lsum(dy * y)), dy)
        dc = ds * _dsilu(c)
        dx = dc * w[DN_K - 1:DN_K, :]
        for k in range(1, DN_K):
            dx = dx + _shift_up(dc, k) * w[DN_K - 1 - k:DN_K - k, :]
        dx_ref[...] = dx.astype(dx_ref.dtype)
        rows = [jnp.sum(dc * shifted[DN_K - 1 - t], axis=0, keepdims=True) for t in range(DN_K)]
        dw_ref[...] = jnp.concatenate(rows, axis=0)

    hb = lambda off: pl.BlockSpec((T, DH), lambda j: (0, jnp.maximum(jnp.minimum(j - off, H - 1), 0)))
    return pl.pallas_call(
        body, name=name, grid=(nblk,),
        in_specs=[pl.BlockSpec((T, DH), lambda j: (0, j)), pl.BlockSpec((DN_K, DH), lambda j: (0, j)),
                  hb(0), hb(H), hb(2 * H)],
        out_specs=[pl.BlockSpec((T, DH), lambda j: (0, j)), pl.BlockSpec((DN_K, DH), lambda j: (0, j))],
        out_shape=[jax.ShapeDtypeStruct((T, 3 * WD), BF16), jax.ShapeDtypeStruct((DN_K, 3 * WD), F32)],
        compiler_params=_cp("parallel"),
    )(proj, convw, dq, dk, dv)


DN_BLOCK = 4


def _split3(a):
    hi = a.astype(BF16)
    r1 = a - hi.astype(F32)
    mid = r1.astype(BF16)
    return hi, mid, (r1 - mid.astype(F32)).astype(BF16)


def _dot3(a, b, dn=NN):
    ah, al, _ = _split3(a)
    bh, bl, _ = _split3(b)
    d = lambda p, q: lax.dot_general(p, q, dn, preferred_element_type=F32)
    return d(ah, bh) + d(ah, bl) + d(al, bh)


def _mask_dot(m, b, dn=NN):
    mb = m.astype(BF16)
    d = lambda q: (lax.dot_general(mb, q, dn, preferred_element_type=F32) if dn != TN
                   else lax.dot_general(q, mb, dn, preferred_element_type=F32))
    b0, b1, b2 = _split3(b)
    return d(b0) + d(b1) + d(b2)


def _tri_inv(A):
    ri = lax.broadcasted_iota(jnp.int32, A.shape, 0)
    ci = lax.broadcasted_iota(jnp.int32, A.shape, 1)
    X = -A
    P = jnp.where(ri == ci, 1.0, 0.0) + X
    Y = X
    for _ in range(int(math.log2(DNC)) - 1):
        Y = _dot3(Y, Y)
        P = P + _dot3(P, Y)
    return P


GH = 4
NG = H // GH
GR = GH * DNC
GK = GH * DH


def _dn_masks():
    ri = lax.broadcasted_iota(jnp.int32, (GR, GR), 0)
    ci = lax.broadcasted_iota(jnp.int32, (GR, GR), 1)
    blk = (ri // DNC) == (ci // DNC)
    wide = (lax.broadcasted_iota(jnp.int32, (GR, GK), 0) // DNC) == (lax.broadcasted_iota(jnp.int32, (GR, GK), 1) // DH)
    return dict(blk=blk, causal=jnp.logical_and(blk, ri >= ci), strict=jnp.logical_and(blk, ri > ci),
                upper=jnp.logical_and(blk, ri <= ci), eye=ri == ci, wide=wide)


def _wide(a, mk):
    return jnp.where(mk["wide"], jnp.tile(a, (1, GH)), 0.0)


def _fold(a, mk):
    a = jnp.where(mk["wide"], a, 0.0)
    out = a[:, :DH]
    for j in range(1, GH):
        out = out + a[:, j * DH:(j + 1) * DH]
    return out


def _stack_heads(ref, rows, g):
    return jnp.concatenate([ref[rows, (g * GH + j) * DH:(g * GH + j + 1) * DH] for j in range(GH)], axis=0)


def _dn_group(q_ref, k_ref, v_ref, rows, bg, gc_cols, g, mk):
    heads = [g * GH + j for j in range(GH)]
    col = lambda a, lane: jnp.concatenate([a[:, lane(h):lane(h) + 1] for h in heads], axis=0)
    q = _stack_heads(q_ref, rows, g) * (DH ** -0.5)
    k = _stack_heads(k_ref, rows, g)
    v = _stack_heads(v_ref, rows, g)
    beta = col(bg, lambda h: h)
    gcol = col(gc_cols, lambda h: H + h)
    last = [gc_cols[DNC - 1:DNC, H + h:H + h + 1] for h in heads]
    gl = jnp.concatenate([jnp.broadcast_to(t, (DNC, 1)) for t in last], axis=0)
    egl_state = jnp.concatenate([jnp.broadcast_to(jnp.exp(t), (DH, 1)) for t in last], axis=0)
    grow = _mask_dot(jnp.ones((GR, GR), F32), jnp.where(mk["eye"], gcol, 0.0))
    dec = jnp.where(mk["causal"], jnp.exp(jnp.where(mk["causal"], gcol - grow, 0.0)), 0.0)
    eg = jnp.exp(gcol)
    ek = jnp.exp(gl - gcol)
    kb = k * beta
    vb = v * beta
    kbe = kb * eg
    A = jnp.where(mk["strict"], _bdot(kb, k, NT) * dec, 0.0)
    P = jnp.where(mk["causal"], _bdot(q, k, NT) * dec, 0.0)
    return dict(q=q, k=k, v=v, beta=beta, dec=dec, eg=eg, ek=ek, egl=jnp.exp(gl), egl_state=egl_state, kb=kb, vb=vb,
                kbe=kbe, A=A, P=P, qd=q * eg, kd=k * ek, heads=heads)


def _gc_cols(bg):
    ri = lax.broadcasted_iota(jnp.int32, (DNC, DNC), 0)
    ci = lax.broadcasted_iota(jnp.int32, (DNC, DNC), 1)
    return _mask_dot(jnp.where(ri >= ci, 1.0, 0.0), bg)


def _dn_core_fwd(qkv, bg, comm=None, name="dn_core_fwd"):
    c_start, c_finish, sp, gather = _comm_fns(comm)
    T = qkv.shape[0]
    n_chunks = T // DNC
    nb = _tile(n_chunks, DN_BLOCK, 1)
    tb = nb * DNC

    ng = len(gather)
    n_steps = n_chunks // nb

    def body(*refs):
        q_ref, k_ref, v_ref, bg_ref = refs[:4]
        o_ref, s_ref, tm_ref = refs[4 + ng:7 + ng]
        S_scr = refs[7 + 2 * ng]
        comm_refs = (refs[4:4 + ng], refs[7 + ng:7 + 2 * ng]) + tuple(refs[8 + 2 * ng:])

        @pl.when(pl.program_id(0) == 0)
        def _():
            S_scr[...] = jnp.zeros_like(S_scr)
            if ng:
                c_start(*comm_refs)

        def chunk(n, carry):
            rows = pl.ds(pl.multiple_of(n * DNC, DNC), DNC)
            mk = _dn_masks()
            bgc = bg_ref[rows, :]
            gc_cols = _gc_cols(bgc)
            for g in range(NG):
                c = _dn_group(q_ref, k_ref, v_ref, rows, bgc, gc_cols, g, mk)
                Tm = _tri_inv(c["A"])
                tm_ref[n, g] = Tm
                S = S_scr[g]
                s_ref[n, g] = S
                u = _bdot(Tm, c["vb"])
                w = _bdot(Tm, c["kbe"])
                vn = u - _bdot(_wide(w, mk), S)
                o = _bdot(_wide(c["qd"], mk), S) + _bdot(c["P"], vn)
                for j, h in enumerate(c["heads"]):
                    o_ref[rows, h * DH:(h + 1) * DH] = o[j * DNC:(j + 1) * DNC]
                S_scr[g] = S * c["egl_state"] + _bdot(_wide(c["kd"], mk), vn, TN)
            return carry

        lax.fori_loop(0, nb, chunk, 0)

        if ng:
            @pl.when(pl.program_id(0) == n_steps - 1)
            def _():
                c_finish(*comm_refs)

    blk = lambda j: pl.BlockSpec((tb, WD), lambda i: (i, j))
    outs = pl.pallas_call(
        body, name=name + ("_" + comm[0] if ng else ""), grid=(n_steps,),
        in_specs=[blk(0), blk(1), blk(2), pl.BlockSpec((tb, 128), lambda i: (i, 0))] + sp["specs"],
        out_specs=[blk(0), pl.BlockSpec((nb, NG, GK, DH), lambda i: (i, 0, 0, 0)),
                   pl.BlockSpec((nb, NG, GR, GR), lambda i: (i, 0, 0, 0))] + sp["specs"],
        out_shape=[jax.ShapeDtypeStruct((T, WD), F32), jax.ShapeDtypeStruct((n_chunks, NG, GK, DH), F32),
                   jax.ShapeDtypeStruct((n_chunks, NG, GR, GR), F32)] + sp["out_shape"],
        scratch_shapes=[pltpu.VMEM((NG, GK, DH), F32)] + (sp["sems"] if ng else []),
        compiler_params=_cp("arbitrary"),
    )(qkv, qkv, qkv, bg, *gather)
    return outs[0], outs[1], outs[2], list(outs[3:])


def _dn_core_bwd(qkv, bg, s_all, tm_all, do, comm=None, name="dn_core_bwd"):
    c_start, c_finish, sp, exchange = _comm_fns(comm)
    T = qkv.shape[0]
    n_chunks = T // DNC
    nb = _tile(n_chunks, DN_BLOCK, 1)
    tb = nb * DNC
    n_blocks = n_chunks // nb

    nx = len(exchange)

    def body(*refs):
        q_ref, k_ref, v_ref, bg_ref, s_ref, tm_ref, do_ref = refs[:7]
        dq_ref, dk_ref, dv_ref, dbg_ref = refs[7 + nx:11 + nx]
        dS_scr = refs[11 + 2 * nx]
        comm_refs = (refs[7:7 + nx], refs[11 + nx:11 + 2 * nx]) + tuple(refs[12 + 2 * nx:])

        @pl.when(pl.program_id(0) == 0)
        def _():
            dS_scr[...] = jnp.zeros_like(dS_scr)
            if nx:
                c_start(*comm_refs)

        lane = lax.broadcasted_iota(jnp.int32, (DNC, 128), 1)
        row = lax.broadcasted_iota(jnp.int32, (GR, 1), 0)

        def chunk(i, carry):
            n = nb - 1 - i
            rows = pl.ds(pl.multiple_of(n * DNC, DNC), DNC)
            mk = _dn_masks()
            ones = jnp.ones((GR, GR), F32)
            blk_f = jnp.where(mk["blk"], 1.0, 0.0)
            wide_f = jnp.where(mk["wide"], 1.0, 0.0)
            per_row = lambda m, a: _mask_dot(m, jnp.broadcast_to(a, (a.shape[0], DH)))[:, :1]
            bgc = bg_ref[rows, :]
            gc_cols = _gc_cols(bgc)
            dbg = jnp.zeros((DNC, 128), F32)
            for g in range(NG):
                c = _dn_group(q_ref, k_ref, v_ref, rows, bgc, gc_cols, g, mk)
                q, k, v, beta = c["q"], c["k"], c["v"], c["beta"]
                dec, eg, ek, egl = c["dec"], c["eg"], c["ek"], c["egl"]
                kb, vb, kbe, A, P, qd, kd = c["kb"], c["vb"], c["kbe"], c["A"], c["P"], c["qd"], c["kd"]
                S = s_ref[n, g]
                Tm = tm_ref[n, g]
                u = _bdot(Tm, vb)
                w = _bdot(Tm, kbe)
                w_wide = _wide(w, mk)
                vn = u - _bdot(w_wide, S)
                d_o = _stack_heads(do_ref, rows, g)
                dS1 = dS_scr[g]
                d_qd = _fold(_bdot(d_o, S, NT), mk)
                dP = jnp.where(mk["causal"], _bdot(d_o, vn, NT), 0.0)
                d_vn = _bdot(P, d_o, TN) + _bdot(_wide(kd, mk), dS1)
                d_kd = _fold(_bdot(vn, dS1, NT), mk)
                d_egl = per_row(wide_f, _lsum(dS1 * S))
                dS_scr[g] = dS1 * c["egl_state"] + _bdot(_wide(qd, mk), d_o, TN) - _bdot(w_wide, d_vn, TN)
                d_w = -_fold(_bdot(d_vn, S, NT), mk)
                d_vb = _bdot(Tm, d_vn, TN)
                d_kbe = _bdot(Tm, d_w, TN)
                dA = jnp.where(mk["strict"], -(_bdot(d_vb, u, NT) + _bdot(d_kbe, w, NT)), 0.0)
                dMA = dA * dec
                dMP = dP * dec
                d_kb = _bdot(dMA, k) + d_kbe * eg
                d_k = _bdot(dMA, kb, TN) + _bdot(dMP, q, TN) + d_kd * ek + d_kb * beta
                d_qs = (_bdot(dMP, k) + d_qd * eg) * (DH ** -0.5)
                d_v = d_vb * beta
                E = dA * A + dP * P
                col_sums = _mask_dot(ones, E, TN)[:, :1]
                t_kd = _lsum(d_kd * kd)
                d_gl = per_row(blk_f, t_kd) + d_egl * egl
                d_gc = (_lsum(E) - col_sums + _lsum(d_qd * qd) + _lsum(d_kbe * kbe) - t_kd
                        + jnp.where(row % DNC == DNC - 1, d_gl, 0.0))
                d_g = per_row(jnp.where(mk["upper"], 1.0, 0.0), d_gc)
                d_beta = _lsum(d_kb * k) + _lsum(d_vb * v)
                for j, h in enumerate(c["heads"]):
                    rs = slice(j * DNC, (j + 1) * DNC)
                    dq_ref[rows, h * DH:(h + 1) * DH] = d_qs[rs]
                    dk_ref[rows, h * DH:(h + 1) * DH] = d_k[rs]
                    dv_ref[rows, h * DH:(h + 1) * DH] = d_v[rs]
                    dbg = dbg + jnp.where(lane == h, d_beta[rs], 0.0) + jnp.where(lane == h + H, d_g[rs], 0.0)
            dbg_ref[rows, :] = dbg
            return carry

        lax.fori_loop(0, nb, chunk, 0)

        if nx:
            @pl.when(pl.program_id(0) == n_blocks - 1)
            def _():
                c_finish(*comm_refs)

    blk = lambda j: pl.BlockSpec((tb, WD), lambda i: (n_blocks - 1 - i, j))
    small = pl.BlockSpec((tb, 128), lambda i: (n_blocks - 1 - i, 0))
    outs = pl.pallas_call(
        body, name=name + ("_" + comm[0] if nx else ""), grid=(n_blocks,),
        in_specs=[blk(0), blk(1), blk(2), small,
                  pl.BlockSpec((nb, NG, GK, DH), lambda i: (n_blocks - 1 - i, 0, 0, 0)),
                  pl.BlockSpec((nb, NG, GR, GR), lambda i: (n_blocks - 1 - i, 0, 0, 0)), blk(0)] + sp["specs"],
        out_specs=[blk(0), blk(0), blk(0), small] + sp["specs"],
        out_shape=[jax.ShapeDtypeStruct((T, WD), F32)] * 3 + [jax.ShapeDtypeStruct((T, 128), F32)] + sp["out_shape"],
        scratch_shapes=[pltpu.VMEM((NG, GK, DH), F32)] + (sp["sems"] if nx else []),
        compiler_params=_cp("arbitrary"),
    )(qkv, qkv, qkv, bg, s_all, tm_all, do, *exchange)
    return outs[0], outs[1], outs[2], outs[3], list(outs[4:])


def _dn_post_fwd(o, proj, gon, name="dn_post_fwd"):
    T = o.shape[0]
    tt = _tile(T, 256, 16)

    def body(o_ref, z_ref, g_ref, y_ref):
        for hh in range(H):
            sl = slice(hh * DH, (hh + 1) * DH)
            ov = o_ref[:, sl]
            zv = z_ref[:, sl]
            r = lax.rsqrt(jnp.mean(ov * ov, axis=-1, keepdims=True) + EPS)
            y_ref[:, sl] = (ov * r * g_ref[...] * (zv * _sig(zv))).astype(y_ref.dtype)

    return pl.pallas_call(
        body, name=name, grid=(T // tt,),
        in_specs=[pl.BlockSpec((tt, WD), lambda i: (i, 0)), pl.BlockSpec((tt, WD), lambda i: (i, OZ // WD)),
                  pl.BlockSpec((1, DH), lambda i: (0, 0))],
        out_specs=pl.BlockSpec((tt, WD), lambda i: (i, 0)),
        out_shape=jax.ShapeDtypeStruct((T, WD), BF16), compiler_params=_cp("parallel"),
    )(o, proj, gon)


def _dn_post_bwd(o, proj, gon, dy, name="dn_post_bwd"):
    T = o.shape[0]
    tt = _tile(T, 256, 16)

    def body(o_ref, z_ref, g_ref, dy_ref, do_ref, dz_ref, dg_ref):
        @pl.when(pl.program_id(0) == 0)
        def _():
            dg_ref[...] = jnp.zeros_like(dg_ref)

        acc = jnp.zeros((1, DH), F32)
        for hh in range(H):
            sl = slice(hh * DH, (hh + 1) * DH)
            ov = o_ref[:, sl]
            zv = z_ref[:, sl]
            dyv = dy_ref[:, sl]
            r = lax.rsqrt(jnp.mean(ov * ov, axis=-1, keepdims=True) + EPS)
            oh = ov * r
            nrm = oh * g_ref[...]
            dn = dyv * (zv * _sig(zv))
            dz_ref[:, sl] = (dyv * nrm * _dsilu(zv)).astype(dz_ref.dtype)
            doh = dn * g_ref[...]
            do_ref[:, sl] = r * (doh - oh * jnp.mean(doh * oh, axis=-1, keepdims=True))
            acc = acc + jnp.sum(dn * oh, axis=0, keepdims=True)
        dg_ref[...] += acc

    row = pl.BlockSpec((tt, WD), lambda i: (i, 0))
    vec = pl.BlockSpec((1, DH), lambda i: (0, 0))
    return pl.pallas_call(
        body, name=name, grid=(T // tt,),
        in_specs=[row, pl.BlockSpec((tt, WD), lambda i: (i, OZ // WD)), vec, row],
        out_specs=[row, row, vec],
        out_shape=[jax.ShapeDtypeStruct((T, WD), F32), jax.ShapeDtypeStruct((T, WD), BF16),
                   jax.ShapeDtypeStruct((1, DH), F32)],
        compiler_params=_cp("arbitrary"),
    )(o, proj, gon, dy)


def _sg_common(u_ref, v_ref, lng_ref, lnb_ref):
    ur = u_ref[...]
    vr = v_ref[...]
    vgel = _gelu(vr)
    mu = jnp.mean(vgel, axis=-1, keepdims=True)
    xc = vgel - mu
    rs = lax.rsqrt(jnp.mean(xc * xc, axis=-1, keepdims=True) + EPS)
    xh = xc * rs
    vg = xh * lng_ref[...] + lnb_ref[...]
    return ur, vr, rs, xh, vg


def _sg_fwd(proj, lng, lnb, sgw, sgbt, name="sg_fwd"):
    T = proj.shape[0]

    def body(u_ref, v_ref, lng_ref, lnb_ref, w_ref, bt_ref, y_ref):
        ur, _, _, _, vg = _sg_common(u_ref, v_ref, lng_ref, lnb_ref)
        ri = lax.broadcasted_iota(jnp.int32, (SGC, SGC), 0)
        ci = lax.broadcasted_iota(jnp.int32, (SGC, SGC), 1)
        ug = _gelu(ur)
        for g in range(H):
            sl = slice(g * DH, (g + 1) * DH)
            ws = jnp.where(ri >= ci, w_ref[g], 0.0)
            mixed = _bdot(ws, vg[:, sl]) + bt_ref[:, g:g + 1]
            y_ref[:, sl] = (ug[:, sl] * mixed).astype(y_ref.dtype)

    vec = pl.BlockSpec((1, WD), lambda i: (0, 0))
    return pl.pallas_call(
        body, name=name, grid=(T // SGC,),
        in_specs=[pl.BlockSpec((SGC, WD), lambda i: (i, OU // WD)), pl.BlockSpec((SGC, WD), lambda i: (i, OV // WD)),
                  vec, vec, pl.BlockSpec((H, SGC, SGC), lambda i: (0, 0, 0)),
                  pl.BlockSpec((SGC, H), lambda i: (0, 0))],
        out_specs=pl.BlockSpec((SGC, WD), lambda i: (i, 0)),
        out_shape=jax.ShapeDtypeStruct((T, WD), BF16), compiler_params=_cp("parallel"),
    )(proj, proj, lng, lnb, sgw, sgbt)


def _sg_bwd(proj, lng, lnb, sgw, sgbt, dy, name="sg_bwd"):
    T = proj.shape[0]

    def body(u_ref, v_ref, lng_ref, lnb_ref, w_ref, bt_ref, dy_ref,
             du_ref, dv_ref, dw_ref, dbt_ref, dlng_ref, dlnb_ref):
        @pl.when(pl.program_id(0) == 0)
        def _():
            dw_ref[...] = jnp.zeros_like(dw_ref)
            dbt_ref[...] = jnp.zeros_like(dbt_ref)
            dlng_ref[...] = jnp.zeros_like(dlng_ref)
            dlnb_ref[...] = jnp.zeros_like(dlnb_ref)

        ur, vr, rs, xh, vg = _sg_common(u_ref, v_ref, lng_ref, lnb_ref)
        ri = lax.broadcasted_iota(jnp.int32, (SGC, SGC), 0)
        ci = lax.broadcasted_iota(jnp.int32, (SGC, SGC), 1)
        ug = _gelu(ur)
        dyv = dy_ref[...]
        dbt = jnp.zeros((SGC, 128), F32)
        dvg_parts = []
        for g in range(H):
            sl = slice(g * DH, (g + 1) * DH)
            ws = jnp.where(ri >= ci, w_ref[g], 0.0)
            mixed = _bdot(ws, vg[:, sl]) + bt_ref[:, g:g + 1]
            dyg = dyv[:, sl]
            du_ref[:, sl] = (dyg * mixed * _dgelu(ur[:, sl])).astype(du_ref.dtype)
            dmix = dyg * ug[:, sl]
            dw_ref[g] += jnp.where(ri >= ci, _bdot(dmix, vg[:, sl], NT), 0.0)
            dbt = dbt + jnp.where(ci == g, _lsum(dmix), 0.0)
            dvg_parts.append(_bdot(ws, dmix, TN))
        dbt_ref[...] += dbt
        dvg = jnp.concatenate(dvg_parts, axis=1)
        dlng_ref[...] += jnp.sum(dvg * xh, axis=0, keepdims=True)
        dlnb_ref[...] += jnp.sum(dvg, axis=0, keepdims=True)
        dxh = dvg * lng_ref[...]
        dvgel = rs * (dxh - jnp.mean(dxh, axis=-1, keepdims=True) - xh * jnp.mean(dxh * xh, axis=-1, keepdims=True))
        dv_ref[...] = (dvgel * _dgelu(vr)).astype(dv_ref.dtype)

    vec = pl.BlockSpec((1, WD), lambda i: (0, 0))
    row = pl.BlockSpec((SGC, WD), lambda i: (i, 0))
    return pl.pallas_call(
        body, name=name, grid=(T // SGC,),
        in_specs=[pl.BlockSpec((SGC, WD), lambda i: (i, OU // WD)), pl.BlockSpec((SGC, WD), lambda i: (i, OV // WD)),
                  vec, vec, pl.BlockSpec((H, SGC, SGC), lambda i: (0, 0, 0)),
                  pl.BlockSpec((SGC, H), lambda i: (0, 0)), row],
        out_specs=[row, row, pl.BlockSpec((H, SGC, SGC), lambda i: (0, 0, 0)),
                   pl.BlockSpec((SGC, 128), lambda i: (0, 0)), vec, vec],
        out_shape=[jax.ShapeDtypeStruct((T, WD), BF16), jax.ShapeDtypeStruct((T, WD), BF16),
                   jax.ShapeDtypeStruct((H, SGC, SGC), F32), jax.ShapeDtypeStruct((SGC, 128), F32),
                   jax.ShapeDtypeStruct((1, WD), F32), jax.ShapeDtypeStruct((1, WD), F32)],
        compiler_params=_cp("arbitrary"),
    )(proj, proj, lng, lnb, sgw, sgbt, dy)


def _merge_fwd(proj, yap, ybp, D, name="merge_fwd"):
    T = proj.shape[0]
    tt = _tile(T, 256, 16)

    def body(ga_ref, gb_ref, a_ref, b_ref, o_ref):
        o_ref[...] = (_sig(ga_ref[...]) * a_ref[...] + _sig(gb_ref[...]) * b_ref[...]).astype(o_ref.dtype)

    row = pl.BlockSpec((tt, D), lambda i: (i, 0))
    return pl.pallas_call(
        body, name=name, grid=(T // tt,),
        in_specs=[pl.BlockSpec((tt, D), lambda i: (i, OGA // D)), pl.BlockSpec((tt, D), lambda i: (i, OGA // D + 1)),
                  row, row],
        out_specs=row, out_shape=jax.ShapeDtypeStruct((T, D), BF16), compiler_params=_cp("parallel"),
    )(proj, proj, yap, ybp)


def _merge_bwd(proj, yap, ybp, dm, D, name="merge_bwd"):
    T = proj.shape[0]
    tt = _tile(T, 256, 16)

    def body(ga_ref, gb_ref, a_ref, b_ref, dm_ref, da_ref, db_ref, dga_ref, dgb_ref):
        d = dm_ref[...]
        sa = _sig(ga_ref[...])
        sb = _sig(gb_ref[...])
        da_ref[...] = (d * sa).astype(da_ref.dtype)
        db_ref[...] = (d * sb).astype(db_ref.dtype)
        dga_ref[...] = (d * a_ref[...] * sa * (1.0 - sa)).astype(dga_ref.dtype)
        dgb_ref[...] = (d * b_ref[...] * sb * (1.0 - sb)).astype(dgb_ref.dtype)

    row = pl.BlockSpec((tt, D), lambda i: (i, 0))
    return pl.pallas_call(
        body, name=name, grid=(T // tt,),
        in_specs=[pl.BlockSpec((tt, D), lambda i: (i, OGA // D)), pl.BlockSpec((tt, D), lambda i: (i, OGA // D + 1)),
                  row, row, row],
        out_specs=[row] * 4, out_shape=[jax.ShapeDtypeStruct((T, D), BF16)] * 4,
        compiler_params=_cp("parallel"),
    )(proj, proj, yap, ybp, dm)


def _ffn_act_fwd(gp, up, cw, cb, name="ffn_act_fwd"):
    T, F = gp.shape

    def body(g_ref, u_ref, w_ref, b_ref, o_ref):
        gv = g_ref[...]
        w = w_ref[...]
        c = gv * w[FF_K - 1:FF_K, :] + b_ref[...]
        for k in range(1, FF_K):
            c = c + _shift_down(gv, k) * w[FF_K - 1 - k:FF_K - k, :]
        o_ref[...] = (c * _sig(c) * u_ref[...]).astype(o_ref.dtype)

    col = pl.BlockSpec((T, 128), lambda j: (0, j))
    return pl.pallas_call(
        body, name=name, grid=(F // 128,),
        in_specs=[col, col, pl.BlockSpec((FF_K, 128), lambda j: (0, j)), pl.BlockSpec((1, 128), lambda j: (0, j))],
        out_specs=col, out_shape=jax.ShapeDtypeStruct((T, F), BF16), compiler_params=_cp("parallel"),
    )(gp, up, cw, cb)


def _ffn_act_bwd(gp, up, cw, cb, dact, name="ffn_act_bwd"):
    T, F = gp.shape

    def body(g_ref, u_ref, w_ref, b_ref, d_ref, dg_ref, du_ref, dw_ref, db_ref):
        gv = g_ref[...]
        w = w_ref[...]
        shifted = [_shift_down(gv, k) for k in range(FF_K)]
        c = shifted[0] * w[FF_K - 1:FF_K, :] + b_ref[...]
        for k in range(1, FF_K):
            c = c + shifted[k] * w[FF_K - 1 - k:FF_K - k, :]
        d = d_ref[...]
        du_ref[...] = (d * c * _sig(c)).astype(du_ref.dtype)
        dc = d * u_ref[...] * _dsilu(c)
        dg = dc * w[FF_K - 1:FF_K, :]
        for k in range(1, FF_K):
            dg = dg + _shift_up(dc, k) * w[FF_K - 1 - k:FF_K - k, :]
        dg_ref[...] = dg.astype(dg_ref.dtype)
        rows = [jnp.sum(dc * shifted[FF_K - 1 - t], axis=0, keepdims=True) for t in range(FF_K)]
        dw_ref[...] = jnp.concatenate(rows, axis=0)
        db_ref[...] = jnp.sum(dc, axis=0, keepdims=True)

    col = pl.BlockSpec((T, 128), lambda j: (0, j))
    wspec = pl.BlockSpec((FF_K, 128), lambda j: (0, j))
    bspec = pl.BlockSpec((1, 128), lambda j: (0, j))
    return pl.pallas_call(
        body, name=name, grid=(F // 128,),
        in_specs=[col, col, wspec, bspec, col], out_specs=[col, col, wspec, bspec],
        out_shape=[jax.ShapeDtypeStruct((T, F), BF16), jax.ShapeDtypeStruct((T, F), BF16),
                   jax.ShapeDtypeStruct((FF_K, F), F32), jax.ShapeDtypeStruct((1, F), F32)],
        compiler_params=_cp("parallel"),
    )(gp, up, cw, cb, dact)


class _Carrier:
    def __init__(self, plan=None, deliver=None):
        self.plan, self.deliver = plan or (lambda kernel: None), deliver

    def run(self, kernel, fn, **kw):
        comm = self.plan(kernel)
        out = fn(comm=comm, **kw)
        if comm:
            self.deliver(kernel, out[-1])
            out = out[:-1]
            return out[0] if len(out) == 1 else out
        return out


def _layer_fwd(x, w, carrier=None):
    cr = carrier or _Carrier()
    D = x.shape[1]
    oba = OGA + 2 * D
    h = _rms_fwd(x, w["norm1_g"], "rms1_fwd")
    proj = cr.run("proj", functools.partial(_mm, h, w["w_in_t"], "nt", F32, name="mm_proj"))
    bg = _ba_fwd(proj, w["alog_row"], w["dtb_row"], oba)
    qkv = _dn_prep_fwd(proj, w["dn_conv_w"])
    r = cr.run("dn_core", functools.partial(_dn_core_fwd, qkv, bg))
    o, s_all, tm_all = r[0], r[1], r[2]
    ya = _dn_post_fwd(o, proj, w["dn_onorm_g"])
    yb = _sg_fwd(proj, w["sg_ln_g"], w["sg_ln_b"], w["sg_w"], w["sg_bt"])
    yap = _mm(ya, w["w_branch_a"], "nn", F32, name="mm_branch")
    ybp = _mm(yb, w["w_branch_b"], "nn", F32, name="mm_branch")
    merged = _merge_fwd(proj, yap, ybp, D)
    x1 = _mm(merged, w["w_out"], "nn", F32, add=x, name="mm_out")
    h2 = _rms_fwd(x1, w["norm2_g"], "rms2_fwd")
    gp = cr.run("ffn_gate", functools.partial(_mm, h2, w["ffn_w_gate"], "nn", F32, name="mm_ffn_in"))
    up = cr.run("ffn_up", functools.partial(_mm, h2, w["ffn_w_up"], "nn", F32, name="mm_ffn_in"))
    act = _ffn_act_fwd(gp, up, w["ffn_conv_w"], w["ffn_conv_b"])
    x2 = cr.run("ffn_down", functools.partial(_mm, act, w["ffn_w_down"], "nn", F32, add=x1, name="mm_ffn_down"))
    saved = dict(x=x, h=h, proj=proj, bg=bg, qkv=qkv, o=o, s_all=s_all, tm_all=tm_all, ya=ya, yb=yb, yap=yap,
                 ybp=ybp, merged=merged, x1=x1, h2=h2, gp=gp, up=up, act=act)
    return x2, saved


def _layer_bwd(dx2, w, s, carrier=None, ffn_grads_ready=None):
    cr = carrier or _Carrier()
    D = dx2.shape[1]
    oba = OGA + 2 * D
    g = {}
    dx2b = dx2.astype(BF16)
    dact = _mm(dx2b, w["ffn_w_down"], "nt", F32, name="mm_d_act")
    g["ffn_w_down"] = _mm(s["act"], dx2b, "tn", BF16, name="mm_dw_down")
    dgp, dup, g["ffn_conv_w"], g["ffn_conv_b"] = _ffn_act_bwd(s["gp"], s["up"], w["ffn_conv_w"], w["ffn_conv_b"], dact)
    dh2 = _mm(dgp, w["ffn_w_gate"], "nt", F32, name="mm_dh2")
    dh2 = _mm(dup, w["ffn_w_up"], "nt", F32, add=dh2, name="mm_dh2_acc")
    g["ffn_w_gate"] = _mm(s["h2"], dgp, "tn", BF16, name="mm_dw_ffn_in")
    g["ffn_w_up"] = _mm(s["h2"], dup, "tn", BF16, name="mm_dw_ffn_in")
    if ffn_grads_ready:
        ffn_grads_ready(g)
    dx1, g["norm2_g"] = _rms_bwd(s["x1"], w["norm2_g"], dh2, dx2, "rms2_bwd")
    dx1b = dx1.astype(BF16)
    dm = _mm(dx1b, w["w_out"], "nt", F32, name="mm_d_merged")
    g["w_out"] = _mm(s["merged"], dx1b, "tn", BF16, name="mm_dw_out")
    dyap, dybp, dga, dgb = _merge_bwd(s["proj"], s["yap"], s["ybp"], dm, D)
    dya = _mm(dyap, w["w_branch_a"], "nt", F32, name="mm_d_branch")
    dyb = _mm(dybp, w["w_branch_b"], "nt", F32, name="mm_d_branch")
    g["w_branch_a"] = _mm(s["ya"], dyap, "tn", BF16, name="mm_dw_branch")
    g["w_branch_b"] = _mm(s["yb"], dybp, "tn", BF16, name="mm_dw_branch")
    du, dv, g["sg_w"], dbt, g["sg_ln_g"], g["sg_ln_b"] = _sg_bwd(
        s["proj"], w["sg_ln_g"], w["sg_ln_b"], w["sg_w"], w["sg_bt"], dyb)
    g["sg_b"] = jnp.transpose(dbt[:, :H])
    do, dz, g["dn_onorm_g"] = _dn_post_bwd(s["o"], s["proj"], w["dn_onorm_g"], dya)
    r = cr.run("dn_core", functools.partial(_dn_core_bwd, s["qkv"], s["bg"], s["s_all"], s["tm_all"], do))
    dq, dk, dvv, dbg = r[0], r[1], r[2], r[3]
    dqkv, g["dn_conv_w"] = _dn_prep_bwd(s["proj"], w["dn_conv_w"], dq, dk, dvv)
    dba, dal, ddt = _ba_bwd(s["proj"], w["alog_row"], w["dtb_row"], dbg, oba)
    g["dn_a_log"] = dal[0, H:2 * H]
    g["dn_dt_bias"] = ddt[0, H:2 * H]
    dproj = jnp.concatenate([dqkv, dz, du, dv, dga, dgb, dba], axis=1)
    dh = cr.run("dh", functools.partial(_mm, dproj, w["w_in_t"], "nn", F32, name="mm_dh"))
    g["w_in_t"] = cr.run("dw_in", functools.partial(_mm, dproj, s["h"], "tn", BF16, name="mm_dw_in"))
    dx, g["norm1_g"] = _rms_bwd(s["x"], w["norm1_g"], dh, dx1, "rms1_bwd")
    return dx, g


def _local_step(x, tgt, layers, final_g):
    saved = []
    for w in layers:
        x, s = _layer_fwd(x, w)
        saved.append(s)
    dx, dgf, loss = _loss_head(x, final_g, tgt)
    grads = [None] * len(layers)
    for l in reversed(range(len(layers))):
        dx, grads[l] = _layer_bwd(dx, layers[l], saved[l])
    return loss[0, 0], dx, grads, dgf


def _w_in_pad(wt):
    c1 = 4 * WD
    return jnp.concatenate([wt[:c1], wt[c1 + 2 * H:], wt[c1:c1 + 2 * H],
                            jnp.zeros((128 - 2 * H, wt.shape[1]), wt.dtype)], axis=0)


def _w_in_unpad(gt):
    c1 = 4 * WD
    n = gt.shape[0] - 128
    return jnp.concatenate([gt[:c1], gt[n:n + 2 * H], gt[c1:n]], axis=0)


def _row128(v, off):
    return jnp.pad(v, (off, 128 - off - v.shape[0]))[None]


def _prep_small(p):
    return dict(
        norm1_g=p["norm1_g"][None], alog_row=_row128(p["dn_a_log"], H), dtb_row=_row128(p["dn_dt_bias"], H),
        dn_conv_w=p["dn_conv_w"], dn_onorm_g=p["dn_onorm_g"][None],
        sg_ln_g=p["sg_ln_g"][None], sg_ln_b=p["sg_ln_b"][None], sg_w=p["sg_w"], sg_bt=jnp.transpose(p["sg_b"]),
        norm2_g=p["norm2_g"][None], ffn_conv_w=p["ffn_conv_w"], ffn_conv_b=p["ffn_conv_b"][None])


def _prep_layer(p):
    return dict(_prep_small(p), w_in_t=_w_in_pad(p["w_in_t"]),
                **{n: p[n] for n in ("w_branch_a", "w_branch_b", "w_out", "ffn_w_gate", "ffn_w_up", "ffn_w_down")})


HBM_SPEC = pl.BlockSpec(memory_space=pltpu.HBM)


def _coords():
    return lax.axis_index("x"), lax.axis_index("y"), lax.axis_index("c")


def _other_chips(x, y):
    return [(1 - x, y), (x, 1 - y), (1 - x, 1 - y)]


def _remote(src, dst, send_sems, recv_sems, k, dev):
    return pltpu.make_async_remote_copy(src_ref=src, dst_ref=dst, send_sem=send_sems.at[k], recv_sem=recv_sems.at[k],
                                        device_id=dev, device_id_type=MESH)


def _ag_copies(w_refs, o_refs, send_sems, recv_sems):
    x, y, c = _coords()
    me = 2 * x + y
    chips = _other_chips(x, y)

    def ici(k, j, owner):
        chip = chips[j]
        return _remote(w_refs[k].at[c], o_refs[k].at[owner, c], send_sems, recv_sems, 6 * k + j, (chip[0], chip[1], c))

    def d2d(k, j, part):
        owner = 2 * chips[j][0] + chips[j][1]
        return _remote(o_refs[k].at[owner, part], o_refs[k].at[owner, part], send_sems, recv_sems, 6 * k + 3 + j,
                       (x, y, 1 - c))

    n = len(w_refs)
    return me, c, chips, ici, d2d, [(k, j) for k in range(n) for j in range(3)]


def _ag_start(w_refs, o_refs, send_sems, recv_sems):
    me, _, _, ici, _, pairs = _ag_copies(w_refs, o_refs, send_sems, recv_sems)
    for k, j in pairs:
        ici(k, j, me).start()


def _ag_finish(w_refs, o_refs, send_sems, recv_sems):
    me, c, chips, ici, d2d, pairs = _ag_copies(w_refs, o_refs, send_sems, recv_sems)
    for k, j in pairs:
        ici(k, j, 2 * chips[j][0] + chips[j][1]).wait_recv()
        d2d(k, j, c).start()
    for k, j in pairs:
        d2d(k, j, 1 - c).wait_recv()
    for k, j in pairs:
        ici(k, j, me).wait_send()
        d2d(k, j, c).wait_send()


def _ag_specs(ws):
    n = len(ws)
    return dict(out_shape=[jax.ShapeDtypeStruct((N_CHIPS,) + w.shape, w.dtype) for w in ws],
                specs=[HBM_SPEC] * n, sems=[pltpu.SemaphoreType.DMA((6 * n,)), pltpu.SemaphoreType.DMA((6 * n,))])


def _ag_layers(ws):
    n = len(ws)

    def body(*refs):
        _ag_start(refs[:n], refs[n:2 * n], *refs[2 * n:])
        _ag_finish(refs[:n], refs[n:2 * n], *refs[2 * n:])

    sp = _ag_specs(ws)
    return pl.pallas_call(
        body, name="ag_weights", out_shape=sp["out_shape"], in_specs=sp["specs"], out_specs=sp["specs"],
        scratch_shapes=sp["sems"],
    )(*ws)


def _rs_pair_exchange(Gs):
    n = len(Gs)

    def body(*refs):
        g_refs, b_refs = refs[:n], refs[n:2 * n]
        send_sems, recv_sems = refs[2 * n:]
        x, y, c = _coords()
        cps = [_remote(g_refs[k].at[i, 1 - c], b_refs[k].at[i], send_sems, recv_sems, N_CHIPS * k + i, (x, y, 1 - c))
               for k in range(n) for i in range(N_CHIPS)]
        for cp in cps:
            cp.start()
        for cp in cps:
            cp.wait()

    return pl.pallas_call(
        body, name="rs_pair_exchange",
        out_shape=[jax.ShapeDtypeStruct((N_CHIPS,) + g.shape[2:], g.dtype) for g in Gs],
        in_specs=[HBM_SPEC] * n, out_specs=[HBM_SPEC] * n,
        scratch_shapes=[pltpu.SemaphoreType.DMA((N_CHIPS * n,)), pltpu.SemaphoreType.DMA((N_CHIPS * n,))],
    )(*Gs)


def _rs_add_pair(G, B, c, name):
    _, _, R, C = G.shape
    tr = _tile(R, 256, 16)

    def body(c_ref, g_ref, b_ref, o_ref):
        o_ref[0] = (g_ref[0, 0].astype(F32) + b_ref[0].astype(F32)).astype(o_ref.dtype)

    grid_spec = pltpu.PrefetchScalarGridSpec(
        num_scalar_prefetch=1, grid=(N_CHIPS, R // tr),
        in_specs=[pl.BlockSpec((1, 1, tr, C), lambda i, r, c_ref: (i, c_ref[0], r, 0)),
                  pl.BlockSpec((1, tr, C), lambda i, r, c_ref: (i, r, 0))],
        out_specs=pl.BlockSpec((1, tr, C), lambda i, r, c_ref: (i, r, 0)))
    return pl.pallas_call(
        body, name=name, grid_spec=grid_spec, out_shape=jax.ShapeDtypeStruct((N_CHIPS, R, C), G.dtype),
        compiler_params=_cp("parallel", "parallel"),
    )(jnp.reshape(c, (1,)).astype(jnp.int32), G, B)


def _rs_chip_exchange(Ps):
    n = len(Ps)

    def body(*refs):
        _rsx_start(refs[:n], refs[n:2 * n], *refs[2 * n:])
        _rsx_finish(refs[:n], refs[n:2 * n], *refs[2 * n:])

    sp = _rsx_specs(Ps)
    return pl.pallas_call(
        body, name="rs_chip_exchange", out_shape=sp["out_shape"], in_specs=sp["specs"], out_specs=sp["specs"],
        scratch_shapes=sp["sems"],
    )(*Ps)


def _rsx_copies(p_refs, b_refs, send_sems, recv_sems):
    x, y, c = _coords()
    me = 2 * x + y
    chips = _other_chips(x, y)

    def cp(k, j, src_slot, dst_slot):
        return _remote(p_refs[k].at[src_slot], b_refs[k].at[dst_slot], send_sems, recv_sems, 3 * k + j,
                       (chips[j][0], chips[j][1], c))

    return me, chips, cp, [(k, j) for k in range(len(p_refs)) for j in range(3)]


def _rsx_start(p_refs, b_refs, send_sems, recv_sems):
    me, chips, cp, pairs = _rsx_copies(p_refs, b_refs, send_sems, recv_sems)
    for k, j in pairs:
        cp(k, j, 2 * chips[j][0] + chips[j][1], me).start()


def _rsx_finish(p_refs, b_refs, send_sems, recv_sems):
    me, chips, cp, pairs = _rsx_copies(p_refs, b_refs, send_sems, recv_sems)
    for k, j in pairs:
        owner = 2 * chips[j][0] + chips[j][1]
        cp(k, j, owner, owner).wait_recv()
    for k, j in pairs:
        cp(k, j, 2 * chips[j][0] + chips[j][1], me).wait_send()


def _rsx_specs(Ps):
    n = len(Ps)
    return dict(out_shape=[jax.ShapeDtypeStruct(p.shape, p.dtype) for p in Ps], specs=[HBM_SPEC] * n,
                sems=[pltpu.SemaphoreType.DMA((3 * n,)), pltpu.SemaphoreType.DMA((3 * n,))])


def _rs_sum_chips(P, B, me, name):
    _, R, C = P.shape
    tr = _tile(R, 256, 16)

    def body(me_ref, p_ref, b1_ref, b2_ref, b3_ref, o_ref):
        o_ref[...] = ((p_ref[0].astype(F32) + b1_ref[0].astype(F32)) + b2_ref[0].astype(F32)) + b3_ref[0].astype(F32)

    slot = lambda d: pl.BlockSpec((1, tr, C), lambda r, me_ref: ((me_ref[0] + d) % N_CHIPS, r, 0))
    grid_spec = pltpu.PrefetchScalarGridSpec(
        num_scalar_prefetch=1, grid=(R // tr,), in_specs=[slot(0), slot(1), slot(2), slot(3)],
        out_specs=pl.BlockSpec((tr, C), lambda r, me_ref: (r, 0)))
    return pl.pallas_call(
        body, name=name, grid_spec=grid_spec, out_shape=jax.ShapeDtypeStruct((R, C), F32),
        compiler_params=_cp("parallel"),
    )(jnp.reshape(me, (1,)).astype(jnp.int32), P, B, B, B)


def _sum_slots(B, name):
    S, R, C = B.shape
    tr = _tile(R, 256, 16)

    def body(b_ref, o_ref):
        acc = b_ref[0].astype(F32)
        for i in range(1, S):
            acc = acc + b_ref[i].astype(F32)
        o_ref[...] = acc

    return pl.pallas_call(
        body, name=name, grid=(R // tr,), in_specs=[pl.BlockSpec((S, tr, C), lambda r: (0, r, 0))],
        out_specs=pl.BlockSpec((tr, C), lambda r: (r, 0)), out_shape=jax.ShapeDtypeStruct((R, C), F32),
        compiler_params=_cp("parallel"),
    )(B)


def _rs_pair_swap(Rs):
    n = len(Rs)

    def body(*refs):
        r_refs, o_refs = refs[:n], refs[n:2 * n]
        send_sems, recv_sems = refs[2 * n:]
        x, y, c = _coords()
        cps = [_remote(r_refs[k], o_refs[k], send_sems, recv_sems, k, (x, y, 1 - c)) for k in range(n)]
        for cp in cps:
            cp.start()
        for cp in cps:
            cp.wait()

    return pl.pallas_call(
        body, name="rs_pair_swap", out_shape=[jax.ShapeDtypeStruct(r.shape, r.dtype) for r in Rs],
        in_specs=[HBM_SPEC] * n, out_specs=[HBM_SPEC] * n,
        scratch_shapes=[pltpu.SemaphoreType.DMA((n,)), pltpu.SemaphoreType.DMA((n,))],
    )(*Rs)


def _ag8(v):
    R = v.shape[0]

    def body(v_ref, out_ref, send_sems, recv_sems, local_sem):
        x, y, c = _coords()
        me, sib = (x, y, c), (x, y, 1 - c)
        chips = _other_chips(x, y)

        def slot(p):
            return out_ref.at[4 * p[0] + 2 * p[1] + p[2]]

        def copy(k, block, to, src=None):
            return _remote(slot(block) if src is None else src, slot(block), send_sems, recv_sems, k, to)

        mine = pltpu.make_async_copy(v_ref, slot(me), local_sem)
        mine.start()
        first = [copy(0, me, sib, src=v_ref)]
        first += [copy(1 + j, me, (chip[0], chip[1], c), src=v_ref) for j, chip in enumerate(chips)]
        for cp in first:
            cp.start()
        passed = [copy(4 + j, (chip[0], chip[1], c), sib) for j, chip in enumerate(chips)]
        for j, chip in enumerate(chips):
            copy(1 + j, (chip[0], chip[1], c), me).wait_recv()
            passed[j].start()
        copy(0, sib, me).wait_recv()
        for j, chip in enumerate(chips):
            copy(4 + j, (chip[0], chip[1], 1 - c), me).wait_recv()
        for cp in first + passed:
            cp.wait_send()
        mine.wait()

    return pl.pallas_call(
        body, name="ag8_small", out_shape=jax.ShapeDtypeStruct((8, R, 128), v.dtype),
        in_specs=[pl.BlockSpec(memory_space=pltpu.VMEM)], out_specs=pl.BlockSpec(memory_space=pltpu.VMEM),
        scratch_shapes=[pltpu.SemaphoreType.DMA((7,)), pltpu.SemaphoreType.DMA((7,)), pltpu.SemaphoreType.DMA],
        compiler_params=pltpu.CompilerParams(vmem_limit_bytes=VMEM_LIMIT),
    )(v)


def _adamw(w, g, m, v, name):
    L, R, C = w.shape
    tr = _tile(R, 128, 8)

    def body(w_ref, g_ref, m_ref, v_ref, d_ref, mo_ref, vo_ref):
        gv = g_ref[...]
        m2 = ADAM_B1 * m_ref[...] + (1.0 - ADAM_B1) * gv
        v2 = ADAM_B2 * v_ref[...] + (1.0 - ADAM_B2) * jnp.square(gv)
        m_hat = m2 / (1.0 - ADAM_B1 ** ADAM_STEP)
        v_hat = v2 / (1.0 - ADAM_B2 ** ADAM_STEP)
        d_ref[...] = -ADAM_LR * (m_hat / (jnp.sqrt(v_hat) + ADAM_EPS) + ADAM_WD * w_ref[...])
        mo_ref[...] = m2
        vo_ref[...] = v2

    blk = pl.BlockSpec((1, tr, C), lambda l, r: (l, r, 0))
    return pl.pallas_call(
        body, name=name, grid=(L, R // tr), in_specs=[blk] * 4, out_specs=[blk] * 3,
        out_shape=[jax.ShapeDtypeStruct(w.shape, F32)] * 3, compiler_params=_cp("parallel", "parallel"),
    )(w, g, m, v)


def _adamw_halves(w, g_mine, g_other, c, m, v, name):
    L, _, R, C = w.shape
    tr = _tile(R, 128, 8)

    def body(c_ref, w_ref, *rest):
        g_refs = rest[:2 * L]
        m_ref, v_ref, g_ref, d_ref, mo_ref, vo_ref = rest[2 * L:]
        l, h = pl.program_id(0), pl.program_id(1)
        gm, go = g_refs[0][...], g_refs[L][...]
        for i in range(1, L):
            gm = jnp.where(l == i, g_refs[i][...], gm)
            go = jnp.where(l == i, g_refs[L + i][...], go)
        gv = jnp.where(h == c_ref[0], gm, go)[None, None]
        g_ref[...] = gv
        m2 = ADAM_B1 * m_ref[...] + (1.0 - ADAM_B1) * gv
        v2 = ADAM_B2 * v_ref[...] + (1.0 - ADAM_B2) * jnp.square(gv)
        m_hat = m2 / (1.0 - ADAM_B1 ** ADAM_STEP)
        v_hat = v2 / (1.0 - ADAM_B2 ** ADAM_STEP)
        d_ref[...] = -ADAM_LR * (m_hat / (jnp.sqrt(v_hat) + ADAM_EPS) + ADAM_WD * w_ref[...])
        mo_ref[...] = m2
        vo_ref[...] = v2

    blk = pl.BlockSpec((1, 1, tr, C), lambda l, h, r, c_ref: (l, h, r, 0))

    def gblk(i, mine):
        def index(l, h, r, c_ref):
            use = jnp.logical_and(l == i, (h == c_ref[0]) == mine)
            return (jnp.where(use, r, 0), 0)
        return pl.BlockSpec((tr, C), index)

    grid_spec = pltpu.PrefetchScalarGridSpec(
        num_scalar_prefetch=1, grid=(L, 2, R // tr),
        in_specs=[blk] + [gblk(i, True) for i in range(L)] + [gblk(i, False) for i in range(L)] + [blk, blk],
        out_specs=[blk] * 4)
    return pl.pallas_call(
        body, name=name, grid_spec=grid_spec, out_shape=[jax.ShapeDtypeStruct(w.shape, F32)] * 4,
        compiler_params=_cp("parallel", "parallel", "parallel"),
    )(jnp.reshape(c, (1,)).astype(jnp.int32), w, *g_mine, *g_other, m, v)


BIG = ("w_in", "w_branch_a", "w_branch_b", "w_out", "ffn_w_gate", "ffn_w_up", "ffn_w_down")
ROW_SHARDED = ("w_out", "ffn_w_down")
SMALL = ("norm1_g", "dn_conv_w", "dn_a_log", "dn_dt_bias", "dn_onorm_g", "sg_ln_g", "sg_ln_b", "sg_w", "sg_b",
         "norm2_g", "ffn_conv_w", "ffn_conv_b", "final_norm_g")
SMALL_SHARDED = ("dn_conv_w", "ffn_conv_w")


def _pack_rows(arrs, mult):
    flat = jnp.concatenate([jnp.reshape(a, (-1,)) for a in arrs])
    n = flat.shape[0]
    rows = -(-n // (128 * mult)) * mult
    return jnp.reshape(jnp.pad(flat, (0, rows * 128 - n)), (rows, 128))


def _unpack(flat2d, shapes):
    flat = jnp.reshape(flat2d, (-1,))
    out, off = [], 0
    for shp in shapes:
        n = math.prod(shp)
        out.append(jnp.reshape(flat[off:off + n], shp))
        off += n
    return out


def _shards_to_full(a, row_sharded):
    if row_sharded:
        a = jnp.moveaxis(a, 0, 1)
        return jnp.reshape(a, (a.shape[0], a.shape[1] * a.shape[2], a.shape[3]))
    a = jnp.moveaxis(a, 0, 2)
    return jnp.reshape(a, (a.shape[0], a.shape[1], a.shape[2] * a.shape[3]))


def _full_to_shards(a, row_sharded):
    L, R, C = a.shape
    if row_sharded:
        return jnp.moveaxis(jnp.reshape(a, (L, N_CHIPS, R // N_CHIPS, C)), 1, 0)
    return jnp.moveaxis(jnp.reshape(a, (L, R, N_CHIPS, C // N_CHIPS)), 2, 0)


def kernel(x, norm1_g, w_in, dn_conv_w, dn_a_log, dn_dt_bias, dn_onorm_g, sg_ln_g, sg_ln_b, sg_w, sg_b, w_branch_a, w_branch_b, w_out, norm2_g, ffn_w_gate, ffn_w_up, ffn_conv_w, ffn_conv_b, ffn_w_down, final_norm_g, loss_target, m_norm1_g, m_w_in, m_dn_conv_w, m_dn_a_log, m_dn_dt_bias, m_dn_onorm_g, m_sg_ln_g, m_sg_ln_b, m_sg_w, m_sg_b, m_w_branch_a, m_w_branch_b, m_w_out, m_norm2_g, m_ffn_w_gate, m_ffn_w_up, m_ffn_conv_w, m_ffn_conv_b, m_ffn_w_down, m_final_norm_g, v_norm1_g, v_w_in, v_dn_conv_w, v_dn_a_log, v_dn_dt_bias, v_dn_onorm_g, v_sg_ln_g, v_sg_ln_b, v_sg_w, v_sg_b, v_w_branch_a, v_w_branch_b, v_w_out, v_norm2_g, v_ffn_w_gate, v_ffn_w_up, v_ffn_conv_w, v_ffn_conv_b, v_ffn_w_down, v_final_norm_g):
    W = dict(norm1_g=norm1_g, w_in=w_in, dn_conv_w=dn_conv_w, dn_a_log=dn_a_log, dn_dt_bias=dn_dt_bias,
             dn_onorm_g=dn_onorm_g, sg_ln_g=sg_ln_g, sg_ln_b=sg_ln_b, sg_w=sg_w, sg_b=sg_b, w_branch_a=w_branch_a,
             w_branch_b=w_branch_b, w_out=w_out, norm2_g=norm2_g, ffn_w_gate=ffn_w_gate, ffn_w_up=ffn_w_up,
             ffn_conv_w=ffn_conv_w, ffn_conv_b=ffn_conv_b, ffn_w_down=ffn_w_down, final_norm_g=final_norm_g)
    M = dict(norm1_g=m_norm1_g, w_in=m_w_in, dn_conv_w=m_dn_conv_w, dn_a_log=m_dn_a_log, dn_dt_bias=m_dn_dt_bias,
             dn_onorm_g=m_dn_onorm_g, sg_ln_g=m_sg_ln_g, sg_ln_b=m_sg_ln_b, sg_w=m_sg_w, sg_b=m_sg_b,
             w_branch_a=m_w_branch_a, w_branch_b=m_w_branch_b, w_out=m_w_out, norm2_g=m_norm2_g,
             ffn_w_gate=m_ffn_w_gate, ffn_w_up=m_ffn_w_up, ffn_conv_w=m_ffn_conv_w, ffn_conv_b=m_ffn_conv_b,
             ffn_w_down=m_ffn_w_down, final_norm_g=m_final_norm_g)
    V = dict(norm1_g=v_norm1_g, w_in=v_w_in, dn_conv_w=v_dn_conv_w, dn_a_log=v_dn_a_log, dn_dt_bias=v_dn_dt_bias,
             dn_onorm_g=v_dn_onorm_g, sg_ln_g=v_sg_ln_g, sg_ln_b=v_sg_ln_b, sg_w=v_sg_w, sg_b=v_sg_b,
             w_branch_a=v_w_branch_a, w_branch_b=v_w_branch_b, w_out=v_w_out, norm2_g=v_norm2_g,
             ffn_w_gate=v_ffn_w_gate, ffn_w_up=v_ffn_w_up, ffn_conv_w=v_ffn_conv_w, ffn_conv_b=v_ffn_conv_b,
             ffn_w_down=v_ffn_w_down, final_norm_g=v_final_norm_g)
    cx, cy, cc = _coords()
    chip = 2 * cx + cy
    L = w_in.shape[0]

    D = w_in.shape[1]
    cs_in = w_in.shape[2]
    rp_in = -(-cs_in // 128) * 128

    def shard_for_gather(n):
        if n == "w_in":
            return jnp.pad(jnp.swapaxes(W[n], 1, 2).astype(BF16), ((0, 0), (0, rp_in - cs_in), (0, 0)))
        return W[n].astype(BF16)

    mine = {n: shard_for_gather(n) for n in BIG}

    def halves(a, lead=0):
        return jnp.reshape(a, a.shape[:lead] + (2, a.shape[lead] // 2) + a.shape[lead + 1:])

    taps = _ag8(_pack_rows([W[n] for n in SMALL_SHARDED], 16))
    tap_shards = [_unpack(taps[2 * i], [W[n].shape for n in SMALL_SHARDED]) for i in range(N_CHIPS)]
    taps_full = {n: jnp.concatenate([tap_shards[i][k] for i in range(N_CHIPS)], axis=-1)
                 for k, n in enumerate(SMALL_SHARDED)}

    ops = []
    for l in range(L):
        p = {n: W[n][l] for n in W if n not in ("final_norm_g",) + BIG + SMALL_SHARDED}
        p.update({n: taps_full[n][l] for n in SMALL_SHARDED})
        ops.append(_prep_small(p))

    def gather_payload(items):
        return ("gather", [halves(mine[n][l]) for l, n in items])

    def weights_landed(items, gathered):
        for (l, n), a in zip(items, gathered):
            a = jnp.reshape(a, (N_CHIPS,) + mine[n].shape[1:])
            parts = [jnp.where(chip == i, mine[n][l], a[i]) for i in range(N_CHIPS)]
            if n == "w_in":
                ops[l]["w_in_t"] = _w_in_pad(jnp.concatenate([q[:cs_in] for q in parts], axis=0))
            else:
                ops[l][n] = jnp.concatenate(parts, axis=0 if n in ROW_SHARDED else 1)

    partial_sums, chip_sums = {}, {}

    def grad_partials(l, names, g):
        Gs = []
        for n in names:
            if n == "w_in":
                gt = jnp.reshape(_w_in_unpad(g["w_in_t"]), (N_CHIPS, cs_in, D))
                a = jnp.pad(gt, ((0, 0), (0, rp_in - cs_in), (0, 0)))
            elif n in ROW_SHARDED:
                a = jnp.reshape(g[n], (N_CHIPS, g[n].shape[0] // N_CHIPS, g[n].shape[1]))
            else:
                a = jnp.moveaxis(jnp.reshape(g[n], (g[n].shape[0], N_CHIPS, g[n].shape[1] // N_CHIPS)), 1, 0)
            Gs.append(halves(a, 1))
        B1s = _rs_pair_exchange(Gs)
        for n, a, b in zip(names, Gs, B1s):
            partial_sums[(l, n)] = _rs_add_pair(a, b, cc, "rs_add_pair_" + n)

    def carrier(l, plan, landed):
        def payload(kernel):
            items = plan.get((l, kernel))
            if not items:
                return None
            return gather_payload(items) if landed is weights_landed else ("exchange", [partial_sums[i] for i in items])
        return _Carrier(payload, lambda kernel, res: landed(plan[(l, kernel)], res))

    FFN = ("ffn_w_gate", "ffn_w_up", "ffn_w_down")
    REST = ("w_in", "w_branch_a", "w_branch_b", "w_out")
    fwd_plan = {(0, "proj"): [(0, "w_branch_a"), (0, "w_branch_b"), (0, "w_out"), (0, "ffn_w_gate")],
                (0, "dn_core"): [(0, "ffn_w_up"), (0, "ffn_w_down"), (1, "w_in")],
                (0, "ffn_gate"): [(1, "w_branch_a"), (1, "w_branch_b"), (1, "w_out")],
                (0, "ffn_up"): [(1, "ffn_w_gate")],
                (0, "ffn_down"): [(1, "ffn_w_up")],
                (1, "proj"): [(1, "ffn_w_down")]}
    bwd_plan = {(1, "dn_core"): [(1, n) for n in FFN],
                (0, "dn_core"): [(1, n) for n in REST],
                (0, "dh"): [(0, "ffn_w_gate"), (0, "ffn_w_up")],
                (0, "dw_in"): [(0, "ffn_w_down")]}

    def sums_landed(items, res):
        chip_sums.update(zip(items, res))

    first = [(0, "w_in")]
    weights_landed(first, _ag_layers(gather_payload(first)[1]))
    xs, saved = x[0], []
    for l in range(L):
        xs, s = _layer_fwd(xs, ops[l], carrier(l, fwd_plan, weights_landed))
        saved.append(s)
    dx, dgf, loss = _loss_head(xs, final_norm_g[None], loss_target[0])
    loss = loss[0, 0]
    grads = [None] * L
    for l in reversed(range(L)):
        dx, grads[l] = _layer_bwd(dx, ops[l], saved[l], carrier(l, bwd_plan, sums_landed),
                                  functools.partial(grad_partials, l, FFN))
        grad_partials(l, REST, grads[l])
    last = [(0, n) for n in REST]
    sums_landed(last, _rs_chip_exchange([partial_sums[i] for i in last]))
    g_mine = [[_rs_sum_chips(partial_sums[(l, n)], chip_sums[(l, n)], chip, "rs_sum_chips_" + n) for n in BIG]
              for l in range(L)]
    swapped = _rs_pair_swap(g_mine[0] + g_mine[1])
    g_other = [swapped[:len(BIG)], swapped[len(BIG):]]

    small = {n: jnp.stack([g[n] for g in grads]) for n in SMALL if n != "final_norm_g"}
    small["final_norm_g"] = dgf
    shapes = [taps_full[n].shape if n in SMALL_SHARDED else W[n].shape for n in SMALL] + [(1,)]
    sflat = _pack_rows([small[n] for n in SMALL] + [jnp.reshape(loss, (1,))], 16)
    sred = _unpack(_sum_slots(_ag8(sflat), "sum_small"), shapes)
    g_small = dict(zip(SMALL, sred[:-1]))
    loss_total = sred[-1][0]
    for n in SMALL_SHARDED:
        cs = W[n].shape[-1]
        g_small[n] = lax.dynamic_slice_in_dim(g_small[n], chip * cs, cs, axis=-1)

    g_big, delta, new_m, new_v = {}, {}, {}, {}
    for k, n in enumerate(BIG):
        gm, go = [g_mine[l][k] for l in range(L)], [g_other[l][k] for l in range(L)]
        if n == "w_in":
            rows = [jnp.where(cc == 0, jnp.concatenate([a, b]), jnp.concatenate([b, a])) for a, b in zip(gm, go)]
            g_big[n] = jnp.stack([jnp.transpose(r[:cs_in]) for r in rows])
            delta[n], new_m[n], new_v[n] = _adamw(W[n], g_big[n], M[n], V[n], "adamw_" + n)
        else:
            outs = _adamw_halves(halves(W[n], 1), gm, go, cc, halves(M[n], 1), halves(V[n], 1), "adamw_" + n)
            g_big[n], delta[n], new_m[n], new_v[n] = [jnp.reshape(o, W[n].shape) for o in outs]
    s_shapes = [W[n].shape for n in SMALL]
    packed = [_pack_rows([d[n] for n in SMALL], 8) for d in (W, g_small, M, V)]
    outs = _adamw(*[a[None] for a in packed], "adamw_small")
    for d, o in zip((delta, new_m, new_v), outs):
        d.update(zip(SMALL, _unpack(o[0], s_shapes)))

    names = list(W)
    grad_w = {**g_big, **g_small}
    return (loss_total, dx[None], *[grad_w[n] for n in names], *[delta[n] for n in names],
            *[new_m[n] for n in names], *[new_v[n] for n in names])
```

```python
import functools
import math

import jax
import jax.numpy as jnp
from jax import lax
from jax.experimental import pallas as pl
from jax.experimental.pallas import tpu as pltpu

F32 = jnp.float32
BF16 = jnp.bfloat16
MESH = pl.DeviceIdType.MESH

EPS = 1e-6
H = 8
DH = 128
WD = H * DH
DNC = 64
SGC = 128
DN_K = 4
FF_K = 3
DEPTH = 2
N_CHIPS = 4

ADAM_LR = 0.001
ADAM_B1 = 0.9
ADAM_B2 = 0.999
ADAM_EPS = 1e-08
ADAM_WD = 0.01
ADAM_STEP = 10

VMEM_LIMIT = 56 * 1024 * 1024

NN = (((1,), (0,)), ((), ()))
NT = (((1,), (1,)), ((), ()))
TN = (((0,), (0,)), ((), ()))

OQ, OZ, OU, OV, OGA = 0, 3 * WD, 4 * WD, 5 * WD, 6 * WD


def _cp(*sem):
    return pltpu.CompilerParams(dimension_semantics=sem or None, vmem_limit_bytes=VMEM_LIMIT)


def _tile(dim, pref, unit=128):
    if dim <= pref:
        return dim
    t = (pref // unit) * unit
    while t >= unit:
        if dim % t == 0:
            return t
        t -= unit
    return dim


def _hdot(a, b, dn=NN):
    return lax.dot_general(a, b, dn, precision=lax.Precision.HIGHEST, preferred_element_type=F32)


def _bdot(a, b, dn=NN):
    return lax.dot_general(a.astype(BF16), b.astype(BF16), dn, preferred_element_type=F32)


def _lsum(x):
    return jnp.sum(x, axis=1, keepdims=True)


def _sig(x):
    return jax.nn.sigmoid(x)


def _dsilu(x):
    s = _sig(x)
    return s * (1.0 + x * (1.0 - s))


def _erf(x):
    a = jnp.abs(x)
    t = 1.0 / (1.0 + 0.3275911 * a)
    poly = t * (0.254829592 + t * (-0.284496736 + t * (1.421413741 + t * (-1.453152027 + t * 1.061405429))))
    r = 1.0 - poly * jnp.exp(-a * a)
    return jnp.where(x < 0, -r, r)


def _gelu(x):
    return 0.5 * x * (1.0 + _erf(x * (2.0 ** -0.5)))


def _dgelu(x):
    cdf = 0.5 * (1.0 + _erf(x * (2.0 ** -0.5)))
    pdf = jnp.exp(-0.5 * x * x) * (1.0 / math.sqrt(2.0 * math.pi))
    return cdf + x * pdf


def _shift_down(x, k):
    if k == 0:
        return x
    rows = lax.broadcasted_iota(jnp.int32, x.shape, 0)
    return jnp.where(rows >= k, pltpu.roll(x, k, 0), 0.0)


def _shift_up(x, k):
    if k == 0:
        return x
    n = x.shape[0]
    rows = lax.broadcasted_iota(jnp.int32, x.shape, 0)
    return jnp.where(rows < n - k, pltpu.roll(x, n - k, 0), 0.0)


def _comm_fns(comm):
    if not comm:
        return None, None, dict(out_shape=[], specs=[], sems=[]), ()
    kind, arrays = comm
    start, finish, specs = {"gather": (_ag_start, _ag_finish, _ag_specs),
                            "exchange": (_rsx_start, _rsx_finish, _rsx_specs)}[kind]
    return start, finish, specs(arrays), tuple(arrays)


def _mm(a, b, mode, out_dtype, add=None, name="mm", comm=None):
    if mode == "tn":
        K, M = a.shape
    else:
        M, K = a.shape
    N = b.shape[0] if mode == "nt" else b.shape[1]
    tm, tn, tk = _tile(M, 1152), _tile(N, 1536), _tile(K, 3584)
    nk = K // tk
    ni, nj = M // tm, N // tn
    dn = {"nn": NN, "nt": NT, "tn": TN}[mode]
    c_start, c_finish, c_sp, payload = _comm_fns(comm)
    nc = len(payload)
    n_add = 0 if add is None else 1

    def body(*refs):
        a_ref, b_ref = refs[:2]
        add_ref = refs[2] if n_add else None
        c_in = refs[2 + n_add:2 + n_add + nc]
        o_ref = refs[2 + n_add + nc]
        c_out = refs[3 + n_add + nc:3 + n_add + 2 * nc]
        rest = refs[3 + n_add + 2 * nc:]
        acc_ref = rest[0] if nk > 1 else None
        sems = rest[1:] if nk > 1 else rest
        i, j, k = pl.program_id(0), pl.program_id(1), pl.program_id(2)

        if nc:
            @pl.when(jnp.logical_and(jnp.logical_and(i == 0, j == 0), k == 0))
            def _():
                c_start(c_in, c_out, *sems)

        def finish(r):
            if add is not None:
                r = r + add_ref[...]
            o_ref[...] = r.astype(o_ref.dtype)

        part = lax.dot_general(a_ref[...], b_ref[...], dn, preferred_element_type=F32)
        if nk == 1:
            finish(part)
        else:
            @pl.when(k == 0)
            def _():
                acc_ref[...] = part

            @pl.when(k > 0)
            def _():
                acc_ref[...] += part

            @pl.when(k == nk - 1)
            def _():
                finish(acc_ref[...])

        if nc:
            @pl.when(jnp.logical_and(jnp.logical_and(i == ni - 1, j == nj - 1), k == nk - 1))
            def _():
                c_finish(c_in, c_out, *sems)

    a_spec = (pl.BlockSpec((tk, tm), lambda i, j, k: (k, i)) if mode == "tn"
              else pl.BlockSpec((tm, tk), lambda i, j, k: (i, k)))
    b_spec = (pl.BlockSpec((tn, tk), lambda i, j, k: (j, k)) if mode == "nt"
              else pl.BlockSpec((tk, tn), lambda i, j, k: (k, j)))
    o_spec = pl.BlockSpec((tm, tn), lambda i, j, k: (i, j))
    in_specs = [a_spec, b_spec] + ([o_spec] if add is not None else []) + c_sp["specs"]
    args = (a, b) + ((add,) if add is not None else ()) + payload
    outs = pl.pallas_call(
        body, name=name + ("_" + comm[0] if nc else ""), grid=(ni, nj, nk), in_specs=in_specs,
        out_specs=[o_spec] + c_sp["specs"],
        out_shape=[jax.ShapeDtypeStruct((M, N), out_dtype)] + c_sp["out_shape"],
        scratch_shapes=([pltpu.VMEM((tm, tn), F32)] if nk > 1 else []) + c_sp["sems"],
        compiler_params=_cp("arbitrary", "arbitrary", "arbitrary") if nc else _cp("parallel", "parallel", "arbitrary"),
    )(*args)
    return (outs[0], list(outs[1:])) if nc else outs[0]


def _rms_fwd(x, g, name):
    T, D = x.shape
    tt = _tile(T, 256, 16)

    def body(x_ref, g_ref, o_ref):
        xv = x_ref[...]
        r = lax.rsqrt(jnp.mean(xv * xv, axis=-1, keepdims=True) + EPS)
        o_ref[...] = (xv * r * g_ref[...]).astype(o_ref.dtype)

    return pl.pallas_call(
        body, name=name, grid=(T // tt,),
        in_specs=[pl.BlockSpec((tt, D), lambda i: (i, 0)), pl.BlockSpec((1, D), lambda i: (0, 0))],
        out_specs=pl.BlockSpec((tt, D), lambda i: (i, 0)),
        out_shape=jax.ShapeDtypeStruct((T, D), BF16), compiler_params=_cp("parallel"),
    )(x, g)


def _rms_bwd(x, g, dh, dres, name):
    T, D = x.shape
    tt = _tile(T, 256, 16)

    def body(x_ref, g_ref, dh_ref, dres_ref, dx_ref, dg_ref):
        @pl.when(pl.program_id(0) == 0)
        def _():
            dg_ref[...] = jnp.zeros_like(dg_ref)

        xv = x_ref[...]
        r = lax.rsqrt(jnp.mean(xv * xv, axis=-1, keepdims=True) + EPS)
        xh = xv * r
        dh_v = dh_ref[...]
        dy = dh_v * g_ref[...]
        dx_ref[...] = dres_ref[...] + r * (dy - xh * jnp.mean(dy * xh, axis=-1, keepdims=True))
        dg_ref[...] += jnp.sum(dh_v * xh, axis=0, keepdims=True)

    row = pl.BlockSpec((tt, D), lambda i: (i, 0))
    vec = pl.BlockSpec((1, D), lambda i: (0, 0))
    return pl.pallas_call(
        body, name=name, grid=(T // tt,), in_specs=[row, vec, row, row], out_specs=[row, vec],
        out_shape=[jax.ShapeDtypeStruct((T, D), F32), jax.ShapeDtypeStruct((1, D), F32)],
        compiler_params=_cp("arbitrary"),
    )(x, g, dh, dres)


def _loss_head(x, g, tgt, name="loss_head"):
    T, D = x.shape
    tt = _tile(T, 256, 16)

    def body(x_ref, g_ref, t_ref, dx_ref, dg_ref, loss_ref):
        @pl.when(pl.program_id(0) == 0)
        def _():
            dg_ref[...] = jnp.zeros_like(dg_ref)
            loss_ref[...] = jnp.zeros_like(loss_ref)

        xv = x_ref[...]
        r = lax.rsqrt(jnp.mean(xv * xv, axis=-1, keepdims=True) + EPS)
        xh = xv * r
        err = xh * g_ref[...] - t_ref[...]
        part = 0.5 * jnp.sum(jnp.mean(err * err, axis=-1, keepdims=True), axis=0, keepdims=True)
        loss_ref[...] += jnp.broadcast_to(part, loss_ref.shape)
        dy = err * (1.0 / D)
        dg_ref[...] += jnp.sum(dy * xh, axis=0, keepdims=True)
        dyh = dy * g_ref[...]
        dx_ref[...] = r * (dyh - xh * jnp.mean(dyh * xh, axis=-1, keepdims=True))

    row = pl.BlockSpec((tt, D), lambda i: (i, 0))
    vec = pl.BlockSpec((1, D), lambda i: (0, 0))
    return pl.pallas_call(
        body, name=name, grid=(T // tt,), in_specs=[row, vec, row],
        out_specs=[row, vec, pl.BlockSpec((1, 128), lambda i: (0, 0))],
        out_shape=[jax.ShapeDtypeStruct((T, D), F32), jax.ShapeDtypeStruct((1, D), F32),
                   jax.ShapeDtypeStruct((1, 128), F32)],
        compiler_params=_cp("arbitrary"),
    )(x, g, tgt)


def _ba_fwd(proj, alog, dtb, oba, name="dn_ba_fwd"):
    T = proj.shape[0]
    tt = _tile(T, 512, 8)

    def body(p_ref, al_ref, dt_ref, o_ref):
        raw = p_ref[...].astype(F32)
        lane = lax.broadcasted_iota(jnp.int32, raw.shape, 1)
        z = raw + dt_ref[...]
        sp = jnp.maximum(z, 0.0) + jnp.log(1.0 + jnp.exp(-jnp.abs(z)))
        gl = -jnp.exp(al_ref[...]) * sp
        o_ref[...] = jnp.where(lane < H, _sig(raw), jnp.where(lane < 2 * H, gl, 0.0))

    vec = pl.BlockSpec((1, 128), lambda i: (0, 0))
    return pl.pallas_call(
        body, name=name, grid=(T // tt,),
        in_specs=[pl.BlockSpec((tt, 128), lambda i: (i, oba // 128)), vec, vec],
        out_specs=pl.BlockSpec((tt, 128), lambda i: (i, 0)),
        out_shape=jax.ShapeDtypeStruct((T, 128), F32), compiler_params=_cp("parallel"),
    )(proj, alog, dtb)


def _ba_bwd(proj, alog, dtb, dbg, oba, name="dn_ba_bwd"):
    T = proj.shape[0]
    tt = _tile(T, 512, 16)

    def body(p_ref, al_ref, dt_ref, d_ref, o_ref, dal_ref, ddt_ref):
        @pl.when(pl.program_id(0) == 0)
        def _():
            dal_ref[...] = jnp.zeros_like(dal_ref)
            ddt_ref[...] = jnp.zeros_like(ddt_ref)

        raw = p_ref[...].astype(F32)
        d = d_ref[...]
        lane = lax.broadcasted_iota(jnp.int32, raw.shape, 1)
        z = raw + dt_ref[...]
        sp = jnp.maximum(z, 0.0) + jnp.log(1.0 + jnp.exp(-jnp.abs(z)))
        na = -jnp.exp(al_ref[...])
        is_g = jnp.logical_and(lane >= H, lane < 2 * H)
        b = _sig(raw)
        dz = jnp.where(is_g, d * na * _sig(z), 0.0)
        o_ref[...] = jnp.where(lane < H, d * b * (1.0 - b), dz).astype(o_ref.dtype)
        dal_ref[...] += jnp.sum(jnp.where(is_g, d * na * sp, 0.0), axis=0, keepdims=True)
        ddt_ref[...] += jnp.sum(dz, axis=0, keepdims=True)

    vec = pl.BlockSpec((1, 128), lambda i: (0, 0))
    return pl.pallas_call(
        body, name=name, grid=(T // tt,),
        in_specs=[pl.BlockSpec((tt, 128), lambda i: (i, oba // 128)), vec, vec,
                  pl.BlockSpec((tt, 128), lambda i: (i, 0))],
        out_specs=[pl.BlockSpec((tt, 128), lambda i: (i, 0)), vec, vec],
        out_shape=[jax.ShapeDtypeStruct((T, 128), BF16), jax.ShapeDtypeStruct((1, 128), F32),
                   jax.ShapeDtypeStruct((1, 128), F32)],
        compiler_params=_cp("arbitrary"),
    )(proj, alog, dtb, dbg)


def _dn_prep_fwd(proj, convw, name="dn_prep_fwd"):
    T = proj.shape[0]
    nblk = 3 * H

    def body(p_ref, w_ref, o_ref):
        j = pl.program_id(0)
        xv = p_ref[...].astype(F32)
        w = w_ref[...]
        c = xv * w[DN_K - 1:DN_K, :]
        for k in range(1, DN_K):
            c = c + _shift_down(xv, k) * w[DN_K - 1 - k:DN_K - k, :]
        s = c * _sig(c)
        r = lax.rsqrt(_lsum(s * s) + EPS)
        o_ref[...] = jnp.where(j < 2 * H, s * r, s)

    return pl.pallas_call(
        body, name=name, grid=(nblk,),
        in_specs=[pl.BlockSpec((T, DH), lambda j: (0, j)), pl.BlockSpec((DN_K, DH), lambda j: (0, j))],
        out_specs=pl.BlockSpec((T, DH), lambda j: (0, j)),
        out_shape=jax.ShapeDtypeStruct((T, 3 * WD), F32), compiler_params=_cp("parallel"),
    )(proj, convw)


def _dn_prep_bwd(proj, convw, dq, dk, dv, name="dn_prep_bwd"):
    T = proj.shape[0]
    nblk = 3 * H

    def body(p_ref, w_ref, dq_ref, dk_ref, dv_ref, dx_ref, dw_ref):
        j = pl.program_id(0)
        xv = p_ref[...].astype(F32)
        w = w_ref[...]
        shifted = [_shift_down(xv, k) for k in range(DN_K)]
        c = shifted[0] * w[DN_K - 1:DN_K, :]
        for k in range(1, DN_K):
            c = c + shifted[k] * w[DN_K - 1 - k:DN_K - k, :]
        s = c * _sig(c)
        r = lax.rsqrt(_lsum(s * s) + EPS)
        y = s * r
        dy = jnp.where(j < H, dq_ref[...], jnp.where(j < 2 * H, dk_ref[...], dv_ref[...]))
        ds = jnp.where(j < 2 * H, r * (dy - y * _lsum(dy * y)), dy)
        dc = ds * _dsilu(c)
        dx = dc * w[DN_K - 1:DN_K, :]
        for k in range(1, DN_K):
            dx = dx + _shift_up(dc, k) * w[DN_K - 1 - k:DN_K - k, :]
        dx_ref[...] = dx.astype(dx_ref.dtype)
        rows = [jnp.sum(dc * shifted[DN_K - 1 - t], axis=0, keepdims=True) for t in range(DN_K)]
        dw_ref[...] = jnp.concatenate(rows, axis=0)

    hb = lambda off: pl.BlockSpec((T, DH), lambda j: (0, jnp.maximum(jnp.minimum(j - off, H - 1), 0)))
    return pl.pallas_call(
        body, name=name, grid=(nblk,),
        in_specs=[pl.BlockSpec((T, DH), lambda j: (0, j)), pl.BlockSpec((DN_K, DH), lambda j: (0, j)),
                  hb(0), hb(H), hb(2 * H)],
        out_specs=[pl.BlockSpec((T, DH), lambda j: (0, j)), pl.BlockSpec((DN_K, DH), lambda j: (0, j))],
        out_shape=[jax.ShapeDtypeStruct((T, 3 * WD), BF16), jax.ShapeDtypeStruct((DN_K, 3 * WD), F32)],
        compiler_params=_cp("parallel"),
    )(proj, convw, dq, dk, dv)


DN_BLOCK = 4


def _split3(a):
    hi = a.astype(BF16)
    r1 = a - hi.astype(F32)
    mid = r1.astype(BF16)
    return hi, mid, (r1 - mid.astype(F32)).astype(BF16)


def _dot3(a, b, dn=NN):
    ah, al, _ = _split3(a)
    bh, bl, _ = _split3(b)
    d = lambda p, q: lax.dot_general(p, q, dn, preferred_element_type=F32)
    return d(ah, bh) + d(ah, bl) + d(al, bh)


def _mask_dot(m, b, dn=NN):
    mb = m.astype(BF16)
    d = lambda q: (lax.dot_general(mb, q, dn, preferred_element_type=F32) if dn != TN
                   else lax.dot_general(q, mb, dn, preferred_element_type=F32))
    b0, b1, b2 = _split3(b)
    return d(b0) + d(b1) + d(b2)


def _tri_inv(A):
    ri = lax.broadcasted_iota(jnp.int32, A.shape, 0)
    ci = lax.broadcasted_iota(jnp.int32, A.shape, 1)
    X = -A
    P = jnp.where(ri == ci, 1.0, 0.0) + X
    Y = X
    for _ in range(int(math.log2(DNC)) - 1):
        Y = _dot3(Y, Y)
        P = P + _dot3(P, Y)
    return P


GH = 4
NG = H // GH
GR = GH * DNC
GK = GH * DH


def _dn_masks():
    ri = lax.broadcasted_iota(jnp.int32, (GR, GR), 0)
    ci = lax.broadcasted_iota(jnp.int32, (GR, GR), 1)
    blk = (ri // DNC) == (ci // DNC)
    wide = (lax.broadcasted_iota(jnp.int32, (GR, GK), 0) // DNC) == (lax.broadcasted_iota(jnp.int32, (GR, GK), 1) // DH)
    return dict(blk=blk, causal=jnp.logical_and(blk, ri >= ci), strict=jnp.logical_and(blk, ri > ci),
                upper=jnp.logical_and(blk, ri <= ci), eye=ri == ci, wide=wide)


def _wide(a, mk):
    return jnp.where(mk["wide"], jnp.tile(a, (1, GH)), 0.0)


def _fold(a, mk):
    a = jnp.where(mk["wide"], a, 0.0)
    out = a[:, :DH]
    for j in range(1, GH):
        out = out + a[:, j * DH:(j + 1) * DH]
    return out


def _stack_heads(ref, rows, g):
    return jnp.concatenate([ref[rows, (g * GH + j) * DH:(g * GH + j + 1) * DH] for j in range(GH)], axis=0)


def _dn_group(q_ref, k_ref, v_ref, rows, bg, gc_cols, g, mk):
    heads = [g * GH + j for j in range(GH)]
    col = lambda a, lane: jnp.concatenate([a[:, lane(h):lane(h) + 1] for h in heads], axis=0)
    q = _stack_heads(q_ref, rows, g) * (DH ** -0.5)
    k = _stack_heads(k_ref, rows, g)
    v = _stack_heads(v_ref, rows, g)
    beta = col(bg, lambda h: h)
    gcol = col(gc_cols, lambda h: H + h)
    last = [gc_cols[DNC - 1:DNC, H + h:H + h + 1] for h in heads]
    gl = jnp.concatenate([jnp.broadcast_to(t, (DNC, 1)) for t in last], axis=0)
    egl_state = jnp.concatenate([jnp.broadcast_to(jnp.exp(t), (DH, 1)) for t in last], axis=0)
    grow = _mask_dot(jnp.ones((GR, GR), F32), jnp.where(mk["eye"], gcol, 0.0))
    dec = jnp.where(mk["causal"], jnp.exp(jnp.where(mk["causal"], gcol - grow, 0.0)), 0.0)
    eg = jnp.exp(gcol)
    ek = jnp.exp(gl - gcol)
    kb = k * beta
    vb = v * beta
    kbe = kb * eg
    A = jnp.where(mk["strict"], _bdot(kb, k, NT) * dec, 0.0)
    P = jnp.where(mk["causal"], _bdot(q, k, NT) * dec, 0.0)
    return dict(q=q, k=k, v=v, beta=beta, dec=dec, eg=eg, ek=ek, egl=jnp.exp(gl), egl_state=egl_state, kb=kb, vb=vb,
                kbe=kbe, A=A, P=P, qd=q * eg, kd=k * ek, heads=heads)


def _gc_cols(bg):
    ri = lax.broadcasted_iota(jnp.int32, (DNC, DNC), 0)
    ci = lax.broadcasted_iota(jnp.int32, (DNC, DNC), 1)
    return _mask_dot(jnp.where(ri >= ci, 1.0, 0.0), bg)


def _dn_core_fwd(qkv, bg, comm=None, name="dn_core_fwd"):
    c_start, c_finish, sp, gather = _comm_fns(comm)
    T = qkv.shape[0]
    n_chunks = T // DNC
    nb = _tile(n_chunks, DN_BLOCK, 1)
    tb = nb * DNC

    ng = len(gather)
    n_steps = n_chunks // nb

    def body(*refs):
        q_ref, k_ref, v_ref, bg_ref = refs[:4]
        o_ref, s_ref, tm_ref = refs[4 + ng:7 + ng]
        S_scr = refs[7 + 2 * ng]
        comm_refs = (refs[4:4 + ng], refs[7 + ng:7 + 2 * ng]) + tuple(refs[8 + 2 * ng:])

        @pl.when(pl.program_id(0) == 0)
        def _():
            S_scr[...] = jnp.zeros_like(S_scr)
            if ng:
                c_start(*comm_refs)

        def chunk(n, carry):
            rows = pl.ds(pl.multiple_of(n * DNC, DNC), DNC)
            mk = _dn_masks()
            bgc = bg_ref[rows, :]
            gc_cols = _gc_cols(bgc)
            for g in range(NG):
                c = _dn_group(q_ref, k_ref, v_ref, rows, bgc, gc_cols, g, mk)
                Tm = _tri_inv(c["A"])
                tm_ref[n, g] = Tm
                S = S_scr[g]
                s_ref[n, g] = S
                u = _bdot(Tm, c["vb"])
                w = _bdot(Tm, c["kbe"])
                vn = u - _bdot(_wide(w, mk), S)
                o = _bdot(_wide(c["qd"], mk), S) + _bdot(c["P"], vn)
                for j, h in enumerate(c["heads"]):
                    o_ref[rows, h * DH:(h + 1) * DH] = o[j * DNC:(j + 1) * DNC]
                S_scr[g] = S * c["egl_state"] + _bdot(_wide(c["kd"], mk), vn, TN)
            return carry

        lax.fori_loop(0, nb, chunk, 0)

        if ng:
            @pl.when(pl.program_id(0) == n_steps - 1)
            def _():
                c_finish(*comm_refs)

    blk = lambda j: pl.BlockSpec((tb, WD), lambda i: (i, j))
    outs = pl.pallas_call(
        body, name=name + ("_" + comm[0] if ng else ""), grid=(n_steps,),
        in_specs=[blk(0), blk(1), blk(2), pl.BlockSpec((tb, 128), lambda i: (i, 0))] + sp["specs"],
        out_specs=[blk(0), pl.BlockSpec((nb, NG, GK, DH), lambda i: (i, 0, 0, 0)),
                   pl.BlockSpec((nb, NG, GR, GR), lambda i: (i, 0, 0, 0))] + sp["specs"],
        out_shape=[jax.ShapeDtypeStruct((T, WD), F32), jax.ShapeDtypeStruct((n_chunks, NG, GK, DH), F32),
                   jax.ShapeDtypeStruct((n_chunks, NG, GR, GR), F32)] + sp["out_shape"],
        scratch_shapes=[pltpu.VMEM((NG, GK, DH), F32)] + (sp["sems"] if ng else []),
        compiler_params=_cp("arbitrary"),
    )(qkv, qkv, qkv, bg, *gather)
    return outs[0], outs[1], outs[2], list(outs[3:])


def _dn_core_bwd(qkv, bg, s_all, tm_all, do, comm=None, name="dn_core_bwd"):
    c_start, c_finish, sp, exchange = _comm_fns(comm)
    T = qkv.shape[0]
    n_chunks = T // DNC
    nb = _tile(n_chunks, DN_BLOCK, 1)
    tb = nb * DNC
    n_blocks = n_chunks // nb

    nx = len(exchange)

    def body(*refs):
        q_ref, k_ref, v_ref, bg_ref, s_ref, tm_ref, do_ref = refs[:7]
        dq_ref, dk_ref, dv_ref, dbg_ref = refs[7 + nx:11 + nx]
        dS_scr = refs[11 + 2 * nx]
        comm_refs = (refs[7:7 + nx], refs[11 + nx:11 + 2 * nx]) + tuple(refs[12 + 2 * nx:])

        @pl.when(pl.program_id(0) == 0)
        def _():
            dS_scr[...] = jnp.zeros_like(dS_scr)
            if nx:
                c_start(*comm_refs)

        lane = lax.broadcasted_iota(jnp.int32, (DNC, 128), 1)
        row = lax.broadcasted_iota(jnp.int32, (GR, 1), 0)

        def chunk(i, carry):
            n = nb - 1 - i
            rows = pl.ds(pl.multiple_of(n * DNC, DNC), DNC)
            mk = _dn_masks()
            ones = jnp.ones((GR, GR), F32)
            blk_f = jnp.where(mk["blk"], 1.0, 0.0)
            wide_f = jnp.where(mk["wide"], 1.0, 0.0)
            per_row = lambda m, a: _mask_dot(m, jnp.broadcast_to(a, (a.shape[0], DH)))[:, :1]
            bgc = bg_ref[rows, :]
            gc_cols = _gc_cols(bgc)
            dbg = jnp.zeros((DNC, 128), F32)
            for g in range(NG):
                c = _dn_group(q_ref, k_ref, v_ref, rows, bgc, gc_cols, g, mk)
                q, k, v, beta = c["q"], c["k"], c["v"], c["beta"]
                dec, eg, ek, egl = c["dec"], c["eg"], c["ek"], c["egl"]
                kb, vb, kbe, A, P, qd, kd = c["kb"], c["vb"], c["kbe"], c["A"], c["P"], c["qd"], c["kd"]
                S = s_ref[n, g]
                Tm = tm_ref[n, g]
                u = _bdot(Tm, vb)
                w = _bdot(Tm, kbe)
                w_wide = _wide(w, mk)
                vn = u - _bdot(w_wide, S)
                d_o = _stack_heads(do_ref, rows, g)
                dS1 = dS_scr[g]
                d_qd = _fold(_bdot(d_o, S, NT), mk)
                dP = jnp.where(mk["causal"], _bdot(d_o, vn, NT), 0.0)
                d_vn = _bdot(P, d_o, TN) + _bdot(_wide(kd, mk), dS1)
                d_kd = _fold(_bdot(vn, dS1, NT), mk)
                d_egl = per_row(wide_f, _lsum(dS1 * S))
                dS_scr[g] = dS1 * c["egl_state"] + _bdot(_wide(qd, mk), d_o, TN) - _bdot(w_wide, d_vn, TN)
                d_w = -_fold(_bdot(d_vn, S, NT), mk)
                d_vb = _bdot(Tm, d_vn, TN)
                d_kbe = _bdot(Tm, d_w, TN)
                dA = jnp.where(mk["strict"], -(_bdot(d_vb, u, NT) + _bdot(d_kbe, w, NT)), 0.0)
                dMA = dA * dec
                dMP = dP * dec
                d_kb = _bdot(dMA, k) + d_kbe * eg
                d_k = _bdot(dMA, kb, TN) + _bdot(dMP, q, TN) + d_kd * ek + d_kb * beta
                d_qs = (_bdot(dMP, k) + d_qd * eg) * (DH ** -0.5)
                d_v = d_vb * beta
                E = dA * A + dP * P
                col_sums = _mask_dot(ones, E, TN)[:, :1]
                t_kd = _lsum(d_kd * kd)
                d_gl = per_row(blk_f, t_kd) + d_egl * egl
                d_gc = (_lsum(E) - col_sums + _lsum(d_qd * qd) + _lsum(d_kbe * kbe) - t_kd
                        + jnp.where(row % DNC == DNC - 1, d_gl, 0.0))
                d_g = per_row(jnp.where(mk["upper"], 1.0, 0.0), d_gc)
                d_beta = _lsum(d_kb * k) + _lsum(d_vb * v)
                for j, h in enumerate(c["heads"]):
                    rs = slice(j * DNC, (j + 1) * DNC)
                    dq_ref[rows, h * DH:(h + 1) * DH] = d_qs[rs]
                    dk_ref[rows, h * DH:(h + 1) * DH] = d_k[rs]
                    dv_ref[rows, h * DH:(h + 1) * DH] = d_v[rs]
                    dbg = dbg + jnp.where(lane == h, d_beta[rs], 0.0) + jnp.where(lane == h + H, d_g[rs], 0.0)
            dbg_ref[rows, :] = dbg
            return carry

        lax.fori_loop(0, nb, chunk, 0)

        if nx:
            @pl.when(pl.program_id(0) == n_blocks - 1)
            def _():
                c_finish(*comm_refs)

    blk = lambda j: pl.BlockSpec((tb, WD), lambda i: (n_blocks - 1 - i, j))
    small = pl.BlockSpec((tb, 128), lambda i: (n_blocks - 1 - i, 0))
    outs = pl.pallas_call(
        body, name=name + ("_" + comm[0] if nx else ""), grid=(n_blocks,),
        in_specs=[blk(0), blk(1), blk(2), small,
                  pl.BlockSpec((nb, NG, GK, DH), lambda i: (n_blocks - 1 - i, 0, 0, 0)),
                  pl.BlockSpec((nb, NG, GR, GR), lambda i: (n_blocks - 1 - i, 0, 0, 0)), blk(0)] + sp["specs"],
        out_specs=[blk(0), blk(0), blk(0), small] + sp["specs"],
        out_shape=[jax.ShapeDtypeStruct((T, WD), F32)] * 3 + [jax.ShapeDtypeStruct((T, 128), F32)] + sp["out_shape"],
        scratch_shapes=[pltpu.VMEM((NG, GK, DH), F32)] + (sp["sems"] if nx else []),
        compiler_params=_cp("arbitrary"),
    )(qkv, qkv, qkv, bg, s_all, tm_all, do, *exchange)
    return outs[0], outs[1], outs[2], outs[3], list(outs[4:])


def _dn_post_fwd(o, proj, gon, name="dn_post_fwd"):
    T = o.shape[0]
    tt = _tile(T, 256, 16)

    def body(o_ref, z_ref, g_ref, y_ref):
        for hh in range(H):
            sl = slice(hh * DH, (hh + 1) * DH)
            ov = o_ref[:, sl]
            zv = z_ref[:, sl].astype(F32)
            r = lax.rsqrt(jnp.mean(ov * ov, axis=-1, keepdims=True) + EPS)
            y_ref[:, sl] = (ov * r * g_ref[...] * (zv * _sig(zv))).astype(y_ref.dtype)

    return pl.pallas_call(
        body, name=name, grid=(T // tt,),
        in_specs=[pl.BlockSpec((tt, WD), lambda i: (i, 0)), pl.BlockSpec((tt, WD), lambda i: (i, OZ // WD)),
                  pl.BlockSpec((1, DH), lambda i: (0, 0))],
        out_specs=pl.BlockSpec((tt, WD), lambda i: (i, 0)),
        out_shape=jax.ShapeDtypeStruct((T, WD), BF16), compiler_params=_cp("parallel"),
    )(o, proj, gon)


def _dn_post_bwd(o, proj, gon, dy, name="dn_post_bwd"):
    T = o.shape[0]
    tt = _tile(T, 256, 16)

    def body(o_ref, z_ref, g_ref, dy_ref, do_ref, dz_ref, dg_ref):
        @pl.when(pl.program_id(0) == 0)
        def _():
            dg_ref[...] = jnp.zeros_like(dg_ref)

        acc = jnp.zeros((1, DH), F32)
        for hh in range(H):
            sl = slice(hh * DH, (hh + 1) * DH)
            ov = o_ref[:, sl]
            zv = z_ref[:, sl].astype(F32)
            dyv = dy_ref[:, sl]
            r = lax.rsqrt(jnp.mean(ov * ov, axis=-1, keepdims=True) + EPS)
            oh = ov * r
            nrm = oh * g_ref[...]
            dn = dyv * (zv * _sig(zv))
            dz_ref[:, sl] = (dyv * nrm * _dsilu(zv)).astype(dz_ref.dtype)
            doh = dn * g_ref[...]
            do_ref[:, sl] = r * (doh - oh * jnp.mean(doh * oh, axis=-1, keepdims=True))
            acc = acc + jnp.sum(dn * oh, axis=0, keepdims=True)
        dg_ref[...] += acc

    row = pl.BlockSpec((tt, WD), lambda i: (i, 0))
    vec = pl.BlockSpec((1, DH), lambda i: (0, 0))
    return pl.pallas_call(
        body, name=name, grid=(T // tt,),
        in_specs=[row, pl.BlockSpec((tt, WD), lambda i: (i, OZ // WD)), vec, row],
        out_specs=[row, row, vec],
        out_shape=[jax.ShapeDtypeStruct((T, WD), F32), jax.ShapeDtypeStruct((T, WD), BF16),
                   jax.ShapeDtypeStruct((1, DH), F32)],
        compiler_params=_cp("arbitrary"),
    )(o, proj, gon, dy)


def _sg_common(u_ref, v_ref, lng_ref, lnb_ref):
    ur = u_ref[...].astype(F32)
    vr = v_ref[...].astype(F32)
    vgel = _gelu(vr)
    mu = jnp.mean(vgel, axis=-1, keepdims=True)
    xc = vgel - mu
    rs = lax.rsqrt(jnp.mean(xc * xc, axis=-1, keepdims=True) + EPS)
    xh = xc * rs
    vg = xh * lng_ref[...] + lnb_ref[...]
    return ur, vr, rs, xh, vg


def _sg_fwd(proj, lng, lnb, sgw, sgbt, name="sg_fwd"):
    T = proj.shape[0]

    def body(u_ref, v_ref, lng_ref, lnb_ref, w_ref, bt_ref, y_ref):
        ur, _, _, _, vg = _sg_common(u_ref, v_ref, lng_ref, lnb_ref)
        ri = lax.broadcasted_iota(jnp.int32, (SGC, SGC), 0)
        ci = lax.broadcasted_iota(jnp.int32, (SGC, SGC), 1)
        ug = _gelu(ur)
        for g in range(H):
            sl = slice(g * DH, (g + 1) * DH)
            ws = jnp.where(ri >= ci, w_ref[g], 0.0)
            mixed = _bdot(ws, vg[:, sl]) + bt_ref[:, g:g + 1]
            y_ref[:, sl] = (ug[:, sl] * mixed).astype(y_ref.dtype)

    vec = pl.BlockSpec((1, WD), lambda i: (0, 0))
    return pl.pallas_call(
        body, name=name, grid=(T // SGC,),
        in_specs=[pl.BlockSpec((SGC, WD), lambda i: (i, OU // WD)), pl.BlockSpec((SGC, WD), lambda i: (i, OV // WD)),
                  vec, vec, pl.BlockSpec((H, SGC, SGC), lambda i: (0, 0, 0)),
                  pl.BlockSpec((SGC, H), lambda i: (0, 0))],
        out_specs=pl.BlockSpec((SGC, WD), lambda i: (i, 0)),
        out_shape=jax.ShapeDtypeStruct((T, WD), BF16), compiler_params=_cp("parallel"),
    )(proj, proj, lng, lnb, sgw, sgbt)


def _sg_bwd(proj, lng, lnb, sgw, sgbt, dy, name="sg_bwd"):
    T = proj.shape[0]

    def body(u_ref, v_ref, lng_ref, lnb_ref, w_ref, bt_ref, dy_ref,
             du_ref, dv_ref, dw_ref, dbt_ref, dlng_ref, dlnb_ref):
        @pl.when(pl.program_id(0) == 0)
        def _():
            dw_ref[...] = jnp.zeros_like(dw_ref)
            dbt_ref[...] = jnp.zeros_like(dbt_ref)
            dlng_ref[...] = jnp.zeros_like(dlng_ref)
            dlnb_ref[...] = jnp.zeros_like(dlnb_ref)

        ur, vr, rs, xh, vg = _sg_common(u_ref, v_ref, lng_ref, lnb_ref)
        ri = lax.broadcasted_iota(jnp.int32, (SGC, SGC), 0)
        ci = lax.broadcasted_iota(jnp.int32, (SGC, SGC), 1)
        ug = _gelu(ur)
        dyv = dy_ref[...]
        dbt = jnp.zeros((SGC, 128), F32)
        dvg_parts = []
        for g in range(H):
            sl = slice(g * DH, (g + 1) * DH)
            ws = jnp.where(ri >= ci, w_ref[g], 0.0)
            mixed = _bdot(ws, vg[:, sl]) + bt_ref[:, g:g + 1]
            dyg = dyv[:, sl]
            du_ref[:, sl] = (dyg * mixed * _dgelu(ur[:, sl])).astype(du_ref.dtype)
            dmix = dyg * ug[:, sl]
            dw_ref[g] += jnp.where(ri >= ci, _bdot(dmix, vg[:, sl], NT), 0.0)
            dbt = dbt + jnp.where(ci == g, _lsum(dmix), 0.0)
            dvg_parts.append(_bdot(ws, dmix, TN))
        dbt_ref[...] += dbt
        dvg = jnp.concatenate(dvg_parts, axis=1)
        dlng_ref[...] += jnp.sum(dvg * xh, axis=0, keepdims=True)
        dlnb_ref[...] += jnp.sum(dvg, axis=0, keepdims=True)
        dxh = dvg * lng_ref[...]
        dvgel = rs * (dxh - jnp.mean(dxh, axis=-1, keepdims=True) - xh * jnp.mean(dxh * xh, axis=-1, keepdims=True))
        dv_ref[...] = (dvgel * _dgelu(vr)).astype(dv_ref.dtype)

    vec = pl.BlockSpec((1, WD), lambda i: (0, 0))
    row = pl.BlockSpec((SGC, WD), lambda i: (i, 0))
    return pl.pallas_call(
        body, name=name, grid=(T // SGC,),
        in_specs=[pl.BlockSpec((SGC, WD), lambda i: (i, OU // WD)), pl.BlockSpec((SGC, WD), lambda i: (i, OV // WD)),
                  vec, vec, pl.BlockSpec((H, SGC, SGC), lambda i: (0, 0, 0)),
                  pl.BlockSpec((SGC, H), lambda i: (0, 0)), row],
        out_specs=[row, row, pl.BlockSpec((H, SGC, SGC), lambda i: (0, 0, 0)),
                   pl.BlockSpec((SGC, 128), lambda i: (0, 0)), vec, vec],
        out_shape=[jax.ShapeDtypeStruct((T, WD), BF16), jax.ShapeDtypeStruct((T, WD), BF16),
                   jax.ShapeDtypeStruct((H, SGC, SGC), F32), jax.ShapeDtypeStruct((SGC, 128), F32),
                   jax.ShapeDtypeStruct((1, WD), F32), jax.ShapeDtypeStruct((1, WD), F32)],
        compiler_params=_cp("arbitrary"),
    )(proj, proj, lng, lnb, sgw, sgbt, dy)


def _merge_fwd(proj, yap, ybp, D, name="merge_fwd"):
    T = proj.shape[0]
    tt = _tile(T, 256, 16)

    def body(ga_ref, gb_ref, a_ref, b_ref, o_ref):
        ga, gb, a, b = [r[...].astype(F32) for r in (ga_ref, gb_ref, a_ref, b_ref)]
        o_ref[...] = (_sig(ga) * a + _sig(gb) * b).astype(o_ref.dtype)

    row = pl.BlockSpec((tt, D), lambda i: (i, 0))
    return pl.pallas_call(
        body, name=name, grid=(T // tt,),
        in_specs=[pl.BlockSpec((tt, D), lambda i: (i, OGA // D)), pl.BlockSpec((tt, D), lambda i: (i, OGA // D + 1)),
                  row, row],
        out_specs=row, out_shape=jax.ShapeDtypeStruct((T, D), BF16), compiler_params=_cp("parallel"),
    )(proj, proj, yap, ybp)


def _merge_bwd(proj, yap, ybp, dm, D, name="merge_bwd"):
    T = proj.shape[0]
    tt = _tile(T, 256, 16)

    def body(ga_ref, gb_ref, a_ref, b_ref, dm_ref, da_ref, db_ref, dga_ref, dgb_ref):
        d, ga, gb, a, b = [r[...].astype(F32) for r in (dm_ref, ga_ref, gb_ref, a_ref, b_ref)]
        sa = _sig(ga)
        sb = _sig(gb)
        da_ref[...] = (d * sa).astype(da_ref.dtype)
        db_ref[...] = (d * sb).astype(db_ref.dtype)
        dga_ref[...] = (d * a * sa * (1.0 - sa)).astype(dga_ref.dtype)
        dgb_ref[...] = (d * b * sb * (1.0 - sb)).astype(dgb_ref.dtype)

    row = pl.BlockSpec((tt, D), lambda i: (i, 0))
    return pl.pallas_call(
        body, name=name, grid=(T // tt,),
        in_specs=[pl.BlockSpec((tt, D), lambda i: (i, OGA // D)), pl.BlockSpec((tt, D), lambda i: (i, OGA // D + 1)),
                  row, row, row],
        out_specs=[row] * 4, out_shape=[jax.ShapeDtypeStruct((T, D), BF16)] * 4,
        compiler_params=_cp("parallel"),
    )(proj, proj, yap, ybp, dm)


def _ffn_act_fwd(gp, up, cw, cb, name="ffn_act_fwd"):
    T, F = gp.shape

    def body(g_ref, u_ref, w_ref, b_ref, o_ref):
        gv = g_ref[...].astype(F32)
        w = w_ref[...]
        c = gv * w[FF_K - 1:FF_K, :] + b_ref[...]
        for k in range(1, FF_K):
            c = c + _shift_down(gv, k) * w[FF_K - 1 - k:FF_K - k, :]
        o_ref[...] = (c * _sig(c) * u_ref[...].astype(F32)).astype(o_ref.dtype)

    col = pl.BlockSpec((T, 128), lambda j: (0, j))
    return pl.pallas_call(
        body, name=name, grid=(F // 128,),
        in_specs=[col, col, pl.BlockSpec((FF_K, 128), lambda j: (0, j)), pl.BlockSpec((1, 128), lambda j: (0, j))],
        out_specs=col, out_shape=jax.ShapeDtypeStruct((T, F), BF16), compiler_params=_cp("parallel"),
    )(gp, up, cw, cb)


def _ffn_act_bwd(gp, up, cw, cb, dact, name="ffn_act_bwd"):
    T, F = gp.shape

    def body(g_ref, u_ref, w_ref, b_ref, d_ref, dg_ref, du_ref, dw_ref, db_ref):
        gv = g_ref[...].astype(F32)
        w = w_ref[...]
        shifted = [_shift_down(gv, k) for k in range(FF_K)]
        c = shifted[0] * w[FF_K - 1:FF_K, :] + b_ref[...]
        for k in range(1, FF_K):
            c = c + shifted[k] * w[FF_K - 1 - k:FF_K - k, :]
        d = d_ref[...].astype(F32)
        du_ref[...] = (d * c * _sig(c)).astype(du_ref.dtype)
        dc = d * u_ref[...].astype(F32) * _dsilu(c)
        dg = dc * w[FF_K - 1:FF_K, :]
        for k in range(1, FF_K):
            dg = dg + _shift_up(dc, k) * w[FF_K - 1 - k:FF_K - k, :]
        dg_ref[...] = dg.astype(dg_ref.dtype)
        rows = [jnp.sum(dc * shifted[FF_K - 1 - t], axis=0, keepdims=True) for t in range(FF_K)]
        dw_ref[...] = jnp.concatenate(rows, axis=0)
        db_ref[...] = jnp.sum(dc, axis=0, keepdims=True)

    col = pl.BlockSpec((T, 128), lambda j: (0, j))
    wspec = pl.BlockSpec((FF_K, 128), lambda j: (0, j))
    bspec = pl.BlockSpec((1, 128), lambda j: (0, j))
    return pl.pallas_call(
        body, name=name, grid=(F // 128,),
        in_specs=[col, col, wspec, bspec, col], out_specs=[col, col, wspec, bspec],
        out_shape=[jax.ShapeDtypeStruct((T, F), BF16), jax.ShapeDtypeStruct((T, F), BF16),
                   jax.ShapeDtypeStruct((FF_K, F), F32), jax.ShapeDtypeStruct((1, F), F32)],
        compiler_params=_cp("parallel"),
    )(gp, up, cw, cb, dact)


class _Carrier:
    def __init__(self, plan=None, deliver=None):
        self.plan, self.deliver = plan or (lambda kernel: None), deliver

    def run(self, kernel, fn, **kw):
        comm = self.plan(kernel)
        out = fn(comm=comm, **kw)
        if comm:
            self.deliver(kernel, out[-1])
            out = out[:-1]
            return out[0] if len(out) == 1 else out
        return out


def _layer_fwd(x, w, carrier=None):
    cr = carrier or _Carrier()
    D = x.shape[1]
    oba = OGA + 2 * D
    h = _rms_fwd(x, w["norm1_g"], "rms1_fwd")
    proj = cr.run("proj", functools.partial(_mm, h, w["w_in_t"], "nt", BF16, name="mm_proj"))
    bg = _ba_fwd(proj, w["alog_row"], w["dtb_row"], oba)
    qkv = _dn_prep_fwd(proj, w["dn_conv_w"])
    r = cr.run("dn_core", functools.partial(_dn_core_fwd, qkv, bg))
    o, s_all, tm_all = r[0], r[1], r[2]
    ya = _dn_post_fwd(o, proj, w["dn_onorm_g"])
    yb = _sg_fwd(proj, w["sg_ln_g"], w["sg_ln_b"], w["sg_w"], w["sg_bt"])
    yap = _mm(ya, w["w_branch_a"], "nn", BF16, name="mm_branch")
    ybp = _mm(yb, w["w_branch_b"], "nn", BF16, name="mm_branch")
    merged = _merge_fwd(proj, yap, ybp, D)
    x1 = _mm(merged, w["w_out"], "nn", F32, add=x, name="mm_out")
    h2 = _rms_fwd(x1, w["norm2_g"], "rms2_fwd")
    gp = cr.run("ffn_gate", functools.partial(_mm, h2, w["ffn_w_gate"], "nn", BF16, name="mm_ffn_in"))
    up = cr.run("ffn_up", functools.partial(_mm, h2, w["ffn_w_up"], "nn", BF16, name="mm_ffn_in"))
    act = _ffn_act_fwd(gp, up, w["ffn_conv_w"], w["ffn_conv_b"])
    x2 = cr.run("ffn_down", functools.partial(_mm, act, w["ffn_w_down"], "nn", F32, add=x1, name="mm_ffn_down"))
    saved = dict(x=x, h=h, proj=proj, bg=bg, qkv=qkv, o=o, s_all=s_all, tm_all=tm_all, ya=ya, yb=yb, yap=yap,
                 ybp=ybp, merged=merged, x1=x1, h2=h2, gp=gp, up=up, act=act)
    return x2, saved


def _layer_bwd(dx2, w, s, carrier=None, ffn_grads_ready=None):
    cr = carrier or _Carrier()
    D = dx2.shape[1]
    oba = OGA + 2 * D
    g = {}
    dx2b = dx2.astype(BF16)
    dact = _mm(dx2b, w["ffn_w_down"], "nt", BF16, name="mm_d_act")
    g["ffn_w_down"] = _mm(s["act"], dx2b, "tn", BF16, name="mm_dw_down")
    dgp, dup, g["ffn_conv_w"], g["ffn_conv_b"] = _ffn_act_bwd(s["gp"], s["up"], w["ffn_conv_w"], w["ffn_conv_b"], dact)
    dh2 = _mm(dgp, w["ffn_w_gate"], "nt", F32, name="mm_dh2")
    dh2 = _mm(dup, w["ffn_w_up"], "nt", F32, add=dh2, name="mm_dh2_acc")
    g["ffn_w_gate"] = _mm(s["h2"], dgp, "tn", BF16, name="mm_dw_ffn_in")
    g["ffn_w_up"] = _mm(s["h2"], dup, "tn", BF16, name="mm_dw_ffn_in")
    if ffn_grads_ready:
        ffn_grads_ready(g)
    dx1, g["norm2_g"] = _rms_bwd(s["x1"], w["norm2_g"], dh2, dx2, "rms2_bwd")
    dx1b = dx1.astype(BF16)
    dm = _mm(dx1b, w["w_out"], "nt", BF16, name="mm_d_merged")
    g["w_out"] = _mm(s["merged"], dx1b, "tn", BF16, name="mm_dw_out")
    dyap, dybp, dga, dgb = _merge_bwd(s["proj"], s["yap"], s["ybp"], dm, D)
    dya = _mm(dyap, w["w_branch_a"], "nt", F32, name="mm_d_branch")
    dyb = _mm(dybp, w["w_branch_b"], "nt", F32, name="mm_d_branch")
    g["w_branch_a"] = _mm(s["ya"], dyap, "tn", BF16, name="mm_dw_branch")
    g["w_branch_b"] = _mm(s["yb"], dybp, "tn", BF16, name="mm_dw_branch")
    du, dv, g["sg_w"], dbt, g["sg_ln_g"], g["sg_ln_b"] = _sg_bwd(
        s["proj"], w["sg_ln_g"], w["sg_ln_b"], w["sg_w"], w["sg_bt"], dyb)
    g["sg_b"] = jnp.transpose(dbt[:, :H])
    do, dz, g["dn_onorm_g"] = _dn_post_bwd(s["o"], s["proj"], w["dn_onorm_g"], dya)
    r = cr.run("dn_core", functools.partial(_dn_core_bwd, s["qkv"], s["bg"], s["s_all"], s["tm_all"], do))
    dq, dk, dvv, dbg = r[0], r[1], r[2], r[3]
    dqkv, g["dn_conv_w"] = _dn_prep_bwd(s["proj"], w["dn_conv_w"], dq, dk, dvv)
    dba, dal, ddt = _ba_bwd(s["proj"], w["alog_row"], w["dtb_row"], dbg, oba)
    g["dn_a_log"] = dal[0, H:2 * H]
    g["dn_dt_bias"] = ddt[0, H:2 * H]
    dproj = jnp.concatenate([dqkv, dz, du, dv, dga, dgb, dba], axis=1)
    dh = cr.run("dh", functools.partial(_mm, dproj, w["w_in_t"], "nn", F32, name="mm_dh"))
    g["w_in_t"] = cr.run("dw_in", functools.partial(_mm, dproj, s["h"], "tn", BF16, name="mm_dw_in"))
    dx, g["norm1_g"] = _rms_bwd(s["x"], w["norm1_g"], dh, dx1, "rms1_bwd")
    return dx, g


def _local_step(x, tgt, layers, final_g):
    saved = []
    for w in layers:
        x, s = _layer_fwd(x, w)
        saved.append(s)
    dx, dgf, loss = _loss_head(x, final_g, tgt)
    grads = [None] * len(layers)
    for l in reversed(range(len(layers))):
        dx, grads[l] = _layer_bwd(dx, layers[l], saved[l])
    return loss[0, 0], dx, grads, dgf


def _w_in_pad(wt):
    c1 = 4 * WD
    return jnp.concatenate([wt[:c1], wt[c1 + 2 * H:], wt[c1:c1 + 2 * H],
                            jnp.zeros((128 - 2 * H, wt.shape[1]), wt.dtype)], axis=0)


def _w_in_unpad(gt):
    c1 = 4 * WD
    n = gt.shape[0] - 128
    return jnp.concatenate([gt[:c1], gt[n:n + 2 * H], gt[c1:n]], axis=0)


def _row128(v, off):
    return jnp.pad(v, (off, 128 - off - v.shape[0]))[None]


def _prep_small(p):
    return dict(
        norm1_g=p["norm1_g"][None], alog_row=_row128(p["dn_a_log"], H), dtb_row=_row128(p["dn_dt_bias"], H),
        dn_conv_w=p["dn_conv_w"], dn_onorm_g=p["dn_onorm_g"][None],
        sg_ln_g=p["sg_ln_g"][None], sg_ln_b=p["sg_ln_b"][None], sg_w=p["sg_w"], sg_bt=jnp.transpose(p["sg_b"]),
        norm2_g=p["norm2_g"][None], ffn_conv_w=p["ffn_conv_w"], ffn_conv_b=p["ffn_conv_b"][None])


def _prep_layer(p):
    return dict(_prep_small(p), w_in_t=_w_in_pad(p["w_in_t"]),
                **{n: p[n] for n in ("w_branch_a", "w_branch_b", "w_out", "ffn_w_gate", "ffn_w_up", "ffn_w_down")})


HBM_SPEC = pl.BlockSpec(memory_space=pltpu.HBM)


def _coords():
    return lax.axis_index("x"), lax.axis_index("y"), lax.axis_index("c")


def _other_chips(x, y):
    return [(1 - x, y), (x, 1 - y), (1 - x, 1 - y)]


def _remote(src, dst, send_sems, recv_sems, k, dev):
    return pltpu.make_async_remote_copy(src_ref=src, dst_ref=dst, send_sem=send_sems.at[k], recv_sem=recv_sems.at[k],
                                        device_id=dev, device_id_type=MESH)


def _ag_copies(w_refs, o_refs, send_sems, recv_sems):
    x, y, c = _coords()
    me = 2 * x + y
    chips = _other_chips(x, y)

    def ici(k, j, owner):
        chip = chips[j]
        return _remote(w_refs[k].at[c], o_refs[k].at[owner, c], send_sems, recv_sems, 6 * k + j, (chip[0], chip[1], c))

    def d2d(k, j, part):
        owner = 2 * chips[j][0] + chips[j][1]
        return _remote(o_refs[k].at[owner, part], o_refs[k].at[owner, part], send_sems, recv_sems, 6 * k + 3 + j,
                       (x, y, 1 - c))

    n = len(w_refs)
    return me, c, chips, ici, d2d, [(k, j) for k in range(n) for j in range(3)]


def _ag_start(w_refs, o_refs, send_sems, recv_sems):
    me, _, _, ici, _, pairs = _ag_copies(w_refs, o_refs, send_sems, recv_sems)
    for k, j in pairs:
        ici(k, j, me).start()


def _ag_finish(w_refs, o_refs, send_sems, recv_sems):
    me, c, chips, ici, d2d, pairs = _ag_copies(w_refs, o_refs, send_sems, recv_sems)
    for k, j in pairs:
        ici(k, j, 2 * chips[j][0] + chips[j][1]).wait_recv()
        d2d(k, j, c).start()
    for k, j in pairs:
        d2d(k, j, 1 - c).wait_recv()
    for k, j in pairs:
        ici(k, j, me).wait_send()
        d2d(k, j, c).wait_send()


def _ag_specs(ws):
    n = len(ws)
    return dict(out_shape=[jax.ShapeDtypeStruct((N_CHIPS,) + w.shape, w.dtype) for w in ws],
                specs=[HBM_SPEC] * n, sems=[pltpu.SemaphoreType.DMA((6 * n,)), pltpu.SemaphoreType.DMA((6 * n,))])


def _ag_layers(ws):
    n = len(ws)

    def body(*refs):
        _ag_start(refs[:n], refs[n:2 * n], *refs[2 * n:])
        _ag_finish(refs[:n], refs[n:2 * n], *refs[2 * n:])

    sp = _ag_specs(ws)
    return pl.pallas_call(
        body, name="ag_weights", out_shape=sp["out_shape"], in_specs=sp["specs"], out_specs=sp["specs"],
        scratch_shapes=sp["sems"],
    )(*ws)


def _rs_pair_exchange(Gs):
    n = len(Gs)

    def body(*refs):
        g_refs, b_refs = refs[:n], refs[n:2 * n]
        send_sems, recv_sems = refs[2 * n:]
        x, y, c = _coords()
        cps = [_remote(g_refs[k].at[i, 1 - c], b_refs[k].at[i], send_sems, recv_sems, N_CHIPS * k + i, (x, y, 1 - c))
               for k in range(n) for i in range(N_CHIPS)]
        for cp in cps:
            cp.start()
        for cp in cps:
            cp.wait()

    return pl.pallas_call(
        body, name="rs_pair_exchange",
        out_shape=[jax.ShapeDtypeStruct((N_CHIPS,) + g.shape[2:], g.dtype) for g in Gs],
        in_specs=[HBM_SPEC] * n, out_specs=[HBM_SPEC] * n,
        scratch_shapes=[pltpu.SemaphoreType.DMA((N_CHIPS * n,)), pltpu.SemaphoreType.DMA((N_CHIPS * n,))],
    )(*Gs)


def _rs_add_pair(G, B, c, name):
    _, _, R, C = G.shape
    tr = _tile(R, 256, 16)

    def body(c_ref, g_ref, b_ref, o_ref):
        o_ref[0] = (g_ref[0, 0].astype(F32) + b_ref[0].astype(F32)).astype(o_ref.dtype)

    grid_spec = pltpu.PrefetchScalarGridSpec(
        num_scalar_prefetch=1, grid=(N_CHIPS, R // tr),
        in_specs=[pl.BlockSpec((1, 1, tr, C), lambda i, r, c_ref: (i, c_ref[0], r, 0)),
                  pl.BlockSpec((1, tr, C), lambda i, r, c_ref: (i, r, 0))],
        out_specs=pl.BlockSpec((1, tr, C), lambda i, r, c_ref: (i, r, 0)))
    return pl.pallas_call(
        body, name=name, grid_spec=grid_spec, out_shape=jax.ShapeDtypeStruct((N_CHIPS, R, C), G.dtype),
        compiler_params=_cp("parallel", "parallel"),
    )(jnp.reshape(c, (1,)).astype(jnp.int32), G, B)


def _rs_chip_exchange(Ps):
    n = len(Ps)

    def body(*refs):
        _rsx_start(refs[:n], refs[n:2 * n], *refs[2 * n:])
        _rsx_finish(refs[:n], refs[n:2 * n], *refs[2 * n:])

    sp = _rsx_specs(Ps)
    return pl.pallas_call(
        body, name="rs_chip_exchange", out_shape=sp["out_shape"], in_specs=sp["specs"], out_specs=sp["specs"],
        scratch_shapes=sp["sems"],
    )(*Ps)


def _rsx_copies(p_refs, b_refs, send_sems, recv_sems):
    x, y, c = _coords()
    me = 2 * x + y
    chips = _other_chips(x, y)

    def cp(k, j, src_slot, dst_slot):
        return _remote(p_refs[k].at[src_slot], b_refs[k].at[dst_slot], send_sems, recv_sems, 3 * k + j,
                       (chips[j][0], chips[j][1], c))

    return me, chips, cp, [(k, j) for k in range(len(p_refs)) for j in range(3)]


def _rsx_start(p_refs, b_refs, send_sems, recv_sems):
    me, chips, cp, pairs = _rsx_copies(p_refs, b_refs, send_sems, recv_sems)
    for k, j in pairs:
        cp(k, j, 2 * chips[j][0] + chips[j][1], me).start()


def _rsx_finish(p_refs, b_refs, send_sems, recv_sems):
    me, chips, cp, pairs = _rsx_copies(p_refs, b_refs, send_sems, recv_sems)
    for k, j in pairs:
        owner = 2 * chips[j][0] + chips[j][1]
        cp(k, j, owner, owner).wait_recv()
    for k, j in pairs:
        cp(k, j, 2 * chips[j][0] + chips[j][1], me).wait_send()


def _rsx_specs(Ps):
    n = len(Ps)
    return dict(out_shape=[jax.ShapeDtypeStruct(p.shape, p.dtype) for p in Ps], specs=[HBM_SPEC] * n,
                sems=[pltpu.SemaphoreType.DMA((3 * n,)), pltpu.SemaphoreType.DMA((3 * n,))])


def _rs_sum_chips(P, B, me, name):
    _, R, C = P.shape
    tr = _tile(R, 256, 16)

    def body(me_ref, p_ref, b1_ref, b2_ref, b3_ref, o_ref):
        o_ref[...] = ((p_ref[0].astype(F32) + b1_ref[0].astype(F32)) + b2_ref[0].astype(F32)) + b3_ref[0].astype(F32)

    slot = lambda d: pl.BlockSpec((1, tr, C), lambda r, me_ref: ((me_ref[0] + d) % N_CHIPS, r, 0))
    grid_spec = pltpu.PrefetchScalarGridSpec(
        num_scalar_prefetch=1, grid=(R // tr,), in_specs=[slot(0), slot(1), slot(2), slot(3)],
        out_specs=pl.BlockSpec((tr, C), lambda r, me_ref: (r, 0)))
    return pl.pallas_call(
        body, name=name, grid_spec=grid_spec, out_shape=jax.ShapeDtypeStruct((R, C), F32),
        compiler_params=_cp("parallel"),
    )(jnp.reshape(me, (1,)).astype(jnp.int32), P, B, B, B)


def _sum_slots(B, name):
    S, R, C = B.shape
    tr = _tile(R, 256, 16)

    def body(b_ref, o_ref):
        acc = b_ref[0].astype(F32)
        for i in range(1, S):
            acc = acc + b_ref[i].astype(F32)
        o_ref[...] = acc

    return pl.pallas_call(
        body, name=name, grid=(R // tr,), in_specs=[pl.BlockSpec((S, tr, C), lambda r: (0, r, 0))],
        out_specs=pl.BlockSpec((tr, C), lambda r: (r, 0)), out_shape=jax.ShapeDtypeStruct((R, C), F32),
        compiler_params=_cp("parallel"),
    )(B)


def _rs_pair_swap(Rs):
    n = len(Rs)

    def body(*refs):
        r_refs, o_refs = refs[:n], refs[n:2 * n]
        send_sems, recv_sems = refs[2 * n:]
        x, y, c = _coords()
        cps = [_remote(r_refs[k], o_refs[k], send_sems, recv_sems, k, (x, y, 1 - c)) for k in range(n)]
        for cp in cps:
            cp.start()
        for cp in cps:
            cp.wait()

    return pl.pallas_call(
        body, name="rs_pair_swap", out_shape=[jax.ShapeDtypeStruct(r.shape, r.dtype) for r in Rs],
        in_specs=[HBM_SPEC] * n, out_specs=[HBM_SPEC] * n,
        scratch_shapes=[pltpu.SemaphoreType.DMA((n,)), pltpu.SemaphoreType.DMA((n,))],
    )(*Rs)


def _ag8(v):
    R = v.shape[0]

    def body(v_ref, out_ref, send_sems, recv_sems, local_sem):
        x, y, c = _coords()
        me, sib = (x, y, c), (x, y, 1 - c)
        chips = _other_chips(x, y)

        def slot(p):
            return out_ref.at[4 * p[0] + 2 * p[1] + p[2]]

        def copy(k, block, to, src=None):
            return _remote(slot(block) if src is None else src, slot(block), send_sems, recv_sems, k, to)

        mine = pltpu.make_async_copy(v_ref, slot(me), local_sem)
        mine.start()
        first = [copy(0, me, sib, src=v_ref)]
        first += [copy(1 + j, me, (chip[0], chip[1], c), src=v_ref) for j, chip in enumerate(chips)]
        for cp in first:
            cp.start()
        passed = [copy(4 + j, (chip[0], chip[1], c), sib) for j, chip in enumerate(chips)]
        for j, chip in enumerate(chips):
            copy(1 + j, (chip[0], chip[1], c), me).wait_recv()
            passed[j].start()
        copy(0, sib, me).wait_recv()
        for j, chip in enumerate(chips):
            copy(4 + j, (chip[0], chip[1], 1 - c), me).wait_recv()
        for cp in first + passed:
            cp.wait_send()
        mine.wait()

    return pl.pallas_call(
        body, name="ag8_small", out_shape=jax.ShapeDtypeStruct((8, R, 128), v.dtype),
        in_specs=[pl.BlockSpec(memory_space=pltpu.VMEM)], out_specs=pl.BlockSpec(memory_space=pltpu.VMEM),
        scratch_shapes=[pltpu.SemaphoreType.DMA((7,)), pltpu.SemaphoreType.DMA((7,)), pltpu.SemaphoreType.DMA],
        compiler_params=pltpu.CompilerParams(vmem_limit_bytes=VMEM_LIMIT),
    )(v)


def _adamw(w, g, m, v, name):
    L, R, C = w.shape
    rows = [R] + [t for t in range(8, min(R, 1024) + 1, 8) if R % t == 0]
    cols = [C] + [t for t in range(128, C, 128) if C % t == 0]
    lead = [t for t in range(1, L + 1) if L % t == 0]
    fits = [(a * r * c, c, r, a) for a in lead for r in rows for c in cols if a * r * c * 4 <= 3 << 19]
    _, tc, tr, tl = max(fits) if fits else (0, min(cols), min(rows), 1)

    def body(w_ref, g_ref, m_ref, v_ref, d_ref, mo_ref, vo_ref):
        gv = g_ref[...]
        m2 = ADAM_B1 * m_ref[...] + (1.0 - ADAM_B1) * gv
        v2 = ADAM_B2 * v_ref[...] + (1.0 - ADAM_B2) * jnp.square(gv)
        m_hat = m2 / (1.0 - ADAM_B1 ** ADAM_STEP)
        v_hat = v2 / (1.0 - ADAM_B2 ** ADAM_STEP)
        d_ref[...] = -ADAM_LR * (m_hat / (jnp.sqrt(v_hat) + ADAM_EPS) + ADAM_WD * w_ref[...])
        mo_ref[...] = m2
        vo_ref[...] = v2

    blk = pl.BlockSpec((tl, tr, tc), lambda l, r, j: (l, r, j))
    return pl.pallas_call(
        body, name=name, grid=(L // tl, R // tr, C // tc), in_specs=[blk] * 4, out_specs=[blk] * 3,
        out_shape=[jax.ShapeDtypeStruct(w.shape, F32)] * 3,
        compiler_params=_cp("parallel", "parallel", "parallel"),
    )(w, g, m, v)


def _adamw_halves(w, g_mine, g_other, c, m, v, name):
    L, _, R, C = w.shape
    tr = _tile(R, 128, 8)

    def body(c_ref, w_ref, *rest):
        g_refs = rest[:2 * L]
        m_ref, v_ref, g_ref, d_ref, mo_ref, vo_ref = rest[2 * L:]
        l, h = pl.program_id(0), pl.program_id(1)
        gm, go = g_refs[0][...], g_refs[L][...]
        for i in range(1, L):
            gm = jnp.where(l == i, g_refs[i][...], gm)
            go = jnp.where(l == i, g_refs[L + i][...], go)
        gv = jnp.where(h == c_ref[0], gm, go)[None, None]
        g_ref[...] = gv
        m2 = ADAM_B1 * m_ref[...] + (1.0 - ADAM_B1) * gv
        v2 = ADAM_B2 * v_ref[...] + (1.0 - ADAM_B2) * jnp.square(gv)
        m_hat = m2 / (1.0 - ADAM_B1 ** ADAM_STEP)
        v_hat = v2 / (1.0 - ADAM_B2 ** ADAM_STEP)
        d_ref[...] = -ADAM_LR * (m_hat / (jnp.sqrt(v_hat) + ADAM_EPS) + ADAM_WD * w_ref[...])
        mo_ref[...] = m2
        vo_ref[...] = v2

    blk = pl.BlockSpec((1, 1, tr, C), lambda l, h, r, c_ref: (l, h, r, 0))

    def gblk(i, mine):
        def index(l, h, r, c_ref):
            use = jnp.logical_and(l == i, (h == c_ref[0]) == mine)
            return (jnp.where(use, r, 0), 0)
        return pl.BlockSpec((tr, C), index)

    grid_spec = pltpu.PrefetchScalarGridSpec(
        num_scalar_prefetch=1, grid=(L, 2, R // tr),
        in_specs=[blk] + [gblk(i, True) for i in range(L)] + [gblk(i, False) for i in range(L)] + [blk, blk],
        out_specs=[blk] * 4)
    return pl.pallas_call(
        body, name=name, grid_spec=grid_spec, out_shape=[jax.ShapeDtypeStruct(w.shape, F32)] * 4,
        compiler_params=_cp("parallel", "parallel", "parallel"),
    )(jnp.reshape(c, (1,)).astype(jnp.int32), w, *g_mine, *g_other, m, v)


BIG = ("w_in", "w_branch_a", "w_branch_b", "w_out", "ffn_w_gate", "ffn_w_up", "ffn_w_down")
ROW_SHARDED = ("w_out", "ffn_w_down")
SMALL = ("norm1_g", "dn_conv_w", "dn_a_log", "dn_dt_bias", "dn_onorm_g", "sg_ln_g", "sg_ln_b", "sg_w", "sg_b",
         "norm2_g", "ffn_conv_w", "ffn_conv_b", "final_norm_g")
SMALL_SHARDED = ("dn_conv_w", "ffn_conv_w")


def _pack_rows(arrs, mult):
    flat = jnp.concatenate([jnp.reshape(a, (-1,)) for a in arrs])
    n = flat.shape[0]
    rows = -(-n // (128 * mult)) * mult
    return jnp.reshape(jnp.pad(flat, (0, rows * 128 - n)), (rows, 128))


def _unpack(flat2d, shapes):
    flat = jnp.reshape(flat2d, (-1,))
    out, off = [], 0
    for shp in shapes:
        n = math.prod(shp)
        out.append(jnp.reshape(flat[off:off + n], shp))
        off += n
    return out


def _shards_to_full(a, row_sharded):
    if row_sharded:
        a = jnp.moveaxis(a, 0, 1)
        return jnp.reshape(a, (a.shape[0], a.shape[1] * a.shape[2], a.shape[3]))
    a = jnp.moveaxis(a, 0, 2)
    return jnp.reshape(a, (a.shape[0], a.shape[1], a.shape[2] * a.shape[3]))


def _full_to_shards(a, row_sharded):
    L, R, C = a.shape
    if row_sharded:
        return jnp.moveaxis(jnp.reshape(a, (L, N_CHIPS, R // N_CHIPS, C)), 1, 0)
    return jnp.moveaxis(jnp.reshape(a, (L, R, N_CHIPS, C // N_CHIPS)), 2, 0)


def kernel(x, norm1_g, w_in, dn_conv_w, dn_a_log, dn_dt_bias, dn_onorm_g, sg_ln_g, sg_ln_b, sg_w, sg_b, w_branch_a, w_branch_b, w_out, norm2_g, ffn_w_gate, ffn_w_up, ffn_conv_w, ffn_conv_b, ffn_w_down, final_norm_g, loss_target, m_norm1_g, m_w_in, m_dn_conv_w, m_dn_a_log, m_dn_dt_bias, m_dn_onorm_g, m_sg_ln_g, m_sg_ln_b, m_sg_w, m_sg_b, m_w_branch_a, m_w_branch_b, m_w_out, m_norm2_g, m_ffn_w_gate, m_ffn_w_up, m_ffn_conv_w, m_ffn_conv_b, m_ffn_w_down, m_final_norm_g, v_norm1_g, v_w_in, v_dn_conv_w, v_dn_a_log, v_dn_dt_bias, v_dn_onorm_g, v_sg_ln_g, v_sg_ln_b, v_sg_w, v_sg_b, v_w_branch_a, v_w_branch_b, v_w_out, v_norm2_g, v_ffn_w_gate, v_ffn_w_up, v_ffn_conv_w, v_ffn_conv_b, v_ffn_w_down, v_final_norm_g):
    W = dict(norm1_g=norm1_g, w_in=w_in, dn_conv_w=dn_conv_w, dn_a_log=dn_a_log, dn_dt_bias=dn_dt_bias,
             dn_onorm_g=dn_onorm_g, sg_ln_g=sg_ln_g, sg_ln_b=sg_ln_b, sg_w=sg_w, sg_b=sg_b, w_branch_a=w_branch_a,
             w_branch_b=w_branch_b, w_out=w_out, norm2_g=norm2_g, ffn_w_gate=ffn_w_gate, ffn_w_up=ffn_w_up,
             ffn_conv_w=ffn_conv_w, ffn_conv_b=ffn_conv_b, ffn_w_down=ffn_w_down, final_norm_g=final_norm_g)
    M = dict(norm1_g=m_norm1_g, w_in=m_w_in, dn_conv_w=m_dn_conv_w, dn_a_log=m_dn_a_log, dn_dt_bias=m_dn_dt_bias,
             dn_onorm_g=m_dn_onorm_g, sg_ln_g=m_sg_ln_g, sg_ln_b=m_sg_ln_b, sg_w=m_sg_w, sg_b=m_sg_b,
             w_branch_a=m_w_branch_a, w_branch_b=m_w_branch_b, w_out=m_w_out, norm2_g=m_norm2_g,
             ffn_w_gate=m_ffn_w_gate, ffn_w_up=m_ffn_w_up, ffn_conv_w=m_ffn_conv_w, ffn_conv_b=m_ffn_conv_b,
             ffn_w_down=m_ffn_w_down, final_norm_g=m_final_norm_g)
    V = dict(norm1_g=v_norm1_g, w_in=v_w_in, dn_conv_w=v_dn_conv_w, dn_a_log=v_dn_a_log, dn_dt_bias=v_dn_dt_bias,
             dn_onorm_g=v_dn_onorm_g, sg_ln_g=v_sg_ln_g, sg_ln_b=v_sg_ln_b, sg_w=v_sg_w, sg_b=v_sg_b,
             w_branch_a=v_w_branch_a, w_branch_b=v_w_branch_b, w_out=v_w_out, norm2_g=v_norm2_g,
             ffn_w_gate=v_ffn_w_gate, ffn_w_up=v_ffn_w_up, ffn_conv_w=v_ffn_conv_w, ffn_conv_b=v_ffn_conv_b,
             ffn_w_down=v_ffn_w_down, final_norm_g=v_final_norm_g)
    cx, cy, cc = _coords()
    chip = 2 * cx + cy
    L = w_in.shape[0]

    D = w_in.shape[1]
    cs_in = w_in.shape[2]
    rp_in = -(-cs_in // 128) * 128

    def shard_for_gather(n):
        if n == "w_in":
            return jnp.pad(jnp.swapaxes(W[n], 1, 2).astype(BF16), ((0, 0), (0, rp_in - cs_in), (0, 0)))
        return W[n].astype(BF16)

    mine = {n: shard_for_gather(n) for n in BIG}

    def halves(a, lead=0):
        return jnp.reshape(a, a.shape[:lead] + (2, a.shape[lead] // 2) + a.shape[lead + 1:])

    taps = _ag8(_pack_rows([W[n] for n in SMALL_SHARDED], 16))
    tap_shards = [_unpack(taps[2 * i], [W[n].shape for n in SMALL_SHARDED]) for i in range(N_CHIPS)]
    taps_full = {n: jnp.concatenate([tap_shards[i][k] for i in range(N_CHIPS)], axis=-1)
                 for k, n in enumerate(SMALL_SHARDED)}

    ops = []
    for l in range(L):
        p = {n: W[n][l] for n in W if n not in ("final_norm_g",) + BIG + SMALL_SHARDED}
        p.update({n: taps_full[n][l] for n in SMALL_SHARDED})
        ops.append(_prep_small(p))

    def gather_payload(items):
        return ("gather", [halves(mine[n][l]) for l, n in items])

    def weights_landed(items, gathered):
        for (l, n), a in zip(items, gathered):
            a = jnp.reshape(a, (N_CHIPS,) + mine[n].shape[1:])
            parts = [jnp.where(chip == i, mine[n][l], a[i]) for i in range(N_CHIPS)]
            if n == "w_in":
                ops[l]["w_in_t"] = _w_in_pad(jnp.concatenate([q[:cs_in] for q in parts], axis=0))
            else:
                ops[l][n] = jnp.concatenate(parts, axis=0 if n in ROW_SHARDED else 1)

    partial_sums, chip_sums = {}, {}

    def grad_partials(l, names, g):
        Gs = []
        for n in names:
            if n == "w_in":
                gt = jnp.reshape(_w_in_unpad(g["w_in_t"]), (N_CHIPS, cs_in, D))
                a = jnp.pad(gt, ((0, 0), (0, rp_in - cs_in), (0, 0)))
            elif n in ROW_SHARDED:
                a = jnp.reshape(g[n], (N_CHIPS, g[n].shape[0] // N_CHIPS, g[n].shape[1]))
            else:
                a = jnp.moveaxis(jnp.reshape(g[n], (g[n].shape[0], N_CHIPS, g[n].shape[1] // N_CHIPS)), 1, 0)
            Gs.append(halves(a, 1))
        B1s = _rs_pair_exchange(Gs)
        for n, a, b in zip(names, Gs, B1s):
            partial_sums[(l, n)] = _rs_add_pair(a, b, cc, "rs_add_pair_" + n)

    def carrier(l, plan, landed):
        def payload(kernel):
            items = plan.get((l, kernel))
            if not items:
                return None
            return gather_payload(items) if landed is weights_landed else ("exchange", [partial_sums[i] for i in items])
        return _Carrier(payload, lambda kernel, res: landed(plan[(l, kernel)], res))

    FFN = ("ffn_w_gate", "ffn_w_up", "ffn_w_down")
    REST = ("w_in", "w_branch_a", "w_branch_b", "w_out")
    fwd_plan = {(0, "proj"): [(0, "w_branch_a"), (0, "w_branch_b"), (0, "w_out"), (0, "ffn_w_gate")],
                (0, "dn_core"): [(0, "ffn_w_up"), (0, "ffn_w_down"), (1, "w_in")],
                (0, "ffn_gate"): [(1, "w_branch_a"), (1, "w_branch_b"), (1, "w_out")],
                (0, "ffn_up"): [(1, "ffn_w_gate")],
                (0, "ffn_down"): [(1, "ffn_w_up")],
                (1, "proj"): [(1, "ffn_w_down")]}
    bwd_plan = {(1, "dn_core"): [(1, n) for n in FFN],
                (0, "dn_core"): [(1, n) for n in REST],
                (0, "dh"): [(0, "ffn_w_gate"), (0, "ffn_w_up")],
                (0, "dw_in"): [(0, "ffn_w_down")]}

    def sums_landed(items, res):
        chip_sums.update(zip(items, res))

    first = [(0, "w_in")]
    weights_landed(first, _ag_layers(gather_payload(first)[1]))
    xs, saved = x[0], []
    for l in range(L):
        xs, s = _layer_fwd(xs, ops[l], carrier(l, fwd_plan, weights_landed))
        saved.append(s)
    dx, dgf, loss = _loss_head(xs, final_norm_g[None], loss_target[0])
    loss = loss[0, 0]
    grads = [None] * L
    for l in reversed(range(L)):
        dx, grads[l] = _layer_bwd(dx, ops[l], saved[l], carrier(l, bwd_plan, sums_landed),
                                  functools.partial(grad_partials, l, FFN))
        grad_partials(l, REST, grads[l])
    last = [(0, n) for n in REST]
    sums_landed(last, _rs_chip_exchange([partial_sums[i] for i in last]))
    g_mine = [[_rs_sum_chips(partial_sums[(l, n)], chip_sums[(l, n)], chip, "rs_sum_chips_" + n) for n in BIG]
              for l in range(L)]
    swapped = _rs_pair_swap(g_mine[0] + g_mine[1])
    g_other = [swapped[:len(BIG)], swapped[len(BIG):]]

    small = {n: jnp.stack([g[n] for g in grads]) for n in SMALL if n != "final_norm_g"}
    small["final_norm_g"] = dgf
    shapes = [taps_full[n].shape if n in SMALL_SHARDED else W[n].shape for n in SMALL] + [(1,)]
    sflat = _pack_rows([small[n] for n in SMALL] + [jnp.reshape(loss, (1,))], 16)
    sred = _unpack(_sum_slots(_ag8(sflat), "sum_small"), shapes)
    g_small = dict(zip(SMALL, sred[:-1]))
    loss_total = sred[-1][0]
    for n in SMALL_SHARDED:
        cs = W[n].shape[-1]
        g_small[n] = lax.dynamic_slice_in_dim(g_small[n], chip * cs, cs, axis=-1)

    g_big, delta, new_m, new_v = {}, {}, {}, {}
    for k, n in enumerate(BIG):
        gm, go = [g_mine[l][k] for l in range(L)], [g_other[l][k] for l in range(L)]
        if n == "w_in":
            rows = [jnp.where(cc == 0, jnp.concatenate([a, b]), jnp.concatenate([b, a])) for a, b in zip(gm, go)]
            g_t = jnp.stack([r[:cs_in] for r in rows], axis=1)
            outs = _adamw(*[jnp.transpose(a, (2, 0, 1)) for a in (W[n],)], g_t,
                          *[jnp.transpose(a, (2, 0, 1)) for a in (M[n], V[n])], "adamw_" + n)
            g_big[n], delta[n], new_m[n], new_v[n] = [jnp.transpose(o, (1, 2, 0)) for o in (g_t,) + tuple(outs)]
        else:
            outs = _adamw_halves(halves(W[n], 1), gm, go, cc, halves(M[n], 1), halves(V[n], 1), "adamw_" + n)
            g_big[n], delta[n], new_m[n], new_v[n] = [jnp.reshape(o, W[n].shape) for o in outs]
    s_shapes = [W[n].shape for n in SMALL]
    packed = [_pack_rows([d[n] for n in SMALL], 8) for d in (W, g_small, M, V)]
    outs = _adamw(*[a[None] for a in packed], "adamw_small")
    for d, o in zip((delta, new_m, new_v), outs):
        d.update(zip(SMALL, _unpack(o[0], s_shapes)))

    names = list(W)
    grad_w = {**g_big, **g_small}
    return (loss_total, dx[None], *[grad_w[n] for n in names], *[delta[n] for n in names],
            *[new_m[n] for n in names], *[new_v[n] for n in names])
```

```python
import functools
import math

import jax
import jax.numpy as jnp
from jax import lax
from jax.experimental import pallas as pl
from jax.experimental.pallas import tpu as pltpu

F32 = jnp.float32
BF16 = jnp.bfloat16
MESH = pl.DeviceIdType.MESH

EPS = 1e-6
H = 8
DH = 128
WD = H * DH
DNC = 64
SGC = 128
DN_K = 4
FF_K = 3
DEPTH = 2
N_CHIPS = 4

ADAM_LR = 0.001
ADAM_B1 = 0.9
ADAM_B2 = 0.999
ADAM_EPS = 1e-08
ADAM_WD = 0.01
ADAM_STEP = 10

VMEM_LIMIT = 56 * 1024 * 1024

NN = (((1,), (0,)), ((), ()))
NT = (((1,), (1,)), ((), ()))
TN = (((0,), (0,)), ((), ()))

OQ, OZ, OU, OV, OGA = 0, 3 * WD, 4 * WD, 5 * WD, 6 * WD


def _cp(*sem):
    return pltpu.CompilerParams(dimension_semantics=sem or None, vmem_limit_bytes=VMEM_LIMIT)


def _tile(dim, pref, unit=128):
    if dim <= pref:
        return dim
    t = (pref // unit) * unit
    while t >= unit:
        if dim % t == 0:
            return t
        t -= unit
    return dim


def _hdot(a, b, dn=NN):
    return lax.dot_general(a, b, dn, precision=lax.Precision.HIGHEST, preferred_element_type=F32)


def _bdot(a, b, dn=NN):
    return lax.dot_general(a.astype(BF16), b.astype(BF16), dn, preferred_element_type=F32)


def _lsum(x):
    return jnp.sum(x, axis=1, keepdims=True)


def _sig(x):
    return jax.nn.sigmoid(x)


def _dsilu(x):
    s = _sig(x)
    return s * (1.0 + x * (1.0 - s))


def _erf(x):
    a = jnp.abs(x)
    t = 1.0 / (1.0 + 0.3275911 * a)
    poly = t * (0.254829592 + t * (-0.284496736 + t * (1.421413741 + t * (-1.453152027 + t * 1.061405429))))
    r = 1.0 - poly * jnp.exp(-a * a)
    return jnp.where(x < 0, -r, r)


def _gelu(x):
    return 0.5 * x * (1.0 + _erf(x * (2.0 ** -0.5)))


def _dgelu(x):
    cdf = 0.5 * (1.0 + _erf(x * (2.0 ** -0.5)))
    pdf = jnp.exp(-0.5 * x * x) * (1.0 / math.sqrt(2.0 * math.pi))
    return cdf + x * pdf


def _shift_down(x, k):
    if k == 0:
        return x
    rows = lax.broadcasted_iota(jnp.int32, x.shape, 0)
    return jnp.where(rows >= k, pltpu.roll(x, k, 0), 0.0)


def _shift_up(x, k):
    if k == 0:
        return x
    n = x.shape[0]
    rows = lax.broadcasted_iota(jnp.int32, x.shape, 0)
    return jnp.where(rows < n - k, pltpu.roll(x, n - k, 0), 0.0)


def _comm_fns(comm):
    if not comm:
        return None, None, dict(out_shape=[], specs=[], sems=[]), ()
    kind, arrays = comm
    start, finish, specs = {"gather": (_ag_start, _ag_finish, _ag_specs),
                            "exchange": (_rsx_start, _rsx_finish, _rsx_specs)}[kind]
    return start, finish, specs(arrays), tuple(arrays)


def _mm(a, b, mode, out_dtype, add=None, name="mm", comm=None):
    if mode == "tn":
        K, M = a.shape
    else:
        M, K = a.shape
    N = b.shape[0] if mode == "nt" else b.shape[1]
    tm, tn, tk = _tile(M, 1152), _tile(N, 1536), _tile(K, 3584)
    nk = K // tk
    ni, nj = M // tm, N // tn
    dn = {"nn": NN, "nt": NT, "tn": TN}[mode]
    c_start, c_finish, c_sp, payload = _comm_fns(comm)
    nc = len(payload)
    n_add = 0 if add is None else 1

    def body(*refs):
        a_ref, b_ref = refs[:2]
        add_ref = refs[2] if n_add else None
        c_in = refs[2 + n_add:2 + n_add + nc]
        o_ref = refs[2 + n_add + nc]
        c_out = refs[3 + n_add + nc:3 + n_add + 2 * nc]
        rest = refs[3 + n_add + 2 * nc:]
        acc_ref = rest[0] if nk > 1 else None
        sems = rest[1:] if nk > 1 else rest
        i, j, k = pl.program_id(0), pl.program_id(1), pl.program_id(2)

        if nc:
            @pl.when(jnp.logical_and(jnp.logical_and(i == 0, j == 0), k == 0))
            def _():
                c_start(c_in, c_out, *sems)

        def finish(r):
            if add is not None:
                r = r + add_ref[...]
            o_ref[...] = r.astype(o_ref.dtype)

        part = lax.dot_general(a_ref[...], b_ref[...], dn, preferred_element_type=F32)
        if nk == 1:
            finish(part)
        else:
            @pl.when(k == 0)
            def _():
                acc_ref[...] = part

            @pl.when(k > 0)
            def _():
                acc_ref[...] += part

            @pl.when(k == nk - 1)
            def _():
                finish(acc_ref[...])

        if nc:
            @pl.when(jnp.logical_and(jnp.logical_and(i == ni - 1, j == nj - 1), k == nk - 1))
            def _():
                c_finish(c_in, c_out, *sems)

    a_spec = (pl.BlockSpec((tk, tm), lambda i, j, k: (k, i)) if mode == "tn"
              else pl.BlockSpec((tm, tk), lambda i, j, k: (i, k)))
    b_spec = (pl.BlockSpec((tn, tk), lambda i, j, k: (j, k)) if mode == "nt"
              else pl.BlockSpec((tk, tn), lambda i, j, k: (k, j)))
    o_spec = pl.BlockSpec((tm, tn), lambda i, j, k: (i, j))
    in_specs = [a_spec, b_spec] + ([o_spec] if add is not None else []) + c_sp["specs"]
    args = (a, b) + ((add,) if add is not None else ()) + payload
    outs = pl.pallas_call(
        body, name=name + ("_" + comm[0] if nc else ""), grid=(ni, nj, nk), in_specs=in_specs,
        out_specs=[o_spec] + c_sp["specs"],
        out_shape=[jax.ShapeDtypeStruct((M, N), out_dtype)] + c_sp["out_shape"],
        scratch_shapes=([pltpu.VMEM((tm, tn), F32)] if nk > 1 else []) + c_sp["sems"],
        compiler_params=_cp("arbitrary", "arbitrary", "arbitrary") if nc else _cp("parallel", "parallel", "arbitrary"),
    )(*args)
    return (outs[0], list(outs[1:])) if nc else outs[0]


def _rms_fwd(x, g, name):
    T, D = x.shape
    tt = _tile(T, 256, 16)

    def body(x_ref, g_ref, o_ref):
        xv = x_ref[...]
        r = lax.rsqrt(jnp.mean(xv * xv, axis=-1, keepdims=True) + EPS)
        o_ref[...] = (xv * r * g_ref[...]).astype(o_ref.dtype)

    return pl.pallas_call(
        body, name=name, grid=(T // tt,),
        in_specs=[pl.BlockSpec((tt, D), lambda i: (i, 0)), pl.BlockSpec((1, D), lambda i: (0, 0))],
        out_specs=pl.BlockSpec((tt, D), lambda i: (i, 0)),
        out_shape=jax.ShapeDtypeStruct((T, D), BF16), compiler_params=_cp("parallel"),
    )(x, g)


def _rms_bwd(x, g, dh, dres, name):
    T, D = x.shape
    tt = _tile(T, 256, 16)

    def body(x_ref, g_ref, dh_ref, dres_ref, dx_ref, dg_ref):
        @pl.when(pl.program_id(0) == 0)
        def _():
            dg_ref[...] = jnp.zeros_like(dg_ref)

        xv = x_ref[...]
        r = lax.rsqrt(jnp.mean(xv * xv, axis=-1, keepdims=True) + EPS)
        xh = xv * r
        dh_v = dh_ref[...]
        dy = dh_v * g_ref[...]
        dx_ref[...] = dres_ref[...] + r * (dy - xh * jnp.mean(dy * xh, axis=-1, keepdims=True))
        dg_ref[...] += jnp.sum(dh_v * xh, axis=0, keepdims=True)

    row = pl.BlockSpec((tt, D), lambda i: (i, 0))
    vec = pl.BlockSpec((1, D), lambda i: (0, 0))
    return pl.pallas_call(
        body, name=name, grid=(T // tt,), in_specs=[row, vec, row, row], out_specs=[row, vec],
        out_shape=[jax.ShapeDtypeStruct((T, D), F32), jax.ShapeDtypeStruct((1, D), F32)],
        compiler_params=_cp("arbitrary"),
    )(x, g, dh, dres)


def _loss_head(x, g, tgt, name="loss_head"):
    T, D = x.shape
    tt = _tile(T, 256, 16)

    def body(x_ref, g_ref, t_ref, dx_ref, dg_ref, loss_ref):
        @pl.when(pl.program_id(0) == 0)
        def _():
            dg_ref[...] = jnp.zeros_like(dg_ref)
            loss_ref[...] = jnp.zeros_like(loss_ref)

        xv = x_ref[...]
        r = lax.rsqrt(jnp.mean(xv * xv, axis=-1, keepdims=True) + EPS)
        xh = xv * r
        err = xh * g_ref[...] - t_ref[...]
        part = 0.5 * jnp.sum(jnp.mean(err * err, axis=-1, keepdims=True), axis=0, keepdims=True)
        loss_ref[...] += jnp.broadcast_to(part, loss_ref.shape)
        dy = err * (1.0 / D)
        dg_ref[...] += jnp.sum(dy * xh, axis=0, keepdims=True)
        dyh = dy * g_ref[...]
        dx_ref[...] = r * (dyh - xh * jnp.mean(dyh * xh, axis=-1, keepdims=True))

    row = pl.BlockSpec((tt, D), lambda i: (i, 0))
    vec = pl.BlockSpec((1, D), lambda i: (0, 0))
    return pl.pallas_call(
        body, name=name, grid=(T // tt,), in_specs=[row, vec, row],
        out_specs=[row, vec, pl.BlockSpec((1, 128), lambda i: (0, 0))],
        out_shape=[jax.ShapeDtypeStruct((T, D), F32), jax.ShapeDtypeStruct((1, D), F32),
                   jax.ShapeDtypeStruct((1, 128), F32)],
        compiler_params=_cp("arbitrary"),
    )(x, g, tgt)


def _ba_fwd(proj, alog, dtb, oba, name="dn_ba_fwd"):
    T = proj.shape[0]
    tt = _tile(T, 512, 8)

    def body(p_ref, al_ref, dt_ref, o_ref):
        raw = p_ref[...].astype(F32)
        lane = lax.broadcasted_iota(jnp.int32, raw.shape, 1)
        z = raw + dt_ref[...]
        sp = jnp.maximum(z, 0.0) + jnp.log(1.0 + jnp.exp(-jnp.abs(z)))
        gl = -jnp.exp(al_ref[...]) * sp
        o_ref[...] = jnp.where(lane < H, _sig(raw), jnp.where(lane < 2 * H, gl, 0.0))

    vec = pl.BlockSpec((1, 128), lambda i: (0, 0))
    return pl.pallas_call(
        body, name=name, grid=(T // tt,),
        in_specs=[pl.BlockSpec((tt, 128), lambda i: (i, oba // 128)), vec, vec],
        out_specs=pl.BlockSpec((tt, 128), lambda i: (i, 0)),
        out_shape=jax.ShapeDtypeStruct((T, 128), F32), compiler_params=_cp("parallel"),
    )(proj, alog, dtb)


def _ba_bwd(proj, alog, dtb, dbg, oba, name="dn_ba_bwd"):
    T = proj.shape[0]
    tt = _tile(T, 512, 16)

    def body(p_ref, al_ref, dt_ref, d_ref, o_ref, dal_ref, ddt_ref):
        @pl.when(pl.program_id(0) == 0)
        def _():
            dal_ref[...] = jnp.zeros_like(dal_ref)
            ddt_ref[...] = jnp.zeros_like(ddt_ref)

        raw = p_ref[...].astype(F32)
        d = d_ref[...]
        lane = lax.broadcasted_iota(jnp.int32, raw.shape, 1)
        z = raw + dt_ref[...]
        sp = jnp.maximum(z, 0.0) + jnp.log(1.0 + jnp.exp(-jnp.abs(z)))
        na = -jnp.exp(al_ref[...])
        is_g = jnp.logical_and(lane >= H, lane < 2 * H)
        b = _sig(raw)
        dz = jnp.where(is_g, d * na * _sig(z), 0.0)
        o_ref[...] = jnp.where(lane < H, d * b * (1.0 - b), dz).astype(o_ref.dtype)
        dal_ref[...] += jnp.sum(jnp.where(is_g, d * na * sp, 0.0), axis=0, keepdims=True)
        ddt_ref[...] += jnp.sum(dz, axis=0, keepdims=True)

    vec = pl.BlockSpec((1, 128), lambda i: (0, 0))
    return pl.pallas_call(
        body, name=name, grid=(T // tt,),
        in_specs=[pl.BlockSpec((tt, 128), lambda i: (i, oba // 128)), vec, vec,
                  pl.BlockSpec((tt, 128), lambda i: (i, 0))],
        out_specs=[pl.BlockSpec((tt, 128), lambda i: (i, 0)), vec, vec],
        out_shape=[jax.ShapeDtypeStruct((T, 128), BF16), jax.ShapeDtypeStruct((1, 128), F32),
                   jax.ShapeDtypeStruct((1, 128), F32)],
        compiler_params=_cp("arbitrary"),
    )(proj, alog, dtb, dbg)


def _dn_prep_fwd(proj, convw, name="dn_prep_fwd"):
    T = proj.shape[0]
    nblk = 3 * H

    def body(p_ref, w_ref, o_ref):
        j = pl.program_id(0)
        xv = p_ref[...].astype(F32)
        w = w_ref[...]
        c = xv * w[DN_K - 1:DN_K, :]
        for k in range(1, DN_K):
            c = c + _shift_down(xv, k) * w[DN_K - 1 - k:DN_K - k, :]
        s = c * _sig(c)
        r = lax.rsqrt(_lsum(s * s) + EPS)
        o_ref[...] = jnp.where(j < 2 * H, s * r, s)

    return pl.pallas_call(
        body, name=name, grid=(nblk,),
        in_specs=[pl.BlockSpec((T, DH), lambda j: (0, j)), pl.BlockSpec((DN_K, DH), lambda j: (0, j))],
        out_specs=pl.BlockSpec((T, DH), lambda j: (0, j)),
        out_shape=jax.ShapeDtypeStruct((T, 3 * WD), F32), compiler_params=_cp("parallel"),
    )(proj, convw)


def _dn_prep_bwd(proj, convw, dq, dk, dv, name="dn_prep_bwd"):
    T = proj.shape[0]
    nblk = 3 * H

    def body(p_ref, w_ref, dq_ref, dk_ref, dv_ref, dx_ref, dw_ref):
        j = pl.program_id(0)
        xv = p_ref[...].astype(F32)
        w = w_ref[...]
        shifted = [_shift_down(xv, k) for k in range(DN_K)]
        c = shifted[0] * w[DN_K - 1:DN_K, :]
        for k in range(1, DN_K):
            c = c + shifted[k] * w[DN_K - 1 - k:DN_K - k, :]
        s = c * _sig(c)
        r = lax.rsqrt(_lsum(s * s) + EPS)
        y = s * r
        dy = jnp.where(j < H, dq_ref[...], jnp.where(j < 2 * H, dk_ref[...], dv_ref[...]))
        ds = jnp.where(j < 2 * H, r * (dy - y * _lsum(dy * y)), dy)
        dc = ds * _dsilu(c)
        dx = dc * w[DN_K - 1:DN_K, :]
        for k in range(1, DN_K):
            dx = dx + _shift_up(dc, k) * w[DN_K - 1 - k:DN_K - k, :]
        dx_ref[...] = dx.astype(dx_ref.dtype)
        rows = [jnp.sum(dc * shifted[DN_K - 1 - t], axis=0, keepdims=True) for t in range(DN_K)]
        dw_ref[...] = jnp.concatenate(rows, axis=0)

    hb = lambda off: pl.BlockSpec((T, DH), lambda j: (0, jnp.maximum(jnp.minimum(j - off, H - 1), 0)))
    return pl.pallas_call(
        body, name=name, grid=(nblk,),
        in_specs=[pl.BlockSpec((T, DH), lambda j: (0, j)), pl.BlockSpec((DN_K, DH), lambda j: (0, j)),
                  hb(0), hb(H), hb(2 * H)],
        out_specs=[pl.BlockSpec((T, DH), lambda j: (0, j)), pl.BlockSpec((DN_K, DH), lambda j: (0, j))],
        out_shape=[jax.ShapeDtypeStruct((T, 3 * WD), BF16), jax.ShapeDtypeStruct((DN_K, 3 * WD), F32)],
        compiler_params=_cp("parallel"),
    )(proj, convw, dq, dk, dv)


DN_BLOCK = 4


def _split3(a):
    hi = a.astype(BF16)
    r1 = a - hi.astype(F32)
    mid = r1.astype(BF16)
    return hi, mid, (r1 - mid.astype(F32)).astype(BF16)


def _dot3(a, b, dn=NN):
    ah, al, _ = _split3(a)
    bh, bl, _ = _split3(b)
    d = lambda p, q: lax.dot_general(p, q, dn, preferred_element_type=F32)
    return d(ah, bh) + d(ah, bl) + d(al, bh)


def _mask_dot(m, b, dn=NN):
    mb = m.astype(BF16)
    d = lambda q: (lax.dot_general(mb, q, dn, preferred_element_type=F32) if dn != TN
                   else lax.dot_general(q, mb, dn, preferred_element_type=F32))
    b0, b1, b2 = _split3(b)
    return d(b0) + d(b1) + d(b2)


def _tri_inv(A):
    ri = lax.broadcasted_iota(jnp.int32, A.shape, 0)
    ci = lax.broadcasted_iota(jnp.int32, A.shape, 1)
    X = -A
    P = jnp.where(ri == ci, 1.0, 0.0) + X
    Y = X
    for _ in range(int(math.log2(DNC)) - 1):
        Y = _dot3(Y, Y)
        P = P + _dot3(P, Y)
    return P


GH = 4
NG = H // GH
GR = GH * DNC
GK = GH * DH


def _dn_masks():
    ri = lax.broadcasted_iota(jnp.int32, (GR, GR), 0)
    ci = lax.broadcasted_iota(jnp.int32, (GR, GR), 1)
    blk = (ri // DNC) == (ci // DNC)
    wide = (lax.broadcasted_iota(jnp.int32, (GR, GK), 0) // DNC) == (lax.broadcasted_iota(jnp.int32, (GR, GK), 1) // DH)
    return dict(blk=blk, causal=jnp.logical_and(blk, ri >= ci), strict=jnp.logical_and(blk, ri > ci),
                upper=jnp.logical_and(blk, ri <= ci), eye=ri == ci, wide=wide)


def _wide(a, mk):
    return jnp.where(mk["wide"], jnp.tile(a, (1, GH)), 0.0)


def _fold(a, mk):
    a = jnp.where(mk["wide"], a, 0.0)
    out = a[:, :DH]
    for j in range(1, GH):
        out = out + a[:, j * DH:(j + 1) * DH]
    return out


def _stack_heads(ref, rows, g):
    return jnp.concatenate([ref[rows, (g * GH + j) * DH:(g * GH + j + 1) * DH] for j in range(GH)], axis=0)


def _dn_group(q_ref, k_ref, v_ref, rows, bg, gc_cols, g, mk):
    heads = [g * GH + j for j in range(GH)]
    col = lambda a, lane: jnp.concatenate([a[:, lane(h):lane(h) + 1] for h in heads], axis=0)
    q = _stack_heads(q_ref, rows, g) * (DH ** -0.5)
    k = _stack_heads(k_ref, rows, g)
    v = _stack_heads(v_ref, rows, g)
    beta = col(bg, lambda h: h)
    gcol = col(gc_cols, lambda h: H + h)
    last = [gc_cols[DNC - 1:DNC, H + h:H + h + 1] for h in heads]
    gl = jnp.concatenate([jnp.broadcast_to(t, (DNC, 1)) for t in last], axis=0)
    egl_state = jnp.concatenate([jnp.broadcast_to(jnp.exp(t), (DH, 1)) for t in last], axis=0)
    grow = _mask_dot(jnp.ones((GR, GR), F32), jnp.where(mk["eye"], gcol, 0.0))
    dec = jnp.where(mk["causal"], jnp.exp(jnp.where(mk["causal"], gcol - grow, 0.0)), 0.0)
    eg = jnp.exp(gcol)
    ek = jnp.exp(gl - gcol)
    kb = k * beta
    vb = v * beta
    kbe = kb * eg
    A = jnp.where(mk["strict"], _bdot(kb, k, NT) * dec, 0.0)
    P = jnp.where(mk["causal"], _bdot(q, k, NT) * dec, 0.0)
    return dict(q=q, k=k, v=v, beta=beta, dec=dec, eg=eg, ek=ek, egl=jnp.exp(gl), egl_state=egl_state, kb=kb, vb=vb,
                kbe=kbe, A=A, P=P, qd=q * eg, kd=k * ek, heads=heads)


def _gc_cols(bg):
    ri = lax.broadcasted_iota(jnp.int32, (DNC, DNC), 0)
    ci = lax.broadcasted_iota(jnp.int32, (DNC, DNC), 1)
    return _mask_dot(jnp.where(ri >= ci, 1.0, 0.0), bg)


def _dn_core_fwd(qkv, bg, comm=None, name="dn_core_fwd"):
    c_start, c_finish, sp, gather = _comm_fns(comm)
    T = qkv.shape[0]
    n_chunks = T // DNC
    nb = _tile(n_chunks, DN_BLOCK, 1)
    tb = nb * DNC

    ng = len(gather)
    n_steps = n_chunks // nb

    def body(*refs):
        q_ref, k_ref, v_ref, bg_ref = refs[:4]
        o_ref, s_ref, tm_ref = refs[4 + ng:7 + ng]
        S_scr = refs[7 + 2 * ng]
        comm_refs = (refs[4:4 + ng], refs[7 + ng:7 + 2 * ng]) + tuple(refs[8 + 2 * ng:])

        @pl.when(pl.program_id(0) == 0)
        def _():
            S_scr[...] = jnp.zeros_like(S_scr)
            if ng:
                c_start(*comm_refs)

        def chunk(n, carry):
            rows = pl.ds(pl.multiple_of(n * DNC, DNC), DNC)
            mk = _dn_masks()
            bgc = bg_ref[rows, :]
            gc_cols = _gc_cols(bgc)
            for g in range(NG):
                c = _dn_group(q_ref, k_ref, v_ref, rows, bgc, gc_cols, g, mk)
                Tm = _tri_inv(c["A"])
                tm_ref[n, g] = Tm
                S = S_scr[g]
                s_ref[n, g] = S
                u = _bdot(Tm, c["vb"])
                w = _bdot(Tm, c["kbe"])
                vn = u - _bdot(_wide(w, mk), S)
                o = _bdot(_wide(c["qd"], mk), S) + _bdot(c["P"], vn)
                for j, h in enumerate(c["heads"]):
                    o_ref[rows, h * DH:(h + 1) * DH] = o[j * DNC:(j + 1) * DNC]
                S_scr[g] = S * c["egl_state"] + _bdot(_wide(c["kd"], mk), vn, TN)
            return carry

        lax.fori_loop(0, nb, chunk, 0)

        if ng:
            @pl.when(pl.program_id(0) == n_steps - 1)
            def _():
                c_finish(*comm_refs)

    blk = lambda j: pl.BlockSpec((tb, WD), lambda i: (i, j))
    outs = pl.pallas_call(
        body, name=name + ("_" + comm[0] if ng else ""), grid=(n_steps,),
        in_specs=[blk(0), blk(1), blk(2), pl.BlockSpec((tb, 128), lambda i: (i, 0))] + sp["specs"],
        out_specs=[blk(0), pl.BlockSpec((nb, NG, GK, DH), lambda i: (i, 0, 0, 0)),
                   pl.BlockSpec((nb, NG, GR, GR), lambda i: (i, 0, 0, 0))] + sp["specs"],
        out_shape=[jax.ShapeDtypeStruct((T, WD), F32), jax.ShapeDtypeStruct((n_chunks, NG, GK, DH), F32),
                   jax.ShapeDtypeStruct((n_chunks, NG, GR, GR), F32)] + sp["out_shape"],
        scratch_shapes=[pltpu.VMEM((NG, GK, DH), F32)] + (sp["sems"] if ng else []),
        compiler_params=_cp("arbitrary"),
    )(qkv, qkv, qkv, bg, *gather)
    return outs[0], outs[1], outs[2], list(outs[3:])


def _dn_core_bwd(qkv, bg, s_all, tm_all, do, comm=None, name="dn_core_bwd"):
    c_start, c_finish, sp, exchange = _comm_fns(comm)
    T = qkv.shape[0]
    n_chunks = T // DNC
    nb = _tile(n_chunks, DN_BLOCK, 1)
    tb = nb * DNC
    n_blocks = n_chunks // nb

    nx = len(exchange)

    def body(*refs):
        q_ref, k_ref, v_ref, bg_ref, s_ref, tm_ref, do_ref = refs[:7]
        dq_ref, dk_ref, dv_ref, dbg_ref = refs[7 + nx:11 + nx]
        dS_scr = refs[11 + 2 * nx]
        comm_refs = (refs[7:7 + nx], refs[11 + nx:11 + 2 * nx]) + tuple(refs[12 + 2 * nx:])

        @pl.when(pl.program_id(0) == 0)
        def _():
            dS_scr[...] = jnp.zeros_like(dS_scr)
            if nx:
                c_start(*comm_refs)

        lane = lax.broadcasted_iota(jnp.int32, (DNC, 128), 1)
        row = lax.broadcasted_iota(jnp.int32, (GR, 1), 0)

        def chunk(i, carry):
            n = nb - 1 - i
            rows = pl.ds(pl.multiple_of(n * DNC, DNC), DNC)
            mk = _dn_masks()
            ones = jnp.ones((GR, GR), F32)
            blk_f = jnp.where(mk["blk"], 1.0, 0.0)
            wide_f = jnp.where(mk["wide"], 1.0, 0.0)
            per_row = lambda m, a: _mask_dot(m, jnp.broadcast_to(a, (a.shape[0], DH)))[:, :1]
            bgc = bg_ref[rows, :]
            gc_cols = _gc_cols(bgc)
            dbg = jnp.zeros((DNC, 128), F32)
            for g in range(NG):
                c = _dn_group(q_ref, k_ref, v_ref, rows, bgc, gc_cols, g, mk)
                q, k, v, beta = c["q"], c["k"], c["v"], c["beta"]
                dec, eg, ek, egl = c["dec"], c["eg"], c["ek"], c["egl"]
                kb, vb, kbe, A, P, qd, kd = c["kb"], c["vb"], c["kbe"], c["A"], c["P"], c["qd"], c["kd"]
                S = s_ref[n, g]
                Tm = tm_ref[n, g]
                u = _bdot(Tm, vb)
                w = _bdot(Tm, kbe)
                w_wide = _wide(w, mk)
                vn = u - _bdot(w_wide, S)
                d_o = _stack_heads(do_ref, rows, g)
                dS1 = dS_scr[g]
                d_qd = _fold(_bdot(d_o, S, NT), mk)
                dP = jnp.where(mk["causal"], _bdot(d_o, vn, NT), 0.0)
                d_vn = _bdot(P, d_o, TN) + _bdot(_wide(kd, mk), dS1)
                d_kd = _fold(_bdot(vn, dS1, NT), mk)
                d_egl = per_row(wide_f, _lsum(dS1 * S))
                dS_scr[g] = dS1 * c["egl_state"] + _bdot(_wide(qd, mk), d_o, TN) - _bdot(w_wide, d_vn, TN)
                d_w = -_fold(_bdot(d_vn, S, NT), mk)
                d_vb = _bdot(Tm, d_vn, TN)
                d_kbe = _bdot(Tm, d_w, TN)
                dA = jnp.where(mk["strict"], -(_bdot(d_vb, u, NT) + _bdot(d_kbe, w, NT)), 0.0)
                dMA = dA * dec
                dMP = dP * dec
                d_kb = _bdot(dMA, k) + d_kbe * eg
                d_k = _bdot(dMA, kb, TN) + _bdot(dMP, q, TN) + d_kd * ek + d_kb * beta
                d_qs = (_bdot(dMP, k) + d_qd * eg) * (DH ** -0.5)
                d_v = d_vb * beta
                E = dA * A + dP * P
                col_sums = _mask_dot(ones, E, TN)[:, :1]
                t_kd = _lsum(d_kd * kd)
                d_gl = per_row(blk_f, t_kd) + d_egl * egl
                d_gc = (_lsum(E) - col_sums + _lsum(d_qd * qd) + _lsum(d_kbe * kbe) - t_kd
                        + jnp.where(row % DNC == DNC - 1, d_gl, 0.0))
                d_g = per_row(jnp.where(mk["upper"], 1.0, 0.0), d_gc)
                d_beta = _lsum(d_kb * k) + _lsum(d_vb * v)
                for j, h in enumerate(c["heads"]):
                    rs = slice(j * DNC, (j + 1) * DNC)
                    dq_ref[rows, h * DH:(h + 1) * DH] = d_qs[rs]
                    dk_ref[rows, h * DH:(h + 1) * DH] = d_k[rs]
                    dv_ref[rows, h * DH:(h + 1) * DH] = d_v[rs]
                    dbg = dbg + jnp.where(lane == h, d_beta[rs], 0.0) + jnp.where(lane == h + H, d_g[rs], 0.0)
            dbg_ref[rows, :] = dbg
            return carry

        lax.fori_loop(0, nb, chunk, 0)

        if nx:
            @pl.when(pl.program_id(0) == n_blocks - 1)
            def _():
                c_finish(*comm_refs)

    blk = lambda j: pl.BlockSpec((tb, WD), lambda i: (n_blocks - 1 - i, j))
    small = pl.BlockSpec((tb, 128), lambda i: (n_blocks - 1 - i, 0))
    outs = pl.pallas_call(
        body, name=name + ("_" + comm[0] if nx else ""), grid=(n_blocks,),
        in_specs=[blk(0), blk(1), blk(2), small,
                  pl.BlockSpec((nb, NG, GK, DH), lambda i: (n_blocks - 1 - i, 0, 0, 0)),
                  pl.BlockSpec((nb, NG, GR, GR), lambda i: (n_blocks - 1 - i, 0, 0, 0)), blk(0)] + sp["specs"],
        out_specs=[blk(0), blk(0), blk(0), small] + sp["specs"],
        out_shape=[jax.ShapeDtypeStruct((T, WD), F32)] * 3 + [jax.ShapeDtypeStruct((T, 128), F32)] + sp["out_shape"],
        scratch_shapes=[pltpu.VMEM((NG, GK, DH), F32)] + (sp["sems"] if nx else []),
        compiler_params=_cp("arbitrary"),
    )(qkv, qkv, qkv, bg, s_all, tm_all, do, *exchange)
    return outs[0], outs[1], outs[2], outs[3], list(outs[4:])


def _dn_post_fwd(o, proj, gon, name="dn_post_fwd"):
    T = o.shape[0]
    tt = _tile(T, 256, 16)

    def body(o_ref, z_ref, g_ref, y_ref):
        for hh in range(H):
            sl = slice(hh * DH, (hh + 1) * DH)
            ov = o_ref[:, sl]
            zv = z_ref[:, sl].astype(F32)
            r = lax.rsqrt(jnp.mean(ov * ov, axis=-1, keepdims=True) + EPS)
            y_ref[:, sl] = (ov * r * g_ref[...] * (zv * _sig(zv))).astype(y_ref.dtype)

    return pl.pallas_call(
        body, name=name, grid=(T // tt,),
        in_specs=[pl.BlockSpec((tt, WD), lambda i: (i, 0)), pl.BlockSpec((tt, WD), lambda i: (i, OZ // WD)),
                  pl.BlockSpec((1, DH), lambda i: (0, 0))],
        out_specs=pl.BlockSpec((tt, WD), lambda i: (i, 0)),
        out_shape=jax.ShapeDtypeStruct((T, WD), BF16), compiler_params=_cp("parallel"),
    )(o, proj, gon)


def _dn_post_bwd(o, proj, gon, dy, name="dn_post_bwd"):
    T = o.shape[0]
    tt = _tile(T, 256, 16)

    def body(o_ref, z_ref, g_ref, dy_ref, do_ref, dz_ref, dg_ref):
        @pl.when(pl.program_id(0) == 0)
        def _():
            dg_ref[...] = jnp.zeros_like(dg_ref)

        acc = jnp.zeros((1, DH), F32)
        for hh in range(H):
            sl = slice(hh * DH, (hh + 1) * DH)
            ov = o_ref[:, sl]
            zv = z_ref[:, sl].astype(F32)
            dyv = dy_ref[:, sl]
            r = lax.rsqrt(jnp.mean(ov * ov, axis=-1, keepdims=True) + EPS)
            oh = ov * r
            nrm = oh * g_ref[...]
            dn = dyv * (zv * _sig(zv))
            dz_ref[:, sl] = (dyv * nrm * _dsilu(zv)).astype(dz_ref.dtype)
            doh = dn * g_ref[...]
            do_ref[:, sl] = r * (doh - oh * jnp.mean(doh * oh, axis=-1, keepdims=True))
            acc = acc + jnp.sum(dn * oh, axis=0, keepdims=True)
        dg_ref[...] += acc

    row = pl.BlockSpec((tt, WD), lambda i: (i, 0))
    vec = pl.BlockSpec((1, DH), lambda i: (0, 0))
    return pl.pallas_call(
        body, name=name, grid=(T // tt,),
        in_specs=[row, pl.BlockSpec((tt, WD), lambda i: (i, OZ // WD)), vec, row],
        out_specs=[row, row, vec],
        out_shape=[jax.ShapeDtypeStruct((T, WD), F32), jax.ShapeDtypeStruct((T, WD), BF16),
                   jax.ShapeDtypeStruct((1, DH), F32)],
        compiler_params=_cp("arbitrary"),
    )(o, proj, gon, dy)


def _sg_common(u_ref, v_ref, lng_ref, lnb_ref):
    ur = u_ref[...].astype(F32)
    vr = v_ref[...].astype(F32)
    vgel = _gelu(vr)
    mu = jnp.mean(vgel, axis=-1, keepdims=True)
    xc = vgel - mu
    rs = lax.rsqrt(jnp.mean(xc * xc, axis=-1, keepdims=True) + EPS)
    xh = xc * rs
    vg = xh * lng_ref[...] + lnb_ref[...]
    return ur, vr, rs, xh, vg


def _sg_fwd(proj, lng, lnb, sgw, sgbt, name="sg_fwd"):
    T = proj.shape[0]

    def body(u_ref, v_ref, lng_ref, lnb_ref, w_ref, bt_ref, y_ref):
        ur, _, _, _, vg = _sg_common(u_ref, v_ref, lng_ref, lnb_ref)
        ri = lax.broadcasted_iota(jnp.int32, (SGC, SGC), 0)
        ci = lax.broadcasted_iota(jnp.int32, (SGC, SGC), 1)
        ug = _gelu(ur)
        for g in range(H):
            sl = slice(g * DH, (g + 1) * DH)
            ws = jnp.where(ri >= ci, w_ref[g], 0.0)
            mixed = _bdot(ws, vg[:, sl]) + bt_ref[:, g:g + 1]
            y_ref[:, sl] = (ug[:, sl] * mixed).astype(y_ref.dtype)

    vec = pl.BlockSpec((1, WD), lambda i: (0, 0))
    return pl.pallas_call(
        body, name=name, grid=(T // SGC,),
        in_specs=[pl.BlockSpec((SGC, WD), lambda i: (i, OU // WD)), pl.BlockSpec((SGC, WD), lambda i: (i, OV // WD)),
                  vec, vec, pl.BlockSpec((H, SGC, SGC), lambda i: (0, 0, 0)),
                  pl.BlockSpec((SGC, H), lambda i: (0, 0))],
        out_specs=pl.BlockSpec((SGC, WD), lambda i: (i, 0)),
        out_shape=jax.ShapeDtypeStruct((T, WD), BF16), compiler_params=_cp("parallel"),
    )(proj, proj, lng, lnb, sgw, sgbt)


def _sg_bwd(proj, lng, lnb, sgw, sgbt, dy, name="sg_bwd"):
    T = proj.shape[0]

    def body(u_ref, v_ref, lng_ref, lnb_ref, w_ref, bt_ref, dy_ref,
             du_ref, dv_ref, dw_ref, dbt_ref, dlng_ref, dlnb_ref):
        @pl.when(pl.program_id(0) == 0)
        def _():
            dw_ref[...] = jnp.zeros_like(dw_ref)
            dbt_ref[...] = jnp.zeros_like(dbt_ref)
            dlng_ref[...] = jnp.zeros_like(dlng_ref)
            dlnb_ref[...] = jnp.zeros_like(dlnb_ref)

        ur, vr, rs, xh, vg = _sg_common(u_ref, v_ref, lng_ref, lnb_ref)
        ri = lax.broadcasted_iota(jnp.int32, (SGC, SGC), 0)
        ci = lax.broadcasted_iota(jnp.int32, (SGC, SGC), 1)
        ug = _gelu(ur)
        dyv = dy_ref[...]
        dbt = jnp.zeros((SGC, 128), F32)
        dvg_parts = []
        for g in range(H):
            sl = slice(g * DH, (g + 1) * DH)
            ws = jnp.where(ri >= ci, w_ref[g], 0.0)
            mixed = _bdot(ws, vg[:, sl]) + bt_ref[:, g:g + 1]
            dyg = dyv[:, sl]
            du_ref[:, sl] = (dyg * mixed * _dgelu(ur[:, sl])).astype(du_ref.dtype)
            dmix = dyg * ug[:, sl]
            dw_ref[g] += jnp.where(ri >= ci, _bdot(dmix, vg[:, sl], NT), 0.0)
            dbt = dbt + jnp.where(ci == g, _lsum(dmix), 0.0)
            dvg_parts.append(_bdot(ws, dmix, TN))
        dbt_ref[...] += dbt
        dvg = jnp.concatenate(dvg_parts, axis=1)
        dlng_ref[...] += jnp.sum(dvg * xh, axis=0, keepdims=True)
        dlnb_ref[...] += jnp.sum(dvg, axis=0, keepdims=True)
        dxh = dvg * lng_ref[...]
        dvgel = rs * (dxh - jnp.mean(dxh, axis=-1, keepdims=True) - xh * jnp.mean(dxh * xh, axis=-1, keepdims=True))
        dv_ref[...] = (dvgel * _dgelu(vr)).astype(dv_ref.dtype)

    vec = pl.BlockSpec((1, WD), lambda i: (0, 0))
    row = pl.BlockSpec((SGC, WD), lambda i: (i, 0))
    return pl.pallas_call(
        body, name=name, grid=(T // SGC,),
        in_specs=[pl.BlockSpec((SGC, WD), lambda i: (i, OU // WD)), pl.BlockSpec((SGC, WD), lambda i: (i, OV // WD)),
                  vec, vec, pl.BlockSpec((H, SGC, SGC), lambda i: (0, 0, 0)),
                  pl.BlockSpec((SGC, H), lambda i: (0, 0)), row],
        out_specs=[row, row, pl.BlockSpec((H, SGC, SGC), lambda i: (0, 0, 0)),
                   pl.BlockSpec((SGC, 128), lambda i: (0, 0)), vec, vec],
        out_shape=[jax.ShapeDtypeStruct((T, WD), BF16), jax.ShapeDtypeStruct((T, WD), BF16),
                   jax.ShapeDtypeStruct((H, SGC, SGC), F32), jax.ShapeDtypeStruct((SGC, 128), F32),
                   jax.ShapeDtypeStruct((1, WD), F32), jax.ShapeDtypeStruct((1, WD), F32)],
        compiler_params=_cp("arbitrary"),
    )(proj, proj, lng, lnb, sgw, sgbt, dy)


def _merge_fwd(proj, yap, ybp, D, name="merge_fwd"):
    T = proj.shape[0]
    tt = _tile(T, 256, 16)

    def body(ga_ref, gb_ref, a_ref, b_ref, o_ref):
        ga, gb, a, b = [r[...].astype(F32) for r in (ga_ref, gb_ref, a_ref, b_ref)]
        o_ref[...] = (_sig(ga) * a + _sig(gb) * b).astype(o_ref.dtype)

    row = pl.BlockSpec((tt, D), lambda i: (i, 0))
    return pl.pallas_call(
        body, name=name, grid=(T // tt,),
        in_specs=[pl.BlockSpec((tt, D), lambda i: (i, OGA // D)), pl.BlockSpec((tt, D), lambda i: (i, OGA // D + 1)),
                  row, row],
        out_specs=row, out_shape=jax.ShapeDtypeStruct((T, D), BF16), compiler_params=_cp("parallel"),
    )(proj, proj, yap, ybp)


def _merge_bwd(proj, yap, ybp, dm, D, name="merge_bwd"):
    T = proj.shape[0]
    tt = _tile(T, 256, 16)

    def body(ga_ref, gb_ref, a_ref, b_ref, dm_ref, da_ref, db_ref, dga_ref, dgb_ref):
        d, ga, gb, a, b = [r[...].astype(F32) for r in (dm_ref, ga_ref, gb_ref, a_ref, b_ref)]
        sa = _sig(ga)
        sb = _sig(gb)
        da_ref[...] = (d * sa).astype(da_ref.dtype)
        db_ref[...] = (d * sb).astype(db_ref.dtype)
        dga_ref[...] = (d * a * sa * (1.0 - sa)).astype(dga_ref.dtype)
        dgb_ref[...] = (d * b * sb * (1.0 - sb)).astype(dgb_ref.dtype)

    row = pl.BlockSpec((tt, D), lambda i: (i, 0))
    return pl.pallas_call(
        body, name=name, grid=(T // tt,),
        in_specs=[pl.BlockSpec((tt, D), lambda i: (i, OGA // D)), pl.BlockSpec((tt, D), lambda i: (i, OGA // D + 1)),
                  row, row, row],
        out_specs=[row] * 4, out_shape=[jax.ShapeDtypeStruct((T, D), BF16)] * 4,
        compiler_params=_cp("parallel"),
    )(proj, proj, yap, ybp, dm)


def _ffn_act_fwd(gp, up, cw, cb, name="ffn_act_fwd"):
    T, F = gp.shape

    def body(g_ref, u_ref, w_ref, b_ref, o_ref):
        gv = g_ref[...].astype(F32)
        w = w_ref[...]
        c = gv * w[FF_K - 1:FF_K, :] + b_ref[...]
        for k in range(1, FF_K):
            c = c + _shift_down(gv, k) * w[FF_K - 1 - k:FF_K - k, :]
        o_ref[...] = (c * _sig(c) * u_ref[...].astype(F32)).astype(o_ref.dtype)

    col = pl.BlockSpec((T, 128), lambda j: (0, j))
    return pl.pallas_call(
        body, name=name, grid=(F // 128,),
        in_specs=[col, col, pl.BlockSpec((FF_K, 128), lambda j: (0, j)), pl.BlockSpec((1, 128), lambda j: (0, j))],
        out_specs=col, out_shape=jax.ShapeDtypeStruct((T, F), BF16), compiler_params=_cp("parallel"),
    )(gp, up, cw, cb)


def _ffn_act_bwd(gp, up, cw, cb, dact, name="ffn_act_bwd"):
    T, F = gp.shape

    def body(g_ref, u_ref, w_ref, b_ref, d_ref, dg_ref, du_ref, dw_ref, db_ref):
        gv = g_ref[...].astype(F32)
        w = w_ref[...]
        shifted = [_shift_down(gv, k) for k in range(FF_K)]
        c = shifted[0] * w[FF_K - 1:FF_K, :] + b_ref[...]
        for k in range(1, FF_K):
            c = c + shifted[k] * w[FF_K - 1 - k:FF_K - k, :]
        d = d_ref[...].astype(F32)
        du_ref[...] = (d * c * _sig(c)).astype(du_ref.dtype)
        dc = d * u_ref[...].astype(F32) * _dsilu(c)
        dg = dc * w[FF_K - 1:FF_K, :]
        for k in range(1, FF_K):
            dg = dg + _shift_up(dc, k) * w[FF_K - 1 - k:FF_K - k, :]
        dg_ref[...] = dg.astype(dg_ref.dtype)
        rows = [jnp.sum(dc * shifted[FF_K - 1 - t], axis=0, keepdims=True) for t in range(FF_K)]
        dw_ref[...] = jnp.concatenate(rows, axis=0)
        db_ref[...] = jnp.sum(dc, axis=0, keepdims=True)

    col = pl.BlockSpec((T, 128), lambda j: (0, j))
    wspec = pl.BlockSpec((FF_K, 128), lambda j: (0, j))
    bspec = pl.BlockSpec((1, 128), lambda j: (0, j))
    return pl.pallas_call(
        body, name=name, grid=(F // 128,),
        in_specs=[col, col, wspec, bspec, col], out_specs=[col, col, wspec, bspec],
        out_shape=[jax.ShapeDtypeStruct((T, F), BF16), jax.ShapeDtypeStruct((T, F), BF16),
                   jax.ShapeDtypeStruct((FF_K, F), F32), jax.ShapeDtypeStruct((1, F), F32)],
        compiler_params=_cp("parallel"),
    )(gp, up, cw, cb, dact)


class _Carrier:
    def __init__(self, plan=None, deliver=None):
        self.plan, self.deliver = plan or (lambda kernel: None), deliver

    def run(self, kernel, fn, **kw):
        comm = self.plan(kernel)
        out = fn(comm=comm, **kw)
        if comm:
            self.deliver(kernel, out[-1])
            out = out[:-1]
            return out[0] if len(out) == 1 else out
        return out


def _layer_fwd(x, w, carrier=None):
    cr = carrier or _Carrier()
    D = x.shape[1]
    oba = OGA + 2 * D
    h = _rms_fwd(x, w["norm1_g"], "rms1_fwd")
    proj = cr.run("proj", functools.partial(_mm, h, w["w_in_t"], "nt", BF16, name="mm_proj"))
    bg = _ba_fwd(proj, w["alog_row"], w["dtb_row"], oba)
    qkv = _dn_prep_fwd(proj, w["dn_conv_w"])
    r = cr.run("dn_core", functools.partial(_dn_core_fwd, qkv, bg))
    o, s_all, tm_all = r[0], r[1], r[2]
    ya = _dn_post_fwd(o, proj, w["dn_onorm_g"])
    yb = _sg_fwd(proj, w["sg_ln_g"], w["sg_ln_b"], w["sg_w"], w["sg_bt"])
    yap = _mm(ya, w["w_branch_a"], "nn", BF16, name="mm_branch")
    ybp = _mm(yb, w["w_branch_b"], "nn", BF16, name="mm_branch")
    merged = _merge_fwd(proj, yap, ybp, D)
    x1 = _mm(merged, w["w_out"], "nn", F32, add=x, name="mm_out")
    h2 = _rms_fwd(x1, w["norm2_g"], "rms2_fwd")
    gp = cr.run("ffn_gate", functools.partial(_mm, h2, w["ffn_w_gate"], "nn", BF16, name="mm_ffn_in"))
    up = cr.run("ffn_up", functools.partial(_mm, h2, w["ffn_w_up"], "nn", BF16, name="mm_ffn_in"))
    act = _ffn_act_fwd(gp, up, w["ffn_conv_w"], w["ffn_conv_b"])
    x2 = cr.run("ffn_down", functools.partial(_mm, act, w["ffn_w_down"], "nn", F32, add=x1, name="mm_ffn_down"))
    saved = dict(x=x, h=h, proj=proj, bg=bg, qkv=qkv, o=o, s_all=s_all, tm_all=tm_all, ya=ya, yb=yb, yap=yap,
                 ybp=ybp, merged=merged, x1=x1, h2=h2, gp=gp, up=up, act=act)
    return x2, saved


def _layer_bwd(dx2, w, s, carrier=None, ffn_grads_ready=None, rest_grads_ready=None):
    cr = carrier or _Carrier()
    D = dx2.shape[1]
    oba = OGA + 2 * D
    g = {}
    dx2b = dx2.astype(BF16)
    dact = cr.run("d_act", functools.partial(_mm, dx2b, w["ffn_w_down"], "nt", BF16, name="mm_d_act"))
    g["ffn_w_down"] = _mm(s["act"], dx2b, "tn", BF16, name="mm_dw_down")
    dgp, dup, g["ffn_conv_w"], g["ffn_conv_b"] = _ffn_act_bwd(s["gp"], s["up"], w["ffn_conv_w"], w["ffn_conv_b"], dact)
    dh2 = _mm(dgp, w["ffn_w_gate"], "nt", F32, name="mm_dh2")
    dh2 = _mm(dup, w["ffn_w_up"], "nt", F32, add=dh2, name="mm_dh2_acc")
    g["ffn_w_gate"] = _mm(s["h2"], dgp, "tn", BF16, name="mm_dw_ffn_in")
    g["ffn_w_up"] = _mm(s["h2"], dup, "tn", BF16, name="mm_dw_ffn_in")
    if ffn_grads_ready:
        ffn_grads_ready(g)
    dx1, g["norm2_g"] = _rms_bwd(s["x1"], w["norm2_g"], dh2, dx2, "rms2_bwd")
    dx1b = dx1.astype(BF16)
    dm = _mm(dx1b, w["w_out"], "nt", BF16, name="mm_d_merged")
    g["w_out"] = _mm(s["merged"], dx1b, "tn", BF16, name="mm_dw_out")
    dyap, dybp, dga, dgb = _merge_bwd(s["proj"], s["yap"], s["ybp"], dm, D)
    dya = _mm(dyap, w["w_branch_a"], "nt", F32, name="mm_d_branch")
    dyb = _mm(dybp, w["w_branch_b"], "nt", F32, name="mm_d_branch")
    g["w_branch_a"] = _mm(s["ya"], dyap, "tn", BF16, name="mm_dw_branch")
    g["w_branch_b"] = _mm(s["yb"], dybp, "tn", BF16, name="mm_dw_branch")
    du, dv, g["sg_w"], dbt, g["sg_ln_g"], g["sg_ln_b"] = _sg_bwd(
        s["proj"], w["sg_ln_g"], w["sg_ln_b"], w["sg_w"], w["sg_bt"], dyb)
    g["sg_b"] = jnp.transpose(dbt[:, :H])
    do, dz, g["dn_onorm_g"] = _dn_post_bwd(s["o"], s["proj"], w["dn_onorm_g"], dya)
    r = cr.run("dn_core", functools.partial(_dn_core_bwd, s["qkv"], s["bg"], s["s_all"], s["tm_all"], do))
    dq, dk, dvv, dbg = r[0], r[1], r[2], r[3]
    dqkv, g["dn_conv_w"] = _dn_prep_bwd(s["proj"], w["dn_conv_w"], dq, dk, dvv)
    dba, dal, ddt = _ba_bwd(s["proj"], w["alog_row"], w["dtb_row"], dbg, oba)
    g["dn_a_log"] = dal[0, H:2 * H]
    g["dn_dt_bias"] = ddt[0, H:2 * H]
    dproj = jnp.concatenate([dqkv, dz, du, dv, dga, dgb, dba], axis=1)
    g["w_in_t"] = cr.run("dw_in", functools.partial(_mm, dproj, s["h"], "tn", BF16, name="mm_dw_in"))
    if rest_grads_ready:
        rest_grads_ready(g)
    dh = cr.run("dh", functools.partial(_mm, dproj, w["w_in_t"], "nn", F32, name="mm_dh"))
    dx, g["norm1_g"] = _rms_bwd(s["x"], w["norm1_g"], dh, dx1, "rms1_bwd")
    return dx, g


def _local_step(x, tgt, layers, final_g):
    saved = []
    for w in layers:
        x, s = _layer_fwd(x, w)
        saved.append(s)
    dx, dgf, loss = _loss_head(x, final_g, tgt)
    grads = [None] * len(layers)
    for l in reversed(range(len(layers))):
        dx, grads[l] = _layer_bwd(dx, layers[l], saved[l])
    return loss[0, 0], dx, grads, dgf


def _w_in_pad(wt):
    c1 = 4 * WD
    return jnp.concatenate([wt[:c1], wt[c1 + 2 * H:], wt[c1:c1 + 2 * H],
                            jnp.zeros((128 - 2 * H, wt.shape[1]), wt.dtype)], axis=0)


def _w_in_unpad(gt):
    c1 = 4 * WD
    n = gt.shape[0] - 128
    return jnp.concatenate([gt[:c1], gt[n:n + 2 * H], gt[c1:n]], axis=0)


def _row128(v, off):
    return jnp.pad(v, (off, 128 - off - v.shape[0]))[None]


def _prep_small(p):
    return dict(
        norm1_g=p["norm1_g"][None], alog_row=_row128(p["dn_a_log"], H), dtb_row=_row128(p["dn_dt_bias"], H),
        dn_conv_w=p["dn_conv_w"], dn_onorm_g=p["dn_onorm_g"][None],
        sg_ln_g=p["sg_ln_g"][None], sg_ln_b=p["sg_ln_b"][None], sg_w=p["sg_w"], sg_bt=jnp.transpose(p["sg_b"]),
        norm2_g=p["norm2_g"][None], ffn_conv_w=p["ffn_conv_w"], ffn_conv_b=p["ffn_conv_b"][None])


def _prep_layer(p):
    return dict(_prep_small(p), w_in_t=_w_in_pad(p["w_in_t"]),
                **{n: p[n] for n in ("w_branch_a", "w_branch_b", "w_out", "ffn_w_gate", "ffn_w_up", "ffn_w_down")})


HBM_SPEC = pl.BlockSpec(memory_space=pltpu.HBM)


def _coords():
    return lax.axis_index("x"), lax.axis_index("y"), lax.axis_index("c")


def _other_chips(x, y):
    return [(1 - x, y), (x, 1 - y), (1 - x, 1 - y)]


def _remote(src, dst, send_sems, recv_sems, k, dev):
    return pltpu.make_async_remote_copy(src_ref=src, dst_ref=dst, send_sem=send_sems.at[k], recv_sem=recv_sems.at[k],
                                        device_id=dev, device_id_type=MESH)


def _ag_copies(w_refs, o_refs, send_sems, recv_sems):
    x, y, c = _coords()
    me = 2 * x + y
    chips = _other_chips(x, y)

    def ici(k, j, owner):
        chip = chips[j]
        return _remote(w_refs[k].at[c], o_refs[k].at[owner, c], send_sems, recv_sems, 6 * k + j, (chip[0], chip[1], c))

    def d2d(k, j, part):
        owner = 2 * chips[j][0] + chips[j][1]
        return _remote(o_refs[k].at[owner, part], o_refs[k].at[owner, part], send_sems, recv_sems, 6 * k + 3 + j,
                       (x, y, 1 - c))

    n = len(w_refs)
    return me, c, chips, ici, d2d, [(k, j) for k in range(n) for j in range(3)]


def _ag_start(w_refs, o_refs, send_sems, recv_sems):
    me, _, _, ici, _, pairs = _ag_copies(w_refs, o_refs, send_sems, recv_sems)
    for k, j in pairs:
        ici(k, j, me).start()


def _ag_finish(w_refs, o_refs, send_sems, recv_sems):
    me, c, chips, ici, d2d, pairs = _ag_copies(w_refs, o_refs, send_sems, recv_sems)
    for k, j in pairs:
        ici(k, j, 2 * chips[j][0] + chips[j][1]).wait_recv()
        d2d(k, j, c).start()
    for k, j in pairs:
        d2d(k, j, 1 - c).wait_recv()
    for k, j in pairs:
        ici(k, j, me).wait_send()
        d2d(k, j, c).wait_send()


def _ag_specs(ws):
    n = len(ws)
    return dict(out_shape=[jax.ShapeDtypeStruct((N_CHIPS,) + w.shape, w.dtype) for w in ws],
                specs=[HBM_SPEC] * n, sems=[pltpu.SemaphoreType.DMA((6 * n,)), pltpu.SemaphoreType.DMA((6 * n,))])


def _ag_layers(ws):
    n = len(ws)

    def body(*refs):
        _ag_start(refs[:n], refs[n:2 * n], *refs[2 * n:])
        _ag_finish(refs[:n], refs[n:2 * n], *refs[2 * n:])

    sp = _ag_specs(ws)
    return pl.pallas_call(
        body, name="ag_weights", out_shape=sp["out_shape"], in_specs=sp["specs"], out_specs=sp["specs"],
        scratch_shapes=sp["sems"],
    )(*ws)


def _rs_pair_exchange(Gs):
    n = len(Gs)

    def body(*refs):
        g_refs, b_refs = refs[:n], refs[n:2 * n]
        send_sems, recv_sems = refs[2 * n:]
        x, y, c = _coords()
        cps = [_remote(g_refs[k].at[i, 1 - c], b_refs[k].at[i], send_sems, recv_sems, N_CHIPS * k + i, (x, y, 1 - c))
               for k in range(n) for i in range(N_CHIPS)]
        for cp in cps:
            cp.start()
        for cp in cps:
            cp.wait()

    return pl.pallas_call(
        body, name="rs_pair_exchange",
        out_shape=[jax.ShapeDtypeStruct((N_CHIPS,) + g.shape[2:], g.dtype) for g in Gs],
        in_specs=[HBM_SPEC] * n, out_specs=[HBM_SPEC] * n,
        scratch_shapes=[pltpu.SemaphoreType.DMA((N_CHIPS * n,)), pltpu.SemaphoreType.DMA((N_CHIPS * n,))],
    )(*Gs)


def _rs_add_pair(G, B, c, name):
    _, _, R, C = G.shape
    tr = _tile(R, 256, 16)

    def body(c_ref, g_ref, b_ref, o_ref):
        o_ref[0] = (g_ref[0, 0].astype(F32) + b_ref[0].astype(F32)).astype(o_ref.dtype)

    grid_spec = pltpu.PrefetchScalarGridSpec(
        num_scalar_prefetch=1, grid=(N_CHIPS, R // tr),
        in_specs=[pl.BlockSpec((1, 1, tr, C), lambda i, r, c_ref: (i, c_ref[0], r, 0)),
                  pl.BlockSpec((1, tr, C), lambda i, r, c_ref: (i, r, 0))],
        out_specs=pl.BlockSpec((1, tr, C), lambda i, r, c_ref: (i, r, 0)))
    return pl.pallas_call(
        body, name=name, grid_spec=grid_spec, out_shape=jax.ShapeDtypeStruct((N_CHIPS, R, C), G.dtype),
        compiler_params=_cp("parallel", "parallel"),
    )(jnp.reshape(c, (1,)).astype(jnp.int32), G, B)


def _rs_chip_exchange(Ps):
    n = len(Ps)

    def body(*refs):
        _rsx_start(refs[:n], refs[n:2 * n], *refs[2 * n:])
        _rsx_finish(refs[:n], refs[n:2 * n], *refs[2 * n:])

    sp = _rsx_specs(Ps)
    return pl.pallas_call(
        body, name="rs_chip_exchange", out_shape=sp["out_shape"], in_specs=sp["specs"], out_specs=sp["specs"],
        scratch_shapes=sp["sems"],
    )(*Ps)


def _rsx_copies(p_refs, b_refs, send_sems, recv_sems):
    x, y, c = _coords()
    me = 2 * x + y
    chips = _other_chips(x, y)

    def cp(k, j, src_slot, dst_slot):
        return _remote(p_refs[k].at[src_slot], b_refs[k].at[dst_slot], send_sems, recv_sems, 3 * k + j,
                       (chips[j][0], chips[j][1], c))

    return me, chips, cp, [(k, j) for k in range(len(p_refs)) for j in range(3)]


def _rsx_start(p_refs, b_refs, send_sems, recv_sems):
    me, chips, cp, pairs = _rsx_copies(p_refs, b_refs, send_sems, recv_sems)
    for k, j in pairs:
        cp(k, j, 2 * chips[j][0] + chips[j][1], me).start()


def _rsx_finish(p_refs, b_refs, send_sems, recv_sems):
    me, chips, cp, pairs = _rsx_copies(p_refs, b_refs, send_sems, recv_sems)
    for k, j in pairs:
        owner = 2 * chips[j][0] + chips[j][1]
        cp(k, j, owner, owner).wait_recv()
    for k, j in pairs:
        cp(k, j, 2 * chips[j][0] + chips[j][1], me).wait_send()


def _rsx_specs(Ps):
    n = len(Ps)
    return dict(out_shape=[jax.ShapeDtypeStruct(p.shape, p.dtype) for p in Ps], specs=[HBM_SPEC] * n,
                sems=[pltpu.SemaphoreType.DMA((3 * n,)), pltpu.SemaphoreType.DMA((3 * n,))])


def _rs_sum_chips(P, B, me, name):
    _, R, C = P.shape
    tr = _tile(R, 256, 16)

    def body(me_ref, p_ref, b1_ref, b2_ref, b3_ref, o_ref):
        o_ref[...] = ((p_ref[0].astype(F32) + b1_ref[0].astype(F32)) + b2_ref[0].astype(F32)) + b3_ref[0].astype(F32)

    slot = lambda d: pl.BlockSpec((1, tr, C), lambda r, me_ref: ((me_ref[0] + d) % N_CHIPS, r, 0))
    grid_spec = pltpu.PrefetchScalarGridSpec(
        num_scalar_prefetch=1, grid=(R // tr,), in_specs=[slot(0), slot(1), slot(2), slot(3)],
        out_specs=pl.BlockSpec((tr, C), lambda r, me_ref: (r, 0)))
    return pl.pallas_call(
        body, name=name, grid_spec=grid_spec, out_shape=jax.ShapeDtypeStruct((R, C), F32),
        compiler_params=_cp("parallel"),
    )(jnp.reshape(me, (1,)).astype(jnp.int32), P, B, B, B)


def _sum_slots(B, name):
    S, R, C = B.shape
    tr = _tile(R, 256, 16)

    def body(b_ref, o_ref):
        acc = b_ref[0].astype(F32)
        for i in range(1, S):
            acc = acc + b_ref[i].astype(F32)
        o_ref[...] = acc

    return pl.pallas_call(
        body, name=name, grid=(R // tr,), in_specs=[pl.BlockSpec((S, tr, C), lambda r: (0, r, 0))],
        out_specs=pl.BlockSpec((tr, C), lambda r: (r, 0)), out_shape=jax.ShapeDtypeStruct((R, C), F32),
        compiler_params=_cp("parallel"),
    )(B)


def _rs_pair_swap(Rs):
    n = len(Rs)

    def body(*refs):
        r_refs, o_refs = refs[:n], refs[n:2 * n]
        send_sems, recv_sems = refs[2 * n:]
        x, y, c = _coords()
        cps = [_remote(r_refs[k], o_refs[k], send_sems, recv_sems, k, (x, y, 1 - c)) for k in range(n)]
        for cp in cps:
            cp.start()
        for cp in cps:
            cp.wait()

    return pl.pallas_call(
        body, name="rs_pair_swap", out_shape=[jax.ShapeDtypeStruct(r.shape, r.dtype) for r in Rs],
        in_specs=[HBM_SPEC] * n, out_specs=[HBM_SPEC] * n,
        scratch_shapes=[pltpu.SemaphoreType.DMA((n,)), pltpu.SemaphoreType.DMA((n,))],
    )(*Rs)


def _ag8(v):
    R = v.shape[0]

    def body(v_ref, out_ref, send_sems, recv_sems, local_sem):
        x, y, c = _coords()
        me, sib = (x, y, c), (x, y, 1 - c)
        chips = _other_chips(x, y)

        def slot(p):
            return out_ref.at[4 * p[0] + 2 * p[1] + p[2]]

        def copy(k, block, to, src=None):
            return _remote(slot(block) if src is None else src, slot(block), send_sems, recv_sems, k, to)

        mine = pltpu.make_async_copy(v_ref, slot(me), local_sem)
        mine.start()
        first = [copy(0, me, sib, src=v_ref)]
        first += [copy(1 + j, me, (chip[0], chip[1], c), src=v_ref) for j, chip in enumerate(chips)]
        for cp in first:
            cp.start()
        passed = [copy(4 + j, (chip[0], chip[1], c), sib) for j, chip in enumerate(chips)]
        for j, chip in enumerate(chips):
            copy(1 + j, (chip[0], chip[1], c), me).wait_recv()
            passed[j].start()
        copy(0, sib, me).wait_recv()
        for j, chip in enumerate(chips):
            copy(4 + j, (chip[0], chip[1], 1 - c), me).wait_recv()
        for cp in first + passed:
            cp.wait_send()
        mine.wait()

    return pl.pallas_call(
        body, name="ag8_small", out_shape=jax.ShapeDtypeStruct((8, R, 128), v.dtype),
        in_specs=[pl.BlockSpec(memory_space=pltpu.VMEM)], out_specs=pl.BlockSpec(memory_space=pltpu.VMEM),
        scratch_shapes=[pltpu.SemaphoreType.DMA((7,)), pltpu.SemaphoreType.DMA((7,)), pltpu.SemaphoreType.DMA],
        compiler_params=pltpu.CompilerParams(vmem_limit_bytes=VMEM_LIMIT),
    )(v)


def _adamw(w, g, m, v, name):
    L, R, C = w.shape
    rows = [R] + [t for t in range(8, min(R, 1024) + 1, 8) if R % t == 0]
    cols = [C] + [t for t in range(128, C, 128) if C % t == 0]
    lead = [t for t in range(1, L + 1) if L % t == 0]
    fits = [(a * r * c, c, r, a) for a in lead for r in rows for c in cols if a * r * c * 4 <= 3 << 19]
    _, tc, tr, tl = max(fits) if fits else (0, min(cols), min(rows), 1)

    def body(w_ref, g_ref, m_ref, v_ref, d_ref, mo_ref, vo_ref):
        gv = g_ref[...]
        m2 = ADAM_B1 * m_ref[...] + (1.0 - ADAM_B1) * gv
        v2 = ADAM_B2 * v_ref[...] + (1.0 - ADAM_B2) * jnp.square(gv)
        m_hat = m2 / (1.0 - ADAM_B1 ** ADAM_STEP)
        v_hat = v2 / (1.0 - ADAM_B2 ** ADAM_STEP)
        d_ref[...] = -ADAM_LR * (m_hat / (jnp.sqrt(v_hat) + ADAM_EPS) + ADAM_WD * w_ref[...])
        mo_ref[...] = m2
        vo_ref[...] = v2

    blk = pl.BlockSpec((tl, tr, tc), lambda l, r, j: (l, r, j))
    return pl.pallas_call(
        body, name=name, grid=(L // tl, R // tr, C // tc), in_specs=[blk] * 4, out_specs=[blk] * 3,
        out_shape=[jax.ShapeDtypeStruct(w.shape, F32)] * 3,
        compiler_params=_cp("parallel", "parallel", "parallel"),
    )(w, g, m, v)


def _adamw_halves(w, g_mine, g_other, c, m, v, name):
    L, _, R, C = w.shape
    tr = _tile(R, 128, 8)

    def body(c_ref, w_ref, *rest):
        g_refs = rest[:2 * L]
        m_ref, v_ref, g_ref, d_ref, mo_ref, vo_ref = rest[2 * L:]
        l, h = pl.program_id(0), pl.program_id(1)
        gm, go = g_refs[0][...], g_refs[L][...]
        for i in range(1, L):
            gm = jnp.where(l == i, g_refs[i][...], gm)
            go = jnp.where(l == i, g_refs[L + i][...], go)
        gv = jnp.where(h == c_ref[0], gm, go)[None, None]
        g_ref[...] = gv
        m2 = ADAM_B1 * m_ref[...] + (1.0 - ADAM_B1) * gv
        v2 = ADAM_B2 * v_ref[...] + (1.0 - ADAM_B2) * jnp.square(gv)
        m_hat = m2 / (1.0 - ADAM_B1 ** ADAM_STEP)
        v_hat = v2 / (1.0 - ADAM_B2 ** ADAM_STEP)
        d_ref[...] = -ADAM_LR * (m_hat / (jnp.sqrt(v_hat) + ADAM_EPS) + ADAM_WD * w_ref[...])
        mo_ref[...] = m2
        vo_ref[...] = v2

    blk = pl.BlockSpec((1, 1, tr, C), lambda l, h, r, c_ref: (l, h, r, 0))

    def gblk(i, mine):
        def index(l, h, r, c_ref):
            use = jnp.logical_and(l == i, (h == c_ref[0]) == mine)
            return (jnp.where(use, r, 0), 0)
        return pl.BlockSpec((tr, C), index)

    grid_spec = pltpu.PrefetchScalarGridSpec(
        num_scalar_prefetch=1, grid=(L, 2, R // tr),
        in_specs=[blk] + [gblk(i, True) for i in range(L)] + [gblk(i, False) for i in range(L)] + [blk, blk],
        out_specs=[blk] * 4)
    return pl.pallas_call(
        body, name=name, grid_spec=grid_spec, out_shape=[jax.ShapeDtypeStruct(w.shape, F32)] * 4,
        compiler_params=_cp("parallel", "parallel", "parallel"),
    )(jnp.reshape(c, (1,)).astype(jnp.int32), w, *g_mine, *g_other, m, v)


BIG = ("w_in", "w_branch_a", "w_branch_b", "w_out", "ffn_w_gate", "ffn_w_up", "ffn_w_down")
ROW_SHARDED = ("w_out", "ffn_w_down")
SMALL = ("norm1_g", "dn_conv_w", "dn_a_log", "dn_dt_bias", "dn_onorm_g", "sg_ln_g", "sg_ln_b", "sg_w", "sg_b",
         "norm2_g", "ffn_conv_w", "ffn_conv_b", "final_norm_g")
SMALL_SHARDED = ("dn_conv_w", "ffn_conv_w")


def _pack_rows(arrs, mult):
    flat = jnp.concatenate([jnp.reshape(a, (-1,)) for a in arrs])
    n = flat.shape[0]
    rows = -(-n // (128 * mult)) * mult
    return jnp.reshape(jnp.pad(flat, (0, rows * 128 - n)), (rows, 128))


def _unpack(flat2d, shapes):
    flat = jnp.reshape(flat2d, (-1,))
    out, off = [], 0
    for shp in shapes:
        n = math.prod(shp)
        out.append(jnp.reshape(flat[off:off + n], shp))
        off += n
    return out


def _shards_to_full(a, row_sharded):
    if row_sharded:
        a = jnp.moveaxis(a, 0, 1)
        return jnp.reshape(a, (a.shape[0], a.shape[1] * a.shape[2], a.shape[3]))
    a = jnp.moveaxis(a, 0, 2)
    return jnp.reshape(a, (a.shape[0], a.shape[1], a.shape[2] * a.shape[3]))


def _full_to_shards(a, row_sharded):
    L, R, C = a.shape
    if row_sharded:
        return jnp.moveaxis(jnp.reshape(a, (L, N_CHIPS, R // N_CHIPS, C)), 1, 0)
    return jnp.moveaxis(jnp.reshape(a, (L, R, N_CHIPS, C // N_CHIPS)), 2, 0)


def kernel(x, norm1_g, w_in, dn_conv_w, dn_a_log, dn_dt_bias, dn_onorm_g, sg_ln_g, sg_ln_b, sg_w, sg_b, w_branch_a, w_branch_b, w_out, norm2_g, ffn_w_gate, ffn_w_up, ffn_conv_w, ffn_conv_b, ffn_w_down, final_norm_g, loss_target, m_norm1_g, m_w_in, m_dn_conv_w, m_dn_a_log, m_dn_dt_bias, m_dn_onorm_g, m_sg_ln_g, m_sg_ln_b, m_sg_w, m_sg_b, m_w_branch_a, m_w_branch_b, m_w_out, m_norm2_g, m_ffn_w_gate, m_ffn_w_up, m_ffn_conv_w, m_ffn_conv_b, m_ffn_w_down, m_final_norm_g, v_norm1_g, v_w_in, v_dn_conv_w, v_dn_a_log, v_dn_dt_bias, v_dn_onorm_g, v_sg_ln_g, v_sg_ln_b, v_sg_w, v_sg_b, v_w_branch_a, v_w_branch_b, v_w_out, v_norm2_g, v_ffn_w_gate, v_ffn_w_up, v_ffn_conv_w, v_ffn_conv_b, v_ffn_w_down, v_final_norm_g):
    W = dict(norm1_g=norm1_g, w_in=w_in, dn_conv_w=dn_conv_w, dn_a_log=dn_a_log, dn_dt_bias=dn_dt_bias,
             dn_onorm_g=dn_onorm_g, sg_ln_g=sg_ln_g, sg_ln_b=sg_ln_b, sg_w=sg_w, sg_b=sg_b, w_branch_a=w_branch_a,
             w_branch_b=w_branch_b, w_out=w_out, norm2_g=norm2_g, ffn_w_gate=ffn_w_gate, ffn_w_up=ffn_w_up,
             ffn_conv_w=ffn_conv_w, ffn_conv_b=ffn_conv_b, ffn_w_down=ffn_w_down, final_norm_g=final_norm_g)
    M = dict(norm1_g=m_norm1_g, w_in=m_w_in, dn_conv_w=m_dn_conv_w, dn_a_log=m_dn_a_log, dn_dt_bias=m_dn_dt_bias,
             dn_onorm_g=m_dn_onorm_g, sg_ln_g=m_sg_ln_g, sg_ln_b=m_sg_ln_b, sg_w=m_sg_w, sg_b=m_sg_b,
             w_branch_a=m_w_branch_a, w_branch_b=m_w_branch_b, w_out=m_w_out, norm2_g=m_norm2_g,
             ffn_w_gate=m_ffn_w_gate, ffn_w_up=m_ffn_w_up, ffn_conv_w=m_ffn_conv_w, ffn_conv_b=m_ffn_conv_b,
             ffn_w_down=m_ffn_w_down, final_norm_g=m_final_norm_g)
    V = dict(norm1_g=v_norm1_g, w_in=v_w_in, dn_conv_w=v_dn_conv_w, dn_a_log=v_dn_a_log, dn_dt_bias=v_dn_dt_bias,
             dn_onorm_g=v_dn_onorm_g, sg_ln_g=v_sg_ln_g, sg_ln_b=v_sg_ln_b, sg_w=v_sg_w, sg_b=v_sg_b,
             w_branch_a=v_w_branch_a, w_branch_b=v_w_branch_b, w_out=v_w_out, norm2_g=v_norm2_g,
             ffn_w_gate=v_ffn_w_gate, ffn_w_up=v_ffn_w_up, ffn_conv_w=v_ffn_conv_w, ffn_conv_b=v_ffn_conv_b,
             ffn_w_down=v_ffn_w_down, final_norm_g=v_final_norm_g)
    cx, cy, cc = _coords()
    chip = 2 * cx + cy
    L = w_in.shape[0]

    D = w_in.shape[1]
    cs_in = w_in.shape[2]
    rp_in = -(-cs_in // 128) * 128

    def shard_for_gather(n):
        if n == "w_in":
            return jnp.pad(jnp.swapaxes(W[n], 1, 2).astype(BF16), ((0, 0), (0, rp_in - cs_in), (0, 0)))
        return W[n].astype(BF16)

    mine = {n: shard_for_gather(n) for n in BIG}

    def halves(a, lead=0):
        return jnp.reshape(a, a.shape[:lead] + (2, a.shape[lead] // 2) + a.shape[lead + 1:])

    taps = _ag8(_pack_rows([W[n] for n in SMALL_SHARDED], 16))
    tap_shards = [_unpack(taps[2 * i], [W[n].shape for n in SMALL_SHARDED]) for i in range(N_CHIPS)]
    taps_full = {n: jnp.concatenate([tap_shards[i][k] for i in range(N_CHIPS)], axis=-1)
                 for k, n in enumerate(SMALL_SHARDED)}

    ops = []
    for l in range(L):
        p = {n: W[n][l] for n in W if n not in ("final_norm_g",) + BIG + SMALL_SHARDED}
        p.update({n: taps_full[n][l] for n in SMALL_SHARDED})
        ops.append(_prep_small(p))

    def gather_payload(items):
        return ("gather", [halves(mine[n][l]) for l, n in items])

    def weights_landed(items, gathered):
        for (l, n), a in zip(items, gathered):
            a = jnp.reshape(a, (N_CHIPS,) + mine[n].shape[1:])
            parts = [jnp.where(chip == i, mine[n][l], a[i]) for i in range(N_CHIPS)]
            if n == "w_in":
                ops[l]["w_in_t"] = _w_in_pad(jnp.concatenate([q[:cs_in] for q in parts], axis=0))
            else:
                ops[l][n] = jnp.concatenate(parts, axis=0 if n in ROW_SHARDED else 1)

    partial_sums, chip_sums = {}, {}

    def grad_partials(l, names, g):
        Gs = []
        for n in names:
            if n == "w_in":
                gt = jnp.reshape(_w_in_unpad(g["w_in_t"]), (N_CHIPS, cs_in, D))
                a = jnp.pad(gt, ((0, 0), (0, rp_in - cs_in), (0, 0)))
            elif n in ROW_SHARDED:
                a = jnp.reshape(g[n], (N_CHIPS, g[n].shape[0] // N_CHIPS, g[n].shape[1]))
            else:
                a = jnp.moveaxis(jnp.reshape(g[n], (g[n].shape[0], N_CHIPS, g[n].shape[1] // N_CHIPS)), 1, 0)
            Gs.append(halves(a, 1))
        B1s = _rs_pair_exchange(Gs)
        for n, a, b in zip(names, Gs, B1s):
            partial_sums[(l, n)] = _rs_add_pair(a, b, cc, "rs_add_pair_" + n)

    def carrier(l, plan, landed):
        def payload(kernel):
            items = plan.get((l, kernel))
            if not items:
                return None
            return gather_payload(items) if landed is weights_landed else ("exchange", [partial_sums[i] for i in items])
        return _Carrier(payload, lambda kernel, res: landed(plan[(l, kernel)], res))

    FFN = ("ffn_w_gate", "ffn_w_up", "ffn_w_down")
    REST = ("w_in", "w_branch_a", "w_branch_b", "w_out")
    fwd_plan = {(0, "proj"): [(0, "w_branch_a"), (0, "w_branch_b"), (0, "w_out"), (0, "ffn_w_gate")],
                (0, "dn_core"): [(0, "ffn_w_up"), (0, "ffn_w_down"), (1, "w_in")],
                (0, "ffn_gate"): [(1, "w_branch_a"), (1, "w_branch_b"), (1, "w_out")],
                (0, "ffn_up"): [(1, "ffn_w_gate")],
                (0, "ffn_down"): [(1, "ffn_w_up")],
                (1, "proj"): [(1, "ffn_w_down")]}
    bwd_plan = {(1, "dn_core"): [(1, n) for n in FFN],
                (0, "d_act"): [(1, "w_branch_a"), (1, "w_branch_b"), (1, "w_out")],
                (0, "dn_core"): [(1, "w_in"), (0, "ffn_w_down")],
                (0, "dw_in"): [(0, "ffn_w_gate"), (0, "ffn_w_up")],
                (0, "dh"): [(0, n) for n in REST]}

    def sums_landed(items, res):
        chip_sums.update(zip(items, res))

    first = [(0, "w_in")]
    weights_landed(first, _ag_layers(gather_payload(first)[1]))
    xs, saved = x[0], []
    for l in range(L):
        xs, s = _layer_fwd(xs, ops[l], carrier(l, fwd_plan, weights_landed))
        saved.append(s)
    dx, dgf, loss = _loss_head(xs, final_norm_g[None], loss_target[0])
    loss = loss[0, 0]
    grads = [None] * L
    for l in reversed(range(L)):
        dx, grads[l] = _layer_bwd(dx, ops[l], saved[l], carrier(l, bwd_plan, sums_landed),
                                  functools.partial(grad_partials, l, FFN), functools.partial(grad_partials, l, REST))
    g_mine = [[_rs_sum_chips(partial_sums[(l, n)], chip_sums[(l, n)], chip, "rs_sum_chips_" + n) for n in BIG]
              for l in range(L)]
    swapped = _rs_pair_swap(g_mine[0] + g_mine[1])
    g_other = [swapped[:len(BIG)], swapped[len(BIG):]]

    small = {n: jnp.stack([g[n] for g in grads]) for n in SMALL if n != "final_norm_g"}
    small["final_norm_g"] = dgf
    shapes = [taps_full[n].shape if n in SMALL_SHARDED else W[n].shape for n in SMALL] + [(1,)]
    sflat = _pack_rows([small[n] for n in SMALL] + [jnp.reshape(loss, (1,))], 16)
    sred = _unpack(_sum_slots(_ag8(sflat), "sum_small"), shapes)
    g_small = dict(zip(SMALL, sred[:-1]))
    loss_total = sred[-1][0]
    for n in SMALL_SHARDED:
        cs = W[n].shape[-1]
        g_small[n] = lax.dynamic_slice_in_dim(g_small[n], chip * cs, cs, axis=-1)

    g_big, delta, new_m, new_v = {}, {}, {}, {}
    for k, n in enumerate(BIG):
        gm, go = [g_mine[l][k] for l in range(L)], [g_other[l][k] for l in range(L)]
        if n == "w_in":
            rows = [jnp.where(cc == 0, jnp.concatenate([a, b]), jnp.concatenate([b, a])) for a, b in zip(gm, go)]
            g_t = jnp.stack([r[:cs_in] for r in rows], axis=1)
            outs = _adamw(*[jnp.transpose(a, (2, 0, 1)) for a in (W[n],)], g_t,
                          *[jnp.transpose(a, (2, 0, 1)) for a in (M[n], V[n])], "adamw_" + n)
            g_big[n], delta[n], new_m[n], new_v[n] = [jnp.transpose(o, (1, 2, 0)) for o in (g_t,) + tuple(outs)]
        else:
            outs = _adamw_halves(halves(W[n], 1), gm, go, cc, halves(M[n], 1), halves(V[n], 1), "adamw_" + n)
            g_big[n], delta[n], new_m[n], new_v[n] = [jnp.reshape(o, W[n].shape) for o in outs]
    s_shapes = [W[n].shape for n in SMALL]
    packed = [_pack_rows([d[n] for n in SMALL], 8) for d in (W, g_small, M, V)]
    outs = _adamw(*[a[None] for a in packed], "adamw_small")
    for d, o in zip((delta, new_m, new_v), outs):
        d.update(zip(SMALL, _unpack(o[0], s_shapes)))

    names = list(W)
    grad_w = {**g_big, **g_small}
    return (loss_total, dx[None], *[grad_w[n] for n in names], *[delta[n] for n in names],
            *[new_m[n] for n in names], *[new_v[n] for n in names])
```

```python
import functools
import math

import jax
import jax.numpy as jnp
from jax import lax
from jax.experimental import pallas as pl
from jax.experimental.pallas import tpu as pltpu

F32 = jnp.float32
BF16 = jnp.bfloat16
MESH = pl.DeviceIdType.MESH

EPS = 1e-6
H = 8
DH = 128
WD = H * DH
DNC = 64
SGC = 128
DN_K = 4
FF_K = 3
DEPTH = 2
N_CHIPS = 4

ADAM_LR = 0.001
ADAM_B1 = 0.9
ADAM_B2 = 0.999
ADAM_EPS = 1e-08
ADAM_WD = 0.01
ADAM_STEP = 10

VMEM_LIMIT = 56 * 1024 * 1024

NN = (((1,), (0,)), ((), ()))
NT = (((1,), (1,)), ((), ()))
TN = (((0,), (0,)), ((), ()))

OQ, OZ, OU, OV, OGA = 0, 3 * WD, 4 * WD, 5 * WD, 6 * WD


def _cp(*sem):
    return pltpu.CompilerParams(dimension_semantics=sem or None, vmem_limit_bytes=VMEM_LIMIT)


def _tile(dim, pref, unit=128):
    if dim <= pref:
        return dim
    t = (pref // unit) * unit
    while t >= unit:
        if dim % t == 0:
            return t
        t -= unit
    return dim


def _hdot(a, b, dn=NN):
    return lax.dot_general(a, b, dn, precision=lax.Precision.HIGHEST, preferred_element_type=F32)


def _bdot(a, b, dn=NN):
    return lax.dot_general(a.astype(BF16), b.astype(BF16), dn, preferred_element_type=F32)


def _lsum(x):
    return jnp.sum(x, axis=1, keepdims=True)


def _sig(x):
    return 0.5 * jnp.tanh(0.5 * x) + 0.5


def _silu_and_grad(x):
    s = _sig(x)
    return x * s, s * (1.0 + x * (1.0 - s))


def _gelu_parts(x):
    a = jnp.abs(x) * (2.0 ** -0.5)
    t = 1.0 / (1.0 + 0.3275911 * a)
    poly = t * (0.254829592 + t * (-0.284496736 + t * (1.421413741 + t * (-1.453152027 + t * 1.061405429))))
    e = jnp.exp(-a * a)
    half = 0.5 * poly * e
    return jnp.where(x < 0, half, 1.0 - half), e * (1.0 / math.sqrt(2.0 * math.pi))


def _gelu(x):
    return x * _gelu_parts(x)[0]


def _gelu_and_grad(x):
    cdf, pdf = _gelu_parts(x)
    return x * cdf, cdf + x * pdf


def _shift_down(x, k):
    if k == 0:
        return x
    y = pltpu.roll(x, k, 0)
    rows = lax.broadcasted_iota(jnp.int32, (8, x.shape[1]), 0)
    return jnp.concatenate([jnp.where(rows >= k, y[:8], 0.0), y[8:]], axis=0)


def _shift_up(x, k):
    if k == 0:
        return x
    n = x.shape[0]
    y = pltpu.roll(x, n - k, 0)
    rows = lax.broadcasted_iota(jnp.int32, (8, x.shape[1]), 0)
    return jnp.concatenate([y[:n - 8], jnp.where(rows < 8 - k, y[n - 8:], 0.0)], axis=0)


def _comm_fns(comm):
    if not comm:
        return None, None, dict(out_shape=[], specs=[], sems=[]), ()
    kind, arrays = comm
    start, finish, specs = {"gather": (_ag_start, _ag_finish, _ag_specs),
                            "exchange": (_rsx_start, _rsx_finish, _rsx_specs)}[kind]
    return start, finish, specs(arrays), tuple(arrays)


def _mm(a, b, mode, out_dtype, add=None, name="mm", comm=None):
    if mode == "tn":
        K, M = a.shape
    else:
        M, K = a.shape
    N = b.shape[0] if mode == "nt" else b.shape[1]
    tm, tn, tk = _tile(M, 1152), _tile(N, 1536), _tile(K, 3584)
    nk = K // tk
    ni, nj = M // tm, N // tn
    dn = {"nn": NN, "nt": NT, "tn": TN}[mode]
    c_start, c_finish, c_sp, payload = _comm_fns(comm)
    nc = len(payload)
    n_add = 0 if add is None else 1

    def body(*refs):
        a_ref, b_ref = refs[:2]
        add_ref = refs[2] if n_add else None
        c_in = refs[2 + n_add:2 + n_add + nc]
        o_ref = refs[2 + n_add + nc]
        c_out = refs[3 + n_add + nc:3 + n_add + 2 * nc]
        rest = refs[3 + n_add + 2 * nc:]
        acc_ref = rest[0] if nk > 1 else None
        sems = rest[1:] if nk > 1 else rest
        i, j, k = pl.program_id(0), pl.program_id(1), pl.program_id(2)

        if nc:
            @pl.when(jnp.logical_and(jnp.logical_and(i == 0, j == 0), k == 0))
            def _():
                c_start(c_in, c_out, *sems)

        def finish(r):
            if add is not None:
                r = r + add_ref[...]
            o_ref[...] = r.astype(o_ref.dtype)

        part = lax.dot_general(a_ref[...], b_ref[...], dn, preferred_element_type=F32)
        if nk == 1:
            finish(part)
        else:
            @pl.when(k == 0)
            def _():
                acc_ref[...] = part

            @pl.when(k > 0)
            def _():
                acc_ref[...] += part

            @pl.when(k == nk - 1)
            def _():
                finish(acc_ref[...])

        if nc:
            @pl.when(jnp.logical_and(jnp.logical_and(i == ni - 1, j == nj - 1), k == nk - 1))
            def _():
                c_finish(c_in, c_out, *sems)

    a_spec = (pl.BlockSpec((tk, tm), lambda i, j, k: (k, i)) if mode == "tn"
              else pl.BlockSpec((tm, tk), lambda i, j, k: (i, k)))
    b_spec = (pl.BlockSpec((tn, tk), lambda i, j, k: (j, k)) if mode == "nt"
              else pl.BlockSpec((tk, tn), lambda i, j, k: (k, j)))
    o_spec = pl.BlockSpec((tm, tn), lambda i, j, k: (i, j))
    in_specs = [a_spec, b_spec] + ([o_spec] if add is not None else []) + c_sp["specs"]
    args = (a, b) + ((add,) if add is not None else ()) + payload
    outs = pl.pallas_call(
        body, name=name + ("_" + comm[0] if nc else ""), grid=(ni, nj, nk), in_specs=in_specs,
        out_specs=[o_spec] + c_sp["specs"],
        out_shape=[jax.ShapeDtypeStruct((M, N), out_dtype)] + c_sp["out_shape"],
        scratch_shapes=([pltpu.VMEM((tm, tn), F32)] if nk > 1 else []) + c_sp["sems"],
        compiler_params=_cp("arbitrary", "arbitrary", "arbitrary") if nc else _cp("parallel", "parallel", "arbitrary"),
    )(*args)
    return (outs[0], list(outs[1:])) if nc else outs[0]


def _rms_fwd(x, g, name):
    T, D = x.shape
    tt = _tile(T, 256, 16)

    def body(x_ref, g_ref, o_ref):
        xv = x_ref[...]
        r = lax.rsqrt(jnp.mean(xv * xv, axis=-1, keepdims=True) + EPS)
        o_ref[...] = (xv * r * g_ref[...]).astype(o_ref.dtype)

    return pl.pallas_call(
        body, name=name, grid=(T // tt,),
        in_specs=[pl.BlockSpec((tt, D), lambda i: (i, 0)), pl.BlockSpec((1, D), lambda i: (0, 0))],
        out_specs=pl.BlockSpec((tt, D), lambda i: (i, 0)),
        out_shape=jax.ShapeDtypeStruct((T, D), BF16), compiler_params=_cp("parallel"),
    )(x, g)


def _rms_bwd(x, g, dh, dres, name):
    T, D = x.shape
    tt = _tile(T, 256, 16)

    def body(x_ref, g_ref, dh_ref, dres_ref, dx_ref, dg_ref):
        @pl.when(pl.program_id(0) == 0)
        def _():
            dg_ref[...] = jnp.zeros_like(dg_ref)

        xv = x_ref[...]
        r = lax.rsqrt(jnp.mean(xv * xv, axis=-1, keepdims=True) + EPS)
        xh = xv * r
        dh_v = dh_ref[...]
        dy = dh_v * g_ref[...]
        dx_ref[...] = dres_ref[...] + r * (dy - xh * jnp.mean(dy * xh, axis=-1, keepdims=True))
        dg_ref[...] += jnp.sum(dh_v * xh, axis=0, keepdims=True)

    row = pl.BlockSpec((tt, D), lambda i: (i, 0))
    vec = pl.BlockSpec((1, D), lambda i: (0, 0))
    return pl.pallas_call(
        body, name=name, grid=(T // tt,), in_specs=[row, vec, row, row], out_specs=[row, vec],
        out_shape=[jax.ShapeDtypeStruct((T, D), F32), jax.ShapeDtypeStruct((1, D), F32)],
        compiler_params=_cp("arbitrary"),
    )(x, g, dh, dres)


def _loss_head(x, g, tgt, name="loss_head"):
    T, D = x.shape
    tt = _tile(T, 256, 16)

    def body(x_ref, g_ref, t_ref, dx_ref, dg_ref, loss_ref):
        @pl.when(pl.program_id(0) == 0)
        def _():
            dg_ref[...] = jnp.zeros_like(dg_ref)
            loss_ref[...] = jnp.zeros_like(loss_ref)

        xv = x_ref[...]
        r = lax.rsqrt(jnp.mean(xv * xv, axis=-1, keepdims=True) + EPS)
        xh = xv * r
        err = xh * g_ref[...] - t_ref[...]
        part = 0.5 * jnp.sum(jnp.mean(err * err, axis=-1, keepdims=True), axis=0, keepdims=True)
        loss_ref[...] += jnp.broadcast_to(part, loss_ref.shape)
        dy = err * (1.0 / D)
        dg_ref[...] += jnp.sum(dy * xh, axis=0, keepdims=True)
        dyh = dy * g_ref[...]
        dx_ref[...] = r * (dyh - xh * jnp.mean(dyh * xh, axis=-1, keepdims=True))

    row = pl.BlockSpec((tt, D), lambda i: (i, 0))
    vec = pl.BlockSpec((1, D), lambda i: (0, 0))
    return pl.pallas_call(
        body, name=name, grid=(T // tt,), in_specs=[row, vec, row],
        out_specs=[row, vec, pl.BlockSpec((1, 128), lambda i: (0, 0))],
        out_shape=[jax.ShapeDtypeStruct((T, D), F32), jax.ShapeDtypeStruct((1, D), F32),
                   jax.ShapeDtypeStruct((1, 128), F32)],
        compiler_params=_cp("arbitrary"),
    )(x, g, tgt)


def _ba_fwd(proj, alog, dtb, oba, name="dn_ba_fwd"):
    T = proj.shape[0]
    tt = _tile(T, 512, 8)

    def body(p_ref, al_ref, dt_ref, o_ref):
        raw = p_ref[...].astype(F32)
        lane = lax.broadcasted_iota(jnp.int32, raw.shape, 1)
        z = raw + dt_ref[...]
        sp = jnp.maximum(z, 0.0) + jnp.log(1.0 + jnp.exp(-jnp.abs(z)))
        gl = -jnp.exp(al_ref[...]) * sp
        o_ref[...] = jnp.where(lane < H, _sig(raw), jnp.where(lane < 2 * H, gl, 0.0))

    vec = pl.BlockSpec((1, 128), lambda i: (0, 0))
    return pl.pallas_call(
        body, name=name, grid=(T // tt,),
        in_specs=[pl.BlockSpec((tt, 128), lambda i: (i, oba // 128)), vec, vec],
        out_specs=pl.BlockSpec((tt, 128), lambda i: (i, 0)),
        out_shape=jax.ShapeDtypeStruct((T, 128), F32), compiler_params=_cp("parallel"),
    )(proj, alog, dtb)


def _ba_bwd(proj, alog, dtb, dbg, oba, name="dn_ba_bwd"):
    T = proj.shape[0]
    tt = _tile(T, 512, 16)

    def body(p_ref, al_ref, dt_ref, d_ref, o_ref, dal_ref, ddt_ref):
        @pl.when(pl.program_id(0) == 0)
        def _():
            dal_ref[...] = jnp.zeros_like(dal_ref)
            ddt_ref[...] = jnp.zeros_like(ddt_ref)

        raw = p_ref[...].astype(F32)
        d = d_ref[...]
        lane = lax.broadcasted_iota(jnp.int32, raw.shape, 1)
        z = raw + dt_ref[...]
        sp = jnp.maximum(z, 0.0) + jnp.log(1.0 + jnp.exp(-jnp.abs(z)))
        na = -jnp.exp(al_ref[...])
        is_g = jnp.logical_and(lane >= H, lane < 2 * H)
        b = _sig(raw)
        dz = jnp.where(is_g, d * na * _sig(z), 0.0)
        o_ref[...] = jnp.where(lane < H, d * b * (1.0 - b), dz).astype(o_ref.dtype)
        dal_ref[...] += jnp.sum(jnp.where(is_g, d * na * sp, 0.0), axis=0, keepdims=True)
        ddt_ref[...] += jnp.sum(dz, axis=0, keepdims=True)

    vec = pl.BlockSpec((1, 128), lambda i: (0, 0))
    return pl.pallas_call(
        body, name=name, grid=(T // tt,),
        in_specs=[pl.BlockSpec((tt, 128), lambda i: (i, oba // 128)), vec, vec,
                  pl.BlockSpec((tt, 128), lambda i: (i, 0))],
        out_specs=[pl.BlockSpec((tt, 128), lambda i: (i, 0)), vec, vec],
        out_shape=[jax.ShapeDtypeStruct((T, 128), BF16), jax.ShapeDtypeStruct((1, 128), F32),
                   jax.ShapeDtypeStruct((1, 128), F32)],
        compiler_params=_cp("arbitrary"),
    )(proj, alog, dtb, dbg)


def _dn_prep_fwd(proj, convw, name="dn_prep_fwd"):
    T = proj.shape[0]
    nblk = 3 * H

    def body(p_ref, w_ref, o_ref):
        j = pl.program_id(0)
        xv = p_ref[...].astype(F32)
        w = w_ref[...]
        c = xv * w[DN_K - 1:DN_K, :]
        for k in range(1, DN_K):
            c = c + _shift_down(xv, k) * w[DN_K - 1 - k:DN_K - k, :]
        s = c * _sig(c)
        r = lax.rsqrt(_lsum(s * s) + EPS)
        o_ref[...] = jnp.where(j < 2 * H, s * r, s)

    return pl.pallas_call(
        body, name=name, grid=(nblk,),
        in_specs=[pl.BlockSpec((T, DH), lambda j: (0, j)), pl.BlockSpec((DN_K, DH), lambda j: (0, j))],
        out_specs=pl.BlockSpec((T, DH), lambda j: (0, j)),
        out_shape=jax.ShapeDtypeStruct((T, 3 * WD), F32), compiler_params=_cp("parallel"),
    )(proj, convw)


def _dn_prep_bwd(proj, convw, dq, dk, dv, name="dn_prep_bwd"):
    T = proj.shape[0]
    nblk = 3 * H

    def body(p_ref, w_ref, dq_ref, dk_ref, dv_ref, dx_ref, dw_ref):
        j = pl.program_id(0)
        xv = p_ref[...].astype(F32)
        w = w_ref[...]
        shifted = [_shift_down(xv, k) for k in range(DN_K)]
        c = shifted[0] * w[DN_K - 1:DN_K, :]
        for k in range(1, DN_K):
            c = c + shifted[k] * w[DN_K - 1 - k:DN_K - k, :]
        s, s_grad = _silu_and_grad(c)
        r = lax.rsqrt(_lsum(s * s) + EPS)
        y = s * r
        dy = jnp.where(j < H, dq_ref[...], jnp.where(j < 2 * H, dk_ref[...], dv_ref[...]))
        ds = jnp.where(j < 2 * H, r * (dy - y * _lsum(dy * y)), dy)
        dc = ds * s_grad
        dx = dc * w[DN_K - 1:DN_K, :]
        for k in range(1, DN_K):
            dx = dx + _shift_up(dc, k) * w[DN_K - 1 - k:DN_K - k, :]
        dx_ref[...] = dx.astype(dx_ref.dtype)
        rows = [jnp.sum(dc * shifted[DN_K - 1 - t], axis=0, keepdims=True) for t in range(DN_K)]
        dw_ref[...] = jnp.concatenate(rows, axis=0)

    hb = lambda off: pl.BlockSpec((T, DH), lambda j: (0, jnp.maximum(jnp.minimum(j - off, H - 1), 0)))
    return pl.pallas_call(
        body, name=name, grid=(nblk,),
        in_specs=[pl.BlockSpec((T, DH), lambda j: (0, j)), pl.BlockSpec((DN_K, DH), lambda j: (0, j)),
                  hb(0), hb(H), hb(2 * H)],
        out_specs=[pl.BlockSpec((T, DH), lambda j: (0, j)), pl.BlockSpec((DN_K, DH), lambda j: (0, j))],
        out_shape=[jax.ShapeDtypeStruct((T, 3 * WD), BF16), jax.ShapeDtypeStruct((DN_K, 3 * WD), F32)],
        compiler_params=_cp("parallel"),
    )(proj, convw, dq, dk, dv)


DN_BLOCK = 4


def _split3(a):
    hi = a.astype(BF16)
    r1 = a - hi.astype(F32)
    mid = r1.astype(BF16)
    return hi, mid, (r1 - mid.astype(F32)).astype(BF16)


def _dot3(a, b, dn=NN):
    ah, al, _ = _split3(a)
    bh, bl, _ = _split3(b)
    d = lambda p, q: lax.dot_general(p, q, dn, preferred_element_type=F32)
    return d(ah, bh) + d(ah, bl) + d(al, bh)


def _mask_dot(m, b, dn=NN):
    mb = m.astype(BF16)
    d = lambda q: (lax.dot_general(mb, q, dn, preferred_element_type=F32) if dn != TN
                   else lax.dot_general(q, mb, dn, preferred_element_type=F32))
    b0, b1, b2 = _split3(b)
    return d(b0) + d(b1) + d(b2)


def _tri_inv(A):
    ri = lax.broadcasted_iota(jnp.int32, A.shape, 0)
    ci = lax.broadcasted_iota(jnp.int32, A.shape, 1)
    X = -A
    P = jnp.where(ri == ci, 1.0, 0.0) + X
    Y = X
    for _ in range(int(math.log2(DNC)) - 1):
        Y = _dot3(Y, Y)
        P = P + _dot3(P, Y)
    return P


GH = 4
NG = H // GH
GR = GH * DNC
GK = GH * DH


def _dn_masks():
    ri = lax.broadcasted_iota(jnp.int32, (GR, GR), 0)
    ci = lax.broadcasted_iota(jnp.int32, (GR, GR), 1)
    blk = (ri // DNC) == (ci // DNC)
    wide = (lax.broadcasted_iota(jnp.int32, (GR, GK), 0) // DNC) == (lax.broadcasted_iota(jnp.int32, (GR, GK), 1) // DH)
    return dict(blk=blk, causal=jnp.logical_and(blk, ri >= ci), strict=jnp.logical_and(blk, ri > ci),
                upper=jnp.logical_and(blk, ri <= ci), eye=ri == ci, wide=wide)


def _wide(a, mk):
    return jnp.where(mk["wide"], jnp.tile(a, (1, GH)), 0.0)


def _fold(a, mk):
    a = jnp.where(mk["wide"], a, 0.0)
    out = a[:, :DH]
    for j in range(1, GH):
        out = out + a[:, j * DH:(j + 1) * DH]
    return out


def _stack_heads(ref, rows, g):
    return jnp.concatenate([ref[rows, (g * GH + j) * DH:(g * GH + j + 1) * DH] for j in range(GH)], axis=0)


def _dn_group(q_ref, k_ref, v_ref, rows, bg, gc_cols, g, mk):
    heads = [g * GH + j for j in range(GH)]
    col = lambda a, lane: jnp.concatenate([a[:, lane(h):lane(h) + 1] for h in heads], axis=0)
    q = _stack_heads(q_ref, rows, g) * (DH ** -0.5)
    k = _stack_heads(k_ref, rows, g)
    v = _stack_heads(v_ref, rows, g)
    beta = col(bg, lambda h: h)
    gcol = col(gc_cols, lambda h: H + h)
    last = [gc_cols[DNC - 1:DNC, H + h:H + h + 1] for h in heads]
    gl = jnp.concatenate([jnp.broadcast_to(t, (DNC, 1)) for t in last], axis=0)
    egl_state = jnp.concatenate([jnp.broadcast_to(jnp.exp(t), (DH, 1)) for t in last], axis=0)
    grow = _mask_dot(jnp.ones((GR, GR), F32), jnp.where(mk["eye"], gcol, 0.0))
    dec = jnp.where(mk["causal"], jnp.exp(jnp.where(mk["causal"], gcol - grow, 0.0)), 0.0)
    eg = jnp.exp(gcol)
    ek = jnp.exp(gl - gcol)
    kb = k * beta
    vb = v * beta
    kbe = kb * eg
    A = jnp.where(mk["strict"], _bdot(kb, k, NT) * dec, 0.0)
    P = jnp.where(mk["causal"], _bdot(q, k, NT) * dec, 0.0)
    return dict(q=q, k=k, v=v, beta=beta, dec=dec, eg=eg, ek=ek, egl=jnp.exp(gl), egl_state=egl_state, kb=kb, vb=vb,
                kbe=kbe, A=A, P=P, qd=q * eg, kd=k * ek, heads=heads)


def _gc_cols(bg):
    ri = lax.broadcasted_iota(jnp.int32, (DNC, DNC), 0)
    ci = lax.broadcasted_iota(jnp.int32, (DNC, DNC), 1)
    return _mask_dot(jnp.where(ri >= ci, 1.0, 0.0), bg)


def _dn_core_fwd(qkv, bg, comm=None, name="dn_core_fwd"):
    c_start, c_finish, sp, gather = _comm_fns(comm)
    T = qkv.shape[0]
    n_chunks = T // DNC
    nb = _tile(n_chunks, DN_BLOCK, 1)
    tb = nb * DNC

    ng = len(gather)
    n_steps = n_chunks // nb

    def body(*refs):
        q_ref, k_ref, v_ref, bg_ref = refs[:4]
        o_ref, s_ref, tm_ref = refs[4 + ng:7 + ng]
        S_scr = refs[7 + 2 * ng]
        comm_refs = (refs[4:4 + ng], refs[7 + ng:7 + 2 * ng]) + tuple(refs[8 + 2 * ng:])

        @pl.when(pl.program_id(0) == 0)
        def _():
            S_scr[...] = jnp.zeros_like(S_scr)
            if ng:
                c_start(*comm_refs)

        def chunk(n, carry):
            rows = pl.ds(pl.multiple_of(n * DNC, DNC), DNC)
            mk = _dn_masks()
            bgc = bg_ref[rows, :]
            gc_cols = _gc_cols(bgc)
            for g in range(NG):
                c = _dn_group(q_ref, k_ref, v_ref, rows, bgc, gc_cols, g, mk)
                Tm = _tri_inv(c["A"])
                tm_ref[n, g] = Tm
                S = S_scr[g]
                s_ref[n, g] = S
                u = _bdot(Tm, c["vb"])
                w = _bdot(Tm, c["kbe"])
                vn = u - _bdot(_wide(w, mk), S)
                o = _bdot(_wide(c["qd"], mk), S) + _bdot(c["P"], vn)
                for j, h in enumerate(c["heads"]):
                    o_ref[rows, h * DH:(h + 1) * DH] = o[j * DNC:(j + 1) * DNC]
                S_scr[g] = S * c["egl_state"] + _bdot(_wide(c["kd"], mk), vn, TN)
            return carry

        lax.fori_loop(0, nb, chunk, 0)

        if ng:
            @pl.when(pl.program_id(0) == n_steps - 1)
            def _():
                c_finish(*comm_refs)

    blk = lambda j: pl.BlockSpec((tb, WD), lambda i: (i, j))
    outs = pl.pallas_call(
        body, name=name + ("_" + comm[0] if ng else ""), grid=(n_steps,),
        in_specs=[blk(0), blk(1), blk(2), pl.BlockSpec((tb, 128), lambda i: (i, 0))] + sp["specs"],
        out_specs=[blk(0), pl.BlockSpec((nb, NG, GK, DH), lambda i: (i, 0, 0, 0)),
                   pl.BlockSpec((nb, NG, GR, GR), lambda i: (i, 0, 0, 0))] + sp["specs"],
        out_shape=[jax.ShapeDtypeStruct((T, WD), F32), jax.ShapeDtypeStruct((n_chunks, NG, GK, DH), F32),
                   jax.ShapeDtypeStruct((n_chunks, NG, GR, GR), F32)] + sp["out_shape"],
        scratch_shapes=[pltpu.VMEM((NG, GK, DH), F32)] + (sp["sems"] if ng else []),
        compiler_params=_cp("arbitrary"),
    )(qkv, qkv, qkv, bg, *gather)
    return outs[0], outs[1], outs[2], list(outs[3:])


def _dn_core_bwd(qkv, bg, s_all, tm_all, do, comm=None, name="dn_core_bwd"):
    c_start, c_finish, sp, exchange = _comm_fns(comm)
    T = qkv.shape[0]
    n_chunks = T // DNC
    nb = _tile(n_chunks, DN_BLOCK, 1)
    tb = nb * DNC
    n_blocks = n_chunks // nb

    nx = len(exchange)

    def body(*refs):
        q_ref, k_ref, v_ref, bg_ref, s_ref, tm_ref, do_ref = refs[:7]
        dq_ref, dk_ref, dv_ref, dbg_ref = refs[7 + nx:11 + nx]
        dS_scr = refs[11 + 2 * nx]
        comm_refs = (refs[7:7 + nx], refs[11 + nx:11 + 2 * nx]) + tuple(refs[12 + 2 * nx:])

        @pl.when(pl.program_id(0) == 0)
        def _():
            dS_scr[...] = jnp.zeros_like(dS_scr)
            if nx:
                c_start(*comm_refs)

        lane = lax.broadcasted_iota(jnp.int32, (DNC, 128), 1)
        row = lax.broadcasted_iota(jnp.int32, (GR, 1), 0)

        def chunk(i, carry):
            n = nb - 1 - i
            rows = pl.ds(pl.multiple_of(n * DNC, DNC), DNC)
            mk = _dn_masks()
            ones = jnp.ones((GR, GR), F32)
            blk_f = jnp.where(mk["blk"], 1.0, 0.0)
            wide_f = jnp.where(mk["wide"], 1.0, 0.0)
            per_row = lambda m, a: _mask_dot(m, jnp.broadcast_to(a, (a.shape[0], DH)))[:, :1]
            bgc = bg_ref[rows, :]
            gc_cols = _gc_cols(bgc)
            dbg = jnp.zeros((DNC, 128), F32)
            for g in range(NG):
                c = _dn_group(q_ref, k_ref, v_ref, rows, bgc, gc_cols, g, mk)
                q, k, v, beta = c["q"], c["k"], c["v"], c["beta"]
                dec, eg, ek, egl = c["dec"], c["eg"], c["ek"], c["egl"]
                kb, vb, kbe, A, P, qd, kd = c["kb"], c["vb"], c["kbe"], c["A"], c["P"], c["qd"], c["kd"]
                S = s_ref[n, g]
                Tm = tm_ref[n, g]
                u = _bdot(Tm, vb)
                w = _bdot(Tm, kbe)
                w_wide = _wide(w, mk)
                vn = u - _bdot(w_wide, S)
                d_o = _stack_heads(do_ref, rows, g)
                dS1 = dS_scr[g]
                d_qd = _fold(_bdot(d_o, S, NT), mk)
                dP = jnp.where(mk["causal"], _bdot(d_o, vn, NT), 0.0)
                d_vn = _bdot(P, d_o, TN) + _bdot(_wide(kd, mk), dS1)
                d_kd = _fold(_bdot(vn, dS1, NT), mk)
                d_egl = per_row(wide_f, _lsum(dS1 * S))
                dS_scr[g] = dS1 * c["egl_state"] + _bdot(_wide(qd, mk), d_o, TN) - _bdot(w_wide, d_vn, TN)
                d_w = -_fold(_bdot(d_vn, S, NT), mk)
                d_vb = _bdot(Tm, d_vn, TN)
                d_kbe = _bdot(Tm, d_w, TN)
                dA = jnp.where(mk["strict"], -(_bdot(d_vb, u, NT) + _bdot(d_kbe, w, NT)), 0.0)
                dMA = dA * dec
                dMP = dP * dec
                d_kb = _bdot(dMA, k) + d_kbe * eg
                d_k = _bdot(dMA, kb, TN) + _bdot(dMP, q, TN) + d_kd * ek + d_kb * beta
                d_qs = (_bdot(dMP, k) + d_qd * eg) * (DH ** -0.5)
                d_v = d_vb * beta
                E = dA * A + dP * P
                col_sums = _mask_dot(ones, E, TN)[:, :1]
                t_kd = _lsum(d_kd * kd)
                d_gl = per_row(blk_f, t_kd) + d_egl * egl
                d_gc = (_lsum(E) - col_sums + _lsum(d_qd * qd) + _lsum(d_kbe * kbe) - t_kd
                        + jnp.where(row % DNC == DNC - 1, d_gl, 0.0))
                d_g = per_row(jnp.where(mk["upper"], 1.0, 0.0), d_gc)
                d_beta = _lsum(d_kb * k) + _lsum(d_vb * v)
                for j, h in enumerate(c["heads"]):
                    rs = slice(j * DNC, (j + 1) * DNC)
                    dq_ref[rows, h * DH:(h + 1) * DH] = d_qs[rs]
                    dk_ref[rows, h * DH:(h + 1) * DH] = d_k[rs]
                    dv_ref[rows, h * DH:(h + 1) * DH] = d_v[rs]
                    dbg = dbg + jnp.where(lane == h, d_beta[rs], 0.0) + jnp.where(lane == h + H, d_g[rs], 0.0)
            dbg_ref[rows, :] = dbg
            return carry

        lax.fori_loop(0, nb, chunk, 0)

        if nx:
            @pl.when(pl.program_id(0) == n_blocks - 1)
            def _():
                c_finish(*comm_refs)

    blk = lambda j: pl.BlockSpec((tb, WD), lambda i: (n_blocks - 1 - i, j))
    small = pl.BlockSpec((tb, 128), lambda i: (n_blocks - 1 - i, 0))
    outs = pl.pallas_call(
        body, name=name + ("_" + comm[0] if nx else ""), grid=(n_blocks,),
        in_specs=[blk(0), blk(1), blk(2), small,
                  pl.BlockSpec((nb, NG, GK, DH), lambda i: (n_blocks - 1 - i, 0, 0, 0)),
                  pl.BlockSpec((nb, NG, GR, GR), lambda i: (n_blocks - 1 - i, 0, 0, 0)), blk(0)] + sp["specs"],
        out_specs=[blk(0), blk(0), blk(0), small] + sp["specs"],
        out_shape=[jax.ShapeDtypeStruct((T, WD), F32)] * 3 + [jax.ShapeDtypeStruct((T, 128), F32)] + sp["out_shape"],
        scratch_shapes=[pltpu.VMEM((NG, GK, DH), F32)] + (sp["sems"] if nx else []),
        compiler_params=_cp("arbitrary"),
    )(qkv, qkv, qkv, bg, s_all, tm_all, do, *exchange)
    return outs[0], outs[1], outs[2], outs[3], list(outs[4:])


def _dn_post_fwd(o, proj, gon, name="dn_post_fwd"):
    T = o.shape[0]
    tt = _tile(T, 256, 16)

    def body(o_ref, z_ref, g_ref, y_ref):
        for hh in range(H):
            sl = slice(hh * DH, (hh + 1) * DH)
            ov = o_ref[:, sl]
            zv = z_ref[:, sl].astype(F32)
            r = lax.rsqrt(jnp.mean(ov * ov, axis=-1, keepdims=True) + EPS)
            y_ref[:, sl] = (ov * r * g_ref[...] * (zv * _sig(zv))).astype(y_ref.dtype)

    return pl.pallas_call(
        body, name=name, grid=(T // tt,),
        in_specs=[pl.BlockSpec((tt, WD), lambda i: (i, 0)), pl.BlockSpec((tt, WD), lambda i: (i, OZ // WD)),
                  pl.BlockSpec((1, DH), lambda i: (0, 0))],
        out_specs=pl.BlockSpec((tt, WD), lambda i: (i, 0)),
        out_shape=jax.ShapeDtypeStruct((T, WD), BF16), compiler_params=_cp("parallel"),
    )(o, proj, gon)


def _dn_post_bwd(o, proj, gon, dy, name="dn_post_bwd"):
    T = o.shape[0]
    tt = _tile(T, 256, 16)

    def body(o_ref, z_ref, g_ref, dy_ref, do_ref, dz_ref, dg_ref):
        @pl.when(pl.program_id(0) == 0)
        def _():
            dg_ref[...] = jnp.zeros_like(dg_ref)

        acc = jnp.zeros((1, DH), F32)
        for hh in range(H):
            sl = slice(hh * DH, (hh + 1) * DH)
            ov = o_ref[:, sl]
            zv = z_ref[:, sl].astype(F32)
            dyv = dy_ref[:, sl]
            r = lax.rsqrt(jnp.mean(ov * ov, axis=-1, keepdims=True) + EPS)
            oh = ov * r
            nrm = oh * g_ref[...]
            gate, gate_grad = _silu_and_grad(zv)
            dn = dyv * gate
            dz_ref[:, sl] = (dyv * nrm * gate_grad).astype(dz_ref.dtype)
            doh = dn * g_ref[...]
            do_ref[:, sl] = r * (doh - oh * jnp.mean(doh * oh, axis=-1, keepdims=True))
            acc = acc + jnp.sum(dn * oh, axis=0, keepdims=True)
        dg_ref[...] += acc

    row = pl.BlockSpec((tt, WD), lambda i: (i, 0))
    vec = pl.BlockSpec((1, DH), lambda i: (0, 0))
    return pl.pallas_call(
        body, name=name, grid=(T // tt,),
        in_specs=[row, pl.BlockSpec((tt, WD), lambda i: (i, OZ // WD)), vec, row],
        out_specs=[row, row, vec],
        out_shape=[jax.ShapeDtypeStruct((T, WD), F32), jax.ShapeDtypeStruct((T, WD), BF16),
                   jax.ShapeDtypeStruct((1, DH), F32)],
        compiler_params=_cp("arbitrary"),
    )(o, proj, gon, dy)


def _sg_common(u_ref, v_ref, lng_ref, lnb_ref, with_grad=True):
    ur = u_ref[...].astype(F32)
    vr = v_ref[...].astype(F32)
    vgel, vgel_grad = _gelu_and_grad(vr) if with_grad else (_gelu(vr), None)
    mu = jnp.mean(vgel, axis=-1, keepdims=True)
    xc = vgel - mu
    rs = lax.rsqrt(jnp.mean(xc * xc, axis=-1, keepdims=True) + EPS)
    xh = xc * rs
    vg = xh * lng_ref[...] + lnb_ref[...]
    return ur, vgel_grad, rs, xh, vg


def _sg_fwd(proj, lng, lnb, sgw, sgbt, name="sg_fwd"):
    T = proj.shape[0]

    def body(u_ref, v_ref, lng_ref, lnb_ref, w_ref, bt_ref, y_ref):
        ur, _, _, _, vg = _sg_common(u_ref, v_ref, lng_ref, lnb_ref, with_grad=False)
        ri = lax.broadcasted_iota(jnp.int32, (SGC, SGC), 0)
        ci = lax.broadcasted_iota(jnp.int32, (SGC, SGC), 1)
        ug = _gelu(ur)
        for g in range(H):
            sl = slice(g * DH, (g + 1) * DH)
            ws = jnp.where(ri >= ci, w_ref[g], 0.0)
            mixed = _bdot(ws, vg[:, sl]) + bt_ref[:, g:g + 1]
            y_ref[:, sl] = (ug[:, sl] * mixed).astype(y_ref.dtype)

    vec = pl.BlockSpec((1, WD), lambda i: (0, 0))
    return pl.pallas_call(
        body, name=name, grid=(T // SGC,),
        in_specs=[pl.BlockSpec((SGC, WD), lambda i: (i, OU // WD)), pl.BlockSpec((SGC, WD), lambda i: (i, OV // WD)),
                  vec, vec, pl.BlockSpec((H, SGC, SGC), lambda i: (0, 0, 0)),
                  pl.BlockSpec((SGC, H), lambda i: (0, 0))],
        out_specs=pl.BlockSpec((SGC, WD), lambda i: (i, 0)),
        out_shape=jax.ShapeDtypeStruct((T, WD), BF16), compiler_params=_cp("parallel"),
    )(proj, proj, lng, lnb, sgw, sgbt)


def _sg_bwd(proj, lng, lnb, sgw, sgbt, dy, name="sg_bwd"):
    T = proj.shape[0]

    def body(u_ref, v_ref, lng_ref, lnb_ref, w_ref, bt_ref, dy_ref,
             du_ref, dv_ref, dw_ref, dbt_ref, dlng_ref, dlnb_ref):
        @pl.when(pl.program_id(0) == 0)
        def _():
            dw_ref[...] = jnp.zeros_like(dw_ref)
            dbt_ref[...] = jnp.zeros_like(dbt_ref)
            dlng_ref[...] = jnp.zeros_like(dlng_ref)
            dlnb_ref[...] = jnp.zeros_like(dlnb_ref)

        ur, vgel_grad, rs, xh, vg = _sg_common(u_ref, v_ref, lng_ref, lnb_ref)
        ri = lax.broadcasted_iota(jnp.int32, (SGC, SGC), 0)
        ci = lax.broadcasted_iota(jnp.int32, (SGC, SGC), 1)
        ug, ug_grad = _gelu_and_grad(ur)
        dyv = dy_ref[...]
        dbt = jnp.zeros((SGC, 128), F32)
        dvg_parts = []
        for g in range(H):
            sl = slice(g * DH, (g + 1) * DH)
            ws = jnp.where(ri >= ci, w_ref[g], 0.0)
            mixed = _bdot(ws, vg[:, sl]) + bt_ref[:, g:g + 1]
            dyg = dyv[:, sl]
            du_ref[:, sl] = (dyg * mixed * ug_grad[:, sl]).astype(du_ref.dtype)
            dmix = dyg * ug[:, sl]
            dw_ref[g] += jnp.where(ri >= ci, _bdot(dmix, vg[:, sl], NT), 0.0)
            dbt = dbt + jnp.where(ci == g, _lsum(dmix), 0.0)
            dvg_parts.append(_bdot(ws, dmix, TN))
        dbt_ref[...] += dbt
        dvg = jnp.concatenate(dvg_parts, axis=1)
        dlng_ref[...] += jnp.sum(dvg * xh, axis=0, keepdims=True)
        dlnb_ref[...] += jnp.sum(dvg, axis=0, keepdims=True)
        dxh = dvg * lng_ref[...]
        dvgel = rs * (dxh - jnp.mean(dxh, axis=-1, keepdims=True) - xh * jnp.mean(dxh * xh, axis=-1, keepdims=True))
        dv_ref[...] = (dvgel * vgel_grad).astype(dv_ref.dtype)

    vec = pl.BlockSpec((1, WD), lambda i: (0, 0))
    row = pl.BlockSpec((SGC, WD), lambda i: (i, 0))
    return pl.pallas_call(
        body, name=name, grid=(T // SGC,),
        in_specs=[pl.BlockSpec((SGC, WD), lambda i: (i, OU // WD)), pl.BlockSpec((SGC, WD), lambda i: (i, OV // WD)),
                  vec, vec, pl.BlockSpec((H, SGC, SGC), lambda i: (0, 0, 0)),
                  pl.BlockSpec((SGC, H), lambda i: (0, 0)), row],
        out_specs=[row, row, pl.BlockSpec((H, SGC, SGC), lambda i: (0, 0, 0)),
                   pl.BlockSpec((SGC, 128), lambda i: (0, 0)), vec, vec],
        out_shape=[jax.ShapeDtypeStruct((T, WD), BF16), jax.ShapeDtypeStruct((T, WD), BF16),
                   jax.ShapeDtypeStruct((H, SGC, SGC), F32), jax.ShapeDtypeStruct((SGC, 128), F32),
                   jax.ShapeDtypeStruct((1, WD), F32), jax.ShapeDtypeStruct((1, WD), F32)],
        compiler_params=_cp("arbitrary"),
    )(proj, proj, lng, lnb, sgw, sgbt, dy)


def _merge_fwd(proj, yap, ybp, D, name="merge_fwd"):
    T = proj.shape[0]
    tt = _tile(T, 256, 16)

    def body(ga_ref, gb_ref, a_ref, b_ref, o_ref):
        ga, gb, a, b = [r[...].astype(F32) for r in (ga_ref, gb_ref, a_ref, b_ref)]
        o_ref[...] = (_sig(ga) * a + _sig(gb) * b).astype(o_ref.dtype)

    row = pl.BlockSpec((tt, D), lambda i: (i, 0))
    return pl.pallas_call(
        body, name=name, grid=(T // tt,),
        in_specs=[pl.BlockSpec((tt, D), lambda i: (i, OGA // D)), pl.BlockSpec((tt, D), lambda i: (i, OGA // D + 1)),
                  row, row],
        out_specs=row, out_shape=jax.ShapeDtypeStruct((T, D), BF16), compiler_params=_cp("parallel"),
    )(proj, proj, yap, ybp)


def _merge_bwd(proj, yap, ybp, dm, D, name="merge_bwd"):
    T = proj.shape[0]
    tt = _tile(T, 256, 16)

    def body(ga_ref, gb_ref, a_ref, b_ref, dm_ref, da_ref, db_ref, dga_ref, dgb_ref):
        d, ga, gb, a, b = [r[...].astype(F32) for r in (dm_ref, ga_ref, gb_ref, a_ref, b_ref)]
        sa = _sig(ga)
        sb = _sig(gb)
        da_ref[...] = (d * sa).astype(da_ref.dtype)
        db_ref[...] = (d * sb).astype(db_ref.dtype)
        dga_ref[...] = (d * a * sa * (1.0 - sa)).astype(dga_ref.dtype)
        dgb_ref[...] = (d * b * sb * (1.0 - sb)).astype(dgb_ref.dtype)

    row = pl.BlockSpec((tt, D), lambda i: (i, 0))
    return pl.pallas_call(
        body, name=name, grid=(T // tt,),
        in_specs=[pl.BlockSpec((tt, D), lambda i: (i, OGA // D)), pl.BlockSpec((tt, D), lambda i: (i, OGA // D + 1)),
                  row, row, row],
        out_specs=[row] * 4, out_shape=[jax.ShapeDtypeStruct((T, D), BF16)] * 4,
        compiler_params=_cp("parallel"),
    )(proj, proj, yap, ybp, dm)


def _ffn_act_fwd(gp, up, cw, cb, name="ffn_act_fwd"):
    T, F = gp.shape

    def body(g_ref, u_ref, w_ref, b_ref, o_ref):
        gv = g_ref[...].astype(F32)
        w = w_ref[...]
        c = gv * w[FF_K - 1:FF_K, :] + b_ref[...]
        for k in range(1, FF_K):
            c = c + _shift_down(gv, k) * w[FF_K - 1 - k:FF_K - k, :]
        o_ref[...] = (c * _sig(c) * u_ref[...].astype(F32)).astype(o_ref.dtype)

    col = pl.BlockSpec((T, 128), lambda j: (0, j))
    return pl.pallas_call(
        body, name=name, grid=(F // 128,),
        in_specs=[col, col, pl.BlockSpec((FF_K, 128), lambda j: (0, j)), pl.BlockSpec((1, 128), lambda j: (0, j))],
        out_specs=col, out_shape=jax.ShapeDtypeStruct((T, F), BF16), compiler_params=_cp("parallel"),
    )(gp, up, cw, cb)


def _ffn_act_bwd(gp, up, cw, cb, dact, name="ffn_act_bwd"):
    T, F = gp.shape

    def body(g_ref, u_ref, w_ref, b_ref, d_ref, dg_ref, du_ref, dw_ref, db_ref):
        gv = g_ref[...].astype(F32)
        w = w_ref[...]
        shifted = [_shift_down(gv, k) for k in range(FF_K)]
        c = shifted[0] * w[FF_K - 1:FF_K, :] + b_ref[...]
        for k in range(1, FF_K):
            c = c + shifted[k] * w[FF_K - 1 - k:FF_K - k, :]
        d = d_ref[...].astype(F32)
        act, act_grad = _silu_and_grad(c)
        du_ref[...] = (d * act).astype(du_ref.dtype)
        dc = d * u_ref[...].astype(F32) * act_grad
        dg = dc * w[FF_K - 1:FF_K, :]
        for k in range(1, FF_K):
            dg = dg + _shift_up(dc, k) * w[FF_K - 1 - k:FF_K - k, :]
        dg_ref[...] = dg.astype(dg_ref.dtype)
        rows = [jnp.sum(dc * shifted[FF_K - 1 - t], axis=0, keepdims=True) for t in range(FF_K)]
        dw_ref[...] = jnp.concatenate(rows, axis=0)
        db_ref[...] = jnp.sum(dc, axis=0, keepdims=True)

    col = pl.BlockSpec((T, 128), lambda j: (0, j))
    wspec = pl.BlockSpec((FF_K, 128), lambda j: (0, j))
    bspec = pl.BlockSpec((1, 128), lambda j: (0, j))
    return pl.pallas_call(
        body, name=name, grid=(F // 128,),
        in_specs=[col, col, wspec, bspec, col], out_specs=[col, col, wspec, bspec],
        out_shape=[jax.ShapeDtypeStruct((T, F), BF16), jax.ShapeDtypeStruct((T, F), BF16),
                   jax.ShapeDtypeStruct((FF_K, F), F32), jax.ShapeDtypeStruct((1, F), F32)],
        compiler_params=_cp("parallel"),
    )(gp, up, cw, cb, dact)


class _Carrier:
    def __init__(self, plan=None, deliver=None):
        self.plan, self.deliver = plan or (lambda kernel: None), deliver

    def run(self, kernel, fn, **kw):
        comm = self.plan(kernel)
        out = fn(comm=comm, **kw)
        if comm:
            self.deliver(kernel, out[-1])
            out = out[:-1]
            return out[0] if len(out) == 1 else out
        return out


def _layer_fwd(x, w, carrier=None):
    cr = carrier or _Carrier()
    D = x.shape[1]
    oba = OGA + 2 * D
    h = _rms_fwd(x, w["norm1_g"], "rms1_fwd")
    proj = cr.run("proj", functools.partial(_mm, h, w["w_in_t"], "nt", BF16, name="mm_proj"))
    bg = _ba_fwd(proj, w["alog_row"], w["dtb_row"], oba)
    qkv = _dn_prep_fwd(proj, w["dn_conv_w"])
    r = cr.run("dn_core", functools.partial(_dn_core_fwd, qkv, bg))
    o, s_all, tm_all = r[0], r[1], r[2]
    ya = _dn_post_fwd(o, proj, w["dn_onorm_g"])
    yb = _sg_fwd(proj, w["sg_ln_g"], w["sg_ln_b"], w["sg_w"], w["sg_bt"])
    yap = _mm(ya, w["w_branch_a"], "nn", BF16, name="mm_branch")
    ybp = _mm(yb, w["w_branch_b"], "nn", BF16, name="mm_branch")
    merged = _merge_fwd(proj, yap, ybp, D)
    x1 = _mm(merged, w["w_out"], "nn", F32, add=x, name="mm_out")
    h2 = _rms_fwd(x1, w["norm2_g"], "rms2_fwd")
    gp = cr.run("ffn_gate", functools.partial(_mm, h2, w["ffn_w_gate"], "nn", BF16, name="mm_ffn_in"))
    up = cr.run("ffn_up", functools.partial(_mm, h2, w["ffn_w_up"], "nn", BF16, name="mm_ffn_in"))
    act = _ffn_act_fwd(gp, up, w["ffn_conv_w"], w["ffn_conv_b"])
    x2 = cr.run("ffn_down", functools.partial(_mm, act, w["ffn_w_down"], "nn", F32, add=x1, name="mm_ffn_down"))
    saved = dict(x=x, h=h, proj=proj, bg=bg, qkv=qkv, o=o, s_all=s_all, tm_all=tm_all, ya=ya, yb=yb, yap=yap,
                 ybp=ybp, merged=merged, x1=x1, h2=h2, gp=gp, up=up, act=act)
    return x2, saved


def _layer_bwd(dx2, w, s, carrier=None, ffn_grads_ready=None, rest_grads_ready=None):
    cr = carrier or _Carrier()
    D = dx2.shape[1]
    oba = OGA + 2 * D
    g = {}
    dx2b = dx2.astype(BF16)
    dact = cr.run("d_act", functools.partial(_mm, dx2b, w["ffn_w_down"], "nt", BF16, name="mm_d_act"))
    g["ffn_w_down"] = _mm(s["act"], dx2b, "tn", BF16, name="mm_dw_down")
    dgp, dup, g["ffn_conv_w"], g["ffn_conv_b"] = _ffn_act_bwd(s["gp"], s["up"], w["ffn_conv_w"], w["ffn_conv_b"], dact)
    dh2 = _mm(dgp, w["ffn_w_gate"], "nt", F32, name="mm_dh2")
    dh2 = _mm(dup, w["ffn_w_up"], "nt", F32, add=dh2, name="mm_dh2_acc")
    g["ffn_w_gate"] = _mm(s["h2"], dgp, "tn", BF16, name="mm_dw_ffn_in")
    g["ffn_w_up"] = _mm(s["h2"], dup, "tn", BF16, name="mm_dw_ffn_in")
    if ffn_grads_ready:
        ffn_grads_ready(g)
    dx1, g["norm2_g"] = _rms_bwd(s["x1"], w["norm2_g"], dh2, dx2, "rms2_bwd")
    dx1b = dx1.astype(BF16)
    dm = _mm(dx1b, w["w_out"], "nt", BF16, name="mm_d_merged")
    g["w_out"] = _mm(s["merged"], dx1b, "tn", BF16, name="mm_dw_out")
    dyap, dybp, dga, dgb = _merge_bwd(s["proj"], s["yap"], s["ybp"], dm, D)
    dya = _mm(dyap, w["w_branch_a"], "nt", F32, name="mm_d_branch")
    dyb = _mm(dybp, w["w_branch_b"], "nt", F32, name="mm_d_branch")
    g["w_branch_a"] = _mm(s["ya"], dyap, "tn", BF16, name="mm_dw_branch")
    g["w_branch_b"] = _mm(s["yb"], dybp, "tn", BF16, name="mm_dw_branch")
    du, dv, g["sg_w"], dbt, g["sg_ln_g"], g["sg_ln_b"] = _sg_bwd(
        s["proj"], w["sg_ln_g"], w["sg_ln_b"], w["sg_w"], w["sg_bt"], dyb)
    g["sg_b"] = jnp.transpose(dbt[:, :H])
    do, dz, g["dn_onorm_g"] = _dn_post_bwd(s["o"], s["proj"], w["dn_onorm_g"], dya)
    r = cr.run("dn_core", functools.partial(_dn_core_bwd, s["qkv"], s["bg"], s["s_all"], s["tm_all"], do))
    dq, dk, dvv, dbg = r[0], r[1], r[2], r[3]
    dqkv, g["dn_conv_w"] = _dn_prep_bwd(s["proj"], w["dn_conv_w"], dq, dk, dvv)
    dba, dal, ddt = _ba_bwd(s["proj"], w["alog_row"], w["dtb_row"], dbg, oba)
    g["dn_a_log"] = dal[0, H:2 * H]
    g["dn_dt_bias"] = ddt[0, H:2 * H]
    dproj = jnp.concatenate([dqkv, dz, du, dv, dga, dgb, dba], axis=1)
    g["w_in_t"] = cr.run("dw_in", functools.partial(_mm, dproj, s["h"], "tn", BF16, name="mm_dw_in"))
    if rest_grads_ready:
        rest_grads_ready(g)
    dh = cr.run("dh", functools.partial(_mm, dproj, w["w_in_t"], "nn", F32, name="mm_dh"))
    dx, g["norm1_g"] = _rms_bwd(s["x"], w["norm1_g"], dh, dx1, "rms1_bwd")
    return dx, g


def _local_step(x, tgt, layers, final_g):
    saved = []
    for w in layers:
        x, s = _layer_fwd(x, w)
        saved.append(s)
    dx, dgf, loss = _loss_head(x, final_g, tgt)
    grads = [None] * len(layers)
    for l in reversed(range(len(layers))):
        dx, grads[l] = _layer_bwd(dx, layers[l], saved[l])
    return loss[0, 0], dx, grads, dgf


def _w_in_pad(wt):
    c1 = 4 * WD
    return jnp.concatenate([wt[:c1], wt[c1 + 2 * H:], wt[c1:c1 + 2 * H],
                            jnp.zeros((128 - 2 * H, wt.shape[1]), wt.dtype)], axis=0)


def _w_in_unpad(gt):
    c1 = 4 * WD
    n = gt.shape[0] - 128
    return jnp.concatenate([gt[:c1], gt[n:n + 2 * H], gt[c1:n]], axis=0)


def _row128(v, off):
    return jnp.pad(v, (off, 128 - off - v.shape[0]))[None]


def _prep_small(p):
    return dict(
        norm1_g=p["norm1_g"][None], alog_row=_row128(p["dn_a_log"], H), dtb_row=_row128(p["dn_dt_bias"], H),
        dn_conv_w=p["dn_conv_w"], dn_onorm_g=p["dn_onorm_g"][None],
        sg_ln_g=p["sg_ln_g"][None], sg_ln_b=p["sg_ln_b"][None], sg_w=p["sg_w"], sg_bt=jnp.transpose(p["sg_b"]),
        norm2_g=p["norm2_g"][None], ffn_conv_w=p["ffn_conv_w"], ffn_conv_b=p["ffn_conv_b"][None])


def _prep_layer(p):
    return dict(_prep_small(p), w_in_t=_w_in_pad(p["w_in_t"]),
                **{n: p[n] for n in ("w_branch_a", "w_branch_b", "w_out", "ffn_w_gate", "ffn_w_up", "ffn_w_down")})


HBM_SPEC = pl.BlockSpec(memory_space=pltpu.HBM)


def _coords():
    return lax.axis_index("x"), lax.axis_index("y"), lax.axis_index("c")


def _other_chips(x, y):
    return [(1 - x, y), (x, 1 - y), (1 - x, 1 - y)]


def _remote(src, dst, send_sems, recv_sems, k, dev):
    return pltpu.make_async_remote_copy(src_ref=src, dst_ref=dst, send_sem=send_sems.at[k], recv_sem=recv_sems.at[k],
                                        device_id=dev, device_id_type=MESH)


def _ag_copies(w_refs, o_refs, send_sems, recv_sems):
    x, y, c = _coords()
    me = 2 * x + y
    chips = _other_chips(x, y)

    def ici(k, j, owner):
        chip = chips[j]
        return _remote(w_refs[k].at[c], o_refs[k].at[owner, c], send_sems, recv_sems, 6 * k + j, (chip[0], chip[1], c))

    def d2d(k, j, part):
        owner = 2 * chips[j][0] + chips[j][1]
        return _remote(o_refs[k].at[owner, part], o_refs[k].at[owner, part], send_sems, recv_sems, 6 * k + 3 + j,
                       (x, y, 1 - c))

    n = len(w_refs)
    return me, c, chips, ici, d2d, [(k, j) for k in range(n) for j in range(3)]


def _ag_start(w_refs, o_refs, send_sems, recv_sems):
    me, _, _, ici, _, pairs = _ag_copies(w_refs, o_refs, send_sems, recv_sems)
    for k, j in pairs:
        ici(k, j, me).start()


def _ag_finish(w_refs, o_refs, send_sems, recv_sems):
    me, c, chips, ici, d2d, pairs = _ag_copies(w_refs, o_refs, send_sems, recv_sems)
    for k, j in pairs:
        ici(k, j, 2 * chips[j][0] + chips[j][1]).wait_recv()
        d2d(k, j, c).start()
    for k, j in pairs:
        d2d(k, j, 1 - c).wait_recv()
    for k, j in pairs:
        ici(k, j, me).wait_send()
        d2d(k, j, c).wait_send()


def _ag_specs(ws):
    n = len(ws)
    return dict(out_shape=[jax.ShapeDtypeStruct((N_CHIPS,) + w.shape, w.dtype) for w in ws],
                specs=[HBM_SPEC] * n, sems=[pltpu.SemaphoreType.DMA((6 * n,)), pltpu.SemaphoreType.DMA((6 * n,))])


def _ag_layers(ws):
    n = len(ws)

    def body(*refs):
        _ag_start(refs[:n], refs[n:2 * n], *refs[2 * n:])
        _ag_finish(refs[:n], refs[n:2 * n], *refs[2 * n:])

    sp = _ag_specs(ws)
    return pl.pallas_call(
        body, name="ag_weights", out_shape=sp["out_shape"], in_specs=sp["specs"], out_specs=sp["specs"],
        scratch_shapes=sp["sems"],
    )(*ws)


def _rs_pair_exchange(Gs):
    n = len(Gs)

    def body(*refs):
        g_refs, b_refs = refs[:n], refs[n:2 * n]
        send_sems, recv_sems = refs[2 * n:]
        x, y, c = _coords()
        cps = [_remote(g_refs[k].at[i, 1 - c], b_refs[k].at[i], send_sems, recv_sems, N_CHIPS * k + i, (x, y, 1 - c))
               for k in range(n) for i in range(N_CHIPS)]
        for cp in cps:
            cp.start()
        for cp in cps:
            cp.wait()

    return pl.pallas_call(
        body, name="rs_pair_exchange",
        out_shape=[jax.ShapeDtypeStruct((N_CHIPS,) + g.shape[2:], g.dtype) for g in Gs],
        in_specs=[HBM_SPEC] * n, out_specs=[HBM_SPEC] * n,
        scratch_shapes=[pltpu.SemaphoreType.DMA((N_CHIPS * n,)), pltpu.SemaphoreType.DMA((N_CHIPS * n,))],
    )(*Gs)


def _rs_add_pair(G, B, c, name):
    _, _, R, C = G.shape
    tr = _tile(R, 256, 16)

    def body(c_ref, g_ref, b_ref, o_ref):
        o_ref[0] = (g_ref[0, 0].astype(F32) + b_ref[0].astype(F32)).astype(o_ref.dtype)

    grid_spec = pltpu.PrefetchScalarGridSpec(
        num_scalar_prefetch=1, grid=(N_CHIPS, R // tr),
        in_specs=[pl.BlockSpec((1, 1, tr, C), lambda i, r, c_ref: (i, c_ref[0], r, 0)),
                  pl.BlockSpec((1, tr, C), lambda i, r, c_ref: (i, r, 0))],
        out_specs=pl.BlockSpec((1, tr, C), lambda i, r, c_ref: (i, r, 0)))
    return pl.pallas_call(
        body, name=name, grid_spec=grid_spec, out_shape=jax.ShapeDtypeStruct((N_CHIPS, R, C), G.dtype),
        compiler_params=_cp("parallel", "parallel"),
    )(jnp.reshape(c, (1,)).astype(jnp.int32), G, B)


def _rs_chip_exchange(Ps):
    n = len(Ps)

    def body(*refs):
        _rsx_start(refs[:n], refs[n:2 * n], *refs[2 * n:])
        _rsx_finish(refs[:n], refs[n:2 * n], *refs[2 * n:])

    sp = _rsx_specs(Ps)
    return pl.pallas_call(
        body, name="rs_chip_exchange", out_shape=sp["out_shape"], in_specs=sp["specs"], out_specs=sp["specs"],
        scratch_shapes=sp["sems"],
    )(*Ps)


def _rsx_copies(p_refs, b_refs, send_sems, recv_sems):
    x, y, c = _coords()
    me = 2 * x + y
    chips = _other_chips(x, y)

    def cp(k, j, src_slot, dst_slot):
        return _remote(p_refs[k].at[src_slot], b_refs[k].at[dst_slot], send_sems, recv_sems, 3 * k + j,
                       (chips[j][0], chips[j][1], c))

    return me, chips, cp, [(k, j) for k in range(len(p_refs)) for j in range(3)]


def _rsx_start(p_refs, b_refs, send_sems, recv_sems):
    me, chips, cp, pairs = _rsx_copies(p_refs, b_refs, send_sems, recv_sems)
    for k, j in pairs:
        cp(k, j, 2 * chips[j][0] + chips[j][1], me).start()


def _rsx_finish(p_refs, b_refs, send_sems, recv_sems):
    me, chips, cp, pairs = _rsx_copies(p_refs, b_refs, send_sems, recv_sems)
    for k, j in pairs:
        owner = 2 * chips[j][0] + chips[j][1]
        cp(k, j, owner, owner).wait_recv()
    for k, j in pairs:
        cp(k, j, 2 * chips[j][0] + chips[j][1], me).wait_send()


def _rsx_specs(Ps):
    n = len(Ps)
    return dict(out_shape=[jax.ShapeDtypeStruct(p.shape, p.dtype) for p in Ps], specs=[HBM_SPEC] * n,
                sems=[pltpu.SemaphoreType.DMA((3 * n,)), pltpu.SemaphoreType.DMA((3 * n,))])


def _rs_sum_chips(P, B, me, name):
    _, R, C = P.shape
    tr = _tile(R, 256, 16)

    def body(me_ref, p_ref, b1_ref, b2_ref, b3_ref, o_ref):
        o_ref[...] = ((p_ref[0].astype(F32) + b1_ref[0].astype(F32)) + b2_ref[0].astype(F32)) + b3_ref[0].astype(F32)

    slot = lambda d: pl.BlockSpec((1, tr, C), lambda r, me_ref: ((me_ref[0] + d) % N_CHIPS, r, 0))
    grid_spec = pltpu.PrefetchScalarGridSpec(
        num_scalar_prefetch=1, grid=(R // tr,), in_specs=[slot(0), slot(1), slot(2), slot(3)],
        out_specs=pl.BlockSpec((tr, C), lambda r, me_ref: (r, 0)))
    return pl.pallas_call(
        body, name=name, grid_spec=grid_spec, out_shape=jax.ShapeDtypeStruct((R, C), F32),
        compiler_params=_cp("parallel"),
    )(jnp.reshape(me, (1,)).astype(jnp.int32), P, B, B, B)


def _sum_slots(B, name):
    S, R, C = B.shape
    tr = _tile(R, 256, 16)

    def body(b_ref, o_ref):
        acc = b_ref[0].astype(F32)
        for i in range(1, S):
            acc = acc + b_ref[i].astype(F32)
        o_ref[...] = acc

    return pl.pallas_call(
        body, name=name, grid=(R // tr,), in_specs=[pl.BlockSpec((S, tr, C), lambda r: (0, r, 0))],
        out_specs=pl.BlockSpec((tr, C), lambda r: (r, 0)), out_shape=jax.ShapeDtypeStruct((R, C), F32),
        compiler_params=_cp("parallel"),
    )(B)


def _rs_pair_swap(Rs):
    n = len(Rs)

    def body(*refs):
        r_refs, o_refs = refs[:n], refs[n:2 * n]
        send_sems, recv_sems = refs[2 * n:]
        x, y, c = _coords()
        cps = [_remote(r_refs[k], o_refs[k], send_sems, recv_sems, k, (x, y, 1 - c)) for k in range(n)]
        for cp in cps:
            cp.start()
        for cp in cps:
            cp.wait()

    return pl.pallas_call(
        body, name="rs_pair_swap", out_shape=[jax.ShapeDtypeStruct(r.shape, r.dtype) for r in Rs],
        in_specs=[HBM_SPEC] * n, out_specs=[HBM_SPEC] * n,
        scratch_shapes=[pltpu.SemaphoreType.DMA((n,)), pltpu.SemaphoreType.DMA((n,))],
    )(*Rs)


def _ag8(v):
    R = v.shape[0]

    def body(v_ref, out_ref, send_sems, recv_sems, local_sem):
        x, y, c = _coords()
        me, sib = (x, y, c), (x, y, 1 - c)
        chips = _other_chips(x, y)

        def slot(p):
            return out_ref.at[4 * p[0] + 2 * p[1] + p[2]]

        def copy(k, block, to, src=None):
            return _remote(slot(block) if src is None else src, slot(block), send_sems, recv_sems, k, to)

        mine = pltpu.make_async_copy(v_ref, slot(me), local_sem)
        mine.start()
        first = [copy(0, me, sib, src=v_ref)]
        first += [copy(1 + j, me, (chip[0], chip[1], c), src=v_ref) for j, chip in enumerate(chips)]
        for cp in first:
            cp.start()
        passed = [copy(4 + j, (chip[0], chip[1], c), sib) for j, chip in enumerate(chips)]
        for j, chip in enumerate(chips):
            copy(1 + j, (chip[0], chip[1], c), me).wait_recv()
            passed[j].start()
        copy(0, sib, me).wait_recv()
        for j, chip in enumerate(chips):
            copy(4 + j, (chip[0], chip[1], 1 - c), me).wait_recv()
        for cp in first + passed:
            cp.wait_send()
        mine.wait()

    return pl.pallas_call(
        body, name="ag8_small", out_shape=jax.ShapeDtypeStruct((8, R, 128), v.dtype),
        in_specs=[pl.BlockSpec(memory_space=pltpu.VMEM)], out_specs=pl.BlockSpec(memory_space=pltpu.VMEM),
        scratch_shapes=[pltpu.SemaphoreType.DMA((7,)), pltpu.SemaphoreType.DMA((7,)), pltpu.SemaphoreType.DMA],
        compiler_params=pltpu.CompilerParams(vmem_limit_bytes=VMEM_LIMIT),
    )(v)


def _adamw(w, g, m, v, name):
    L, R, C = w.shape
    rows = [R] + [t for t in range(8, min(R, 1024) + 1, 8) if R % t == 0]
    cols = [C] + [t for t in range(128, C, 128) if C % t == 0]
    lead = [t for t in range(1, L + 1) if L % t == 0]
    fits = [(a * r * c, c, r, a) for a in lead for r in rows for c in cols if a * r * c * 4 <= 3 << 19]
    _, tc, tr, tl = max(fits) if fits else (0, min(cols), min(rows), 1)

    def body(w_ref, g_ref, m_ref, v_ref, d_ref, mo_ref, vo_ref):
        gv = g_ref[...]
        m2 = ADAM_B1 * m_ref[...] + (1.0 - ADAM_B1) * gv
        v2 = ADAM_B2 * v_ref[...] + (1.0 - ADAM_B2) * jnp.square(gv)
        m_hat = m2 / (1.0 - ADAM_B1 ** ADAM_STEP)
        v_hat = v2 / (1.0 - ADAM_B2 ** ADAM_STEP)
        d_ref[...] = -ADAM_LR * (m_hat / (jnp.sqrt(v_hat) + ADAM_EPS) + ADAM_WD * w_ref[...])
        mo_ref[...] = m2
        vo_ref[...] = v2

    blk = pl.BlockSpec((tl, tr, tc), lambda l, r, j: (l, r, j))
    return pl.pallas_call(
        body, name=name, grid=(L // tl, R // tr, C // tc), in_specs=[blk] * 4, out_specs=[blk] * 3,
        out_shape=[jax.ShapeDtypeStruct(w.shape, F32)] * 3,
        compiler_params=_cp("parallel", "parallel", "parallel"),
    )(w, g, m, v)


def _adamw_halves(w, g_mine, g_other, c, m, v, name):
    L, _, R, C = w.shape
    tr = _tile(R, 128, 8)

    def body(c_ref, w_ref, *rest):
        g_refs = rest[:2 * L]
        m_ref, v_ref, g_ref, d_ref, mo_ref, vo_ref = rest[2 * L:]
        l, h = pl.program_id(0), pl.program_id(1)
        gm, go = g_refs[0][...], g_refs[L][...]
        for i in range(1, L):
            gm = jnp.where(l == i, g_refs[i][...], gm)
            go = jnp.where(l == i, g_refs[L + i][...], go)
        gv = jnp.where(h == c_ref[0], gm, go)[None, None]
        g_ref[...] = gv
        m2 = ADAM_B1 * m_ref[...] + (1.0 - ADAM_B1) * gv
        v2 = ADAM_B2 * v_ref[...] + (1.0 - ADAM_B2) * jnp.square(gv)
        m_hat = m2 / (1.0 - ADAM_B1 ** ADAM_STEP)
        v_hat = v2 / (1.0 - ADAM_B2 ** ADAM_STEP)
        d_ref[...] = -ADAM_LR * (m_hat / (jnp.sqrt(v_hat) + ADAM_EPS) + ADAM_WD * w_ref[...])
        mo_ref[...] = m2
        vo_ref[...] = v2

    blk = pl.BlockSpec((1, 1, tr, C), lambda l, h, r, c_ref: (l, h, r, 0))

    def gblk(i, mine):
        def index(l, h, r, c_ref):
            use = jnp.logical_and(l == i, (h == c_ref[0]) == mine)
            return (jnp.where(use, r, 0), 0)
        return pl.BlockSpec((tr, C), index)

    grid_spec = pltpu.PrefetchScalarGridSpec(
        num_scalar_prefetch=1, grid=(L, 2, R // tr),
        in_specs=[blk] + [gblk(i, True) for i in range(L)] + [gblk(i, False) for i in range(L)] + [blk, blk],
        out_specs=[blk] * 4)
    return pl.pallas_call(
        body, name=name, grid_spec=grid_spec, out_shape=[jax.ShapeDtypeStruct(w.shape, F32)] * 4,
        compiler_params=_cp("parallel", "parallel", "parallel"),
    )(jnp.reshape(c, (1,)).astype(jnp.int32), w, *g_mine, *g_other, m, v)


BIG = ("w_in", "w_branch_a", "w_branch_b", "w_out", "ffn_w_gate", "ffn_w_up", "ffn_w_down")
ROW_SHARDED = ("w_out", "ffn_w_down")
SMALL = ("norm1_g", "dn_conv_w", "dn_a_log", "dn_dt_bias", "dn_onorm_g", "sg_ln_g", "sg_ln_b", "sg_w", "sg_b",
         "norm2_g", "ffn_conv_w", "ffn_conv_b", "final_norm_g")
SMALL_SHARDED = ("dn_conv_w", "ffn_conv_w")


def _pack_rows(arrs, mult):
    flat = jnp.concatenate([jnp.reshape(a, (-1,)) for a in arrs])
    n = flat.shape[0]
    rows = -(-n // (128 * mult)) * mult
    return jnp.reshape(jnp.pad(flat, (0, rows * 128 - n)), (rows, 128))


def _unpack(flat2d, shapes):
    flat = jnp.reshape(flat2d, (-1,))
    out, off = [], 0
    for shp in shapes:
        n = math.prod(shp)
        out.append(jnp.reshape(flat[off:off + n], shp))
        off += n
    return out


def _shards_to_full(a, row_sharded):
    if row_sharded:
        a = jnp.moveaxis(a, 0, 1)
        return jnp.reshape(a, (a.shape[0], a.shape[1] * a.shape[2], a.shape[3]))
    a = jnp.moveaxis(a, 0, 2)
    return jnp.reshape(a, (a.shape[0], a.shape[1], a.shape[2] * a.shape[3]))


def _full_to_shards(a, row_sharded):
    L, R, C = a.shape
    if row_sharded:
        return jnp.moveaxis(jnp.reshape(a, (L, N_CHIPS, R // N_CHIPS, C)), 1, 0)
    return jnp.moveaxis(jnp.reshape(a, (L, R, N_CHIPS, C // N_CHIPS)), 2, 0)


def kernel(x, norm1_g, w_in, dn_conv_w, dn_a_log, dn_dt_bias, dn_onorm_g, sg_ln_g, sg_ln_b, sg_w, sg_b, w_branch_a, w_branch_b, w_out, norm2_g, ffn_w_gate, ffn_w_up, ffn_conv_w, ffn_conv_b, ffn_w_down, final_norm_g, loss_target, m_norm1_g, m_w_in, m_dn_conv_w, m_dn_a_log, m_dn_dt_bias, m_dn_onorm_g, m_sg_ln_g, m_sg_ln_b, m_sg_w, m_sg_b, m_w_branch_a, m_w_branch_b, m_w_out, m_norm2_g, m_ffn_w_gate, m_ffn_w_up, m_ffn_conv_w, m_ffn_conv_b, m_ffn_w_down, m_final_norm_g, v_norm1_g, v_w_in, v_dn_conv_w, v_dn_a_log, v_dn_dt_bias, v_dn_onorm_g, v_sg_ln_g, v_sg_ln_b, v_sg_w, v_sg_b, v_w_branch_a, v_w_branch_b, v_w_out, v_norm2_g, v_ffn_w_gate, v_ffn_w_up, v_ffn_conv_w, v_ffn_conv_b, v_ffn_w_down, v_final_norm_g):
    W = dict(norm1_g=norm1_g, w_in=w_in, dn_conv_w=dn_conv_w, dn_a_log=dn_a_log, dn_dt_bias=dn_dt_bias,
             dn_onorm_g=dn_onorm_g, sg_ln_g=sg_ln_g, sg_ln_b=sg_ln_b, sg_w=sg_w, sg_b=sg_b, w_branch_a=w_branch_a,
             w_branch_b=w_branch_b, w_out=w_out, norm2_g=norm2_g, ffn_w_gate=ffn_w_gate, ffn_w_up=ffn_w_up,
             ffn_conv_w=ffn_conv_w, ffn_conv_b=ffn_conv_b, ffn_w_down=ffn_w_down, final_norm_g=final_norm_g)
    M = dict(norm1_g=m_norm1_g, w_in=m_w_in, dn_conv_w=m_dn_conv_w, dn_a_log=m_dn_a_log, dn_dt_bias=m_dn_dt_bias,
             dn_onorm_g=m_dn_onorm_g, sg_ln_g=m_sg_ln_g, sg_ln_b=m_sg_ln_b, sg_w=m_sg_w, sg_b=m_sg_b,
             w_branch_a=m_w_branch_a, w_branch_b=m_w_branch_b, w_out=m_w_out, norm2_g=m_norm2_g,
             ffn_w_gate=m_ffn_w_gate, ffn_w_up=m_ffn_w_up, ffn_conv_w=m_ffn_conv_w, ffn_conv_b=m_ffn_conv_b,
             ffn_w_down=m_ffn_w_down, final_norm_g=m_final_norm_g)
    V = dict(norm1_g=v_norm1_g, w_in=v_w_in, dn_conv_w=v_dn_conv_w, dn_a_log=v_dn_a_log, dn_dt_bias=v_dn_dt_bias,
             dn_onorm_g=v_dn_onorm_g, sg_ln_g=v_sg_ln_g, sg_ln_b=v_sg_ln_b, sg_w=v_sg_w, sg_b=v_sg_b,
             w_branch_a=v_w_branch_a, w_branch_b=v_w_branch_b, w_out=v_w_out, norm2_g=v_norm2_g,
             ffn_w_gate=v_ffn_w_gate, ffn_w_up=v_ffn_w_up, ffn_conv_w=v_ffn_conv_w, ffn_conv_b=v_ffn_conv_b,
             ffn_w_down=v_ffn_w_down, final_norm_g=v_final_norm_g)
    cx, cy, cc = _coords()
    chip = 2 * cx + cy
    L = w_in.shape[0]

    D = w_in.shape[1]
    cs_in = w_in.shape[2]
    c1 = 4 * WD
    ba_chip, ba_off = c1 // cs_in, c1 % cs_in
    assert ba_off + 2 * H <= cs_in
    n_main = N_CHIPS * cs_in - 2 * H
    main_start = [i * cs_in - (2 * H if i > ba_chip else 0) for i in range(N_CHIPS)]
    main_len = [cs_in - (2 * H if i == ba_chip else 0) for i in range(N_CHIPS)]
    tile0 = [16 * (s // 16) for s in main_start]
    shift = [s - t for s, t in zip(main_start, tile0)]
    rp_in = -(-max(sh + ln for sh, ln in zip(shift, main_len)) // 32) * 32
    seg = [tile0[i + 1] - tile0[i] for i in range(N_CHIPS - 1)] + [n_main - tile0[-1]]
    assert all(s + 16 <= rp_in for s in seg[:-1]) and seg[-1] <= rp_in and tile0[-1] + rp_in <= n_main + 128
    my_shift = jnp.asarray(shift, jnp.int32)[chip]

    mine = {n: W[n].astype(BF16) for n in BIG if n != "w_in"}
    wt = jnp.swapaxes(W["w_in"], 1, 2).astype(BF16)
    ba = wt[:, ba_off:ba_off + 2 * H]
    local_row = lax.broadcasted_iota(jnp.int32, (cs_in, 1), 0)
    without_ba = jnp.where(local_row < ba_off, wt, jnp.pad(wt[:, 2 * H:], ((0, 0), (0, 2 * H), (0, 0))))
    mine["w_in"] = lax.dynamic_update_slice(jnp.zeros((L, rp_in, D), BF16),
                                            jnp.where(chip == ba_chip, without_ba, wt), (0, my_shift, 0))
    mine["w_ba"] = jnp.pad(jnp.where(chip == ba_chip, ba, jnp.zeros_like(ba)), ((0, 0), (0, 32 - 2 * H), (0, 0)))

    def halves(a, lead=0):
        return jnp.reshape(a, a.shape[:lead] + (2, a.shape[lead] // 2) + a.shape[lead + 1:])

    taps = _ag8(_pack_rows([W[n] for n in SMALL_SHARDED], 16))
    tap_shards = [_unpack(taps[2 * i], [W[n].shape for n in SMALL_SHARDED]) for i in range(N_CHIPS)]
    taps_full = {n: jnp.concatenate([tap_shards[i][k] for i in range(N_CHIPS)], axis=-1)
                 for k, n in enumerate(SMALL_SHARDED)}

    ops = []
    for l in range(L):
        p = {n: W[n][l] for n in W if n not in ("final_norm_g",) + BIG + SMALL_SHARDED}
        p.update({n: taps_full[n][l] for n in SMALL_SHARDED})
        ops.append(_prep_small(p))

    def gather_payload(items):
        return ("gather", [halves(mine[n][l]) for l, n in items])

    def weights_landed(items, gathered):
        got = {}
        for (l, n), a in zip(items, gathered):
            a = jnp.reshape(a, (N_CHIPS,) + mine[n].shape[1:])
            got[(l, n)] = [jnp.where(chip == i, mine[n][l], a[i]) for i in range(N_CHIPS)]
        for (l, n), parts in got.items():
            if n == "w_in":
                pieces = [parts[0][:seg[0]]]
                for i in range(1, N_CHIPS):
                    pieces += [parts[i][:16] + parts[i - 1][seg[i - 1]:seg[i - 1] + 16], parts[i][16:seg[i]]]
                ba_rows = got[(l, "w_ba")][ba_chip][:2 * H]
                ops[l]["w_in_t"] = jnp.concatenate(pieces + [ba_rows, jnp.zeros((128 - 2 * H, D), BF16)], axis=0)
            elif n != "w_ba":
                ops[l][n] = jnp.concatenate(parts, axis=0 if n in ROW_SHARDED else 1)

    partial_sums, chip_sums = {}, {}

    def grad_partials(l, names, g):
        Gs = []
        for n in names:
            if n == "w_in":
                a = jnp.stack([g["w_in_t"][t:t + rp_in] for t in tile0])
            elif n == "w_ba":
                a = jnp.broadcast_to(g["w_in_t"][n_main:n_main + 32][None], (N_CHIPS, 32, D))
            elif n in ROW_SHARDED:
                a = jnp.reshape(g[n], (N_CHIPS, g[n].shape[0] // N_CHIPS, g[n].shape[1]))
            else:
                a = jnp.moveaxis(jnp.reshape(g[n], (g[n].shape[0], N_CHIPS, g[n].shape[1] // N_CHIPS)), 1, 0)
            Gs.append(halves(a, 1))
        B1s = _rs_pair_exchange(Gs)
        for n, a, b in zip(names, Gs, B1s):
            partial_sums[(l, n)] = _rs_add_pair(a, b, cc, "rs_add_pair_" + n)

    def carrier(l, plan, landed):
        def payload(kernel):
            items = plan.get((l, kernel))
            if not items:
                return None
            return gather_payload(items) if landed is weights_landed else ("exchange", [partial_sums[i] for i in items])
        return _Carrier(payload, lambda kernel, res: landed(plan[(l, kernel)], res))

    FFN = ("ffn_w_gate", "ffn_w_up", "ffn_w_down")
    REST = ("w_in", "w_ba", "w_branch_a", "w_branch_b", "w_out")
    fwd_plan = {(0, "proj"): [(0, "w_branch_a"), (0, "w_branch_b"), (0, "w_out"), (0, "ffn_w_gate")],
                (0, "dn_core"): [(0, "ffn_w_up"), (0, "ffn_w_down"), (1, "w_in"), (1, "w_ba")],
                (0, "ffn_gate"): [(1, "w_branch_a"), (1, "w_branch_b"), (1, "w_out")],
                (0, "ffn_up"): [(1, "ffn_w_gate")],
                (0, "ffn_down"): [(1, "ffn_w_up")],
                (1, "proj"): [(1, "ffn_w_down")]}
    bwd_plan = {(1, "dn_core"): [(1, n) for n in FFN],
                (0, "d_act"): [(1, "w_branch_a"), (1, "w_branch_b"), (1, "w_out")],
                (0, "dn_core"): [(1, "w_in"), (1, "w_ba"), (0, "ffn_w_down")],
                (0, "dw_in"): [(0, "ffn_w_gate"), (0, "ffn_w_up")],
                (0, "dh"): [(0, n) for n in REST]}

    def sums_landed(items, res):
        chip_sums.update(zip(items, res))

    first = [(0, "w_in"), (0, "w_ba")]
    weights_landed(first, _ag_layers(gather_payload(first)[1]))
    xs, saved = x[0], []
    for l in range(L):
        xs, s = _layer_fwd(xs, ops[l], carrier(l, fwd_plan, weights_landed))
        saved.append(s)
    dx, dgf, loss = _loss_head(xs, final_norm_g[None], loss_target[0])
    loss = loss[0, 0]
    grads = [None] * L
    for l in reversed(range(L)):
        dx, grads[l] = _layer_bwd(dx, ops[l], saved[l], carrier(l, bwd_plan, sums_landed),
                                  functools.partial(grad_partials, l, FFN), functools.partial(grad_partials, l, REST))
    travelled = BIG + ("w_ba",)
    g_mine = [[_rs_sum_chips(partial_sums[(l, n)], chip_sums[(l, n)], chip, "rs_sum_chips_" + n) for n in travelled]
              for l in range(L)]
    swapped = _rs_pair_swap(g_mine[0] + g_mine[1])
    g_other = [swapped[:len(travelled)], swapped[len(travelled):]]

    def both_halves(l, n):
        a, b = g_mine[l][travelled.index(n)], g_other[l][travelled.index(n)]
        return jnp.where(cc == 0, jnp.concatenate([a, b]), jnp.concatenate([b, a]))

    def w_in_grad_rows(l):
        m = lax.dynamic_slice_in_dim(both_halves(l, "w_in"), my_shift, cs_in, axis=0)
        ba_rows = jnp.pad(both_halves(l, "w_ba")[:2 * H], ((ba_off, cs_in - ba_off - 2 * H), (0, 0)))
        moved = jnp.pad(m[:cs_in - 2 * H], ((2 * H, 0), (0, 0)))
        with_ba = jnp.where(local_row < ba_off, m, jnp.where(local_row < ba_off + 2 * H, ba_rows, moved))
        return jnp.where(chip == ba_chip, with_ba, m)

    small = {n: jnp.stack([g[n] for g in grads]) for n in SMALL if n != "final_norm_g"}
    small["final_norm_g"] = dgf
    shapes = [taps_full[n].shape if n in SMALL_SHARDED else W[n].shape for n in SMALL] + [(1,)]
    sflat = _pack_rows([small[n] for n in SMALL] + [jnp.reshape(loss, (1,))], 16)
    sred = _unpack(_sum_slots(_ag8(sflat), "sum_small"), shapes)
    g_small = dict(zip(SMALL, sred[:-1]))
    loss_total = sred[-1][0]
    for n in SMALL_SHARDED:
        cs = W[n].shape[-1]
        g_small[n] = lax.dynamic_slice_in_dim(g_small[n], chip * cs, cs, axis=-1)

    g_big, delta, new_m, new_v = {}, {}, {}, {}
    for k, n in enumerate(BIG):
        gm, go = [g_mine[l][k] for l in range(L)], [g_other[l][k] for l in range(L)]
        if n == "w_in":
            g_t = jnp.stack([w_in_grad_rows(l) for l in range(L)], axis=1)
            outs = _adamw(*[jnp.transpose(a, (2, 0, 1)) for a in (W[n],)], g_t,
                          *[jnp.transpose(a, (2, 0, 1)) for a in (M[n], V[n])], "adamw_" + n)
            g_big[n], delta[n], new_m[n], new_v[n] = [jnp.transpose(o, (1, 2, 0)) for o in (g_t,) + tuple(outs)]
        else:
            outs = _adamw_halves(halves(W[n], 1), gm, go, cc, halves(M[n], 1), halves(V[n], 1), "adamw_" + n)
            g_big[n], delta[n], new_m[n], new_v[n] = [jnp.reshape(o, W[n].shape) for o in outs]
    s_shapes = [W[n].shape for n in SMALL]
    packed = [_pack_rows([d[n] for n in SMALL], 8) for d in (W, g_small, M, V)]
    outs = _adamw(*[a[None] for a in packed], "adamw_small")
    for d, o in zip((delta, new_m, new_v), outs):
        d.update(zip(SMALL, _unpack(o[0], s_shapes)))

    names = list(W)
    grad_w = {**g_big, **g_small}
    return (loss_total, dx[None], *[grad_w[n] for n in names], *[delta[n] for n in names],
            *[new_m[n] for n in names], *[new_v[n] for n in names])
```

```python
import functools
import math

import jax
import jax.numpy as jnp
from jax import lax
from jax.experimental import pallas as pl
from jax.experimental.pallas import tpu as pltpu

F32 = jnp.float32
BF16 = jnp.bfloat16
MESH = pl.DeviceIdType.MESH

EPS = 1e-6
H = 8
DH = 128
WD = H * DH
DNC = 64
SGC = 128
DN_K = 4
FF_K = 3
DEPTH = 2
N_CHIPS = 4

ADAM_LR = 0.001
ADAM_B1 = 0.9
ADAM_B2 = 0.999
ADAM_EPS = 1e-08
ADAM_WD = 0.01
ADAM_STEP = 10

VMEM_LIMIT = 56 * 1024 * 1024

NN = (((1,), (0,)), ((), ()))
NT = (((1,), (1,)), ((), ()))
TN = (((0,), (0,)), ((), ()))

OQ, OZ, OU, OV, OGA = 0, 3 * WD, 4 * WD, 5 * WD, 6 * WD


def _cp(*sem):
    return pltpu.CompilerParams(dimension_semantics=sem or None, vmem_limit_bytes=VMEM_LIMIT)


def _tile(dim, pref, unit=128):
    if dim <= pref:
        return dim
    t = (pref // unit) * unit
    while t >= unit:
        if dim % t == 0:
            return t
        t -= unit
    return dim


def _bdot(a, b, dn=NN):
    return lax.dot_general(a.astype(BF16), b.astype(BF16), dn, preferred_element_type=F32)


def _lsum(x):
    return jnp.sum(x, axis=1, keepdims=True)


def _sig(x):
    return 0.5 * jnp.tanh(0.5 * x) + 0.5


def _silu_and_grad(x):
    s = _sig(x)
    return x * s, s * (1.0 + x * (1.0 - s))


def _gelu_parts(x):
    a = jnp.abs(x) * (2.0 ** -0.5)
    t = 1.0 / (1.0 + 0.3275911 * a)
    poly = t * (0.254829592 + t * (-0.284496736 + t * (1.421413741 + t * (-1.453152027 + t * 1.061405429))))
    e = jnp.exp(-a * a)
    half = 0.5 * poly * e
    return jnp.where(x < 0, half, 1.0 - half), e * (1.0 / math.sqrt(2.0 * math.pi))


def _gelu(x):
    return x * _gelu_parts(x)[0]


def _gelu_and_grad(x):
    cdf, pdf = _gelu_parts(x)
    return x * cdf, cdf + x * pdf


def _shift_down(x, k):
    if k == 0:
        return x
    y = pltpu.roll(x, k, 0)
    rows = lax.broadcasted_iota(jnp.int32, (8, x.shape[1]), 0)
    return jnp.concatenate([jnp.where(rows >= k, y[:8], 0.0), y[8:]], axis=0)


def _shift_up(x, k):
    if k == 0:
        return x
    n = x.shape[0]
    y = pltpu.roll(x, n - k, 0)
    rows = lax.broadcasted_iota(jnp.int32, (8, x.shape[1]), 0)
    return jnp.concatenate([y[:n - 8], jnp.where(rows < 8 - k, y[n - 8:], 0.0)], axis=0)


def _comm_fns(comm):
    if not comm:
        return None, None, dict(out_shape=[], specs=[], sems=[]), ()
    kind, arrays = comm
    start, finish, specs = {"gather": (_ag_start, _ag_finish, _ag_specs),
                            "exchange": (_rsx_start, _rsx_finish, _rsx_specs)}[kind]
    return start, finish, specs(arrays), tuple(arrays)


def _mm(a, b, mode, out_dtype, add=None, name="mm", comm=None):
    if mode == "tn":
        K, M = a.shape
    else:
        M, K = a.shape
    N = b.shape[0] if mode == "nt" else b.shape[1]
    tm, tn, tk = _tile(M, 1152), _tile(N, 1536), _tile(K, 3584)
    nk = K // tk
    ni, nj = M // tm, N // tn
    dn = {"nn": NN, "nt": NT, "tn": TN}[mode]
    c_start, c_finish, c_sp, payload = _comm_fns(comm)
    nc = len(payload)
    n_add = 0 if add is None else 1

    def body(*refs):
        a_ref, b_ref = refs[:2]
        add_ref = refs[2] if n_add else None
        c_in = refs[2 + n_add:2 + n_add + nc]
        o_ref = refs[2 + n_add + nc]
        c_out = refs[3 + n_add + nc:3 + n_add + 2 * nc]
        rest = refs[3 + n_add + 2 * nc:]
        acc_ref = rest[0] if nk > 1 else None
        sems = rest[1:] if nk > 1 else rest
        i, j, k = pl.program_id(0), pl.program_id(1), pl.program_id(2)

        if nc:
            @pl.when(jnp.logical_and(jnp.logical_and(i == 0, j == 0), k == 0))
            def _():
                c_start(c_in, c_out, *sems)

        def finish(r):
            if add is not None:
                r = r + add_ref[...]
            o_ref[...] = r.astype(o_ref.dtype)

        part = lax.dot_general(a_ref[...], b_ref[...], dn, preferred_element_type=F32)
        if nk == 1:
            finish(part)
        else:
            @pl.when(k == 0)
            def _():
                acc_ref[...] = part

            @pl.when(k > 0)
            def _():
                acc_ref[...] += part

            @pl.when(k == nk - 1)
            def _():
                finish(acc_ref[...])

        if nc:
            @pl.when(jnp.logical_and(jnp.logical_and(i == ni - 1, j == nj - 1), k == nk - 1))
            def _():
                c_finish(c_in, c_out, *sems)

    a_spec = (pl.BlockSpec((tk, tm), lambda i, j, k: (k, i)) if mode == "tn"
              else pl.BlockSpec((tm, tk), lambda i, j, k: (i, k)))
    b_spec = (pl.BlockSpec((tn, tk), lambda i, j, k: (j, k)) if mode == "nt"
              else pl.BlockSpec((tk, tn), lambda i, j, k: (k, j)))
    o_spec = pl.BlockSpec((tm, tn), lambda i, j, k: (i, j))
    in_specs = [a_spec, b_spec] + ([o_spec] if add is not None else []) + c_sp["specs"]
    args = (a, b) + ((add,) if add is not None else ()) + payload
    outs = pl.pallas_call(
        body, name=name + ("_" + comm[0] if nc else ""), grid=(ni, nj, nk), in_specs=in_specs,
        out_specs=[o_spec] + c_sp["specs"],
        out_shape=[jax.ShapeDtypeStruct((M, N), out_dtype)] + c_sp["out_shape"],
        scratch_shapes=([pltpu.VMEM((tm, tn), F32)] if nk > 1 else []) + c_sp["sems"],
        compiler_params=_cp("arbitrary", "arbitrary", "arbitrary") if nc else _cp("parallel", "parallel", "arbitrary"),
    )(*args)
    return (outs[0], list(outs[1:])) if nc else outs[0]


def _rms_fwd(x, g, name):
    T, D = x.shape
    tt = _tile(T, 256, 16)

    def body(x_ref, g_ref, o_ref):
        xv = x_ref[...]
        r = lax.rsqrt(jnp.mean(xv * xv, axis=-1, keepdims=True) + EPS)
        o_ref[...] = (xv * r * g_ref[...]).astype(o_ref.dtype)

    return pl.pallas_call(
        body, name=name, grid=(T // tt,),
        in_specs=[pl.BlockSpec((tt, D), lambda i: (i, 0)), pl.BlockSpec((1, D), lambda i: (0, 0))],
        out_specs=pl.BlockSpec((tt, D), lambda i: (i, 0)),
        out_shape=jax.ShapeDtypeStruct((T, D), BF16), compiler_params=_cp("parallel"),
    )(x, g)


def _rms_bwd(x, g, dh, dres, name):
    T, D = x.shape
    tt = _tile(T, 256, 16)

    def body(x_ref, g_ref, dh_ref, dres_ref, dx_ref, dg_ref):
        @pl.when(pl.program_id(0) == 0)
        def _():
            dg_ref[...] = jnp.zeros_like(dg_ref)

        xv = x_ref[...]
        r = lax.rsqrt(jnp.mean(xv * xv, axis=-1, keepdims=True) + EPS)
        xh = xv * r
        dh_v = dh_ref[...]
        dy = dh_v * g_ref[...]
        dx_ref[...] = dres_ref[...] + r * (dy - xh * jnp.mean(dy * xh, axis=-1, keepdims=True))
        dg_ref[...] += jnp.sum(dh_v * xh, axis=0, keepdims=True)

    row = pl.BlockSpec((tt, D), lambda i: (i, 0))
    vec = pl.BlockSpec((1, D), lambda i: (0, 0))
    return pl.pallas_call(
        body, name=name, grid=(T // tt,), in_specs=[row, vec, row, row], out_specs=[row, vec],
        out_shape=[jax.ShapeDtypeStruct((T, D), F32), jax.ShapeDtypeStruct((1, D), F32)],
        compiler_params=_cp("arbitrary"),
    )(x, g, dh, dres)


def _loss_head(x, g, tgt, name="loss_head"):
    T, D = x.shape
    tt = _tile(T, 256, 16)

    def body(x_ref, g_ref, t_ref, dx_ref, dg_ref, loss_ref):
        @pl.when(pl.program_id(0) == 0)
        def _():
            dg_ref[...] = jnp.zeros_like(dg_ref)
            loss_ref[...] = jnp.zeros_like(loss_ref)

        xv = x_ref[...]
        r = lax.rsqrt(jnp.mean(xv * xv, axis=-1, keepdims=True) + EPS)
        xh = xv * r
        err = xh * g_ref[...] - t_ref[...]
        part = 0.5 * jnp.sum(jnp.mean(err * err, axis=-1, keepdims=True), axis=0, keepdims=True)
        loss_ref[...] += jnp.broadcast_to(part, loss_ref.shape)
        dy = err * (1.0 / D)
        dg_ref[...] += jnp.sum(dy * xh, axis=0, keepdims=True)
        dyh = dy * g_ref[...]
        dx_ref[...] = r * (dyh - xh * jnp.mean(dyh * xh, axis=-1, keepdims=True))

    row = pl.BlockSpec((tt, D), lambda i: (i, 0))
    vec = pl.BlockSpec((1, D), lambda i: (0, 0))
    return pl.pallas_call(
        body, name=name, grid=(T // tt,), in_specs=[row, vec, row],
        out_specs=[row, vec, pl.BlockSpec((1, 128), lambda i: (0, 0))],
        out_shape=[jax.ShapeDtypeStruct((T, D), F32), jax.ShapeDtypeStruct((1, D), F32),
                   jax.ShapeDtypeStruct((1, 128), F32)],
        compiler_params=_cp("arbitrary"),
    )(x, g, tgt)


def _ba_fwd(proj, alog, dtb, oba, name="dn_ba_fwd"):
    T = proj.shape[0]
    tt = _tile(T, 512, 8)

    def body(p_ref, al_ref, dt_ref, o_ref):
        raw = p_ref[...].astype(F32)
        lane = lax.broadcasted_iota(jnp.int32, raw.shape, 1)
        z = raw + dt_ref[...]
        sp = jnp.maximum(z, 0.0) + jnp.log(1.0 + jnp.exp(-jnp.abs(z)))
        gl = -jnp.exp(al_ref[...]) * sp
        o_ref[...] = jnp.where(lane < H, _sig(raw), jnp.where(lane < 2 * H, gl, 0.0))

    vec = pl.BlockSpec((1, 128), lambda i: (0, 0))
    return pl.pallas_call(
        body, name=name, grid=(T // tt,),
        in_specs=[pl.BlockSpec((tt, 128), lambda i: (i, oba // 128)), vec, vec],
        out_specs=pl.BlockSpec((tt, 128), lambda i: (i, 0)),
        out_shape=jax.ShapeDtypeStruct((T, 128), F32), compiler_params=_cp("parallel"),
    )(proj, alog, dtb)


def _ba_bwd(proj, alog, dtb, dbg, oba, name="dn_ba_bwd"):
    T = proj.shape[0]
    tt = _tile(T, 512, 16)

    def body(p_ref, al_ref, dt_ref, d_ref, o_ref, dal_ref, ddt_ref):
        @pl.when(pl.program_id(0) == 0)
        def _():
            dal_ref[...] = jnp.zeros_like(dal_ref)
            ddt_ref[...] = jnp.zeros_like(ddt_ref)

        raw = p_ref[...].astype(F32)
        d = d_ref[...]
        lane = lax.broadcasted_iota(jnp.int32, raw.shape, 1)
        z = raw + dt_ref[...]
        sp = jnp.maximum(z, 0.0) + jnp.log(1.0 + jnp.exp(-jnp.abs(z)))
        na = -jnp.exp(al_ref[...])
        is_g = jnp.logical_and(lane >= H, lane < 2 * H)
        b = _sig(raw)
        dz = jnp.where(is_g, d * na * _sig(z), 0.0)
        o_ref[...] = jnp.where(lane < H, d * b * (1.0 - b), dz).astype(o_ref.dtype)
        dal_ref[...] += jnp.sum(jnp.where(is_g, d * na * sp, 0.0), axis=0, keepdims=True)
        ddt_ref[...] += jnp.sum(dz, axis=0, keepdims=True)

    vec = pl.BlockSpec((1, 128), lambda i: (0, 0))
    return pl.pallas_call(
        body, name=name, grid=(T // tt,),
        in_specs=[pl.BlockSpec((tt, 128), lambda i: (i, oba // 128)), vec, vec,
                  pl.BlockSpec((tt, 128), lambda i: (i, 0))],
        out_specs=[pl.BlockSpec((tt, 128), lambda i: (i, 0)), vec, vec],
        out_shape=[jax.ShapeDtypeStruct((T, 128), BF16), jax.ShapeDtypeStruct((1, 128), F32),
                   jax.ShapeDtypeStruct((1, 128), F32)],
        compiler_params=_cp("arbitrary"),
    )(proj, alog, dtb, dbg)


def _dn_prep_fwd(proj, convw, name="dn_prep_fwd"):
    T = proj.shape[0]
    nblk = 3 * H

    def body(p_ref, w_ref, o_ref):
        j = pl.program_id(0)
        xv = p_ref[...].astype(F32)
        w = w_ref[...]
        c = xv * w[DN_K - 1:DN_K, :]
        for k in range(1, DN_K):
            c = c + _shift_down(xv, k) * w[DN_K - 1 - k:DN_K - k, :]
        s = c * _sig(c)
        r = lax.rsqrt(_lsum(s * s) + EPS)
        o_ref[...] = jnp.where(j < 2 * H, s * r, s)

    return pl.pallas_call(
        body, name=name, grid=(nblk,),
        in_specs=[pl.BlockSpec((T, DH), lambda j: (0, j)), pl.BlockSpec((DN_K, DH), lambda j: (0, j))],
        out_specs=pl.BlockSpec((T, DH), lambda j: (0, j)),
        out_shape=jax.ShapeDtypeStruct((T, 3 * WD), F32), compiler_params=_cp("parallel"),
    )(proj, convw)


def _dn_prep_bwd(proj, convw, dq, dk, dv, name="dn_prep_bwd"):
    T = proj.shape[0]
    nblk = 3 * H

    def body(p_ref, w_ref, dq_ref, dk_ref, dv_ref, dx_ref, dw_ref):
        j = pl.program_id(0)
        xv = p_ref[...].astype(F32)
        w = w_ref[...]
        shifted = [_shift_down(xv, k) for k in range(DN_K)]
        c = shifted[0] * w[DN_K - 1:DN_K, :]
        for k in range(1, DN_K):
            c = c + shifted[k] * w[DN_K - 1 - k:DN_K - k, :]
        s, s_grad = _silu_and_grad(c)
        r = lax.rsqrt(_lsum(s * s) + EPS)
        y = s * r
        dy = jnp.where(j < H, dq_ref[...], jnp.where(j < 2 * H, dk_ref[...], dv_ref[...]))
        ds = jnp.where(j < 2 * H, r * (dy - y * _lsum(dy * y)), dy)
        dc = ds * s_grad
        dx = dc * w[DN_K - 1:DN_K, :]
        for k in range(1, DN_K):
            dx = dx + _shift_up(dc, k) * w[DN_K - 1 - k:DN_K - k, :]
        dx_ref[...] = dx.astype(dx_ref.dtype)
        rows = [jnp.sum(dc * shifted[DN_K - 1 - t], axis=0, keepdims=True) for t in range(DN_K)]
        dw_ref[...] = jnp.concatenate(rows, axis=0)

    hb = lambda off: pl.BlockSpec((T, DH), lambda j: (0, jnp.maximum(jnp.minimum(j - off, H - 1), 0)))
    return pl.pallas_call(
        body, name=name, grid=(nblk,),
        in_specs=[pl.BlockSpec((T, DH), lambda j: (0, j)), pl.BlockSpec((DN_K, DH), lambda j: (0, j)),
                  hb(0), hb(H), hb(2 * H)],
        out_specs=[pl.BlockSpec((T, DH), lambda j: (0, j)), pl.BlockSpec((DN_K, DH), lambda j: (0, j))],
        out_shape=[jax.ShapeDtypeStruct((T, 3 * WD), BF16), jax.ShapeDtypeStruct((DN_K, 3 * WD), F32)],
        compiler_params=_cp("parallel"),
    )(proj, convw, dq, dk, dv)


DN_BLOCK = 4


def _split3(a):
    hi = a.astype(BF16)
    r1 = a - hi.astype(F32)
    mid = r1.astype(BF16)
    return hi, mid, (r1 - mid.astype(F32)).astype(BF16)


def _dot3(a, b, dn=NN):
    ah, al, _ = _split3(a)
    bh, bl, _ = _split3(b)
    d = lambda p, q: lax.dot_general(p, q, dn, preferred_element_type=F32)
    return d(ah, bh) + d(ah, bl) + d(al, bh)


def _mask_dot(m, b, dn=NN):
    mb = m.astype(BF16)
    d = lambda q: (lax.dot_general(mb, q, dn, preferred_element_type=F32) if dn != TN
                   else lax.dot_general(q, mb, dn, preferred_element_type=F32))
    b0, b1, b2 = _split3(b)
    return d(b0) + d(b1) + d(b2)


def _tri_inv(A):
    ri = lax.broadcasted_iota(jnp.int32, A.shape, 0)
    ci = lax.broadcasted_iota(jnp.int32, A.shape, 1)
    T = jnp.where(ri == ci, 1.0, 0.0) - jnp.where((ri // 2) == (ci // 2), A, 0.0)
    s = 2
    while s < DNC:
        off = jnp.logical_and((ri // (2 * s)) == (ci // (2 * s)), (ri // s) != (ci // s))
        T = T - _dot3(_dot3(T, jnp.where(off, A, 0.0)), T)
        s *= 2
    return T


GH = 4
NG = H // GH
GR = GH * DNC
GK = GH * DH


def _dn_masks():
    ri = lax.broadcasted_iota(jnp.int32, (GR, GR), 0)
    ci = lax.broadcasted_iota(jnp.int32, (GR, GR), 1)
    blk = (ri // DNC) == (ci // DNC)
    wide = (lax.broadcasted_iota(jnp.int32, (GR, GK), 0) // DNC) == (lax.broadcasted_iota(jnp.int32, (GR, GK), 1) // DH)
    return dict(blk=blk, causal=jnp.logical_and(blk, ri >= ci), strict=jnp.logical_and(blk, ri > ci),
                upper=jnp.logical_and(blk, ri <= ci), eye=ri == ci, wide=wide)


def _wide(a, mk):
    return jnp.where(mk["wide"], jnp.tile(a, (1, GH)), 0.0)


def _fold(a, mk):
    a = jnp.where(mk["wide"], a, 0.0)
    out = a[:, :DH]
    for j in range(1, GH):
        out = out + a[:, j * DH:(j + 1) * DH]
    return out


def _stack_heads(ref, rows, g):
    return jnp.concatenate([ref[rows, (g * GH + j) * DH:(g * GH + j + 1) * DH] for j in range(GH)], axis=0)


def _dn_group(q_ref, k_ref, v_ref, rows, bg, gc_cols, g, mk):
    heads = [g * GH + j for j in range(GH)]
    col = lambda a, lane: jnp.concatenate([a[:, lane(h):lane(h) + 1] for h in heads], axis=0)
    q = _stack_heads(q_ref, rows, g) * (DH ** -0.5)
    k = _stack_heads(k_ref, rows, g)
    v = _stack_heads(v_ref, rows, g)
    beta = col(bg, lambda h: h)
    gcol = col(gc_cols, lambda h: H + h)
    last = [gc_cols[DNC - 1:DNC, H + h:H + h + 1] for h in heads]
    gl = jnp.concatenate([jnp.broadcast_to(t, (DNC, 1)) for t in last], axis=0)
    egl_state = jnp.concatenate([jnp.broadcast_to(jnp.exp(t), (DH, 1)) for t in last], axis=0)
    grow = _mask_dot(jnp.ones((GR, GR), F32), jnp.where(mk["eye"], gcol, 0.0))
    dec = jnp.where(mk["causal"], jnp.exp(jnp.where(mk["causal"], gcol - grow, 0.0)), 0.0)
    eg = jnp.exp(gcol)
    ek = jnp.exp(gl - gcol)
    kb = k * beta
    vb = v * beta
    kbe = kb * eg
    A = jnp.where(mk["strict"], _bdot(kb, k, NT) * dec, 0.0)
    P = jnp.where(mk["causal"], _bdot(q, k, NT) * dec, 0.0)
    return dict(q=q, k=k, v=v, beta=beta, dec=dec, eg=eg, ek=ek, egl=jnp.exp(gl), egl_state=egl_state, kb=kb, vb=vb,
                kbe=kbe, A=A, P=P, qd=q * eg, kd=k * ek, heads=heads)


def _gc_cols(bg):
    ri = lax.broadcasted_iota(jnp.int32, (DNC, DNC), 0)
    ci = lax.broadcasted_iota(jnp.int32, (DNC, DNC), 1)
    return _mask_dot(jnp.where(ri >= ci, 1.0, 0.0), bg)


def _dn_core_fwd(qkv, bg, comm=None, name="dn_core_fwd"):
    c_start, c_finish, sp, gather = _comm_fns(comm)
    T = qkv.shape[0]
    n_chunks = T // DNC
    nb = _tile(n_chunks, DN_BLOCK, 1)
    tb = nb * DNC

    ng = len(gather)
    n_steps = n_chunks // nb

    def body(*refs):
        q_ref, k_ref, v_ref, bg_ref = refs[:4]
        o_ref, s_ref, tm_ref = refs[4 + ng:7 + ng]
        S_scr = refs[7 + 2 * ng]
        comm_refs = (refs[4:4 + ng], refs[7 + ng:7 + 2 * ng]) + tuple(refs[8 + 2 * ng:])

        @pl.when(pl.program_id(0) == 0)
        def _():
            S_scr[...] = jnp.zeros_like(S_scr)
            if ng:
                c_start(*comm_refs)

        def chunk(n, carry):
            rows = pl.ds(pl.multiple_of(n * DNC, DNC), DNC)
            mk = _dn_masks()
            bgc = bg_ref[rows, :]
            gc_cols = _gc_cols(bgc)
            for g in range(NG):
                c = _dn_group(q_ref, k_ref, v_ref, rows, bgc, gc_cols, g, mk)
                Tm = _tri_inv(c["A"])
                tm_ref[n, g] = Tm
                S = S_scr[g]
                s_ref[n, g] = S
                u = _bdot(Tm, c["vb"])
                w = _bdot(Tm, c["kbe"])
                vn = u - _bdot(_wide(w, mk), S)
                o = _bdot(_wide(c["qd"], mk), S) + _bdot(c["P"], vn)
                for j, h in enumerate(c["heads"]):
                    o_ref[rows, h * DH:(h + 1) * DH] = o[j * DNC:(j + 1) * DNC]
                S_scr[g] = S * c["egl_state"] + _bdot(_wide(c["kd"], mk), vn, TN)
            return carry

        lax.fori_loop(0, nb, chunk, 0)

        if ng:
            @pl.when(pl.program_id(0) == n_steps - 1)
            def _():
                c_finish(*comm_refs)

    blk = lambda j: pl.BlockSpec((tb, WD), lambda i: (i, j))
    outs = pl.pallas_call(
        body, name=name + ("_" + comm[0] if ng else ""), grid=(n_steps,),
        in_specs=[blk(0), blk(1), blk(2), pl.BlockSpec((tb, 128), lambda i: (i, 0))] + sp["specs"],
        out_specs=[blk(0), pl.BlockSpec((nb, NG, GK, DH), lambda i: (i, 0, 0, 0)),
                   pl.BlockSpec((nb, NG, GR, GR), lambda i: (i, 0, 0, 0))] + sp["specs"],
        out_shape=[jax.ShapeDtypeStruct((T, WD), F32), jax.ShapeDtypeStruct((n_chunks, NG, GK, DH), F32),
                   jax.ShapeDtypeStruct((n_chunks, NG, GR, GR), F32)] + sp["out_shape"],
        scratch_shapes=[pltpu.VMEM((NG, GK, DH), F32)] + (sp["sems"] if ng else []),
        compiler_params=_cp("arbitrary"),
    )(qkv, qkv, qkv, bg, *gather)
    return outs[0], outs[1], outs[2], list(outs[3:])


def _dn_core_bwd(qkv, bg, s_all, tm_all, do, comm=None, name="dn_core_bwd"):
    c_start, c_finish, sp, exchange = _comm_fns(comm)
    T = qkv.shape[0]
    n_chunks = T // DNC
    nb = _tile(n_chunks, DN_BLOCK, 1)
    tb = nb * DNC
    n_blocks = n_chunks // nb

    nx = len(exchange)

    def body(*refs):
        q_ref, k_ref, v_ref, bg_ref, s_ref, tm_ref, do_ref = refs[:7]
        dq_ref, dk_ref, dv_ref, dbg_ref = refs[7 + nx:11 + nx]
        dS_scr = refs[11 + 2 * nx]
        comm_refs = (refs[7:7 + nx], refs[11 + nx:11 + 2 * nx]) + tuple(refs[12 + 2 * nx:])

        @pl.when(pl.program_id(0) == 0)
        def _():
            dS_scr[...] = jnp.zeros_like(dS_scr)
            if nx:
                c_start(*comm_refs)

        lane = lax.broadcasted_iota(jnp.int32, (DNC, 128), 1)
        row = lax.broadcasted_iota(jnp.int32, (GR, 1), 0)

        def chunk(i, carry):
            n = nb - 1 - i
            rows = pl.ds(pl.multiple_of(n * DNC, DNC), DNC)
            mk = _dn_masks()
            ones = jnp.ones((GR, GR), F32)
            blk_f = jnp.where(mk["blk"], 1.0, 0.0)
            wide_f = jnp.where(mk["wide"], 1.0, 0.0)
            per_row = lambda m, a: _mask_dot(m, jnp.broadcast_to(a, (a.shape[0], DH)))[:, :1]
            bgc = bg_ref[rows, :]
            gc_cols = _gc_cols(bgc)
            dbg = jnp.zeros((DNC, 128), F32)
            for g in range(NG):
                c = _dn_group(q_ref, k_ref, v_ref, rows, bgc, gc_cols, g, mk)
                q, k, v, beta = c["q"], c["k"], c["v"], c["beta"]
                dec, eg, ek, egl = c["dec"], c["eg"], c["ek"], c["egl"]
                kb, vb, kbe, A, P, qd, kd = c["kb"], c["vb"], c["kbe"], c["A"], c["P"], c["qd"], c["kd"]
                S = s_ref[n, g]
                Tm = tm_ref[n, g]
                u = _bdot(Tm, vb)
                w = _bdot(Tm, kbe)
                w_wide = _wide(w, mk)
                vn = u - _bdot(w_wide, S)
                d_o = _stack_heads(do_ref, rows, g)
                dS1 = dS_scr[g]
                d_qd = _fold(_bdot(d_o, S, NT), mk)
                dP = jnp.where(mk["causal"], _bdot(d_o, vn, NT), 0.0)
                d_vn = _bdot(P, d_o, TN) + _bdot(_wide(kd, mk), dS1)
                d_kd = _fold(_bdot(vn, dS1, NT), mk)
                d_egl = per_row(wide_f, _lsum(dS1 * S))
                dS_scr[g] = dS1 * c["egl_state"] + _bdot(_wide(qd, mk), d_o, TN) - _bdot(w_wide, d_vn, TN)
                d_w = -_fold(_bdot(d_vn, S, NT), mk)
                d_vb = _bdot(Tm, d_vn, TN)
                d_kbe = _bdot(Tm, d_w, TN)
                dA = jnp.where(mk["strict"], -(_bdot(d_vb, u, NT) + _bdot(d_kbe, w, NT)), 0.0)
                dMA = dA * dec
                dMP = dP * dec
                d_kb = _bdot(dMA, k) + d_kbe * eg
                d_k = _bdot(dMA, kb, TN) + _bdot(dMP, q, TN) + d_kd * ek + d_kb * beta
                d_qs = (_bdot(dMP, k) + d_qd * eg) * (DH ** -0.5)
                d_v = d_vb * beta
                E = dA * A + dP * P
                col_sums = _mask_dot(ones, E, TN)[:, :1]
                t_kd = _lsum(d_kd * kd)
                d_gl = per_row(blk_f, t_kd) + d_egl * egl
                d_gc = (_lsum(E) - col_sums + _lsum(d_qd * qd) + _lsum(d_kbe * kbe) - t_kd
                        + jnp.where(row % DNC == DNC - 1, d_gl, 0.0))
                d_g = per_row(jnp.where(mk["upper"], 1.0, 0.0), d_gc)
                d_beta = _lsum(d_kb * k) + _lsum(d_vb * v)
                for j, h in enumerate(c["heads"]):
                    rs = slice(j * DNC, (j + 1) * DNC)
                    dq_ref[rows, h * DH:(h + 1) * DH] = d_qs[rs]
                    dk_ref[rows, h * DH:(h + 1) * DH] = d_k[rs]
                    dv_ref[rows, h * DH:(h + 1) * DH] = d_v[rs]
                    dbg = dbg + jnp.where(lane == h, d_beta[rs], 0.0) + jnp.where(lane == h + H, d_g[rs], 0.0)
            dbg_ref[rows, :] = dbg
            return carry

        lax.fori_loop(0, nb, chunk, 0)

        if nx:
            @pl.when(pl.program_id(0) == n_blocks - 1)
            def _():
                c_finish(*comm_refs)

    blk = lambda j: pl.BlockSpec((tb, WD), lambda i: (n_blocks - 1 - i, j))
    small = pl.BlockSpec((tb, 128), lambda i: (n_blocks - 1 - i, 0))
    outs = pl.pallas_call(
        body, name=name + ("_" + comm[0] if nx else ""), grid=(n_blocks,),
        in_specs=[blk(0), blk(1), blk(2), small,
                  pl.BlockSpec((nb, NG, GK, DH), lambda i: (n_blocks - 1 - i, 0, 0, 0)),
                  pl.BlockSpec((nb, NG, GR, GR), lambda i: (n_blocks - 1 - i, 0, 0, 0)), blk(0)] + sp["specs"],
        out_specs=[blk(0), blk(0), blk(0), small] + sp["specs"],
        out_shape=[jax.ShapeDtypeStruct((T, WD), F32)] * 3 + [jax.ShapeDtypeStruct((T, 128), F32)] + sp["out_shape"],
        scratch_shapes=[pltpu.VMEM((NG, GK, DH), F32)] + (sp["sems"] if nx else []),
        compiler_params=_cp("arbitrary"),
    )(qkv, qkv, qkv, bg, s_all, tm_all, do, *exchange)
    return outs[0], outs[1], outs[2], outs[3], list(outs[4:])


def _dn_post_fwd(o, proj, gon, name="dn_post_fwd"):
    T = o.shape[0]
    tt = _tile(T, 256, 16)

    def body(o_ref, z_ref, g_ref, y_ref):
        for hh in range(H):
            sl = slice(hh * DH, (hh + 1) * DH)
            ov = o_ref[:, sl]
            zv = z_ref[:, sl].astype(F32)
            r = lax.rsqrt(jnp.mean(ov * ov, axis=-1, keepdims=True) + EPS)
            y_ref[:, sl] = (ov * r * g_ref[...] * (zv * _sig(zv))).astype(y_ref.dtype)

    return pl.pallas_call(
        body, name=name, grid=(T // tt,),
        in_specs=[pl.BlockSpec((tt, WD), lambda i: (i, 0)), pl.BlockSpec((tt, WD), lambda i: (i, OZ // WD)),
                  pl.BlockSpec((1, DH), lambda i: (0, 0))],
        out_specs=pl.BlockSpec((tt, WD), lambda i: (i, 0)),
        out_shape=jax.ShapeDtypeStruct((T, WD), BF16), compiler_params=_cp("parallel"),
    )(o, proj, gon)


def _dn_post_bwd(o, proj, gon, dy, name="dn_post_bwd"):
    T = o.shape[0]
    tt = _tile(T, 256, 16)

    def body(o_ref, z_ref, g_ref, dy_ref, do_ref, dz_ref, dg_ref):
        @pl.when(pl.program_id(0) == 0)
        def _():
            dg_ref[...] = jnp.zeros_like(dg_ref)

        acc = jnp.zeros((1, DH), F32)
        for hh in range(H):
            sl = slice(hh * DH, (hh + 1) * DH)
            ov = o_ref[:, sl]
            zv = z_ref[:, sl].astype(F32)
            dyv = dy_ref[:, sl]
            r = lax.rsqrt(jnp.mean(ov * ov, axis=-1, keepdims=True) + EPS)
            oh = ov * r
            nrm = oh * g_ref[...]
            gate, gate_grad = _silu_and_grad(zv)
            dn = dyv * gate
            dz_ref[:, sl] = (dyv * nrm * gate_grad).astype(dz_ref.dtype)
            doh = dn * g_ref[...]
            do_ref[:, sl] = r * (doh - oh * jnp.mean(doh * oh, axis=-1, keepdims=True))
            acc = acc + jnp.sum(dn * oh, axis=0, keepdims=True)
        dg_ref[...] += acc

    row = pl.BlockSpec((tt, WD), lambda i: (i, 0))
    vec = pl.BlockSpec((1, DH), lambda i: (0, 0))
    return pl.pallas_call(
        body, name=name, grid=(T // tt,),
        in_specs=[row, pl.BlockSpec((tt, WD), lambda i: (i, OZ // WD)), vec, row],
        out_specs=[row, row, vec],
        out_shape=[jax.ShapeDtypeStruct((T, WD), F32), jax.ShapeDtypeStruct((T, WD), BF16),
                   jax.ShapeDtypeStruct((1, DH), F32)],
        compiler_params=_cp("arbitrary"),
    )(o, proj, gon, dy)


def _sg_common(u_ref, v_ref, lng_ref, lnb_ref, with_grad=True):
    ur = u_ref[...].astype(F32)
    vr = v_ref[...].astype(F32)
    vgel, vgel_grad = _gelu_and_grad(vr) if with_grad else (_gelu(vr), None)
    mu = jnp.mean(vgel, axis=-1, keepdims=True)
    xc = vgel - mu
    rs = lax.rsqrt(jnp.mean(xc * xc, axis=-1, keepdims=True) + EPS)
    xh = xc * rs
    vg = xh * lng_ref[...] + lnb_ref[...]
    return ur, vgel_grad, rs, xh, vg


def _sg_fwd(proj, lng, lnb, sgw, sgbt, name="sg_fwd"):
    T = proj.shape[0]

    def body(u_ref, v_ref, lng_ref, lnb_ref, w_ref, bt_ref, y_ref):
        ur, _, _, _, vg = _sg_common(u_ref, v_ref, lng_ref, lnb_ref, with_grad=False)
        ri = lax.broadcasted_iota(jnp.int32, (SGC, SGC), 0)
        ci = lax.broadcasted_iota(jnp.int32, (SGC, SGC), 1)
        ug = _gelu(ur)
        for g in range(H):
            sl = slice(g * DH, (g + 1) * DH)
            ws = jnp.where(ri >= ci, w_ref[g], 0.0)
            mixed = _bdot(ws, vg[:, sl]) + bt_ref[:, g:g + 1]
            y_ref[:, sl] = (ug[:, sl] * mixed).astype(y_ref.dtype)

    vec = pl.BlockSpec((1, WD), lambda i: (0, 0))
    return pl.pallas_call(
        body, name=name, grid=(T // SGC,),
        in_specs=[pl.BlockSpec((SGC, WD), lambda i: (i, OU // WD)), pl.BlockSpec((SGC, WD), lambda i: (i, OV // WD)),
                  vec, vec, pl.BlockSpec((H, SGC, SGC), lambda i: (0, 0, 0)),
                  pl.BlockSpec((SGC, H), lambda i: (0, 0))],
        out_specs=pl.BlockSpec((SGC, WD), lambda i: (i, 0)),
        out_shape=jax.ShapeDtypeStruct((T, WD), BF16), compiler_params=_cp("parallel"),
    )(proj, proj, lng, lnb, sgw, sgbt)


def _sg_bwd(proj, lng, lnb, sgw, sgbt, dy, name="sg_bwd"):
    T = proj.shape[0]

    def body(u_ref, v_ref, lng_ref, lnb_ref, w_ref, bt_ref, dy_ref,
             du_ref, dv_ref, dw_ref, dbt_ref, dlng_ref, dlnb_ref):
        @pl.when(pl.program_id(0) == 0)
        def _():
            dw_ref[...] = jnp.zeros_like(dw_ref)
            dbt_ref[...] = jnp.zeros_like(dbt_ref)
            dlng_ref[...] = jnp.zeros_like(dlng_ref)
            dlnb_ref[...] = jnp.zeros_like(dlnb_ref)

        ur, vgel_grad, rs, xh, vg = _sg_common(u_ref, v_ref, lng_ref, lnb_ref)
        ri = lax.broadcasted_iota(jnp.int32, (SGC, SGC), 0)
        ci = lax.broadcasted_iota(jnp.int32, (SGC, SGC), 1)
        ug, ug_grad = _gelu_and_grad(ur)
        dyv = dy_ref[...]
        dbt = jnp.zeros((SGC, 128), F32)
        dvg_parts = []
        for g in range(H):
            sl = slice(g * DH, (g + 1) * DH)
            ws = jnp.where(ri >= ci, w_ref[g], 0.0)
            mixed = _bdot(ws, vg[:, sl]) + bt_ref[:, g:g + 1]
            dyg = dyv[:, sl]
            du_ref[:, sl] = (dyg * mixed * ug_grad[:, sl]).astype(du_ref.dtype)
            dmix = dyg * ug[:, sl]
            dw_ref[g] += jnp.where(ri >= ci, _bdot(dmix, vg[:, sl], NT), 0.0)
            dbt = dbt + jnp.where(ci == g, _lsum(dmix), 0.0)
            dvg_parts.append(_bdot(ws, dmix, TN))
        dbt_ref[...] += dbt
        dvg = jnp.concatenate(dvg_parts, axis=1)
        dlng_ref[...] += jnp.sum(dvg * xh, axis=0, keepdims=True)
        dlnb_ref[...] += jnp.sum(dvg, axis=0, keepdims=True)
        dxh = dvg * lng_ref[...]
        dvgel = rs * (dxh - jnp.mean(dxh, axis=-1, keepdims=True) - xh * jnp.mean(dxh * xh, axis=-1, keepdims=True))
        dv_ref[...] = (dvgel * vgel_grad).astype(dv_ref.dtype)

    vec = pl.BlockSpec((1, WD), lambda i: (0, 0))
    row = pl.BlockSpec((SGC, WD), lambda i: (i, 0))
    return pl.pallas_call(
        body, name=name, grid=(T // SGC,),
        in_specs=[pl.BlockSpec((SGC, WD), lambda i: (i, OU // WD)), pl.BlockSpec((SGC, WD), lambda i: (i, OV // WD)),
                  vec, vec, pl.BlockSpec((H, SGC, SGC), lambda i: (0, 0, 0)),
                  pl.BlockSpec((SGC, H), lambda i: (0, 0)), row],
        out_specs=[row, row, pl.BlockSpec((H, SGC, SGC), lambda i: (0, 0, 0)),
                   pl.BlockSpec((SGC, 128), lambda i: (0, 0)), vec, vec],
        out_shape=[jax.ShapeDtypeStruct((T, WD), BF16), jax.ShapeDtypeStruct((T, WD), BF16),
                   jax.ShapeDtypeStruct((H, SGC, SGC), F32), jax.ShapeDtypeStruct((SGC, 128), F32),
                   jax.ShapeDtypeStruct((1, WD), F32), jax.ShapeDtypeStruct((1, WD), F32)],
        compiler_params=_cp("arbitrary"),
    )(proj, proj, lng, lnb, sgw, sgbt, dy)


def _merge_fwd(proj, yap, ybp, D, name="merge_fwd"):
    T = proj.shape[0]
    tt = _tile(T, 256, 16)

    def body(ga_ref, gb_ref, a_ref, b_ref, o_ref):
        ga, gb, a, b = [r[...].astype(F32) for r in (ga_ref, gb_ref, a_ref, b_ref)]
        o_ref[...] = (_sig(ga) * a + _sig(gb) * b).astype(o_ref.dtype)

    row = pl.BlockSpec((tt, D), lambda i: (i, 0))
    return pl.pallas_call(
        body, name=name, grid=(T // tt,),
        in_specs=[pl.BlockSpec((tt, D), lambda i: (i, OGA // D)), pl.BlockSpec((tt, D), lambda i: (i, OGA // D + 1)),
                  row, row],
        out_specs=row, out_shape=jax.ShapeDtypeStruct((T, D), BF16), compiler_params=_cp("parallel"),
    )(proj, proj, yap, ybp)


def _merge_bwd(proj, yap, ybp, dm, D, name="merge_bwd"):
    T = proj.shape[0]
    tt = _tile(T, 256, 16)

    def body(ga_ref, gb_ref, a_ref, b_ref, dm_ref, da_ref, db_ref, dga_ref, dgb_ref):
        d, ga, gb, a, b = [r[...].astype(F32) for r in (dm_ref, ga_ref, gb_ref, a_ref, b_ref)]
        sa = _sig(ga)
        sb = _sig(gb)
        da_ref[...] = (d * sa).astype(da_ref.dtype)
        db_ref[...] = (d * sb).astype(db_ref.dtype)
        dga_ref[...] = (d * a * sa * (1.0 - sa)).astype(dga_ref.dtype)
        dgb_ref[...] = (d * b * sb * (1.0 - sb)).astype(dgb_ref.dtype)

    row = pl.BlockSpec((tt, D), lambda i: (i, 0))
    return pl.pallas_call(
        body, name=name, grid=(T // tt,),
        in_specs=[pl.BlockSpec((tt, D), lambda i: (i, OGA // D)), pl.BlockSpec((tt, D), lambda i: (i, OGA // D + 1)),
                  row, row, row],
        out_specs=[row] * 4, out_shape=[jax.ShapeDtypeStruct((T, D), BF16)] * 4,
        compiler_params=_cp("parallel"),
    )(proj, proj, yap, ybp, dm)


def _ffn_act_fwd(gp, up, cw, cb, name="ffn_act_fwd"):
    T, F = gp.shape

    def body(g_ref, u_ref, w_ref, b_ref, o_ref):
        gv = g_ref[...].astype(F32)
        w = w_ref[...]
        c = gv * w[FF_K - 1:FF_K, :] + b_ref[...]
        for k in range(1, FF_K):
            c = c + _shift_down(gv, k) * w[FF_K - 1 - k:FF_K - k, :]
        o_ref[...] = (c * _sig(c) * u_ref[...].astype(F32)).astype(o_ref.dtype)

    col = pl.BlockSpec((T, 128), lambda j: (0, j))
    return pl.pallas_call(
        body, name=name, grid=(F // 128,),
        in_specs=[col, col, pl.BlockSpec((FF_K, 128), lambda j: (0, j)), pl.BlockSpec((1, 128), lambda j: (0, j))],
        out_specs=col, out_shape=jax.ShapeDtypeStruct((T, F), BF16), compiler_params=_cp("parallel"),
    )(gp, up, cw, cb)


def _ffn_act_bwd(gp, up, cw, cb, dact, name="ffn_act_bwd"):
    T, F = gp.shape

    def body(g_ref, u_ref, w_ref, b_ref, d_ref, dg_ref, du_ref, dw_ref, db_ref):
        gv = g_ref[...].astype(F32)
        w = w_ref[...]
        shifted = [_shift_down(gv, k) for k in range(FF_K)]
        c = shifted[0] * w[FF_K - 1:FF_K, :] + b_ref[...]
        for k in range(1, FF_K):
            c = c + shifted[k] * w[FF_K - 1 - k:FF_K - k, :]
        d = d_ref[...].astype(F32)
        act, act_grad = _silu_and_grad(c)
        du_ref[...] = (d * act).astype(du_ref.dtype)
        dc = d * u_ref[...].astype(F32) * act_grad
        dg = dc * w[FF_K - 1:FF_K, :]
        for k in range(1, FF_K):
            dg = dg + _shift_up(dc, k) * w[FF_K - 1 - k:FF_K - k, :]
        dg_ref[...] = dg.astype(dg_ref.dtype)
        rows = [jnp.sum(dc * shifted[FF_K - 1 - t], axis=0, keepdims=True) for t in range(FF_K)]
        dw_ref[...] = jnp.concatenate(rows, axis=0)
        db_ref[...] = jnp.sum(dc, axis=0, keepdims=True)

    col = pl.BlockSpec((T, 128), lambda j: (0, j))
    wspec = pl.BlockSpec((FF_K, 128), lambda j: (0, j))
    bspec = pl.BlockSpec((1, 128), lambda j: (0, j))
    return pl.pallas_call(
        body, name=name, grid=(F // 128,),
        in_specs=[col, col, wspec, bspec, col], out_specs=[col, col, wspec, bspec],
        out_shape=[jax.ShapeDtypeStruct((T, F), BF16), jax.ShapeDtypeStruct((T, F), BF16),
                   jax.ShapeDtypeStruct((FF_K, F), F32), jax.ShapeDtypeStruct((1, F), F32)],
        compiler_params=_cp("parallel"),
    )(gp, up, cw, cb, dact)


class _Carrier:
    def __init__(self, plan=None, deliver=None):
        self.plan, self.deliver = plan or (lambda kernel: None), deliver

    def run(self, kernel, fn, **kw):
        comm = self.plan(kernel)
        out = fn(comm=comm, **kw)
        if comm:
            self.deliver(kernel, out[-1])
            out = out[:-1]
            return out[0] if len(out) == 1 else out
        return out


def _layer_fwd(x, w, carrier=None):
    cr = carrier or _Carrier()
    D = x.shape[1]
    oba = OGA + 2 * D
    h = _rms_fwd(x, w["norm1_g"], "rms1_fwd")
    proj = cr.run("proj", functools.partial(_mm, h, w["w_in_t"], "nt", BF16, name="mm_proj"))
    bg = _ba_fwd(proj, w["alog_row"], w["dtb_row"], oba)
    qkv = _dn_prep_fwd(proj, w["dn_conv_w"])
    r = cr.run("dn_core", functools.partial(_dn_core_fwd, qkv, bg))
    o, s_all, tm_all = r[0], r[1], r[2]
    ya = _dn_post_fwd(o, proj, w["dn_onorm_g"])
    yb = _sg_fwd(proj, w["sg_ln_g"], w["sg_ln_b"], w["sg_w"], w["sg_bt"])
    yap = _mm(ya, w["w_branch_a"], "nn", BF16, name="mm_branch")
    ybp = _mm(yb, w["w_branch_b"], "nn", BF16, name="mm_branch")
    merged = _merge_fwd(proj, yap, ybp, D)
    x1 = _mm(merged, w["w_out"], "nn", F32, add=x, name="mm_out")
    h2 = _rms_fwd(x1, w["norm2_g"], "rms2_fwd")
    gp = cr.run("ffn_gate", functools.partial(_mm, h2, w["ffn_w_gate"], "nn", BF16, name="mm_ffn_in"))
    up = cr.run("ffn_up", functools.partial(_mm, h2, w["ffn_w_up"], "nn", BF16, name="mm_ffn_in"))
    act = _ffn_act_fwd(gp, up, w["ffn_conv_w"], w["ffn_conv_b"])
    x2 = cr.run("ffn_down", functools.partial(_mm, act, w["ffn_w_down"], "nn", F32, add=x1, name="mm_ffn_down"))
    saved = dict(x=x, h=h, proj=proj, bg=bg, qkv=qkv, o=o, s_all=s_all, tm_all=tm_all, ya=ya, yb=yb, yap=yap,
                 ybp=ybp, merged=merged, x1=x1, h2=h2, gp=gp, up=up, act=act)
    return x2, saved


def _layer_bwd(dx2, w, s, carrier=None, ffn_grads_ready=None, rest_grads_ready=None):
    cr = carrier or _Carrier()
    D = dx2.shape[1]
    oba = OGA + 2 * D
    g = {}
    dx2b = dx2.astype(BF16)
    dact = cr.run("d_act", functools.partial(_mm, dx2b, w["ffn_w_down"], "nt", BF16, name="mm_d_act"))
    g["ffn_w_down"] = _mm(s["act"], dx2b, "tn", BF16, name="mm_dw_down")
    dgp, dup, g["ffn_conv_w"], g["ffn_conv_b"] = _ffn_act_bwd(s["gp"], s["up"], w["ffn_conv_w"], w["ffn_conv_b"], dact)
    dh2 = _mm(dgp, w["ffn_w_gate"], "nt", F32, name="mm_dh2")
    dh2 = _mm(dup, w["ffn_w_up"], "nt", F32, add=dh2, name="mm_dh2_acc")
    g["ffn_w_gate"] = _mm(s["h2"], dgp, "tn", BF16, name="mm_dw_ffn_in")
    g["ffn_w_up"] = _mm(s["h2"], dup, "tn", BF16, name="mm_dw_ffn_in")
    if ffn_grads_ready:
        ffn_grads_ready(g)
    dx1, g["norm2_g"] = _rms_bwd(s["x1"], w["norm2_g"], dh2, dx2, "rms2_bwd")
    dx1b = dx1.astype(BF16)
    dm = _mm(dx1b, w["w_out"], "nt", BF16, name="mm_d_merged")
    g["w_out"] = _mm(s["merged"], dx1b, "tn", BF16, name="mm_dw_out")
    dyap, dybp, dga, dgb = _merge_bwd(s["proj"], s["yap"], s["ybp"], dm, D)
    dya = _mm(dyap, w["w_branch_a"], "nt", F32, name="mm_d_branch")
    dyb = _mm(dybp, w["w_branch_b"], "nt", F32, name="mm_d_branch")
    g["w_branch_a"] = _mm(s["ya"], dyap, "tn", BF16, name="mm_dw_branch")
    g["w_branch_b"] = _mm(s["yb"], dybp, "tn", BF16, name="mm_dw_branch")
    du, dv, g["sg_w"], dbt, g["sg_ln_g"], g["sg_ln_b"] = _sg_bwd(
        s["proj"], w["sg_ln_g"], w["sg_ln_b"], w["sg_w"], w["sg_bt"], dyb)
    g["sg_b"] = jnp.transpose(dbt[:, :H])
    do, dz, g["dn_onorm_g"] = _dn_post_bwd(s["o"], s["proj"], w["dn_onorm_g"], dya)
    r = cr.run("dn_core", functools.partial(_dn_core_bwd, s["qkv"], s["bg"], s["s_all"], s["tm_all"], do))
    dq, dk, dvv, dbg = r[0], r[1], r[2], r[3]
    dqkv, g["dn_conv_w"] = _dn_prep_bwd(s["proj"], w["dn_conv_w"], dq, dk, dvv)
    dba, dal, ddt = _ba_bwd(s["proj"], w["alog_row"], w["dtb_row"], dbg, oba)
    g["dn_a_log"] = dal[0, H:2 * H]
    g["dn_dt_bias"] = ddt[0, H:2 * H]
    dproj = jnp.concatenate([dqkv, dz, du, dv, dga, dgb, dba], axis=1)
    g["w_in_t"] = cr.run("dw_in", functools.partial(_mm, dproj, s["h"], "tn", BF16, name="mm_dw_in"))
    if rest_grads_ready:
        rest_grads_ready(g)
    dh = cr.run("dh", functools.partial(_mm, dproj, w["w_in_t"], "nn", F32, name="mm_dh"))
    dx, g["norm1_g"] = _rms_bwd(s["x"], w["norm1_g"], dh, dx1, "rms1_bwd")
    return dx, g


def _local_step(x, tgt, layers, final_g):
    saved = []
    for w in layers:
        x, s = _layer_fwd(x, w)
        saved.append(s)
    dx, dgf, loss = _loss_head(x, final_g, tgt)
    grads = [None] * len(layers)
    for l in reversed(range(len(layers))):
        dx, grads[l] = _layer_bwd(dx, layers[l], saved[l])
    return loss[0, 0], dx, grads, dgf


def _w_in_pad(wt):
    c1 = 4 * WD
    return jnp.concatenate([wt[:c1], wt[c1 + 2 * H:], wt[c1:c1 + 2 * H],
                            jnp.zeros((128 - 2 * H, wt.shape[1]), wt.dtype)], axis=0)


def _w_in_unpad(gt):
    c1 = 4 * WD
    n = gt.shape[0] - 128
    return jnp.concatenate([gt[:c1], gt[n:n + 2 * H], gt[c1:n]], axis=0)


def _row128(v, off):
    return jnp.pad(v, (off, 128 - off - v.shape[0]))[None]


def _prep_small(p):
    return dict(
        norm1_g=p["norm1_g"][None], alog_row=_row128(p["dn_a_log"], H), dtb_row=_row128(p["dn_dt_bias"], H),
        dn_conv_w=p["dn_conv_w"], dn_onorm_g=p["dn_onorm_g"][None],
        sg_ln_g=p["sg_ln_g"][None], sg_ln_b=p["sg_ln_b"][None], sg_w=p["sg_w"], sg_bt=jnp.transpose(p["sg_b"]),
        norm2_g=p["norm2_g"][None], ffn_conv_w=p["ffn_conv_w"], ffn_conv_b=p["ffn_conv_b"][None])


def _prep_layer(p):
    return dict(_prep_small(p), w_in_t=_w_in_pad(p["w_in_t"]),
                **{n: p[n] for n in ("w_branch_a", "w_branch_b", "w_out", "ffn_w_gate", "ffn_w_up", "ffn_w_down")})


HBM_SPEC = pl.BlockSpec(memory_space=pltpu.HBM)


def _coords():
    return lax.axis_index("x"), lax.axis_index("y"), lax.axis_index("c")


def _other_chips(x, y):
    return [(1 - x, y), (x, 1 - y), (1 - x, 1 - y)]


def _remote(src, dst, send_sems, recv_sems, k, dev):
    return pltpu.make_async_remote_copy(src_ref=src, dst_ref=dst, send_sem=send_sems.at[k], recv_sem=recv_sems.at[k],
                                        device_id=dev, device_id_type=MESH)


def _ag_copies(w_refs, o_refs, send_sems, recv_sems):
    x, y, c = _coords()
    me = 2 * x + y
    chips = _other_chips(x, y)

    def ici(k, j, owner):
        chip = chips[j]
        return _remote(w_refs[k].at[c], o_refs[k].at[owner, c], send_sems, recv_sems, 6 * k + j, (chip[0], chip[1], c))

    def d2d(k, j, part):
        owner = 2 * chips[j][0] + chips[j][1]
        return _remote(o_refs[k].at[owner, part], o_refs[k].at[owner, part], send_sems, recv_sems, 6 * k + 3 + j,
                       (x, y, 1 - c))

    n = len(w_refs)
    return me, c, chips, ici, d2d, [(k, j) for k in range(n) for j in range(3)]


def _ag_start(w_refs, o_refs, send_sems, recv_sems):
    me, _, _, ici, _, pairs = _ag_copies(w_refs, o_refs, send_sems, recv_sems)
    for k, j in pairs:
        ici(k, j, me).start()


def _ag_finish(w_refs, o_refs, send_sems, recv_sems):
    me, c, chips, ici, d2d, pairs = _ag_copies(w_refs, o_refs, send_sems, recv_sems)
    for k, j in pairs:
        ici(k, j, 2 * chips[j][0] + chips[j][1]).wait_recv()
        d2d(k, j, c).start()
    for k, j in pairs:
        d2d(k, j, 1 - c).wait_recv()
    for k, j in pairs:
        ici(k, j, me).wait_send()
        d2d(k, j, c).wait_send()


def _ag_specs(ws):
    n = len(ws)
    return dict(out_shape=[jax.ShapeDtypeStruct((N_CHIPS,) + w.shape, w.dtype) for w in ws],
                specs=[HBM_SPEC] * n, sems=[pltpu.SemaphoreType.DMA((6 * n,)), pltpu.SemaphoreType.DMA((6 * n,))])


def _ag_layers(ws):
    n = len(ws)

    def body(*refs):
        _ag_start(refs[:n], refs[n:2 * n], *refs[2 * n:])
        _ag_finish(refs[:n], refs[n:2 * n], *refs[2 * n:])

    sp = _ag_specs(ws)
    return pl.pallas_call(
        body, name="ag_weights", out_shape=sp["out_shape"], in_specs=sp["specs"], out_specs=sp["specs"],
        scratch_shapes=sp["sems"],
    )(*ws)


def _rs_pair_exchange(Gs):
    n = len(Gs)

    def body(*refs):
        g_refs, b_refs = refs[:n], refs[n:2 * n]
        send_sems, recv_sems = refs[2 * n:]
        x, y, c = _coords()
        cps = [_remote(g_refs[k].at[i, 1 - c], b_refs[k].at[i], send_sems, recv_sems, N_CHIPS * k + i, (x, y, 1 - c))
               for k in range(n) for i in range(N_CHIPS)]
        for cp in cps:
            cp.start()
        for cp in cps:
            cp.wait()

    return pl.pallas_call(
        body, name="rs_pair_exchange",
        out_shape=[jax.ShapeDtypeStruct((N_CHIPS,) + g.shape[2:], g.dtype) for g in Gs],
        in_specs=[HBM_SPEC] * n, out_specs=[HBM_SPEC] * n,
        scratch_shapes=[pltpu.SemaphoreType.DMA((N_CHIPS * n,)), pltpu.SemaphoreType.DMA((N_CHIPS * n,))],
    )(*Gs)


def _rs_add_pair(G, B, c, name):
    _, _, R, C = G.shape
    tr = _tile(R, 256, 16)

    def body(c_ref, g_ref, b_ref, o_ref):
        o_ref[0] = (g_ref[0, 0].astype(F32) + b_ref[0].astype(F32)).astype(o_ref.dtype)

    grid_spec = pltpu.PrefetchScalarGridSpec(
        num_scalar_prefetch=1, grid=(N_CHIPS, R // tr),
        in_specs=[pl.BlockSpec((1, 1, tr, C), lambda i, r, c_ref: (i, c_ref[0], r, 0)),
                  pl.BlockSpec((1, tr, C), lambda i, r, c_ref: (i, r, 0))],
        out_specs=pl.BlockSpec((1, tr, C), lambda i, r, c_ref: (i, r, 0)))
    return pl.pallas_call(
        body, name=name, grid_spec=grid_spec, out_shape=jax.ShapeDtypeStruct((N_CHIPS, R, C), G.dtype),
        compiler_params=_cp("parallel", "parallel"),
    )(jnp.reshape(c, (1,)).astype(jnp.int32), G, B)


def _rsx_copies(p_refs, b_refs, send_sems, recv_sems):
    x, y, c = _coords()
    me = 2 * x + y
    chips = _other_chips(x, y)

    def cp(k, j, src_slot, dst_slot):
        return _remote(p_refs[k].at[src_slot], b_refs[k].at[dst_slot], send_sems, recv_sems, 3 * k + j,
                       (chips[j][0], chips[j][1], c))

    return me, chips, cp, [(k, j) for k in range(len(p_refs)) for j in range(3)]


def _rsx_start(p_refs, b_refs, send_sems, recv_sems):
    me, chips, cp, pairs = _rsx_copies(p_refs, b_refs, send_sems, recv_sems)
    for k, j in pairs:
        cp(k, j, 2 * chips[j][0] + chips[j][1], me).start()


def _rsx_finish(p_refs, b_refs, send_sems, recv_sems):
    me, chips, cp, pairs = _rsx_copies(p_refs, b_refs, send_sems, recv_sems)
    for k, j in pairs:
        owner = 2 * chips[j][0] + chips[j][1]
        cp(k, j, owner, owner).wait_recv()
    for k, j in pairs:
        cp(k, j, 2 * chips[j][0] + chips[j][1], me).wait_send()


def _rsx_specs(Ps):
    n = len(Ps)
    return dict(out_shape=[jax.ShapeDtypeStruct(p.shape, p.dtype) for p in Ps], specs=[HBM_SPEC] * n,
                sems=[pltpu.SemaphoreType.DMA((3 * n,)), pltpu.SemaphoreType.DMA((3 * n,))])


def _rs_sum_chips(P, B, me, name):
    _, R, C = P.shape
    tr = _tile(R, 256, 16)

    def body(me_ref, p_ref, b1_ref, b2_ref, b3_ref, o_ref):
        o_ref[...] = ((p_ref[0].astype(F32) + b1_ref[0].astype(F32)) + b2_ref[0].astype(F32)) + b3_ref[0].astype(F32)

    slot = lambda d: pl.BlockSpec((1, tr, C), lambda r, me_ref: ((me_ref[0] + d) % N_CHIPS, r, 0))
    grid_spec = pltpu.PrefetchScalarGridSpec(
        num_scalar_prefetch=1, grid=(R // tr,), in_specs=[slot(0), slot(1), slot(2), slot(3)],
        out_specs=pl.BlockSpec((tr, C), lambda r, me_ref: (r, 0)))
    return pl.pallas_call(
        body, name=name, grid_spec=grid_spec, out_shape=jax.ShapeDtypeStruct((R, C), F32),
        compiler_params=_cp("parallel"),
    )(jnp.reshape(me, (1,)).astype(jnp.int32), P, B, B, B)


def _sum_slots(B, name):
    S, R, C = B.shape
    tr = _tile(R, 256, 16)

    def body(b_ref, o_ref):
        acc = b_ref[0].astype(F32)
        for i in range(1, S):
            acc = acc + b_ref[i].astype(F32)
        o_ref[...] = acc

    return pl.pallas_call(
        body, name=name, grid=(R // tr,), in_specs=[pl.BlockSpec((S, tr, C), lambda r: (0, r, 0))],
        out_specs=pl.BlockSpec((tr, C), lambda r: (r, 0)), out_shape=jax.ShapeDtypeStruct((R, C), F32),
        compiler_params=_cp("parallel"),
    )(B)


def _rs_pair_swap(Rs):
    n = len(Rs)

    def body(*refs):
        r_refs, o_refs = refs[:n], refs[n:2 * n]
        send_sems, recv_sems = refs[2 * n:]
        x, y, c = _coords()
        cps = [_remote(r_refs[k], o_refs[k], send_sems, recv_sems, k, (x, y, 1 - c)) for k in range(n)]
        for cp in cps:
            cp.start()
        for cp in cps:
            cp.wait()

    return pl.pallas_call(
        body, name="rs_pair_swap", out_shape=[jax.ShapeDtypeStruct(r.shape, r.dtype) for r in Rs],
        in_specs=[HBM_SPEC] * n, out_specs=[HBM_SPEC] * n,
        scratch_shapes=[pltpu.SemaphoreType.DMA((n,)), pltpu.SemaphoreType.DMA((n,))],
    )(*Rs)


def _ag8(v):
    R = v.shape[0]

    def body(v_ref, out_ref, send_sems, recv_sems, local_sem):
        x, y, c = _coords()
        me, sib = (x, y, c), (x, y, 1 - c)
        chips = _other_chips(x, y)

        def slot(p):
            return out_ref.at[4 * p[0] + 2 * p[1] + p[2]]

        def copy(k, block, to, src=None):
            return _remote(slot(block) if src is None else src, slot(block), send_sems, recv_sems, k, to)

        mine = pltpu.make_async_copy(v_ref, slot(me), local_sem)
        mine.start()
        first = [copy(0, me, sib, src=v_ref)]
        first += [copy(1 + j, me, (chip[0], chip[1], c), src=v_ref) for j, chip in enumerate(chips)]
        for cp in first:
            cp.start()
        passed = [copy(4 + j, (chip[0], chip[1], c), sib) for j, chip in enumerate(chips)]
        for j, chip in enumerate(chips):
            copy(1 + j, (chip[0], chip[1], c), me).wait_recv()
            passed[j].start()
        copy(0, sib, me).wait_recv()
        for j, chip in enumerate(chips):
            copy(4 + j, (chip[0], chip[1], 1 - c), me).wait_recv()
        for cp in first + passed:
            cp.wait_send()
        mine.wait()

    return pl.pallas_call(
        body, name="ag8_small", out_shape=jax.ShapeDtypeStruct((8, R, 128), v.dtype),
        in_specs=[pl.BlockSpec(memory_space=pltpu.VMEM)], out_specs=pl.BlockSpec(memory_space=pltpu.VMEM),
        scratch_shapes=[pltpu.SemaphoreType.DMA((7,)), pltpu.SemaphoreType.DMA((7,)), pltpu.SemaphoreType.DMA],
        compiler_params=pltpu.CompilerParams(vmem_limit_bytes=VMEM_LIMIT),
    )(v)


def _adamw(w, g, m, v, name):
    L, R, C = w.shape
    rows = [R] + [t for t in range(8, min(R, 1024) + 1, 8) if R % t == 0]
    cols = [C] + [t for t in range(128, C, 128) if C % t == 0]
    lead = [t for t in range(1, L + 1) if L % t == 0]
    fits = [(a * r * c, c, r, a) for a in lead for r in rows for c in cols if a * r * c * 4 <= 3 << 19]
    _, tc, tr, tl = max(fits) if fits else (0, min(cols), min(rows), 1)

    def body(w_ref, g_ref, m_ref, v_ref, d_ref, mo_ref, vo_ref):
        gv = g_ref[...]
        m2 = ADAM_B1 * m_ref[...] + (1.0 - ADAM_B1) * gv
        v2 = ADAM_B2 * v_ref[...] + (1.0 - ADAM_B2) * jnp.square(gv)
        m_hat = m2 / (1.0 - ADAM_B1 ** ADAM_STEP)
        v_hat = v2 / (1.0 - ADAM_B2 ** ADAM_STEP)
        d_ref[...] = -ADAM_LR * (m_hat / (jnp.sqrt(v_hat) + ADAM_EPS) + ADAM_WD * w_ref[...])
        mo_ref[...] = m2
        vo_ref[...] = v2

    blk = pl.BlockSpec((tl, tr, tc), lambda l, r, j: (l, r, j))
    return pl.pallas_call(
        body, name=name, grid=(L // tl, R // tr, C // tc), in_specs=[blk] * 4, out_specs=[blk] * 3,
        out_shape=[jax.ShapeDtypeStruct(w.shape, F32)] * 3,
        compiler_params=_cp("parallel", "parallel", "parallel"),
    )(w, g, m, v)


def _adamw_halves(w, g_mine, g_other, c, m, v, name):
    L, _, R, C = w.shape
    tr = _tile(R, 128, 8)

    def body(c_ref, w_ref, *rest):
        g_refs = rest[:2 * L]
        m_ref, v_ref, g_ref, d_ref, mo_ref, vo_ref = rest[2 * L:]
        l, h = pl.program_id(0), pl.program_id(1)
        gm, go = g_refs[0][...], g_refs[L][...]
        for i in range(1, L):
            gm = jnp.where(l == i, g_refs[i][...], gm)
            go = jnp.where(l == i, g_refs[L + i][...], go)
        gv = jnp.where(h == c_ref[0], gm, go)[None, None]
        g_ref[...] = gv
        m2 = ADAM_B1 * m_ref[...] + (1.0 - ADAM_B1) * gv
        v2 = ADAM_B2 * v_ref[...] + (1.0 - ADAM_B2) * jnp.square(gv)
        m_hat = m2 / (1.0 - ADAM_B1 ** ADAM_STEP)
        v_hat = v2 / (1.0 - ADAM_B2 ** ADAM_STEP)
        d_ref[...] = -ADAM_LR * (m_hat / (jnp.sqrt(v_hat) + ADAM_EPS) + ADAM_WD * w_ref[...])
        mo_ref[...] = m2
        vo_ref[...] = v2

    blk = pl.BlockSpec((1, 1, tr, C), lambda l, h, r, c_ref: (l, h, r, 0))

    def gblk(i, mine):
        def index(l, h, r, c_ref):
            use = jnp.logical_and(l == i, (h == c_ref[0]) == mine)
            return (jnp.where(use, r, 0), 0)
        return pl.BlockSpec((tr, C), index)

    grid_spec = pltpu.PrefetchScalarGridSpec(
        num_scalar_prefetch=1, grid=(L, 2, R // tr),
        in_specs=[blk] + [gblk(i, True) for i in range(L)] + [gblk(i, False) for i in range(L)] + [blk, blk],
        out_specs=[blk] * 4)
    return pl.pallas_call(
        body, name=name, grid_spec=grid_spec, out_shape=[jax.ShapeDtypeStruct(w.shape, F32)] * 4,
        compiler_params=_cp("parallel", "parallel", "parallel"),
    )(jnp.reshape(c, (1,)).astype(jnp.int32), w, *g_mine, *g_other, m, v)


BIG = ("w_in", "w_branch_a", "w_branch_b", "w_out", "ffn_w_gate", "ffn_w_up", "ffn_w_down")
ROW_SHARDED = ("w_out", "ffn_w_down")
SMALL = ("norm1_g", "dn_conv_w", "dn_a_log", "dn_dt_bias", "dn_onorm_g", "sg_ln_g", "sg_ln_b", "sg_w", "sg_b",
         "norm2_g", "ffn_conv_w", "ffn_conv_b", "final_norm_g")
SMALL_SHARDED = ("dn_conv_w", "ffn_conv_w")


def _pack_rows(arrs, mult):
    flat = jnp.concatenate([jnp.reshape(a, (-1,)) for a in arrs])
    n = flat.shape[0]
    rows = -(-n // (128 * mult)) * mult
    return jnp.reshape(jnp.pad(flat, (0, rows * 128 - n)), (rows, 128))


def _unpack(flat2d, shapes):
    flat = jnp.reshape(flat2d, (-1,))
    out, off = [], 0
    for shp in shapes:
        n = math.prod(shp)
        out.append(jnp.reshape(flat[off:off + n], shp))
        off += n
    return out


def kernel(x, norm1_g, w_in, dn_conv_w, dn_a_log, dn_dt_bias, dn_onorm_g, sg_ln_g, sg_ln_b, sg_w, sg_b, w_branch_a, w_branch_b, w_out, norm2_g, ffn_w_gate, ffn_w_up, ffn_conv_w, ffn_conv_b, ffn_w_down, final_norm_g, loss_target, m_norm1_g, m_w_in, m_dn_conv_w, m_dn_a_log, m_dn_dt_bias, m_dn_onorm_g, m_sg_ln_g, m_sg_ln_b, m_sg_w, m_sg_b, m_w_branch_a, m_w_branch_b, m_w_out, m_norm2_g, m_ffn_w_gate, m_ffn_w_up, m_ffn_conv_w, m_ffn_conv_b, m_ffn_w_down, m_final_norm_g, v_norm1_g, v_w_in, v_dn_conv_w, v_dn_a_log, v_dn_dt_bias, v_dn_onorm_g, v_sg_ln_g, v_sg_ln_b, v_sg_w, v_sg_b, v_w_branch_a, v_w_branch_b, v_w_out, v_norm2_g, v_ffn_w_gate, v_ffn_w_up, v_ffn_conv_w, v_ffn_conv_b, v_ffn_w_down, v_final_norm_g):
    W = dict(norm1_g=norm1_g, w_in=w_in, dn_conv_w=dn_conv_w, dn_a_log=dn_a_log, dn_dt_bias=dn_dt_bias,
             dn_onorm_g=dn_onorm_g, sg_ln_g=sg_ln_g, sg_ln_b=sg_ln_b, sg_w=sg_w, sg_b=sg_b, w_branch_a=w_branch_a,
             w_branch_b=w_branch_b, w_out=w_out, norm2_g=norm2_g, ffn_w_gate=ffn_w_gate, ffn_w_up=ffn_w_up,
             ffn_conv_w=ffn_conv_w, ffn_conv_b=ffn_conv_b, ffn_w_down=ffn_w_down, final_norm_g=final_norm_g)
    M = dict(norm1_g=m_norm1_g, w_in=m_w_in, dn_conv_w=m_dn_conv_w, dn_a_log=m_dn_a_log, dn_dt_bias=m_dn_dt_bias,
             dn_onorm_g=m_dn_onorm_g, sg_ln_g=m_sg_ln_g, sg_ln_b=m_sg_ln_b, sg_w=m_sg_w, sg_b=m_sg_b,
             w_branch_a=m_w_branch_a, w_branch_b=m_w_branch_b, w_out=m_w_out, norm2_g=m_norm2_g,
             ffn_w_gate=m_ffn_w_gate, ffn_w_up=m_ffn_w_up, ffn_conv_w=m_ffn_conv_w, ffn_conv_b=m_ffn_conv_b,
             ffn_w_down=m_ffn_w_down, final_norm_g=m_final_norm_g)
    V = dict(norm1_g=v_norm1_g, w_in=v_w_in, dn_conv_w=v_dn_conv_w, dn_a_log=v_dn_a_log, dn_dt_bias=v_dn_dt_bias,
             dn_onorm_g=v_dn_onorm_g, sg_ln_g=v_sg_ln_g, sg_ln_b=v_sg_ln_b, sg_w=v_sg_w, sg_b=v_sg_b,
             w_branch_a=v_w_branch_a, w_branch_b=v_w_branch_b, w_out=v_w_out, norm2_g=v_norm2_g,
             ffn_w_gate=v_ffn_w_gate, ffn_w_up=v_ffn_w_up, ffn_conv_w=v_ffn_conv_w, ffn_conv_b=v_ffn_conv_b,
             ffn_w_down=v_ffn_w_down, final_norm_g=v_final_norm_g)
    cx, cy, cc = _coords()
    chip = 2 * cx + cy
    L = w_in.shape[0]

    D = w_in.shape[1]
    cs_in = w_in.shape[2]
    c1 = 4 * WD
    ba_chip, ba_off = c1 // cs_in, c1 % cs_in
    assert ba_off + 2 * H <= cs_in
    n_main = N_CHIPS * cs_in - 2 * H
    main_start = [i * cs_in - (2 * H if i > ba_chip else 0) for i in range(N_CHIPS)]
    main_len = [cs_in - (2 * H if i == ba_chip else 0) for i in range(N_CHIPS)]
    tile0 = [16 * (s // 16) for s in main_start]
    shift = [s - t for s, t in zip(main_start, tile0)]
    rp_in = -(-max(sh + ln for sh, ln in zip(shift, main_len)) // 32) * 32
    seg = [tile0[i + 1] - tile0[i] for i in range(N_CHIPS - 1)] + [n_main - tile0[-1]]
    assert all(s + 16 <= rp_in for s in seg[:-1]) and seg[-1] <= rp_in and tile0[-1] + rp_in <= n_main + 128
    my_shift = jnp.asarray(shift, jnp.int32)[chip]

    mine = {n: W[n].astype(BF16) for n in BIG if n != "w_in"}
    wt = jnp.swapaxes(W["w_in"], 1, 2).astype(BF16)
    ba = wt[:, ba_off:ba_off + 2 * H]
    local_row = lax.broadcasted_iota(jnp.int32, (cs_in, 1), 0)
    without_ba = jnp.where(local_row < ba_off, wt, jnp.pad(wt[:, 2 * H:], ((0, 0), (0, 2 * H), (0, 0))))
    mine["w_in"] = lax.dynamic_update_slice(jnp.zeros((L, rp_in, D), BF16),
                                            jnp.where(chip == ba_chip, without_ba, wt), (0, my_shift, 0))
    mine["w_ba"] = jnp.pad(jnp.where(chip == ba_chip, ba, jnp.zeros_like(ba)), ((0, 0), (0, 32 - 2 * H), (0, 0)))

    def halves(a, lead=0):
        return jnp.reshape(a, a.shape[:lead] + (2, a.shape[lead] // 2) + a.shape[lead + 1:])

    taps = _ag8(_pack_rows([W[n] for n in SMALL_SHARDED], 16))
    tap_shards = [_unpack(taps[2 * i], [W[n].shape for n in SMALL_SHARDED]) for i in range(N_CHIPS)]
    taps_full = {n: jnp.concatenate([tap_shards[i][k] for i in range(N_CHIPS)], axis=-1)
                 for k, n in enumerate(SMALL_SHARDED)}

    ops = []
    for l in range(L):
        p = {n: W[n][l] for n in W if n not in ("final_norm_g",) + BIG + SMALL_SHARDED}
        p.update({n: taps_full[n][l] for n in SMALL_SHARDED})
        ops.append(_prep_small(p))

    def gather_payload(items):
        return ("gather", [halves(mine[n][l]) for l, n in items])

    def weights_landed(items, gathered):
        got = {}
        for (l, n), a in zip(items, gathered):
            a = jnp.reshape(a, (N_CHIPS,) + mine[n].shape[1:])
            got[(l, n)] = [jnp.where(chip == i, mine[n][l], a[i]) for i in range(N_CHIPS)]
        for (l, n), parts in got.items():
            if n == "w_in":
                pieces = [parts[0][:seg[0]]]
                for i in range(1, N_CHIPS):
                    pieces += [parts[i][:16] + parts[i - 1][seg[i - 1]:seg[i - 1] + 16], parts[i][16:seg[i]]]
                ba_rows = got[(l, "w_ba")][ba_chip][:2 * H]
                ops[l]["w_in_t"] = jnp.concatenate(pieces + [ba_rows, jnp.zeros((128 - 2 * H, D), BF16)], axis=0)
            elif n != "w_ba":
                ops[l][n] = jnp.concatenate(parts, axis=0 if n in ROW_SHARDED else 1)

    partial_sums, chip_sums = {}, {}

    def grad_partials(l, names, g):
        Gs = []
        for n in names:
            if n == "w_in":
                a = jnp.stack([g["w_in_t"][t:t + rp_in] for t in tile0])
            elif n == "w_ba":
                a = jnp.broadcast_to(g["w_in_t"][n_main:n_main + 32][None], (N_CHIPS, 32, D))
            elif n in ROW_SHARDED:
                a = jnp.reshape(g[n], (N_CHIPS, g[n].shape[0] // N_CHIPS, g[n].shape[1]))
            else:
                a = jnp.moveaxis(jnp.reshape(g[n], (g[n].shape[0], N_CHIPS, g[n].shape[1] // N_CHIPS)), 1, 0)
            Gs.append(halves(a, 1))
        B1s = _rs_pair_exchange(Gs)
        for n, a, b in zip(names, Gs, B1s):
            partial_sums[(l, n)] = _rs_add_pair(a, b, cc, "rs_add_pair_" + n)

    def carrier(l, plan, landed):
        def payload(kernel):
            items = plan.get((l, kernel))
            if not items:
                return None
            return gather_payload(items) if landed is weights_landed else ("exchange", [partial_sums[i] for i in items])
        return _Carrier(payload, lambda kernel, res: landed(plan[(l, kernel)], res))

    FFN = ("ffn_w_gate", "ffn_w_up", "ffn_w_down")
    REST = ("w_in", "w_ba", "w_branch_a", "w_branch_b", "w_out")
    fwd_plan = {(0, "proj"): [(0, "w_branch_a"), (0, "w_branch_b"), (0, "w_out"), (0, "ffn_w_gate")],
                (0, "dn_core"): [(0, "ffn_w_up"), (0, "ffn_w_down"), (1, "w_in"), (1, "w_ba")],
                (0, "ffn_gate"): [(1, "w_branch_a"), (1, "w_branch_b"), (1, "w_out")],
                (0, "ffn_up"): [(1, "ffn_w_gate")],
                (0, "ffn_down"): [(1, "ffn_w_up")],
                (1, "proj"): [(1, "ffn_w_down")]}
    bwd_plan = {(1, "dn_core"): [(1, n) for n in FFN],
                (0, "d_act"): [(1, "w_branch_a"), (1, "w_branch_b"), (1, "w_out")],
                (0, "dn_core"): [(1, "w_in"), (1, "w_ba"), (0, "ffn_w_down")],
                (0, "dw_in"): [(0, "ffn_w_gate"), (0, "ffn_w_up")],
                (0, "dh"): [(0, n) for n in REST]}

    def sums_landed(items, res):
        chip_sums.update(zip(items, res))

    first = [(0, "w_in"), (0, "w_ba")]
    weights_landed(first, _ag_layers(gather_payload(first)[1]))
    xs, saved = x[0], []
    for l in range(L):
        xs, s = _layer_fwd(xs, ops[l], carrier(l, fwd_plan, weights_landed))
        saved.append(s)
    dx, dgf, loss = _loss_head(xs, final_norm_g[None], loss_target[0])
    loss = loss[0, 0]
    grads = [None] * L
    for l in reversed(range(L)):
        dx, grads[l] = _layer_bwd(dx, ops[l], saved[l], carrier(l, bwd_plan, sums_landed),
                                  functools.partial(grad_partials, l, FFN), functools.partial(grad_partials, l, REST))
    travelled = BIG + ("w_ba",)
    g_mine = [[_rs_sum_chips(partial_sums[(l, n)], chip_sums[(l, n)], chip, "rs_sum_chips_" + n) for n in travelled]
              for l in range(L)]
    swapped = _rs_pair_swap(g_mine[0] + g_mine[1])
    g_other = [swapped[:len(travelled)], swapped[len(travelled):]]

    def both_halves(l, n):
        a, b = g_mine[l][travelled.index(n)], g_other[l][travelled.index(n)]
        return jnp.where(cc == 0, jnp.concatenate([a, b]), jnp.concatenate([b, a]))

    def w_in_grad_rows(l):
        m = lax.dynamic_slice_in_dim(both_halves(l, "w_in"), my_shift, cs_in, axis=0)
        ba_rows = jnp.pad(both_halves(l, "w_ba")[:2 * H], ((ba_off, cs_in - ba_off - 2 * H), (0, 0)))
        moved = jnp.pad(m[:cs_in - 2 * H], ((2 * H, 0), (0, 0)))
        with_ba = jnp.where(local_row < ba_off, m, jnp.where(local_row < ba_off + 2 * H, ba_rows, moved))
        return jnp.where(chip == ba_chip, with_ba, m)

    small = {n: jnp.stack([g[n] for g in grads]) for n in SMALL if n != "final_norm_g"}
    small["final_norm_g"] = dgf
    shapes = [taps_full[n].shape if n in SMALL_SHARDED else W[n].shape for n in SMALL] + [(1,)]
    sflat = _pack_rows([small[n] for n in SMALL] + [jnp.reshape(loss, (1,))], 16)
    sred = _unpack(_sum_slots(_ag8(sflat), "sum_small"), shapes)
    g_small = dict(zip(SMALL, sred[:-1]))
    loss_total = sred[-1][0]
    for n in SMALL_SHARDED:
        cs = W[n].shape[-1]
        g_small[n] = lax.dynamic_slice_in_dim(g_small[n], chip * cs, cs, axis=-1)

    g_big, delta, new_m, new_v = {}, {}, {}, {}
    for k, n in enumerate(BIG):
        gm, go = [g_mine[l][k] for l in range(L)], [g_other[l][k] for l in range(L)]
        if n == "w_in":
            g_t = jnp.stack([w_in_grad_rows(l) for l in range(L)], axis=1)
            outs = _adamw(*[jnp.transpose(a, (2, 0, 1)) for a in (W[n],)], g_t,
                          *[jnp.transpose(a, (2, 0, 1)) for a in (M[n], V[n])], "adamw_" + n)
            g_big[n], delta[n], new_m[n], new_v[n] = [jnp.transpose(o, (1, 2, 0)) for o in (g_t,) + tuple(outs)]
        else:
            outs = _adamw_halves(halves(W[n], 1), gm, go, cc, halves(M[n], 1), halves(V[n], 1), "adamw_" + n)
            g_big[n], delta[n], new_m[n], new_v[n] = [jnp.reshape(o, W[n].shape) for o in outs]
    for n in SMALL:
        shp = W[n].shape
        as3d = (1,) * (3 - len(shp)) + shp if len(shp) <= 3 else (-1,) + shp[-2:]
        outs = _adamw(*[jnp.reshape(d[n], as3d) for d in (W, g_small, M, V)], "adamw_" + n)
        delta[n], new_m[n], new_v[n] = [jnp.reshape(o, shp) for o in outs]

    names = list(W)
    grad_w = {**g_big, **g_small}
    return (loss_total, dx[None], *[grad_w[n] for n in names], *[delta[n] for n in names],
            *[new_m[n] for n in names], *[new_v[n] for n in names])
```

```python
import functools
import math

import jax
import jax.numpy as jnp
from jax import lax
from jax.experimental import pallas as pl
from jax.experimental.pallas import tpu as pltpu

F32 = jnp.float32
BF16 = jnp.bfloat16
MESH = pl.DeviceIdType.MESH

EPS = 1e-6
H = 8
DH = 128
WD = H * DH
DNC = 64
SGC = 128
DN_K = 4
FF_K = 3
DEPTH = 2
N_CHIPS = 4

ADAM_LR = 0.001
ADAM_B1 = 0.9
ADAM_B2 = 0.999
ADAM_EPS = 1e-08
ADAM_WD = 0.01
ADAM_STEP = 10

VMEM_LIMIT = 56 * 1024 * 1024

NN = (((1,), (0,)), ((), ()))
NT = (((1,), (1,)), ((), ()))
TN = (((0,), (0,)), ((), ()))

OQ, OZ, OU, OV, OGA = 0, 3 * WD, 4 * WD, 5 * WD, 6 * WD


def _cp(*sem):
    return pltpu.CompilerParams(dimension_semantics=sem or None, vmem_limit_bytes=VMEM_LIMIT)


def _tile(dim, pref, unit=128):
    if dim <= pref:
        return dim
    t = (pref // unit) * unit
    while t >= unit:
        if dim % t == 0:
            return t
        t -= unit
    return dim


def _bdot(a, b, dn=NN):
    return lax.dot_general(a.astype(BF16), b.astype(BF16), dn, preferred_element_type=F32)


def _lsum(x):
    return jnp.sum(x, axis=1, keepdims=True)


def _sig(x):
    return 0.5 * jnp.tanh(0.5 * x) + 0.5


def _silu_and_grad(x):
    s = _sig(x)
    return x * s, s * (1.0 + x * (1.0 - s))


def _gelu_parts(x):
    a = jnp.abs(x) * (2.0 ** -0.5)
    t = 1.0 / (1.0 + 0.3275911 * a)
    poly = t * (0.254829592 + t * (-0.284496736 + t * (1.421413741 + t * (-1.453152027 + t * 1.061405429))))
    e = jnp.exp(-a * a)
    half = 0.5 * poly * e
    return jnp.where(x < 0, half, 1.0 - half), e * (1.0 / math.sqrt(2.0 * math.pi))


def _gelu(x):
    return x * _gelu_parts(x)[0]


def _gelu_and_grad(x):
    cdf, pdf = _gelu_parts(x)
    return x * cdf, cdf + x * pdf


def _shift_down(x, k):
    if k == 0:
        return x
    y = pltpu.roll(x, k, 0)
    rows = lax.broadcasted_iota(jnp.int32, (8, x.shape[1]), 0)
    return jnp.concatenate([jnp.where(rows >= k, y[:8], 0.0), y[8:]], axis=0)


def _shift_up(x, k):
    if k == 0:
        return x
    n = x.shape[0]
    y = pltpu.roll(x, n - k, 0)
    rows = lax.broadcasted_iota(jnp.int32, (8, x.shape[1]), 0)
    return jnp.concatenate([y[:n - 8], jnp.where(rows < 8 - k, y[n - 8:], 0.0)], axis=0)


def _comm_fns(comm):
    if not comm:
        return None, None, dict(out_shape=[], specs=[], sems=[]), ()
    kind, arrays = comm
    start, finish, specs = {"gather": (_ag_start, _ag_finish, _ag_specs),
                            "exchange": (_rsx_start, _rsx_finish, _rsx_specs)}[kind]
    return start, finish, specs(arrays), tuple(arrays)


def _mm(a, b, mode, out_dtype, add=None, name="mm", comm=None):
    if mode == "tn":
        K, M = a.shape
    else:
        M, K = a.shape
    N = b.shape[0] if mode == "nt" else b.shape[1]
    tm, tn, tk = _tile(M, 1152), _tile(N, 1536), _tile(K, 3584)
    nk = K // tk
    ni, nj = M // tm, N // tn
    dn = {"nn": NN, "nt": NT, "tn": TN}[mode]
    c_start, c_finish, c_sp, payload = _comm_fns(comm)
    nc = len(payload)
    n_add = 0 if add is None else 1

    def body(*refs):
        a_ref, b_ref = refs[:2]
        add_ref = refs[2] if n_add else None
        c_in = refs[2 + n_add:2 + n_add + nc]
        o_ref = refs[2 + n_add + nc]
        c_out = refs[3 + n_add + nc:3 + n_add + 2 * nc]
        rest = refs[3 + n_add + 2 * nc:]
        acc_ref = rest[0] if nk > 1 else None
        sems = rest[1:] if nk > 1 else rest
        i, j, k = pl.program_id(0), pl.program_id(1), pl.program_id(2)

        if nc:
            @pl.when(jnp.logical_and(jnp.logical_and(i == 0, j == 0), k == 0))
            def _():
                c_start(c_in, c_out, *sems)

        def finish(r):
            if add is not None:
                r = r + add_ref[...]
            o_ref[...] = r.astype(o_ref.dtype)

        part = lax.dot_general(a_ref[...], b_ref[...], dn, preferred_element_type=F32)
        if nk == 1:
            finish(part)
        else:
            @pl.when(k == 0)
            def _():
                acc_ref[...] = part

            @pl.when(k > 0)
            def _():
                acc_ref[...] += part

            @pl.when(k == nk - 1)
            def _():
                finish(acc_ref[...])

        if nc:
            @pl.when(jnp.logical_and(jnp.logical_and(i == ni - 1, j == nj - 1), k == nk - 1))
            def _():
                c_finish(c_in, c_out, *sems)

    a_spec = (pl.BlockSpec((tk, tm), lambda i, j, k: (k, i)) if mode == "tn"
              else pl.BlockSpec((tm, tk), lambda i, j, k: (i, k)))
    b_spec = (pl.BlockSpec((tn, tk), lambda i, j, k: (j, k)) if mode == "nt"
              else pl.BlockSpec((tk, tn), lambda i, j, k: (k, j)))
    o_spec = pl.BlockSpec((tm, tn), lambda i, j, k: (i, j))
    in_specs = [a_spec, b_spec] + ([o_spec] if add is not None else []) + c_sp["specs"]
    args = (a, b) + ((add,) if add is not None else ()) + payload
    outs = pl.pallas_call(
        body, name=name + ("_" + comm[0] if nc else ""), grid=(ni, nj, nk), in_specs=in_specs,
        out_specs=[o_spec] + c_sp["specs"],
        out_shape=[jax.ShapeDtypeStruct((M, N), out_dtype)] + c_sp["out_shape"],
        scratch_shapes=([pltpu.VMEM((tm, tn), F32)] if nk > 1 else []) + c_sp["sems"],
        compiler_params=_cp("arbitrary", "arbitrary", "arbitrary") if nc else _cp("parallel", "parallel", "arbitrary"),
    )(*args)
    return (outs[0], list(outs[1:])) if nc else outs[0]


def _rms_fwd(x, g, name):
    T, D = x.shape
    tt = _tile(T, 256, 16)

    def body(x_ref, g_ref, o_ref):
        xv = x_ref[...]
        r = lax.rsqrt(jnp.mean(xv * xv, axis=-1, keepdims=True) + EPS)
        o_ref[...] = (xv * r * g_ref[...]).astype(o_ref.dtype)

    return pl.pallas_call(
        body, name=name, grid=(T // tt,),
        in_specs=[pl.BlockSpec((tt, D), lambda i: (i, 0)), pl.BlockSpec((1, D), lambda i: (0, 0))],
        out_specs=pl.BlockSpec((tt, D), lambda i: (i, 0)),
        out_shape=jax.ShapeDtypeStruct((T, D), BF16), compiler_params=_cp("parallel"),
    )(x, g)


def _rms_bwd(x, g, dh, dres, name):
    T, D = x.shape
    tt = _tile(T, 256, 16)

    def body(x_ref, g_ref, dh_ref, dres_ref, dx_ref, dg_ref):
        @pl.when(pl.program_id(0) == 0)
        def _():
            dg_ref[...] = jnp.zeros_like(dg_ref)

        xv = x_ref[...]
        r = lax.rsqrt(jnp.mean(xv * xv, axis=-1, keepdims=True) + EPS)
        xh = xv * r
        dh_v = dh_ref[...]
        dy = dh_v * g_ref[...]
        dx_ref[...] = dres_ref[...] + r * (dy - xh * jnp.mean(dy * xh, axis=-1, keepdims=True))
        dg_ref[...] += jnp.sum(dh_v * xh, axis=0, keepdims=True)

    row = pl.BlockSpec((tt, D), lambda i: (i, 0))
    vec = pl.BlockSpec((1, D), lambda i: (0, 0))
    return pl.pallas_call(
        body, name=name, grid=(T // tt,), in_specs=[row, vec, row, row], out_specs=[row, vec],
        out_shape=[jax.ShapeDtypeStruct((T, D), F32), jax.ShapeDtypeStruct((1, D), F32)],
        compiler_params=_cp("arbitrary"),
    )(x, g, dh, dres)


def _loss_head(x, g, tgt, name="loss_head"):
    T, D = x.shape
    tt = _tile(T, 256, 16)

    def body(x_ref, g_ref, t_ref, dx_ref, dg_ref, loss_ref):
        @pl.when(pl.program_id(0) == 0)
        def _():
            dg_ref[...] = jnp.zeros_like(dg_ref)
            loss_ref[...] = jnp.zeros_like(loss_ref)

        xv = x_ref[...]
        r = lax.rsqrt(jnp.mean(xv * xv, axis=-1, keepdims=True) + EPS)
        xh = xv * r
        err = xh * g_ref[...] - t_ref[...]
        part = 0.5 * jnp.sum(jnp.mean(err * err, axis=-1, keepdims=True), axis=0, keepdims=True)
        loss_ref[...] += jnp.broadcast_to(part, loss_ref.shape)
        dy = err * (1.0 / D)
        dg_ref[...] += jnp.sum(dy * xh, axis=0, keepdims=True)
        dyh = dy * g_ref[...]
        dx_ref[...] = r * (dyh - xh * jnp.mean(dyh * xh, axis=-1, keepdims=True))

    row = pl.BlockSpec((tt, D), lambda i: (i, 0))
    vec = pl.BlockSpec((1, D), lambda i: (0, 0))
    return pl.pallas_call(
        body, name=name, grid=(T // tt,), in_specs=[row, vec, row],
        out_specs=[row, vec, pl.BlockSpec((1, 128), lambda i: (0, 0))],
        out_shape=[jax.ShapeDtypeStruct((T, D), F32), jax.ShapeDtypeStruct((1, D), F32),
                   jax.ShapeDtypeStruct((1, 128), F32)],
        compiler_params=_cp("arbitrary"),
    )(x, g, tgt)


def _ba_fwd(proj, alog, dtb, oba, name="dn_ba_fwd"):
    T = proj.shape[0]
    tt = _tile(T, 512, 8)

    def body(p_ref, al_ref, dt_ref, o_ref):
        raw = p_ref[...].astype(F32)
        lane = lax.broadcasted_iota(jnp.int32, raw.shape, 1)
        z = raw + dt_ref[...]
        sp = jnp.maximum(z, 0.0) + jnp.log(1.0 + jnp.exp(-jnp.abs(z)))
        gl = -jnp.exp(al_ref[...]) * sp
        o_ref[...] = jnp.where(lane < H, _sig(raw), jnp.where(lane < 2 * H, gl, 0.0))

    vec = pl.BlockSpec((1, 128), lambda i: (0, 0))
    return pl.pallas_call(
        body, name=name, grid=(T // tt,),
        in_specs=[pl.BlockSpec((tt, 128), lambda i: (i, oba // 128)), vec, vec],
        out_specs=pl.BlockSpec((tt, 128), lambda i: (i, 0)),
        out_shape=jax.ShapeDtypeStruct((T, 128), F32), compiler_params=_cp("parallel"),
    )(proj, alog, dtb)


def _ba_bwd(proj, alog, dtb, dbg, oba, name="dn_ba_bwd"):
    T = proj.shape[0]
    tt = _tile(T, 512, 16)

    def body(p_ref, al_ref, dt_ref, d_ref, o_ref, dal_ref, ddt_ref):
        @pl.when(pl.program_id(0) == 0)
        def _():
            dal_ref[...] = jnp.zeros_like(dal_ref)
            ddt_ref[...] = jnp.zeros_like(ddt_ref)

        raw = p_ref[...].astype(F32)
        d = d_ref[...]
        lane = lax.broadcasted_iota(jnp.int32, raw.shape, 1)
        z = raw + dt_ref[...]
        sp = jnp.maximum(z, 0.0) + jnp.log(1.0 + jnp.exp(-jnp.abs(z)))
        na = -jnp.exp(al_ref[...])
        is_g = jnp.logical_and(lane >= H, lane < 2 * H)
        b = _sig(raw)
        dz = jnp.where(is_g, d * na * _sig(z), 0.0)
        o_ref[...] = jnp.where(lane < H, d * b * (1.0 - b), dz).astype(o_ref.dtype)
        dal_ref[...] += jnp.sum(jnp.where(is_g, d * na * sp, 0.0), axis=0, keepdims=True)
        ddt_ref[...] += jnp.sum(dz, axis=0, keepdims=True)

    vec = pl.BlockSpec((1, 128), lambda i: (0, 0))
    return pl.pallas_call(
        body, name=name, grid=(T // tt,),
        in_specs=[pl.BlockSpec((tt, 128), lambda i: (i, oba // 128)), vec, vec,
                  pl.BlockSpec((tt, 128), lambda i: (i, 0))],
        out_specs=[pl.BlockSpec((tt, 128), lambda i: (i, 0)), vec, vec],
        out_shape=[jax.ShapeDtypeStruct((T, 128), BF16), jax.ShapeDtypeStruct((1, 128), F32),
                   jax.ShapeDtypeStruct((1, 128), F32)],
        compiler_params=_cp("arbitrary"),
    )(proj, alog, dtb, dbg)


def _dn_prep_fwd(proj, convw, name="dn_prep_fwd"):
    T = proj.shape[0]
    nblk = 3 * H

    def body(p_ref, w_ref, o_ref):
        j = pl.program_id(0)
        xv = p_ref[...].astype(F32)
        w = w_ref[...]
        c = xv * w[DN_K - 1:DN_K, :]
        for k in range(1, DN_K):
            c = c + _shift_down(xv, k) * w[DN_K - 1 - k:DN_K - k, :]
        s = c * _sig(c)
        r = lax.rsqrt(_lsum(s * s) + EPS)
        o_ref[...] = jnp.where(j < 2 * H, s * r, s)

    return pl.pallas_call(
        body, name=name, grid=(nblk,),
        in_specs=[pl.BlockSpec((T, DH), lambda j: (0, j)), pl.BlockSpec((DN_K, DH), lambda j: (0, j))],
        out_specs=pl.BlockSpec((T, DH), lambda j: (0, j)),
        out_shape=jax.ShapeDtypeStruct((T, 3 * WD), F32), compiler_params=_cp("parallel"),
    )(proj, convw)


def _dn_prep_bwd(proj, convw, dq, dk, dv, name="dn_prep_bwd"):
    T = proj.shape[0]
    nblk = 3 * H

    def body(p_ref, w_ref, dq_ref, dk_ref, dv_ref, dx_ref, dw_ref):
        j = pl.program_id(0)
        xv = p_ref[...].astype(F32)
        w = w_ref[...]
        shifted = [_shift_down(xv, k) for k in range(DN_K)]
        c = shifted[0] * w[DN_K - 1:DN_K, :]
        for k in range(1, DN_K):
            c = c + shifted[k] * w[DN_K - 1 - k:DN_K - k, :]
        s, s_grad = _silu_and_grad(c)
        r = lax.rsqrt(_lsum(s * s) + EPS)
        y = s * r
        dy = jnp.where(j < H, dq_ref[...], jnp.where(j < 2 * H, dk_ref[...], dv_ref[...]))
        ds = jnp.where(j < 2 * H, r * (dy - y * _lsum(dy * y)), dy)
        dc = ds * s_grad
        dx = dc * w[DN_K - 1:DN_K, :]
        for k in range(1, DN_K):
            dx = dx + _shift_up(dc, k) * w[DN_K - 1 - k:DN_K - k, :]
        dx_ref[...] = dx.astype(dx_ref.dtype)
        rows = [jnp.sum(dc * shifted[DN_K - 1 - t], axis=0, keepdims=True) for t in range(DN_K)]
        dw_ref[...] = jnp.concatenate(rows, axis=0)

    hb = lambda off: pl.BlockSpec((T, DH), lambda j: (0, jnp.maximum(jnp.minimum(j - off, H - 1), 0)))
    return pl.pallas_call(
        body, name=name, grid=(nblk,),
        in_specs=[pl.BlockSpec((T, DH), lambda j: (0, j)), pl.BlockSpec((DN_K, DH), lambda j: (0, j)),
                  hb(0), hb(H), hb(2 * H)],
        out_specs=[pl.BlockSpec((T, DH), lambda j: (0, j)), pl.BlockSpec((DN_K, DH), lambda j: (0, j))],
        out_shape=[jax.ShapeDtypeStruct((T, 3 * WD), BF16), jax.ShapeDtypeStruct((DN_K, 3 * WD), F32)],
        compiler_params=_cp("parallel"),
    )(proj, convw, dq, dk, dv)


DN_BLOCK = 4


def _split3(a):
    hi = a.astype(BF16)
    r1 = a - hi.astype(F32)
    mid = r1.astype(BF16)
    return hi, mid, (r1 - mid.astype(F32)).astype(BF16)


def _dot3(a, b, dn=NN):
    ah, al, _ = _split3(a)
    bh, bl, _ = _split3(b)
    d = lambda p, q: lax.dot_general(p, q, dn, preferred_element_type=F32)
    return d(ah, bh) + d(ah, bl) + d(al, bh)


def _mask_dot(m, b, dn=NN):
    mb = m.astype(BF16)
    d = lambda q: (lax.dot_general(mb, q, dn, preferred_element_type=F32) if dn != TN
                   else lax.dot_general(q, mb, dn, preferred_element_type=F32))
    b0, b1, b2 = _split3(b)
    return d(b0) + d(b1) + d(b2)


def _tri_inv(A):
    ri = lax.broadcasted_iota(jnp.int32, A.shape, 0)
    ci = lax.broadcasted_iota(jnp.int32, A.shape, 1)
    T = jnp.where(ri == ci, 1.0, 0.0) - jnp.where((ri // 2) == (ci // 2), A, 0.0)
    s = 2
    while s < DNC:
        off = jnp.logical_and((ri // (2 * s)) == (ci // (2 * s)), (ri // s) != (ci // s))
        T = T - _dot3(_dot3(T, jnp.where(off, A, 0.0)), T)
        s *= 2
    return T


GH = 4
NG = H // GH
GR = GH * DNC
GK = GH * DH


def _dn_masks():
    ri = lax.broadcasted_iota(jnp.int32, (GR, GR), 0)
    ci = lax.broadcasted_iota(jnp.int32, (GR, GR), 1)
    blk = (ri // DNC) == (ci // DNC)
    wide = (lax.broadcasted_iota(jnp.int32, (GR, GK), 0) // DNC) == (lax.broadcasted_iota(jnp.int32, (GR, GK), 1) // DH)
    return dict(blk=blk, causal=jnp.logical_and(blk, ri >= ci), strict=jnp.logical_and(blk, ri > ci),
                upper=jnp.logical_and(blk, ri <= ci), eye=ri == ci, wide=wide)


def _wide(a, mk):
    return jnp.where(mk["wide"], jnp.tile(a, (1, GH)), 0.0)


def _fold(a, mk):
    a = jnp.where(mk["wide"], a, 0.0)
    out = a[:, :DH]
    for j in range(1, GH):
        out = out + a[:, j * DH:(j + 1) * DH]
    return out


def _stack_heads(ref, rows, g):
    return jnp.concatenate([ref[rows, (g * GH + j) * DH:(g * GH + j + 1) * DH] for j in range(GH)], axis=0)


def _dn_group(q_ref, k_ref, v_ref, rows, bg, gc_cols, g, mk):
    heads = [g * GH + j for j in range(GH)]
    col = lambda a, lane: jnp.concatenate([a[:, lane(h):lane(h) + 1] for h in heads], axis=0)
    q = _stack_heads(q_ref, rows, g) * (DH ** -0.5)
    k = _stack_heads(k_ref, rows, g)
    v = _stack_heads(v_ref, rows, g)
    beta = col(bg, lambda h: h)
    gcol = col(gc_cols, lambda h: H + h)
    last = [gc_cols[DNC - 1:DNC, H + h:H + h + 1] for h in heads]
    gl = jnp.concatenate([jnp.broadcast_to(t, (DNC, 1)) for t in last], axis=0)
    egl_state = jnp.concatenate([jnp.broadcast_to(jnp.exp(t), (DH, 1)) for t in last], axis=0)
    grow = _mask_dot(jnp.ones((GR, GR), F32), jnp.where(mk["eye"], gcol, 0.0))
    dec = jnp.where(mk["causal"], jnp.exp(jnp.where(mk["causal"], gcol - grow, 0.0)), 0.0)
    eg = jnp.exp(gcol)
    ek = jnp.exp(gl - gcol)
    kb = k * beta
    vb = v * beta
    kbe = kb * eg
    A = jnp.where(mk["strict"], _bdot(kb, k, NT) * dec, 0.0)
    P = jnp.where(mk["causal"], _bdot(q, k, NT) * dec, 0.0)
    return dict(q=q, k=k, v=v, beta=beta, dec=dec, eg=eg, ek=ek, egl=jnp.exp(gl), egl_state=egl_state, kb=kb, vb=vb,
                kbe=kbe, A=A, P=P, qd=q * eg, kd=k * ek, heads=heads)


def _gc_cols(bg):
    ri = lax.broadcasted_iota(jnp.int32, (DNC, DNC), 0)
    ci = lax.broadcasted_iota(jnp.int32, (DNC, DNC), 1)
    return _mask_dot(jnp.where(ri >= ci, 1.0, 0.0), bg)


def _dn_core_fwd(qkv, bg, comm=None, name="dn_core_fwd"):
    c_start, c_finish, sp, gather = _comm_fns(comm)
    T = qkv.shape[0]
    n_chunks = T // DNC
    nb = _tile(n_chunks, DN_BLOCK, 1)
    tb = nb * DNC

    ng = len(gather)
    n_steps = n_chunks // nb

    def body(*refs):
        q_ref, k_ref, v_ref, bg_ref = refs[:4]
        o_ref, s_ref, tm_ref = refs[4 + ng:7 + ng]
        S_scr = refs[7 + 2 * ng]
        comm_refs = (refs[4:4 + ng], refs[7 + ng:7 + 2 * ng]) + tuple(refs[8 + 2 * ng:])

        @pl.when(pl.program_id(0) == 0)
        def _():
            S_scr[...] = jnp.zeros_like(S_scr)
            if ng:
                c_start(*comm_refs)

        mk = _dn_masks()

        def chunk(n):
            rows = pl.ds(n * DNC, DNC)
            bgc = bg_ref[rows, :]
            gc_cols = _gc_cols(bgc)
            for g in range(NG):
                c = _dn_group(q_ref, k_ref, v_ref, rows, bgc, gc_cols, g, mk)
                Tm = _tri_inv(c["A"])
                tm_ref[n, g] = Tm
                S = S_scr[g]
                s_ref[n, g] = S
                u = _bdot(Tm, c["vb"])
                w = _bdot(Tm, c["kbe"])
                vn = u - _bdot(_wide(w, mk), S)
                o = _bdot(_wide(c["qd"], mk), S) + _bdot(c["P"], vn)
                for j, h in enumerate(c["heads"]):
                    o_ref[rows, h * DH:(h + 1) * DH] = o[j * DNC:(j + 1) * DNC]
                S_scr[g] = S * c["egl_state"] + _bdot(_wide(c["kd"], mk), vn, TN)

        for n in range(nb):
            chunk(n)

        if ng:
            @pl.when(pl.program_id(0) == n_steps - 1)
            def _():
                c_finish(*comm_refs)

    blk = lambda j: pl.BlockSpec((tb, WD), lambda i: (i, j))
    outs = pl.pallas_call(
        body, name=name + ("_" + comm[0] if ng else ""), grid=(n_steps,),
        in_specs=[blk(0), blk(1), blk(2), pl.BlockSpec((tb, 128), lambda i: (i, 0))] + sp["specs"],
        out_specs=[blk(0), pl.BlockSpec((nb, NG, GK, DH), lambda i: (i, 0, 0, 0)),
                   pl.BlockSpec((nb, NG, GR, GR), lambda i: (i, 0, 0, 0))] + sp["specs"],
        out_shape=[jax.ShapeDtypeStruct((T, WD), F32), jax.ShapeDtypeStruct((n_chunks, NG, GK, DH), F32),
                   jax.ShapeDtypeStruct((n_chunks, NG, GR, GR), F32)] + sp["out_shape"],
        scratch_shapes=[pltpu.VMEM((NG, GK, DH), F32)] + (sp["sems"] if ng else []),
        compiler_params=_cp("arbitrary"),
    )(qkv, qkv, qkv, bg, *gather)
    return outs[0], outs[1], outs[2], list(outs[3:])


def _dn_core_bwd(qkv, bg, s_all, tm_all, do, comm=None, name="dn_core_bwd"):
    c_start, c_finish, sp, exchange = _comm_fns(comm)
    T = qkv.shape[0]
    n_chunks = T // DNC
    nb = _tile(n_chunks, DN_BLOCK, 1)
    tb = nb * DNC
    n_blocks = n_chunks // nb

    nx = len(exchange)

    def body(*refs):
        q_ref, k_ref, v_ref, bg_ref, s_ref, tm_ref, do_ref = refs[:7]
        dq_ref, dk_ref, dv_ref, dbg_ref = refs[7 + nx:11 + nx]
        dS_scr = refs[11 + 2 * nx]
        comm_refs = (refs[7:7 + nx], refs[11 + nx:11 + 2 * nx]) + tuple(refs[12 + 2 * nx:])

        @pl.when(pl.program_id(0) == 0)
        def _():
            dS_scr[...] = jnp.zeros_like(dS_scr)
            if nx:
                c_start(*comm_refs)

        lane = lax.broadcasted_iota(jnp.int32, (DNC, 128), 1)
        row = lax.broadcasted_iota(jnp.int32, (GR, 1), 0)

        mk = _dn_masks()

        def chunk(n):
            rows = pl.ds(n * DNC, DNC)
            ones = jnp.ones((GR, GR), F32)
            blk_f = jnp.where(mk["blk"], 1.0, 0.0)
            wide_f = jnp.where(mk["wide"], 1.0, 0.0)
            per_row = lambda m, a: _mask_dot(m, jnp.broadcast_to(a, (a.shape[0], DH)))[:, :1]
            bgc = bg_ref[rows, :]
            gc_cols = _gc_cols(bgc)
            dbg = jnp.zeros((DNC, 128), F32)
            for g in range(NG):
                c = _dn_group(q_ref, k_ref, v_ref, rows, bgc, gc_cols, g, mk)
                q, k, v, beta = c["q"], c["k"], c["v"], c["beta"]
                dec, eg, ek, egl = c["dec"], c["eg"], c["ek"], c["egl"]
                kb, vb, kbe, A, P, qd, kd = c["kb"], c["vb"], c["kbe"], c["A"], c["P"], c["qd"], c["kd"]
                S = s_ref[n, g]
                Tm = tm_ref[n, g]
                u = _bdot(Tm, vb)
                w = _bdot(Tm, kbe)
                w_wide = _wide(w, mk)
                vn = u - _bdot(w_wide, S)
                d_o = _stack_heads(do_ref, rows, g)
                dS1 = dS_scr[g]
                d_qd = _fold(_bdot(d_o, S, NT), mk)
                dP = jnp.where(mk["causal"], _bdot(d_o, vn, NT), 0.0)
                d_vn = _bdot(P, d_o, TN) + _bdot(_wide(kd, mk), dS1)
                d_kd = _fold(_bdot(vn, dS1, NT), mk)
                d_egl = per_row(wide_f, _lsum(dS1 * S))
                dS_scr[g] = dS1 * c["egl_state"] + _bdot(_wide(qd, mk), d_o, TN) - _bdot(w_wide, d_vn, TN)
                d_w = -_fold(_bdot(d_vn, S, NT), mk)
                d_vb = _bdot(Tm, d_vn, TN)
                d_kbe = _bdot(Tm, d_w, TN)
                dA = jnp.where(mk["strict"], -(_bdot(d_vb, u, NT) + _bdot(d_kbe, w, NT)), 0.0)
                dMA = dA * dec
                dMP = dP * dec
                d_kb = _bdot(dMA, k) + d_kbe * eg
                d_k = _bdot(dMA, kb, TN) + _bdot(dMP, q, TN) + d_kd * ek + d_kb * beta
                d_qs = (_bdot(dMP, k) + d_qd * eg) * (DH ** -0.5)
                d_v = d_vb * beta
                E = dA * A + dP * P
                col_sums = _mask_dot(ones, E, TN)[:, :1]
                t_kd = _lsum(d_kd * kd)
                d_gl = per_row(blk_f, t_kd) + d_egl * egl
                d_gc = (_lsum(E) - col_sums + _lsum(d_qd * qd) + _lsum(d_kbe * kbe) - t_kd
                        + jnp.where(row % DNC == DNC - 1, d_gl, 0.0))
                d_g = per_row(jnp.where(mk["upper"], 1.0, 0.0), d_gc)
                d_beta = _lsum(d_kb * k) + _lsum(d_vb * v)
                for j, h in enumerate(c["heads"]):
                    rs = slice(j * DNC, (j + 1) * DNC)
                    dq_ref[rows, h * DH:(h + 1) * DH] = d_qs[rs]
                    dk_ref[rows, h * DH:(h + 1) * DH] = d_k[rs]
                    dv_ref[rows, h * DH:(h + 1) * DH] = d_v[rs]
                    dbg = dbg + jnp.where(lane == h, d_beta[rs], 0.0) + jnp.where(lane == h + H, d_g[rs], 0.0)
            dbg_ref[rows, :] = dbg

        for n in reversed(range(nb)):
            chunk(n)

        if nx:
            @pl.when(pl.program_id(0) == n_blocks - 1)
            def _():
                c_finish(*comm_refs)

    blk = lambda j: pl.BlockSpec((tb, WD), lambda i: (n_blocks - 1 - i, j))
    small = pl.BlockSpec((tb, 128), lambda i: (n_blocks - 1 - i, 0))
    outs = pl.pallas_call(
        body, name=name + ("_" + comm[0] if nx else ""), grid=(n_blocks,),
        in_specs=[blk(0), blk(1), blk(2), small,
                  pl.BlockSpec((nb, NG, GK, DH), lambda i: (n_blocks - 1 - i, 0, 0, 0)),
                  pl.BlockSpec((nb, NG, GR, GR), lambda i: (n_blocks - 1 - i, 0, 0, 0)), blk(0)] + sp["specs"],
        out_specs=[blk(0), blk(0), blk(0), small] + sp["specs"],
        out_shape=[jax.ShapeDtypeStruct((T, WD), F32)] * 3 + [jax.ShapeDtypeStruct((T, 128), F32)] + sp["out_shape"],
        scratch_shapes=[pltpu.VMEM((NG, GK, DH), F32)] + (sp["sems"] if nx else []),
        compiler_params=_cp("arbitrary"),
    )(qkv, qkv, qkv, bg, s_all, tm_all, do, *exchange)
    return outs[0], outs[1], outs[2], outs[3], list(outs[4:])


def _dn_post_fwd(o, proj, gon, name="dn_post_fwd"):
    T = o.shape[0]
    tt = _tile(T, 256, 16)

    def body(o_ref, z_ref, g_ref, y_ref):
        for hh in range(H):
            sl = slice(hh * DH, (hh + 1) * DH)
            ov = o_ref[:, sl]
            zv = z_ref[:, sl].astype(F32)
            r = lax.rsqrt(jnp.mean(ov * ov, axis=-1, keepdims=True) + EPS)
            y_ref[:, sl] = (ov * r * g_ref[...] * (zv * _sig(zv))).astype(y_ref.dtype)

    return pl.pallas_call(
        body, name=name, grid=(T // tt,),
        in_specs=[pl.BlockSpec((tt, WD), lambda i: (i, 0)), pl.BlockSpec((tt, WD), lambda i: (i, OZ // WD)),
                  pl.BlockSpec((1, DH), lambda i: (0, 0))],
        out_specs=pl.BlockSpec((tt, WD), lambda i: (i, 0)),
        out_shape=jax.ShapeDtypeStruct((T, WD), BF16), compiler_params=_cp("parallel"),
    )(o, proj, gon)


def _dn_post_bwd(o, proj, gon, dy, name="dn_post_bwd"):
    T = o.shape[0]
    tt = _tile(T, 256, 16)

    def body(o_ref, z_ref, g_ref, dy_ref, do_ref, dz_ref, dg_ref):
        @pl.when(pl.program_id(0) == 0)
        def _():
            dg_ref[...] = jnp.zeros_like(dg_ref)

        acc = jnp.zeros((1, DH), F32)
        for hh in range(H):
            sl = slice(hh * DH, (hh + 1) * DH)
            ov = o_ref[:, sl]
            zv = z_ref[:, sl].astype(F32)
            dyv = dy_ref[:, sl]
            r = lax.rsqrt(jnp.mean(ov * ov, axis=-1, keepdims=True) + EPS)
            oh = ov * r
            nrm = oh * g_ref[...]
            gate, gate_grad = _silu_and_grad(zv)
            dn = dyv * gate
            dz_ref[:, sl] = (dyv * nrm * gate_grad).astype(dz_ref.dtype)
            doh = dn * g_ref[...]
            do_ref[:, sl] = r * (doh - oh * jnp.mean(doh * oh, axis=-1, keepdims=True))
            acc = acc + jnp.sum(dn * oh, axis=0, keepdims=True)
        dg_ref[...] += acc

    row = pl.BlockSpec((tt, WD), lambda i: (i, 0))
    vec = pl.BlockSpec((1, DH), lambda i: (0, 0))
    return pl.pallas_call(
        body, name=name, grid=(T // tt,),
        in_specs=[row, pl.BlockSpec((tt, WD), lambda i: (i, OZ // WD)), vec, row],
        out_specs=[row, row, vec],
        out_shape=[jax.ShapeDtypeStruct((T, WD), F32), jax.ShapeDtypeStruct((T, WD), BF16),
                   jax.ShapeDtypeStruct((1, DH), F32)],
        compiler_params=_cp("arbitrary"),
    )(o, proj, gon, dy)


def _sg_common(u_ref, v_ref, lng_ref, lnb_ref, with_grad=True):
    ur = u_ref[...].astype(F32)
    vr = v_ref[...].astype(F32)
    vgel, vgel_grad = _gelu_and_grad(vr) if with_grad else (_gelu(vr), None)
    mu = jnp.mean(vgel, axis=-1, keepdims=True)
    xc = vgel - mu
    rs = lax.rsqrt(jnp.mean(xc * xc, axis=-1, keepdims=True) + EPS)
    xh = xc * rs
    vg = xh * lng_ref[...] + lnb_ref[...]
    return ur, vgel_grad, rs, xh, vg


def _sg_fwd(proj, lng, lnb, sgw, sgbt, name="sg_fwd"):
    T = proj.shape[0]

    def body(u_ref, v_ref, lng_ref, lnb_ref, w_ref, bt_ref, y_ref):
        ur, _, _, _, vg = _sg_common(u_ref, v_ref, lng_ref, lnb_ref, with_grad=False)
        ri = lax.broadcasted_iota(jnp.int32, (SGC, SGC), 0)
        ci = lax.broadcasted_iota(jnp.int32, (SGC, SGC), 1)
        ug = _gelu(ur)
        for g in range(H):
            sl = slice(g * DH, (g + 1) * DH)
            ws = jnp.where(ri >= ci, w_ref[g], 0.0)
            mixed = _bdot(ws, vg[:, sl]) + bt_ref[:, g:g + 1]
            y_ref[:, sl] = (ug[:, sl] * mixed).astype(y_ref.dtype)

    vec = pl.BlockSpec((1, WD), lambda i: (0, 0))
    return pl.pallas_call(
        body, name=name, grid=(T // SGC,),
        in_specs=[pl.BlockSpec((SGC, WD), lambda i: (i, OU // WD)), pl.BlockSpec((SGC, WD), lambda i: (i, OV // WD)),
                  vec, vec, pl.BlockSpec((H, SGC, SGC), lambda i: (0, 0, 0)),
                  pl.BlockSpec((SGC, H), lambda i: (0, 0))],
        out_specs=pl.BlockSpec((SGC, WD), lambda i: (i, 0)),
        out_shape=jax.ShapeDtypeStruct((T, WD), BF16), compiler_params=_cp("parallel"),
    )(proj, proj, lng, lnb, sgw, sgbt)


def _sg_bwd(proj, lng, lnb, sgw, sgbt, dy, name="sg_bwd"):
    T = proj.shape[0]

    def body(u_ref, v_ref, lng_ref, lnb_ref, w_ref, bt_ref, dy_ref,
             du_ref, dv_ref, dw_ref, dbt_ref, dlng_ref, dlnb_ref):
        @pl.when(pl.program_id(0) == 0)
        def _():
            dw_ref[...] = jnp.zeros_like(dw_ref)
            dbt_ref[...] = jnp.zeros_like(dbt_ref)
            dlng_ref[...] = jnp.zeros_like(dlng_ref)
            dlnb_ref[...] = jnp.zeros_like(dlnb_ref)

        ur, vgel_grad, rs, xh, vg = _sg_common(u_ref, v_ref, lng_ref, lnb_ref)
        ri = lax.broadcasted_iota(jnp.int32, (SGC, SGC), 0)
        ci = lax.broadcasted_iota(jnp.int32, (SGC, SGC), 1)
        ug, ug_grad = _gelu_and_grad(ur)
        dyv = dy_ref[...]
        dbt = jnp.zeros((SGC, 128), F32)
        dvg_parts = []
        for g in range(H):
            sl = slice(g * DH, (g + 1) * DH)
            ws = jnp.where(ri >= ci, w_ref[g], 0.0)
            mixed = _bdot(ws, vg[:, sl]) + bt_ref[:, g:g + 1]
            dyg = dyv[:, sl]
            du_ref[:, sl] = (dyg * mixed * ug_grad[:, sl]).astype(du_ref.dtype)
            dmix = dyg * ug[:, sl]
            dw_ref[g] += jnp.where(ri >= ci, _bdot(dmix, vg[:, sl], NT), 0.0)
            dbt = dbt + jnp.where(ci == g, _lsum(dmix), 0.0)
            dvg_parts.append(_bdot(ws, dmix, TN))
        dbt_ref[...] += dbt
        dvg = jnp.concatenate(dvg_parts, axis=1)
        dlng_ref[...] += jnp.sum(dvg * xh, axis=0, keepdims=True)
        dlnb_ref[...] += jnp.sum(dvg, axis=0, keepdims=True)
        dxh = dvg * lng_ref[...]
        dvgel = rs * (dxh - jnp.mean(dxh, axis=-1, keepdims=True) - xh * jnp.mean(dxh * xh, axis=-1, keepdims=True))
        dv_ref[...] = (dvgel * vgel_grad).astype(dv_ref.dtype)

    vec = pl.BlockSpec((1, WD), lambda i: (0, 0))
    row = pl.BlockSpec((SGC, WD), lambda i: (i, 0))
    return pl.pallas_call(
        body, name=name, grid=(T // SGC,),
        in_specs=[pl.BlockSpec((SGC, WD), lambda i: (i, OU // WD)), pl.BlockSpec((SGC, WD), lambda i: (i, OV // WD)),
                  vec, vec, pl.BlockSpec((H, SGC, SGC), lambda i: (0, 0, 0)),
                  pl.BlockSpec((SGC, H), lambda i: (0, 0)), row],
        out_specs=[row, row, pl.BlockSpec((H, SGC, SGC), lambda i: (0, 0, 0)),
                   pl.BlockSpec((SGC, 128), lambda i: (0, 0)), vec, vec],
        out_shape=[jax.ShapeDtypeStruct((T, WD), BF16), jax.ShapeDtypeStruct((T, WD), BF16),
                   jax.ShapeDtypeStruct((H, SGC, SGC), F32), jax.ShapeDtypeStruct((SGC, 128), F32),
                   jax.ShapeDtypeStruct((1, WD), F32), jax.ShapeDtypeStruct((1, WD), F32)],
        compiler_params=_cp("arbitrary"),
    )(proj, proj, lng, lnb, sgw, sgbt, dy)


def _merge_fwd(proj, yap, ybp, D, name="merge_fwd"):
    T = proj.shape[0]
    tt = _tile(T, 256, 16)

    def body(ga_ref, gb_ref, a_ref, b_ref, o_ref):
        ga, gb, a, b = [r[...].astype(F32) for r in (ga_ref, gb_ref, a_ref, b_ref)]
        o_ref[...] = (_sig(ga) * a + _sig(gb) * b).astype(o_ref.dtype)

    row = pl.BlockSpec((tt, D), lambda i: (i, 0))
    return pl.pallas_call(
        body, name=name, grid=(T // tt,),
        in_specs=[pl.BlockSpec((tt, D), lambda i: (i, OGA // D)), pl.BlockSpec((tt, D), lambda i: (i, OGA // D + 1)),
                  row, row],
        out_specs=row, out_shape=jax.ShapeDtypeStruct((T, D), BF16), compiler_params=_cp("parallel"),
    )(proj, proj, yap, ybp)


def _merge_bwd(proj, yap, ybp, dm, D, name="merge_bwd"):
    T = proj.shape[0]
    tt = _tile(T, 256, 16)

    def body(ga_ref, gb_ref, a_ref, b_ref, dm_ref, da_ref, db_ref, dga_ref, dgb_ref):
        d, ga, gb, a, b = [r[...].astype(F32) for r in (dm_ref, ga_ref, gb_ref, a_ref, b_ref)]
        sa = _sig(ga)
        sb = _sig(gb)
        da_ref[...] = (d * sa).astype(da_ref.dtype)
        db_ref[...] = (d * sb).astype(db_ref.dtype)
        dga_ref[...] = (d * a * sa * (1.0 - sa)).astype(dga_ref.dtype)
        dgb_ref[...] = (d * b * sb * (1.0 - sb)).astype(dgb_ref.dtype)

    row = pl.BlockSpec((tt, D), lambda i: (i, 0))
    return pl.pallas_call(
        body, name=name, grid=(T // tt,),
        in_specs=[pl.BlockSpec((tt, D), lambda i: (i, OGA // D)), pl.BlockSpec((tt, D), lambda i: (i, OGA // D + 1)),
                  row, row, row],
        out_specs=[row] * 4, out_shape=[jax.ShapeDtypeStruct((T, D), BF16)] * 4,
        compiler_params=_cp("parallel"),
    )(proj, proj, yap, ybp, dm)


def _ffn_act_fwd(gp, up, cw, cb, name="ffn_act_fwd"):
    T, F = gp.shape

    def body(g_ref, u_ref, w_ref, b_ref, o_ref):
        gv = g_ref[...].astype(F32)
        w = w_ref[...]
        c = gv * w[FF_K - 1:FF_K, :] + b_ref[...]
        for k in range(1, FF_K):
            c = c + _shift_down(gv, k) * w[FF_K - 1 - k:FF_K - k, :]
        o_ref[...] = (c * _sig(c) * u_ref[...].astype(F32)).astype(o_ref.dtype)

    col = pl.BlockSpec((T, 128), lambda j: (0, j))
    return pl.pallas_call(
        body, name=name, grid=(F // 128,),
        in_specs=[col, col, pl.BlockSpec((FF_K, 128), lambda j: (0, j)), pl.BlockSpec((1, 128), lambda j: (0, j))],
        out_specs=col, out_shape=jax.ShapeDtypeStruct((T, F), BF16), compiler_params=_cp("parallel"),
    )(gp, up, cw, cb)


def _ffn_act_bwd(gp, up, cw, cb, dact, name="ffn_act_bwd"):
    T, F = gp.shape

    def body(g_ref, u_ref, w_ref, b_ref, d_ref, dg_ref, du_ref, dw_ref, db_ref):
        gv = g_ref[...].astype(F32)
        w = w_ref[...]
        shifted = [_shift_down(gv, k) for k in range(FF_K)]
        c = shifted[0] * w[FF_K - 1:FF_K, :] + b_ref[...]
        for k in range(1, FF_K):
            c = c + shifted[k] * w[FF_K - 1 - k:FF_K - k, :]
        d = d_ref[...].astype(F32)
        act, act_grad = _silu_and_grad(c)
        du_ref[...] = (d * act).astype(du_ref.dtype)
        dc = d * u_ref[...].astype(F32) * act_grad
        dg = dc * w[FF_K - 1:FF_K, :]
        for k in range(1, FF_K):
            dg = dg + _shift_up(dc, k) * w[FF_K - 1 - k:FF_K - k, :]
        dg_ref[...] = dg.astype(dg_ref.dtype)
        rows = [jnp.sum(dc * shifted[FF_K - 1 - t], axis=0, keepdims=True) for t in range(FF_K)]
        dw_ref[...] = jnp.concatenate(rows, axis=0)
        db_ref[...] = jnp.sum(dc, axis=0, keepdims=True)

    col = pl.BlockSpec((T, 128), lambda j: (0, j))
    wspec = pl.BlockSpec((FF_K, 128), lambda j: (0, j))
    bspec = pl.BlockSpec((1, 128), lambda j: (0, j))
    return pl.pallas_call(
        body, name=name, grid=(F // 128,),
        in_specs=[col, col, wspec, bspec, col], out_specs=[col, col, wspec, bspec],
        out_shape=[jax.ShapeDtypeStruct((T, F), BF16), jax.ShapeDtypeStruct((T, F), BF16),
                   jax.ShapeDtypeStruct((FF_K, F), F32), jax.ShapeDtypeStruct((1, F), F32)],
        compiler_params=_cp("parallel"),
    )(gp, up, cw, cb, dact)


class _Carrier:
    def __init__(self, plan=None, deliver=None):
        self.plan, self.deliver = plan or (lambda kernel: None), deliver

    def run(self, kernel, fn, **kw):
        comm = self.plan(kernel)
        out = fn(comm=comm, **kw)
        if comm:
            self.deliver(kernel, out[-1])
            out = out[:-1]
            return out[0] if len(out) == 1 else out
        return out


def _layer_fwd(x, w, carrier=None):
    cr = carrier or _Carrier()
    D = x.shape[1]
    oba = OGA + 2 * D
    h = _rms_fwd(x, w["norm1_g"], "rms1_fwd")
    proj = cr.run("proj", functools.partial(_mm, h, w["w_in_t"], "nt", BF16, name="mm_proj"))
    bg = _ba_fwd(proj, w["alog_row"], w["dtb_row"], oba)
    qkv = _dn_prep_fwd(proj, w["dn_conv_w"])
    r = cr.run("dn_core", functools.partial(_dn_core_fwd, qkv, bg))
    o, s_all, tm_all = r[0], r[1], r[2]
    ya = _dn_post_fwd(o, proj, w["dn_onorm_g"])
    yb = _sg_fwd(proj, w["sg_ln_g"], w["sg_ln_b"], w["sg_w"], w["sg_bt"])
    yap = _mm(ya, w["w_branch_a"], "nn", BF16, name="mm_branch")
    ybp = _mm(yb, w["w_branch_b"], "nn", BF16, name="mm_branch")
    merged = _merge_fwd(proj, yap, ybp, D)
    x1 = _mm(merged, w["w_out"], "nn", F32, add=x, name="mm_out")
    h2 = _rms_fwd(x1, w["norm2_g"], "rms2_fwd")
    gp = cr.run("ffn_gate", functools.partial(_mm, h2, w["ffn_w_gate"], "nn", BF16, name="mm_ffn_in"))
    up = cr.run("ffn_up", functools.partial(_mm, h2, w["ffn_w_up"], "nn", BF16, name="mm_ffn_in"))
    act = _ffn_act_fwd(gp, up, w["ffn_conv_w"], w["ffn_conv_b"])
    x2 = cr.run("ffn_down", functools.partial(_mm, act, w["ffn_w_down"], "nn", F32, add=x1, name="mm_ffn_down"))
    saved = dict(x=x, h=h, proj=proj, bg=bg, qkv=qkv, o=o, s_all=s_all, tm_all=tm_all, ya=ya, yb=yb, yap=yap,
                 ybp=ybp, merged=merged, x1=x1, h2=h2, gp=gp, up=up, act=act)
    return x2, saved


def _layer_bwd(dx2, w, s, carrier=None, ffn_grads_ready=None, rest_grads_ready=None):
    cr = carrier or _Carrier()
    D = dx2.shape[1]
    oba = OGA + 2 * D
    g = {}
    dx2b = dx2.astype(BF16)
    dact = cr.run("d_act", functools.partial(_mm, dx2b, w["ffn_w_down"], "nt", BF16, name="mm_d_act"))
    g["ffn_w_down"] = _mm(s["act"], dx2b, "tn", BF16, name="mm_dw_down")
    dgp, dup, g["ffn_conv_w"], g["ffn_conv_b"] = _ffn_act_bwd(s["gp"], s["up"], w["ffn_conv_w"], w["ffn_conv_b"], dact)
    dh2 = _mm(dgp, w["ffn_w_gate"], "nt", F32, name="mm_dh2")
    dh2 = _mm(dup, w["ffn_w_up"], "nt", F32, add=dh2, name="mm_dh2_acc")
    g["ffn_w_gate"] = _mm(s["h2"], dgp, "tn", BF16, name="mm_dw_ffn_in")
    g["ffn_w_up"] = _mm(s["h2"], dup, "tn", BF16, name="mm_dw_ffn_in")
    if ffn_grads_ready:
        ffn_grads_ready(g)
    dx1, g["norm2_g"] = _rms_bwd(s["x1"], w["norm2_g"], dh2, dx2, "rms2_bwd")
    dx1b = dx1.astype(BF16)
    dm = _mm(dx1b, w["w_out"], "nt", BF16, name="mm_d_merged")
    g["w_out"] = _mm(s["merged"], dx1b, "tn", BF16, name="mm_dw_out")
    dyap, dybp, dga, dgb = _merge_bwd(s["proj"], s["yap"], s["ybp"], dm, D)
    dya = _mm(dyap, w["w_branch_a"], "nt", F32, name="mm_d_branch")
    dyb = _mm(dybp, w["w_branch_b"], "nt", F32, name="mm_d_branch")
    g["w_branch_a"] = _mm(s["ya"], dyap, "tn", BF16, name="mm_dw_branch")
    g["w_branch_b"] = _mm(s["yb"], dybp, "tn", BF16, name="mm_dw_branch")
    du, dv, g["sg_w"], dbt, g["sg_ln_g"], g["sg_ln_b"] = _sg_bwd(
        s["proj"], w["sg_ln_g"], w["sg_ln_b"], w["sg_w"], w["sg_bt"], dyb)
    g["sg_b"] = jnp.transpose(dbt[:, :H])
    do, dz, g["dn_onorm_g"] = _dn_post_bwd(s["o"], s["proj"], w["dn_onorm_g"], dya)
    r = cr.run("dn_core", functools.partial(_dn_core_bwd, s["qkv"], s["bg"], s["s_all"], s["tm_all"], do))
    dq, dk, dvv, dbg = r[0], r[1], r[2], r[3]
    dqkv, g["dn_conv_w"] = _dn_prep_bwd(s["proj"], w["dn_conv_w"], dq, dk, dvv)
    dba, dal, ddt = _ba_bwd(s["proj"], w["alog_row"], w["dtb_row"], dbg, oba)
    g["dn_a_log"] = dal[0, H:2 * H]
    g["dn_dt_bias"] = ddt[0, H:2 * H]
    dproj = jnp.concatenate([dqkv, dz, du, dv, dga, dgb, dba], axis=1)
    g["w_in_t"] = cr.run("dw_in", functools.partial(_mm, dproj, s["h"], "tn", BF16, name="mm_dw_in"))
    if rest_grads_ready:
        rest_grads_ready(g)
    dh = cr.run("dh", functools.partial(_mm, dproj, w["w_in_t"], "nn", F32, name="mm_dh"))
    dx, g["norm1_g"] = _rms_bwd(s["x"], w["norm1_g"], dh, dx1, "rms1_bwd")
    return dx, g


def _local_step(x, tgt, layers, final_g):
    saved = []
    for w in layers:
        x, s = _layer_fwd(x, w)
        saved.append(s)
    dx, dgf, loss = _loss_head(x, final_g, tgt)
    grads = [None] * len(layers)
    for l in reversed(range(len(layers))):
        dx, grads[l] = _layer_bwd(dx, layers[l], saved[l])
    return loss[0, 0], dx, grads, dgf


def _w_in_pad(wt):
    c1 = 4 * WD
    return jnp.concatenate([wt[:c1], wt[c1 + 2 * H:], wt[c1:c1 + 2 * H],
                            jnp.zeros((128 - 2 * H, wt.shape[1]), wt.dtype)], axis=0)


def _w_in_unpad(gt):
    c1 = 4 * WD
    n = gt.shape[0] - 128
    return jnp.concatenate([gt[:c1], gt[n:n + 2 * H], gt[c1:n]], axis=0)


def _row128(v, off):
    return jnp.pad(v, (off, 128 - off - v.shape[0]))[None]


def _prep_small(p):
    return dict(
        norm1_g=p["norm1_g"][None], alog_row=_row128(p["dn_a_log"], H), dtb_row=_row128(p["dn_dt_bias"], H),
        dn_conv_w=p["dn_conv_w"], dn_onorm_g=p["dn_onorm_g"][None],
        sg_ln_g=p["sg_ln_g"][None], sg_ln_b=p["sg_ln_b"][None], sg_w=p["sg_w"], sg_bt=jnp.transpose(p["sg_b"]),
        norm2_g=p["norm2_g"][None], ffn_conv_w=p["ffn_conv_w"], ffn_conv_b=p["ffn_conv_b"][None])


def _prep_layer(p):
    return dict(_prep_small(p), w_in_t=_w_in_pad(p["w_in_t"]),
                **{n: p[n] for n in ("w_branch_a", "w_branch_b", "w_out", "ffn_w_gate", "ffn_w_up", "ffn_w_down")})


HBM_SPEC = pl.BlockSpec(memory_space=pltpu.HBM)


def _coords():
    return lax.axis_index("x"), lax.axis_index("y"), lax.axis_index("c")


def _other_chips(x, y):
    return [(1 - x, y), (x, 1 - y), (1 - x, 1 - y)]


def _remote(src, dst, send_sems, recv_sems, k, dev):
    return pltpu.make_async_remote_copy(src_ref=src, dst_ref=dst, send_sem=send_sems.at[k], recv_sem=recv_sems.at[k],
                                        device_id=dev, device_id_type=MESH)


def _ag_copies(w_refs, o_refs, send_sems, recv_sems):
    x, y, c = _coords()
    me = 2 * x + y
    chips = _other_chips(x, y)

    def ici(k, j, owner):
        chip = chips[j]
        return _remote(w_refs[k].at[c], o_refs[k].at[owner, c], send_sems, recv_sems, 6 * k + j, (chip[0], chip[1], c))

    def d2d(k, j, part):
        owner = 2 * chips[j][0] + chips[j][1]
        return _remote(o_refs[k].at[owner, part], o_refs[k].at[owner, part], send_sems, recv_sems, 6 * k + 3 + j,
                       (x, y, 1 - c))

    n = len(w_refs)
    return me, c, chips, ici, d2d, [(k, j) for k in range(n) for j in range(3)]


def _ag_start(w_refs, o_refs, send_sems, recv_sems):
    me, _, _, ici, _, pairs = _ag_copies(w_refs, o_refs, send_sems, recv_sems)
    for k, j in pairs:
        ici(k, j, me).start()


def _ag_finish(w_refs, o_refs, send_sems, recv_sems):
    me, c, chips, ici, d2d, pairs = _ag_copies(w_refs, o_refs, send_sems, recv_sems)
    for k, j in pairs:
        ici(k, j, 2 * chips[j][0] + chips[j][1]).wait_recv()
        d2d(k, j, c).start()
    for k, j in pairs:
        d2d(k, j, 1 - c).wait_recv()
    for k, j in pairs:
        ici(k, j, me).wait_send()
        d2d(k, j, c).wait_send()


def _ag_specs(ws):
    n = len(ws)
    return dict(out_shape=[jax.ShapeDtypeStruct((N_CHIPS,) + w.shape, w.dtype) for w in ws],
                specs=[HBM_SPEC] * n, sems=[pltpu.SemaphoreType.DMA((6 * n,)), pltpu.SemaphoreType.DMA((6 * n,))])


def _ag_layers(ws):
    n = len(ws)

    def body(*refs):
        _ag_start(refs[:n], refs[n:2 * n], *refs[2 * n:])
        _ag_finish(refs[:n], refs[n:2 * n], *refs[2 * n:])

    sp = _ag_specs(ws)
    return pl.pallas_call(
        body, name="ag_weights", out_shape=sp["out_shape"], in_specs=sp["specs"], out_specs=sp["specs"],
        scratch_shapes=sp["sems"],
    )(*ws)


def _rs_pair_exchange(Gs):
    n = len(Gs)

    def body(*refs):
        g_refs, b_refs = refs[:n], refs[n:2 * n]
        send_sems, recv_sems = refs[2 * n:]
        x, y, c = _coords()
        cps = [_remote(g_refs[k].at[i, 1 - c], b_refs[k].at[i], send_sems, recv_sems, N_CHIPS * k + i, (x, y, 1 - c))
               for k in range(n) for i in range(N_CHIPS)]
        for cp in cps:
            cp.start()
        for cp in cps:
            cp.wait()

    return pl.pallas_call(
        body, name="rs_pair_exchange",
        out_shape=[jax.ShapeDtypeStruct((N_CHIPS,) + g.shape[2:], g.dtype) for g in Gs],
        in_specs=[HBM_SPEC] * n, out_specs=[HBM_SPEC] * n,
        scratch_shapes=[pltpu.SemaphoreType.DMA((N_CHIPS * n,)), pltpu.SemaphoreType.DMA((N_CHIPS * n,))],
    )(*Gs)


def _rs_add_pair(G, B, c, name):
    _, _, R, C = G.shape
    tr = _tile(R, 256, 16)

    def body(c_ref, g_ref, b_ref, o_ref):
        o_ref[0] = (g_ref[0, 0].astype(F32) + b_ref[0].astype(F32)).astype(o_ref.dtype)

    grid_spec = pltpu.PrefetchScalarGridSpec(
        num_scalar_prefetch=1, grid=(N_CHIPS, R // tr),
        in_specs=[pl.BlockSpec((1, 1, tr, C), lambda i, r, c_ref: (i, c_ref[0], r, 0)),
                  pl.BlockSpec((1, tr, C), lambda i, r, c_ref: (i, r, 0))],
        out_specs=pl.BlockSpec((1, tr, C), lambda i, r, c_ref: (i, r, 0)))
    return pl.pallas_call(
        body, name=name, grid_spec=grid_spec, out_shape=jax.ShapeDtypeStruct((N_CHIPS, R, C), G.dtype),
        compiler_params=_cp("parallel", "parallel"),
    )(jnp.reshape(c, (1,)).astype(jnp.int32), G, B)


def _rsx_copies(p_refs, b_refs, send_sems, recv_sems):
    x, y, c = _coords()
    me = 2 * x + y
    chips = _other_chips(x, y)

    def cp(k, j, src_slot, dst_slot):
        return _remote(p_refs[k].at[src_slot], b_refs[k].at[dst_slot], send_sems, recv_sems, 3 * k + j,
                       (chips[j][0], chips[j][1], c))

    return me, chips, cp, [(k, j) for k in range(len(p_refs)) for j in range(3)]


def _rsx_start(p_refs, b_refs, send_sems, recv_sems):
    me, chips, cp, pairs = _rsx_copies(p_refs, b_refs, send_sems, recv_sems)
    for k, j in pairs:
        cp(k, j, 2 * chips[j][0] + chips[j][1], me).start()


def _rsx_finish(p_refs, b_refs, send_sems, recv_sems):
    me, chips, cp, pairs = _rsx_copies(p_refs, b_refs, send_sems, recv_sems)
    for k, j in pairs:
        owner = 2 * chips[j][0] + chips[j][1]
        cp(k, j, owner, owner).wait_recv()
    for k, j in pairs:
        cp(k, j, 2 * chips[j][0] + chips[j][1], me).wait_send()


def _rsx_specs(Ps):
    n = len(Ps)
    return dict(out_shape=[jax.ShapeDtypeStruct(p.shape, p.dtype) for p in Ps], specs=[HBM_SPEC] * n,
                sems=[pltpu.SemaphoreType.DMA((3 * n,)), pltpu.SemaphoreType.DMA((3 * n,))])


def _rs_sum_chips(P, B, me, name):
    _, R, C = P.shape
    tr = _tile(R, 256, 16)

    def body(me_ref, p_ref, b1_ref, b2_ref, b3_ref, o_ref):
        o_ref[...] = ((p_ref[0].astype(F32) + b1_ref[0].astype(F32)) + b2_ref[0].astype(F32)) + b3_ref[0].astype(F32)

    slot = lambda d: pl.BlockSpec((1, tr, C), lambda r, me_ref: ((me_ref[0] + d) % N_CHIPS, r, 0))
    grid_spec = pltpu.PrefetchScalarGridSpec(
        num_scalar_prefetch=1, grid=(R // tr,), in_specs=[slot(0), slot(1), slot(2), slot(3)],
        out_specs=pl.BlockSpec((tr, C), lambda r, me_ref: (r, 0)))
    return pl.pallas_call(
        body, name=name, grid_spec=grid_spec, out_shape=jax.ShapeDtypeStruct((R, C), F32),
        compiler_params=_cp("parallel"),
    )(jnp.reshape(me, (1,)).astype(jnp.int32), P, B, B, B)


def _sum_slots(B, name):
    S, R, C = B.shape
    tr = _tile(R, 256, 16)

    def body(b_ref, o_ref):
        acc = b_ref[0].astype(F32)
        for i in range(1, S):
            acc = acc + b_ref[i].astype(F32)
        o_ref[...] = acc

    return pl.pallas_call(
        body, name=name, grid=(R // tr,), in_specs=[pl.BlockSpec((S, tr, C), lambda r: (0, r, 0))],
        out_specs=pl.BlockSpec((tr, C), lambda r: (r, 0)), out_shape=jax.ShapeDtypeStruct((R, C), F32),
        compiler_params=_cp("parallel"),
    )(B)


def _rs_pair_swap(Rs):
    n = len(Rs)

    def body(*refs):
        r_refs, o_refs = refs[:n], refs[n:2 * n]
        send_sems, recv_sems = refs[2 * n:]
        x, y, c = _coords()
        cps = [_remote(r_refs[k], o_refs[k], send_sems, recv_sems, k, (x, y, 1 - c)) for k in range(n)]
        for cp in cps:
            cp.start()
        for cp in cps:
            cp.wait()

    return pl.pallas_call(
        body, name="rs_pair_swap", out_shape=[jax.ShapeDtypeStruct(r.shape, r.dtype) for r in Rs],
        in_specs=[HBM_SPEC] * n, out_specs=[HBM_SPEC] * n,
        scratch_shapes=[pltpu.SemaphoreType.DMA((n,)), pltpu.SemaphoreType.DMA((n,))],
    )(*Rs)


def _ag8(v):
    R = v.shape[0]

    def body(v_ref, out_ref, send_sems, recv_sems, local_sem):
        x, y, c = _coords()
        me, sib = (x, y, c), (x, y, 1 - c)
        chips = _other_chips(x, y)

        def slot(p):
            return out_ref.at[4 * p[0] + 2 * p[1] + p[2]]

        def copy(k, block, to, src=None):
            return _remote(slot(block) if src is None else src, slot(block), send_sems, recv_sems, k, to)

        mine = pltpu.make_async_copy(v_ref, slot(me), local_sem)
        mine.start()
        first = [copy(0, me, sib, src=v_ref)]
        first += [copy(1 + j, me, (chip[0], chip[1], c), src=v_ref) for j, chip in enumerate(chips)]
        for cp in first:
            cp.start()
        passed = [copy(4 + j, (chip[0], chip[1], c), sib) for j, chip in enumerate(chips)]
        for j, chip in enumerate(chips):
            copy(1 + j, (chip[0], chip[1], c), me).wait_recv()
            passed[j].start()
        copy(0, sib, me).wait_recv()
        for j, chip in enumerate(chips):
            copy(4 + j, (chip[0], chip[1], 1 - c), me).wait_recv()
        for cp in first + passed:
            cp.wait_send()
        mine.wait()

    return pl.pallas_call(
        body, name="ag8_small", out_shape=jax.ShapeDtypeStruct((8, R, 128), v.dtype),
        in_specs=[pl.BlockSpec(memory_space=pltpu.VMEM)], out_specs=pl.BlockSpec(memory_space=pltpu.VMEM),
        scratch_shapes=[pltpu.SemaphoreType.DMA((7,)), pltpu.SemaphoreType.DMA((7,)), pltpu.SemaphoreType.DMA],
        compiler_params=pltpu.CompilerParams(vmem_limit_bytes=VMEM_LIMIT),
    )(v)


def _adamw(w, g, m, v, name):
    L, R, C = w.shape
    rows = [R] + [t for t in range(8, min(R, 1024) + 1, 8) if R % t == 0]
    cols = [C] + [t for t in range(128, C, 128) if C % t == 0]
    lead = [t for t in range(1, L + 1) if L % t == 0]
    fits = [(a * r * c, c, r, a) for a in lead for r in rows for c in cols if a * r * c * 4 <= 3 << 19]
    _, tc, tr, tl = max(fits) if fits else (0, min(cols), min(rows), 1)

    def body(w_ref, g_ref, m_ref, v_ref, d_ref, mo_ref, vo_ref):
        gv = g_ref[...]
        m2 = ADAM_B1 * m_ref[...] + (1.0 - ADAM_B1) * gv
        v2 = ADAM_B2 * v_ref[...] + (1.0 - ADAM_B2) * jnp.square(gv)
        m_hat = m2 / (1.0 - ADAM_B1 ** ADAM_STEP)
        v_hat = v2 / (1.0 - ADAM_B2 ** ADAM_STEP)
        d_ref[...] = -ADAM_LR * (m_hat / (jnp.sqrt(v_hat) + ADAM_EPS) + ADAM_WD * w_ref[...])
        mo_ref[...] = m2
        vo_ref[...] = v2

    blk = pl.BlockSpec((tl, tr, tc), lambda l, r, j: (l, r, j))
    return pl.pallas_call(
        body, name=name, grid=(L // tl, R // tr, C // tc), in_specs=[blk] * 4, out_specs=[blk] * 3,
        out_shape=[jax.ShapeDtypeStruct(w.shape, F32)] * 3,
        compiler_params=_cp("parallel", "parallel", "parallel"),
    )(w, g, m, v)


def _adamw_halves(w, g_mine, g_other, c, m, v, name):
    L, _, R, C = w.shape
    tr = _tile(R, 128, 8)

    def body(c_ref, w_ref, *rest):
        g_refs = rest[:2 * L]
        m_ref, v_ref, g_ref, d_ref, mo_ref, vo_ref = rest[2 * L:]
        l, h = pl.program_id(0), pl.program_id(1)
        gm, go = g_refs[0][...], g_refs[L][...]
        for i in range(1, L):
            gm = jnp.where(l == i, g_refs[i][...], gm)
            go = jnp.where(l == i, g_refs[L + i][...], go)
        gv = jnp.where(h == c_ref[0], gm, go)[None, None]
        g_ref[...] = gv
        m2 = ADAM_B1 * m_ref[...] + (1.0 - ADAM_B1) * gv
        v2 = ADAM_B2 * v_ref[...] + (1.0 - ADAM_B2) * jnp.square(gv)
        m_hat = m2 / (1.0 - ADAM_B1 ** ADAM_STEP)
        v_hat = v2 / (1.0 - ADAM_B2 ** ADAM_STEP)
        d_ref[...] = -ADAM_LR * (m_hat / (jnp.sqrt(v_hat) + ADAM_EPS) + ADAM_WD * w_ref[...])
        mo_ref[...] = m2
        vo_ref[...] = v2

    blk = pl.BlockSpec((1, 1, tr, C), lambda l, h, r, c_ref: (l, h, r, 0))

    def gblk(i, mine):
        def index(l, h, r, c_ref):
            use = jnp.logical_and(l == i, (h == c_ref[0]) == mine)
            return (jnp.where(use, r, 0), 0)
        return pl.BlockSpec((tr, C), index)

    grid_spec = pltpu.PrefetchScalarGridSpec(
        num_scalar_prefetch=1, grid=(L, 2, R // tr),
        in_specs=[blk] + [gblk(i, True) for i in range(L)] + [gblk(i, False) for i in range(L)] + [blk, blk],
        out_specs=[blk] * 4)
    return pl.pallas_call(
        body, name=name, grid_spec=grid_spec, out_shape=[jax.ShapeDtypeStruct(w.shape, F32)] * 4,
        compiler_params=_cp("parallel", "parallel", "parallel"),
    )(jnp.reshape(c, (1,)).astype(jnp.int32), w, *g_mine, *g_other, m, v)


BIG = ("w_in", "w_branch_a", "w_branch_b", "w_out", "ffn_w_gate", "ffn_w_up", "ffn_w_down")
ROW_SHARDED = ("w_out", "ffn_w_down")
SMALL = ("norm1_g", "dn_conv_w", "dn_a_log", "dn_dt_bias", "dn_onorm_g", "sg_ln_g", "sg_ln_b", "sg_w", "sg_b",
         "norm2_g", "ffn_conv_w", "ffn_conv_b", "final_norm_g")
SMALL_SHARDED = ("dn_conv_w", "ffn_conv_w")


def _pack_rows(arrs, mult):
    flat = jnp.concatenate([jnp.reshape(a, (-1,)) for a in arrs])
    n = flat.shape[0]
    rows = -(-n // (128 * mult)) * mult
    return jnp.reshape(jnp.pad(flat, (0, rows * 128 - n)), (rows, 128))


def _unpack(flat2d, shapes):
    flat = jnp.reshape(flat2d, (-1,))
    out, off = [], 0
    for shp in shapes:
        n = math.prod(shp)
        out.append(jnp.reshape(flat[off:off + n], shp))
        off += n
    return out


def kernel(x, norm1_g, w_in, dn_conv_w, dn_a_log, dn_dt_bias, dn_onorm_g, sg_ln_g, sg_ln_b, sg_w, sg_b, w_branch_a, w_branch_b, w_out, norm2_g, ffn_w_gate, ffn_w_up, ffn_conv_w, ffn_conv_b, ffn_w_down, final_norm_g, loss_target, m_norm1_g, m_w_in, m_dn_conv_w, m_dn_a_log, m_dn_dt_bias, m_dn_onorm_g, m_sg_ln_g, m_sg_ln_b, m_sg_w, m_sg_b, m_w_branch_a, m_w_branch_b, m_w_out, m_norm2_g, m_ffn_w_gate, m_ffn_w_up, m_ffn_conv_w, m_ffn_conv_b, m_ffn_w_down, m_final_norm_g, v_norm1_g, v_w_in, v_dn_conv_w, v_dn_a_log, v_dn_dt_bias, v_dn_onorm_g, v_sg_ln_g, v_sg_ln_b, v_sg_w, v_sg_b, v_w_branch_a, v_w_branch_b, v_w_out, v_norm2_g, v_ffn_w_gate, v_ffn_w_up, v_ffn_conv_w, v_ffn_conv_b, v_ffn_w_down, v_final_norm_g):
    W = dict(norm1_g=norm1_g, w_in=w_in, dn_conv_w=dn_conv_w, dn_a_log=dn_a_log, dn_dt_bias=dn_dt_bias,
             dn_onorm_g=dn_onorm_g, sg_ln_g=sg_ln_g, sg_ln_b=sg_ln_b, sg_w=sg_w, sg_b=sg_b, w_branch_a=w_branch_a,
             w_branch_b=w_branch_b, w_out=w_out, norm2_g=norm2_g, ffn_w_gate=ffn_w_gate, ffn_w_up=ffn_w_up,
             ffn_conv_w=ffn_conv_w, ffn_conv_b=ffn_conv_b, ffn_w_down=ffn_w_down, final_norm_g=final_norm_g)
    M = dict(norm1_g=m_norm1_g, w_in=m_w_in, dn_conv_w=m_dn_conv_w, dn_a_log=m_dn_a_log, dn_dt_bias=m_dn_dt_bias,
             dn_onorm_g=m_dn_onorm_g, sg_ln_g=m_sg_ln_g, sg_ln_b=m_sg_ln_b, sg_w=m_sg_w, sg_b=m_sg_b,
             w_branch_a=m_w_branch_a, w_branch_b=m_w_branch_b, w_out=m_w_out, norm2_g=m_norm2_g,
             ffn_w_gate=m_ffn_w_gate, ffn_w_up=m_ffn_w_up, ffn_conv_w=m_ffn_conv_w, ffn_conv_b=m_ffn_conv_b,
             ffn_w_down=m_ffn_w_down, final_norm_g=m_final_norm_g)
    V = dict(norm1_g=v_norm1_g, w_in=v_w_in, dn_conv_w=v_dn_conv_w, dn_a_log=v_dn_a_log, dn_dt_bias=v_dn_dt_bias,
             dn_onorm_g=v_dn_onorm_g, sg_ln_g=v_sg_ln_g, sg_ln_b=v_sg_ln_b, sg_w=v_sg_w, sg_b=v_sg_b,
             w_branch_a=v_w_branch_a, w_branch_b=v_w_branch_b, w_out=v_w_out, norm2_g=v_norm2_g,
             ffn_w_gate=v_ffn_w_gate, ffn_w_up=v_ffn_w_up, ffn_conv_w=v_ffn_conv_w, ffn_conv_b=v_ffn_conv_b,
             ffn_w_down=v_ffn_w_down, final_norm_g=v_final_norm_g)
    cx, cy, cc = _coords()
    chip = 2 * cx + cy
    L = w_in.shape[0]

    D = w_in.shape[1]
    cs_in = w_in.shape[2]
    c1 = 4 * WD
    ba_chip, ba_off = c1 // cs_in, c1 % cs_in
    assert ba_off + 2 * H <= cs_in
    n_main = N_CHIPS * cs_in - 2 * H
    main_start = [i * cs_in - (2 * H if i > ba_chip else 0) for i in range(N_CHIPS)]
    main_len = [cs_in - (2 * H if i == ba_chip else 0) for i in range(N_CHIPS)]
    tile0 = [16 * (s // 16) for s in main_start]
    shift = [s - t for s, t in zip(main_start, tile0)]
    rp_in = -(-max(sh + ln for sh, ln in zip(shift, main_len)) // 32) * 32
    seg = [tile0[i + 1] - tile0[i] for i in range(N_CHIPS - 1)] + [n_main - tile0[-1]]
    assert all(s + 16 <= rp_in for s in seg[:-1]) and seg[-1] <= rp_in and tile0[-1] + rp_in <= n_main + 128
    my_shift = jnp.asarray(shift, jnp.int32)[chip]

    mine = {n: W[n].astype(BF16) for n in BIG if n != "w_in"}
    wt = jnp.swapaxes(W["w_in"], 1, 2).astype(BF16)
    ba = wt[:, ba_off:ba_off + 2 * H]
    local_row = lax.broadcasted_iota(jnp.int32, (cs_in, 1), 0)
    without_ba = jnp.where(local_row < ba_off, wt, jnp.pad(wt[:, 2 * H:], ((0, 0), (0, 2 * H), (0, 0))))
    mine["w_in"] = lax.dynamic_update_slice(jnp.zeros((L, rp_in, D), BF16),
                                            jnp.where(chip == ba_chip, without_ba, wt), (0, my_shift, 0))
    mine["w_ba"] = jnp.pad(jnp.where(chip == ba_chip, ba, jnp.zeros_like(ba)), ((0, 0), (0, 32 - 2 * H), (0, 0)))

    def halves(a, lead=0):
        return jnp.reshape(a, a.shape[:lead] + (2, a.shape[lead] // 2) + a.shape[lead + 1:])

    first = [(0, "w_in"), (0, "w_ba")]
    my_taps = _pack_rows([W[n] for n in SMALL_SHARDED], 32)
    first_gathered = _ag_layers([halves(mine[n][l]) for l, n in first] + [halves(my_taps)])
    all_taps = jnp.reshape(first_gathered[-1], (N_CHIPS,) + my_taps.shape)
    tap_shards = [_unpack(jnp.where(chip == i, my_taps, all_taps[i]), [W[n].shape for n in SMALL_SHARDED])
                  for i in range(N_CHIPS)]
    taps_full = {n: jnp.concatenate([tap_shards[i][k] for i in range(N_CHIPS)], axis=-1)
                 for k, n in enumerate(SMALL_SHARDED)}

    ops = []
    for l in range(L):
        p = {n: W[n][l] for n in W if n not in ("final_norm_g",) + BIG + SMALL_SHARDED}
        p.update({n: taps_full[n][l] for n in SMALL_SHARDED})
        ops.append(_prep_small(p))

    def gather_payload(items):
        return ("gather", [halves(mine[n][l]) for l, n in items])

    def weights_landed(items, gathered):
        got = {}
        for (l, n), a in zip(items, gathered):
            a = jnp.reshape(a, (N_CHIPS,) + mine[n].shape[1:])
            got[(l, n)] = [jnp.where(chip == i, mine[n][l], a[i]) for i in range(N_CHIPS)]
        for (l, n), parts in got.items():
            if n == "w_in":
                pieces = [parts[0][:seg[0]]]
                for i in range(1, N_CHIPS):
                    pieces += [parts[i][:16] + parts[i - 1][seg[i - 1]:seg[i - 1] + 16], parts[i][16:seg[i]]]
                ba_rows = got[(l, "w_ba")][ba_chip][:2 * H]
                ops[l]["w_in_t"] = jnp.concatenate(pieces + [ba_rows, jnp.zeros((128 - 2 * H, D), BF16)], axis=0)
            elif n != "w_ba":
                ops[l][n] = jnp.concatenate(parts, axis=0 if n in ROW_SHARDED else 1)

    partial_sums, chip_sums = {}, {}

    def grad_partials(l, names, g):
        Gs = []
        for n in names:
            if n == "w_in":
                a = jnp.stack([g["w_in_t"][t:t + rp_in] for t in tile0])
            elif n == "w_ba":
                a = jnp.broadcast_to(g["w_in_t"][n_main:n_main + 32][None], (N_CHIPS, 32, D))
            elif n in ROW_SHARDED:
                a = jnp.reshape(g[n], (N_CHIPS, g[n].shape[0] // N_CHIPS, g[n].shape[1]))
            else:
                a = jnp.moveaxis(jnp.reshape(g[n], (g[n].shape[0], N_CHIPS, g[n].shape[1] // N_CHIPS)), 1, 0)
            Gs.append(halves(a, 1))
        B1s = _rs_pair_exchange(Gs)
        for n, a, b in zip(names, Gs, B1s):
            partial_sums[(l, n)] = _rs_add_pair(a, b, cc, "rs_add_pair_" + n)

    def carrier(l, plan, landed):
        def payload(kernel):
            items = plan.get((l, kernel))
            if not items:
                return None
            return gather_payload(items) if landed is weights_landed else ("exchange", [partial_sums[i] for i in items])
        return _Carrier(payload, lambda kernel, res: landed(plan[(l, kernel)], res))

    FFN = ("ffn_w_gate", "ffn_w_up", "ffn_w_down")
    REST = ("w_in", "w_ba", "w_branch_a", "w_branch_b", "w_out")
    fwd_plan = {(0, "proj"): [(0, "w_branch_a"), (0, "w_branch_b"), (0, "w_out"), (0, "ffn_w_gate")],
                (0, "dn_core"): [(0, "ffn_w_up"), (0, "ffn_w_down"), (1, "w_in"), (1, "w_ba")],
                (0, "ffn_gate"): [(1, "w_branch_a"), (1, "w_branch_b"), (1, "w_out")],
                (0, "ffn_up"): [(1, "ffn_w_gate")],
                (0, "ffn_down"): [(1, "ffn_w_up")],
                (1, "proj"): [(1, "ffn_w_down")]}
    bwd_plan = {(1, "dn_core"): [(1, n) for n in FFN],
                (0, "d_act"): [(1, "w_branch_a"), (1, "w_branch_b"), (1, "w_out")],
                (0, "dn_core"): [(1, "w_in"), (1, "w_ba"), (0, "ffn_w_down")],
                (0, "dw_in"): [(0, "ffn_w_gate"), (0, "ffn_w_up")],
                (0, "dh"): [(0, n) for n in REST]}

    def sums_landed(items, res):
        chip_sums.update(zip(items, res))

    weights_landed(first, first_gathered[:-1])
    xs, saved = x[0], []
    for l in range(L):
        xs, s = _layer_fwd(xs, ops[l], carrier(l, fwd_plan, weights_landed))
        saved.append(s)
    dx, dgf, loss = _loss_head(xs, final_norm_g[None], loss_target[0])
    loss = loss[0, 0]
    grads = [None] * L
    for l in reversed(range(L)):
        dx, grads[l] = _layer_bwd(dx, ops[l], saved[l], carrier(l, bwd_plan, sums_landed),
                                  functools.partial(grad_partials, l, FFN), functools.partial(grad_partials, l, REST))
    travelled = BIG + ("w_ba",)
    g_mine = [[_rs_sum_chips(partial_sums[(l, n)], chip_sums[(l, n)], chip, "rs_sum_chips_" + n) for n in travelled]
              for l in range(L)]
    swapped = _rs_pair_swap(g_mine[0] + g_mine[1])
    g_other = [swapped[:len(travelled)], swapped[len(travelled):]]

    def both_halves(l, n):
        a, b = g_mine[l][travelled.index(n)], g_other[l][travelled.index(n)]
        return jnp.where(cc == 0, jnp.concatenate([a, b]), jnp.concatenate([b, a]))

    def w_in_grad_rows(l):
        m = lax.dynamic_slice_in_dim(both_halves(l, "w_in"), my_shift, cs_in, axis=0)
        ba_rows = jnp.pad(both_halves(l, "w_ba")[:2 * H], ((ba_off, cs_in - ba_off - 2 * H), (0, 0)))
        moved = jnp.pad(m[:cs_in - 2 * H], ((2 * H, 0), (0, 0)))
        with_ba = jnp.where(local_row < ba_off, m, jnp.where(local_row < ba_off + 2 * H, ba_rows, moved))
        return jnp.where(chip == ba_chip, with_ba, m)

    small = {n: jnp.stack([g[n] for g in grads]) for n in SMALL if n != "final_norm_g"}
    small["final_norm_g"] = dgf
    shapes = [taps_full[n].shape if n in SMALL_SHARDED else W[n].shape for n in SMALL] + [(1,)]
    sflat = _pack_rows([small[n] for n in SMALL] + [jnp.reshape(loss, (1,))], 16)
    sred = _unpack(_sum_slots(_ag8(sflat), "sum_small"), shapes)
    g_small = dict(zip(SMALL, sred[:-1]))
    loss_total = sred[-1][0]
    for n in SMALL_SHARDED:
        cs = W[n].shape[-1]
        g_small[n] = lax.dynamic_slice_in_dim(g_small[n], chip * cs, cs, axis=-1)

    g_big, delta, new_m, new_v = {}, {}, {}, {}
    for k, n in enumerate(BIG):
        gm, go = [g_mine[l][k] for l in range(L)], [g_other[l][k] for l in range(L)]
        if n == "w_in":
            g_t = jnp.stack([w_in_grad_rows(l) for l in range(L)], axis=1)
            outs = _adamw(*[jnp.transpose(a, (2, 0, 1)) for a in (W[n],)], g_t,
                          *[jnp.transpose(a, (2, 0, 1)) for a in (M[n], V[n])], "adamw_" + n)
            g_big[n], delta[n], new_m[n], new_v[n] = [jnp.transpose(o, (1, 2, 0)) for o in (g_t,) + tuple(outs)]
        else:
            outs = _adamw_halves(halves(W[n], 1), gm, go, cc, halves(M[n], 1), halves(V[n], 1), "adamw_" + n)
            g_big[n], delta[n], new_m[n], new_v[n] = [jnp.reshape(o, W[n].shape) for o in outs]
    for n in SMALL:
        shp = W[n].shape
        as3d = (1,) * (3 - len(shp)) + shp if len(shp) <= 3 else (-1,) + shp[-2:]
        outs = _adamw(*[jnp.reshape(d[n], as3d) for d in (W, g_small, M, V)], "adamw_" + n)
        delta[n], new_m[n], new_v[n] = [jnp.reshape(o, shp) for o in outs]

    names = list(W)
    grad_w = {**g_big, **g_small}
    return (loss_total, dx[None], *[grad_w[n] for n in names], *[delta[n] for n in names],
            *[new_m[n] for n in names], *[new_v[n] for n in names])
```

```python
import functools
import math

import jax
import jax.numpy as jnp
from jax import lax
from jax.experimental import pallas as pl
from jax.experimental.pallas import tpu as pltpu

F32 = jnp.float32
BF16 = jnp.bfloat16
MESH = pl.DeviceIdType.MESH

EPS = 1e-6
H = 8
DH = 128
WD = H * DH
DNC = 64
SGC = 128
DN_K = 4
FF_K = 3
DEPTH = 2
N_CHIPS = 4

ADAM_LR = 0.001
ADAM_B1 = 0.9
ADAM_B2 = 0.999
ADAM_EPS = 1e-08
ADAM_WD = 0.01
ADAM_STEP = 10

VMEM_LIMIT = 56 * 1024 * 1024

NN = (((1,), (0,)), ((), ()))
NT = (((1,), (1,)), ((), ()))
TN = (((0,), (0,)), ((), ()))

OQ, OZ, OU, OV, OGA = 0, 3 * WD, 4 * WD, 5 * WD, 6 * WD


def _cp(*sem):
    return pltpu.CompilerParams(dimension_semantics=sem or None, vmem_limit_bytes=VMEM_LIMIT)


def _tile(dim, pref, unit=128):
    if dim <= pref:
        return dim
    t = (pref // unit) * unit
    while t >= unit:
        if dim % t == 0:
            return t
        t -= unit
    return dim


def _bdot(a, b, dn=NN):
    return lax.dot_general(a.astype(BF16), b.astype(BF16), dn, preferred_element_type=F32)


def _lsum(x):
    return jnp.sum(x, axis=1, keepdims=True)


def _sig(x):
    return 0.5 * jnp.tanh(0.5 * x) + 0.5


def _silu_and_grad(x):
    s = _sig(x)
    return x * s, s * (1.0 + x * (1.0 - s))


def _gelu_parts(x):
    a = jnp.abs(x) * (2.0 ** -0.5)
    t = 1.0 / (1.0 + 0.3275911 * a)
    poly = t * (0.254829592 + t * (-0.284496736 + t * (1.421413741 + t * (-1.453152027 + t * 1.061405429))))
    e = jnp.exp(-a * a)
    half = 0.5 * poly * e
    return jnp.where(x < 0, half, 1.0 - half), e * (1.0 / math.sqrt(2.0 * math.pi))


def _gelu(x):
    return x * _gelu_parts(x)[0]


def _gelu_and_grad(x):
    cdf, pdf = _gelu_parts(x)
    return x * cdf, cdf + x * pdf


def _shift_down(x, k):
    if k == 0:
        return x
    y = pltpu.roll(x, k, 0)
    rows = lax.broadcasted_iota(jnp.int32, (8, x.shape[1]), 0)
    return jnp.concatenate([jnp.where(rows >= k, y[:8], 0.0), y[8:]], axis=0)


def _shift_up(x, k):
    if k == 0:
        return x
    n = x.shape[0]
    y = pltpu.roll(x, n - k, 0)
    rows = lax.broadcasted_iota(jnp.int32, (8, x.shape[1]), 0)
    return jnp.concatenate([y[:n - 8], jnp.where(rows < 8 - k, y[n - 8:], 0.0)], axis=0)


def _comm_fns(comm):
    if not comm:
        return None, None, dict(out_shape=[], specs=[], sems=[]), ()
    kind, arrays = comm
    start, finish, specs = {"gather": (_ag_start, _ag_finish, _ag_specs),
                            "exchange": (_rsx_start, _rsx_finish, _rsx_specs),
                            "pair": (_pair_start, _pair_finish, _pair_specs)}[kind]
    return start, finish, specs(arrays), tuple(arrays)


def _mm(a, b, mode, out_dtype, add=None, name="mm", comm=None):
    if mode == "tn":
        K, M = a.shape
    else:
        M, K = a.shape
    N = b.shape[0] if mode == "nt" else b.shape[1]
    tm, tn, tk = _tile(M, 1152), _tile(N, 1536), _tile(K, 3584)
    nk = K // tk
    ni, nj = M // tm, N // tn
    dn = {"nn": NN, "nt": NT, "tn": TN}[mode]
    c_start, c_finish, c_sp, payload = _comm_fns(comm)
    nc = len(payload)
    n_add = 0 if add is None else 1

    def body(*refs):
        a_ref, b_ref = refs[:2]
        add_ref = refs[2] if n_add else None
        c_in = refs[2 + n_add:2 + n_add + nc]
        o_ref = refs[2 + n_add + nc]
        c_out = refs[3 + n_add + nc:3 + n_add + 2 * nc]
        rest = refs[3 + n_add + 2 * nc:]
        acc_ref = rest[0] if nk > 1 else None
        sems = rest[1:] if nk > 1 else rest
        i, j, k = pl.program_id(0), pl.program_id(1), pl.program_id(2)

        if nc:
            @pl.when(jnp.logical_and(jnp.logical_and(i == 0, j == 0), k == 0))
            def _():
                c_start(c_in, c_out, *sems)

        def finish(r):
            if add is not None:
                r = r + add_ref[...]
            o_ref[...] = r.astype(o_ref.dtype)

        part = lax.dot_general(a_ref[...], b_ref[...], dn, preferred_element_type=F32)
        if nk == 1:
            finish(part)
        else:
            @pl.when(k == 0)
            def _():
                acc_ref[...] = part

            @pl.when(k > 0)
            def _():
                acc_ref[...] += part

            @pl.when(k == nk - 1)
            def _():
                finish(acc_ref[...])

        if nc:
            @pl.when(jnp.logical_and(jnp.logical_and(i == ni - 1, j == nj - 1), k == nk - 1))
            def _():
                c_finish(c_in, c_out, *sems)

    a_spec = (pl.BlockSpec((tk, tm), lambda i, j, k: (k, i)) if mode == "tn"
              else pl.BlockSpec((tm, tk), lambda i, j, k: (i, k)))
    b_spec = (pl.BlockSpec((tn, tk), lambda i, j, k: (j, k)) if mode == "nt"
              else pl.BlockSpec((tk, tn), lambda i, j, k: (k, j)))
    o_spec = pl.BlockSpec((tm, tn), lambda i, j, k: (i, j))
    in_specs = [a_spec, b_spec] + ([o_spec] if add is not None else []) + c_sp["specs"]
    args = (a, b) + ((add,) if add is not None else ()) + payload
    outs = pl.pallas_call(
        body, name=name + ("_" + comm[0] if nc else ""), grid=(ni, nj, nk), in_specs=in_specs,
        out_specs=[o_spec] + c_sp["specs"],
        out_shape=[jax.ShapeDtypeStruct((M, N), out_dtype)] + c_sp["out_shape"],
        scratch_shapes=([pltpu.VMEM((tm, tn), F32)] if nk > 1 else []) + c_sp["sems"],
        compiler_params=_cp("arbitrary", "arbitrary", "arbitrary") if nc else _cp("parallel", "parallel", "arbitrary"),
    )(*args)
    return (outs[0], list(outs[1:])) if nc else outs[0]


def _rms_fwd(x, g, name):
    T, D = x.shape
    tt = _tile(T, 256, 16)

    def body(x_ref, g_ref, o_ref):
        xv = x_ref[...]
        r = lax.rsqrt(jnp.mean(xv * xv, axis=-1, keepdims=True) + EPS)
        o_ref[...] = (xv * r * g_ref[...]).astype(o_ref.dtype)

    return pl.pallas_call(
        body, name=name, grid=(T // tt,),
        in_specs=[pl.BlockSpec((tt, D), lambda i: (i, 0)), pl.BlockSpec((1, D), lambda i: (0, 0))],
        out_specs=pl.BlockSpec((tt, D), lambda i: (i, 0)),
        out_shape=jax.ShapeDtypeStruct((T, D), BF16), compiler_params=_cp("parallel"),
    )(x, g)


def _rms_bwd(x, g, dh, dres, name):
    T, D = x.shape
    tt = _tile(T, 256, 16)

    def body(x_ref, g_ref, dh_ref, dres_ref, dx_ref, dg_ref):
        @pl.when(pl.program_id(0) == 0)
        def _():
            dg_ref[...] = jnp.zeros_like(dg_ref)

        xv = x_ref[...]
        r = lax.rsqrt(jnp.mean(xv * xv, axis=-1, keepdims=True) + EPS)
        xh = xv * r
        dh_v = dh_ref[...]
        dy = dh_v * g_ref[...]
        dx_ref[...] = dres_ref[...] + r * (dy - xh * jnp.mean(dy * xh, axis=-1, keepdims=True))
        dg_ref[...] += jnp.sum(dh_v * xh, axis=0, keepdims=True)

    row = pl.BlockSpec((tt, D), lambda i: (i, 0))
    vec = pl.BlockSpec((1, D), lambda i: (0, 0))
    return pl.pallas_call(
        body, name=name, grid=(T // tt,), in_specs=[row, vec, row, row], out_specs=[row, vec],
        out_shape=[jax.ShapeDtypeStruct((T, D), F32), jax.ShapeDtypeStruct((1, D), F32)],
        compiler_params=_cp("arbitrary"),
    )(x, g, dh, dres)


def _loss_head(x, g, tgt, name="loss_head"):
    T, D = x.shape
    tt = _tile(T, 256, 16)

    def body(x_ref, g_ref, t_ref, dx_ref, dg_ref, loss_ref):
        @pl.when(pl.program_id(0) == 0)
        def _():
            dg_ref[...] = jnp.zeros_like(dg_ref)
            loss_ref[...] = jnp.zeros_like(loss_ref)

        xv = x_ref[...]
        r = lax.rsqrt(jnp.mean(xv * xv, axis=-1, keepdims=True) + EPS)
        xh = xv * r
        err = xh * g_ref[...] - t_ref[...]
        part = 0.5 * jnp.sum(jnp.mean(err * err, axis=-1, keepdims=True), axis=0, keepdims=True)
        loss_ref[...] += jnp.broadcast_to(part, loss_ref.shape)
        dy = err * (1.0 / D)
        dg_ref[...] += jnp.sum(dy * xh, axis=0, keepdims=True)
        dyh = dy * g_ref[...]
        dx_ref[...] = r * (dyh - xh * jnp.mean(dyh * xh, axis=-1, keepdims=True))

    row = pl.BlockSpec((tt, D), lambda i: (i, 0))
    vec = pl.BlockSpec((1, D), lambda i: (0, 0))
    return pl.pallas_call(
        body, name=name, grid=(T // tt,), in_specs=[row, vec, row],
        out_specs=[row, vec, pl.BlockSpec((1, 128), lambda i: (0, 0))],
        out_shape=[jax.ShapeDtypeStruct((T, D), F32), jax.ShapeDtypeStruct((1, D), F32),
                   jax.ShapeDtypeStruct((1, 128), F32)],
        compiler_params=_cp("arbitrary"),
    )(x, g, tgt)


def _ba_fwd(proj, alog, dtb, oba, name="dn_ba_fwd"):
    T = proj.shape[0]
    tt = _tile(T, 512, 8)

    def body(p_ref, al_ref, dt_ref, o_ref):
        raw = p_ref[...].astype(F32)
        lane = lax.broadcasted_iota(jnp.int32, raw.shape, 1)
        z = raw + dt_ref[...]
        sp = jnp.maximum(z, 0.0) + jnp.log(1.0 + jnp.exp(-jnp.abs(z)))
        gl = -jnp.exp(al_ref[...]) * sp
        o_ref[...] = jnp.where(lane < H, _sig(raw), jnp.where(lane < 2 * H, gl, 0.0))

    vec = pl.BlockSpec((1, 128), lambda i: (0, 0))
    return pl.pallas_call(
        body, name=name, grid=(T // tt,),
        in_specs=[pl.BlockSpec((tt, 128), lambda i: (i, oba // 128)), vec, vec],
        out_specs=pl.BlockSpec((tt, 128), lambda i: (i, 0)),
        out_shape=jax.ShapeDtypeStruct((T, 128), F32), compiler_params=_cp("parallel"),
    )(proj, alog, dtb)


def _ba_bwd(proj, alog, dtb, dbg, oba, name="dn_ba_bwd"):
    T = proj.shape[0]
    tt = _tile(T, 512, 16)

    def body(p_ref, al_ref, dt_ref, d_ref, o_ref, dal_ref, ddt_ref):
        @pl.when(pl.program_id(0) == 0)
        def _():
            dal_ref[...] = jnp.zeros_like(dal_ref)
            ddt_ref[...] = jnp.zeros_like(ddt_ref)

        raw = p_ref[...].astype(F32)
        d = d_ref[...]
        lane = lax.broadcasted_iota(jnp.int32, raw.shape, 1)
        z = raw + dt_ref[...]
        sp = jnp.maximum(z, 0.0) + jnp.log(1.0 + jnp.exp(-jnp.abs(z)))
        na = -jnp.exp(al_ref[...])
        is_g = jnp.logical_and(lane >= H, lane < 2 * H)
        b = _sig(raw)
        dz = jnp.where(is_g, d * na * _sig(z), 0.0)
        o_ref[...] = jnp.where(lane < H, d * b * (1.0 - b), dz).astype(o_ref.dtype)
        dal_ref[...] += jnp.sum(jnp.where(is_g, d * na * sp, 0.0), axis=0, keepdims=True)
        ddt_ref[...] += jnp.sum(dz, axis=0, keepdims=True)

    vec = pl.BlockSpec((1, 128), lambda i: (0, 0))
    return pl.pallas_call(
        body, name=name, grid=(T // tt,),
        in_specs=[pl.BlockSpec((tt, 128), lambda i: (i, oba // 128)), vec, vec,
                  pl.BlockSpec((tt, 128), lambda i: (i, 0))],
        out_specs=[pl.BlockSpec((tt, 128), lambda i: (i, 0)), vec, vec],
        out_shape=[jax.ShapeDtypeStruct((T, 128), BF16), jax.ShapeDtypeStruct((1, 128), F32),
                   jax.ShapeDtypeStruct((1, 128), F32)],
        compiler_params=_cp("arbitrary"),
    )(proj, alog, dtb, dbg)


def _dn_prep_fwd(proj, convw, name="dn_prep_fwd"):
    T = proj.shape[0]
    nblk = 3 * H

    def body(p_ref, w_ref, o_ref):
        j = pl.program_id(0)
        xv = p_ref[...].astype(F32)
        w = w_ref[...]
        c = xv * w[DN_K - 1:DN_K, :]
        for k in range(1, DN_K):
            c = c + _shift_down(xv, k) * w[DN_K - 1 - k:DN_K - k, :]
        s = c * _sig(c)
        r = lax.rsqrt(_lsum(s * s) + EPS)
        o_ref[...] = jnp.where(j < 2 * H, s * r, s)

    return pl.pallas_call(
        body, name=name, grid=(nblk,),
        in_specs=[pl.BlockSpec((T, DH), lambda j: (0, j)), pl.BlockSpec((DN_K, DH), lambda j: (0, j))],
        out_specs=pl.BlockSpec((T, DH), lambda j: (0, j)),
        out_shape=jax.ShapeDtypeStruct((T, 3 * WD), F32), compiler_params=_cp("parallel"),
    )(proj, convw)


def _dn_prep_bwd(proj, convw, dq, dk, dv, name="dn_prep_bwd"):
    T = proj.shape[0]
    nblk = 3 * H

    def body(p_ref, w_ref, dq_ref, dk_ref, dv_ref, dx_ref, dw_ref):
        j = pl.program_id(0)
        xv = p_ref[...].astype(F32)
        w = w_ref[...]
        shifted = [_shift_down(xv, k) for k in range(DN_K)]
        c = shifted[0] * w[DN_K - 1:DN_K, :]
        for k in range(1, DN_K):
            c = c + shifted[k] * w[DN_K - 1 - k:DN_K - k, :]
        s, s_grad = _silu_and_grad(c)
        r = lax.rsqrt(_lsum(s * s) + EPS)
        y = s * r
        dy = jnp.where(j < H, dq_ref[...], jnp.where(j < 2 * H, dk_ref[...], dv_ref[...]))
        ds = jnp.where(j < 2 * H, r * (dy - y * _lsum(dy * y)), dy)
        dc = ds * s_grad
        dx = dc * w[DN_K - 1:DN_K, :]
        for k in range(1, DN_K):
            dx = dx + _shift_up(dc, k) * w[DN_K - 1 - k:DN_K - k, :]
        dx_ref[...] = dx.astype(dx_ref.dtype)
        rows = [jnp.sum(dc * shifted[DN_K - 1 - t], axis=0, keepdims=True) for t in range(DN_K)]
        dw_ref[...] = jnp.concatenate(rows, axis=0)

    hb = lambda off: pl.BlockSpec((T, DH), lambda j: (0, jnp.maximum(jnp.minimum(j - off, H - 1), 0)))
    return pl.pallas_call(
        body, name=name, grid=(nblk,),
        in_specs=[pl.BlockSpec((T, DH), lambda j: (0, j)), pl.BlockSpec((DN_K, DH), lambda j: (0, j)),
                  hb(0), hb(H), hb(2 * H)],
        out_specs=[pl.BlockSpec((T, DH), lambda j: (0, j)), pl.BlockSpec((DN_K, DH), lambda j: (0, j))],
        out_shape=[jax.ShapeDtypeStruct((T, 3 * WD), BF16), jax.ShapeDtypeStruct((DN_K, 3 * WD), F32)],
        compiler_params=_cp("parallel"),
    )(proj, convw, dq, dk, dv)


DN_BLOCK = 4


def _split3(a):
    hi = a.astype(BF16)
    r1 = a - hi.astype(F32)
    mid = r1.astype(BF16)
    return hi, mid, (r1 - mid.astype(F32)).astype(BF16)


def _dot3(a, b, dn=NN):
    ah, al, _ = _split3(a)
    bh, bl, _ = _split3(b)
    d = lambda p, q: lax.dot_general(p, q, dn, preferred_element_type=F32)
    return d(ah, bh) + d(ah, bl) + d(al, bh)


def _mask_dot(m, b, dn=NN):
    mb = m.astype(BF16)
    d = lambda q: (lax.dot_general(mb, q, dn, preferred_element_type=F32) if dn != TN
                   else lax.dot_general(q, mb, dn, preferred_element_type=F32))
    b0, b1, b2 = _split3(b)
    return d(b0) + d(b1) + d(b2)


def _tri_inv(A):
    ri = lax.broadcasted_iota(jnp.int32, A.shape, 0)
    ci = lax.broadcasted_iota(jnp.int32, A.shape, 1)
    T = jnp.where(ri == ci, 1.0, 0.0) - jnp.where((ri // 2) == (ci // 2), A, 0.0)
    s = 2
    while s < DNC:
        off = jnp.logical_and((ri // (2 * s)) == (ci // (2 * s)), (ri // s) != (ci // s))
        T = T - _dot3(_dot3(T, jnp.where(off, A, 0.0)), T)
        s *= 2
    return T


GH = 4
NG = H // GH
GR = GH * DNC
GK = GH * DH


def _dn_masks():
    ri = lax.broadcasted_iota(jnp.int32, (GR, GR), 0)
    ci = lax.broadcasted_iota(jnp.int32, (GR, GR), 1)
    blk = (ri // DNC) == (ci // DNC)
    wide = (lax.broadcasted_iota(jnp.int32, (GR, GK), 0) // DNC) == (lax.broadcasted_iota(jnp.int32, (GR, GK), 1) // DH)
    return dict(blk=blk, causal=jnp.logical_and(blk, ri >= ci), strict=jnp.logical_and(blk, ri > ci),
                upper=jnp.logical_and(blk, ri <= ci), eye=ri == ci, wide=wide)


def _wide(a, mk):
    return jnp.where(mk["wide"], jnp.tile(a, (1, GH)), 0.0)


def _fold(a, mk):
    a = jnp.where(mk["wide"], a, 0.0)
    out = a[:, :DH]
    for j in range(1, GH):
        out = out + a[:, j * DH:(j + 1) * DH]
    return out


def _stack_heads(ref, rows, g):
    return jnp.concatenate([ref[rows, (g * GH + j) * DH:(g * GH + j + 1) * DH] for j in range(GH)], axis=0)


def _dn_group(q_ref, k_ref, v_ref, rows, bg, gc_cols, g, mk):
    heads = [g * GH + j for j in range(GH)]
    col = lambda a, lane: jnp.concatenate([a[:, lane(h):lane(h) + 1] for h in heads], axis=0)
    q = _stack_heads(q_ref, rows, g) * (DH ** -0.5)
    k = _stack_heads(k_ref, rows, g)
    v = _stack_heads(v_ref, rows, g)
    beta = col(bg, lambda h: h)
    gcol = col(gc_cols, lambda h: H + h)
    last = [gc_cols[DNC - 1:DNC, H + h:H + h + 1] for h in heads]
    gl = jnp.concatenate([jnp.broadcast_to(t, (DNC, 1)) for t in last], axis=0)
    egl_state = jnp.concatenate([jnp.broadcast_to(jnp.exp(t), (DH, 1)) for t in last], axis=0)
    grow = _mask_dot(jnp.ones((GR, GR), F32), jnp.where(mk["eye"], gcol, 0.0))
    dec = jnp.where(mk["causal"], jnp.exp(jnp.where(mk["causal"], gcol - grow, 0.0)), 0.0)
    eg = jnp.exp(gcol)
    ek = jnp.exp(gl - gcol)
    kb = k * beta
    vb = v * beta
    kbe = kb * eg
    A = jnp.where(mk["strict"], _bdot(kb, k, NT) * dec, 0.0)
    P = jnp.where(mk["causal"], _bdot(q, k, NT) * dec, 0.0)
    return dict(q=q, k=k, v=v, beta=beta, dec=dec, eg=eg, ek=ek, egl=jnp.exp(gl), egl_state=egl_state, kb=kb, vb=vb,
                kbe=kbe, A=A, P=P, qd=q * eg, kd=k * ek, heads=heads)


def _gc_cols(bg):
    ri = lax.broadcasted_iota(jnp.int32, (DNC, DNC), 0)
    ci = lax.broadcasted_iota(jnp.int32, (DNC, DNC), 1)
    return _mask_dot(jnp.where(ri >= ci, 1.0, 0.0), bg)


def _dn_core_fwd(qkv, bg, comm=None, name="dn_core_fwd"):
    c_start, c_finish, sp, gather = _comm_fns(comm)
    T = qkv.shape[0]
    n_chunks = T // DNC
    nb = _tile(n_chunks, DN_BLOCK, 1)
    tb = nb * DNC

    ng = len(gather)
    n_steps = n_chunks // nb

    def body(*refs):
        q_ref, k_ref, v_ref, bg_ref = refs[:4]
        o_ref, s_ref, tm_ref = refs[4 + ng:7 + ng]
        S_scr = refs[7 + 2 * ng]
        comm_refs = (refs[4:4 + ng], refs[7 + ng:7 + 2 * ng]) + tuple(refs[8 + 2 * ng:])

        @pl.when(pl.program_id(0) == 0)
        def _():
            S_scr[...] = jnp.zeros_like(S_scr)
            if ng:
                c_start(*comm_refs)

        mk = _dn_masks()

        def chunk(n):
            rows = pl.ds(n * DNC, DNC)
            bgc = bg_ref[rows, :]
            gc_cols = _gc_cols(bgc)
            for g in range(NG):
                c = _dn_group(q_ref, k_ref, v_ref, rows, bgc, gc_cols, g, mk)
                Tm = _tri_inv(c["A"])
                tm_ref[n, g] = Tm
                S = S_scr[g]
                s_ref[n, g] = S
                u = _bdot(Tm, c["vb"])
                w = _bdot(Tm, c["kbe"])
                vn = u - _bdot(_wide(w, mk), S)
                o = _bdot(_wide(c["qd"], mk), S) + _bdot(c["P"], vn)
                for j, h in enumerate(c["heads"]):
                    o_ref[rows, h * DH:(h + 1) * DH] = o[j * DNC:(j + 1) * DNC]
                S_scr[g] = S * c["egl_state"] + _bdot(_wide(c["kd"], mk), vn, TN)

        for n in range(nb):
            chunk(n)

        if ng:
            @pl.when(pl.program_id(0) == n_steps - 1)
            def _():
                c_finish(*comm_refs)

    blk = lambda j: pl.BlockSpec((tb, WD), lambda i: (i, j))
    outs = pl.pallas_call(
        body, name=name + ("_" + comm[0] if ng else ""), grid=(n_steps,),
        in_specs=[blk(0), blk(1), blk(2), pl.BlockSpec((tb, 128), lambda i: (i, 0))] + sp["specs"],
        out_specs=[blk(0), pl.BlockSpec((nb, NG, GK, DH), lambda i: (i, 0, 0, 0)),
                   pl.BlockSpec((nb, NG, GR, GR), lambda i: (i, 0, 0, 0))] + sp["specs"],
        out_shape=[jax.ShapeDtypeStruct((T, WD), F32), jax.ShapeDtypeStruct((n_chunks, NG, GK, DH), F32),
                   jax.ShapeDtypeStruct((n_chunks, NG, GR, GR), F32)] + sp["out_shape"],
        scratch_shapes=[pltpu.VMEM((NG, GK, DH), F32)] + (sp["sems"] if ng else []),
        compiler_params=_cp("arbitrary"),
    )(qkv, qkv, qkv, bg, *gather)
    return outs[0], outs[1], outs[2], list(outs[3:])


def _dn_core_bwd(qkv, bg, s_all, tm_all, do, comm=None, name="dn_core_bwd"):
    c_start, c_finish, sp, exchange = _comm_fns(comm)
    T = qkv.shape[0]
    n_chunks = T // DNC
    nb = _tile(n_chunks, DN_BLOCK, 1)
    tb = nb * DNC
    n_blocks = n_chunks // nb

    nx = len(exchange)

    def body(*refs):
        q_ref, k_ref, v_ref, bg_ref, s_ref, tm_ref, do_ref = refs[:7]
        dq_ref, dk_ref, dv_ref, dbg_ref = refs[7 + nx:11 + nx]
        dS_scr = refs[11 + 2 * nx]
        comm_refs = (refs[7:7 + nx], refs[11 + nx:11 + 2 * nx]) + tuple(refs[12 + 2 * nx:])

        @pl.when(pl.program_id(0) == 0)
        def _():
            dS_scr[...] = jnp.zeros_like(dS_scr)
            if nx:
                c_start(*comm_refs)

        lane = lax.broadcasted_iota(jnp.int32, (DNC, 128), 1)
        row = lax.broadcasted_iota(jnp.int32, (GR, 1), 0)

        mk = _dn_masks()

        def chunk(n):
            rows = pl.ds(n * DNC, DNC)
            ones = jnp.ones((GR, GR), F32)
            blk_f = jnp.where(mk["blk"], 1.0, 0.0)
            wide_f = jnp.where(mk["wide"], 1.0, 0.0)
            per_row = lambda m, a: _mask_dot(m, jnp.broadcast_to(a, (a.shape[0], DH)))[:, :1]
            bgc = bg_ref[rows, :]
            gc_cols = _gc_cols(bgc)
            dbg = jnp.zeros((DNC, 128), F32)
            for g in range(NG):
                c = _dn_group(q_ref, k_ref, v_ref, rows, bgc, gc_cols, g, mk)
                q, k, v, beta = c["q"], c["k"], c["v"], c["beta"]
                dec, eg, ek, egl = c["dec"], c["eg"], c["ek"], c["egl"]
                kb, vb, kbe, A, P, qd, kd = c["kb"], c["vb"], c["kbe"], c["A"], c["P"], c["qd"], c["kd"]
                S = s_ref[n, g]
                Tm = tm_ref[n, g]
                u = _bdot(Tm, vb)
                w = _bdot(Tm, kbe)
                w_wide = _wide(w, mk)
                vn = u - _bdot(w_wide, S)
                d_o = _stack_heads(do_ref, rows, g)
                dS1 = dS_scr[g]
                d_qd = _fold(_bdot(d_o, S, NT), mk)
                dP = jnp.where(mk["causal"], _bdot(d_o, vn, NT), 0.0)
                d_vn = _bdot(P, d_o, TN) + _bdot(_wide(kd, mk), dS1)
                d_kd = _fold(_bdot(vn, dS1, NT), mk)
                d_egl = per_row(wide_f, _lsum(dS1 * S))
                dS_scr[g] = dS1 * c["egl_state"] + _bdot(_wide(qd, mk), d_o, TN) - _bdot(w_wide, d_vn, TN)
                d_w = -_fold(_bdot(d_vn, S, NT), mk)
                d_vb = _bdot(Tm, d_vn, TN)
                d_kbe = _bdot(Tm, d_w, TN)
                dA = jnp.where(mk["strict"], -(_bdot(d_vb, u, NT) + _bdot(d_kbe, w, NT)), 0.0)
                dMA = dA * dec
                dMP = dP * dec
                d_kb = _bdot(dMA, k) + d_kbe * eg
                d_k = _bdot(dMA, kb, TN) + _bdot(dMP, q, TN) + d_kd * ek + d_kb * beta
                d_qs = (_bdot(dMP, k) + d_qd * eg) * (DH ** -0.5)
                d_v = d_vb * beta
                E = dA * A + dP * P
                col_sums = _mask_dot(ones, E, TN)[:, :1]
                t_kd = _lsum(d_kd * kd)
                d_gl = per_row(blk_f, t_kd) + d_egl * egl
                d_gc = (_lsum(E) - col_sums + _lsum(d_qd * qd) + _lsum(d_kbe * kbe) - t_kd
                        + jnp.where(row % DNC == DNC - 1, d_gl, 0.0))
                d_g = per_row(jnp.where(mk["upper"], 1.0, 0.0), d_gc)
                d_beta = _lsum(d_kb * k) + _lsum(d_vb * v)
                for j, h in enumerate(c["heads"]):
                    rs = slice(j * DNC, (j + 1) * DNC)
                    dq_ref[rows, h * DH:(h + 1) * DH] = d_qs[rs]
                    dk_ref[rows, h * DH:(h + 1) * DH] = d_k[rs]
                    dv_ref[rows, h * DH:(h + 1) * DH] = d_v[rs]
                    dbg = dbg + jnp.where(lane == h, d_beta[rs], 0.0) + jnp.where(lane == h + H, d_g[rs], 0.0)
            dbg_ref[rows, :] = dbg

        for n in reversed(range(nb)):
            chunk(n)

        if nx:
            @pl.when(pl.program_id(0) == n_blocks - 1)
            def _():
                c_finish(*comm_refs)

    blk = lambda j: pl.BlockSpec((tb, WD), lambda i: (n_blocks - 1 - i, j))
    small = pl.BlockSpec((tb, 128), lambda i: (n_blocks - 1 - i, 0))
    outs = pl.pallas_call(
        body, name=name + ("_" + comm[0] if nx else ""), grid=(n_blocks,),
        in_specs=[blk(0), blk(1), blk(2), small,
                  pl.BlockSpec((nb, NG, GK, DH), lambda i: (n_blocks - 1 - i, 0, 0, 0)),
                  pl.BlockSpec((nb, NG, GR, GR), lambda i: (n_blocks - 1 - i, 0, 0, 0)), blk(0)] + sp["specs"],
        out_specs=[blk(0), blk(0), blk(0), small] + sp["specs"],
        out_shape=[jax.ShapeDtypeStruct((T, WD), F32)] * 3 + [jax.ShapeDtypeStruct((T, 128), F32)] + sp["out_shape"],
        scratch_shapes=[pltpu.VMEM((NG, GK, DH), F32)] + (sp["sems"] if nx else []),
        compiler_params=_cp("arbitrary"),
    )(qkv, qkv, qkv, bg, s_all, tm_all, do, *exchange)
    return outs[0], outs[1], outs[2], outs[3], list(outs[4:])


def _dn_post_fwd(o, proj, gon, name="dn_post_fwd"):
    T = o.shape[0]
    tt = _tile(T, 256, 16)

    def body(o_ref, z_ref, g_ref, y_ref):
        for hh in range(H):
            sl = slice(hh * DH, (hh + 1) * DH)
            ov = o_ref[:, sl]
            zv = z_ref[:, sl].astype(F32)
            r = lax.rsqrt(jnp.mean(ov * ov, axis=-1, keepdims=True) + EPS)
            y_ref[:, sl] = (ov * r * g_ref[...] * (zv * _sig(zv))).astype(y_ref.dtype)

    return pl.pallas_call(
        body, name=name, grid=(T // tt,),
        in_specs=[pl.BlockSpec((tt, WD), lambda i: (i, 0)), pl.BlockSpec((tt, WD), lambda i: (i, OZ // WD)),
                  pl.BlockSpec((1, DH), lambda i: (0, 0))],
        out_specs=pl.BlockSpec((tt, WD), lambda i: (i, 0)),
        out_shape=jax.ShapeDtypeStruct((T, WD), BF16), compiler_params=_cp("parallel"),
    )(o, proj, gon)


def _dn_post_bwd(o, proj, gon, dy, name="dn_post_bwd"):
    T = o.shape[0]
    tt = _tile(T, 256, 16)

    def body(o_ref, z_ref, g_ref, dy_ref, do_ref, dz_ref, dg_ref):
        @pl.when(pl.program_id(0) == 0)
        def _():
            dg_ref[...] = jnp.zeros_like(dg_ref)

        acc = jnp.zeros((1, DH), F32)
        for hh in range(H):
            sl = slice(hh * DH, (hh + 1) * DH)
            ov = o_ref[:, sl]
            zv = z_ref[:, sl].astype(F32)
            dyv = dy_ref[:, sl]
            r = lax.rsqrt(jnp.mean(ov * ov, axis=-1, keepdims=True) + EPS)
            oh = ov * r
            nrm = oh * g_ref[...]
            gate, gate_grad = _silu_and_grad(zv)
            dn = dyv * gate
            dz_ref[:, sl] = (dyv * nrm * gate_grad).astype(dz_ref.dtype)
            doh = dn * g_ref[...]
            do_ref[:, sl] = r * (doh - oh * jnp.mean(doh * oh, axis=-1, keepdims=True))
            acc = acc + jnp.sum(dn * oh, axis=0, keepdims=True)
        dg_ref[...] += acc

    row = pl.BlockSpec((tt, WD), lambda i: (i, 0))
    vec = pl.BlockSpec((1, DH), lambda i: (0, 0))
    return pl.pallas_call(
        body, name=name, grid=(T // tt,),
        in_specs=[row, pl.BlockSpec((tt, WD), lambda i: (i, OZ // WD)), vec, row],
        out_specs=[row, row, vec],
        out_shape=[jax.ShapeDtypeStruct((T, WD), F32), jax.ShapeDtypeStruct((T, WD), BF16),
                   jax.ShapeDtypeStruct((1, DH), F32)],
        compiler_params=_cp("arbitrary"),
    )(o, proj, gon, dy)


def _sg_common(u_ref, v_ref, lng_ref, lnb_ref, with_grad=True):
    ur = u_ref[...].astype(F32)
    vr = v_ref[...].astype(F32)
    vgel, vgel_grad = _gelu_and_grad(vr) if with_grad else (_gelu(vr), None)
    mu = jnp.mean(vgel, axis=-1, keepdims=True)
    xc = vgel - mu
    rs = lax.rsqrt(jnp.mean(xc * xc, axis=-1, keepdims=True) + EPS)
    xh = xc * rs
    vg = xh * lng_ref[...] + lnb_ref[...]
    return ur, vgel_grad, rs, xh, vg


def _sg_fwd(proj, lng, lnb, sgw, sgbt, name="sg_fwd"):
    T = proj.shape[0]

    def body(u_ref, v_ref, lng_ref, lnb_ref, w_ref, bt_ref, y_ref):
        ur, _, _, _, vg = _sg_common(u_ref, v_ref, lng_ref, lnb_ref, with_grad=False)
        ri = lax.broadcasted_iota(jnp.int32, (SGC, SGC), 0)
        ci = lax.broadcasted_iota(jnp.int32, (SGC, SGC), 1)
        ug = _gelu(ur)
        for g in range(H):
            sl = slice(g * DH, (g + 1) * DH)
            ws = jnp.where(ri >= ci, w_ref[g], 0.0)
            mixed = _bdot(ws, vg[:, sl]) + bt_ref[:, g:g + 1]
            y_ref[:, sl] = (ug[:, sl] * mixed).astype(y_ref.dtype)

    vec = pl.BlockSpec((1, WD), lambda i: (0, 0))
    return pl.pallas_call(
        body, name=name, grid=(T // SGC,),
        in_specs=[pl.BlockSpec((SGC, WD), lambda i: (i, OU // WD)), pl.BlockSpec((SGC, WD), lambda i: (i, OV // WD)),
                  vec, vec, pl.BlockSpec((H, SGC, SGC), lambda i: (0, 0, 0)),
                  pl.BlockSpec((SGC, H), lambda i: (0, 0))],
        out_specs=pl.BlockSpec((SGC, WD), lambda i: (i, 0)),
        out_shape=jax.ShapeDtypeStruct((T, WD), BF16), compiler_params=_cp("parallel"),
    )(proj, proj, lng, lnb, sgw, sgbt)


def _sg_bwd(proj, lng, lnb, sgw, sgbt, dy, name="sg_bwd"):
    T = proj.shape[0]

    def body(u_ref, v_ref, lng_ref, lnb_ref, w_ref, bt_ref, dy_ref,
             du_ref, dv_ref, dw_ref, dbt_ref, dlng_ref, dlnb_ref):
        @pl.when(pl.program_id(0) == 0)
        def _():
            dw_ref[...] = jnp.zeros_like(dw_ref)
            dbt_ref[...] = jnp.zeros_like(dbt_ref)
            dlng_ref[...] = jnp.zeros_like(dlng_ref)
            dlnb_ref[...] = jnp.zeros_like(dlnb_ref)

        ur, vgel_grad, rs, xh, vg = _sg_common(u_ref, v_ref, lng_ref, lnb_ref)
        ri = lax.broadcasted_iota(jnp.int32, (SGC, SGC), 0)
        ci = lax.broadcasted_iota(jnp.int32, (SGC, SGC), 1)
        ug, ug_grad = _gelu_and_grad(ur)
        dyv = dy_ref[...]
        dbt = jnp.zeros((SGC, 128), F32)
        dvg_parts = []
        for g in range(H):
            sl = slice(g * DH, (g + 1) * DH)
            ws = jnp.where(ri >= ci, w_ref[g], 0.0)
            mixed = _bdot(ws, vg[:, sl]) + bt_ref[:, g:g + 1]
            dyg = dyv[:, sl]
            du_ref[:, sl] = (dyg * mixed * ug_grad[:, sl]).astype(du_ref.dtype)
            dmix = dyg * ug[:, sl]
            dw_ref[g] += jnp.where(ri >= ci, _bdot(dmix, vg[:, sl], NT), 0.0)
            dbt = dbt + jnp.where(ci == g, _lsum(dmix), 0.0)
            dvg_parts.append(_bdot(ws, dmix, TN))
        dbt_ref[...] += dbt
        dvg = jnp.concatenate(dvg_parts, axis=1)
        dlng_ref[...] += jnp.sum(dvg * xh, axis=0, keepdims=True)
        dlnb_ref[...] += jnp.sum(dvg, axis=0, keepdims=True)
        dxh = dvg * lng_ref[...]
        dvgel = rs * (dxh - jnp.mean(dxh, axis=-1, keepdims=True) - xh * jnp.mean(dxh * xh, axis=-1, keepdims=True))
        dv_ref[...] = (dvgel * vgel_grad).astype(dv_ref.dtype)

    vec = pl.BlockSpec((1, WD), lambda i: (0, 0))
    row = pl.BlockSpec((SGC, WD), lambda i: (i, 0))
    return pl.pallas_call(
        body, name=name, grid=(T // SGC,),
        in_specs=[pl.BlockSpec((SGC, WD), lambda i: (i, OU // WD)), pl.BlockSpec((SGC, WD), lambda i: (i, OV // WD)),
                  vec, vec, pl.BlockSpec((H, SGC, SGC), lambda i: (0, 0, 0)),
                  pl.BlockSpec((SGC, H), lambda i: (0, 0)), row],
        out_specs=[row, row, pl.BlockSpec((H, SGC, SGC), lambda i: (0, 0, 0)),
                   pl.BlockSpec((SGC, 128), lambda i: (0, 0)), vec, vec],
        out_shape=[jax.ShapeDtypeStruct((T, WD), BF16), jax.ShapeDtypeStruct((T, WD), BF16),
                   jax.ShapeDtypeStruct((H, SGC, SGC), F32), jax.ShapeDtypeStruct((SGC, 128), F32),
                   jax.ShapeDtypeStruct((1, WD), F32), jax.ShapeDtypeStruct((1, WD), F32)],
        compiler_params=_cp("arbitrary"),
    )(proj, proj, lng, lnb, sgw, sgbt, dy)


def _merge_fwd(proj, yap, ybp, D, name="merge_fwd"):
    T = proj.shape[0]
    tt = _tile(T, 256, 16)

    def body(ga_ref, gb_ref, a_ref, b_ref, o_ref):
        ga, gb, a, b = [r[...].astype(F32) for r in (ga_ref, gb_ref, a_ref, b_ref)]
        o_ref[...] = (_sig(ga) * a + _sig(gb) * b).astype(o_ref.dtype)

    row = pl.BlockSpec((tt, D), lambda i: (i, 0))
    return pl.pallas_call(
        body, name=name, grid=(T // tt,),
        in_specs=[pl.BlockSpec((tt, D), lambda i: (i, OGA // D)), pl.BlockSpec((tt, D), lambda i: (i, OGA // D + 1)),
                  row, row],
        out_specs=row, out_shape=jax.ShapeDtypeStruct((T, D), BF16), compiler_params=_cp("parallel"),
    )(proj, proj, yap, ybp)


def _merge_bwd(proj, yap, ybp, dm, D, name="merge_bwd"):
    T = proj.shape[0]
    tt = _tile(T, 256, 16)

    def body(ga_ref, gb_ref, a_ref, b_ref, dm_ref, da_ref, db_ref, dga_ref, dgb_ref):
        d, ga, gb, a, b = [r[...].astype(F32) for r in (dm_ref, ga_ref, gb_ref, a_ref, b_ref)]
        sa = _sig(ga)
        sb = _sig(gb)
        da_ref[...] = (d * sa).astype(da_ref.dtype)
        db_ref[...] = (d * sb).astype(db_ref.dtype)
        dga_ref[...] = (d * a * sa * (1.0 - sa)).astype(dga_ref.dtype)
        dgb_ref[...] = (d * b * sb * (1.0 - sb)).astype(dgb_ref.dtype)

    row = pl.BlockSpec((tt, D), lambda i: (i, 0))
    return pl.pallas_call(
        body, name=name, grid=(T // tt,),
        in_specs=[pl.BlockSpec((tt, D), lambda i: (i, OGA // D)), pl.BlockSpec((tt, D), lambda i: (i, OGA // D + 1)),
                  row, row, row],
        out_specs=[row] * 4, out_shape=[jax.ShapeDtypeStruct((T, D), BF16)] * 4,
        compiler_params=_cp("parallel"),
    )(proj, proj, yap, ybp, dm)


def _ffn_act_fwd(gp, up, cw, cb, name="ffn_act_fwd"):
    T, F = gp.shape

    def body(g_ref, u_ref, w_ref, b_ref, o_ref):
        gv = g_ref[...].astype(F32)
        w = w_ref[...]
        c = gv * w[FF_K - 1:FF_K, :] + b_ref[...]
        for k in range(1, FF_K):
            c = c + _shift_down(gv, k) * w[FF_K - 1 - k:FF_K - k, :]
        o_ref[...] = (c * _sig(c) * u_ref[...].astype(F32)).astype(o_ref.dtype)

    col = pl.BlockSpec((T, 128), lambda j: (0, j))
    return pl.pallas_call(
        body, name=name, grid=(F // 128,),
        in_specs=[col, col, pl.BlockSpec((FF_K, 128), lambda j: (0, j)), pl.BlockSpec((1, 128), lambda j: (0, j))],
        out_specs=col, out_shape=jax.ShapeDtypeStruct((T, F), BF16), compiler_params=_cp("parallel"),
    )(gp, up, cw, cb)


def _ffn_act_bwd(gp, up, cw, cb, dact, name="ffn_act_bwd"):
    T, F = gp.shape

    def body(g_ref, u_ref, w_ref, b_ref, d_ref, dg_ref, du_ref, dw_ref, db_ref):
        gv = g_ref[...].astype(F32)
        w = w_ref[...]
        shifted = [_shift_down(gv, k) for k in range(FF_K)]
        c = shifted[0] * w[FF_K - 1:FF_K, :] + b_ref[...]
        for k in range(1, FF_K):
            c = c + shifted[k] * w[FF_K - 1 - k:FF_K - k, :]
        d = d_ref[...].astype(F32)
        act, act_grad = _silu_and_grad(c)
        du_ref[...] = (d * act).astype(du_ref.dtype)
        dc = d * u_ref[...].astype(F32) * act_grad
        dg = dc * w[FF_K - 1:FF_K, :]
        for k in range(1, FF_K):
            dg = dg + _shift_up(dc, k) * w[FF_K - 1 - k:FF_K - k, :]
        dg_ref[...] = dg.astype(dg_ref.dtype)
        rows = [jnp.sum(dc * shifted[FF_K - 1 - t], axis=0, keepdims=True) for t in range(FF_K)]
        dw_ref[...] = jnp.concatenate(rows, axis=0)
        db_ref[...] = jnp.sum(dc, axis=0, keepdims=True)

    col = pl.BlockSpec((T, 128), lambda j: (0, j))
    wspec = pl.BlockSpec((FF_K, 128), lambda j: (0, j))
    bspec = pl.BlockSpec((1, 128), lambda j: (0, j))
    return pl.pallas_call(
        body, name=name, grid=(F // 128,),
        in_specs=[col, col, wspec, bspec, col], out_specs=[col, col, wspec, bspec],
        out_shape=[jax.ShapeDtypeStruct((T, F), BF16), jax.ShapeDtypeStruct((T, F), BF16),
                   jax.ShapeDtypeStruct((FF_K, F), F32), jax.ShapeDtypeStruct((1, F), F32)],
        compiler_params=_cp("parallel"),
    )(gp, up, cw, cb, dact)


class _Carrier:
    def __init__(self, plan=None, deliver=None):
        self.plan, self.deliver = plan or (lambda kernel: None), deliver

    def run(self, kernel, fn, **kw):
        comm = self.plan(kernel)
        out = fn(comm=comm, **kw)
        if comm:
            self.deliver(kernel, out[-1])
            out = out[:-1]
            return out[0] if len(out) == 1 else out
        return out


def _layer_fwd(x, w, carrier=None):
    cr = carrier or _Carrier()
    D = x.shape[1]
    oba = OGA + 2 * D
    h = _rms_fwd(x, w["norm1_g"], "rms1_fwd")
    proj = cr.run("proj", functools.partial(_mm, h, w["w_in_t"], "nt", BF16, name="mm_proj"))
    bg = _ba_fwd(proj, w["alog_row"], w["dtb_row"], oba)
    qkv = _dn_prep_fwd(proj, w["dn_conv_w"])
    r = cr.run("dn_core", functools.partial(_dn_core_fwd, qkv, bg))
    o, s_all, tm_all = r[0], r[1], r[2]
    ya = _dn_post_fwd(o, proj, w["dn_onorm_g"])
    yb = _sg_fwd(proj, w["sg_ln_g"], w["sg_ln_b"], w["sg_w"], w["sg_bt"])
    yap = _mm(ya, w["w_branch_a"], "nn", BF16, name="mm_branch")
    ybp = _mm(yb, w["w_branch_b"], "nn", BF16, name="mm_branch")
    merged = _merge_fwd(proj, yap, ybp, D)
    x1 = _mm(merged, w["w_out"], "nn", F32, add=x, name="mm_out")
    h2 = _rms_fwd(x1, w["norm2_g"], "rms2_fwd")
    gp = cr.run("ffn_gate", functools.partial(_mm, h2, w["ffn_w_gate"], "nn", BF16, name="mm_ffn_in"))
    up = cr.run("ffn_up", functools.partial(_mm, h2, w["ffn_w_up"], "nn", BF16, name="mm_ffn_in"))
    act = _ffn_act_fwd(gp, up, w["ffn_conv_w"], w["ffn_conv_b"])
    x2 = cr.run("ffn_down", functools.partial(_mm, act, w["ffn_w_down"], "nn", F32, add=x1, name="mm_ffn_down"))
    saved = dict(x=x, h=h, proj=proj, bg=bg, qkv=qkv, o=o, s_all=s_all, tm_all=tm_all, ya=ya, yb=yb, yap=yap,
                 ybp=ybp, merged=merged, x1=x1, h2=h2, gp=gp, up=up, act=act)
    return x2, saved


def _layer_bwd(dx2, w, s, carrier=None, ffn_grads_ready=None, rest_grads_ready=None):
    cr = carrier or _Carrier()
    D = dx2.shape[1]
    oba = OGA + 2 * D
    g = {}
    dx2b = dx2.astype(BF16)
    dact = cr.run("d_act", functools.partial(_mm, dx2b, w["ffn_w_down"], "nt", BF16, name="mm_d_act"))
    g["ffn_w_down"] = _mm(s["act"], dx2b, "tn", BF16, name="mm_dw_down")
    dgp, dup, g["ffn_conv_w"], g["ffn_conv_b"] = _ffn_act_bwd(s["gp"], s["up"], w["ffn_conv_w"], w["ffn_conv_b"], dact)
    dh2 = _mm(dgp, w["ffn_w_gate"], "nt", F32, name="mm_dh2")
    dh2 = _mm(dup, w["ffn_w_up"], "nt", F32, add=dh2, name="mm_dh2_acc")
    g["ffn_w_gate"] = _mm(s["h2"], dgp, "tn", BF16, name="mm_dw_ffn_in")
    g["ffn_w_up"] = _mm(s["h2"], dup, "tn", BF16, name="mm_dw_ffn_in")
    if ffn_grads_ready:
        ffn_grads_ready(g)
    dx1, g["norm2_g"] = _rms_bwd(s["x1"], w["norm2_g"], dh2, dx2, "rms2_bwd")
    dx1b = dx1.astype(BF16)
    dm = cr.run("d_merged", functools.partial(_mm, dx1b, w["w_out"], "nt", BF16, name="mm_d_merged"))
    g["w_out"] = _mm(s["merged"], dx1b, "tn", BF16, name="mm_dw_out")
    dyap, dybp, dga, dgb = _merge_bwd(s["proj"], s["yap"], s["ybp"], dm, D)
    dya = _mm(dyap, w["w_branch_a"], "nt", F32, name="mm_d_branch")
    dyb = _mm(dybp, w["w_branch_b"], "nt", F32, name="mm_d_branch")
    g["w_branch_a"] = _mm(s["ya"], dyap, "tn", BF16, name="mm_dw_branch")
    g["w_branch_b"] = _mm(s["yb"], dybp, "tn", BF16, name="mm_dw_branch")
    du, dv, g["sg_w"], dbt, g["sg_ln_g"], g["sg_ln_b"] = _sg_bwd(
        s["proj"], w["sg_ln_g"], w["sg_ln_b"], w["sg_w"], w["sg_bt"], dyb)
    g["sg_b"] = jnp.transpose(dbt[:, :H])
    do, dz, g["dn_onorm_g"] = _dn_post_bwd(s["o"], s["proj"], w["dn_onorm_g"], dya)
    r = cr.run("dn_core", functools.partial(_dn_core_bwd, s["qkv"], s["bg"], s["s_all"], s["tm_all"], do))
    dq, dk, dvv, dbg = r[0], r[1], r[2], r[3]
    dqkv, g["dn_conv_w"] = _dn_prep_bwd(s["proj"], w["dn_conv_w"], dq, dk, dvv)
    dba, dal, ddt = _ba_bwd(s["proj"], w["alog_row"], w["dtb_row"], dbg, oba)
    g["dn_a_log"] = dal[0, H:2 * H]
    g["dn_dt_bias"] = ddt[0, H:2 * H]
    dproj = jnp.concatenate([dqkv, dz, du, dv, dga, dgb, dba], axis=1)
    g["w_in_t"] = cr.run("dw_in", functools.partial(_mm, dproj, s["h"], "tn", BF16, name="mm_dw_in"))
    if rest_grads_ready:
        rest_grads_ready(g)
    dh = cr.run("dh", functools.partial(_mm, dproj, w["w_in_t"], "nn", F32, name="mm_dh"))
    dx, g["norm1_g"] = _rms_bwd(s["x"], w["norm1_g"], dh, dx1, "rms1_bwd")
    return dx, g


def _row128(v, off):
    return jnp.pad(v, (off, 128 - off - v.shape[0]))[None]


def _prep_small(p):
    return dict(
        norm1_g=p["norm1_g"][None], alog_row=_row128(p["dn_a_log"], H), dtb_row=_row128(p["dn_dt_bias"], H),
        dn_conv_w=p["dn_conv_w"], dn_onorm_g=p["dn_onorm_g"][None],
        sg_ln_g=p["sg_ln_g"][None], sg_ln_b=p["sg_ln_b"][None], sg_w=p["sg_w"], sg_bt=jnp.transpose(p["sg_b"]),
        norm2_g=p["norm2_g"][None], ffn_conv_w=p["ffn_conv_w"], ffn_conv_b=p["ffn_conv_b"][None])


HBM_SPEC = pl.BlockSpec(memory_space=pltpu.HBM)


def _coords():
    return lax.axis_index("x"), lax.axis_index("y"), lax.axis_index("c")


def _other_chips(x, y):
    return [(1 - x, y), (x, 1 - y), (1 - x, 1 - y)]


def _remote(src, dst, send_sems, recv_sems, k, dev):
    return pltpu.make_async_remote_copy(src_ref=src, dst_ref=dst, send_sem=send_sems.at[k], recv_sem=recv_sems.at[k],
                                        device_id=dev, device_id_type=MESH)


def _ag_copies(w_refs, o_refs, send_sems, recv_sems):
    x, y, c = _coords()
    me = 2 * x + y
    chips = _other_chips(x, y)

    def ici(k, j, owner):
        chip = chips[j]
        return _remote(w_refs[k].at[c], o_refs[k].at[owner, c], send_sems, recv_sems, 6 * k + j, (chip[0], chip[1], c))

    def d2d(k, j, part):
        owner = 2 * chips[j][0] + chips[j][1]
        return _remote(o_refs[k].at[owner, part], o_refs[k].at[owner, part], send_sems, recv_sems, 6 * k + 3 + j,
                       (x, y, 1 - c))

    n = len(w_refs)
    return me, c, chips, ici, d2d, [(k, j) for k in range(n) for j in range(3)]


def _ag_start(w_refs, o_refs, send_sems, recv_sems):
    me, _, _, ici, _, pairs = _ag_copies(w_refs, o_refs, send_sems, recv_sems)
    for k, j in pairs:
        ici(k, j, me).start()


def _ag_finish(w_refs, o_refs, send_sems, recv_sems):
    me, c, chips, ici, d2d, pairs = _ag_copies(w_refs, o_refs, send_sems, recv_sems)
    for k, j in pairs:
        ici(k, j, 2 * chips[j][0] + chips[j][1]).wait_recv()
        d2d(k, j, c).start()
    for k, j in pairs:
        d2d(k, j, 1 - c).wait_recv()
    for k, j in pairs:
        ici(k, j, me).wait_send()
        d2d(k, j, c).wait_send()


def _ag_specs(ws):
    n = len(ws)
    return dict(out_shape=[jax.ShapeDtypeStruct((N_CHIPS,) + w.shape, w.dtype) for w in ws],
                specs=[HBM_SPEC] * n, sems=[pltpu.SemaphoreType.DMA((6 * n,)), pltpu.SemaphoreType.DMA((6 * n,))])


def _ag_layers(ws):
    n = len(ws)

    def body(*refs):
        _ag_start(refs[:n], refs[n:2 * n], *refs[2 * n:])
        _ag_finish(refs[:n], refs[n:2 * n], *refs[2 * n:])

    sp = _ag_specs(ws)
    return pl.pallas_call(
        body, name="ag_weights", out_shape=sp["out_shape"], in_specs=sp["specs"], out_specs=sp["specs"],
        scratch_shapes=sp["sems"],
    )(*ws)


def _pair_copies(g_refs, b_refs, send_sems, recv_sems):
    x, y, c = _coords()
    return [_remote(g_refs[k].at[i, 1 - c], b_refs[k].at[i], send_sems, recv_sems, N_CHIPS * k + i, (x, y, 1 - c))
            for k in range(len(g_refs)) for i in range(N_CHIPS)]


def _pair_start(g_refs, b_refs, send_sems, recv_sems):
    for cp in _pair_copies(g_refs, b_refs, send_sems, recv_sems):
        cp.start()


def _pair_finish(g_refs, b_refs, send_sems, recv_sems):
    for cp in _pair_copies(g_refs, b_refs, send_sems, recv_sems):
        cp.wait()


def _pair_specs(Gs):
    n = len(Gs)
    return dict(out_shape=[jax.ShapeDtypeStruct((N_CHIPS,) + g.shape[2:], g.dtype) for g in Gs],
                specs=[HBM_SPEC] * n,
                sems=[pltpu.SemaphoreType.DMA((N_CHIPS * n,)), pltpu.SemaphoreType.DMA((N_CHIPS * n,))])


def _rs_pair_exchange(Gs):
    n = len(Gs)

    def body(*refs):
        _pair_start(refs[:n], refs[n:2 * n], *refs[2 * n:])
        _pair_finish(refs[:n], refs[n:2 * n], *refs[2 * n:])

    sp = _pair_specs(Gs)
    return pl.pallas_call(
        body, name="rs_pair_exchange", out_shape=sp["out_shape"], in_specs=sp["specs"], out_specs=sp["specs"],
        scratch_shapes=sp["sems"],
    )(*Gs)


def _rs_add_pair(G, B, c, name):
    _, _, R, C = G.shape
    tr = _tile(R, 256, 16)

    def body(c_ref, g_ref, b_ref, o_ref):
        o_ref[0] = (g_ref[0, 0].astype(F32) + b_ref[0].astype(F32)).astype(o_ref.dtype)

    grid_spec = pltpu.PrefetchScalarGridSpec(
        num_scalar_prefetch=1, grid=(N_CHIPS, R // tr),
        in_specs=[pl.BlockSpec((1, 1, tr, C), lambda i, r, c_ref: (i, c_ref[0], r, 0)),
                  pl.BlockSpec((1, tr, C), lambda i, r, c_ref: (i, r, 0))],
        out_specs=pl.BlockSpec((1, tr, C), lambda i, r, c_ref: (i, r, 0)))
    return pl.pallas_call(
        body, name=name, grid_spec=grid_spec, out_shape=jax.ShapeDtypeStruct((N_CHIPS, R, C), G.dtype),
        compiler_params=_cp("parallel", "parallel"),
    )(jnp.reshape(c, (1,)).astype(jnp.int32), G, B)


def _rsx_copies(p_refs, b_refs, send_sems, recv_sems):
    x, y, c = _coords()
    me = 2 * x + y
    chips = _other_chips(x, y)

    def cp(k, j, src_slot, dst_slot):
        return _remote(p_refs[k].at[src_slot], b_refs[k].at[dst_slot], send_sems, recv_sems, 3 * k + j,
                       (chips[j][0], chips[j][1], c))

    return me, chips, cp, [(k, j) for k in range(len(p_refs)) for j in range(3)]


def _rsx_start(p_refs, b_refs, send_sems, recv_sems):
    me, chips, cp, pairs = _rsx_copies(p_refs, b_refs, send_sems, recv_sems)
    for k, j in pairs:
        cp(k, j, 2 * chips[j][0] + chips[j][1], me).start()


def _rsx_finish(p_refs, b_refs, send_sems, recv_sems):
    me, chips, cp, pairs = _rsx_copies(p_refs, b_refs, send_sems, recv_sems)
    for k, j in pairs:
        owner = 2 * chips[j][0] + chips[j][1]
        cp(k, j, owner, owner).wait_recv()
    for k, j in pairs:
        cp(k, j, 2 * chips[j][0] + chips[j][1], me).wait_send()


def _rsx_specs(Ps):
    n = len(Ps)
    return dict(out_shape=[jax.ShapeDtypeStruct(p.shape, p.dtype) for p in Ps], specs=[HBM_SPEC] * n,
                sems=[pltpu.SemaphoreType.DMA((3 * n,)), pltpu.SemaphoreType.DMA((3 * n,))])


def _rs_sum_chips(P, B, me, name):
    _, R, C = P.shape
    tr = _tile(R, 256, 16)

    def body(me_ref, p_ref, b1_ref, b2_ref, b3_ref, o_ref):
        o_ref[...] = ((p_ref[0].astype(F32) + b1_ref[0].astype(F32)) + b2_ref[0].astype(F32)) + b3_ref[0].astype(F32)

    slot = lambda d: pl.BlockSpec((1, tr, C), lambda r, me_ref: ((me_ref[0] + d) % N_CHIPS, r, 0))
    grid_spec = pltpu.PrefetchScalarGridSpec(
        num_scalar_prefetch=1, grid=(R // tr,), in_specs=[slot(0), slot(1), slot(2), slot(3)],
        out_specs=pl.BlockSpec((tr, C), lambda r, me_ref: (r, 0)))
    return pl.pallas_call(
        body, name=name, grid_spec=grid_spec, out_shape=jax.ShapeDtypeStruct((R, C), F32),
        compiler_params=_cp("parallel"),
    )(jnp.reshape(me, (1,)).astype(jnp.int32), P, B, B, B)


def _sum_slots(B, name):
    S, R, C = B.shape
    tr = _tile(R, 256, 16)

    def body(b_ref, o_ref):
        acc = b_ref[0].astype(F32)
        for i in range(1, S):
            acc = acc + b_ref[i].astype(F32)
        o_ref[...] = acc

    return pl.pallas_call(
        body, name=name, grid=(R // tr,), in_specs=[pl.BlockSpec((S, tr, C), lambda r: (0, r, 0))],
        out_specs=pl.BlockSpec((tr, C), lambda r: (r, 0)), out_shape=jax.ShapeDtypeStruct((R, C), F32),
        compiler_params=_cp("parallel"),
    )(B)


def _rs_pair_swap(Rs):
    n = len(Rs)

    def body(*refs):
        r_refs, o_refs = refs[:n], refs[n:2 * n]
        send_sems, recv_sems = refs[2 * n:]
        x, y, c = _coords()
        cps = [_remote(r_refs[k], o_refs[k], send_sems, recv_sems, k, (x, y, 1 - c)) for k in range(n)]
        for cp in cps:
            cp.start()
        for cp in cps:
            cp.wait()

    return pl.pallas_call(
        body, name="rs_pair_swap", out_shape=[jax.ShapeDtypeStruct(r.shape, r.dtype) for r in Rs],
        in_specs=[HBM_SPEC] * n, out_specs=[HBM_SPEC] * n,
        scratch_shapes=[pltpu.SemaphoreType.DMA((n,)), pltpu.SemaphoreType.DMA((n,))],
    )(*Rs)


def _ag8(v):
    R = v.shape[0]

    def body(v_ref, out_ref, send_sems, recv_sems, local_sem):
        x, y, c = _coords()
        me, sib = (x, y, c), (x, y, 1 - c)
        chips = _other_chips(x, y)

        def slot(p):
            return out_ref.at[4 * p[0] + 2 * p[1] + p[2]]

        def copy(k, block, to, src=None):
            return _remote(slot(block) if src is None else src, slot(block), send_sems, recv_sems, k, to)

        mine = pltpu.make_async_copy(v_ref, slot(me), local_sem)
        mine.start()
        first = [copy(0, me, sib, src=v_ref)]
        first += [copy(1 + j, me, (chip[0], chip[1], c), src=v_ref) for j, chip in enumerate(chips)]
        for cp in first:
            cp.start()
        passed = [copy(4 + j, (chip[0], chip[1], c), sib) for j, chip in enumerate(chips)]
        for j, chip in enumerate(chips):
            copy(1 + j, (chip[0], chip[1], c), me).wait_recv()
            passed[j].start()
        copy(0, sib, me).wait_recv()
        for j, chip in enumerate(chips):
            copy(4 + j, (chip[0], chip[1], 1 - c), me).wait_recv()
        for cp in first + passed:
            cp.wait_send()
        mine.wait()

    return pl.pallas_call(
        body, name="ag8_small", out_shape=jax.ShapeDtypeStruct((8, R, 128), v.dtype),
        in_specs=[pl.BlockSpec(memory_space=pltpu.VMEM)], out_specs=pl.BlockSpec(memory_space=pltpu.VMEM),
        scratch_shapes=[pltpu.SemaphoreType.DMA((7,)), pltpu.SemaphoreType.DMA((7,)), pltpu.SemaphoreType.DMA],
        compiler_params=pltpu.CompilerParams(vmem_limit_bytes=VMEM_LIMIT),
    )(v)


def _adamw(w, g, m, v, name):
    L, R, C = w.shape
    rows = [R] + [t for t in range(8, min(R, 1024) + 1, 8) if R % t == 0]
    cols = [C] + [t for t in range(128, C, 128) if C % t == 0]
    lead = [t for t in range(1, L + 1) if L % t == 0]
    fits = [(a * r * c, c, r, a) for a in lead for r in rows for c in cols if a * r * c * 4 <= 3 << 19]
    _, tc, tr, tl = max(fits) if fits else (0, min(cols), min(rows), 1)

    def body(w_ref, g_ref, m_ref, v_ref, d_ref, mo_ref, vo_ref):
        gv = g_ref[...]
        m2 = ADAM_B1 * m_ref[...] + (1.0 - ADAM_B1) * gv
        v2 = ADAM_B2 * v_ref[...] + (1.0 - ADAM_B2) * jnp.square(gv)
        m_hat = m2 / (1.0 - ADAM_B1 ** ADAM_STEP)
        v_hat = v2 / (1.0 - ADAM_B2 ** ADAM_STEP)
        d_ref[...] = -ADAM_LR * (m_hat / (jnp.sqrt(v_hat) + ADAM_EPS) + ADAM_WD * w_ref[...])
        mo_ref[...] = m2
        vo_ref[...] = v2

    blk = pl.BlockSpec((tl, tr, tc), lambda l, r, j: (l, r, j))
    return pl.pallas_call(
        body, name=name, grid=(L // tl, R // tr, C // tc), in_specs=[blk] * 4, out_specs=[blk] * 3,
        out_shape=[jax.ShapeDtypeStruct(w.shape, F32)] * 3,
        compiler_params=_cp("parallel", "parallel", "parallel"),
    )(w, g, m, v)


def _adamw_halves(w, g_mine, g_other, c, m, v, name):
    L, _, R, C = w.shape
    tr = _tile(R, 128, 8)

    def body(c_ref, w_ref, *rest):
        g_refs = rest[:2 * L]
        m_ref, v_ref, g_ref, d_ref, mo_ref, vo_ref = rest[2 * L:]
        l, h = pl.program_id(0), pl.program_id(1)
        gm, go = g_refs[0][...], g_refs[L][...]
        for i in range(1, L):
            gm = jnp.where(l == i, g_refs[i][...], gm)
            go = jnp.where(l == i, g_refs[L + i][...], go)
        gv = jnp.where(h == c_ref[0], gm, go)[None, None]
        g_ref[...] = gv
        m2 = ADAM_B1 * m_ref[...] + (1.0 - ADAM_B1) * gv
        v2 = ADAM_B2 * v_ref[...] + (1.0 - ADAM_B2) * jnp.square(gv)
        m_hat = m2 / (1.0 - ADAM_B1 ** ADAM_STEP)
        v_hat = v2 / (1.0 - ADAM_B2 ** ADAM_STEP)
        d_ref[...] = -ADAM_LR * (m_hat / (jnp.sqrt(v_hat) + ADAM_EPS) + ADAM_WD * w_ref[...])
        mo_ref[...] = m2
        vo_ref[...] = v2

    blk = pl.BlockSpec((1, 1, tr, C), lambda l, h, r, c_ref: (l, h, r, 0))

    def gblk(i, mine):
        def index(l, h, r, c_ref):
            use = jnp.logical_and(l == i, (h == c_ref[0]) == mine)
            return (jnp.where(use, r, 0), 0)
        return pl.BlockSpec((tr, C), index)

    grid_spec = pltpu.PrefetchScalarGridSpec(
        num_scalar_prefetch=1, grid=(L, 2, R // tr),
        in_specs=[blk] + [gblk(i, True) for i in range(L)] + [gblk(i, False) for i in range(L)] + [blk, blk],
        out_specs=[blk] * 4)
    return pl.pallas_call(
        body, name=name, grid_spec=grid_spec, out_shape=[jax.ShapeDtypeStruct(w.shape, F32)] * 4,
        compiler_params=_cp("parallel", "parallel", "parallel"),
    )(jnp.reshape(c, (1,)).astype(jnp.int32), w, *g_mine, *g_other, m, v)


BIG = ("w_in", "w_branch_a", "w_branch_b", "w_out", "ffn_w_gate", "ffn_w_up", "ffn_w_down")
ROW_SHARDED = ("w_out", "ffn_w_down")
SMALL = ("norm1_g", "dn_conv_w", "dn_a_log", "dn_dt_bias", "dn_onorm_g", "sg_ln_g", "sg_ln_b", "sg_w", "sg_b",
         "norm2_g", "ffn_conv_w", "ffn_conv_b", "final_norm_g")
SMALL_SHARDED = ("dn_conv_w", "ffn_conv_w")


def _pack_rows(arrs, mult):
    flat = jnp.concatenate([jnp.reshape(a, (-1,)) for a in arrs])
    n = flat.shape[0]
    rows = -(-n // (128 * mult)) * mult
    return jnp.reshape(jnp.pad(flat, (0, rows * 128 - n)), (rows, 128))


def _unpack(flat2d, shapes):
    flat = jnp.reshape(flat2d, (-1,))
    out, off = [], 0
    for shp in shapes:
        n = math.prod(shp)
        out.append(jnp.reshape(flat[off:off + n], shp))
        off += n
    return out


def kernel(x, norm1_g, w_in, dn_conv_w, dn_a_log, dn_dt_bias, dn_onorm_g, sg_ln_g, sg_ln_b, sg_w, sg_b, w_branch_a, w_branch_b, w_out, norm2_g, ffn_w_gate, ffn_w_up, ffn_conv_w, ffn_conv_b, ffn_w_down, final_norm_g, loss_target, m_norm1_g, m_w_in, m_dn_conv_w, m_dn_a_log, m_dn_dt_bias, m_dn_onorm_g, m_sg_ln_g, m_sg_ln_b, m_sg_w, m_sg_b, m_w_branch_a, m_w_branch_b, m_w_out, m_norm2_g, m_ffn_w_gate, m_ffn_w_up, m_ffn_conv_w, m_ffn_conv_b, m_ffn_w_down, m_final_norm_g, v_norm1_g, v_w_in, v_dn_conv_w, v_dn_a_log, v_dn_dt_bias, v_dn_onorm_g, v_sg_ln_g, v_sg_ln_b, v_sg_w, v_sg_b, v_w_branch_a, v_w_branch_b, v_w_out, v_norm2_g, v_ffn_w_gate, v_ffn_w_up, v_ffn_conv_w, v_ffn_conv_b, v_ffn_w_down, v_final_norm_g):
    W = dict(norm1_g=norm1_g, w_in=w_in, dn_conv_w=dn_conv_w, dn_a_log=dn_a_log, dn_dt_bias=dn_dt_bias,
             dn_onorm_g=dn_onorm_g, sg_ln_g=sg_ln_g, sg_ln_b=sg_ln_b, sg_w=sg_w, sg_b=sg_b, w_branch_a=w_branch_a,
             w_branch_b=w_branch_b, w_out=w_out, norm2_g=norm2_g, ffn_w_gate=ffn_w_gate, ffn_w_up=ffn_w_up,
             ffn_conv_w=ffn_conv_w, ffn_conv_b=ffn_conv_b, ffn_w_down=ffn_w_down, final_norm_g=final_norm_g)
    M = dict(norm1_g=m_norm1_g, w_in=m_w_in, dn_conv_w=m_dn_conv_w, dn_a_log=m_dn_a_log, dn_dt_bias=m_dn_dt_bias,
             dn_onorm_g=m_dn_onorm_g, sg_ln_g=m_sg_ln_g, sg_ln_b=m_sg_ln_b, sg_w=m_sg_w, sg_b=m_sg_b,
             w_branch_a=m_w_branch_a, w_branch_b=m_w_branch_b, w_out=m_w_out, norm2_g=m_norm2_g,
             ffn_w_gate=m_ffn_w_gate, ffn_w_up=m_ffn_w_up, ffn_conv_w=m_ffn_conv_w, ffn_conv_b=m_ffn_conv_b,
             ffn_w_down=m_ffn_w_down, final_norm_g=m_final_norm_g)
    V = dict(norm1_g=v_norm1_g, w_in=v_w_in, dn_conv_w=v_dn_conv_w, dn_a_log=v_dn_a_log, dn_dt_bias=v_dn_dt_bias,
             dn_onorm_g=v_dn_onorm_g, sg_ln_g=v_sg_ln_g, sg_ln_b=v_sg_ln_b, sg_w=v_sg_w, sg_b=v_sg_b,
             w_branch_a=v_w_branch_a, w_branch_b=v_w_branch_b, w_out=v_w_out, norm2_g=v_norm2_g,
             ffn_w_gate=v_ffn_w_gate, ffn_w_up=v_ffn_w_up, ffn_conv_w=v_ffn_conv_w, ffn_conv_b=v_ffn_conv_b,
             ffn_w_down=v_ffn_w_down, final_norm_g=v_final_norm_g)
    cx, cy, cc = _coords()
    chip = 2 * cx + cy
    L = w_in.shape[0]

    D = w_in.shape[1]
    cs_in = w_in.shape[2]
    c1 = 4 * WD
    ba_chip, ba_off = c1 // cs_in, c1 % cs_in
    assert ba_off + 2 * H <= cs_in
    n_main = N_CHIPS * cs_in - 2 * H
    main_start = [i * cs_in - (2 * H if i > ba_chip else 0) for i in range(N_CHIPS)]
    main_len = [cs_in - (2 * H if i == ba_chip else 0) for i in range(N_CHIPS)]
    tile0 = [16 * (s // 16) for s in main_start]
    shift = [s - t for s, t in zip(main_start, tile0)]
    rp_in = -(-max(sh + ln for sh, ln in zip(shift, main_len)) // 32) * 32
    seg = [tile0[i + 1] - tile0[i] for i in range(N_CHIPS - 1)] + [n_main - tile0[-1]]
    assert all(s + 16 <= rp_in for s in seg[:-1]) and seg[-1] <= rp_in and tile0[-1] + rp_in <= n_main + 128
    my_shift = jnp.asarray(shift, jnp.int32)[chip]

    mine = {n: W[n].astype(BF16) for n in BIG if n != "w_in"}
    wt = jnp.swapaxes(W["w_in"], 1, 2).astype(BF16)
    ba = wt[:, ba_off:ba_off + 2 * H]
    local_row = lax.broadcasted_iota(jnp.int32, (cs_in, 1), 0)
    without_ba = jnp.where(local_row < ba_off, wt, jnp.pad(wt[:, 2 * H:], ((0, 0), (0, 2 * H), (0, 0))))
    mine["w_in"] = lax.dynamic_update_slice(jnp.zeros((L, rp_in, D), BF16),
                                            jnp.where(chip == ba_chip, without_ba, wt), (0, my_shift, 0))
    mine["w_ba"] = jnp.pad(jnp.where(chip == ba_chip, ba, jnp.zeros_like(ba)), ((0, 0), (0, 32 - 2 * H), (0, 0)))

    def halves(a, lead=0):
        return jnp.reshape(a, a.shape[:lead] + (2, a.shape[lead] // 2) + a.shape[lead + 1:])

    first = [(0, "w_in"), (0, "w_ba")]
    my_taps = _pack_rows([W[n] for n in SMALL_SHARDED], 32)
    first_gathered = _ag_layers([halves(mine[n][l]) for l, n in first] + [halves(my_taps)])
    all_taps = jnp.reshape(first_gathered[-1], (N_CHIPS,) + my_taps.shape)
    tap_shards = [_unpack(jnp.where(chip == i, my_taps, all_taps[i]), [W[n].shape for n in SMALL_SHARDED])
                  for i in range(N_CHIPS)]
    taps_full = {n: jnp.concatenate([tap_shards[i][k] for i in range(N_CHIPS)], axis=-1)
                 for k, n in enumerate(SMALL_SHARDED)}

    ops = []
    for l in range(L):
        p = {n: W[n][l] for n in W if n not in ("final_norm_g",) + BIG + SMALL_SHARDED}
        p.update({n: taps_full[n][l] for n in SMALL_SHARDED})
        ops.append(_prep_small(p))

    def weights_landed(items, gathered):
        got = {}
        for (l, n), a in zip(items, gathered):
            a = jnp.reshape(a, (N_CHIPS,) + mine[n].shape[1:])
            got[(l, n)] = [jnp.where(chip == i, mine[n][l], a[i]) for i in range(N_CHIPS)]
        for (l, n), parts in got.items():
            if n == "w_in":
                pieces = [parts[0][:seg[0]]]
                for i in range(1, N_CHIPS):
                    pieces += [parts[i][:16] + parts[i - 1][seg[i - 1]:seg[i - 1] + 16], parts[i][16:seg[i]]]
                ba_rows = got[(l, "w_ba")][ba_chip][:2 * H]
                ops[l]["w_in_t"] = jnp.concatenate(pieces + [ba_rows, jnp.zeros((128 - 2 * H, D), BF16)], axis=0)
            elif n != "w_ba":
                ops[l][n] = jnp.concatenate(parts, axis=0 if n in ROW_SHARDED else 1)

    partial_sums, chip_sums = {}, {}

    grad_slices = {}

    def pair_sums(items, shares):
        for (l, n), b in zip(items, shares):
            partial_sums[(l, n)] = _rs_add_pair(grad_slices[(l, n)], b, cc, "rs_add_pair_" + n)

    def grad_partials(l, names, g, exchange_now=True):
        Gs = []
        for n in names:
            if n == "w_in":
                a = jnp.stack([g["w_in_t"][t:t + rp_in] for t in tile0])
            elif n == "w_ba":
                a = jnp.broadcast_to(g["w_in_t"][n_main:n_main + 32][None], (N_CHIPS, 32, D))
            elif n in ROW_SHARDED:
                a = jnp.reshape(g[n], (N_CHIPS, g[n].shape[0] // N_CHIPS, g[n].shape[1]))
            else:
                a = jnp.moveaxis(jnp.reshape(g[n], (g[n].shape[0], N_CHIPS, g[n].shape[1] // N_CHIPS)), 1, 0)
            grad_slices[(l, n)] = halves(a, 1)
        if exchange_now:
            items = [(l, n) for n in names]
            pair_sums(items, _rs_pair_exchange([grad_slices[i] for i in items]))

    def carrier(l, plan, kind, landed):
        source = {"gather": lambda i: halves(mine[i[1]][i[0]]), "pair": grad_slices.get, "exchange": partial_sums.get}

        def payload(kernel):
            items = plan.get((l, kernel))
            return (kind(kernel), [source[kind(kernel)](i) for i in items]) if items else None

        return _Carrier(payload, lambda kernel, res: landed(kernel)(plan[(l, kernel)], res))

    FFN = ("ffn_w_gate", "ffn_w_up", "ffn_w_down")
    REST = ("w_in", "w_ba", "w_branch_a", "w_branch_b", "w_out")
    fwd_plan = {(0, "proj"): [(0, "w_branch_a"), (0, "w_branch_b"), (0, "w_out"), (0, "ffn_w_gate")],
                (0, "dn_core"): [(0, "ffn_w_up"), (0, "ffn_w_down"), (1, "w_in"), (1, "w_ba")],
                (0, "ffn_gate"): [(1, "w_branch_a"), (1, "w_branch_b"), (1, "w_out")],
                (0, "ffn_up"): [(1, "ffn_w_gate")],
                (0, "ffn_down"): [(1, "ffn_w_up")],
                (1, "proj"): [(1, "ffn_w_down")]}
    bwd_plan = {(1, "d_merged"): [(1, n) for n in FFN],
                (0, "d_merged"): [(0, n) for n in FFN],
                (1, "dn_core"): [(1, n) for n in FFN],
                (0, "d_act"): [(1, "w_branch_a"), (1, "w_branch_b"), (1, "w_out")],
                (0, "dn_core"): [(1, "w_in"), (1, "w_ba"), (0, "ffn_w_down")],
                (0, "dw_in"): [(0, "ffn_w_gate"), (0, "ffn_w_up")],
                (0, "dh"): [(0, n) for n in REST]}

    def sums_landed(items, res):
        chip_sums.update(zip(items, res))

    weights_landed(first, first_gathered[:-1])
    xs, saved = x[0], []
    for l in range(L):
        xs, s = _layer_fwd(xs, ops[l], carrier(l, fwd_plan, lambda kernel: "gather", lambda kernel: weights_landed))
        saved.append(s)
    dx, dgf, loss = _loss_head(xs, final_norm_g[None], loss_target[0])
    loss = loss[0, 0]
    grads = [None] * L
    for l in reversed(range(L)):
        dx, grads[l] = _layer_bwd(
            dx, ops[l], saved[l],
            carrier(l, bwd_plan, lambda kernel: "pair" if kernel == "d_merged" else "exchange",
                    lambda kernel: pair_sums if kernel == "d_merged" else sums_landed),
            functools.partial(grad_partials, l, FFN, exchange_now=False), functools.partial(grad_partials, l, REST))
    travelled = BIG + ("w_ba",)
    g_mine = [[_rs_sum_chips(partial_sums[(l, n)], chip_sums[(l, n)], chip, "rs_sum_chips_" + n) for n in travelled]
              for l in range(L)]
    swapped = _rs_pair_swap(g_mine[0] + g_mine[1])
    g_other = [swapped[:len(travelled)], swapped[len(travelled):]]

    def both_halves(l, n):
        a, b = g_mine[l][travelled.index(n)], g_other[l][travelled.index(n)]
        return jnp.where(cc == 0, jnp.concatenate([a, b]), jnp.concatenate([b, a]))

    def w_in_grad_rows(l):
        m = lax.dynamic_slice_in_dim(both_halves(l, "w_in"), my_shift, cs_in, axis=0)
        ba_rows = jnp.pad(both_halves(l, "w_ba")[:2 * H], ((ba_off, cs_in - ba_off - 2 * H), (0, 0)))
        moved = jnp.pad(m[:cs_in - 2 * H], ((2 * H, 0), (0, 0)))
        with_ba = jnp.where(local_row < ba_off, m, jnp.where(local_row < ba_off + 2 * H, ba_rows, moved))
        return jnp.where(chip == ba_chip, with_ba, m)

    small = {n: jnp.stack([g[n] for g in grads]) for n in SMALL if n != "final_norm_g"}
    small["final_norm_g"] = dgf
    shapes = [taps_full[n].shape if n in SMALL_SHARDED else W[n].shape for n in SMALL] + [(1,)]
    sflat = _pack_rows([small[n] for n in SMALL] + [jnp.reshape(loss, (1,))], 16)
    sred = _unpack(_sum_slots(_ag8(sflat), "sum_small"), shapes)
    g_small = dict(zip(SMALL, sred[:-1]))
    loss_total = sred[-1][0]
    for n in SMALL_SHARDED:
        cs = W[n].shape[-1]
        g_small[n] = lax.dynamic_slice_in_dim(g_small[n], chip * cs, cs, axis=-1)

    g_big, delta, new_m, new_v = {}, {}, {}, {}
    for k, n in enumerate(BIG):
        gm, go = [g_mine[l][k] for l in range(L)], [g_other[l][k] for l in range(L)]
        if n == "w_in":
            g_t = jnp.stack([w_in_grad_rows(l) for l in range(L)], axis=1)
            outs = _adamw(*[jnp.transpose(a, (2, 0, 1)) for a in (W[n],)], g_t,
                          *[jnp.transpose(a, (2, 0, 1)) for a in (M[n], V[n])], "adamw_" + n)
            g_big[n], delta[n], new_m[n], new_v[n] = [jnp.transpose(o, (1, 2, 0)) for o in (g_t,) + tuple(outs)]
        else:
            outs = _adamw_halves(halves(W[n], 1), gm, go, cc, halves(M[n], 1), halves(V[n], 1), "adamw_" + n)
            g_big[n], delta[n], new_m[n], new_v[n] = [jnp.reshape(o, W[n].shape) for o in outs]
    for n in SMALL:
        shp = W[n].shape
        as3d = (1,) * (3 - len(shp)) + shp if len(shp) <= 3 else (-1,) + shp[-2:]
        outs = _adamw(*[jnp.reshape(d[n], as3d) for d in (W, g_small, M, V)], "adamw_" + n)
        delta[n], new_m[n], new_v[n] = [jnp.reshape(o, shp) for o in outs]

    names = list(W)
    grad_w = {**g_big, **g_small}
    return (loss_total, dx[None], *[grad_w[n] for n in names], *[delta[n] for n in names],
            *[new_m[n] for n in names], *[new_v[n] for n in names])
```

```python
import functools
import math

import jax
import jax.numpy as jnp
from jax import lax
from jax.experimental import pallas as pl
from jax.experimental.pallas import tpu as pltpu

F32 = jnp.float32
BF16 = jnp.bfloat16
MESH = pl.DeviceIdType.MESH

EPS = 1e-6
H = 8
DH = 128
WD = H * DH
DNC = 64
SGC = 128
DN_K = 4
FF_K = 3
DEPTH = 2
N_CHIPS = 4

ADAM_LR = 0.001
ADAM_B1 = 0.9
ADAM_B2 = 0.999
ADAM_EPS = 1e-08
ADAM_WD = 0.01
ADAM_STEP = 10

VMEM_LIMIT = 56 * 1024 * 1024

NN = (((1,), (0,)), ((), ()))
NT = (((1,), (1,)), ((), ()))
TN = (((0,), (0,)), ((), ()))

OQ, OZ, OU, OV, OGA = 0, 3 * WD, 4 * WD, 5 * WD, 6 * WD


def _cp(*sem):
    return pltpu.CompilerParams(dimension_semantics=sem or None, vmem_limit_bytes=VMEM_LIMIT)


def _tile(dim, pref, unit=128):
    if dim <= pref:
        return dim
    t = (pref // unit) * unit
    while t >= unit:
        if dim % t == 0:
            return t
        t -= unit
    return dim


def _bdot(a, b, dn=NN):
    return lax.dot_general(a.astype(BF16), b.astype(BF16), dn, preferred_element_type=F32)


def _lsum(x):
    return jnp.sum(x, axis=1, keepdims=True)


def _sig(x):
    return 0.5 * jnp.tanh(0.5 * x) + 0.5


def _silu_and_grad(x):
    s = _sig(x)
    return x * s, s * (1.0 + x * (1.0 - s))


def _gelu_parts(x):
    a = jnp.abs(x) * (2.0 ** -0.5)
    t = 1.0 / (1.0 + 0.3275911 * a)
    poly = t * (0.254829592 + t * (-0.284496736 + t * (1.421413741 + t * (-1.453152027 + t * 1.061405429))))
    e = jnp.exp(-a * a)
    half = 0.5 * poly * e
    return jnp.where(x < 0, half, 1.0 - half), e * (1.0 / math.sqrt(2.0 * math.pi))


def _gelu(x):
    return x * _gelu_parts(x)[0]


def _gelu_and_grad(x):
    cdf, pdf = _gelu_parts(x)
    return x * cdf, cdf + x * pdf


def _shift_down(x, k):
    if k == 0:
        return x
    y = pltpu.roll(x, k, 0)
    rows = lax.broadcasted_iota(jnp.int32, (8, x.shape[1]), 0)
    return jnp.concatenate([jnp.where(rows >= k, y[:8], 0.0), y[8:]], axis=0)


def _shift_up(x, k):
    if k == 0:
        return x
    n = x.shape[0]
    y = pltpu.roll(x, n - k, 0)
    rows = lax.broadcasted_iota(jnp.int32, (8, x.shape[1]), 0)
    return jnp.concatenate([y[:n - 8], jnp.where(rows < 8 - k, y[n - 8:], 0.0)], axis=0)


def _comm_fns(comm):
    if not comm:
        return None, None, dict(out_shape=[], specs=[], sems=[]), ()
    kind, arrays = comm
    start, finish, specs = {"gather": (_ag_start, _ag_finish, _ag_specs),
                            "exchange": (_rsx_start, _rsx_finish, _rsx_specs),
                            "pair": (_pair_start, _pair_finish, _pair_specs)}[kind]
    return start, finish, specs(arrays), tuple(arrays)


def _mm(a, b, mode, out_dtype, add=None, name="mm", comm=None):
    if mode == "tn":
        K, M = a.shape
    else:
        M, K = a.shape
    N = b.shape[0] if mode == "nt" else b.shape[1]
    tm, tn, tk = _tile(M, 1152), _tile(N, 1536), _tile(K, 3584)
    nk = K // tk
    ni, nj = M // tm, N // tn
    dn = {"nn": NN, "nt": NT, "tn": TN}[mode]
    c_start, c_finish, c_sp, payload = _comm_fns(comm)
    nc = len(payload)
    n_add = 0 if add is None else 1

    def body(*refs):
        a_ref, b_ref = refs[:2]
        add_ref = refs[2] if n_add else None
        c_in = refs[2 + n_add:2 + n_add + nc]
        o_ref = refs[2 + n_add + nc]
        c_out = refs[3 + n_add + nc:3 + n_add + 2 * nc]
        rest = refs[3 + n_add + 2 * nc:]
        acc_ref = rest[0] if nk > 1 else None
        sems = rest[1:] if nk > 1 else rest
        i, j, k = pl.program_id(0), pl.program_id(1), pl.program_id(2)

        if nc:
            @pl.when(jnp.logical_and(jnp.logical_and(i == 0, j == 0), k == 0))
            def _():
                c_start(c_in, c_out, *sems)

        def finish(r):
            if add is not None:
                r = r + add_ref[...]
            o_ref[...] = r.astype(o_ref.dtype)

        part = lax.dot_general(a_ref[...], b_ref[...], dn, preferred_element_type=F32)
        if nk == 1:
            finish(part)
        else:
            @pl.when(k == 0)
            def _():
                acc_ref[...] = part

            @pl.when(k > 0)
            def _():
                acc_ref[...] += part

            @pl.when(k == nk - 1)
            def _():
                finish(acc_ref[...])

        if nc:
            @pl.when(jnp.logical_and(jnp.logical_and(i == ni - 1, j == nj - 1), k == nk - 1))
            def _():
                c_finish(c_in, c_out, *sems)

    a_spec = (pl.BlockSpec((tk, tm), lambda i, j, k: (k, i)) if mode == "tn"
              else pl.BlockSpec((tm, tk), lambda i, j, k: (i, k)))
    b_spec = (pl.BlockSpec((tn, tk), lambda i, j, k: (j, k)) if mode == "nt"
              else pl.BlockSpec((tk, tn), lambda i, j, k: (k, j)))
    o_spec = pl.BlockSpec((tm, tn), lambda i, j, k: (i, j))
    in_specs = [a_spec, b_spec] + ([o_spec] if add is not None else []) + c_sp["specs"]
    args = (a, b) + ((add,) if add is not None else ()) + payload
    outs = pl.pallas_call(
        body, name=name + ("_" + comm[0] if nc else ""), grid=(ni, nj, nk), in_specs=in_specs,
        out_specs=[o_spec] + c_sp["specs"],
        out_shape=[jax.ShapeDtypeStruct((M, N), out_dtype)] + c_sp["out_shape"],
        scratch_shapes=([pltpu.VMEM((tm, tn), F32)] if nk > 1 else []) + c_sp["sems"],
        compiler_params=_cp("arbitrary", "arbitrary", "arbitrary") if nc else _cp("parallel", "parallel", "arbitrary"),
    )(*args)
    return (outs[0], list(outs[1:])) if nc else outs[0]


def _rms_fwd(x, g, name):
    T, D = x.shape
    tt = _tile(T, 256, 16)

    def body(x_ref, g_ref, o_ref):
        xv = x_ref[...]
        r = lax.rsqrt(jnp.mean(xv * xv, axis=-1, keepdims=True) + EPS)
        o_ref[...] = (xv * r * g_ref[...]).astype(o_ref.dtype)

    return pl.pallas_call(
        body, name=name, grid=(T // tt,),
        in_specs=[pl.BlockSpec((tt, D), lambda i: (i, 0)), pl.BlockSpec((1, D), lambda i: (0, 0))],
        out_specs=pl.BlockSpec((tt, D), lambda i: (i, 0)),
        out_shape=jax.ShapeDtypeStruct((T, D), BF16), compiler_params=_cp("parallel"),
    )(x, g)


def _rms_bwd(x, g, dh, dres, name):
    T, D = x.shape
    tt = _tile(T, 256, 16)

    def body(x_ref, g_ref, dh_ref, dres_ref, dx_ref, dg_ref):
        @pl.when(pl.program_id(0) == 0)
        def _():
            dg_ref[...] = jnp.zeros_like(dg_ref)

        xv = x_ref[...]
        r = lax.rsqrt(jnp.mean(xv * xv, axis=-1, keepdims=True) + EPS)
        xh = xv * r
        dh_v = dh_ref[...]
        dy = dh_v * g_ref[...]
        dx_ref[...] = dres_ref[...] + r * (dy - xh * jnp.mean(dy * xh, axis=-1, keepdims=True))
        dg_ref[...] += jnp.sum(dh_v * xh, axis=0, keepdims=True)

    row = pl.BlockSpec((tt, D), lambda i: (i, 0))
    vec = pl.BlockSpec((1, D), lambda i: (0, 0))
    return pl.pallas_call(
        body, name=name, grid=(T // tt,), in_specs=[row, vec, row, row], out_specs=[row, vec],
        out_shape=[jax.ShapeDtypeStruct((T, D), F32), jax.ShapeDtypeStruct((1, D), F32)],
        compiler_params=_cp("arbitrary"),
    )(x, g, dh, dres)


def _loss_head(x, g, tgt, name="loss_head"):
    T, D = x.shape
    tt = _tile(T, 256, 16)

    def body(x_ref, g_ref, t_ref, dx_ref, dg_ref, loss_ref):
        @pl.when(pl.program_id(0) == 0)
        def _():
            dg_ref[...] = jnp.zeros_like(dg_ref)
            loss_ref[...] = jnp.zeros_like(loss_ref)

        xv = x_ref[...]
        r = lax.rsqrt(jnp.mean(xv * xv, axis=-1, keepdims=True) + EPS)
        xh = xv * r
        err = xh * g_ref[...] - t_ref[...]
        part = 0.5 * jnp.sum(jnp.mean(err * err, axis=-1, keepdims=True), axis=0, keepdims=True)
        loss_ref[...] += jnp.broadcast_to(part, loss_ref.shape)
        dy = err * (1.0 / D)
        dg_ref[...] += jnp.sum(dy * xh, axis=0, keepdims=True)
        dyh = dy * g_ref[...]
        dx_ref[...] = r * (dyh - xh * jnp.mean(dyh * xh, axis=-1, keepdims=True))

    row = pl.BlockSpec((tt, D), lambda i: (i, 0))
    vec = pl.BlockSpec((1, D), lambda i: (0, 0))
    return pl.pallas_call(
        body, name=name, grid=(T // tt,), in_specs=[row, vec, row],
        out_specs=[row, vec, pl.BlockSpec((1, 128), lambda i: (0, 0))],
        out_shape=[jax.ShapeDtypeStruct((T, D), F32), jax.ShapeDtypeStruct((1, D), F32),
                   jax.ShapeDtypeStruct((1, 128), F32)],
        compiler_params=_cp("arbitrary"),
    )(x, g, tgt)


def _ba_fwd(proj, alog, dtb, oba, name="dn_ba_fwd"):
    T = proj.shape[0]
    tt = _tile(T, 512, 8)

    def body(p_ref, al_ref, dt_ref, o_ref):
        raw = p_ref[...].astype(F32)
        lane = lax.broadcasted_iota(jnp.int32, raw.shape, 1)
        z = raw + dt_ref[...]
        sp = jnp.maximum(z, 0.0) + jnp.log(1.0 + jnp.exp(-jnp.abs(z)))
        gl = -jnp.exp(al_ref[...]) * sp
        o_ref[...] = jnp.where(lane < H, _sig(raw), jnp.where(lane < 2 * H, gl, 0.0))

    vec = pl.BlockSpec((1, 128), lambda i: (0, 0))
    return pl.pallas_call(
        body, name=name, grid=(T // tt,),
        in_specs=[pl.BlockSpec((tt, 128), lambda i: (i, oba // 128)), vec, vec],
        out_specs=pl.BlockSpec((tt, 128), lambda i: (i, 0)),
        out_shape=jax.ShapeDtypeStruct((T, 128), F32), compiler_params=_cp("parallel"),
    )(proj, alog, dtb)


def _ba_bwd(proj, alog, dtb, dbg, oba, name="dn_ba_bwd"):
    T = proj.shape[0]
    tt = _tile(T, 512, 16)

    def body(p_ref, al_ref, dt_ref, d_ref, o_ref, dal_ref, ddt_ref):
        @pl.when(pl.program_id(0) == 0)
        def _():
            dal_ref[...] = jnp.zeros_like(dal_ref)
            ddt_ref[...] = jnp.zeros_like(ddt_ref)

        raw = p_ref[...].astype(F32)
        d = d_ref[...]
        lane = lax.broadcasted_iota(jnp.int32, raw.shape, 1)
        z = raw + dt_ref[...]
        sp = jnp.maximum(z, 0.0) + jnp.log(1.0 + jnp.exp(-jnp.abs(z)))
        na = -jnp.exp(al_ref[...])
        is_g = jnp.logical_and(lane >= H, lane < 2 * H)
        b = _sig(raw)
        dz = jnp.where(is_g, d * na * _sig(z), 0.0)
        o_ref[...] = jnp.where(lane < H, d * b * (1.0 - b), dz).astype(o_ref.dtype)
        dal_ref[...] += jnp.sum(jnp.where(is_g, d * na * sp, 0.0), axis=0, keepdims=True)
        ddt_ref[...] += jnp.sum(dz, axis=0, keepdims=True)

    vec = pl.BlockSpec((1, 128), lambda i: (0, 0))
    return pl.pallas_call(
        body, name=name, grid=(T // tt,),
        in_specs=[pl.BlockSpec((tt, 128), lambda i: (i, oba // 128)), vec, vec,
                  pl.BlockSpec((tt, 128), lambda i: (i, 0))],
        out_specs=[pl.BlockSpec((tt, 128), lambda i: (i, 0)), vec, vec],
        out_shape=[jax.ShapeDtypeStruct((T, 128), BF16), jax.ShapeDtypeStruct((1, 128), F32),
                   jax.ShapeDtypeStruct((1, 128), F32)],
        compiler_params=_cp("arbitrary"),
    )(proj, alog, dtb, dbg)


def _dn_prep_fwd(proj, convw, name="dn_prep_fwd"):
    T = proj.shape[0]
    nblk = 3 * H

    def body(p_ref, w_ref, o_ref):
        j = pl.program_id(0)
        xv = p_ref[...].astype(F32)
        w = w_ref[...]
        c = xv * w[DN_K - 1:DN_K, :]
        for k in range(1, DN_K):
            c = c + _shift_down(xv, k) * w[DN_K - 1 - k:DN_K - k, :]
        s = c * _sig(c)
        r = lax.rsqrt(_lsum(s * s) + EPS)
        o_ref[...] = jnp.where(j < 2 * H, s * r, s)

    return pl.pallas_call(
        body, name=name, grid=(nblk,),
        in_specs=[pl.BlockSpec((T, DH), lambda j: (0, j)), pl.BlockSpec((DN_K, DH), lambda j: (0, j))],
        out_specs=pl.BlockSpec((T, DH), lambda j: (0, j)),
        out_shape=jax.ShapeDtypeStruct((T, 3 * WD), F32), compiler_params=_cp("parallel"),
    )(proj, convw)


def _dn_prep_bwd(proj, convw, dq, dk, dv, name="dn_prep_bwd"):
    T = proj.shape[0]
    nblk = 3 * H

    def body(p_ref, w_ref, dq_ref, dk_ref, dv_ref, dx_ref, dw_ref):
        j = pl.program_id(0)
        xv = p_ref[...].astype(F32)
        w = w_ref[...]
        shifted = [_shift_down(xv, k) for k in range(DN_K)]
        c = shifted[0] * w[DN_K - 1:DN_K, :]
        for k in range(1, DN_K):
            c = c + shifted[k] * w[DN_K - 1 - k:DN_K - k, :]
        s, s_grad = _silu_and_grad(c)
        r = lax.rsqrt(_lsum(s * s) + EPS)
        y = s * r
        dy = jnp.where(j < H, dq_ref[...], jnp.where(j < 2 * H, dk_ref[...], dv_ref[...]))
        ds = jnp.where(j < 2 * H, r * (dy - y * _lsum(dy * y)), dy)
        dc = ds * s_grad
        dx = dc * w[DN_K - 1:DN_K, :]
        for k in range(1, DN_K):
            dx = dx + _shift_up(dc, k) * w[DN_K - 1 - k:DN_K - k, :]
        dx_ref[...] = dx.astype(dx_ref.dtype)
        rows = [jnp.sum(dc * shifted[DN_K - 1 - t], axis=0, keepdims=True) for t in range(DN_K)]
        dw_ref[...] = jnp.concatenate(rows, axis=0)

    hb = lambda off: pl.BlockSpec((T, DH), lambda j: (0, jnp.maximum(jnp.minimum(j - off, H - 1), 0)))
    return pl.pallas_call(
        body, name=name, grid=(nblk,),
        in_specs=[pl.BlockSpec((T, DH), lambda j: (0, j)), pl.BlockSpec((DN_K, DH), lambda j: (0, j)),
                  hb(0), hb(H), hb(2 * H)],
        out_specs=[pl.BlockSpec((T, DH), lambda j: (0, j)), pl.BlockSpec((DN_K, DH), lambda j: (0, j))],
        out_shape=[jax.ShapeDtypeStruct((T, 3 * WD), BF16), jax.ShapeDtypeStruct((DN_K, 3 * WD), F32)],
        compiler_params=_cp("parallel"),
    )(proj, convw, dq, dk, dv)


DN_BLOCK = 4


def _split3(a):
    hi = a.astype(BF16)
    r1 = a - hi.astype(F32)
    mid = r1.astype(BF16)
    return hi, mid, (r1 - mid.astype(F32)).astype(BF16)


def _dot3(a, b, dn=NN):
    ah, al, _ = _split3(a)
    bh, bl, _ = _split3(b)
    d = lambda p, q: lax.dot_general(p, q, dn, preferred_element_type=F32)
    return d(ah, bh) + d(ah, bl) + d(al, bh)


def _mask_dot(m, b, dn=NN):
    mb = m.astype(BF16)
    d = lambda q: (lax.dot_general(mb, q, dn, preferred_element_type=F32) if dn != TN
                   else lax.dot_general(q, mb, dn, preferred_element_type=F32))
    b0, b1, b2 = _split3(b)
    return d(b0) + d(b1) + d(b2)


def _tri_inv(A):
    ri = lax.broadcasted_iota(jnp.int32, A.shape, 0)
    ci = lax.broadcasted_iota(jnp.int32, A.shape, 1)
    T = jnp.where(ri == ci, 1.0, 0.0) - jnp.where((ri // 2) == (ci // 2), A, 0.0)
    s = 2
    while s < DNC:
        off = jnp.logical_and((ri // (2 * s)) == (ci // (2 * s)), (ri // s) != (ci // s))
        T = T - _dot3(_dot3(T, jnp.where(off, A, 0.0)), T)
        s *= 2
    return T


GH = 4
NG = H // GH
GR = GH * DNC
GK = GH * DH


def _dn_masks():
    ri = lax.broadcasted_iota(jnp.int32, (GR, GR), 0)
    ci = lax.broadcasted_iota(jnp.int32, (GR, GR), 1)
    blk = (ri // DNC) == (ci // DNC)
    wide = (lax.broadcasted_iota(jnp.int32, (GR, GK), 0) // DNC) == (lax.broadcasted_iota(jnp.int32, (GR, GK), 1) // DH)
    return dict(blk=blk, causal=jnp.logical_and(blk, ri >= ci), strict=jnp.logical_and(blk, ri > ci),
                upper=jnp.logical_and(blk, ri <= ci), eye=ri == ci, wide=wide)


def _wide(a, mk):
    return jnp.where(mk["wide"], jnp.tile(a, (1, GH)), 0.0)


def _fold(a, mk):
    a = jnp.where(mk["wide"], a, 0.0)
    out = a[:, :DH]
    for j in range(1, GH):
        out = out + a[:, j * DH:(j + 1) * DH]
    return out


def _stack_heads(ref, rows, g):
    return jnp.concatenate([ref[rows, (g * GH + j) * DH:(g * GH + j + 1) * DH] for j in range(GH)], axis=0)


def _dn_group(q_ref, k_ref, v_ref, rows, bg, gc_cols, g, mk):
    heads = [g * GH + j for j in range(GH)]
    col = lambda a, lane: jnp.concatenate([a[:, lane(h):lane(h) + 1] for h in heads], axis=0)
    q = _stack_heads(q_ref, rows, g) * (DH ** -0.5)
    k = _stack_heads(k_ref, rows, g)
    v = _stack_heads(v_ref, rows, g)
    beta = col(bg, lambda h: h)
    gcol = col(gc_cols, lambda h: H + h)
    last = [gc_cols[DNC - 1:DNC, H + h:H + h + 1] for h in heads]
    gl = jnp.concatenate([jnp.broadcast_to(t, (DNC, 1)) for t in last], axis=0)
    egl_state = jnp.concatenate([jnp.broadcast_to(jnp.exp(t), (DH, 1)) for t in last], axis=0)
    grow = _mask_dot(jnp.ones((GR, GR), F32), jnp.where(mk["eye"], gcol, 0.0))
    dec = jnp.where(mk["causal"], jnp.exp(jnp.where(mk["causal"], gcol - grow, 0.0)), 0.0)
    eg = jnp.exp(gcol)
    ek = jnp.exp(gl - gcol)
    kb = k * beta
    vb = v * beta
    kbe = kb * eg
    A = jnp.where(mk["strict"], _bdot(kb, k, NT) * dec, 0.0)
    P = jnp.where(mk["causal"], _bdot(q, k, NT) * dec, 0.0)
    return dict(q=q, k=k, v=v, beta=beta, dec=dec, eg=eg, ek=ek, egl=jnp.exp(gl), egl_state=egl_state, kb=kb, vb=vb,
                kbe=kbe, A=A, P=P, qd=q * eg, kd=k * ek, heads=heads)


def _gc_cols(bg):
    ri = lax.broadcasted_iota(jnp.int32, (DNC, DNC), 0)
    ci = lax.broadcasted_iota(jnp.int32, (DNC, DNC), 1)
    return _mask_dot(jnp.where(ri >= ci, 1.0, 0.0), bg)


def _dn_core_fwd(qkv, bg, comm=None, name="dn_core_fwd"):
    c_start, c_finish, sp, gather = _comm_fns(comm)
    T = qkv.shape[0]
    n_chunks = T // DNC
    nb = _tile(n_chunks, DN_BLOCK, 1)
    tb = nb * DNC

    ng = len(gather)
    n_steps = n_chunks // nb

    def body(*refs):
        q_ref, k_ref, v_ref, bg_ref = refs[:4]
        o_ref, s_ref, tm_ref = refs[4 + ng:7 + ng]
        S_scr = refs[7 + 2 * ng]
        comm_refs = (refs[4:4 + ng], refs[7 + ng:7 + 2 * ng]) + tuple(refs[8 + 2 * ng:])

        @pl.when(pl.program_id(0) == 0)
        def _():
            S_scr[...] = jnp.zeros_like(S_scr)
            if ng:
                c_start(*comm_refs)

        mk = _dn_masks()

        def chunk(n):
            rows = pl.ds(n * DNC, DNC)
            bgc = bg_ref[rows, :]
            gc_cols = _gc_cols(bgc)
            for g in range(NG):
                c = _dn_group(q_ref, k_ref, v_ref, rows, bgc, gc_cols, g, mk)
                Tm = _tri_inv(c["A"])
                tm_ref[n, g] = Tm
                S = S_scr[g]
                s_ref[n, g] = S
                u = _bdot(Tm, c["vb"])
                w = _bdot(Tm, c["kbe"])
                vn = u - _bdot(_wide(w, mk), S)
                o = _bdot(_wide(c["qd"], mk), S) + _bdot(c["P"], vn)
                for j, h in enumerate(c["heads"]):
                    o_ref[rows, h * DH:(h + 1) * DH] = o[j * DNC:(j + 1) * DNC]
                S_scr[g] = S * c["egl_state"] + _bdot(_wide(c["kd"], mk), vn, TN)

        for n in range(nb):
            chunk(n)

        if ng:
            @pl.when(pl.program_id(0) == n_steps - 1)
            def _():
                c_finish(*comm_refs)

    blk = lambda j: pl.BlockSpec((tb, WD), lambda i: (i, j))
    outs = pl.pallas_call(
        body, name=name + ("_" + comm[0] if ng else ""), grid=(n_steps,),
        in_specs=[blk(0), blk(1), blk(2), pl.BlockSpec((tb, 128), lambda i: (i, 0))] + sp["specs"],
        out_specs=[blk(0), pl.BlockSpec((nb, NG, GK, DH), lambda i: (i, 0, 0, 0)),
                   pl.BlockSpec((nb, NG, GR, GR), lambda i: (i, 0, 0, 0))] + sp["specs"],
        out_shape=[jax.ShapeDtypeStruct((T, WD), F32), jax.ShapeDtypeStruct((n_chunks, NG, GK, DH), F32),
                   jax.ShapeDtypeStruct((n_chunks, NG, GR, GR), F32)] + sp["out_shape"],
        scratch_shapes=[pltpu.VMEM((NG, GK, DH), F32)] + (sp["sems"] if ng else []),
        compiler_params=_cp("arbitrary"),
    )(qkv, qkv, qkv, bg, *gather)
    return outs[0], outs[1], outs[2], list(outs[3:])


def _dn_core_bwd(qkv, bg, s_all, tm_all, do, comm=None, name="dn_core_bwd"):
    c_start, c_finish, sp, exchange = _comm_fns(comm)
    T = qkv.shape[0]
    n_chunks = T // DNC
    nb = _tile(n_chunks, DN_BLOCK, 1)
    tb = nb * DNC
    n_blocks = n_chunks // nb

    nx = len(exchange)

    def body(*refs):
        q_ref, k_ref, v_ref, bg_ref, s_ref, tm_ref, do_ref = refs[:7]
        dq_ref, dk_ref, dv_ref, dbg_ref = refs[7 + nx:11 + nx]
        dS_scr = refs[11 + 2 * nx]
        comm_refs = (refs[7:7 + nx], refs[11 + nx:11 + 2 * nx]) + tuple(refs[12 + 2 * nx:])

        @pl.when(pl.program_id(0) == 0)
        def _():
            dS_scr[...] = jnp.zeros_like(dS_scr)
            if nx:
                c_start(*comm_refs)

        lane = lax.broadcasted_iota(jnp.int32, (DNC, 128), 1)
        row = lax.broadcasted_iota(jnp.int32, (GR, 1), 0)

        mk = _dn_masks()

        def chunk(n):
            rows = pl.ds(n * DNC, DNC)
            ones = jnp.ones((GR, GR), F32)
            blk_f = jnp.where(mk["blk"], 1.0, 0.0)
            wide_f = jnp.where(mk["wide"], 1.0, 0.0)
            per_row = lambda m, a: _mask_dot(m, jnp.broadcast_to(a, (a.shape[0], DH)))[:, :1]
            bgc = bg_ref[rows, :]
            gc_cols = _gc_cols(bgc)
            dbg = jnp.zeros((DNC, 128), F32)
            for g in range(NG):
                c = _dn_group(q_ref, k_ref, v_ref, rows, bgc, gc_cols, g, mk)
                q, k, v, beta = c["q"], c["k"], c["v"], c["beta"]
                dec, eg, ek, egl = c["dec"], c["eg"], c["ek"], c["egl"]
                kb, vb, kbe, A, P, qd, kd = c["kb"], c["vb"], c["kbe"], c["A"], c["P"], c["qd"], c["kd"]
                S = s_ref[n, g]
                Tm = tm_ref[n, g]
                u = _bdot(Tm, vb)
                w = _bdot(Tm, kbe)
                w_wide = _wide(w, mk)
                vn = u - _bdot(w_wide, S)
                d_o = _stack_heads(do_ref, rows, g)
                dS1 = dS_scr[g]
                d_qd = _fold(_bdot(d_o, S, NT), mk)
                dP = jnp.where(mk["causal"], _bdot(d_o, vn, NT), 0.0)
                d_vn = _bdot(P, d_o, TN) + _bdot(_wide(kd, mk), dS1)
                d_kd = _fold(_bdot(vn, dS1, NT), mk)
                d_egl = per_row(wide_f, _lsum(dS1 * S))
                dS_scr[g] = dS1 * c["egl_state"] + _bdot(_wide(qd, mk), d_o, TN) - _bdot(w_wide, d_vn, TN)
                d_w = -_fold(_bdot(d_vn, S, NT), mk)
                d_vb = _bdot(Tm, d_vn, TN)
                d_kbe = _bdot(Tm, d_w, TN)
                dA = jnp.where(mk["strict"], -(_bdot(d_vb, u, NT) + _bdot(d_kbe, w, NT)), 0.0)
                dMA = dA * dec
                dMP = dP * dec
                d_kb = _bdot(dMA, k) + d_kbe * eg
                d_k = _bdot(dMA, kb, TN) + _bdot(dMP, q, TN) + d_kd * ek + d_kb * beta
                d_qs = (_bdot(dMP, k) + d_qd * eg) * (DH ** -0.5)
                d_v = d_vb * beta
                E = dA * A + dP * P
                col_sums = _mask_dot(ones, E, TN)[:, :1]
                t_kd = _lsum(d_kd * kd)
                d_gl = per_row(blk_f, t_kd) + d_egl * egl
                d_gc = (_lsum(E) - col_sums + _lsum(d_qd * qd) + _lsum(d_kbe * kbe) - t_kd
                        + jnp.where(row % DNC == DNC - 1, d_gl, 0.0))
                d_g = per_row(jnp.where(mk["upper"], 1.0, 0.0), d_gc)
                d_beta = _lsum(d_kb * k) + _lsum(d_vb * v)
                for j, h in enumerate(c["heads"]):
                    rs = slice(j * DNC, (j + 1) * DNC)
                    dq_ref[rows, h * DH:(h + 1) * DH] = d_qs[rs]
                    dk_ref[rows, h * DH:(h + 1) * DH] = d_k[rs]
                    dv_ref[rows, h * DH:(h + 1) * DH] = d_v[rs]
                    dbg = dbg + jnp.where(lane == h, d_beta[rs], 0.0) + jnp.where(lane == h + H, d_g[rs], 0.0)
            dbg_ref[rows, :] = dbg

        for n in reversed(range(nb)):
            chunk(n)

        if nx:
            @pl.when(pl.program_id(0) == n_blocks - 1)
            def _():
                c_finish(*comm_refs)

    blk = lambda j: pl.BlockSpec((tb, WD), lambda i: (n_blocks - 1 - i, j))
    small = pl.BlockSpec((tb, 128), lambda i: (n_blocks - 1 - i, 0))
    outs = pl.pallas_call(
        body, name=name + ("_" + comm[0] if nx else ""), grid=(n_blocks,),
        in_specs=[blk(0), blk(1), blk(2), small,
                  pl.BlockSpec((nb, NG, GK, DH), lambda i: (n_blocks - 1 - i, 0, 0, 0)),
                  pl.BlockSpec((nb, NG, GR, GR), lambda i: (n_blocks - 1 - i, 0, 0, 0)), blk(0)] + sp["specs"],
        out_specs=[blk(0), blk(0), blk(0), small] + sp["specs"],
        out_shape=[jax.ShapeDtypeStruct((T, WD), F32)] * 3 + [jax.ShapeDtypeStruct((T, 128), F32)] + sp["out_shape"],
        scratch_shapes=[pltpu.VMEM((NG, GK, DH), F32)] + (sp["sems"] if nx else []),
        compiler_params=_cp("arbitrary"),
    )(qkv, qkv, qkv, bg, s_all, tm_all, do, *exchange)
    return outs[0], outs[1], outs[2], outs[3], list(outs[4:])


def _dn_post_fwd(o, proj, gon, name="dn_post_fwd"):
    T = o.shape[0]
    tt = _tile(T, 256, 16)

    def body(o_ref, z_ref, g_ref, y_ref):
        for hh in range(H):
            sl = slice(hh * DH, (hh + 1) * DH)
            ov = o_ref[:, sl]
            zv = z_ref[:, sl].astype(F32)
            r = lax.rsqrt(jnp.mean(ov * ov, axis=-1, keepdims=True) + EPS)
            y_ref[:, sl] = (ov * r * g_ref[...] * (zv * _sig(zv))).astype(y_ref.dtype)

    return pl.pallas_call(
        body, name=name, grid=(T // tt,),
        in_specs=[pl.BlockSpec((tt, WD), lambda i: (i, 0)), pl.BlockSpec((tt, WD), lambda i: (i, OZ // WD)),
                  pl.BlockSpec((1, DH), lambda i: (0, 0))],
        out_specs=pl.BlockSpec((tt, WD), lambda i: (i, 0)),
        out_shape=jax.ShapeDtypeStruct((T, WD), BF16), compiler_params=_cp("parallel"),
    )(o, proj, gon)


def _dn_post_bwd(o, proj, gon, dy, name="dn_post_bwd"):
    T = o.shape[0]
    tt = _tile(T, 256, 16)

    def body(o_ref, z_ref, g_ref, dy_ref, do_ref, dz_ref, dg_ref):
        @pl.when(pl.program_id(0) == 0)
        def _():
            dg_ref[...] = jnp.zeros_like(dg_ref)

        acc = jnp.zeros((1, DH), F32)
        for hh in range(H):
            sl = slice(hh * DH, (hh + 1) * DH)
            ov = o_ref[:, sl]
            zv = z_ref[:, sl].astype(F32)
            dyv = dy_ref[:, sl]
            r = lax.rsqrt(jnp.mean(ov * ov, axis=-1, keepdims=True) + EPS)
            oh = ov * r
            nrm = oh * g_ref[...]
            gate, gate_grad = _silu_and_grad(zv)
            dn = dyv * gate
            dz_ref[:, sl] = (dyv * nrm * gate_grad).astype(dz_ref.dtype)
            doh = dn * g_ref[...]
            do_ref[:, sl] = r * (doh - oh * jnp.mean(doh * oh, axis=-1, keepdims=True))
            acc = acc + jnp.sum(dn * oh, axis=0, keepdims=True)
        dg_ref[...] += acc

    row = pl.BlockSpec((tt, WD), lambda i: (i, 0))
    vec = pl.BlockSpec((1, DH), lambda i: (0, 0))
    return pl.pallas_call(
        body, name=name, grid=(T // tt,),
        in_specs=[row, pl.BlockSpec((tt, WD), lambda i: (i, OZ // WD)), vec, row],
        out_specs=[row, row, vec],
        out_shape=[jax.ShapeDtypeStruct((T, WD), F32), jax.ShapeDtypeStruct((T, WD), BF16),
                   jax.ShapeDtypeStruct((1, DH), F32)],
        compiler_params=_cp("arbitrary"),
    )(o, proj, gon, dy)


def _sg_common(u_ref, v_ref, lng_ref, lnb_ref, with_grad=True):
    ur = u_ref[...].astype(F32)
    vr = v_ref[...].astype(F32)
    vgel, vgel_grad = _gelu_and_grad(vr) if with_grad else (_gelu(vr), None)
    mu = jnp.mean(vgel, axis=-1, keepdims=True)
    xc = vgel - mu
    rs = lax.rsqrt(jnp.mean(xc * xc, axis=-1, keepdims=True) + EPS)
    xh = xc * rs
    vg = xh * lng_ref[...] + lnb_ref[...]
    return ur, vgel_grad, rs, xh, vg


def _sg_fwd(proj, lng, lnb, sgw, sgbt, name="sg_fwd"):
    T = proj.shape[0]

    def body(u_ref, v_ref, lng_ref, lnb_ref, w_ref, bt_ref, y_ref):
        ur, _, _, _, vg = _sg_common(u_ref, v_ref, lng_ref, lnb_ref, with_grad=False)
        ri = lax.broadcasted_iota(jnp.int32, (SGC, SGC), 0)
        ci = lax.broadcasted_iota(jnp.int32, (SGC, SGC), 1)
        ug = _gelu(ur)
        for g in range(H):
            sl = slice(g * DH, (g + 1) * DH)
            ws = jnp.where(ri >= ci, w_ref[g], 0.0)
            mixed = _bdot(ws, vg[:, sl]) + bt_ref[:, g:g + 1]
            y_ref[:, sl] = (ug[:, sl] * mixed).astype(y_ref.dtype)

    vec = pl.BlockSpec((1, WD), lambda i: (0, 0))
    return pl.pallas_call(
        body, name=name, grid=(T // SGC,),
        in_specs=[pl.BlockSpec((SGC, WD), lambda i: (i, OU // WD)), pl.BlockSpec((SGC, WD), lambda i: (i, OV // WD)),
                  vec, vec, pl.BlockSpec((H, SGC, SGC), lambda i: (0, 0, 0)),
                  pl.BlockSpec((SGC, H), lambda i: (0, 0))],
        out_specs=pl.BlockSpec((SGC, WD), lambda i: (i, 0)),
        out_shape=jax.ShapeDtypeStruct((T, WD), BF16), compiler_params=_cp("parallel"),
    )(proj, proj, lng, lnb, sgw, sgbt)


def _sg_bwd(proj, lng, lnb, sgw, sgbt, dy, name="sg_bwd"):
    T = proj.shape[0]

    def body(u_ref, v_ref, lng_ref, lnb_ref, w_ref, bt_ref, dy_ref,
             du_ref, dv_ref, dw_ref, dbt_ref, dlng_ref, dlnb_ref):
        @pl.when(pl.program_id(0) == 0)
        def _():
            dw_ref[...] = jnp.zeros_like(dw_ref)
            dbt_ref[...] = jnp.zeros_like(dbt_ref)
            dlng_ref[...] = jnp.zeros_like(dlng_ref)
            dlnb_ref[...] = jnp.zeros_like(dlnb_ref)

        ur, vgel_grad, rs, xh, vg = _sg_common(u_ref, v_ref, lng_ref, lnb_ref)
        ri = lax.broadcasted_iota(jnp.int32, (SGC, SGC), 0)
        ci = lax.broadcasted_iota(jnp.int32, (SGC, SGC), 1)
        ug, ug_grad = _gelu_and_grad(ur)
        dyv = dy_ref[...]
        dbt = jnp.zeros((SGC, 128), F32)
        dvg_parts = []
        for g in range(H):
            sl = slice(g * DH, (g + 1) * DH)
            ws = jnp.where(ri >= ci, w_ref[g], 0.0)
            mixed = _bdot(ws, vg[:, sl]) + bt_ref[:, g:g + 1]
            dyg = dyv[:, sl]
            du_ref[:, sl] = (dyg * mixed * ug_grad[:, sl]).astype(du_ref.dtype)
            dmix = dyg * ug[:, sl]
            dw_ref[g] += jnp.where(ri >= ci, _bdot(dmix, vg[:, sl], NT), 0.0)
            dbt = dbt + jnp.where(ci == g, _lsum(dmix), 0.0)
            dvg_parts.append(_bdot(ws, dmix, TN))
        dbt_ref[...] += dbt
        dvg = jnp.concatenate(dvg_parts, axis=1)
        dlng_ref[...] += jnp.sum(dvg * xh, axis=0, keepdims=True)
        dlnb_ref[...] += jnp.sum(dvg, axis=0, keepdims=True)
        dxh = dvg * lng_ref[...]
        dvgel = rs * (dxh - jnp.mean(dxh, axis=-1, keepdims=True) - xh * jnp.mean(dxh * xh, axis=-1, keepdims=True))
        dv_ref[...] = (dvgel * vgel_grad).astype(dv_ref.dtype)

    vec = pl.BlockSpec((1, WD), lambda i: (0, 0))
    row = pl.BlockSpec((SGC, WD), lambda i: (i, 0))
    return pl.pallas_call(
        body, name=name, grid=(T // SGC,),
        in_specs=[pl.BlockSpec((SGC, WD), lambda i: (i, OU // WD)), pl.BlockSpec((SGC, WD), lambda i: (i, OV // WD)),
                  vec, vec, pl.BlockSpec((H, SGC, SGC), lambda i: (0, 0, 0)),
                  pl.BlockSpec((SGC, H), lambda i: (0, 0)), row],
        out_specs=[row, row, pl.BlockSpec((H, SGC, SGC), lambda i: (0, 0, 0)),
                   pl.BlockSpec((SGC, 128), lambda i: (0, 0)), vec, vec],
        out_shape=[jax.ShapeDtypeStruct((T, WD), BF16), jax.ShapeDtypeStruct((T, WD), BF16),
                   jax.ShapeDtypeStruct((H, SGC, SGC), F32), jax.ShapeDtypeStruct((SGC, 128), F32),
                   jax.ShapeDtypeStruct((1, WD), F32), jax.ShapeDtypeStruct((1, WD), F32)],
        compiler_params=_cp("arbitrary"),
    )(proj, proj, lng, lnb, sgw, sgbt, dy)


def _merge_fwd(proj, yap, ybp, D, name="merge_fwd"):
    T = proj.shape[0]
    tt = _tile(T, 256, 16)

    def body(ga_ref, gb_ref, a_ref, b_ref, o_ref):
        ga, gb, a, b = [r[...].astype(F32) for r in (ga_ref, gb_ref, a_ref, b_ref)]
        o_ref[...] = (_sig(ga) * a + _sig(gb) * b).astype(o_ref.dtype)

    row = pl.BlockSpec((tt, D), lambda i: (i, 0))
    return pl.pallas_call(
        body, name=name, grid=(T // tt,),
        in_specs=[pl.BlockSpec((tt, D), lambda i: (i, OGA // D)), pl.BlockSpec((tt, D), lambda i: (i, OGA // D + 1)),
                  row, row],
        out_specs=row, out_shape=jax.ShapeDtypeStruct((T, D), BF16), compiler_params=_cp("parallel"),
    )(proj, proj, yap, ybp)


def _merge_bwd(proj, yap, ybp, dm, D, name="merge_bwd"):
    T = proj.shape[0]
    tt = _tile(T, 256, 16)

    def body(ga_ref, gb_ref, a_ref, b_ref, dm_ref, da_ref, db_ref, dga_ref, dgb_ref):
        d, ga, gb, a, b = [r[...].astype(F32) for r in (dm_ref, ga_ref, gb_ref, a_ref, b_ref)]
        sa = _sig(ga)
        sb = _sig(gb)
        da_ref[...] = (d * sa).astype(da_ref.dtype)
        db_ref[...] = (d * sb).astype(db_ref.dtype)
        dga_ref[...] = (d * a * sa * (1.0 - sa)).astype(dga_ref.dtype)
        dgb_ref[...] = (d * b * sb * (1.0 - sb)).astype(dgb_ref.dtype)

    row = pl.BlockSpec((tt, D), lambda i: (i, 0))
    return pl.pallas_call(
        body, name=name, grid=(T // tt,),
        in_specs=[pl.BlockSpec((tt, D), lambda i: (i, OGA // D)), pl.BlockSpec((tt, D), lambda i: (i, OGA // D + 1)),
                  row, row, row],
        out_specs=[row] * 4, out_shape=[jax.ShapeDtypeStruct((T, D), BF16)] * 4,
        compiler_params=_cp("parallel"),
    )(proj, proj, yap, ybp, dm)


def _ffn_act_fwd(gp, up, cw, cb, name="ffn_act_fwd"):
    T, F = gp.shape

    def body(g_ref, u_ref, w_ref, b_ref, o_ref):
        gv = g_ref[...].astype(F32)
        w = w_ref[...]
        c = gv * w[FF_K - 1:FF_K, :] + b_ref[...]
        for k in range(1, FF_K):
            c = c + _shift_down(gv, k) * w[FF_K - 1 - k:FF_K - k, :]
        o_ref[...] = (c * _sig(c) * u_ref[...].astype(F32)).astype(o_ref.dtype)

    col = pl.BlockSpec((T, 128), lambda j: (0, j))
    return pl.pallas_call(
        body, name=name, grid=(F // 128,),
        in_specs=[col, col, pl.BlockSpec((FF_K, 128), lambda j: (0, j)), pl.BlockSpec((1, 128), lambda j: (0, j))],
        out_specs=col, out_shape=jax.ShapeDtypeStruct((T, F), BF16), compiler_params=_cp("parallel"),
    )(gp, up, cw, cb)


def _ffn_act_bwd(gp, up, cw, cb, dact, name="ffn_act_bwd"):
    T, F = gp.shape

    def body(g_ref, u_ref, w_ref, b_ref, d_ref, dg_ref, du_ref, dw_ref, db_ref):
        gv = g_ref[...].astype(F32)
        w = w_ref[...]
        shifted = [_shift_down(gv, k) for k in range(FF_K)]
        c = shifted[0] * w[FF_K - 1:FF_K, :] + b_ref[...]
        for k in range(1, FF_K):
            c = c + shifted[k] * w[FF_K - 1 - k:FF_K - k, :]
        d = d_ref[...].astype(F32)
        act, act_grad = _silu_and_grad(c)
        du_ref[...] = (d * act).astype(du_ref.dtype)
        dc = d * u_ref[...].astype(F32) * act_grad
        dg = dc * w[FF_K - 1:FF_K, :]
        for k in range(1, FF_K):
            dg = dg + _shift_up(dc, k) * w[FF_K - 1 - k:FF_K - k, :]
        dg_ref[...] = dg.astype(dg_ref.dtype)
        rows = [jnp.sum(dc * shifted[FF_K - 1 - t], axis=0, keepdims=True) for t in range(FF_K)]
        dw_ref[...] = jnp.concatenate(rows, axis=0)
        db_ref[...] = jnp.sum(dc, axis=0, keepdims=True)

    col = pl.BlockSpec((T, 128), lambda j: (0, j))
    wspec = pl.BlockSpec((FF_K, 128), lambda j: (0, j))
    bspec = pl.BlockSpec((1, 128), lambda j: (0, j))
    return pl.pallas_call(
        body, name=name, grid=(F // 128,),
        in_specs=[col, col, wspec, bspec, col], out_specs=[col, col, wspec, bspec],
        out_shape=[jax.ShapeDtypeStruct((T, F), BF16), jax.ShapeDtypeStruct((T, F), BF16),
                   jax.ShapeDtypeStruct((FF_K, F), F32), jax.ShapeDtypeStruct((1, F), F32)],
        compiler_params=_cp("parallel"),
    )(gp, up, cw, cb, dact)


class _Carrier:
    def __init__(self, plan=None, deliver=None):
        self.plan, self.deliver = plan or (lambda kernel: None), deliver

    def run(self, kernel, fn, **kw):
        comm = self.plan(kernel)
        out = fn(comm=comm, **kw)
        if comm:
            self.deliver(kernel, out[-1])
            out = out[:-1]
            return out[0] if len(out) == 1 else out
        return out


def _layer_fwd(x, w, carrier=None):
    cr = carrier or _Carrier()
    D = x.shape[1]
    oba = OGA + 2 * D
    h = _rms_fwd(x, w["norm1_g"], "rms1_fwd")
    proj = cr.run("proj", functools.partial(_mm, h, w["w_in_t"], "nt", BF16, name="mm_proj"))
    bg = _ba_fwd(proj, w["alog_row"], w["dtb_row"], oba)
    qkv = _dn_prep_fwd(proj, w["dn_conv_w"])
    r = cr.run("dn_core", functools.partial(_dn_core_fwd, qkv, bg))
    o, s_all, tm_all = r[0], r[1], r[2]
    ya = _dn_post_fwd(o, proj, w["dn_onorm_g"])
    yb = _sg_fwd(proj, w["sg_ln_g"], w["sg_ln_b"], w["sg_w"], w["sg_bt"])
    yap = _mm(ya, w["w_branch_a"], "nn", BF16, name="mm_branch")
    ybp = _mm(yb, w["w_branch_b"], "nn", BF16, name="mm_branch")
    merged = _merge_fwd(proj, yap, ybp, D)
    x1 = _mm(merged, w["w_out"], "nn", F32, add=x, name="mm_out")
    h2 = _rms_fwd(x1, w["norm2_g"], "rms2_fwd")
    gp = cr.run("ffn_gate", functools.partial(_mm, h2, w["ffn_w_gate"], "nn", BF16, name="mm_ffn_in"))
    up = cr.run("ffn_up", functools.partial(_mm, h2, w["ffn_w_up"], "nn", BF16, name="mm_ffn_in"))
    act = _ffn_act_fwd(gp, up, w["ffn_conv_w"], w["ffn_conv_b"])
    x2 = cr.run("ffn_down", functools.partial(_mm, act, w["ffn_w_down"], "nn", F32, add=x1, name="mm_ffn_down"))
    saved = dict(x=x, h=h, proj=proj, bg=bg, qkv=qkv, o=o, s_all=s_all, tm_all=tm_all, ya=ya, yb=yb, yap=yap,
                 ybp=ybp, merged=merged, x1=x1, h2=h2, gp=gp, up=up, act=act)
    return x2, saved


def _layer_bwd(dx2, w, s, carrier=None, ffn_grads_ready=None, rest_grads_ready=None):
    cr = carrier or _Carrier()
    D = dx2.shape[1]
    oba = OGA + 2 * D
    g = {}
    dx2b = dx2.astype(BF16)
    dact = cr.run("d_act", functools.partial(_mm, dx2b, w["ffn_w_down"], "nt", BF16, name="mm_d_act"))
    g["ffn_w_down"] = _mm(s["act"], dx2b, "tn", BF16, name="mm_dw_down")
    dgp, dup, g["ffn_conv_w"], g["ffn_conv_b"] = _ffn_act_bwd(s["gp"], s["up"], w["ffn_conv_w"], w["ffn_conv_b"], dact)
    dh2 = _mm(dgp, w["ffn_w_gate"], "nt", F32, name="mm_dh2")
    dh2 = _mm(dup, w["ffn_w_up"], "nt", F32, add=dh2, name="mm_dh2_acc")
    g["ffn_w_gate"] = _mm(s["h2"], dgp, "tn", BF16, name="mm_dw_ffn_in")
    g["ffn_w_up"] = _mm(s["h2"], dup, "tn", BF16, name="mm_dw_ffn_in")
    if ffn_grads_ready:
        ffn_grads_ready(g)
    dx1, g["norm2_g"] = _rms_bwd(s["x1"], w["norm2_g"], dh2, dx2, "rms2_bwd")
    dx1b = dx1.astype(BF16)
    dm = cr.run("d_merged", functools.partial(_mm, dx1b, w["w_out"], "nt", BF16, name="mm_d_merged"))
    g["w_out"] = _mm(s["merged"], dx1b, "tn", BF16, name="mm_dw_out")
    dyap, dybp, dga, dgb = _merge_bwd(s["proj"], s["yap"], s["ybp"], dm, D)
    dya = _mm(dyap, w["w_branch_a"], "nt", F32, name="mm_d_branch")
    dyb = _mm(dybp, w["w_branch_b"], "nt", F32, name="mm_d_branch")
    g["w_branch_a"] = _mm(s["ya"], dyap, "tn", BF16, name="mm_dw_branch")
    g["w_branch_b"] = _mm(s["yb"], dybp, "tn", BF16, name="mm_dw_branch")
    du, dv, g["sg_w"], dbt, g["sg_ln_g"], g["sg_ln_b"] = _sg_bwd(
        s["proj"], w["sg_ln_g"], w["sg_ln_b"], w["sg_w"], w["sg_bt"], dyb)
    g["sg_b"] = jnp.transpose(dbt[:, :H])
    do, dz, g["dn_onorm_g"] = _dn_post_bwd(s["o"], s["proj"], w["dn_onorm_g"], dya)
    r = cr.run("dn_core", functools.partial(_dn_core_bwd, s["qkv"], s["bg"], s["s_all"], s["tm_all"], do))
    dq, dk, dvv, dbg = r[0], r[1], r[2], r[3]
    dqkv, g["dn_conv_w"] = _dn_prep_bwd(s["proj"], w["dn_conv_w"], dq, dk, dvv)
    dba, dal, ddt = _ba_bwd(s["proj"], w["alog_row"], w["dtb_row"], dbg, oba)
    g["dn_a_log"] = dal[0, H:2 * H]
    g["dn_dt_bias"] = ddt[0, H:2 * H]
    dproj = jnp.concatenate([dqkv, dz, du, dv, dga, dgb, dba], axis=1)
    g["w_in_t"] = cr.run("dw_in", functools.partial(_mm, dproj, s["h"], "tn", BF16, name="mm_dw_in"))
    if rest_grads_ready:
        rest_grads_ready(g)
    dh = cr.run("dh", functools.partial(_mm, dproj, w["w_in_t"], "nn", F32, name="mm_dh"))
    dx, g["norm1_g"] = _rms_bwd(s["x"], w["norm1_g"], dh, dx1, "rms1_bwd")
    return dx, g


def _row128(v, off):
    return jnp.pad(v, (off, 128 - off - v.shape[0]))[None]


def _prep_small(p):
    return dict(
        norm1_g=p["norm1_g"][None], alog_row=_row128(p["dn_a_log"], H), dtb_row=_row128(p["dn_dt_bias"], H),
        dn_conv_w=p["dn_conv_w"], dn_onorm_g=p["dn_onorm_g"][None],
        sg_ln_g=p["sg_ln_g"][None], sg_ln_b=p["sg_ln_b"][None], sg_w=p["sg_w"], sg_bt=jnp.transpose(p["sg_b"]),
        norm2_g=p["norm2_g"][None], ffn_conv_w=p["ffn_conv_w"], ffn_conv_b=p["ffn_conv_b"][None])


HBM_SPEC = pl.BlockSpec(memory_space=pltpu.HBM)


def _coords():
    return lax.axis_index("x"), lax.axis_index("y"), lax.axis_index("c")


def _other_chips(x, y):
    return [(1 - x, y), (x, 1 - y), (1 - x, 1 - y)]


def _remote(src, dst, send_sems, recv_sems, k, dev):
    return pltpu.make_async_remote_copy(src_ref=src, dst_ref=dst, send_sem=send_sems.at[k], recv_sem=recv_sems.at[k],
                                        device_id=dev, device_id_type=MESH)


def _ag_copies(w_refs, o_refs, send_sems, recv_sems):
    x, y, c = _coords()
    me = 2 * x + y
    chips = _other_chips(x, y)

    def ici(k, j, owner):
        chip = chips[j]
        return _remote(w_refs[k].at[c], o_refs[k].at[owner, c], send_sems, recv_sems, 6 * k + j, (chip[0], chip[1], c))

    def d2d(k, j, part):
        owner = 2 * chips[j][0] + chips[j][1]
        return _remote(o_refs[k].at[owner, part], o_refs[k].at[owner, part], send_sems, recv_sems, 6 * k + 3 + j,
                       (x, y, 1 - c))

    n = len(w_refs)
    return me, c, chips, ici, d2d, [(k, j) for k in range(n) for j in range(3)]


def _ag_start(w_refs, o_refs, send_sems, recv_sems):
    me, _, _, ici, _, pairs = _ag_copies(w_refs, o_refs, send_sems, recv_sems)
    for k, j in pairs:
        ici(k, j, me).start()


def _ag_finish(w_refs, o_refs, send_sems, recv_sems):
    me, c, chips, ici, d2d, pairs = _ag_copies(w_refs, o_refs, send_sems, recv_sems)
    for k, j in pairs:
        ici(k, j, 2 * chips[j][0] + chips[j][1]).wait_recv()
        d2d(k, j, c).start()
    for k, j in pairs:
        d2d(k, j, 1 - c).wait_recv()
    for k, j in pairs:
        ici(k, j, me).wait_send()
        d2d(k, j, c).wait_send()


def _ag_specs(ws):
    n = len(ws)
    return dict(out_shape=[jax.ShapeDtypeStruct((N_CHIPS,) + w.shape, w.dtype) for w in ws],
                specs=[HBM_SPEC] * n, sems=[pltpu.SemaphoreType.DMA((6 * n,)), pltpu.SemaphoreType.DMA((6 * n,))])


def _ag_layers(ws):
    n = len(ws)

    def body(*refs):
        _ag_start(refs[:n], refs[n:2 * n], *refs[2 * n:])
        _ag_finish(refs[:n], refs[n:2 * n], *refs[2 * n:])

    sp = _ag_specs(ws)
    return pl.pallas_call(
        body, name="ag_weights", out_shape=sp["out_shape"], in_specs=sp["specs"], out_specs=sp["specs"],
        scratch_shapes=sp["sems"],
    )(*ws)


def _pair_copies(g_refs, b_refs, send_sems, recv_sems):
    x, y, c = _coords()
    return [_remote(g_refs[k].at[i, 1 - c], b_refs[k].at[i], send_sems, recv_sems, N_CHIPS * k + i, (x, y, 1 - c))
            for k in range(len(g_refs)) for i in range(N_CHIPS)]


def _pair_start(g_refs, b_refs, send_sems, recv_sems):
    for cp in _pair_copies(g_refs, b_refs, send_sems, recv_sems):
        cp.start()


def _pair_finish(g_refs, b_refs, send_sems, recv_sems):
    for cp in _pair_copies(g_refs, b_refs, send_sems, recv_sems):
        cp.wait()


def _pair_specs(Gs):
    n = len(Gs)
    return dict(out_shape=[jax.ShapeDtypeStruct((N_CHIPS,) + g.shape[2:], g.dtype) for g in Gs],
                specs=[HBM_SPEC] * n,
                sems=[pltpu.SemaphoreType.DMA((N_CHIPS * n,)), pltpu.SemaphoreType.DMA((N_CHIPS * n,))])


def _rs_pair_exchange(Gs):
    n = len(Gs)

    def body(*refs):
        _pair_start(refs[:n], refs[n:2 * n], *refs[2 * n:])
        _pair_finish(refs[:n], refs[n:2 * n], *refs[2 * n:])

    sp = _pair_specs(Gs)
    return pl.pallas_call(
        body, name="rs_pair_exchange", out_shape=sp["out_shape"], in_specs=sp["specs"], out_specs=sp["specs"],
        scratch_shapes=sp["sems"],
    )(*Gs)


def _rs_add_pair(G, B, c, name):
    _, _, R, C = G.shape
    tr = _tile(R, 256, 16)

    def body(c_ref, g_ref, b_ref, o_ref):
        o_ref[0] = (g_ref[0, 0].astype(F32) + b_ref[0].astype(F32)).astype(o_ref.dtype)

    grid_spec = pltpu.PrefetchScalarGridSpec(
        num_scalar_prefetch=1, grid=(N_CHIPS, R // tr),
        in_specs=[pl.BlockSpec((1, 1, tr, C), lambda i, r, c_ref: (i, c_ref[0], r, 0)),
                  pl.BlockSpec((1, tr, C), lambda i, r, c_ref: (i, r, 0))],
        out_specs=pl.BlockSpec((1, tr, C), lambda i, r, c_ref: (i, r, 0)))
    return pl.pallas_call(
        body, name=name, grid_spec=grid_spec, out_shape=jax.ShapeDtypeStruct((N_CHIPS, R, C), G.dtype),
        compiler_params=_cp("parallel", "parallel"),
    )(jnp.reshape(c, (1,)).astype(jnp.int32), G, B)


def _rsx_copies(p_refs, b_refs, send_sems, recv_sems):
    x, y, c = _coords()
    me = 2 * x + y
    chips = _other_chips(x, y)

    def cp(k, j, src_slot, dst_slot):
        return _remote(p_refs[k].at[src_slot], b_refs[k].at[dst_slot], send_sems, recv_sems, 3 * k + j,
                       (chips[j][0], chips[j][1], c))

    return me, chips, cp, [(k, j) for k in range(len(p_refs)) for j in range(3)]


def _rsx_start(p_refs, b_refs, send_sems, recv_sems):
    me, chips, cp, pairs = _rsx_copies(p_refs, b_refs, send_sems, recv_sems)
    for k, j in pairs:
        cp(k, j, 2 * chips[j][0] + chips[j][1], me).start()


def _rsx_finish(p_refs, b_refs, send_sems, recv_sems):
    me, chips, cp, pairs = _rsx_copies(p_refs, b_refs, send_sems, recv_sems)
    for k, j in pairs:
        owner = 2 * chips[j][0] + chips[j][1]
        cp(k, j, owner, owner).wait_recv()
    for k, j in pairs:
        cp(k, j, 2 * chips[j][0] + chips[j][1], me).wait_send()


def _rsx_specs(Ps):
    n = len(Ps)
    return dict(out_shape=[jax.ShapeDtypeStruct(p.shape, p.dtype) for p in Ps], specs=[HBM_SPEC] * n,
                sems=[pltpu.SemaphoreType.DMA((3 * n,)), pltpu.SemaphoreType.DMA((3 * n,))])


def _rs_sum_chips(P, B, me, name):
    _, R, C = P.shape
    tr = _tile(R, 256, 16)

    def body(me_ref, p_ref, b1_ref, b2_ref, b3_ref, o_ref):
        o_ref[...] = ((p_ref[0].astype(F32) + b1_ref[0].astype(F32)) + b2_ref[0].astype(F32)) + b3_ref[0].astype(F32)

    slot = lambda d: pl.BlockSpec((1, tr, C), lambda r, me_ref: ((me_ref[0] + d) % N_CHIPS, r, 0))
    grid_spec = pltpu.PrefetchScalarGridSpec(
        num_scalar_prefetch=1, grid=(R // tr,), in_specs=[slot(0), slot(1), slot(2), slot(3)],
        out_specs=pl.BlockSpec((tr, C), lambda r, me_ref: (r, 0)))
    return pl.pallas_call(
        body, name=name, grid_spec=grid_spec, out_shape=jax.ShapeDtypeStruct((R, C), F32),
        compiler_params=_cp("parallel"),
    )(jnp.reshape(me, (1,)).astype(jnp.int32), P, B, B, B)


def _sum_slots(B, name):
    S, R, C = B.shape
    tr = _tile(R, 256, 16)

    def body(b_ref, o_ref):
        acc = b_ref[0].astype(F32)
        for i in range(1, S):
            acc = acc + b_ref[i].astype(F32)
        o_ref[...] = acc

    return pl.pallas_call(
        body, name=name, grid=(R // tr,), in_specs=[pl.BlockSpec((S, tr, C), lambda r: (0, r, 0))],
        out_specs=pl.BlockSpec((tr, C), lambda r: (r, 0)), out_shape=jax.ShapeDtypeStruct((R, C), F32),
        compiler_params=_cp("parallel"),
    )(B)


def _rs_pair_swap(Rs):
    n = len(Rs)

    def body(*refs):
        r_refs, o_refs = refs[:n], refs[n:2 * n]
        send_sems, recv_sems = refs[2 * n:]
        x, y, c = _coords()
        cps = [_remote(r_refs[k], o_refs[k], send_sems, recv_sems, k, (x, y, 1 - c)) for k in range(n)]
        for cp in cps:
            cp.start()
        for cp in cps:
            cp.wait()

    return pl.pallas_call(
        body, name="rs_pair_swap", out_shape=[jax.ShapeDtypeStruct(r.shape, r.dtype) for r in Rs],
        in_specs=[HBM_SPEC] * n, out_specs=[HBM_SPEC] * n,
        scratch_shapes=[pltpu.SemaphoreType.DMA((n,)), pltpu.SemaphoreType.DMA((n,))],
    )(*Rs)


def _ag8(v):
    R = v.shape[0]

    def body(v_ref, out_ref, send_sems, recv_sems, local_sem):
        x, y, c = _coords()
        me, sib = (x, y, c), (x, y, 1 - c)
        chips = _other_chips(x, y)

        def slot(p):
            return out_ref.at[4 * p[0] + 2 * p[1] + p[2]]

        def copy(k, block, to, src=None):
            return _remote(slot(block) if src is None else src, slot(block), send_sems, recv_sems, k, to)

        mine = pltpu.make_async_copy(v_ref, slot(me), local_sem)
        mine.start()
        first = [copy(0, me, sib, src=v_ref)]
        first += [copy(1 + j, me, (chip[0], chip[1], c), src=v_ref) for j, chip in enumerate(chips)]
        for cp in first:
            cp.start()
        passed = [copy(4 + j, (chip[0], chip[1], c), sib) for j, chip in enumerate(chips)]
        for j, chip in enumerate(chips):
            copy(1 + j, (chip[0], chip[1], c), me).wait_recv()
            passed[j].start()
        copy(0, sib, me).wait_recv()
        for j, chip in enumerate(chips):
            copy(4 + j, (chip[0], chip[1], 1 - c), me).wait_recv()
        for cp in first + passed:
            cp.wait_send()
        mine.wait()

    return pl.pallas_call(
        body, name="ag8_small", out_shape=jax.ShapeDtypeStruct((8, R, 128), v.dtype),
        in_specs=[pl.BlockSpec(memory_space=pltpu.VMEM)], out_specs=pl.BlockSpec(memory_space=pltpu.VMEM),
        scratch_shapes=[pltpu.SemaphoreType.DMA((7,)), pltpu.SemaphoreType.DMA((7,)), pltpu.SemaphoreType.DMA],
        compiler_params=pltpu.CompilerParams(vmem_limit_bytes=VMEM_LIMIT),
    )(v)


def _adamw(w, g, m, v, name):
    L, R, C = w.shape
    rows = [R] + [t for t in range(8, min(R, 1024) + 1, 8) if R % t == 0]
    cols = [C] + [t for t in range(128, C, 128) if C % t == 0]
    lead = [t for t in range(1, L + 1) if L % t == 0]
    fits = [(a * r * c, c, r, a) for a in lead for r in rows for c in cols if a * r * c * 4 <= 3 << 19]
    _, tc, tr, tl = max(fits) if fits else (0, min(cols), min(rows), 1)

    def body(w_ref, g_ref, m_ref, v_ref, d_ref, mo_ref, vo_ref):
        gv = g_ref[...]
        m2 = ADAM_B1 * m_ref[...] + (1.0 - ADAM_B1) * gv
        v2 = ADAM_B2 * v_ref[...] + (1.0 - ADAM_B2) * jnp.square(gv)
        m_hat = m2 / (1.0 - ADAM_B1 ** ADAM_STEP)
        v_hat = v2 / (1.0 - ADAM_B2 ** ADAM_STEP)
        d_ref[...] = -ADAM_LR * (m_hat / (jnp.sqrt(v_hat) + ADAM_EPS) + ADAM_WD * w_ref[...])
        mo_ref[...] = m2
        vo_ref[...] = v2

    blk = pl.BlockSpec((tl, tr, tc), lambda l, r, j: (l, r, j))
    return pl.pallas_call(
        body, name=name, grid=(L // tl, R // tr, C // tc), in_specs=[blk] * 4, out_specs=[blk] * 3,
        out_shape=[jax.ShapeDtypeStruct(w.shape, F32)] * 3,
        compiler_params=_cp("parallel", "parallel", "parallel"),
    )(w, g, m, v)


def _adamw_halves(w, g_mine, g_other, c, m, v, name):
    L, _, R, C = w.shape
    tr = _tile(R, 128, 8)

    def body(c_ref, w_ref, *rest):
        g_refs = rest[:2 * L]
        m_ref, v_ref, g_ref, d_ref, mo_ref, vo_ref = rest[2 * L:]
        l, h = pl.program_id(0), pl.program_id(1)
        gm, go = g_refs[0][...], g_refs[L][...]
        for i in range(1, L):
            gm = jnp.where(l == i, g_refs[i][...], gm)
            go = jnp.where(l == i, g_refs[L + i][...], go)
        gv = jnp.where(h == c_ref[0], gm, go)[None, None]
        g_ref[...] = gv
        m2 = ADAM_B1 * m_ref[...] + (1.0 - ADAM_B1) * gv
        v2 = ADAM_B2 * v_ref[...] + (1.0 - ADAM_B2) * jnp.square(gv)
        m_hat = m2 / (1.0 - ADAM_B1 ** ADAM_STEP)
        v_hat = v2 / (1.0 - ADAM_B2 ** ADAM_STEP)
        d_ref[...] = -ADAM_LR * (m_hat / (jnp.sqrt(v_hat) + ADAM_EPS) + ADAM_WD * w_ref[...])
        mo_ref[...] = m2
        vo_ref[...] = v2

    blk = pl.BlockSpec((1, 1, tr, C), lambda l, h, r, c_ref: (l, h, r, 0))

    def gblk(i, mine):
        def index(l, h, r, c_ref):
            use = jnp.logical_and(l == i, (h == c_ref[0]) == mine)
            return (jnp.where(use, r, 0), 0)
        return pl.BlockSpec((tr, C), index)

    grid_spec = pltpu.PrefetchScalarGridSpec(
        num_scalar_prefetch=1, grid=(L, 2, R // tr),
        in_specs=[blk] + [gblk(i, True) for i in range(L)] + [gblk(i, False) for i in range(L)] + [blk, blk],
        out_specs=[blk] * 4)
    return pl.pallas_call(
        body, name=name, grid_spec=grid_spec, out_shape=[jax.ShapeDtypeStruct(w.shape, F32)] * 4,
        compiler_params=_cp("parallel", "parallel", "parallel"),
    )(jnp.reshape(c, (1,)).astype(jnp.int32), w, *g_mine, *g_other, m, v)


BIG = ("w_in", "w_branch_a", "w_branch_b", "w_out", "ffn_w_gate", "ffn_w_up", "ffn_w_down")
ROW_SHARDED = ("w_out", "ffn_w_down")
SMALL = ("norm1_g", "dn_conv_w", "dn_a_log", "dn_dt_bias", "dn_onorm_g", "sg_ln_g", "sg_ln_b", "sg_w", "sg_b",
         "norm2_g", "ffn_conv_w", "ffn_conv_b", "final_norm_g")
SMALL_SHARDED = ("dn_conv_w", "ffn_conv_w")


def _pack_rows(arrs, mult):
    flat = jnp.concatenate([jnp.reshape(a, (-1,)) for a in arrs])
    n = flat.shape[0]
    rows = -(-n // (128 * mult)) * mult
    return jnp.reshape(jnp.pad(flat, (0, rows * 128 - n)), (rows, 128))


def _unpack(flat2d, shapes):
    flat = jnp.reshape(flat2d, (-1,))
    out, off = [], 0
    for shp in shapes:
        n = math.prod(shp)
        out.append(jnp.reshape(flat[off:off + n], shp))
        off += n
    return out


def kernel(x, norm1_g, w_in, dn_conv_w, dn_a_log, dn_dt_bias, dn_onorm_g, sg_ln_g, sg_ln_b, sg_w, sg_b, w_branch_a, w_branch_b, w_out, norm2_g, ffn_w_gate, ffn_w_up, ffn_conv_w, ffn_conv_b, ffn_w_down, final_norm_g, loss_target, m_norm1_g, m_w_in, m_dn_conv_w, m_dn_a_log, m_dn_dt_bias, m_dn_onorm_g, m_sg_ln_g, m_sg_ln_b, m_sg_w, m_sg_b, m_w_branch_a, m_w_branch_b, m_w_out, m_norm2_g, m_ffn_w_gate, m_ffn_w_up, m_ffn_conv_w, m_ffn_conv_b, m_ffn_w_down, m_final_norm_g, v_norm1_g, v_w_in, v_dn_conv_w, v_dn_a_log, v_dn_dt_bias, v_dn_onorm_g, v_sg_ln_g, v_sg_ln_b, v_sg_w, v_sg_b, v_w_branch_a, v_w_branch_b, v_w_out, v_norm2_g, v_ffn_w_gate, v_ffn_w_up, v_ffn_conv_w, v_ffn_conv_b, v_ffn_w_down, v_final_norm_g):
    W = dict(norm1_g=norm1_g, w_in=w_in, dn_conv_w=dn_conv_w, dn_a_log=dn_a_log, dn_dt_bias=dn_dt_bias,
             dn_onorm_g=dn_onorm_g, sg_ln_g=sg_ln_g, sg_ln_b=sg_ln_b, sg_w=sg_w, sg_b=sg_b, w_branch_a=w_branch_a,
             w_branch_b=w_branch_b, w_out=w_out, norm2_g=norm2_g, ffn_w_gate=ffn_w_gate, ffn_w_up=ffn_w_up,
             ffn_conv_w=ffn_conv_w, ffn_conv_b=ffn_conv_b, ffn_w_down=ffn_w_down, final_norm_g=final_norm_g)
    M = dict(norm1_g=m_norm1_g, w_in=m_w_in, dn_conv_w=m_dn_conv_w, dn_a_log=m_dn_a_log, dn_dt_bias=m_dn_dt_bias,
             dn_onorm_g=m_dn_onorm_g, sg_ln_g=m_sg_ln_g, sg_ln_b=m_sg_ln_b, sg_w=m_sg_w, sg_b=m_sg_b,
             w_branch_a=m_w_branch_a, w_branch_b=m_w_branch_b, w_out=m_w_out, norm2_g=m_norm2_g,
             ffn_w_gate=m_ffn_w_gate, ffn_w_up=m_ffn_w_up, ffn_conv_w=m_ffn_conv_w, ffn_conv_b=m_ffn_conv_b,
             ffn_w_down=m_ffn_w_down, final_norm_g=m_final_norm_g)
    V = dict(norm1_g=v_norm1_g, w_in=v_w_in, dn_conv_w=v_dn_conv_w, dn_a_log=v_dn_a_log, dn_dt_bias=v_dn_dt_bias,
             dn_onorm_g=v_dn_onorm_g, sg_ln_g=v_sg_ln_g, sg_ln_b=v_sg_ln_b, sg_w=v_sg_w, sg_b=v_sg_b,
             w_branch_a=v_w_branch_a, w_branch_b=v_w_branch_b, w_out=v_w_out, norm2_g=v_norm2_g,
             ffn_w_gate=v_ffn_w_gate, ffn_w_up=v_ffn_w_up, ffn_conv_w=v_ffn_conv_w, ffn_conv_b=v_ffn_conv_b,
             ffn_w_down=v_ffn_w_down, final_norm_g=v_final_norm_g)
    cx, cy, cc = _coords()
    chip = 2 * cx + cy
    L = w_in.shape[0]

    D = w_in.shape[1]
    cs_in = w_in.shape[2]
    c1 = 4 * WD
    ba_chip, ba_off = c1 // cs_in, c1 % cs_in
    assert ba_off + 2 * H <= cs_in
    n_main = N_CHIPS * cs_in - 2 * H
    main_start = [i * cs_in - (2 * H if i > ba_chip else 0) for i in range(N_CHIPS)]
    main_len = [cs_in - (2 * H if i == ba_chip else 0) for i in range(N_CHIPS)]
    tile0 = [16 * (s // 16) for s in main_start]
    shift = [s - t for s, t in zip(main_start, tile0)]
    rp_in = -(-max(sh + ln for sh, ln in zip(shift, main_len)) // 32) * 32
    seg = [tile0[i + 1] - tile0[i] for i in range(N_CHIPS - 1)] + [n_main - tile0[-1]]
    assert all(s + 16 <= rp_in for s in seg[:-1]) and seg[-1] <= rp_in and tile0[-1] + rp_in <= n_main + 128
    my_shift = jnp.asarray(shift, jnp.int32)[chip]

    mine = {n: W[n].astype(BF16) for n in BIG if n != "w_in"}
    wt = jnp.swapaxes(W["w_in"], 1, 2).astype(BF16)
    ba = wt[:, ba_off:ba_off + 2 * H]
    local_row = lax.broadcasted_iota(jnp.int32, (cs_in, 1), 0)
    without_ba = jnp.where(local_row < ba_off, wt, jnp.pad(wt[:, 2 * H:], ((0, 0), (0, 2 * H), (0, 0))))
    mine["w_in"] = lax.dynamic_update_slice(jnp.zeros((L, rp_in, D), BF16),
                                            jnp.where(chip == ba_chip, without_ba, wt), (0, my_shift, 0))
    mine["w_ba"] = jnp.pad(jnp.where(chip == ba_chip, ba, jnp.zeros_like(ba)), ((0, 0), (0, 32 - 2 * H), (0, 0)))

    def halves(a, lead=0):
        return jnp.reshape(a, a.shape[:lead] + (2, a.shape[lead] // 2) + a.shape[lead + 1:])

    first = [(0, "w_in"), (0, "w_ba")]
    my_taps = _pack_rows([W[n] for n in SMALL_SHARDED], 32)
    first_gathered = _ag_layers([halves(mine[n][l]) for l, n in first] + [halves(my_taps)])
    all_taps = jnp.reshape(first_gathered[-1], (N_CHIPS,) + my_taps.shape)
    tap_shards = [_unpack(jnp.where(chip == i, my_taps, all_taps[i]), [W[n].shape for n in SMALL_SHARDED])
                  for i in range(N_CHIPS)]
    taps_full = {n: jnp.concatenate([tap_shards[i][k] for i in range(N_CHIPS)], axis=-1)
                 for k, n in enumerate(SMALL_SHARDED)}

    ops = []
    for l in range(L):
        p = {n: W[n][l] for n in W if n not in ("final_norm_g",) + BIG + SMALL_SHARDED}
        p.update({n: taps_full[n][l] for n in SMALL_SHARDED})
        ops.append(_prep_small(p))

    def weights_landed(items, gathered):
        got = {}
        for (l, n), a in zip(items, gathered):
            a = jnp.reshape(a, (N_CHIPS,) + mine[n].shape[1:])
            got[(l, n)] = [jnp.where(chip == i, mine[n][l], a[i]) for i in range(N_CHIPS)]
        for (l, n), parts in got.items():
            if n == "w_in":
                pieces = [parts[0][:seg[0]]]
                for i in range(1, N_CHIPS):
                    pieces += [parts[i][:16] + parts[i - 1][seg[i - 1]:seg[i - 1] + 16], parts[i][16:seg[i]]]
                ba_rows = got[(l, "w_ba")][ba_chip][:2 * H]
                ops[l]["w_in_t"] = jnp.concatenate(pieces + [ba_rows, jnp.zeros((128 - 2 * H, D), BF16)], axis=0)
            elif n != "w_ba":
                ops[l][n] = jnp.concatenate(parts, axis=0 if n in ROW_SHARDED else 1)

    partial_sums, chip_sums = {}, {}

    grad_slices = {}

    def pair_sums(items, shares):
        for (l, n), b in zip(items, shares):
            partial_sums[(l, n)] = _rs_add_pair(grad_slices[(l, n)], b, cc, "rs_add_pair_" + n)

    def grad_partials(l, names, g, exchange_now=True):
        Gs = []
        for n in names:
            if n == "w_in":
                a = jnp.stack([g["w_in_t"][t:t + rp_in] for t in tile0])
            elif n == "w_ba":
                a = jnp.broadcast_to(g["w_in_t"][n_main:n_main + 32][None], (N_CHIPS, 32, D))
            elif n in ROW_SHARDED:
                a = jnp.reshape(g[n], (N_CHIPS, g[n].shape[0] // N_CHIPS, g[n].shape[1]))
            else:
                a = jnp.moveaxis(jnp.reshape(g[n], (g[n].shape[0], N_CHIPS, g[n].shape[1] // N_CHIPS)), 1, 0)
            grad_slices[(l, n)] = halves(a, 1)
        if exchange_now:
            items = [(l, n) for n in names]
            pair_sums(items, _rs_pair_exchange([grad_slices[i] for i in items]))

    def carrier(l, plan, kind, landed):
        source = {"gather": lambda i: halves(mine[i[1]][i[0]]), "pair": grad_slices.get, "exchange": partial_sums.get}

        def payload(kernel):
            items = plan.get((l, kernel))
            return (kind(kernel), [source[kind(kernel)](i) for i in items]) if items else None

        return _Carrier(payload, lambda kernel, res: landed(kernel)(plan[(l, kernel)], res))

    FFN = ("ffn_w_gate", "ffn_w_up", "ffn_w_down")
    REST = ("w_in", "w_ba", "w_branch_a", "w_branch_b", "w_out")
    fwd_plan = {(0, "proj"): [(0, "w_branch_a"), (0, "w_branch_b"), (0, "w_out"), (0, "ffn_w_gate")],
                (0, "dn_core"): [(0, "ffn_w_up"), (1, "w_in"), (1, "w_ba")],
                (0, "ffn_gate"): [(0, "ffn_w_down")],
                (1, "dn_core"): [(1, "w_branch_a"), (1, "w_branch_b"), (1, "w_out")] + [(1, n) for n in FFN]}
    bwd_plan = {(1, "d_merged"): [(1, n) for n in FFN],
                (0, "d_merged"): [(0, n) for n in FFN],
                (1, "dn_core"): [(1, n) for n in FFN],
                (0, "d_act"): [(1, "w_branch_a"), (1, "w_branch_b"), (1, "w_out")],
                (0, "dn_core"): [(1, "w_in"), (1, "w_ba"), (0, "ffn_w_down")],
                (0, "dw_in"): [(0, "ffn_w_gate"), (0, "ffn_w_up")],
                (0, "dh"): [(0, n) for n in REST]}

    def sums_landed(items, res):
        chip_sums.update(zip(items, res))

    weights_landed(first, first_gathered[:-1])
    xs, saved = x[0], []
    for l in range(L):
        xs, s = _layer_fwd(xs, ops[l], carrier(l, fwd_plan, lambda kernel: "gather", lambda kernel: weights_landed))
        saved.append(s)
    dx, dgf, loss = _loss_head(xs, final_norm_g[None], loss_target[0])
    loss = loss[0, 0]
    grads = [None] * L
    for l in reversed(range(L)):
        dx, grads[l] = _layer_bwd(
            dx, ops[l], saved[l],
            carrier(l, bwd_plan, lambda kernel: "pair" if kernel == "d_merged" else "exchange",
                    lambda kernel: pair_sums if kernel == "d_merged" else sums_landed),
            functools.partial(grad_partials, l, FFN, exchange_now=False), functools.partial(grad_partials, l, REST))
    travelled = BIG + ("w_ba",)
    g_mine = [[_rs_sum_chips(partial_sums[(l, n)], chip_sums[(l, n)], chip, "rs_sum_chips_" + n) for n in travelled]
              for l in range(L)]
    swapped = _rs_pair_swap(g_mine[0] + g_mine[1])
    g_other = [swapped[:len(travelled)], swapped[len(travelled):]]

    def both_halves(l, n):
        a, b = g_mine[l][travelled.index(n)], g_other[l][travelled.index(n)]
        return jnp.where(cc == 0, jnp.concatenate([a, b]), jnp.concatenate([b, a]))

    def w_in_grad_rows(l):
        m = lax.dynamic_slice_in_dim(both_halves(l, "w_in"), my_shift, cs_in, axis=0)
        ba_rows = jnp.pad(both_halves(l, "w_ba")[:2 * H], ((ba_off, cs_in - ba_off - 2 * H), (0, 0)))
        moved = jnp.pad(m[:cs_in - 2 * H], ((2 * H, 0), (0, 0)))
        with_ba = jnp.where(local_row < ba_off, m, jnp.where(local_row < ba_off + 2 * H, ba_rows, moved))
        return jnp.where(chip == ba_chip, with_ba, m)

    small = {n: jnp.stack([g[n] for g in grads]) for n in SMALL if n != "final_norm_g"}
    small["final_norm_g"] = dgf
    shapes = [taps_full[n].shape if n in SMALL_SHARDED else W[n].shape for n in SMALL] + [(1,)]
    sflat = _pack_rows([small[n] for n in SMALL] + [jnp.reshape(loss, (1,))], 16)
    sred = _unpack(_sum_slots(_ag8(sflat), "sum_small"), shapes)
    g_small = dict(zip(SMALL, sred[:-1]))
    loss_total = sred[-1][0]
    for n in SMALL_SHARDED:
        cs = W[n].shape[-1]
        g_small[n] = lax.dynamic_slice_in_dim(g_small[n], chip * cs, cs, axis=-1)

    g_big, delta, new_m, new_v = {}, {}, {}, {}
    for k, n in enumerate(BIG):
        gm, go = [g_mine[l][k] for l in range(L)], [g_other[l][k] for l in range(L)]
        if n == "w_in":
            g_t = jnp.stack([w_in_grad_rows(l) for l in range(L)], axis=1)
            outs = _adamw(*[jnp.transpose(a, (2, 0, 1)) for a in (W[n],)], g_t,
                          *[jnp.transpose(a, (2, 0, 1)) for a in (M[n], V[n])], "adamw_" + n)
            g_big[n], delta[n], new_m[n], new_v[n] = [jnp.transpose(o, (1, 2, 0)) for o in (g_t,) + tuple(outs)]
        else:
            outs = _adamw_halves(halves(W[n], 1), gm, go, cc, halves(M[n], 1), halves(V[n], 1), "adamw_" + n)
            g_big[n], delta[n], new_m[n], new_v[n] = [jnp.reshape(o, W[n].shape) for o in outs]
    for n in SMALL:
        shp = W[n].shape
        as3d = (1,) * (3 - len(shp)) + shp if len(shp) <= 3 else (-1,) + shp[-2:]
        outs = _adamw(*[jnp.reshape(d[n], as3d) for d in (W, g_small, M, V)], "adamw_" + n)
        delta[n], new_m[n], new_v[n] = [jnp.reshape(o, shp) for o in outs]

    names = list(W)
    grad_w = {**g_big, **g_small}
    return (loss_total, dx[None], *[grad_w[n] for n in names], *[delta[n] for n in names],
            *[new_m[n] for n in names], *[new_v[n] for n in names])
```

```python
import functools
import math

import jax
import jax.numpy as jnp
from jax import lax
from jax.experimental import pallas as pl
from jax.experimental.pallas import tpu as pltpu

F32 = jnp.float32
BF16 = jnp.bfloat16
MESH = pl.DeviceIdType.MESH

EPS = 1e-6
H = 8
DH = 128
WD = H * DH
DNC = 64
SGC = 128
DN_K = 4
FF_K = 3
DEPTH = 2
N_CHIPS = 4

ADAM_LR = 0.001
ADAM_B1 = 0.9
ADAM_B2 = 0.999
ADAM_EPS = 1e-08
ADAM_WD = 0.01
ADAM_STEP = 10

VMEM_LIMIT = 56 * 1024 * 1024

NN = (((1,), (0,)), ((), ()))
NT = (((1,), (1,)), ((), ()))
TN = (((0,), (0,)), ((), ()))

OQ, OZ, OU, OV, OGA = 0, 3 * WD, 4 * WD, 5 * WD, 6 * WD


def _cp(*sem):
    return pltpu.CompilerParams(dimension_semantics=sem or None, vmem_limit_bytes=VMEM_LIMIT)


def _tile(dim, pref, unit=128):
    if dim <= pref:
        return dim
    t = (pref // unit) * unit
    while t >= unit:
        if dim % t == 0:
            return t
        t -= unit
    return dim


def _bdot(a, b, dn=NN):
    return lax.dot_general(a.astype(BF16), b.astype(BF16), dn, preferred_element_type=F32)


def _lsum(x):
    return jnp.sum(x, axis=1, keepdims=True)


def _sig(x):
    return 0.5 * jnp.tanh(0.5 * x) + 0.5


def _silu_and_grad(x):
    s = _sig(x)
    return x * s, s * (1.0 + x * (1.0 - s))


def _gelu_parts(x):
    a = jnp.abs(x) * (2.0 ** -0.5)
    t = 1.0 / (1.0 + 0.3275911 * a)
    poly = t * (0.254829592 + t * (-0.284496736 + t * (1.421413741 + t * (-1.453152027 + t * 1.061405429))))
    e = jnp.exp(-a * a)
    half = 0.5 * poly * e
    return jnp.where(x < 0, half, 1.0 - half), e * (1.0 / math.sqrt(2.0 * math.pi))


def _gelu(x):
    return x * _gelu_parts(x)[0]


def _gelu_and_grad(x):
    cdf, pdf = _gelu_parts(x)
    return x * cdf, cdf + x * pdf


def _shift_down(x, k):
    if k == 0:
        return x
    y = pltpu.roll(x, k, 0)
    rows = lax.broadcasted_iota(jnp.int32, (8, x.shape[1]), 0)
    return jnp.concatenate([jnp.where(rows >= k, y[:8], 0.0), y[8:]], axis=0)


def _shift_up(x, k):
    if k == 0:
        return x
    n = x.shape[0]
    y = pltpu.roll(x, n - k, 0)
    rows = lax.broadcasted_iota(jnp.int32, (8, x.shape[1]), 0)
    return jnp.concatenate([y[:n - 8], jnp.where(rows < 8 - k, y[n - 8:], 0.0)], axis=0)


def _comm_fns(comm):
    if not comm:
        return None, None, dict(out_shape=[], specs=[], sems=[]), ()
    kind, arrays = comm
    start, finish, specs = {"gather": (_ag_start, _ag_finish, _ag_specs),
                            "exchange": (_rsx_start, _rsx_finish, _rsx_specs),
                            "pair": (_pair_start, _pair_finish, _pair_specs)}[kind]
    return start, finish, specs(arrays), tuple(arrays)


def _mm(a, b, mode, out_dtype, add=None, name="mm", comm=None):
    if mode == "tn":
        K, M = a.shape
    else:
        M, K = a.shape
    N = b.shape[0] if mode == "nt" else b.shape[1]
    tm, tn, tk = _tile(M, 1152), _tile(N, 1536), _tile(K, 3584)
    nk = K // tk
    ni, nj = M // tm, N // tn
    dn = {"nn": NN, "nt": NT, "tn": TN}[mode]
    c_start, c_finish, c_sp, payload = _comm_fns(comm)
    nc = len(payload)
    n_add = 0 if add is None else 1

    def body(*refs):
        a_ref, b_ref = refs[:2]
        add_ref = refs[2] if n_add else None
        c_in = refs[2 + n_add:2 + n_add + nc]
        o_ref = refs[2 + n_add + nc]
        c_out = refs[3 + n_add + nc:3 + n_add + 2 * nc]
        rest = refs[3 + n_add + 2 * nc:]
        acc_ref = rest[0] if nk > 1 else None
        sems = rest[1:] if nk > 1 else rest
        i, j, k = pl.program_id(0), pl.program_id(1), pl.program_id(2)

        if nc:
            @pl.when(jnp.logical_and(jnp.logical_and(i == 0, j == 0), k == 0))
            def _():
                c_start(c_in, c_out, *sems)

        def finish(r):
            if add is not None:
                r = r + add_ref[...]
            o_ref[...] = r.astype(o_ref.dtype)

        part = lax.dot_general(a_ref[...], b_ref[...], dn, preferred_element_type=F32)
        if nk == 1:
            finish(part)
        else:
            @pl.when(k == 0)
            def _():
                acc_ref[...] = part

            @pl.when(k > 0)
            def _():
                acc_ref[...] += part

            @pl.when(k == nk - 1)
            def _():
                finish(acc_ref[...])

        if nc:
            @pl.when(jnp.logical_and(jnp.logical_and(i == ni - 1, j == nj - 1), k == nk - 1))
            def _():
                c_finish(c_in, c_out, *sems)

    a_spec = (pl.BlockSpec((tk, tm), lambda i, j, k: (k, i)) if mode == "tn"
              else pl.BlockSpec((tm, tk), lambda i, j, k: (i, k)))
    b_spec = (pl.BlockSpec((tn, tk), lambda i, j, k: (j, k)) if mode == "nt"
              else pl.BlockSpec((tk, tn), lambda i, j, k: (k, j)))
    o_spec = pl.BlockSpec((tm, tn), lambda i, j, k: (i, j))
    in_specs = [a_spec, b_spec] + ([o_spec] if add is not None else []) + c_sp["specs"]
    args = (a, b) + ((add,) if add is not None else ()) + payload
    outs = pl.pallas_call(
        body, name=name + ("_" + comm[0] if nc else ""), grid=(ni, nj, nk), in_specs=in_specs,
        out_specs=[o_spec] + c_sp["specs"],
        out_shape=[jax.ShapeDtypeStruct((M, N), out_dtype)] + c_sp["out_shape"],
        scratch_shapes=([pltpu.VMEM((tm, tn), F32)] if nk > 1 else []) + c_sp["sems"],
        compiler_params=_cp("arbitrary", "arbitrary", "arbitrary") if nc else _cp("parallel", "parallel", "arbitrary"),
    )(*args)
    return (outs[0], list(outs[1:])) if nc else outs[0]


def _rms_fwd(x, g, name):
    T, D = x.shape
    tt = _tile(T, 256, 16)

    def body(x_ref, g_ref, o_ref):
        xv = x_ref[...]
        r = lax.rsqrt(jnp.mean(xv * xv, axis=-1, keepdims=True) + EPS)
        o_ref[...] = (xv * r * g_ref[...]).astype(o_ref.dtype)

    return pl.pallas_call(
        body, name=name, grid=(T // tt,),
        in_specs=[pl.BlockSpec((tt, D), lambda i: (i, 0)), pl.BlockSpec((1, D), lambda i: (0, 0))],
        out_specs=pl.BlockSpec((tt, D), lambda i: (i, 0)),
        out_shape=jax.ShapeDtypeStruct((T, D), BF16), compiler_params=_cp("parallel"),
    )(x, g)


def _rms_bwd(x, g, dh, dres, name):
    T, D = x.shape
    tt = _tile(T, 256, 16)

    def body(x_ref, g_ref, dh_ref, dres_ref, dx_ref, dg_ref):
        @pl.when(pl.program_id(0) == 0)
        def _():
            dg_ref[...] = jnp.zeros_like(dg_ref)

        xv = x_ref[...]
        r = lax.rsqrt(jnp.mean(xv * xv, axis=-1, keepdims=True) + EPS)
        xh = xv * r
        dh_v = dh_ref[...]
        dy = dh_v * g_ref[...]
        dx_ref[...] = dres_ref[...] + r * (dy - xh * jnp.mean(dy * xh, axis=-1, keepdims=True))
        dg_ref[...] += jnp.sum(dh_v * xh, axis=0, keepdims=True)

    row = pl.BlockSpec((tt, D), lambda i: (i, 0))
    vec = pl.BlockSpec((1, D), lambda i: (0, 0))
    return pl.pallas_call(
        body, name=name, grid=(T // tt,), in_specs=[row, vec, row, row], out_specs=[row, vec],
        out_shape=[jax.ShapeDtypeStruct((T, D), F32), jax.ShapeDtypeStruct((1, D), F32)],
        compiler_params=_cp("arbitrary"),
    )(x, g, dh, dres)


def _loss_head(x, g, tgt, name="loss_head"):
    T, D = x.shape
    tt = _tile(T, 256, 16)

    def body(x_ref, g_ref, t_ref, dx_ref, dg_ref, loss_ref):
        @pl.when(pl.program_id(0) == 0)
        def _():
            dg_ref[...] = jnp.zeros_like(dg_ref)
            loss_ref[...] = jnp.zeros_like(loss_ref)

        xv = x_ref[...]
        r = lax.rsqrt(jnp.mean(xv * xv, axis=-1, keepdims=True) + EPS)
        xh = xv * r
        err = xh * g_ref[...] - t_ref[...]
        part = 0.5 * jnp.sum(jnp.mean(err * err, axis=-1, keepdims=True), axis=0, keepdims=True)
        loss_ref[...] += jnp.broadcast_to(part, loss_ref.shape)
        dy = err * (1.0 / D)
        dg_ref[...] += jnp.sum(dy * xh, axis=0, keepdims=True)
        dyh = dy * g_ref[...]
        dx_ref[...] = r * (dyh - xh * jnp.mean(dyh * xh, axis=-1, keepdims=True))

    row = pl.BlockSpec((tt, D), lambda i: (i, 0))
    vec = pl.BlockSpec((1, D), lambda i: (0, 0))
    return pl.pallas_call(
        body, name=name, grid=(T // tt,), in_specs=[row, vec, row],
        out_specs=[row, vec, pl.BlockSpec((1, 128), lambda i: (0, 0))],
        out_shape=[jax.ShapeDtypeStruct((T, D), F32), jax.ShapeDtypeStruct((1, D), F32),
                   jax.ShapeDtypeStruct((1, 128), F32)],
        compiler_params=_cp("arbitrary"),
    )(x, g, tgt)


def _ba_fwd(proj, alog, dtb, oba, name="dn_ba_fwd"):
    T = proj.shape[0]
    tt = _tile(T, 512, 8)

    def body(p_ref, al_ref, dt_ref, o_ref):
        raw = p_ref[...].astype(F32)
        lane = lax.broadcasted_iota(jnp.int32, raw.shape, 1)
        z = raw + dt_ref[...]
        sp = jnp.maximum(z, 0.0) + jnp.log(1.0 + jnp.exp(-jnp.abs(z)))
        gl = -jnp.exp(al_ref[...]) * sp
        o_ref[...] = jnp.where(lane < H, _sig(raw), jnp.where(lane < 2 * H, gl, 0.0))

    vec = pl.BlockSpec((1, 128), lambda i: (0, 0))
    return pl.pallas_call(
        body, name=name, grid=(T // tt,),
        in_specs=[pl.BlockSpec((tt, 128), lambda i: (i, oba // 128)), vec, vec],
        out_specs=pl.BlockSpec((tt, 128), lambda i: (i, 0)),
        out_shape=jax.ShapeDtypeStruct((T, 128), F32), compiler_params=_cp("parallel"),
    )(proj, alog, dtb)


def _ba_bwd(proj, alog, dtb, dbg, oba, name="dn_ba_bwd"):
    T = proj.shape[0]
    tt = _tile(T, 512, 16)

    def body(p_ref, al_ref, dt_ref, d_ref, o_ref, dal_ref, ddt_ref):
        @pl.when(pl.program_id(0) == 0)
        def _():
            dal_ref[...] = jnp.zeros_like(dal_ref)
            ddt_ref[...] = jnp.zeros_like(ddt_ref)

        raw = p_ref[...].astype(F32)
        d = d_ref[...]
        lane = lax.broadcasted_iota(jnp.int32, raw.shape, 1)
        z = raw + dt_ref[...]
        sp = jnp.maximum(z, 0.0) + jnp.log(1.0 + jnp.exp(-jnp.abs(z)))
        na = -jnp.exp(al_ref[...])
        is_g = jnp.logical_and(lane >= H, lane < 2 * H)
        b = _sig(raw)
        dz = jnp.where(is_g, d * na * _sig(z), 0.0)
        o_ref[...] = jnp.where(lane < H, d * b * (1.0 - b), dz).astype(o_ref.dtype)
        dal_ref[...] += jnp.sum(jnp.where(is_g, d * na * sp, 0.0), axis=0, keepdims=True)
        ddt_ref[...] += jnp.sum(dz, axis=0, keepdims=True)

    vec = pl.BlockSpec((1, 128), lambda i: (0, 0))
    return pl.pallas_call(
        body, name=name, grid=(T // tt,),
        in_specs=[pl.BlockSpec((tt, 128), lambda i: (i, oba // 128)), vec, vec,
                  pl.BlockSpec((tt, 128), lambda i: (i, 0))],
        out_specs=[pl.BlockSpec((tt, 128), lambda i: (i, 0)), vec, vec],
        out_shape=[jax.ShapeDtypeStruct((T, 128), BF16), jax.ShapeDtypeStruct((1, 128), F32),
                   jax.ShapeDtypeStruct((1, 128), F32)],
        compiler_params=_cp("arbitrary"),
    )(proj, alog, dtb, dbg)


def _dn_prep_fwd(proj, convw, name="dn_prep_fwd"):
    T = proj.shape[0]
    nblk = 3 * H

    def body(p_ref, w_ref, o_ref):
        j = pl.program_id(0)
        xv = p_ref[...].astype(F32)
        w = w_ref[...]
        c = xv * w[DN_K - 1:DN_K, :]
        for k in range(1, DN_K):
            c = c + _shift_down(xv, k) * w[DN_K - 1 - k:DN_K - k, :]
        s = c * _sig(c)
        r = lax.rsqrt(_lsum(s * s) + EPS)
        o_ref[...] = jnp.where(j < 2 * H, s * r, s)

    return pl.pallas_call(
        body, name=name, grid=(nblk,),
        in_specs=[pl.BlockSpec((T, DH), lambda j: (0, j)), pl.BlockSpec((DN_K, DH), lambda j: (0, j))],
        out_specs=pl.BlockSpec((T, DH), lambda j: (0, j)),
        out_shape=jax.ShapeDtypeStruct((T, 3 * WD), F32), compiler_params=_cp("parallel"),
    )(proj, convw)


def _dn_prep_bwd(proj, convw, dq, dk, dv, name="dn_prep_bwd"):
    T = proj.shape[0]
    nblk = 3 * H

    def body(p_ref, w_ref, dq_ref, dk_ref, dv_ref, dx_ref, dw_ref):
        j = pl.program_id(0)
        xv = p_ref[...].astype(F32)
        w = w_ref[...]
        shifted = [_shift_down(xv, k) for k in range(DN_K)]
        c = shifted[0] * w[DN_K - 1:DN_K, :]
        for k in range(1, DN_K):
            c = c + shifted[k] * w[DN_K - 1 - k:DN_K - k, :]
        s, s_grad = _silu_and_grad(c)
        r = lax.rsqrt(_lsum(s * s) + EPS)
        y = s * r
        dy = jnp.where(j < H, dq_ref[...], jnp.where(j < 2 * H, dk_ref[...], dv_ref[...]))
        ds = jnp.where(j < 2 * H, r * (dy - y * _lsum(dy * y)), dy)
        dc = ds * s_grad
        dx = dc * w[DN_K - 1:DN_K, :]
        for k in range(1, DN_K):
            dx = dx + _shift_up(dc, k) * w[DN_K - 1 - k:DN_K - k, :]
        dx_ref[...] = dx.astype(dx_ref.dtype)
        rows = [jnp.sum(dc * shifted[DN_K - 1 - t], axis=0, keepdims=True) for t in range(DN_K)]
        dw_ref[...] = jnp.concatenate(rows, axis=0)

    hb = lambda off: pl.BlockSpec((T, DH), lambda j: (0, jnp.maximum(jnp.minimum(j - off, H - 1), 0)))
    return pl.pallas_call(
        body, name=name, grid=(nblk,),
        in_specs=[pl.BlockSpec((T, DH), lambda j: (0, j)), pl.BlockSpec((DN_K, DH), lambda j: (0, j)),
                  hb(0), hb(H), hb(2 * H)],
        out_specs=[pl.BlockSpec((T, DH), lambda j: (0, j)), pl.BlockSpec((DN_K, DH), lambda j: (0, j))],
        out_shape=[jax.ShapeDtypeStruct((T, 3 * WD), BF16), jax.ShapeDtypeStruct((DN_K, 3 * WD), F32)],
        compiler_params=_cp("parallel"),
    )(proj, convw, dq, dk, dv)


DN_BLOCK = 4


def _split3(a):
    hi = a.astype(BF16)
    r1 = a - hi.astype(F32)
    mid = r1.astype(BF16)
    return hi, mid, (r1 - mid.astype(F32)).astype(BF16)


def _dot3(a, b, dn=NN):
    ah, al, _ = _split3(a)
    bh, bl, _ = _split3(b)
    d = lambda p, q: lax.dot_general(p, q, dn, preferred_element_type=F32)
    return d(ah, bh) + d(ah, bl) + d(al, bh)


def _mask_dot(m, b, dn=NN):
    mb = m.astype(BF16)
    d = lambda q: (lax.dot_general(mb, q, dn, preferred_element_type=F32) if dn != TN
                   else lax.dot_general(q, mb, dn, preferred_element_type=F32))
    b0, b1, b2 = _split3(b)
    return d(b0) + d(b1) + d(b2)


def _tri_inv(A):
    ri = lax.broadcasted_iota(jnp.int32, A.shape, 0)
    ci = lax.broadcasted_iota(jnp.int32, A.shape, 1)
    T = jnp.where(ri == ci, 1.0, 0.0) - jnp.where((ri // 2) == (ci // 2), A, 0.0)
    s = 2
    while s < DNC:
        off = jnp.logical_and((ri // (2 * s)) == (ci // (2 * s)), (ri // s) != (ci // s))
        T = T - _dot3(_dot3(T, jnp.where(off, A, 0.0)), T)
        s *= 2
    return T


GH = 4
NG = H // GH
GR = GH * DNC
GK = GH * DH


def _dn_masks():
    ri = lax.broadcasted_iota(jnp.int32, (GR, GR), 0)
    ci = lax.broadcasted_iota(jnp.int32, (GR, GR), 1)
    blk = (ri // DNC) == (ci // DNC)
    wide = (lax.broadcasted_iota(jnp.int32, (GR, GK), 0) // DNC) == (lax.broadcasted_iota(jnp.int32, (GR, GK), 1) // DH)
    return dict(blk=blk, causal=jnp.logical_and(blk, ri >= ci), strict=jnp.logical_and(blk, ri > ci),
                upper=jnp.logical_and(blk, ri <= ci), eye=ri == ci, wide=wide)


def _wide(a, mk):
    return jnp.where(mk["wide"], jnp.tile(a, (1, GH)), 0.0)


def _fold(a, mk):
    a = jnp.where(mk["wide"], a, 0.0)
    out = a[:, :DH]
    for j in range(1, GH):
        out = out + a[:, j * DH:(j + 1) * DH]
    return out


def _stack_heads(ref, rows, g):
    return jnp.concatenate([ref[rows, (g * GH + j) * DH:(g * GH + j + 1) * DH] for j in range(GH)], axis=0)


def _dn_group(q_ref, k_ref, v_ref, rows, bg, gc_cols, g, mk):
    heads = [g * GH + j for j in range(GH)]
    col = lambda a, lane: jnp.concatenate([a[:, lane(h):lane(h) + 1] for h in heads], axis=0)
    q = _stack_heads(q_ref, rows, g) * (DH ** -0.5)
    k = _stack_heads(k_ref, rows, g)
    v = _stack_heads(v_ref, rows, g)
    beta = col(bg, lambda h: h)
    gcol = col(gc_cols, lambda h: H + h)
    last = [gc_cols[DNC - 1:DNC, H + h:H + h + 1] for h in heads]
    gl = jnp.concatenate([jnp.broadcast_to(t, (DNC, 1)) for t in last], axis=0)
    egl_state = jnp.concatenate([jnp.broadcast_to(jnp.exp(t), (DH, 1)) for t in last], axis=0)
    grow = _mask_dot(jnp.ones((GR, GR), F32), jnp.where(mk["eye"], gcol, 0.0))
    dec = jnp.where(mk["causal"], jnp.exp(jnp.where(mk["causal"], gcol - grow, 0.0)), 0.0)
    eg = jnp.exp(gcol)
    ek = jnp.exp(gl - gcol)
    kb = k * beta
    vb = v * beta
    kbe = kb * eg
    A = jnp.where(mk["strict"], _bdot(kb, k, NT) * dec, 0.0)
    P = jnp.where(mk["causal"], _bdot(q, k, NT) * dec, 0.0)
    return dict(q=q, k=k, v=v, beta=beta, dec=dec, eg=eg, ek=ek, egl=jnp.exp(gl), egl_state=egl_state, kb=kb, vb=vb,
                kbe=kbe, A=A, P=P, qd=q * eg, kd=k * ek, heads=heads)


def _gc_cols(bg):
    ri = lax.broadcasted_iota(jnp.int32, (DNC, DNC), 0)
    ci = lax.broadcasted_iota(jnp.int32, (DNC, DNC), 1)
    return _mask_dot(jnp.where(ri >= ci, 1.0, 0.0), bg)


def _dn_core_fwd(qkv, bg, comm=None, name="dn_core_fwd"):
    c_start, c_finish, sp, gather = _comm_fns(comm)
    T = qkv.shape[0]
    n_chunks = T // DNC
    nb = _tile(n_chunks, DN_BLOCK, 1)
    tb = nb * DNC

    ng = len(gather)
    n_steps = n_chunks // nb

    def body(*refs):
        q_ref, k_ref, v_ref, bg_ref = refs[:4]
        o_ref, s_ref, tm_ref = refs[4 + ng:7 + ng]
        S_scr = refs[7 + 2 * ng]
        comm_refs = (refs[4:4 + ng], refs[7 + ng:7 + 2 * ng]) + tuple(refs[8 + 2 * ng:])

        @pl.when(pl.program_id(0) == 0)
        def _():
            S_scr[...] = jnp.zeros_like(S_scr)
            if ng:
                c_start(*comm_refs)

        mk = _dn_masks()

        def chunk(n):
            rows = pl.ds(n * DNC, DNC)
            bgc = bg_ref[rows, :]
            gc_cols = _gc_cols(bgc)
            for g in range(NG):
                c = _dn_group(q_ref, k_ref, v_ref, rows, bgc, gc_cols, g, mk)
                Tm = _tri_inv(c["A"])
                tm_ref[n, g] = Tm
                S = S_scr[g]
                s_ref[n, g] = S
                u = _bdot(Tm, c["vb"])
                w = _bdot(Tm, c["kbe"])
                vn = u - _bdot(_wide(w, mk), S)
                o = _bdot(_wide(c["qd"], mk), S) + _bdot(c["P"], vn)
                for j, h in enumerate(c["heads"]):
                    o_ref[rows, h * DH:(h + 1) * DH] = o[j * DNC:(j + 1) * DNC]
                S_scr[g] = S * c["egl_state"] + _bdot(_wide(c["kd"], mk), vn, TN)

        for n in range(nb):
            chunk(n)

        if ng:
            @pl.when(pl.program_id(0) == n_steps - 1)
            def _():
                c_finish(*comm_refs)

    blk = lambda j: pl.BlockSpec((tb, WD), lambda i: (i, j))
    outs = pl.pallas_call(
        body, name=name + ("_" + comm[0] if ng else ""), grid=(n_steps,),
        in_specs=[blk(0), blk(1), blk(2), pl.BlockSpec((tb, 128), lambda i: (i, 0))] + sp["specs"],
        out_specs=[blk(0), pl.BlockSpec((nb, NG, GK, DH), lambda i: (i, 0, 0, 0)),
                   pl.BlockSpec((nb, NG, GR, GR), lambda i: (i, 0, 0, 0))] + sp["specs"],
        out_shape=[jax.ShapeDtypeStruct((T, WD), F32), jax.ShapeDtypeStruct((n_chunks, NG, GK, DH), F32),
                   jax.ShapeDtypeStruct((n_chunks, NG, GR, GR), F32)] + sp["out_shape"],
        scratch_shapes=[pltpu.VMEM((NG, GK, DH), F32)] + (sp["sems"] if ng else []),
        compiler_params=_cp("arbitrary"),
    )(qkv, qkv, qkv, bg, *gather)
    return outs[0], outs[1], outs[2], list(outs[3:])


def _dn_core_bwd(qkv, bg, s_all, tm_all, do, comm=None, name="dn_core_bwd"):
    c_start, c_finish, sp, exchange = _comm_fns(comm)
    T = qkv.shape[0]
    n_chunks = T // DNC
    nb = _tile(n_chunks, DN_BLOCK, 1)
    tb = nb * DNC
    n_blocks = n_chunks // nb

    nx = len(exchange)

    def body(*refs):
        q_ref, k_ref, v_ref, bg_ref, s_ref, tm_ref, do_ref = refs[:7]
        dq_ref, dk_ref, dv_ref, dbg_ref = refs[7 + nx:11 + nx]
        dS_scr = refs[11 + 2 * nx]
        comm_refs = (refs[7:7 + nx], refs[11 + nx:11 + 2 * nx]) + tuple(refs[12 + 2 * nx:])

        @pl.when(pl.program_id(0) == 0)
        def _():
            dS_scr[...] = jnp.zeros_like(dS_scr)
            if nx:
                c_start(*comm_refs)

        lane = lax.broadcasted_iota(jnp.int32, (DNC, 128), 1)
        row = lax.broadcasted_iota(jnp.int32, (GR, 1), 0)

        mk = _dn_masks()

        def chunk(n):
            rows = pl.ds(n * DNC, DNC)
            ones = jnp.ones((GR, GR), F32)
            blk_f = jnp.where(mk["blk"], 1.0, 0.0)
            wide_f = jnp.where(mk["wide"], 1.0, 0.0)
            per_row = lambda m, a: _mask_dot(m, jnp.broadcast_to(a, (a.shape[0], DH)))[:, :1]
            bgc = bg_ref[rows, :]
            gc_cols = _gc_cols(bgc)
            dbg = jnp.zeros((DNC, 128), F32)
            for g in range(NG):
                c = _dn_group(q_ref, k_ref, v_ref, rows, bgc, gc_cols, g, mk)
                q, k, v, beta = c["q"], c["k"], c["v"], c["beta"]
                dec, eg, ek, egl = c["dec"], c["eg"], c["ek"], c["egl"]
                kb, vb, kbe, A, P, qd, kd = c["kb"], c["vb"], c["kbe"], c["A"], c["P"], c["qd"], c["kd"]
                S = s_ref[n, g]
                Tm = tm_ref[n, g]
                u = _bdot(Tm, vb)
                w = _bdot(Tm, kbe)
                w_wide = _wide(w, mk)
                vn = u - _bdot(w_wide, S)
                d_o = _stack_heads(do_ref, rows, g)
                dS1 = dS_scr[g]
                d_qd = _fold(_bdot(d_o, S, NT), mk)
                dP = jnp.where(mk["causal"], _bdot(d_o, vn, NT), 0.0)
                d_vn = _bdot(P, d_o, TN) + _bdot(_wide(kd, mk), dS1)
                d_kd = _fold(_bdot(vn, dS1, NT), mk)
                d_egl = per_row(wide_f, _lsum(dS1 * S))
                dS_scr[g] = dS1 * c["egl_state"] + _bdot(_wide(qd, mk), d_o, TN) - _bdot(w_wide, d_vn, TN)
                d_w = -_fold(_bdot(d_vn, S, NT), mk)
                d_vb = _bdot(Tm, d_vn, TN)
                d_kbe = _bdot(Tm, d_w, TN)
                dA = jnp.where(mk["strict"], -(_bdot(d_vb, u, NT) + _bdot(d_kbe, w, NT)), 0.0)
                dMA = dA * dec
                dMP = dP * dec
                d_kb = _bdot(dMA, k) + d_kbe * eg
                d_k = _bdot(dMA, kb, TN) + _bdot(dMP, q, TN) + d_kd * ek + d_kb * beta
                d_qs = (_bdot(dMP, k) + d_qd * eg) * (DH ** -0.5)
                d_v = d_vb * beta
                E = dA * A + dP * P
                col_sums = _mask_dot(ones, E, TN)[:, :1]
                t_kd = _lsum(d_kd * kd)
                d_gl = per_row(blk_f, t_kd) + d_egl * egl
                d_gc = (_lsum(E) - col_sums + _lsum(d_qd * qd) + _lsum(d_kbe * kbe) - t_kd
                        + jnp.where(row % DNC == DNC - 1, d_gl, 0.0))
                d_g = per_row(jnp.where(mk["upper"], 1.0, 0.0), d_gc)
                d_beta = _lsum(d_kb * k) + _lsum(d_vb * v)
                for j, h in enumerate(c["heads"]):
                    rs = slice(j * DNC, (j + 1) * DNC)
                    dq_ref[rows, h * DH:(h + 1) * DH] = d_qs[rs]
                    dk_ref[rows, h * DH:(h + 1) * DH] = d_k[rs]
                    dv_ref[rows, h * DH:(h + 1) * DH] = d_v[rs]
                    dbg = dbg + jnp.where(lane == h, d_beta[rs], 0.0) + jnp.where(lane == h + H, d_g[rs], 0.0)
            dbg_ref[rows, :] = dbg

        for n in reversed(range(nb)):
            chunk(n)

        if nx:
            @pl.when(pl.program_id(0) == n_blocks - 1)
            def _():
                c_finish(*comm_refs)

    blk = lambda j: pl.BlockSpec((tb, WD), lambda i: (n_blocks - 1 - i, j))
    small = pl.BlockSpec((tb, 128), lambda i: (n_blocks - 1 - i, 0))
    outs = pl.pallas_call(
        body, name=name + ("_" + comm[0] if nx else ""), grid=(n_blocks,),
        in_specs=[blk(0), blk(1), blk(2), small,
                  pl.BlockSpec((nb, NG, GK, DH), lambda i: (n_blocks - 1 - i, 0, 0, 0)),
                  pl.BlockSpec((nb, NG, GR, GR), lambda i: (n_blocks - 1 - i, 0, 0, 0)), blk(0)] + sp["specs"],
        out_specs=[blk(0), blk(0), blk(0), small] + sp["specs"],
        out_shape=[jax.ShapeDtypeStruct((T, WD), F32)] * 3 + [jax.ShapeDtypeStruct((T, 128), F32)] + sp["out_shape"],
        scratch_shapes=[pltpu.VMEM((NG, GK, DH), F32)] + (sp["sems"] if nx else []),
        compiler_params=_cp("arbitrary"),
    )(qkv, qkv, qkv, bg, s_all, tm_all, do, *exchange)
    return outs[0], outs[1], outs[2], outs[3], list(outs[4:])


def _dn_post_fwd(o, proj, gon, name="dn_post_fwd"):
    T = o.shape[0]
    tt = _tile(T, 256, 16)

    def body(o_ref, z_ref, g_ref, y_ref):
        for hh in range(H):
            sl = slice(hh * DH, (hh + 1) * DH)
            ov = o_ref[:, sl]
            zv = z_ref[:, sl].astype(F32)
            r = lax.rsqrt(jnp.mean(ov * ov, axis=-1, keepdims=True) + EPS)
            y_ref[:, sl] = (ov * r * g_ref[...] * (zv * _sig(zv))).astype(y_ref.dtype)

    return pl.pallas_call(
        body, name=name, grid=(T // tt,),
        in_specs=[pl.BlockSpec((tt, WD), lambda i: (i, 0)), pl.BlockSpec((tt, WD), lambda i: (i, OZ // WD)),
                  pl.BlockSpec((1, DH), lambda i: (0, 0))],
        out_specs=pl.BlockSpec((tt, WD), lambda i: (i, 0)),
        out_shape=jax.ShapeDtypeStruct((T, WD), BF16), compiler_params=_cp("parallel"),
    )(o, proj, gon)


def _dn_post_bwd(o, proj, gon, dy, name="dn_post_bwd"):
    T = o.shape[0]
    tt = _tile(T, 256, 16)

    def body(o_ref, z_ref, g_ref, dy_ref, do_ref, dz_ref, dg_ref):
        @pl.when(pl.program_id(0) == 0)
        def _():
            dg_ref[...] = jnp.zeros_like(dg_ref)

        acc = jnp.zeros((1, DH), F32)
        for hh in range(H):
            sl = slice(hh * DH, (hh + 1) * DH)
            ov = o_ref[:, sl]
            zv = z_ref[:, sl].astype(F32)
            dyv = dy_ref[:, sl]
            r = lax.rsqrt(jnp.mean(ov * ov, axis=-1, keepdims=True) + EPS)
            oh = ov * r
            nrm = oh * g_ref[...]
            gate, gate_grad = _silu_and_grad(zv)
            dn = dyv * gate
            dz_ref[:, sl] = (dyv * nrm * gate_grad).astype(dz_ref.dtype)
            doh = dn * g_ref[...]
            do_ref[:, sl] = r * (doh - oh * jnp.mean(doh * oh, axis=-1, keepdims=True))
            acc = acc + jnp.sum(dn * oh, axis=0, keepdims=True)
        dg_ref[...] += acc

    row = pl.BlockSpec((tt, WD), lambda i: (i, 0))
    vec = pl.BlockSpec((1, DH), lambda i: (0, 0))
    return pl.pallas_call(
        body, name=name, grid=(T // tt,),
        in_specs=[row, pl.BlockSpec((tt, WD), lambda i: (i, OZ // WD)), vec, row],
        out_specs=[row, row, vec],
        out_shape=[jax.ShapeDtypeStruct((T, WD), F32), jax.ShapeDtypeStruct((T, WD), BF16),
                   jax.ShapeDtypeStruct((1, DH), F32)],
        compiler_params=_cp("arbitrary"),
    )(o, proj, gon, dy)


def _sg_common(u_ref, v_ref, lng_ref, lnb_ref, with_grad=True):
    ur = u_ref[...].astype(F32)
    vr = v_ref[...].astype(F32)
    vgel, vgel_grad = _gelu_and_grad(vr) if with_grad else (_gelu(vr), None)
    mu = jnp.mean(vgel, axis=-1, keepdims=True)
    xc = vgel - mu
    rs = lax.rsqrt(jnp.mean(xc * xc, axis=-1, keepdims=True) + EPS)
    xh = xc * rs
    vg = xh * lng_ref[...] + lnb_ref[...]
    return ur, vgel_grad, rs, xh, vg


def _sg_fwd(proj, lng, lnb, sgw, sgbt, name="sg_fwd"):
    T = proj.shape[0]

    def body(u_ref, v_ref, lng_ref, lnb_ref, w_ref, bt_ref, y_ref):
        ur, _, _, _, vg = _sg_common(u_ref, v_ref, lng_ref, lnb_ref, with_grad=False)
        ri = lax.broadcasted_iota(jnp.int32, (SGC, SGC), 0)
        ci = lax.broadcasted_iota(jnp.int32, (SGC, SGC), 1)
        ug = _gelu(ur)
        for g in range(H):
            sl = slice(g * DH, (g + 1) * DH)
            ws = jnp.where(ri >= ci, w_ref[g], 0.0)
            mixed = _bdot(ws, vg[:, sl]) + bt_ref[:, g:g + 1]
            y_ref[:, sl] = (ug[:, sl] * mixed).astype(y_ref.dtype)

    vec = pl.BlockSpec((1, WD), lambda i: (0, 0))
    return pl.pallas_call(
        body, name=name, grid=(T // SGC,),
        in_specs=[pl.BlockSpec((SGC, WD), lambda i: (i, OU // WD)), pl.BlockSpec((SGC, WD), lambda i: (i, OV // WD)),
                  vec, vec, pl.BlockSpec((H, SGC, SGC), lambda i: (0, 0, 0)),
                  pl.BlockSpec((SGC, H), lambda i: (0, 0))],
        out_specs=pl.BlockSpec((SGC, WD), lambda i: (i, 0)),
        out_shape=jax.ShapeDtypeStruct((T, WD), BF16), compiler_params=_cp("parallel"),
    )(proj, proj, lng, lnb, sgw, sgbt)


def _sg_bwd(proj, lng, lnb, sgw, sgbt, dy, name="sg_bwd"):
    T = proj.shape[0]

    def body(u_ref, v_ref, lng_ref, lnb_ref, w_ref, bt_ref, dy_ref,
             du_ref, dv_ref, dw_ref, dbt_ref, dlng_ref, dlnb_ref):
        @pl.when(pl.program_id(0) == 0)
        def _():
            dw_ref[...] = jnp.zeros_like(dw_ref)
            dbt_ref[...] = jnp.zeros_like(dbt_ref)
            dlng_ref[...] = jnp.zeros_like(dlng_ref)
            dlnb_ref[...] = jnp.zeros_like(dlnb_ref)

        ur, vgel_grad, rs, xh, vg = _sg_common(u_ref, v_ref, lng_ref, lnb_ref)
        ri = lax.broadcasted_iota(jnp.int32, (SGC, SGC), 0)
        ci = lax.broadcasted_iota(jnp.int32, (SGC, SGC), 1)
        ug, ug_grad = _gelu_and_grad(ur)
        dyv = dy_ref[...]
        dbt = jnp.zeros((SGC, 128), F32)
        dvg_parts = []
        for g in range(H):
            sl = slice(g * DH, (g + 1) * DH)
            ws = jnp.where(ri >= ci, w_ref[g], 0.0)
            mixed = _bdot(ws, vg[:, sl]) + bt_ref[:, g:g + 1]
            dyg = dyv[:, sl]
            du_ref[:, sl] = (dyg * mixed * ug_grad[:, sl]).astype(du_ref.dtype)
            dmix = dyg * ug[:, sl]
            dw_ref[g] += jnp.where(ri >= ci, _bdot(dmix, vg[:, sl], NT), 0.0)
            dbt = dbt + jnp.where(ci == g, _lsum(dmix), 0.0)
            dvg_parts.append(_bdot(ws, dmix, TN))
        dbt_ref[...] += dbt
        dvg = jnp.concatenate(dvg_parts, axis=1)
        dlng_ref[...] += jnp.sum(dvg * xh, axis=0, keepdims=True)
        dlnb_ref[...] += jnp.sum(dvg, axis=0, keepdims=True)
        dxh = dvg * lng_ref[...]
        dvgel = rs * (dxh - jnp.mean(dxh, axis=-1, keepdims=True) - xh * jnp.mean(dxh * xh, axis=-1, keepdims=True))
        dv_ref[...] = (dvgel * vgel_grad).astype(dv_ref.dtype)

    vec = pl.BlockSpec((1, WD), lambda i: (0, 0))
    row = pl.BlockSpec((SGC, WD), lambda i: (i, 0))
    return pl.pallas_call(
        body, name=name, grid=(T // SGC,),
        in_specs=[pl.BlockSpec((SGC, WD), lambda i: (i, OU // WD)), pl.BlockSpec((SGC, WD), lambda i: (i, OV // WD)),
                  vec, vec, pl.BlockSpec((H, SGC, SGC), lambda i: (0, 0, 0)),
                  pl.BlockSpec((SGC, H), lambda i: (0, 0)), row],
        out_specs=[row, row, pl.BlockSpec((H, SGC, SGC), lambda i: (0, 0, 0)),
                   pl.BlockSpec((SGC, 128), lambda i: (0, 0)), vec, vec],
        out_shape=[jax.ShapeDtypeStruct((T, WD), BF16), jax.ShapeDtypeStruct((T, WD), BF16),
                   jax.ShapeDtypeStruct((H, SGC, SGC), F32), jax.ShapeDtypeStruct((SGC, 128), F32),
                   jax.ShapeDtypeStruct((1, WD), F32), jax.ShapeDtypeStruct((1, WD), F32)],
        compiler_params=_cp("arbitrary"),
    )(proj, proj, lng, lnb, sgw, sgbt, dy)


def _merge_fwd(proj, yap, ybp, D, name="merge_fwd"):
    T = proj.shape[0]
    tt = _tile(T, 256, 16)

    def body(ga_ref, gb_ref, a_ref, b_ref, o_ref):
        ga, gb, a, b = [r[...].astype(F32) for r in (ga_ref, gb_ref, a_ref, b_ref)]
        o_ref[...] = (_sig(ga) * a + _sig(gb) * b).astype(o_ref.dtype)

    row = pl.BlockSpec((tt, D), lambda i: (i, 0))
    return pl.pallas_call(
        body, name=name, grid=(T // tt,),
        in_specs=[pl.BlockSpec((tt, D), lambda i: (i, OGA // D)), pl.BlockSpec((tt, D), lambda i: (i, OGA // D + 1)),
                  row, row],
        out_specs=row, out_shape=jax.ShapeDtypeStruct((T, D), BF16), compiler_params=_cp("parallel"),
    )(proj, proj, yap, ybp)


def _merge_bwd(proj, yap, ybp, dm, D, name="merge_bwd"):
    T = proj.shape[0]
    tt = _tile(T, 256, 16)

    def body(ga_ref, gb_ref, a_ref, b_ref, dm_ref, da_ref, db_ref, dga_ref, dgb_ref):
        d, ga, gb, a, b = [r[...].astype(F32) for r in (dm_ref, ga_ref, gb_ref, a_ref, b_ref)]
        sa = _sig(ga)
        sb = _sig(gb)
        da_ref[...] = (d * sa).astype(da_ref.dtype)
        db_ref[...] = (d * sb).astype(db_ref.dtype)
        dga_ref[...] = (d * a * sa * (1.0 - sa)).astype(dga_ref.dtype)
        dgb_ref[...] = (d * b * sb * (1.0 - sb)).astype(dgb_ref.dtype)

    row = pl.BlockSpec((tt, D), lambda i: (i, 0))
    return pl.pallas_call(
        body, name=name, grid=(T // tt,),
        in_specs=[pl.BlockSpec((tt, D), lambda i: (i, OGA // D)), pl.BlockSpec((tt, D), lambda i: (i, OGA // D + 1)),
                  row, row, row],
        out_specs=[row] * 4, out_shape=[jax.ShapeDtypeStruct((T, D), BF16)] * 4,
        compiler_params=_cp("parallel"),
    )(proj, proj, yap, ybp, dm)


def _ffn_act_fwd(gp, up, cw, cb, name="ffn_act_fwd"):
    T, F = gp.shape

    def body(g_ref, u_ref, w_ref, b_ref, o_ref):
        gv = g_ref[...].astype(F32)
        w = w_ref[...]
        c = gv * w[FF_K - 1:FF_K, :] + b_ref[...]
        for k in range(1, FF_K):
            c = c + _shift_down(gv, k) * w[FF_K - 1 - k:FF_K - k, :]
        o_ref[...] = (c * _sig(c) * u_ref[...].astype(F32)).astype(o_ref.dtype)

    col = pl.BlockSpec((T, 128), lambda j: (0, j))
    return pl.pallas_call(
        body, name=name, grid=(F // 128,),
        in_specs=[col, col, pl.BlockSpec((FF_K, 128), lambda j: (0, j)), pl.BlockSpec((1, 128), lambda j: (0, j))],
        out_specs=col, out_shape=jax.ShapeDtypeStruct((T, F), BF16), compiler_params=_cp("parallel"),
    )(gp, up, cw, cb)


def _ffn_act_bwd(gp, up, cw, cb, dact, name="ffn_act_bwd"):
    T, F = gp.shape

    def body(g_ref, u_ref, w_ref, b_ref, d_ref, dg_ref, du_ref, dw_ref, db_ref):
        gv = g_ref[...].astype(F32)
        w = w_ref[...]
        shifted = [_shift_down(gv, k) for k in range(FF_K)]
        c = shifted[0] * w[FF_K - 1:FF_K, :] + b_ref[...]
        for k in range(1, FF_K):
            c = c + shifted[k] * w[FF_K - 1 - k:FF_K - k, :]
        d = d_ref[...].astype(F32)
        act, act_grad = _silu_and_grad(c)
        du_ref[...] = (d * act).astype(du_ref.dtype)
        dc = d * u_ref[...].astype(F32) * act_grad
        dg = dc * w[FF_K - 1:FF_K, :]
        for k in range(1, FF_K):
            dg = dg + _shift_up(dc, k) * w[FF_K - 1 - k:FF_K - k, :]
        dg_ref[...] = dg.astype(dg_ref.dtype)
        rows = [jnp.sum(dc * shifted[FF_K - 1 - t], axis=0, keepdims=True) for t in range(FF_K)]
        dw_ref[...] = jnp.concatenate(rows, axis=0)
        db_ref[...] = jnp.sum(dc, axis=0, keepdims=True)

    col = pl.BlockSpec((T, 128), lambda j: (0, j))
    wspec = pl.BlockSpec((FF_K, 128), lambda j: (0, j))
    bspec = pl.BlockSpec((1, 128), lambda j: (0, j))
    return pl.pallas_call(
        body, name=name, grid=(F // 128,),
        in_specs=[col, col, wspec, bspec, col], out_specs=[col, col, wspec, bspec],
        out_shape=[jax.ShapeDtypeStruct((T, F), BF16), jax.ShapeDtypeStruct((T, F), BF16),
                   jax.ShapeDtypeStruct((FF_K, F), F32), jax.ShapeDtypeStruct((1, F), F32)],
        compiler_params=_cp("parallel"),
    )(gp, up, cw, cb, dact)


class _Carrier:
    def __init__(self, plan=None, deliver=None):
        self.plan, self.deliver = plan or (lambda kernel: None), deliver

    def run(self, kernel, fn, **kw):
        comm = self.plan(kernel)
        out = fn(comm=comm, **kw)
        if comm:
            self.deliver(kernel, out[-1])
            out = out[:-1]
            return out[0] if len(out) == 1 else out
        return out


def _layer_fwd(x, w, carrier=None):
    cr = carrier or _Carrier()
    D = x.shape[1]
    oba = OGA + 2 * D
    h = _rms_fwd(x, w["norm1_g"], "rms1_fwd")
    proj = cr.run("proj", functools.partial(_mm, h, w["w_in_t"], "nt", BF16, name="mm_proj"))
    bg = _ba_fwd(proj, w["alog_row"], w["dtb_row"], oba)
    qkv = _dn_prep_fwd(proj, w["dn_conv_w"])
    r = cr.run("dn_core", functools.partial(_dn_core_fwd, qkv, bg))
    o, s_all, tm_all = r[0], r[1], r[2]
    ya = _dn_post_fwd(o, proj, w["dn_onorm_g"])
    yb = _sg_fwd(proj, w["sg_ln_g"], w["sg_ln_b"], w["sg_w"], w["sg_bt"])
    yap = _mm(ya, w["w_branch_a"], "nn", BF16, name="mm_branch")
    ybp = _mm(yb, w["w_branch_b"], "nn", BF16, name="mm_branch")
    merged = _merge_fwd(proj, yap, ybp, D)
    x1 = _mm(merged, w["w_out"], "nn", F32, add=x, name="mm_out")
    h2 = _rms_fwd(x1, w["norm2_g"], "rms2_fwd")
    gp = cr.run("ffn_gate", functools.partial(_mm, h2, w["ffn_w_gate"], "nn", BF16, name="mm_ffn_in"))
    up = cr.run("ffn_up", functools.partial(_mm, h2, w["ffn_w_up"], "nn", BF16, name="mm_ffn_in"))
    act = _ffn_act_fwd(gp, up, w["ffn_conv_w"], w["ffn_conv_b"])
    x2 = cr.run("ffn_down", functools.partial(_mm, act, w["ffn_w_down"], "nn", F32, add=x1, name="mm_ffn_down"))
    saved = dict(x=x, h=h, proj=proj, bg=bg, qkv=qkv, o=o, s_all=s_all, tm_all=tm_all, ya=ya, yb=yb, yap=yap,
                 ybp=ybp, merged=merged, x1=x1, h2=h2, gp=gp, up=up, act=act)
    return x2, saved


def _layer_bwd(dx2, w, s, carrier=None, ffn_grads_ready=None, rest_grads_ready=None):
    cr = carrier or _Carrier()
    D = dx2.shape[1]
    oba = OGA + 2 * D
    g = {}
    dx2b = dx2.astype(BF16)
    dact = cr.run("d_act", functools.partial(_mm, dx2b, w["ffn_w_down"], "nt", BF16, name="mm_d_act"))
    g["ffn_w_down"] = _mm(s["act"], dx2b, "tn", BF16, name="mm_dw_down")
    dgp, dup, g["ffn_conv_w"], g["ffn_conv_b"] = _ffn_act_bwd(s["gp"], s["up"], w["ffn_conv_w"], w["ffn_conv_b"], dact)
    dh2 = _mm(dgp, w["ffn_w_gate"], "nt", F32, name="mm_dh2")
    dh2 = _mm(dup, w["ffn_w_up"], "nt", F32, add=dh2, name="mm_dh2_acc")
    g["ffn_w_gate"] = _mm(s["h2"], dgp, "tn", BF16, name="mm_dw_ffn_in")
    g["ffn_w_up"] = _mm(s["h2"], dup, "tn", BF16, name="mm_dw_ffn_in")
    if ffn_grads_ready:
        ffn_grads_ready(g)
    dx1, g["norm2_g"] = _rms_bwd(s["x1"], w["norm2_g"], dh2, dx2, "rms2_bwd")
    dx1b = dx1.astype(BF16)
    dm = cr.run("d_merged", functools.partial(_mm, dx1b, w["w_out"], "nt", BF16, name="mm_d_merged"))
    g["w_out"] = _mm(s["merged"], dx1b, "tn", BF16, name="mm_dw_out")
    dyap, dybp, dga, dgb = _merge_bwd(s["proj"], s["yap"], s["ybp"], dm, D)
    dya = _mm(dyap, w["w_branch_a"], "nt", F32, name="mm_d_branch")
    dyb = _mm(dybp, w["w_branch_b"], "nt", F32, name="mm_d_branch")
    g["w_branch_a"] = _mm(s["ya"], dyap, "tn", BF16, name="mm_dw_branch")
    g["w_branch_b"] = _mm(s["yb"], dybp, "tn", BF16, name="mm_dw_branch")
    du, dv, g["sg_w"], dbt, g["sg_ln_g"], g["sg_ln_b"] = _sg_bwd(
        s["proj"], w["sg_ln_g"], w["sg_ln_b"], w["sg_w"], w["sg_bt"], dyb)
    g["sg_b"] = jnp.transpose(dbt[:, :H])
    do, dz, g["dn_onorm_g"] = _dn_post_bwd(s["o"], s["proj"], w["dn_onorm_g"], dya)
    r = cr.run("dn_core", functools.partial(_dn_core_bwd, s["qkv"], s["bg"], s["s_all"], s["tm_all"], do))
    dq, dk, dvv, dbg = r[0], r[1], r[2], r[3]
    dqkv, g["dn_conv_w"] = _dn_prep_bwd(s["proj"], w["dn_conv_w"], dq, dk, dvv)
    dba, dal, ddt = _ba_bwd(s["proj"], w["alog_row"], w["dtb_row"], dbg, oba)
    g["dn_a_log"] = dal[0, H:2 * H]
    g["dn_dt_bias"] = ddt[0, H:2 * H]
    dproj = jnp.concatenate([dqkv, dz, du, dv, dga, dgb, dba], axis=1)
    g["w_in_t"] = cr.run("dw_in", functools.partial(_mm, dproj, s["h"], "tn", BF16, name="mm_dw_in"))
    if rest_grads_ready:
        rest_grads_ready(g)
    dh = cr.run("dh", functools.partial(_mm, dproj, w["w_in_t"], "nn", F32, name="mm_dh"))
    dx, g["norm1_g"] = _rms_bwd(s["x"], w["norm1_g"], dh, dx1, "rms1_bwd")
    return dx, g


def _row128(v, off):
    return jnp.pad(v, (off, 128 - off - v.shape[0]))[None]


def _prep_small(p):
    return dict(
        norm1_g=p["norm1_g"][None], alog_row=_row128(p["dn_a_log"], H), dtb_row=_row128(p["dn_dt_bias"], H),
        dn_conv_w=p["dn_conv_w"], dn_onorm_g=p["dn_onorm_g"][None],
        sg_ln_g=p["sg_ln_g"][None], sg_ln_b=p["sg_ln_b"][None], sg_w=p["sg_w"], sg_bt=jnp.transpose(p["sg_b"]),
        norm2_g=p["norm2_g"][None], ffn_conv_w=p["ffn_conv_w"], ffn_conv_b=p["ffn_conv_b"][None])


HBM_SPEC = pl.BlockSpec(memory_space=pltpu.HBM)


def _coords():
    return lax.axis_index("x"), lax.axis_index("y"), lax.axis_index("c")


def _other_chips(x, y):
    return [(1 - x, y), (x, 1 - y), (1 - x, 1 - y)]


def _remote(src, dst, send_sems, recv_sems, k, dev):
    return pltpu.make_async_remote_copy(src_ref=src, dst_ref=dst, send_sem=send_sems.at[k], recv_sem=recv_sems.at[k],
                                        device_id=dev, device_id_type=MESH)


def _ag_copies(w_refs, o_refs, send_sems, recv_sems):
    x, y, c = _coords()
    me = 2 * x + y
    chips = _other_chips(x, y)

    def ici(k, j, owner):
        chip = chips[j]
        return _remote(w_refs[k].at[c], o_refs[k].at[owner, c], send_sems, recv_sems, 6 * k + j, (chip[0], chip[1], c))

    def d2d(k, j, part):
        owner = 2 * chips[j][0] + chips[j][1]
        return _remote(o_refs[k].at[owner, part], o_refs[k].at[owner, part], send_sems, recv_sems, 6 * k + 3 + j,
                       (x, y, 1 - c))

    n = len(w_refs)
    return me, c, chips, ici, d2d, [(k, j) for k in range(n) for j in range(3)]


def _ag_start(w_refs, o_refs, send_sems, recv_sems):
    me, _, _, ici, _, pairs = _ag_copies(w_refs, o_refs, send_sems, recv_sems)
    for k, j in pairs:
        ici(k, j, me).start()


def _ag_finish(w_refs, o_refs, send_sems, recv_sems):
    me, c, chips, ici, d2d, pairs = _ag_copies(w_refs, o_refs, send_sems, recv_sems)
    for k, j in pairs:
        ici(k, j, 2 * chips[j][0] + chips[j][1]).wait_recv()
        d2d(k, j, c).start()
    for k, j in pairs:
        d2d(k, j, 1 - c).wait_recv()
    for k, j in pairs:
        ici(k, j, me).wait_send()
        d2d(k, j, c).wait_send()


def _ag_specs(ws):
    n = len(ws)
    return dict(out_shape=[jax.ShapeDtypeStruct((N_CHIPS,) + w.shape, w.dtype) for w in ws],
                specs=[HBM_SPEC] * n, sems=[pltpu.SemaphoreType.DMA((6 * n,)), pltpu.SemaphoreType.DMA((6 * n,))])


def _ag_layers(ws):
    n = len(ws)

    def body(*refs):
        _ag_start(refs[:n], refs[n:2 * n], *refs[2 * n:])
        _ag_finish(refs[:n], refs[n:2 * n], *refs[2 * n:])

    sp = _ag_specs(ws)
    return pl.pallas_call(
        body, name="ag_weights", out_shape=sp["out_shape"], in_specs=sp["specs"], out_specs=sp["specs"],
        scratch_shapes=sp["sems"],
    )(*ws)


def _pair_copies(g_refs, b_refs, send_sems, recv_sems):
    x, y, c = _coords()
    return [_remote(g_refs[k].at[i, 1 - c], b_refs[k].at[i], send_sems, recv_sems, N_CHIPS * k + i, (x, y, 1 - c))
            for k in range(len(g_refs)) for i in range(N_CHIPS)]


def _pair_start(g_refs, b_refs, send_sems, recv_sems):
    for cp in _pair_copies(g_refs, b_refs, send_sems, recv_sems):
        cp.start()


def _pair_finish(g_refs, b_refs, send_sems, recv_sems):
    for cp in _pair_copies(g_refs, b_refs, send_sems, recv_sems):
        cp.wait()


def _pair_specs(Gs):
    n = len(Gs)
    return dict(out_shape=[jax.ShapeDtypeStruct((N_CHIPS,) + g.shape[2:], g.dtype) for g in Gs],
                specs=[HBM_SPEC] * n,
                sems=[pltpu.SemaphoreType.DMA((N_CHIPS * n,)), pltpu.SemaphoreType.DMA((N_CHIPS * n,))])


def _rs_pair_exchange(Gs):
    n = len(Gs)

    def body(*refs):
        _pair_start(refs[:n], refs[n:2 * n], *refs[2 * n:])
        _pair_finish(refs[:n], refs[n:2 * n], *refs[2 * n:])

    sp = _pair_specs(Gs)
    return pl.pallas_call(
        body, name="rs_pair_exchange", out_shape=sp["out_shape"], in_specs=sp["specs"], out_specs=sp["specs"],
        scratch_shapes=sp["sems"],
    )(*Gs)


def _rs_add_pair(G, B, c, name):
    _, _, R, C = G.shape
    tr = _tile(R, 256, 16)

    def body(c_ref, g_ref, b_ref, o_ref):
        o_ref[0] = (g_ref[0, 0].astype(F32) + b_ref[0].astype(F32)).astype(o_ref.dtype)

    grid_spec = pltpu.PrefetchScalarGridSpec(
        num_scalar_prefetch=1, grid=(N_CHIPS, R // tr),
        in_specs=[pl.BlockSpec((1, 1, tr, C), lambda i, r, c_ref: (i, c_ref[0], r, 0)),
                  pl.BlockSpec((1, tr, C), lambda i, r, c_ref: (i, r, 0))],
        out_specs=pl.BlockSpec((1, tr, C), lambda i, r, c_ref: (i, r, 0)))
    return pl.pallas_call(
        body, name=name, grid_spec=grid_spec, out_shape=jax.ShapeDtypeStruct((N_CHIPS, R, C), G.dtype),
        compiler_params=_cp("parallel", "parallel"),
    )(jnp.reshape(c, (1,)).astype(jnp.int32), G, B)


def _rsx_copies(p_refs, b_refs, send_sems, recv_sems):
    x, y, c = _coords()
    me = 2 * x + y
    chips = _other_chips(x, y)

    def cp(k, j, src_slot, dst_slot):
        return _remote(p_refs[k].at[src_slot], b_refs[k].at[dst_slot], send_sems, recv_sems, 3 * k + j,
                       (chips[j][0], chips[j][1], c))

    return me, chips, cp, [(k, j) for k in range(len(p_refs)) for j in range(3)]


def _rsx_start(p_refs, b_refs, send_sems, recv_sems):
    me, chips, cp, pairs = _rsx_copies(p_refs, b_refs, send_sems, recv_sems)
    for k, j in pairs:
        cp(k, j, 2 * chips[j][0] + chips[j][1], me).start()


def _rsx_finish(p_refs, b_refs, send_sems, recv_sems):
    me, chips, cp, pairs = _rsx_copies(p_refs, b_refs, send_sems, recv_sems)
    for k, j in pairs:
        owner = 2 * chips[j][0] + chips[j][1]
        cp(k, j, owner, owner).wait_recv()
    for k, j in pairs:
        cp(k, j, 2 * chips[j][0] + chips[j][1], me).wait_send()


def _rsx_specs(Ps):
    n = len(Ps)
    return dict(out_shape=[jax.ShapeDtypeStruct(p.shape, p.dtype) for p in Ps], specs=[HBM_SPEC] * n,
                sems=[pltpu.SemaphoreType.DMA((3 * n,)), pltpu.SemaphoreType.DMA((3 * n,))])


def _rs_sum_chips(P, B, me, name):
    _, R, C = P.shape
    tr = _tile(R, 256, 16)

    def body(me_ref, p_ref, b1_ref, b2_ref, b3_ref, o_ref):
        o_ref[...] = ((p_ref[0].astype(F32) + b1_ref[0].astype(F32)) + b2_ref[0].astype(F32)) + b3_ref[0].astype(F32)

    slot = lambda d: pl.BlockSpec((1, tr, C), lambda r, me_ref: ((me_ref[0] + d) % N_CHIPS, r, 0))
    grid_spec = pltpu.PrefetchScalarGridSpec(
        num_scalar_prefetch=1, grid=(R // tr,), in_specs=[slot(0), slot(1), slot(2), slot(3)],
        out_specs=pl.BlockSpec((tr, C), lambda r, me_ref: (r, 0)))
    return pl.pallas_call(
        body, name=name, grid_spec=grid_spec, out_shape=jax.ShapeDtypeStruct((R, C), F32),
        compiler_params=_cp("parallel"),
    )(jnp.reshape(me, (1,)).astype(jnp.int32), P, B, B, B)


def _sum_slots(B, name):
    S, R, C = B.shape
    tr = _tile(R, 256, 16)

    def body(b_ref, o_ref):
        acc = b_ref[0].astype(F32)
        for i in range(1, S):
            acc = acc + b_ref[i].astype(F32)
        o_ref[...] = acc

    return pl.pallas_call(
        body, name=name, grid=(R // tr,), in_specs=[pl.BlockSpec((S, tr, C), lambda r: (0, r, 0))],
        out_specs=pl.BlockSpec((tr, C), lambda r: (r, 0)), out_shape=jax.ShapeDtypeStruct((R, C), F32),
        compiler_params=_cp("parallel"),
    )(B)


def _rs_pair_swap(Rs):
    n = len(Rs)

    def body(*refs):
        r_refs, o_refs = refs[:n], refs[n:2 * n]
        send_sems, recv_sems = refs[2 * n:]
        x, y, c = _coords()
        cps = [_remote(r_refs[k], o_refs[k], send_sems, recv_sems, k, (x, y, 1 - c)) for k in range(n)]
        for cp in cps:
            cp.start()
        for cp in cps:
            cp.wait()

    return pl.pallas_call(
        body, name="rs_pair_swap", out_shape=[jax.ShapeDtypeStruct(r.shape, r.dtype) for r in Rs],
        in_specs=[HBM_SPEC] * n, out_specs=[HBM_SPEC] * n,
        scratch_shapes=[pltpu.SemaphoreType.DMA((n,)), pltpu.SemaphoreType.DMA((n,))],
    )(*Rs)


def _ag8(v):
    R = v.shape[0]

    def body(v_ref, out_ref, send_sems, recv_sems, local_sem):
        x, y, c = _coords()
        me, sib = (x, y, c), (x, y, 1 - c)
        chips = _other_chips(x, y)

        def slot(p):
            return out_ref.at[4 * p[0] + 2 * p[1] + p[2]]

        def copy(k, block, to, src=None):
            return _remote(slot(block) if src is None else src, slot(block), send_sems, recv_sems, k, to)

        mine = pltpu.make_async_copy(v_ref, slot(me), local_sem)
        mine.start()
        first = [copy(0, me, sib, src=v_ref)]
        first += [copy(1 + j, me, (chip[0], chip[1], c), src=v_ref) for j, chip in enumerate(chips)]
        for cp in first:
            cp.start()
        passed = [copy(4 + j, (chip[0], chip[1], c), sib) for j, chip in enumerate(chips)]
        for j, chip in enumerate(chips):
            copy(1 + j, (chip[0], chip[1], c), me).wait_recv()
            passed[j].start()
        copy(0, sib, me).wait_recv()
        for j, chip in enumerate(chips):
            copy(4 + j, (chip[0], chip[1], 1 - c), me).wait_recv()
        for cp in first + passed:
            cp.wait_send()
        mine.wait()

    return pl.pallas_call(
        body, name="ag8_small", out_shape=jax.ShapeDtypeStruct((8, R, 128), v.dtype),
        in_specs=[pl.BlockSpec(memory_space=pltpu.VMEM)], out_specs=pl.BlockSpec(memory_space=pltpu.VMEM),
        scratch_shapes=[pltpu.SemaphoreType.DMA((7,)), pltpu.SemaphoreType.DMA((7,)), pltpu.SemaphoreType.DMA],
        compiler_params=pltpu.CompilerParams(vmem_limit_bytes=VMEM_LIMIT),
    )(v)


def _adamw(w, g, m, v, name):
    L, R, C = w.shape
    rows = [R] + [t for t in range(8, min(R, 1024) + 1, 8) if R % t == 0]
    cols = [C] + [t for t in range(128, C, 128) if C % t == 0]
    lead = [t for t in range(1, L + 1) if L % t == 0]
    fits = [(a * r * c, c, r, a) for a in lead for r in rows for c in cols if a * r * c * 4 <= 3 << 19]
    _, tc, tr, tl = max(fits) if fits else (0, min(cols), min(rows), 1)

    def body(w_ref, g_ref, m_ref, v_ref, d_ref, mo_ref, vo_ref):
        gv = g_ref[...]
        m2 = ADAM_B1 * m_ref[...] + (1.0 - ADAM_B1) * gv
        v2 = ADAM_B2 * v_ref[...] + (1.0 - ADAM_B2) * jnp.square(gv)
        m_hat = m2 / (1.0 - ADAM_B1 ** ADAM_STEP)
        v_hat = v2 / (1.0 - ADAM_B2 ** ADAM_STEP)
        d_ref[...] = -ADAM_LR * (m_hat / (jnp.sqrt(v_hat) + ADAM_EPS) + ADAM_WD * w_ref[...])
        mo_ref[...] = m2
        vo_ref[...] = v2

    blk = pl.BlockSpec((tl, tr, tc), lambda l, r, j: (l, r, j))
    return pl.pallas_call(
        body, name=name, grid=(L // tl, R // tr, C // tc), in_specs=[blk] * 4, out_specs=[blk] * 3,
        out_shape=[jax.ShapeDtypeStruct(w.shape, F32)] * 3,
        compiler_params=_cp("parallel", "parallel", "parallel"),
    )(w, g, m, v)


def _adamw_halves(w, g_mine, g_other, c, m, v, name):
    L, _, R, C = w.shape
    tr = _tile(R, 128, 8)

    def body(c_ref, w_ref, *rest):
        g_refs = rest[:2 * L]
        m_ref, v_ref, g_ref, d_ref, mo_ref, vo_ref = rest[2 * L:]
        l, h = pl.program_id(0), pl.program_id(1)
        gm, go = g_refs[0][...], g_refs[L][...]
        for i in range(1, L):
            gm = jnp.where(l == i, g_refs[i][...], gm)
            go = jnp.where(l == i, g_refs[L + i][...], go)
        gv = jnp.where(h == c_ref[0], gm, go)[None, None]
        g_ref[...] = gv
        m2 = ADAM_B1 * m_ref[...] + (1.0 - ADAM_B1) * gv
        v2 = ADAM_B2 * v_ref[...] + (1.0 - ADAM_B2) * jnp.square(gv)
        m_hat = m2 / (1.0 - ADAM_B1 ** ADAM_STEP)
        v_hat = v2 / (1.0 - ADAM_B2 ** ADAM_STEP)
        d_ref[...] = -ADAM_LR * (m_hat / (jnp.sqrt(v_hat) + ADAM_EPS) + ADAM_WD * w_ref[...])
        mo_ref[...] = m2
        vo_ref[...] = v2

    blk = pl.BlockSpec((1, 1, tr, C), lambda l, h, r, c_ref: (l, h, r, 0))

    def gblk(i, mine):
        def index(l, h, r, c_ref):
            use = jnp.logical_and(l == i, (h == c_ref[0]) == mine)
            return (jnp.where(use, r, 0), 0)
        return pl.BlockSpec((tr, C), index)

    grid_spec = pltpu.PrefetchScalarGridSpec(
        num_scalar_prefetch=1, grid=(L, 2, R // tr),
        in_specs=[blk] + [gblk(i, True) for i in range(L)] + [gblk(i, False) for i in range(L)] + [blk, blk],
        out_specs=[blk] * 4)
    return pl.pallas_call(
        body, name=name, grid_spec=grid_spec, out_shape=[jax.ShapeDtypeStruct(w.shape, F32)] * 4,
        compiler_params=_cp("parallel", "parallel", "parallel"),
    )(jnp.reshape(c, (1,)).astype(jnp.int32), w, *g_mine, *g_other, m, v)


BIG = ("w_in", "w_branch_a", "w_branch_b", "w_out", "ffn_w_gate", "ffn_w_up", "ffn_w_down")
ROW_SHARDED = ("w_out", "ffn_w_down")
SMALL = ("norm1_g", "dn_conv_w", "dn_a_log", "dn_dt_bias", "dn_onorm_g", "sg_ln_g", "sg_ln_b", "sg_w", "sg_b",
         "norm2_g", "ffn_conv_w", "ffn_conv_b", "final_norm_g")
SMALL_SHARDED = ("dn_conv_w", "ffn_conv_w")


def _pack_rows(arrs, mult):
    flat = jnp.concatenate([jnp.reshape(a, (-1,)) for a in arrs])
    n = flat.shape[0]
    rows = -(-n // (128 * mult)) * mult
    return jnp.reshape(jnp.pad(flat, (0, rows * 128 - n)), (rows, 128))


def _unpack(flat2d, shapes):
    flat = jnp.reshape(flat2d, (-1,))
    out, off = [], 0
    for shp in shapes:
        n = math.prod(shp)
        out.append(jnp.reshape(flat[off:off + n], shp))
        off += n
    return out


def kernel(x, norm1_g, w_in, dn_conv_w, dn_a_log, dn_dt_bias, dn_onorm_g, sg_ln_g, sg_ln_b, sg_w, sg_b, w_branch_a, w_branch_b, w_out, norm2_g, ffn_w_gate, ffn_w_up, ffn_conv_w, ffn_conv_b, ffn_w_down, final_norm_g, loss_target, m_norm1_g, m_w_in, m_dn_conv_w, m_dn_a_log, m_dn_dt_bias, m_dn_onorm_g, m_sg_ln_g, m_sg_ln_b, m_sg_w, m_sg_b, m_w_branch_a, m_w_branch_b, m_w_out, m_norm2_g, m_ffn_w_gate, m_ffn_w_up, m_ffn_conv_w, m_ffn_conv_b, m_ffn_w_down, m_final_norm_g, v_norm1_g, v_w_in, v_dn_conv_w, v_dn_a_log, v_dn_dt_bias, v_dn_onorm_g, v_sg_ln_g, v_sg_ln_b, v_sg_w, v_sg_b, v_w_branch_a, v_w_branch_b, v_w_out, v_norm2_g, v_ffn_w_gate, v_ffn_w_up, v_ffn_conv_w, v_ffn_conv_b, v_ffn_w_down, v_final_norm_g):
    W = dict(norm1_g=norm1_g, w_in=w_in, dn_conv_w=dn_conv_w, dn_a_log=dn_a_log, dn_dt_bias=dn_dt_bias,
             dn_onorm_g=dn_onorm_g, sg_ln_g=sg_ln_g, sg_ln_b=sg_ln_b, sg_w=sg_w, sg_b=sg_b, w_branch_a=w_branch_a,
             w_branch_b=w_branch_b, w_out=w_out, norm2_g=norm2_g, ffn_w_gate=ffn_w_gate, ffn_w_up=ffn_w_up,
             ffn_conv_w=ffn_conv_w, ffn_conv_b=ffn_conv_b, ffn_w_down=ffn_w_down, final_norm_g=final_norm_g)
    M = dict(norm1_g=m_norm1_g, w_in=m_w_in, dn_conv_w=m_dn_conv_w, dn_a_log=m_dn_a_log, dn_dt_bias=m_dn_dt_bias,
             dn_onorm_g=m_dn_onorm_g, sg_ln_g=m_sg_ln_g, sg_ln_b=m_sg_ln_b, sg_w=m_sg_w, sg_b=m_sg_b,
             w_branch_a=m_w_branch_a, w_branch_b=m_w_branch_b, w_out=m_w_out, norm2_g=m_norm2_g,
             ffn_w_gate=m_ffn_w_gate, ffn_w_up=m_ffn_w_up, ffn_conv_w=m_ffn_conv_w, ffn_conv_b=m_ffn_conv_b,
             ffn_w_down=m_ffn_w_down, final_norm_g=m_final_norm_g)
    V = dict(norm1_g=v_norm1_g, w_in=v_w_in, dn_conv_w=v_dn_conv_w, dn_a_log=v_dn_a_log, dn_dt_bias=v_dn_dt_bias,
             dn_onorm_g=v_dn_onorm_g, sg_ln_g=v_sg_ln_g, sg_ln_b=v_sg_ln_b, sg_w=v_sg_w, sg_b=v_sg_b,
             w_branch_a=v_w_branch_a, w_branch_b=v_w_branch_b, w_out=v_w_out, norm2_g=v_norm2_g,
             ffn_w_gate=v_ffn_w_gate, ffn_w_up=v_ffn_w_up, ffn_conv_w=v_ffn_conv_w, ffn_conv_b=v_ffn_conv_b,
             ffn_w_down=v_ffn_w_down, final_norm_g=v_final_norm_g)
    cx, cy, cc = _coords()
    chip = 2 * cx + cy
    L = w_in.shape[0]

    D = w_in.shape[1]
    cs_in = w_in.shape[2]
    c1 = 4 * WD
    ba_chip, ba_off = c1 // cs_in, c1 % cs_in
    assert ba_off + 2 * H <= cs_in
    n_main = N_CHIPS * cs_in - 2 * H
    main_start = [i * cs_in - (2 * H if i > ba_chip else 0) for i in range(N_CHIPS)]
    main_len = [cs_in - (2 * H if i == ba_chip else 0) for i in range(N_CHIPS)]
    tile0 = [16 * (s // 16) for s in main_start]
    shift = [s - t for s, t in zip(main_start, tile0)]
    rp_in = -(-max(sh + ln for sh, ln in zip(shift, main_len)) // 32) * 32
    seg = [tile0[i + 1] - tile0[i] for i in range(N_CHIPS - 1)] + [n_main - tile0[-1]]
    assert all(s + 16 <= rp_in for s in seg[:-1]) and seg[-1] <= rp_in and tile0[-1] + rp_in <= n_main + 128
    my_shift = jnp.asarray(shift, jnp.int32)[chip]

    mine = {n: W[n].astype(BF16) for n in BIG if n != "w_in"}
    wt = jnp.swapaxes(W["w_in"], 1, 2).astype(BF16)
    ba = wt[:, ba_off:ba_off + 2 * H]
    local_row = lax.broadcasted_iota(jnp.int32, (cs_in, 1), 0)
    without_ba = jnp.where(local_row < ba_off, wt, jnp.pad(wt[:, 2 * H:], ((0, 0), (0, 2 * H), (0, 0))))
    mine["w_in"] = lax.dynamic_update_slice(jnp.zeros((L, rp_in, D), BF16),
                                            jnp.where(chip == ba_chip, without_ba, wt), (0, my_shift, 0))
    mine["w_ba"] = jnp.pad(jnp.where(chip == ba_chip, ba, jnp.zeros_like(ba)), ((0, 0), (0, 32 - 2 * H), (0, 0)))

    def halves(a, lead=0):
        return jnp.reshape(a, a.shape[:lead] + (2, a.shape[lead] // 2) + a.shape[lead + 1:])

    first = [(0, "w_in"), (0, "w_ba")]
    my_taps = _pack_rows([W[n] for n in SMALL_SHARDED], 32)
    first_gathered = _ag_layers([halves(mine[n][l]) for l, n in first] + [halves(my_taps)])
    all_taps = jnp.reshape(first_gathered[-1], (N_CHIPS,) + my_taps.shape)
    tap_shards = [_unpack(jnp.where(chip == i, my_taps, all_taps[i]), [W[n].shape for n in SMALL_SHARDED])
                  for i in range(N_CHIPS)]
    taps_full = {n: jnp.concatenate([tap_shards[i][k] for i in range(N_CHIPS)], axis=-1)
                 for k, n in enumerate(SMALL_SHARDED)}

    ops = []
    for l in range(L):
        p = {n: W[n][l] for n in W if n not in ("final_norm_g",) + BIG + SMALL_SHARDED}
        p.update({n: taps_full[n][l] for n in SMALL_SHARDED})
        ops.append(_prep_small(p))

    def weights_landed(items, gathered):
        got = {}
        for (l, n), a in zip(items, gathered):
            a = jnp.reshape(a, (N_CHIPS,) + mine[n].shape[1:])
            got[(l, n)] = [jnp.where(chip == i, mine[n][l], a[i]) for i in range(N_CHIPS)]
        for (l, n), parts in got.items():
            if n == "w_in":
                pieces = [parts[0][:seg[0]]]
                for i in range(1, N_CHIPS):
                    pieces += [parts[i][:16] + parts[i - 1][seg[i - 1]:seg[i - 1] + 16], parts[i][16:seg[i]]]
                ba_rows = got[(l, "w_ba")][ba_chip][:2 * H]
                ops[l]["w_in_t"] = jnp.concatenate(pieces + [ba_rows, jnp.zeros((128 - 2 * H, D), BF16)], axis=0)
            elif n != "w_ba":
                ops[l][n] = jnp.concatenate(parts, axis=0 if n in ROW_SHARDED else 1)

    partial_sums, chip_sums = {}, {}

    grad_slices = {}

    def pair_sums(items, shares):
        for (l, n), b in zip(items, shares):
            partial_sums[(l, n)] = _rs_add_pair(grad_slices[(l, n)], b, cc, "rs_add_pair_" + n)

    def grad_partials(l, names, g, exchange_now=True):
        Gs = []
        for n in names:
            if n == "w_in":
                a = jnp.stack([g["w_in_t"][t:t + rp_in] for t in tile0])
            elif n == "w_ba":
                a = jnp.broadcast_to(g["w_in_t"][n_main:n_main + 32][None], (N_CHIPS, 32, D))
            elif n in ROW_SHARDED:
                a = jnp.reshape(g[n], (N_CHIPS, g[n].shape[0] // N_CHIPS, g[n].shape[1]))
            else:
                a = jnp.moveaxis(jnp.reshape(g[n], (g[n].shape[0], N_CHIPS, g[n].shape[1] // N_CHIPS)), 1, 0)
            grad_slices[(l, n)] = halves(a, 1)
        if exchange_now:
            items = [(l, n) for n in names]
            pair_sums(items, _rs_pair_exchange([grad_slices[i] for i in items]))

    def carrier(l, plan, kind, landed):
        source = {"gather": lambda i: halves(mine[i[1]][i[0]]), "pair": grad_slices.get, "exchange": partial_sums.get}

        def payload(kernel):
            items = plan.get((l, kernel))
            return (kind(kernel), [source[kind(kernel)](i) for i in items]) if items else None

        return _Carrier(payload, lambda kernel, res: landed(kernel)(plan[(l, kernel)], res))

    FFN = ("ffn_w_gate", "ffn_w_up", "ffn_w_down")
    REST = ("w_in", "w_ba", "w_branch_a", "w_branch_b", "w_out")
    fwd_plan = {(0, "proj"): [(0, "w_branch_a"), (0, "w_branch_b"), (0, "w_out"), (0, "ffn_w_gate")],
                (0, "dn_core"): [(0, "ffn_w_up"), (1, "w_in"), (1, "w_ba")],
                (0, "ffn_gate"): [(0, "ffn_w_down")],
                (1, "dn_core"): [(1, "w_branch_a"), (1, "w_branch_b"), (1, "w_out")] + [(1, n) for n in FFN]}
    bwd_plan = {(1, "d_merged"): [(1, n) for n in FFN],
                (0, "d_merged"): [(0, n) for n in FFN],
                (1, "dn_core"): [(1, "ffn_w_gate"), (1, "ffn_w_up")],
                (1, "dw_in"): [(1, "ffn_w_down")],
                (0, "d_act"): [(1, "w_branch_a"), (1, "w_branch_b"), (1, "w_out")],
                (0, "dn_core"): [(1, "w_in"), (1, "w_ba"), (0, "ffn_w_down")],
                (0, "dw_in"): [(0, "ffn_w_gate"), (0, "ffn_w_up")],
                (0, "dh"): [(0, n) for n in REST]}

    def sums_landed(items, res):
        chip_sums.update(zip(items, res))

    weights_landed(first, first_gathered[:-1])
    xs, saved = x[0], []
    for l in range(L):
        xs, s = _layer_fwd(xs, ops[l], carrier(l, fwd_plan, lambda kernel: "gather", lambda kernel: weights_landed))
        saved.append(s)
    dx, dgf, loss = _loss_head(xs, final_norm_g[None], loss_target[0])
    loss = loss[0, 0]
    grads = [None] * L
    for l in reversed(range(L)):
        dx, grads[l] = _layer_bwd(
            dx, ops[l], saved[l],
            carrier(l, bwd_plan, lambda kernel: "pair" if kernel == "d_merged" else "exchange",
                    lambda kernel: pair_sums if kernel == "d_merged" else sums_landed),
            functools.partial(grad_partials, l, FFN, exchange_now=False), functools.partial(grad_partials, l, REST))
    travelled = BIG + ("w_ba",)
    g_mine = [[_rs_sum_chips(partial_sums[(l, n)], chip_sums[(l, n)], chip, "rs_sum_chips_" + n) for n in travelled]
              for l in range(L)]
    swapped = _rs_pair_swap(g_mine[0] + g_mine[1])
    g_other = [swapped[:len(travelled)], swapped[len(travelled):]]

    def both_halves(l, n):
        a, b = g_mine[l][travelled.index(n)], g_other[l][travelled.index(n)]
        return jnp.where(cc == 0, jnp.concatenate([a, b]), jnp.concatenate([b, a]))

    def w_in_grad_rows(l):
        m = lax.dynamic_slice_in_dim(both_halves(l, "w_in"), my_shift, cs_in, axis=0)
        ba_rows = jnp.pad(both_halves(l, "w_ba")[:2 * H], ((ba_off, cs_in - ba_off - 2 * H), (0, 0)))
        moved = jnp.pad(m[:cs_in - 2 * H], ((2 * H, 0), (0, 0)))
        with_ba = jnp.where(local_row < ba_off, m, jnp.where(local_row < ba_off + 2 * H, ba_rows, moved))
        return jnp.where(chip == ba_chip, with_ba, m)

    small = {n: jnp.stack([g[n] for g in grads]) for n in SMALL if n != "final_norm_g"}
    small["final_norm_g"] = dgf
    shapes = [taps_full[n].shape if n in SMALL_SHARDED else W[n].shape for n in SMALL] + [(1,)]
    sflat = _pack_rows([small[n] for n in SMALL] + [jnp.reshape(loss, (1,))], 16)
    sred = _unpack(_sum_slots(_ag8(sflat), "sum_small"), shapes)
    g_small = dict(zip(SMALL, sred[:-1]))
    loss_total = sred[-1][0]
    for n in SMALL_SHARDED:
        cs = W[n].shape[-1]
        g_small[n] = lax.dynamic_slice_in_dim(g_small[n], chip * cs, cs, axis=-1)

    g_big, delta, new_m, new_v = {}, {}, {}, {}
    for k, n in enumerate(BIG):
        gm, go = [g_mine[l][k] for l in range(L)], [g_other[l][k] for l in range(L)]
        if n == "w_in":
            g_t = jnp.stack([w_in_grad_rows(l) for l in range(L)], axis=1)
            outs = _adamw(*[jnp.transpose(a, (2, 0, 1)) for a in (W[n],)], g_t,
                          *[jnp.transpose(a, (2, 0, 1)) for a in (M[n], V[n])], "adamw_" + n)
            g_big[n], delta[n], new_m[n], new_v[n] = [jnp.transpose(o, (1, 2, 0)) for o in (g_t,) + tuple(outs)]
        else:
            outs = _adamw_halves(halves(W[n], 1), gm, go, cc, halves(M[n], 1), halves(V[n], 1), "adamw_" + n)
            g_big[n], delta[n], new_m[n], new_v[n] = [jnp.reshape(o, W[n].shape) for o in outs]
    for n in SMALL:
        shp = W[n].shape
        as3d = (1,) * (3 - len(shp)) + shp if len(shp) <= 3 else (-1,) + shp[-2:]
        outs = _adamw(*[jnp.reshape(d[n], as3d) for d in (W, g_small, M, V)], "adamw_" + n)
        delta[n], new_m[n], new_v[n] = [jnp.reshape(o, shp) for o in outs]

    names = list(W)
    grad_w = {**g_big, **g_small}
    return (loss_total, dx[None], *[grad_w[n] for n in names], *[delta[n] for n in names],
            *[new_m[n] for n in names], *[new_v[n] for n in names])
```

```python
import functools
import math

import jax
import jax.numpy as jnp
from jax import lax
from jax.experimental import pallas as pl
from jax.experimental.pallas import tpu as pltpu

F32 = jnp.float32
BF16 = jnp.bfloat16
MESH = pl.DeviceIdType.MESH

EPS = 1e-6
H = 8
DH = 128
WD = H * DH
DNC = 64
SGC = 128
DN_K = 4
FF_K = 3
DEPTH = 2
N_CHIPS = 4

ADAM_LR = 0.001
ADAM_B1 = 0.9
ADAM_B2 = 0.999
ADAM_EPS = 1e-08
ADAM_WD = 0.01
ADAM_STEP = 10

VMEM_LIMIT = 56 * 1024 * 1024

NN = (((1,), (0,)), ((), ()))
NT = (((1,), (1,)), ((), ()))
TN = (((0,), (0,)), ((), ()))

OQ, OZ, OU, OV, OGA = 0, 3 * WD, 4 * WD, 5 * WD, 6 * WD


def _cp(*sem):
    return pltpu.CompilerParams(dimension_semantics=sem or None, vmem_limit_bytes=VMEM_LIMIT)


def _tile(dim, pref, unit=128):
    if dim <= pref:
        return dim
    t = (pref // unit) * unit
    while t >= unit:
        if dim % t == 0:
            return t
        t -= unit
    return dim


def _bdot(a, b, dn=NN):
    return lax.dot_general(a.astype(BF16), b.astype(BF16), dn, preferred_element_type=F32)


def _lsum(x):
    return jnp.sum(x, axis=1, keepdims=True)


def _sig(x):
    return 0.5 * jnp.tanh(0.5 * x) + 0.5


def _silu_and_grad(x):
    s = _sig(x)
    return x * s, s * (1.0 + x * (1.0 - s))


def _gelu_parts(x):
    a = jnp.abs(x) * (2.0 ** -0.5)
    t = 1.0 / (1.0 + 0.3275911 * a)
    poly = t * (0.254829592 + t * (-0.284496736 + t * (1.421413741 + t * (-1.453152027 + t * 1.061405429))))
    e = jnp.exp(-a * a)
    half = 0.5 * poly * e
    return jnp.where(x < 0, half, 1.0 - half), e * (1.0 / math.sqrt(2.0 * math.pi))


def _gelu(x):
    return x * _gelu_parts(x)[0]


def _gelu_and_grad(x):
    cdf, pdf = _gelu_parts(x)
    return x * cdf, cdf + x * pdf


def _shift_down(x, k):
    if k == 0:
        return x
    y = pltpu.roll(x, k, 0)
    rows = lax.broadcasted_iota(jnp.int32, (8, x.shape[1]), 0)
    return jnp.concatenate([jnp.where(rows >= k, y[:8], 0.0), y[8:]], axis=0)


def _shift_up(x, k):
    if k == 0:
        return x
    n = x.shape[0]
    y = pltpu.roll(x, n - k, 0)
    rows = lax.broadcasted_iota(jnp.int32, (8, x.shape[1]), 0)
    return jnp.concatenate([y[:n - 8], jnp.where(rows < 8 - k, y[n - 8:], 0.0)], axis=0)


def _comm_fns(comm):
    if not comm:
        return None, None, dict(out_shape=[], specs=[], sems=[]), ()
    kind, arrays = comm
    start, finish, specs = {"gather": (_ag_start, _ag_finish, _ag_specs),
                            "exchange": (_rsx_start, _rsx_finish, _rsx_specs),
                            "pair": (_pair_start, _pair_finish, _pair_specs)}[kind]
    return start, finish, specs(arrays), tuple(arrays)


def _mm(a, b, mode, out_dtype, add=None, name="mm", comm=None):
    if mode == "tn":
        K, M = a.shape
    else:
        M, K = a.shape
    N = b.shape[0] if mode == "nt" else b.shape[1]
    tm, tn, tk = _tile(M, 1152), _tile(N, 1536), _tile(K, 3584)
    nk = K // tk
    ni, nj = M // tm, N // tn
    dn = {"nn": NN, "nt": NT, "tn": TN}[mode]
    c_start, c_finish, c_sp, payload = _comm_fns(comm)
    nc = len(payload)
    n_add = 0 if add is None else 1

    def body(*refs):
        a_ref, b_ref = refs[:2]
        add_ref = refs[2] if n_add else None
        c_in = refs[2 + n_add:2 + n_add + nc]
        o_ref = refs[2 + n_add + nc]
        c_out = refs[3 + n_add + nc:3 + n_add + 2 * nc]
        rest = refs[3 + n_add + 2 * nc:]
        acc_ref = rest[0] if nk > 1 else None
        sems = rest[1:] if nk > 1 else rest
        i, j, k = pl.program_id(0), pl.program_id(1), pl.program_id(2)

        if nc:
            @pl.when(jnp.logical_and(jnp.logical_and(i == 0, j == 0), k == 0))
            def _():
                c_start(c_in, c_out, *sems)

        def finish(r):
            if add is not None:
                r = r + add_ref[...]
            o_ref[...] = r.astype(o_ref.dtype)

        part = lax.dot_general(a_ref[...], b_ref[...], dn, preferred_element_type=F32)
        if nk == 1:
            finish(part)
        else:
            @pl.when(k == 0)
            def _():
                acc_ref[...] = part

            @pl.when(k > 0)
            def _():
                acc_ref[...] += part

            @pl.when(k == nk - 1)
            def _():
                finish(acc_ref[...])

        if nc:
            @pl.when(jnp.logical_and(jnp.logical_and(i == ni - 1, j == nj - 1), k == nk - 1))
            def _():
                c_finish(c_in, c_out, *sems)

    a_spec = (pl.BlockSpec((tk, tm), lambda i, j, k: (k, i)) if mode == "tn"
              else pl.BlockSpec((tm, tk), lambda i, j, k: (i, k)))
    b_spec = (pl.BlockSpec((tn, tk), lambda i, j, k: (j, k)) if mode == "nt"
              else pl.BlockSpec((tk, tn), lambda i, j, k: (k, j)))
    o_spec = pl.BlockSpec((tm, tn), lambda i, j, k: (i, j))
    in_specs = [a_spec, b_spec] + ([o_spec] if add is not None else []) + c_sp["specs"]
    args = (a, b) + ((add,) if add is not None else ()) + payload
    outs = pl.pallas_call(
        body, name=name + ("_" + comm[0] if nc else ""), grid=(ni, nj, nk), in_specs=in_specs,
        out_specs=[o_spec] + c_sp["specs"],
        out_shape=[jax.ShapeDtypeStruct((M, N), out_dtype)] + c_sp["out_shape"],
        scratch_shapes=([pltpu.VMEM((tm, tn), F32)] if nk > 1 else []) + c_sp["sems"],
        compiler_params=_cp("arbitrary", "arbitrary", "arbitrary") if nc else _cp("parallel", "parallel", "arbitrary"),
    )(*args)
    return (outs[0], list(outs[1:])) if nc else outs[0]


def _rms_fwd(x, g, name):
    T, D = x.shape
    tt = _tile(T, 256, 16)

    def body(x_ref, g_ref, o_ref):
        xv = x_ref[...]
        r = lax.rsqrt(jnp.mean(xv * xv, axis=-1, keepdims=True) + EPS)
        o_ref[...] = (xv * r * g_ref[...]).astype(o_ref.dtype)

    return pl.pallas_call(
        body, name=name, grid=(T // tt,),
        in_specs=[pl.BlockSpec((tt, D), lambda i: (i, 0)), pl.BlockSpec((1, D), lambda i: (0, 0))],
        out_specs=pl.BlockSpec((tt, D), lambda i: (i, 0)),
        out_shape=jax.ShapeDtypeStruct((T, D), BF16), compiler_params=_cp("parallel"),
    )(x, g)


def _rms_bwd(x, g, dh, dres, name):
    T, D = x.shape
    tt = _tile(T, 256, 16)

    def body(x_ref, g_ref, dh_ref, dres_ref, dx_ref, dg_ref):
        @pl.when(pl.program_id(0) == 0)
        def _():
            dg_ref[...] = jnp.zeros_like(dg_ref)

        xv = x_ref[...]
        r = lax.rsqrt(jnp.mean(xv * xv, axis=-1, keepdims=True) + EPS)
        xh = xv * r
        dh_v = dh_ref[...]
        dy = dh_v * g_ref[...]
        dx_ref[...] = dres_ref[...] + r * (dy - xh * jnp.mean(dy * xh, axis=-1, keepdims=True))
        dg_ref[...] += jnp.sum(dh_v * xh, axis=0, keepdims=True)

    row = pl.BlockSpec((tt, D), lambda i: (i, 0))
    vec = pl.BlockSpec((1, D), lambda i: (0, 0))
    return pl.pallas_call(
        body, name=name, grid=(T // tt,), in_specs=[row, vec, row, row], out_specs=[row, vec],
        out_shape=[jax.ShapeDtypeStruct((T, D), F32), jax.ShapeDtypeStruct((1, D), F32)],
        compiler_params=_cp("arbitrary"),
    )(x, g, dh, dres)


def _loss_head(x, g, tgt, name="loss_head"):
    T, D = x.shape
    tt = _tile(T, 256, 16)

    def body(x_ref, g_ref, t_ref, dx_ref, dg_ref, loss_ref):
        @pl.when(pl.program_id(0) == 0)
        def _():
            dg_ref[...] = jnp.zeros_like(dg_ref)
            loss_ref[...] = jnp.zeros_like(loss_ref)

        xv = x_ref[...]
        r = lax.rsqrt(jnp.mean(xv * xv, axis=-1, keepdims=True) + EPS)
        xh = xv * r
        err = xh * g_ref[...] - t_ref[...]
        part = 0.5 * jnp.sum(jnp.mean(err * err, axis=-1, keepdims=True), axis=0, keepdims=True)
        loss_ref[...] += jnp.broadcast_to(part, loss_ref.shape)
        dy = err * (1.0 / D)
        dg_ref[...] += jnp.sum(dy * xh, axis=0, keepdims=True)
        dyh = dy * g_ref[...]
        dx_ref[...] = r * (dyh - xh * jnp.mean(dyh * xh, axis=-1, keepdims=True))

    row = pl.BlockSpec((tt, D), lambda i: (i, 0))
    vec = pl.BlockSpec((1, D), lambda i: (0, 0))
    return pl.pallas_call(
        body, name=name, grid=(T // tt,), in_specs=[row, vec, row],
        out_specs=[row, vec, pl.BlockSpec((1, 128), lambda i: (0, 0))],
        out_shape=[jax.ShapeDtypeStruct((T, D), F32), jax.ShapeDtypeStruct((1, D), F32),
                   jax.ShapeDtypeStruct((1, 128), F32)],
        compiler_params=_cp("arbitrary"),
    )(x, g, tgt)


def _ba_fwd(proj, alog, dtb, oba, name="dn_ba_fwd"):
    T = proj.shape[0]
    tt = _tile(T, 512, 8)

    def body(p_ref, al_ref, dt_ref, o_ref):
        raw = p_ref[...].astype(F32)
        lane = lax.broadcasted_iota(jnp.int32, raw.shape, 1)
        z = raw + dt_ref[...]
        sp = jnp.maximum(z, 0.0) + jnp.log(1.0 + jnp.exp(-jnp.abs(z)))
        gl = -jnp.exp(al_ref[...]) * sp
        o_ref[...] = jnp.where(lane < H, _sig(raw), jnp.where(lane < 2 * H, gl, 0.0))

    vec = pl.BlockSpec((1, 128), lambda i: (0, 0))
    return pl.pallas_call(
        body, name=name, grid=(T // tt,),
        in_specs=[pl.BlockSpec((tt, 128), lambda i: (i, oba // 128)), vec, vec],
        out_specs=pl.BlockSpec((tt, 128), lambda i: (i, 0)),
        out_shape=jax.ShapeDtypeStruct((T, 128), F32), compiler_params=_cp("parallel"),
    )(proj, alog, dtb)


def _ba_bwd(proj, alog, dtb, dbg, oba, name="dn_ba_bwd"):
    T = proj.shape[0]
    tt = _tile(T, 512, 16)

    def body(p_ref, al_ref, dt_ref, d_ref, o_ref, dal_ref, ddt_ref):
        @pl.when(pl.program_id(0) == 0)
        def _():
            dal_ref[...] = jnp.zeros_like(dal_ref)
            ddt_ref[...] = jnp.zeros_like(ddt_ref)

        raw = p_ref[...].astype(F32)
        d = d_ref[...]
        lane = lax.broadcasted_iota(jnp.int32, raw.shape, 1)
        z = raw + dt_ref[...]
        sp = jnp.maximum(z, 0.0) + jnp.log(1.0 + jnp.exp(-jnp.abs(z)))
        na = -jnp.exp(al_ref[...])
        is_g = jnp.logical_and(lane >= H, lane < 2 * H)
        b = _sig(raw)
        dz = jnp.where(is_g, d * na * _sig(z), 0.0)
        o_ref[...] = jnp.where(lane < H, d * b * (1.0 - b), dz).astype(o_ref.dtype)
        dal_ref[...] += jnp.sum(jnp.where(is_g, d * na * sp, 0.0), axis=0, keepdims=True)
        ddt_ref[...] += jnp.sum(dz, axis=0, keepdims=True)

    vec = pl.BlockSpec((1, 128), lambda i: (0, 0))
    return pl.pallas_call(
        body, name=name, grid=(T // tt,),
        in_specs=[pl.BlockSpec((tt, 128), lambda i: (i, oba // 128)), vec, vec,
                  pl.BlockSpec((tt, 128), lambda i: (i, 0))],
        out_specs=[pl.BlockSpec((tt, 128), lambda i: (i, 0)), vec, vec],
        out_shape=[jax.ShapeDtypeStruct((T, 128), BF16), jax.ShapeDtypeStruct((1, 128), F32),
                   jax.ShapeDtypeStruct((1, 128), F32)],
        compiler_params=_cp("arbitrary"),
    )(proj, alog, dtb, dbg)


def _dn_prep_fwd(proj, convw, name="dn_prep_fwd"):
    T = proj.shape[0]
    nblk = 3 * H

    def body(p_ref, w_ref, o_ref):
        j = pl.program_id(0)
        xv = p_ref[...].astype(F32)
        w = w_ref[...]
        c = xv * w[DN_K - 1:DN_K, :]
        for k in range(1, DN_K):
            c = c + _shift_down(xv, k) * w[DN_K - 1 - k:DN_K - k, :]
        s = c * _sig(c)
        r = lax.rsqrt(_lsum(s * s) + EPS)
        o_ref[...] = jnp.where(j < 2 * H, s * r, s)

    return pl.pallas_call(
        body, name=name, grid=(nblk,),
        in_specs=[pl.BlockSpec((T, DH), lambda j: (0, j)), pl.BlockSpec((DN_K, DH), lambda j: (0, j))],
        out_specs=pl.BlockSpec((T, DH), lambda j: (0, j)),
        out_shape=jax.ShapeDtypeStruct((T, 3 * WD), F32), compiler_params=_cp("parallel"),
    )(proj, convw)


def _dn_prep_bwd(proj, convw, dq, dk, dv, name="dn_prep_bwd"):
    T = proj.shape[0]
    nblk = 3 * H

    def body(p_ref, w_ref, dq_ref, dk_ref, dv_ref, dx_ref, dw_ref):
        j = pl.program_id(0)
        xv = p_ref[...].astype(F32)
        w = w_ref[...]
        shifted = [_shift_down(xv, k) for k in range(DN_K)]
        c = shifted[0] * w[DN_K - 1:DN_K, :]
        for k in range(1, DN_K):
            c = c + shifted[k] * w[DN_K - 1 - k:DN_K - k, :]
        s, s_grad = _silu_and_grad(c)
        r = lax.rsqrt(_lsum(s * s) + EPS)
        y = s * r
        dy = jnp.where(j < H, dq_ref[...], jnp.where(j < 2 * H, dk_ref[...], dv_ref[...]))
        ds = jnp.where(j < 2 * H, r * (dy - y * _lsum(dy * y)), dy)
        dc = ds * s_grad
        dx = dc * w[DN_K - 1:DN_K, :]
        for k in range(1, DN_K):
            dx = dx + _shift_up(dc, k) * w[DN_K - 1 - k:DN_K - k, :]
        dx_ref[...] = dx.astype(dx_ref.dtype)
        rows = [jnp.sum(dc * shifted[DN_K - 1 - t], axis=0, keepdims=True) for t in range(DN_K)]
        dw_ref[...] = jnp.concatenate(rows, axis=0)

    hb = lambda off: pl.BlockSpec((T, DH), lambda j: (0, jnp.maximum(jnp.minimum(j - off, H - 1), 0)))
    return pl.pallas_call(
        body, name=name, grid=(nblk,),
        in_specs=[pl.BlockSpec((T, DH), lambda j: (0, j)), pl.BlockSpec((DN_K, DH), lambda j: (0, j)),
                  hb(0), hb(H), hb(2 * H)],
        out_specs=[pl.BlockSpec((T, DH), lambda j: (0, j)), pl.BlockSpec((DN_K, DH), lambda j: (0, j))],
        out_shape=[jax.ShapeDtypeStruct((T, 3 * WD), BF16), jax.ShapeDtypeStruct((DN_K, 3 * WD), F32)],
        compiler_params=_cp("parallel"),
    )(proj, convw, dq, dk, dv)


DN_BLOCK = 4


def _split3(a):
    hi = a.astype(BF16)
    r1 = a - hi.astype(F32)
    mid = r1.astype(BF16)
    return hi, mid, (r1 - mid.astype(F32)).astype(BF16)


def _dot3(a, b, dn=NN):
    ah, al, _ = _split3(a)
    bh, bl, _ = _split3(b)
    d = lambda p, q: lax.dot_general(p, q, dn, preferred_element_type=F32)
    return d(ah, bh) + d(ah, bl) + d(al, bh)


def _mask_dot(m, b, dn=NN):
    mb = m.astype(BF16)
    d = lambda q: (lax.dot_general(mb, q, dn, preferred_element_type=F32) if dn != TN
                   else lax.dot_general(q, mb, dn, preferred_element_type=F32))
    b0, b1, b2 = _split3(b)
    return d(b0) + d(b1) + d(b2)


def _tri_inv(A):
    ri = lax.broadcasted_iota(jnp.int32, A.shape, 0)
    ci = lax.broadcasted_iota(jnp.int32, A.shape, 1)
    T = jnp.where(ri == ci, 1.0, 0.0) - jnp.where((ri // 2) == (ci // 2), A, 0.0)
    s = 2
    while s < DNC:
        off = jnp.logical_and((ri // (2 * s)) == (ci // (2 * s)), (ri // s) != (ci // s))
        T = T - _dot3(_dot3(T, jnp.where(off, A, 0.0)), T)
        s *= 2
    return T


GH = 4
NG = H // GH
GR = GH * DNC
GK = GH * DH


def _dn_masks():
    ri = lax.broadcasted_iota(jnp.int32, (GR, GR), 0)
    ci = lax.broadcasted_iota(jnp.int32, (GR, GR), 1)
    blk = (ri // DNC) == (ci // DNC)
    wide = (lax.broadcasted_iota(jnp.int32, (GR, GK), 0) // DNC) == (lax.broadcasted_iota(jnp.int32, (GR, GK), 1) // DH)
    return dict(blk=blk, causal=jnp.logical_and(blk, ri >= ci), strict=jnp.logical_and(blk, ri > ci),
                upper=jnp.logical_and(blk, ri <= ci), eye=ri == ci, wide=wide)


def _wide(a, mk):
    return jnp.where(mk["wide"], jnp.tile(a, (1, GH)), 0.0)


def _fold(a, mk):
    a = jnp.where(mk["wide"], a, 0.0)
    out = a[:, :DH]
    for j in range(1, GH):
        out = out + a[:, j * DH:(j + 1) * DH]
    return out


def _stack_heads(ref, rows, g):
    return jnp.concatenate([ref[rows, (g * GH + j) * DH:(g * GH + j + 1) * DH] for j in range(GH)], axis=0)


def _dn_group(q_ref, k_ref, v_ref, rows, bg, gc_cols, g, mk):
    heads = [g * GH + j for j in range(GH)]
    col = lambda a, lane: jnp.concatenate([a[:, lane(h):lane(h) + 1] for h in heads], axis=0)
    q = _stack_heads(q_ref, rows, g) * (DH ** -0.5)
    k = _stack_heads(k_ref, rows, g)
    v = _stack_heads(v_ref, rows, g)
    beta = col(bg, lambda h: h)
    gcol = col(gc_cols, lambda h: H + h)
    last = [gc_cols[DNC - 1:DNC, H + h:H + h + 1] for h in heads]
    gl = jnp.concatenate([jnp.broadcast_to(t, (DNC, 1)) for t in last], axis=0)
    egl_state = jnp.concatenate([jnp.broadcast_to(jnp.exp(t), (DH, 1)) for t in last], axis=0)
    grow = _mask_dot(jnp.ones((GR, GR), F32), jnp.where(mk["eye"], gcol, 0.0))
    dec = jnp.where(mk["causal"], jnp.exp(jnp.where(mk["causal"], gcol - grow, 0.0)), 0.0)
    eg = jnp.exp(gcol)
    ek = jnp.exp(gl - gcol)
    kb = k * beta
    vb = v * beta
    kbe = kb * eg
    A = jnp.where(mk["strict"], _bdot(kb, k, NT) * dec, 0.0)
    P = jnp.where(mk["causal"], _bdot(q, k, NT) * dec, 0.0)
    return dict(q=q, k=k, v=v, beta=beta, dec=dec, eg=eg, ek=ek, egl=jnp.exp(gl), egl_state=egl_state, kb=kb, vb=vb,
                kbe=kbe, A=A, P=P, qd=q * eg, kd=k * ek, heads=heads)


def _gc_cols(bg):
    ri = lax.broadcasted_iota(jnp.int32, (DNC, DNC), 0)
    ci = lax.broadcasted_iota(jnp.int32, (DNC, DNC), 1)
    return _mask_dot(jnp.where(ri >= ci, 1.0, 0.0), bg)


def _dn_core_fwd(qkv, bg, comm=None, name="dn_core_fwd"):
    c_start, c_finish, sp, gather = _comm_fns(comm)
    T = qkv.shape[0]
    n_chunks = T // DNC
    nb = _tile(n_chunks, DN_BLOCK, 1)
    tb = nb * DNC

    ng = len(gather)
    n_steps = n_chunks // nb

    def body(*refs):
        q_ref, k_ref, v_ref, bg_ref = refs[:4]
        o_ref, s_ref, tm_ref = refs[4 + ng:7 + ng]
        S_scr = refs[7 + 2 * ng]
        comm_refs = (refs[4:4 + ng], refs[7 + ng:7 + 2 * ng]) + tuple(refs[8 + 2 * ng:])

        @pl.when(pl.program_id(0) == 0)
        def _():
            S_scr[...] = jnp.zeros_like(S_scr)
            if ng:
                c_start(*comm_refs)

        mk = _dn_masks()

        def chunk(n):
            rows = pl.ds(n * DNC, DNC)
            bgc = bg_ref[rows, :]
            gc_cols = _gc_cols(bgc)
            for g in range(NG):
                c = _dn_group(q_ref, k_ref, v_ref, rows, bgc, gc_cols, g, mk)
                Tm = _tri_inv(c["A"])
                tm_ref[n, g] = Tm
                S = S_scr[g]
                s_ref[n, g] = S
                u = _bdot(Tm, c["vb"])
                w = _bdot(Tm, c["kbe"])
                vn = u - _bdot(_wide(w, mk), S)
                o = _bdot(_wide(c["qd"], mk), S) + _bdot(c["P"], vn)
                for j, h in enumerate(c["heads"]):
                    o_ref[rows, h * DH:(h + 1) * DH] = o[j * DNC:(j + 1) * DNC]
                S_scr[g] = S * c["egl_state"] + _bdot(_wide(c["kd"], mk), vn, TN)

        for n in range(nb):
            chunk(n)

        if ng:
            @pl.when(pl.program_id(0) == n_steps - 1)
            def _():
                c_finish(*comm_refs)

    blk = lambda j: pl.BlockSpec((tb, WD), lambda i: (i, j))
    outs = pl.pallas_call(
        body, name=name + ("_" + comm[0] if ng else ""), grid=(n_steps,),
        in_specs=[blk(0), blk(1), blk(2), pl.BlockSpec((tb, 128), lambda i: (i, 0))] + sp["specs"],
        out_specs=[blk(0), pl.BlockSpec((nb, NG, GK, DH), lambda i: (i, 0, 0, 0)),
                   pl.BlockSpec((nb, NG, GR, GR), lambda i: (i, 0, 0, 0))] + sp["specs"],
        out_shape=[jax.ShapeDtypeStruct((T, WD), F32), jax.ShapeDtypeStruct((n_chunks, NG, GK, DH), F32),
                   jax.ShapeDtypeStruct((n_chunks, NG, GR, GR), F32)] + sp["out_shape"],
        scratch_shapes=[pltpu.VMEM((NG, GK, DH), F32)] + (sp["sems"] if ng else []),
        compiler_params=_cp("arbitrary"),
    )(qkv, qkv, qkv, bg, *gather)
    return outs[0], outs[1], outs[2], list(outs[3:])


def _dn_core_bwd(qkv, bg, s_all, tm_all, do, comm=None, name="dn_core_bwd"):
    c_start, c_finish, sp, exchange = _comm_fns(comm)
    T = qkv.shape[0]
    n_chunks = T // DNC
    nb = _tile(n_chunks, DN_BLOCK, 1)
    tb = nb * DNC
    n_blocks = n_chunks // nb

    nx = len(exchange)

    def body(*refs):
        q_ref, k_ref, v_ref, bg_ref, s_ref, tm_ref, do_ref = refs[:7]
        dq_ref, dk_ref, dv_ref, dbg_ref = refs[7 + nx:11 + nx]
        dS_scr = refs[11 + 2 * nx]
        comm_refs = (refs[7:7 + nx], refs[11 + nx:11 + 2 * nx]) + tuple(refs[12 + 2 * nx:])

        @pl.when(pl.program_id(0) == 0)
        def _():
            dS_scr[...] = jnp.zeros_like(dS_scr)
            if nx:
                c_start(*comm_refs)

        lane = lax.broadcasted_iota(jnp.int32, (DNC, 128), 1)
        row = lax.broadcasted_iota(jnp.int32, (GR, 1), 0)

        mk = _dn_masks()

        def chunk(n):
            rows = pl.ds(n * DNC, DNC)
            ones = jnp.ones((GR, GR), F32)
            blk_f = jnp.where(mk["blk"], 1.0, 0.0)
            wide_f = jnp.where(mk["wide"], 1.0, 0.0)
            per_row = lambda m, a: _mask_dot(m, jnp.broadcast_to(a, (a.shape[0], DH)))[:, :1]
            bgc = bg_ref[rows, :]
            gc_cols = _gc_cols(bgc)
            dbg = jnp.zeros((DNC, 128), F32)
            for g in range(NG):
                c = _dn_group(q_ref, k_ref, v_ref, rows, bgc, gc_cols, g, mk)
                q, k, v, beta = c["q"], c["k"], c["v"], c["beta"]
                dec, eg, ek, egl = c["dec"], c["eg"], c["ek"], c["egl"]
                kb, vb, kbe, A, P, qd, kd = c["kb"], c["vb"], c["kbe"], c["A"], c["P"], c["qd"], c["kd"]
                S = s_ref[n, g]
                Tm = tm_ref[n, g]
                u = _bdot(Tm, vb)
                w = _bdot(Tm, kbe)
                w_wide = _wide(w, mk)
                vn = u - _bdot(w_wide, S)
                d_o = _stack_heads(do_ref, rows, g)
                dS1 = dS_scr[g]
                d_qd = _fold(_bdot(d_o, S, NT), mk)
                dP = jnp.where(mk["causal"], _bdot(d_o, vn, NT), 0.0)
                d_vn = _bdot(P, d_o, TN) + _bdot(_wide(kd, mk), dS1)
                d_kd = _fold(_bdot(vn, dS1, NT), mk)
                d_egl = per_row(wide_f, _lsum(dS1 * S))
                dS_scr[g] = dS1 * c["egl_state"] + _bdot(_wide(qd, mk), d_o, TN) - _bdot(w_wide, d_vn, TN)
                d_w = -_fold(_bdot(d_vn, S, NT), mk)
                d_vb = _bdot(Tm, d_vn, TN)
                d_kbe = _bdot(Tm, d_w, TN)
                dA = jnp.where(mk["strict"], -(_bdot(d_vb, u, NT) + _bdot(d_kbe, w, NT)), 0.0)
                dMA = dA * dec
                dMP = dP * dec
                d_kb = _bdot(dMA, k) + d_kbe * eg
                d_k = _bdot(dMA, kb, TN) + _bdot(dMP, q, TN) + d_kd * ek + d_kb * beta
                d_qs = (_bdot(dMP, k) + d_qd * eg) * (DH ** -0.5)
                d_v = d_vb * beta
                E = dA * A + dP * P
                col_sums = _mask_dot(ones, E, TN)[:, :1]
                t_kd = _lsum(d_kd * kd)
                d_gl = per_row(blk_f, t_kd) + d_egl * egl
                d_gc = (_lsum(E) - col_sums + _lsum(d_qd * qd) + _lsum(d_kbe * kbe) - t_kd
                        + jnp.where(row % DNC == DNC - 1, d_gl, 0.0))
                d_g = per_row(jnp.where(mk["upper"], 1.0, 0.0), d_gc)
                d_beta = _lsum(d_kb * k) + _lsum(d_vb * v)
                for j, h in enumerate(c["heads"]):
                    rs = slice(j * DNC, (j + 1) * DNC)
                    dq_ref[rows, h * DH:(h + 1) * DH] = d_qs[rs]
                    dk_ref[rows, h * DH:(h + 1) * DH] = d_k[rs]
                    dv_ref[rows, h * DH:(h + 1) * DH] = d_v[rs]
                    dbg = dbg + jnp.where(lane == h, d_beta[rs], 0.0) + jnp.where(lane == h + H, d_g[rs], 0.0)
            dbg_ref[rows, :] = dbg

        for n in reversed(range(nb)):
            chunk(n)

        if nx:
            @pl.when(pl.program_id(0) == n_blocks - 1)
            def _():
                c_finish(*comm_refs)

    blk = lambda j: pl.BlockSpec((tb, WD), lambda i: (n_blocks - 1 - i, j))
    small = pl.BlockSpec((tb, 128), lambda i: (n_blocks - 1 - i, 0))
    outs = pl.pallas_call(
        body, name=name + ("_" + comm[0] if nx else ""), grid=(n_blocks,),
        in_specs=[blk(0), blk(1), blk(2), small,
                  pl.BlockSpec((nb, NG, GK, DH), lambda i: (n_blocks - 1 - i, 0, 0, 0)),
                  pl.BlockSpec((nb, NG, GR, GR), lambda i: (n_blocks - 1 - i, 0, 0, 0)), blk(0)] + sp["specs"],
        out_specs=[blk(0), blk(0), blk(0), small] + sp["specs"],
        out_shape=[jax.ShapeDtypeStruct((T, WD), F32)] * 3 + [jax.ShapeDtypeStruct((T, 128), F32)] + sp["out_shape"],
        scratch_shapes=[pltpu.VMEM((NG, GK, DH), F32)] + (sp["sems"] if nx else []),
        compiler_params=_cp("arbitrary"),
    )(qkv, qkv, qkv, bg, s_all, tm_all, do, *exchange)
    return outs[0], outs[1], outs[2], outs[3], list(outs[4:])


def _dn_post_fwd(o, proj, gon, name="dn_post_fwd"):
    T = o.shape[0]
    tt = _tile(T, 256, 16)

    def body(o_ref, z_ref, g_ref, y_ref):
        for hh in range(H):
            sl = slice(hh * DH, (hh + 1) * DH)
            ov = o_ref[:, sl]
            zv = z_ref[:, sl].astype(F32)
            r = lax.rsqrt(jnp.mean(ov * ov, axis=-1, keepdims=True) + EPS)
            y_ref[:, sl] = (ov * r * g_ref[...] * (zv * _sig(zv))).astype(y_ref.dtype)

    return pl.pallas_call(
        body, name=name, grid=(T // tt,),
        in_specs=[pl.BlockSpec((tt, WD), lambda i: (i, 0)), pl.BlockSpec((tt, WD), lambda i: (i, OZ // WD)),
                  pl.BlockSpec((1, DH), lambda i: (0, 0))],
        out_specs=pl.BlockSpec((tt, WD), lambda i: (i, 0)),
        out_shape=jax.ShapeDtypeStruct((T, WD), BF16), compiler_params=_cp("parallel"),
    )(o, proj, gon)


def _dn_post_bwd(o, proj, gon, dy, name="dn_post_bwd"):
    T = o.shape[0]
    tt = _tile(T, 256, 16)

    def body(o_ref, z_ref, g_ref, dy_ref, do_ref, dz_ref, dg_ref):
        @pl.when(pl.program_id(0) == 0)
        def _():
            dg_ref[...] = jnp.zeros_like(dg_ref)

        acc = jnp.zeros((1, DH), F32)
        for hh in range(H):
            sl = slice(hh * DH, (hh + 1) * DH)
            ov = o_ref[:, sl]
            zv = z_ref[:, sl].astype(F32)
            dyv = dy_ref[:, sl]
            r = lax.rsqrt(jnp.mean(ov * ov, axis=-1, keepdims=True) + EPS)
            oh = ov * r
            nrm = oh * g_ref[...]
            gate, gate_grad = _silu_and_grad(zv)
            dn = dyv * gate
            dz_ref[:, sl] = (dyv * nrm * gate_grad).astype(dz_ref.dtype)
            doh = dn * g_ref[...]
            do_ref[:, sl] = r * (doh - oh * jnp.mean(doh * oh, axis=-1, keepdims=True))
            acc = acc + jnp.sum(dn * oh, axis=0, keepdims=True)
        dg_ref[...] += acc

    row = pl.BlockSpec((tt, WD), lambda i: (i, 0))
    vec = pl.BlockSpec((1, DH), lambda i: (0, 0))
    return pl.pallas_call(
        body, name=name, grid=(T // tt,),
        in_specs=[row, pl.BlockSpec((tt, WD), lambda i: (i, OZ // WD)), vec, row],
        out_specs=[row, row, vec],
        out_shape=[jax.ShapeDtypeStruct((T, WD), F32), jax.ShapeDtypeStruct((T, WD), BF16),
                   jax.ShapeDtypeStruct((1, DH), F32)],
        compiler_params=_cp("arbitrary"),
    )(o, proj, gon, dy)


def _sg_common(u_ref, v_ref, lng_ref, lnb_ref, with_grad=True):
    ur = u_ref[...].astype(F32)
    vr = v_ref[...].astype(F32)
    vgel, vgel_grad = _gelu_and_grad(vr) if with_grad else (_gelu(vr), None)
    mu = jnp.mean(vgel, axis=-1, keepdims=True)
    xc = vgel - mu
    rs = lax.rsqrt(jnp.mean(xc * xc, axis=-1, keepdims=True) + EPS)
    xh = xc * rs
    vg = xh * lng_ref[...] + lnb_ref[...]
    return ur, vgel_grad, rs, xh, vg


def _sg_fwd(proj, lng, lnb, sgw, sgbt, name="sg_fwd"):
    T = proj.shape[0]

    def body(u_ref, v_ref, lng_ref, lnb_ref, w_ref, bt_ref, y_ref):
        ur, _, _, _, vg = _sg_common(u_ref, v_ref, lng_ref, lnb_ref, with_grad=False)
        ri = lax.broadcasted_iota(jnp.int32, (SGC, SGC), 0)
        ci = lax.broadcasted_iota(jnp.int32, (SGC, SGC), 1)
        ug = _gelu(ur)
        for g in range(H):
            sl = slice(g * DH, (g + 1) * DH)
            ws = jnp.where(ri >= ci, w_ref[g], 0.0)
            mixed = _bdot(ws, vg[:, sl]) + bt_ref[:, g:g + 1]
            y_ref[:, sl] = (ug[:, sl] * mixed).astype(y_ref.dtype)

    vec = pl.BlockSpec((1, WD), lambda i: (0, 0))
    return pl.pallas_call(
        body, name=name, grid=(T // SGC,),
        in_specs=[pl.BlockSpec((SGC, WD), lambda i: (i, OU // WD)), pl.BlockSpec((SGC, WD), lambda i: (i, OV // WD)),
                  vec, vec, pl.BlockSpec((H, SGC, SGC), lambda i: (0, 0, 0)),
                  pl.BlockSpec((SGC, H), lambda i: (0, 0))],
        out_specs=pl.BlockSpec((SGC, WD), lambda i: (i, 0)),
        out_shape=jax.ShapeDtypeStruct((T, WD), BF16), compiler_params=_cp("parallel"),
    )(proj, proj, lng, lnb, sgw, sgbt)


def _sg_bwd(proj, lng, lnb, sgw, sgbt, dy, name="sg_bwd"):
    T = proj.shape[0]

    def body(u_ref, v_ref, lng_ref, lnb_ref, w_ref, bt_ref, dy_ref,
             du_ref, dv_ref, dw_ref, dbt_ref, dlng_ref, dlnb_ref):
        @pl.when(pl.program_id(0) == 0)
        def _():
            dw_ref[...] = jnp.zeros_like(dw_ref)
            dbt_ref[...] = jnp.zeros_like(dbt_ref)
            dlng_ref[...] = jnp.zeros_like(dlng_ref)
            dlnb_ref[...] = jnp.zeros_like(dlnb_ref)

        ur, vgel_grad, rs, xh, vg = _sg_common(u_ref, v_ref, lng_ref, lnb_ref)
        ri = lax.broadcasted_iota(jnp.int32, (SGC, SGC), 0)
        ci = lax.broadcasted_iota(jnp.int32, (SGC, SGC), 1)
        ug, ug_grad = _gelu_and_grad(ur)
        dyv = dy_ref[...]
        dbt = jnp.zeros((SGC, 128), F32)
        dvg_parts = []
        for g in range(H):
            sl = slice(g * DH, (g + 1) * DH)
            ws = jnp.where(ri >= ci, w_ref[g], 0.0)
            mixed = _bdot(ws, vg[:, sl]) + bt_ref[:, g:g + 1]
            dyg = dyv[:, sl]
            du_ref[:, sl] = (dyg * mixed * ug_grad[:, sl]).astype(du_ref.dtype)
            dmix = dyg * ug[:, sl]
            dw_ref[g] += jnp.where(ri >= ci, _bdot(dmix, vg[:, sl], NT), 0.0)
            dbt = dbt + jnp.where(ci == g, _lsum(dmix), 0.0)
            dvg_parts.append(_bdot(ws, dmix, TN))
        dbt_ref[...] += dbt
        dvg = jnp.concatenate(dvg_parts, axis=1)
        dlng_ref[...] += jnp.sum(dvg * xh, axis=0, keepdims=True)
        dlnb_ref[...] += jnp.sum(dvg, axis=0, keepdims=True)
        dxh = dvg * lng_ref[...]
        dvgel = rs * (dxh - jnp.mean(dxh, axis=-1, keepdims=True) - xh * jnp.mean(dxh * xh, axis=-1, keepdims=True))
        dv_ref[...] = (dvgel * vgel_grad).astype(dv_ref.dtype)

    vec = pl.BlockSpec((1, WD), lambda i: (0, 0))
    row = pl.BlockSpec((SGC, WD), lambda i: (i, 0))
    return pl.pallas_call(
        body, name=name, grid=(T // SGC,),
        in_specs=[pl.BlockSpec((SGC, WD), lambda i: (i, OU // WD)), pl.BlockSpec((SGC, WD), lambda i: (i, OV // WD)),
                  vec, vec, pl.BlockSpec((H, SGC, SGC), lambda i: (0, 0, 0)),
                  pl.BlockSpec((SGC, H), lambda i: (0, 0)), row],
        out_specs=[row, row, pl.BlockSpec((H, SGC, SGC), lambda i: (0, 0, 0)),
                   pl.BlockSpec((SGC, 128), lambda i: (0, 0)), vec, vec],
        out_shape=[jax.ShapeDtypeStruct((T, WD), BF16), jax.ShapeDtypeStruct((T, WD), BF16),
                   jax.ShapeDtypeStruct((H, SGC, SGC), F32), jax.ShapeDtypeStruct((SGC, 128), F32),
                   jax.ShapeDtypeStruct((1, WD), F32), jax.ShapeDtypeStruct((1, WD), F32)],
        compiler_params=_cp("arbitrary"),
    )(proj, proj, lng, lnb, sgw, sgbt, dy)


def _merge_fwd(proj, yap, ybp, D, name="merge_fwd"):
    T = proj.shape[0]
    tt = _tile(T, 256, 16)

    def body(ga_ref, gb_ref, a_ref, b_ref, o_ref):
        ga, gb, a, b = [r[...].astype(F32) for r in (ga_ref, gb_ref, a_ref, b_ref)]
        o_ref[...] = (_sig(ga) * a + _sig(gb) * b).astype(o_ref.dtype)

    row = pl.BlockSpec((tt, D), lambda i: (i, 0))
    return pl.pallas_call(
        body, name=name, grid=(T // tt,),
        in_specs=[pl.BlockSpec((tt, D), lambda i: (i, OGA // D)), pl.BlockSpec((tt, D), lambda i: (i, OGA // D + 1)),
                  row, row],
        out_specs=row, out_shape=jax.ShapeDtypeStruct((T, D), BF16), compiler_params=_cp("parallel"),
    )(proj, proj, yap, ybp)


def _merge_bwd(proj, yap, ybp, dm, D, name="merge_bwd"):
    T = proj.shape[0]
    tt = _tile(T, 256, 16)

    def body(ga_ref, gb_ref, a_ref, b_ref, dm_ref, da_ref, db_ref, dga_ref, dgb_ref):
        d, ga, gb, a, b = [r[...].astype(F32) for r in (dm_ref, ga_ref, gb_ref, a_ref, b_ref)]
        sa = _sig(ga)
        sb = _sig(gb)
        da_ref[...] = (d * sa).astype(da_ref.dtype)
        db_ref[...] = (d * sb).astype(db_ref.dtype)
        dga_ref[...] = (d * a * sa * (1.0 - sa)).astype(dga_ref.dtype)
        dgb_ref[...] = (d * b * sb * (1.0 - sb)).astype(dgb_ref.dtype)

    row = pl.BlockSpec((tt, D), lambda i: (i, 0))
    return pl.pallas_call(
        body, name=name, grid=(T // tt,),
        in_specs=[pl.BlockSpec((tt, D), lambda i: (i, OGA // D)), pl.BlockSpec((tt, D), lambda i: (i, OGA // D + 1)),
                  row, row, row],
        out_specs=[row] * 4, out_shape=[jax.ShapeDtypeStruct((T, D), BF16)] * 4,
        compiler_params=_cp("parallel"),
    )(proj, proj, yap, ybp, dm)


def _ffn_act_fwd(gp, up, cw, cb, name="ffn_act_fwd"):
    T, F = gp.shape

    def body(g_ref, u_ref, w_ref, b_ref, o_ref):
        gv = g_ref[...].astype(F32)
        w = w_ref[...]
        c = gv * w[FF_K - 1:FF_K, :] + b_ref[...]
        for k in range(1, FF_K):
            c = c + _shift_down(gv, k) * w[FF_K - 1 - k:FF_K - k, :]
        o_ref[...] = (c * _sig(c) * u_ref[...].astype(F32)).astype(o_ref.dtype)

    col = pl.BlockSpec((T, 128), lambda j: (0, j))
    return pl.pallas_call(
        body, name=name, grid=(F // 128,),
        in_specs=[col, col, pl.BlockSpec((FF_K, 128), lambda j: (0, j)), pl.BlockSpec((1, 128), lambda j: (0, j))],
        out_specs=col, out_shape=jax.ShapeDtypeStruct((T, F), BF16), compiler_params=_cp("parallel"),
    )(gp, up, cw, cb)


def _ffn_act_bwd(gp, up, cw, cb, dact, name="ffn_act_bwd"):
    T, F = gp.shape

    def body(g_ref, u_ref, w_ref, b_ref, d_ref, dg_ref, du_ref, dw_ref, db_ref):
        gv = g_ref[...].astype(F32)
        w = w_ref[...]
        shifted = [_shift_down(gv, k) for k in range(FF_K)]
        c = shifted[0] * w[FF_K - 1:FF_K, :] + b_ref[...]
        for k in range(1, FF_K):
            c = c + shifted[k] * w[FF_K - 1 - k:FF_K - k, :]
        d = d_ref[...].astype(F32)
        act, act_grad = _silu_and_grad(c)
        du_ref[...] = (d * act).astype(du_ref.dtype)
        dc = d * u_ref[...].astype(F32) * act_grad
        dg = dc * w[FF_K - 1:FF_K, :]
        for k in range(1, FF_K):
            dg = dg + _shift_up(dc, k) * w[FF_K - 1 - k:FF_K - k, :]
        dg_ref[...] = dg.astype(dg_ref.dtype)
        rows = [jnp.sum(dc * shifted[FF_K - 1 - t], axis=0, keepdims=True) for t in range(FF_K)]
        dw_ref[...] = jnp.concatenate(rows, axis=0)
        db_ref[...] = jnp.sum(dc, axis=0, keepdims=True)

    col = pl.BlockSpec((T, 128), lambda j: (0, j))
    wspec = pl.BlockSpec((FF_K, 128), lambda j: (0, j))
    bspec = pl.BlockSpec((1, 128), lambda j: (0, j))
    return pl.pallas_call(
        body, name=name, grid=(F // 128,),
        in_specs=[col, col, wspec, bspec, col], out_specs=[col, col, wspec, bspec],
        out_shape=[jax.ShapeDtypeStruct((T, F), BF16), jax.ShapeDtypeStruct((T, F), BF16),
                   jax.ShapeDtypeStruct((FF_K, F), F32), jax.ShapeDtypeStruct((1, F), F32)],
        compiler_params=_cp("parallel"),
    )(gp, up, cw, cb, dact)


class _Carrier:
    def __init__(self, plan=None, deliver=None):
        self.plan, self.deliver = plan or (lambda kernel: None), deliver

    def run(self, kernel, fn, **kw):
        comm = self.plan(kernel)
        out = fn(comm=comm, **kw)
        if comm:
            self.deliver(kernel, out[-1])
            out = out[:-1]
            return out[0] if len(out) == 1 else out
        return out


def _layer_fwd(x, w, carrier=None):
    cr = carrier or _Carrier()
    D = x.shape[1]
    oba = OGA + 2 * D
    h = _rms_fwd(x, w["norm1_g"], "rms1_fwd")
    proj = cr.run("proj", functools.partial(_mm, h, w["w_in_t"], "nt", BF16, name="mm_proj"))
    bg = _ba_fwd(proj, w["alog_row"], w["dtb_row"], oba)
    qkv = _dn_prep_fwd(proj, w["dn_conv_w"])
    r = cr.run("dn_core", functools.partial(_dn_core_fwd, qkv, bg))
    o, s_all, tm_all = r[0], r[1], r[2]
    ya = _dn_post_fwd(o, proj, w["dn_onorm_g"])
    yb = _sg_fwd(proj, w["sg_ln_g"], w["sg_ln_b"], w["sg_w"], w["sg_bt"])
    yap = _mm(ya, w["w_branch_a"], "nn", BF16, name="mm_branch")
    ybp = _mm(yb, w["w_branch_b"], "nn", BF16, name="mm_branch")
    merged = _merge_fwd(proj, yap, ybp, D)
    x1 = _mm(merged, w["w_out"], "nn", F32, add=x, name="mm_out")
    h2 = _rms_fwd(x1, w["norm2_g"], "rms2_fwd")
    gp = cr.run("ffn_gate", functools.partial(_mm, h2, w["ffn_w_gate"], "nn", BF16, name="mm_ffn_in"))
    up = cr.run("ffn_up", functools.partial(_mm, h2, w["ffn_w_up"], "nn", BF16, name="mm_ffn_in"))
    act = _ffn_act_fwd(gp, up, w["ffn_conv_w"], w["ffn_conv_b"])
    x2 = cr.run("ffn_down", functools.partial(_mm, act, w["ffn_w_down"], "nn", F32, add=x1, name="mm_ffn_down"))
    saved = dict(x=x, h=h, proj=proj, bg=bg, qkv=qkv, o=o, s_all=s_all, tm_all=tm_all, ya=ya, yb=yb, yap=yap,
                 ybp=ybp, merged=merged, x1=x1, h2=h2, gp=gp, up=up, act=act)
    return x2, saved


def _layer_bwd(dx2, w, s, carrier=None, ffn_grads_ready=None, rest_grads_ready=None):
    cr = carrier or _Carrier()
    D = dx2.shape[1]
    oba = OGA + 2 * D
    g = {}
    dx2b = dx2.astype(BF16)
    dact = cr.run("d_act", functools.partial(_mm, dx2b, w["ffn_w_down"], "nt", BF16, name="mm_d_act"))
    g["ffn_w_down"] = _mm(s["act"], dx2b, "tn", BF16, name="mm_dw_down")
    dgp, dup, g["ffn_conv_w"], g["ffn_conv_b"] = _ffn_act_bwd(s["gp"], s["up"], w["ffn_conv_w"], w["ffn_conv_b"], dact)
    dh2 = _mm(dgp, w["ffn_w_gate"], "nt", F32, name="mm_dh2")
    dh2 = _mm(dup, w["ffn_w_up"], "nt", F32, add=dh2, name="mm_dh2_acc")
    g["ffn_w_gate"] = _mm(s["h2"], dgp, "tn", BF16, name="mm_dw_ffn_in")
    g["ffn_w_up"] = _mm(s["h2"], dup, "tn", BF16, name="mm_dw_ffn_in")
    if ffn_grads_ready:
        ffn_grads_ready(g)
    dx1, g["norm2_g"] = _rms_bwd(s["x1"], w["norm2_g"], dh2, dx2, "rms2_bwd")
    dx1b = dx1.astype(BF16)
    dm = cr.run("d_merged", functools.partial(_mm, dx1b, w["w_out"], "nt", BF16, name="mm_d_merged"))
    g["w_out"] = _mm(s["merged"], dx1b, "tn", BF16, name="mm_dw_out")
    dyap, dybp, dga, dgb = _merge_bwd(s["proj"], s["yap"], s["ybp"], dm, D)
    dya = _mm(dyap, w["w_branch_a"], "nt", F32, name="mm_d_branch")
    dyb = _mm(dybp, w["w_branch_b"], "nt", F32, name="mm_d_branch")
    g["w_branch_a"] = _mm(s["ya"], dyap, "tn", BF16, name="mm_dw_branch")
    g["w_branch_b"] = _mm(s["yb"], dybp, "tn", BF16, name="mm_dw_branch")
    du, dv, g["sg_w"], dbt, g["sg_ln_g"], g["sg_ln_b"] = _sg_bwd(
        s["proj"], w["sg_ln_g"], w["sg_ln_b"], w["sg_w"], w["sg_bt"], dyb)
    g["sg_b"] = jnp.transpose(dbt[:, :H])
    do, dz, g["dn_onorm_g"] = _dn_post_bwd(s["o"], s["proj"], w["dn_onorm_g"], dya)
    r = cr.run("dn_core", functools.partial(_dn_core_bwd, s["qkv"], s["bg"], s["s_all"], s["tm_all"], do))
    dq, dk, dvv, dbg = r[0], r[1], r[2], r[3]
    dqkv, g["dn_conv_w"] = _dn_prep_bwd(s["proj"], w["dn_conv_w"], dq, dk, dvv)
    dba, dal, ddt = _ba_bwd(s["proj"], w["alog_row"], w["dtb_row"], dbg, oba)
    g["dn_a_log"] = dal[0, H:2 * H]
    g["dn_dt_bias"] = ddt[0, H:2 * H]
    dproj = jnp.concatenate([dqkv, dz, du, dv, dga, dgb, dba], axis=1)
    g["w_in_t"] = cr.run("dw_in", functools.partial(_mm, dproj, s["h"], "tn", BF16, name="mm_dw_in"))
    if rest_grads_ready:
        rest_grads_ready(g)
    dh = cr.run("dh", functools.partial(_mm, dproj, w["w_in_t"], "nn", F32, name="mm_dh"))
    dx, g["norm1_g"] = _rms_bwd(s["x"], w["norm1_g"], dh, dx1, "rms1_bwd")
    return dx, g


def _row128(v, off):
    return jnp.pad(v, (off, 128 - off - v.shape[0]))[None]


def _prep_small(p):
    return dict(
        norm1_g=p["norm1_g"][None], alog_row=_row128(p["dn_a_log"], H), dtb_row=_row128(p["dn_dt_bias"], H),
        dn_conv_w=p["dn_conv_w"], dn_onorm_g=p["dn_onorm_g"][None],
        sg_ln_g=p["sg_ln_g"][None], sg_ln_b=p["sg_ln_b"][None], sg_w=p["sg_w"], sg_bt=jnp.transpose(p["sg_b"]),
        norm2_g=p["norm2_g"][None], ffn_conv_w=p["ffn_conv_w"], ffn_conv_b=p["ffn_conv_b"][None])


HBM_SPEC = pl.BlockSpec(memory_space=pltpu.HBM)


def _coords():
    return lax.axis_index("x"), lax.axis_index("y"), lax.axis_index("c")


def _other_chips(x, y):
    return [(1 - x, y), (x, 1 - y), (1 - x, 1 - y)]


def _remote(src, dst, send_sems, recv_sems, k, dev):
    return pltpu.make_async_remote_copy(src_ref=src, dst_ref=dst, send_sem=send_sems.at[k], recv_sem=recv_sems.at[k],
                                        device_id=dev, device_id_type=MESH)


def _ag_copies(w_refs, o_refs, send_sems, recv_sems):
    x, y, c = _coords()
    me = 2 * x + y
    chips = _other_chips(x, y)

    def ici(k, j, owner):
        chip = chips[j]
        return _remote(w_refs[k].at[c], o_refs[k].at[owner, c], send_sems, recv_sems, 6 * k + j, (chip[0], chip[1], c))

    def d2d(k, j, part):
        owner = 2 * chips[j][0] + chips[j][1]
        return _remote(o_refs[k].at[owner, part], o_refs[k].at[owner, part], send_sems, recv_sems, 6 * k + 3 + j,
                       (x, y, 1 - c))

    n = len(w_refs)
    return me, c, chips, ici, d2d, [(k, j) for k in range(n) for j in range(3)]


def _ag_start(w_refs, o_refs, send_sems, recv_sems):
    me, _, _, ici, _, pairs = _ag_copies(w_refs, o_refs, send_sems, recv_sems)
    for k, j in pairs:
        ici(k, j, me).start()


def _ag_finish(w_refs, o_refs, send_sems, recv_sems):
    me, c, chips, ici, d2d, pairs = _ag_copies(w_refs, o_refs, send_sems, recv_sems)
    for k, j in pairs:
        ici(k, j, 2 * chips[j][0] + chips[j][1]).wait_recv()
        d2d(k, j, c).start()
    for k, j in pairs:
        d2d(k, j, 1 - c).wait_recv()
    for k, j in pairs:
        ici(k, j, me).wait_send()
        d2d(k, j, c).wait_send()


def _ag_specs(ws):
    n = len(ws)
    return dict(out_shape=[jax.ShapeDtypeStruct((N_CHIPS,) + w.shape, w.dtype) for w in ws],
                specs=[HBM_SPEC] * n, sems=[pltpu.SemaphoreType.DMA((6 * n,)), pltpu.SemaphoreType.DMA((6 * n,))])


def _ag_layers(ws):
    n = len(ws)

    def body(*refs):
        _ag_start(refs[:n], refs[n:2 * n], *refs[2 * n:])
        _ag_finish(refs[:n], refs[n:2 * n], *refs[2 * n:])

    sp = _ag_specs(ws)
    return pl.pallas_call(
        body, name="ag_weights", out_shape=sp["out_shape"], in_specs=sp["specs"], out_specs=sp["specs"],
        scratch_shapes=sp["sems"],
    )(*ws)


def _pair_copies(g_refs, b_refs, send_sems, recv_sems):
    x, y, c = _coords()
    return [_remote(g_refs[k].at[i, 1 - c], b_refs[k].at[i], send_sems, recv_sems, N_CHIPS * k + i, (x, y, 1 - c))
            for k in range(len(g_refs)) for i in range(N_CHIPS)]


def _pair_start(g_refs, b_refs, send_sems, recv_sems):
    for cp in _pair_copies(g_refs, b_refs, send_sems, recv_sems):
        cp.start()


def _pair_finish(g_refs, b_refs, send_sems, recv_sems):
    for cp in _pair_copies(g_refs, b_refs, send_sems, recv_sems):
        cp.wait()


def _pair_specs(Gs):
    n = len(Gs)
    return dict(out_shape=[jax.ShapeDtypeStruct((N_CHIPS,) + g.shape[2:], g.dtype) for g in Gs],
                specs=[HBM_SPEC] * n,
                sems=[pltpu.SemaphoreType.DMA((N_CHIPS * n,)), pltpu.SemaphoreType.DMA((N_CHIPS * n,))])


def _rs_pair_exchange(Gs):
    n = len(Gs)

    def body(*refs):
        _pair_start(refs[:n], refs[n:2 * n], *refs[2 * n:])
        _pair_finish(refs[:n], refs[n:2 * n], *refs[2 * n:])

    sp = _pair_specs(Gs)
    return pl.pallas_call(
        body, name="rs_pair_exchange", out_shape=sp["out_shape"], in_specs=sp["specs"], out_specs=sp["specs"],
        scratch_shapes=sp["sems"],
    )(*Gs)


def _rs_add_pair(G, B, c, name):
    _, _, R, C = G.shape
    tr = _tile(R, 256, 16)

    def body(c_ref, g_ref, b_ref, o_ref):
        o_ref[0] = (g_ref[0, 0].astype(F32) + b_ref[0].astype(F32)).astype(o_ref.dtype)

    grid_spec = pltpu.PrefetchScalarGridSpec(
        num_scalar_prefetch=1, grid=(N_CHIPS, R // tr),
        in_specs=[pl.BlockSpec((1, 1, tr, C), lambda i, r, c_ref: (i, c_ref[0], r, 0)),
                  pl.BlockSpec((1, tr, C), lambda i, r, c_ref: (i, r, 0))],
        out_specs=pl.BlockSpec((1, tr, C), lambda i, r, c_ref: (i, r, 0)))
    return pl.pallas_call(
        body, name=name, grid_spec=grid_spec, out_shape=jax.ShapeDtypeStruct((N_CHIPS, R, C), G.dtype),
        compiler_params=_cp("parallel", "parallel"),
    )(jnp.reshape(c, (1,)).astype(jnp.int32), G, B)


def _rsx_copies(p_refs, b_refs, send_sems, recv_sems):
    x, y, c = _coords()
    me = 2 * x + y
    chips = _other_chips(x, y)

    def cp(k, j, src_slot, dst_slot):
        return _remote(p_refs[k].at[src_slot], b_refs[k].at[dst_slot], send_sems, recv_sems, 3 * k + j,
                       (chips[j][0], chips[j][1], c))

    return me, chips, cp, [(k, j) for k in range(len(p_refs)) for j in range(3)]


def _rsx_start(p_refs, b_refs, send_sems, recv_sems):
    me, chips, cp, pairs = _rsx_copies(p_refs, b_refs, send_sems, recv_sems)
    for k, j in pairs:
        cp(k, j, 2 * chips[j][0] + chips[j][1], me).start()


def _rsx_finish(p_refs, b_refs, send_sems, recv_sems):
    me, chips, cp, pairs = _rsx_copies(p_refs, b_refs, send_sems, recv_sems)
    for k, j in pairs:
        owner = 2 * chips[j][0] + chips[j][1]
        cp(k, j, owner, owner).wait_recv()
    for k, j in pairs:
        cp(k, j, 2 * chips[j][0] + chips[j][1], me).wait_send()


def _rsx_specs(Ps):
    n = len(Ps)
    return dict(out_shape=[jax.ShapeDtypeStruct(p.shape, p.dtype) for p in Ps], specs=[HBM_SPEC] * n,
                sems=[pltpu.SemaphoreType.DMA((3 * n,)), pltpu.SemaphoreType.DMA((3 * n,))])


def _rs_sum_chips(P, B, me, name):
    _, R, C = P.shape
    tr = _tile(R, 256, 16)

    def body(me_ref, p_ref, b1_ref, b2_ref, b3_ref, o_ref):
        o_ref[...] = ((p_ref[0].astype(F32) + b1_ref[0].astype(F32)) + b2_ref[0].astype(F32)) + b3_ref[0].astype(F32)

    slot = lambda d: pl.BlockSpec((1, tr, C), lambda r, me_ref: ((me_ref[0] + d) % N_CHIPS, r, 0))
    grid_spec = pltpu.PrefetchScalarGridSpec(
        num_scalar_prefetch=1, grid=(R // tr,), in_specs=[slot(0), slot(1), slot(2), slot(3)],
        out_specs=pl.BlockSpec((tr, C), lambda r, me_ref: (r, 0)))
    return pl.pallas_call(
        body, name=name, grid_spec=grid_spec, out_shape=jax.ShapeDtypeStruct((R, C), F32),
        compiler_params=_cp("parallel"),
    )(jnp.reshape(me, (1,)).astype(jnp.int32), P, B, B, B)


def _sum_slots(B, name):
    S, R, C = B.shape
    tr = _tile(R, 256, 16)

    def body(b_ref, o_ref):
        acc = b_ref[0].astype(F32)
        for i in range(1, S):
            acc = acc + b_ref[i].astype(F32)
        o_ref[...] = acc

    return pl.pallas_call(
        body, name=name, grid=(R // tr,), in_specs=[pl.BlockSpec((S, tr, C), lambda r: (0, r, 0))],
        out_specs=pl.BlockSpec((tr, C), lambda r: (r, 0)), out_shape=jax.ShapeDtypeStruct((R, C), F32),
        compiler_params=_cp("parallel"),
    )(B)


def _rs_pair_swap(Rs):
    n = len(Rs)

    def body(*refs):
        r_refs, o_refs = refs[:n], refs[n:2 * n]
        send_sems, recv_sems = refs[2 * n:]
        x, y, c = _coords()
        cps = [_remote(r_refs[k], o_refs[k], send_sems, recv_sems, k, (x, y, 1 - c)) for k in range(n)]
        for cp in cps:
            cp.start()
        for cp in cps:
            cp.wait()

    return pl.pallas_call(
        body, name="rs_pair_swap", out_shape=[jax.ShapeDtypeStruct(r.shape, r.dtype) for r in Rs],
        in_specs=[HBM_SPEC] * n, out_specs=[HBM_SPEC] * n,
        scratch_shapes=[pltpu.SemaphoreType.DMA((n,)), pltpu.SemaphoreType.DMA((n,))],
    )(*Rs)


def _ag8(v):
    R = v.shape[0]

    def body(v_ref, out_ref, send_sems, recv_sems, local_sem):
        x, y, c = _coords()
        me, sib = (x, y, c), (x, y, 1 - c)
        chips = _other_chips(x, y)

        def slot(p):
            return out_ref.at[4 * p[0] + 2 * p[1] + p[2]]

        def copy(k, block, to, src=None):
            return _remote(slot(block) if src is None else src, slot(block), send_sems, recv_sems, k, to)

        mine = pltpu.make_async_copy(v_ref, slot(me), local_sem)
        mine.start()
        first = [copy(0, me, sib, src=v_ref)]
        first += [copy(1 + j, me, (chip[0], chip[1], c), src=v_ref) for j, chip in enumerate(chips)]
        for cp in first:
            cp.start()
        passed = [copy(4 + j, (chip[0], chip[1], c), sib) for j, chip in enumerate(chips)]
        for j, chip in enumerate(chips):
            copy(1 + j, (chip[0], chip[1], c), me).wait_recv()
            passed[j].start()
        copy(0, sib, me).wait_recv()
        for j, chip in enumerate(chips):
            copy(4 + j, (chip[0], chip[1], 1 - c), me).wait_recv()
        for cp in first + passed:
            cp.wait_send()
        mine.wait()

    return pl.pallas_call(
        body, name="ag8_small", out_shape=jax.ShapeDtypeStruct((8, R, 128), v.dtype),
        in_specs=[pl.BlockSpec(memory_space=pltpu.VMEM)], out_specs=pl.BlockSpec(memory_space=pltpu.VMEM),
        scratch_shapes=[pltpu.SemaphoreType.DMA((7,)), pltpu.SemaphoreType.DMA((7,)), pltpu.SemaphoreType.DMA],
        compiler_params=pltpu.CompilerParams(vmem_limit_bytes=VMEM_LIMIT),
    )(v)


def _adamw(w, g, m, v, name):
    L, R, C = w.shape
    rows = [R] + [t for t in range(8, min(R, 1024) + 1, 8) if R % t == 0]
    cols = [C] + [t for t in range(128, C, 128) if C % t == 0]
    lead = [t for t in range(1, L + 1) if L % t == 0]
    fits = [(a * r * c, c, r, a) for a in lead for r in rows for c in cols if a * r * c * 4 <= 3 << 19]
    _, tc, tr, tl = max(fits) if fits else (0, min(cols), min(rows), 1)

    def body(w_ref, g_ref, m_ref, v_ref, d_ref, mo_ref, vo_ref):
        gv = g_ref[...]
        m2 = ADAM_B1 * m_ref[...] + (1.0 - ADAM_B1) * gv
        v2 = ADAM_B2 * v_ref[...] + (1.0 - ADAM_B2) * jnp.square(gv)
        m_hat = m2 / (1.0 - ADAM_B1 ** ADAM_STEP)
        v_hat = v2 / (1.0 - ADAM_B2 ** ADAM_STEP)
        d_ref[...] = -ADAM_LR * (m_hat / (jnp.sqrt(v_hat) + ADAM_EPS) + ADAM_WD * w_ref[...])
        mo_ref[...] = m2
        vo_ref[...] = v2

    blk = pl.BlockSpec((tl, tr, tc), lambda l, r, j: (l, r, j))
    return pl.pallas_call(
        body, name=name, grid=(L // tl, R // tr, C // tc), in_specs=[blk] * 4, out_specs=[blk] * 3,
        out_shape=[jax.ShapeDtypeStruct(w.shape, F32)] * 3,
        compiler_params=_cp("parallel", "parallel", "parallel"),
    )(w, g, m, v)


def _adamw_halves(w, g_mine, g_other, c, m, v, name):
    L, _, R, C = w.shape
    tr = _tile(R, 128, 8)

    def body(c_ref, w_ref, *rest):
        g_refs = rest[:2 * L]
        m_ref, v_ref, g_ref, d_ref, mo_ref, vo_ref = rest[2 * L:]
        l, h = pl.program_id(0), pl.program_id(1)
        gm, go = g_refs[0][...], g_refs[L][...]
        for i in range(1, L):
            gm = jnp.where(l == i, g_refs[i][...], gm)
            go = jnp.where(l == i, g_refs[L + i][...], go)
        gv = jnp.where(h == c_ref[0], gm, go)[None, None]
        g_ref[...] = gv
        m2 = ADAM_B1 * m_ref[...] + (1.0 - ADAM_B1) * gv
        v2 = ADAM_B2 * v_ref[...] + (1.0 - ADAM_B2) * jnp.square(gv)
        m_hat = m2 / (1.0 - ADAM_B1 ** ADAM_STEP)
        v_hat = v2 / (1.0 - ADAM_B2 ** ADAM_STEP)
        d_ref[...] = -ADAM_LR * (m_hat / (jnp.sqrt(v_hat) + ADAM_EPS) + ADAM_WD * w_ref[...])
        mo_ref[...] = m2
        vo_ref[...] = v2

    blk = pl.BlockSpec((1, 1, tr, C), lambda l, h, r, c_ref: (l, h, r, 0))

    def gblk(i, mine):
        def index(l, h, r, c_ref):
            use = jnp.logical_and(l == i, (h == c_ref[0]) == mine)
            return (jnp.where(use, r, 0), 0)
        return pl.BlockSpec((tr, C), index)

    grid_spec = pltpu.PrefetchScalarGridSpec(
        num_scalar_prefetch=1, grid=(L, 2, R // tr),
        in_specs=[blk] + [gblk(i, True) for i in range(L)] + [gblk(i, False) for i in range(L)] + [blk, blk],
        out_specs=[blk] * 4)
    return pl.pallas_call(
        body, name=name, grid_spec=grid_spec, out_shape=[jax.ShapeDtypeStruct(w.shape, F32)] * 4,
        compiler_params=_cp("parallel", "parallel", "parallel"),
    )(jnp.reshape(c, (1,)).astype(jnp.int32), w, *g_mine, *g_other, m, v)


BIG = ("w_in", "w_branch_a", "w_branch_b", "w_out", "ffn_w_gate", "ffn_w_up", "ffn_w_down")
ROW_SHARDED = ("w_out", "ffn_w_down")
SMALL = ("norm1_g", "dn_conv_w", "dn_a_log", "dn_dt_bias", "dn_onorm_g", "sg_ln_g", "sg_ln_b", "sg_w", "sg_b",
         "norm2_g", "ffn_conv_w", "ffn_conv_b", "final_norm_g")
SMALL_SHARDED = ("dn_conv_w", "ffn_conv_w")


def _pack_rows(arrs, mult):
    flat = jnp.concatenate([jnp.reshape(a, (-1,)) for a in arrs])
    n = flat.shape[0]
    rows = -(-n // (128 * mult)) * mult
    return jnp.reshape(jnp.pad(flat, (0, rows * 128 - n)), (rows, 128))


def _unpack(flat2d, shapes):
    flat = jnp.reshape(flat2d, (-1,))
    out, off = [], 0
    for shp in shapes:
        n = math.prod(shp)
        out.append(jnp.reshape(flat[off:off + n], shp))
        off += n
    return out


def kernel(x, norm1_g, w_in, dn_conv_w, dn_a_log, dn_dt_bias, dn_onorm_g, sg_ln_g, sg_ln_b, sg_w, sg_b, w_branch_a, w_branch_b, w_out, norm2_g, ffn_w_gate, ffn_w_up, ffn_conv_w, ffn_conv_b, ffn_w_down, final_norm_g, loss_target, m_norm1_g, m_w_in, m_dn_conv_w, m_dn_a_log, m_dn_dt_bias, m_dn_onorm_g, m_sg_ln_g, m_sg_ln_b, m_sg_w, m_sg_b, m_w_branch_a, m_w_branch_b, m_w_out, m_norm2_g, m_ffn_w_gate, m_ffn_w_up, m_ffn_conv_w, m_ffn_conv_b, m_ffn_w_down, m_final_norm_g, v_norm1_g, v_w_in, v_dn_conv_w, v_dn_a_log, v_dn_dt_bias, v_dn_onorm_g, v_sg_ln_g, v_sg_ln_b, v_sg_w, v_sg_b, v_w_branch_a, v_w_branch_b, v_w_out, v_norm2_g, v_ffn_w_gate, v_ffn_w_up, v_ffn_conv_w, v_ffn_conv_b, v_ffn_w_down, v_final_norm_g):
    W = dict(norm1_g=norm1_g, w_in=w_in, dn_conv_w=dn_conv_w, dn_a_log=dn_a_log, dn_dt_bias=dn_dt_bias,
             dn_onorm_g=dn_onorm_g, sg_ln_g=sg_ln_g, sg_ln_b=sg_ln_b, sg_w=sg_w, sg_b=sg_b, w_branch_a=w_branch_a,
             w_branch_b=w_branch_b, w_out=w_out, norm2_g=norm2_g, ffn_w_gate=ffn_w_gate, ffn_w_up=ffn_w_up,
             ffn_conv_w=ffn_conv_w, ffn_conv_b=ffn_conv_b, ffn_w_down=ffn_w_down, final_norm_g=final_norm_g)
    M = dict(norm1_g=m_norm1_g, w_in=m_w_in, dn_conv_w=m_dn_conv_w, dn_a_log=m_dn_a_log, dn_dt_bias=m_dn_dt_bias,
             dn_onorm_g=m_dn_onorm_g, sg_ln_g=m_sg_ln_g, sg_ln_b=m_sg_ln_b, sg_w=m_sg_w, sg_b=m_sg_b,
             w_branch_a=m_w_branch_a, w_branch_b=m_w_branch_b, w_out=m_w_out, norm2_g=m_norm2_g,
             ffn_w_gate=m_ffn_w_gate, ffn_w_up=m_ffn_w_up, ffn_conv_w=m_ffn_conv_w, ffn_conv_b=m_ffn_conv_b,
             ffn_w_down=m_ffn_w_down, final_norm_g=m_final_norm_g)
    V = dict(norm1_g=v_norm1_g, w_in=v_w_in, dn_conv_w=v_dn_conv_w, dn_a_log=v_dn_a_log, dn_dt_bias=v_dn_dt_bias,
             dn_onorm_g=v_dn_onorm_g, sg_ln_g=v_sg_ln_g, sg_ln_b=v_sg_ln_b, sg_w=v_sg_w, sg_b=v_sg_b,
             w_branch_a=v_w_branch_a, w_branch_b=v_w_branch_b, w_out=v_w_out, norm2_g=v_norm2_g,
             ffn_w_gate=v_ffn_w_gate, ffn_w_up=v_ffn_w_up, ffn_conv_w=v_ffn_conv_w, ffn_conv_b=v_ffn_conv_b,
             ffn_w_down=v_ffn_w_down, final_norm_g=v_final_norm_g)
    cx, cy, cc = _coords()
    chip = 2 * cx + cy
    L = w_in.shape[0]

    D = w_in.shape[1]
    cs_in = w_in.shape[2]
    c1 = 4 * WD
    ba_chip, ba_off = c1 // cs_in, c1 % cs_in
    assert ba_off + 2 * H <= cs_in
    n_main = N_CHIPS * cs_in - 2 * H
    main_start = [i * cs_in - (2 * H if i > ba_chip else 0) for i in range(N_CHIPS)]
    main_len = [cs_in - (2 * H if i == ba_chip else 0) for i in range(N_CHIPS)]
    tile0 = [16 * (s // 16) for s in main_start]
    shift = [s - t for s, t in zip(main_start, tile0)]
    rp_in = -(-max(sh + ln for sh, ln in zip(shift, main_len)) // 32) * 32
    seg = [tile0[i + 1] - tile0[i] for i in range(N_CHIPS - 1)] + [n_main - tile0[-1]]
    assert all(s + 16 <= rp_in for s in seg[:-1]) and seg[-1] <= rp_in and tile0[-1] + rp_in <= n_main + 128
    my_shift = jnp.asarray(shift, jnp.int32)[chip]

    mine = {n: W[n].astype(BF16) for n in BIG if n != "w_in"}
    wt = jnp.swapaxes(W["w_in"], 1, 2).astype(BF16)
    ba = wt[:, ba_off:ba_off + 2 * H]
    local_row = lax.broadcasted_iota(jnp.int32, (cs_in, 1), 0)
    without_ba = jnp.where(local_row < ba_off, wt, jnp.pad(wt[:, 2 * H:], ((0, 0), (0, 2 * H), (0, 0))))
    mine["w_in"] = lax.dynamic_update_slice(jnp.zeros((L, rp_in, D), BF16),
                                            jnp.where(chip == ba_chip, without_ba, wt), (0, my_shift, 0))
    mine["w_ba"] = jnp.pad(jnp.where(chip == ba_chip, ba, jnp.zeros_like(ba)), ((0, 0), (0, 32 - 2 * H), (0, 0)))

    def halves(a, lead=0):
        return jnp.reshape(a, a.shape[:lead] + (2, a.shape[lead] // 2) + a.shape[lead + 1:])

    first = [(0, "w_in"), (0, "w_ba")]
    my_taps = _pack_rows([W[n] for n in SMALL_SHARDED], 32)
    first_gathered = _ag_layers([halves(mine[n][l]) for l, n in first] + [halves(my_taps)])
    all_taps = jnp.reshape(first_gathered[-1], (N_CHIPS,) + my_taps.shape)
    tap_shards = [_unpack(jnp.where(chip == i, my_taps, all_taps[i]), [W[n].shape for n in SMALL_SHARDED])
                  for i in range(N_CHIPS)]
    taps_full = {n: jnp.concatenate([tap_shards[i][k] for i in range(N_CHIPS)], axis=-1)
                 for k, n in enumerate(SMALL_SHARDED)}

    ops = []
    for l in range(L):
        p = {n: W[n][l] for n in W if n not in ("final_norm_g",) + BIG + SMALL_SHARDED}
        p.update({n: taps_full[n][l] for n in SMALL_SHARDED})
        ops.append(_prep_small(p))

    def weights_landed(items, gathered):
        got = {}
        for (l, n), a in zip(items, gathered):
            a = jnp.reshape(a, (N_CHIPS,) + mine[n].shape[1:])
            got[(l, n)] = [jnp.where(chip == i, mine[n][l], a[i]) for i in range(N_CHIPS)]
        for (l, n), parts in got.items():
            if n == "w_in":
                pieces = [parts[0][:seg[0]]]
                for i in range(1, N_CHIPS):
                    pieces += [parts[i][:16] + parts[i - 1][seg[i - 1]:seg[i - 1] + 16], parts[i][16:seg[i]]]
                ba_rows = got[(l, "w_ba")][ba_chip][:2 * H]
                ops[l]["w_in_t"] = jnp.concatenate(pieces + [ba_rows, jnp.zeros((128 - 2 * H, D), BF16)], axis=0)
            elif n != "w_ba":
                ops[l][n] = jnp.concatenate(parts, axis=0 if n in ROW_SHARDED else 1)

    partial_sums, chip_sums = {}, {}

    grad_slices = {}

    def pair_sums(items, shares):
        for (l, n), b in zip(items, shares):
            partial_sums[(l, n)] = _rs_add_pair(grad_slices[(l, n)], b, cc, "rs_add_pair_" + n)

    def grad_partials(l, names, g, exchange_now=True):
        Gs = []
        for n in names:
            if n == "w_in":
                a = jnp.stack([g["w_in_t"][t:t + rp_in] for t in tile0])
            elif n == "w_ba":
                a = jnp.broadcast_to(g["w_in_t"][n_main:n_main + 32][None], (N_CHIPS, 32, D))
            elif n in ROW_SHARDED:
                a = jnp.reshape(g[n], (N_CHIPS, g[n].shape[0] // N_CHIPS, g[n].shape[1]))
            else:
                a = jnp.moveaxis(jnp.reshape(g[n], (g[n].shape[0], N_CHIPS, g[n].shape[1] // N_CHIPS)), 1, 0)
            grad_slices[(l, n)] = halves(a, 1)
        if exchange_now:
            items = [(l, n) for n in names]
            pair_sums(items, _rs_pair_exchange([grad_slices[i] for i in items]))

    def carrier(l, plan, kind, landed):
        source = {"gather": lambda i: halves(mine[i[1]][i[0]]), "pair": grad_slices.get, "exchange": partial_sums.get}

        def payload(kernel):
            items = plan.get((l, kernel))
            return (kind(kernel), [source[kind(kernel)](i) for i in items]) if items else None

        return _Carrier(payload, lambda kernel, res: landed(kernel)(plan[(l, kernel)], res))

    FFN = ("ffn_w_gate", "ffn_w_up", "ffn_w_down")
    REST = ("w_in", "w_ba", "w_branch_a", "w_branch_b", "w_out")
    fwd_plan = {(0, "proj"): [(0, "w_branch_a"), (0, "w_branch_b"), (0, "w_out"), (0, "ffn_w_gate")],
                (0, "dn_core"): [(0, "ffn_w_up"), (1, "w_in"), (1, "w_ba")],
                (0, "ffn_gate"): [(0, "ffn_w_down")],
                (1, "proj"): [(1, "w_branch_a"), (1, "w_branch_b"), (1, "w_out")],
                (1, "dn_core"): [(1, n) for n in FFN]}
    bwd_plan = {(1, "d_merged"): [(1, n) for n in FFN],
                (0, "d_merged"): [(0, n) for n in FFN],
                (1, "dn_core"): [(1, "ffn_w_gate"), (1, "ffn_w_up")],
                (1, "dw_in"): [(1, "ffn_w_down")],
                (0, "d_act"): [(1, "w_branch_a"), (1, "w_branch_b"), (1, "w_out")],
                (0, "dn_core"): [(1, "w_in"), (1, "w_ba"), (0, "ffn_w_down")],
                (0, "dw_in"): [(0, "ffn_w_gate"), (0, "ffn_w_up")],
                (0, "dh"): [(0, n) for n in REST]}

    def sums_landed(items, res):
        chip_sums.update(zip(items, res))

    weights_landed(first, first_gathered[:-1])
    xs, saved = x[0], []
    for l in range(L):
        xs, s = _layer_fwd(xs, ops[l], carrier(l, fwd_plan, lambda kernel: "gather", lambda kernel: weights_landed))
        saved.append(s)
    dx, dgf, loss = _loss_head(xs, final_norm_g[None], loss_target[0])
    loss = loss[0, 0]
    grads = [None] * L
    for l in reversed(range(L)):
        dx, grads[l] = _layer_bwd(
            dx, ops[l], saved[l],
            carrier(l, bwd_plan, lambda kernel: "pair" if kernel == "d_merged" else "exchange",
                    lambda kernel: pair_sums if kernel == "d_merged" else sums_landed),
            functools.partial(grad_partials, l, FFN, exchange_now=False), functools.partial(grad_partials, l, REST))
    travelled = BIG + ("w_ba",)
    g_mine = [[_rs_sum_chips(partial_sums[(l, n)], chip_sums[(l, n)], chip, "rs_sum_chips_" + n) for n in travelled]
              for l in range(L)]
    swapped = _rs_pair_swap(g_mine[0] + g_mine[1])
    g_other = [swapped[:len(travelled)], swapped[len(travelled):]]

    def both_halves(l, n):
        a, b = g_mine[l][travelled.index(n)], g_other[l][travelled.index(n)]
        return jnp.where(cc == 0, jnp.concatenate([a, b]), jnp.concatenate([b, a]))

    def w_in_grad_rows(l):
        m = lax.dynamic_slice_in_dim(both_halves(l, "w_in"), my_shift, cs_in, axis=0)
        ba_rows = jnp.pad(both_halves(l, "w_ba")[:2 * H], ((ba_off, cs_in - ba_off - 2 * H), (0, 0)))
        moved = jnp.pad(m[:cs_in - 2 * H], ((2 * H, 0), (0, 0)))
        with_ba = jnp.where(local_row < ba_off, m, jnp.where(local_row < ba_off + 2 * H, ba_rows, moved))
        return jnp.where(chip == ba_chip, with_ba, m)

    small = {n: jnp.stack([g[n] for g in grads]) for n in SMALL if n != "final_norm_g"}
    small["final_norm_g"] = dgf
    shapes = [taps_full[n].shape if n in SMALL_SHARDED else W[n].shape for n in SMALL] + [(1,)]
    sflat = _pack_rows([small[n] for n in SMALL] + [jnp.reshape(loss, (1,))], 16)
    sred = _unpack(_sum_slots(_ag8(sflat), "sum_small"), shapes)
    g_small = dict(zip(SMALL, sred[:-1]))
    loss_total = sred[-1][0]
    for n in SMALL_SHARDED:
        cs = W[n].shape[-1]
        g_small[n] = lax.dynamic_slice_in_dim(g_small[n], chip * cs, cs, axis=-1)

    g_big, delta, new_m, new_v = {}, {}, {}, {}
    for k, n in enumerate(BIG):
        gm, go = [g_mine[l][k] for l in range(L)], [g_other[l][k] for l in range(L)]
        if n == "w_in":
            g_t = jnp.stack([w_in_grad_rows(l) for l in range(L)], axis=1)
            outs = _adamw(*[jnp.transpose(a, (2, 0, 1)) for a in (W[n],)], g_t,
                          *[jnp.transpose(a, (2, 0, 1)) for a in (M[n], V[n])], "adamw_" + n)
            g_big[n], delta[n], new_m[n], new_v[n] = [jnp.transpose(o, (1, 2, 0)) for o in (g_t,) + tuple(outs)]
        else:
            outs = _adamw_halves(halves(W[n], 1), gm, go, cc, halves(M[n], 1), halves(V[n], 1), "adamw_" + n)
            g_big[n], delta[n], new_m[n], new_v[n] = [jnp.reshape(o, W[n].shape) for o in outs]
    for n in SMALL:
        shp = W[n].shape
        as3d = (1,) * (3 - len(shp)) + shp if len(shp) <= 3 else (-1,) + shp[-2:]
        outs = _adamw(*[jnp.reshape(d[n], as3d) for d in (W, g_small, M, V)], "adamw_" + n)
        delta[n], new_m[n], new_v[n] = [jnp.reshape(o, shp) for o in outs]

    names = list(W)
    grad_w = {**g_big, **g_small}
    return (loss_total, dx[None], *[grad_w[n] for n in names], *[delta[n] for n in names],
            *[new_m[n] for n in names], *[new_v[n] for n in names])
```
